```python
import math
import jax, jax.numpy as jnp
from jax import lax
import numpy as np

D_MODEL = 1024
BATCH = 8
SEQ = 4096
DEPTH = 1

CHUNK = 64
EPS = 1e-6
HG_HEADS = 8
HG_DK = 128
HG_DV = D_MODEL // HG_HEADS
HG_WIDTH_K = HG_HEADS * HG_DK
HG_WIDTH_V = HG_HEADS * HG_DV
GDN_QK_HEADS = 8
GDN_V_HEADS = 16
GDN_DK = 128
GDN_DV = 128
GDN_WIDTH_K = GDN_QK_HEADS * GDN_DK
GDN_WIDTH_V = GDN_V_HEADS * GDN_DV
CONV_K = 4
D_FF = 2816
IN_SIZES = (HG_WIDTH_K, HG_WIDTH_K, HG_WIDTH_V, HG_WIDTH_V,
            GDN_WIDTH_K, GDN_WIDTH_K, GDN_WIDTH_V, GDN_V_HEADS, GDN_V_HEADS, GDN_WIDTH_V,
            D_MODEL, D_MODEL)
IN_WIDTH = sum(IN_SIZES)

kernel_name = "hgrn2_gdn_gated_macaron_block"


def rmsnorm(x, g):
    xf = x.astype(jnp.float32)
    y = xf * lax.rsqrt(jnp.mean(xf * xf, axis=-1, keepdims=True) + EPS)
    return (y * g).astype(x.dtype)


def l2norm(x):
    return x * lax.rsqrt(jnp.sum(x * x, axis=-1, keepdims=True) + EPS)


def swiglu(x, w_in, w_out):
    a, b = jnp.split(x @ w_in, 2, axis=-1)
    return (jax.nn.silu(a) * b) @ w_out


def to_chunks(t, n_heads):
    B, S = t.shape[:2]
    t = t.reshape(B, S // CHUNK, CHUNK, n_heads, -1)
    return jnp.transpose(t, (0, 3, 1, 2, 4))


def from_chunks(t):
    B, H, NC, C, d = t.shape
    return jnp.transpose(t, (0, 2, 3, 1, 4)).reshape(B, NC * C, H, d)


def causal_short_conv(x, w):
    K = w.shape[0]
    S = x.shape[1]
    xp = jnp.pad(x, ((0, 0), (K - 1, 0), (0, 0)))
    return sum(xp[:, j:j + S] * w[j] for j in range(K))


def hgrn2_chunked(q, k, v, log_f):
    B, H, NC, C, DK = q.shape
    DV = v.shape[-1]
    b_cum = jnp.cumsum(log_f, axis=3)
    causal = jnp.tril(jnp.ones((C, C), dtype=bool))[:, :, None]

    def step(S, inp):
        q_c, k_c, v_c, b_c = inp
        inter = jnp.einsum('bhtk,bhkv->bhtv', q_c * jnp.exp(b_c), S)
        diff = b_c[:, :, :, None, :] - b_c[:, :, None, :, :]
        decay = jnp.exp(jnp.where(causal, diff, -jnp.inf))
        scores = jnp.einsum('bhtk,bhsk,bhtsk->bhts', q_c, k_c, decay)
        intra = jnp.einsum('bhts,bhsv->bhtv', scores, v_c)
        b_end = b_c[:, :, -1, :]
        k_to_end = k_c * jnp.exp(b_end[:, :, None, :] - b_c)
        S = jnp.exp(b_end)[..., None] * S + jnp.einsum('bhsk,bhsv->bhkv', k_to_end, v_c)
        return S, inter + intra

    S0 = jnp.zeros((B, H, DK, DV), jnp.float32)
    xs = (jnp.moveaxis(q, 2, 0), jnp.moveaxis(k, 2, 0), jnp.moveaxis(v, 2, 0), jnp.moveaxis(b_cum, 2, 0))
    _, o = lax.scan(step, S0, xs)
    return jnp.moveaxis(o, 0, 2)


def gated_delta_chunked(q, k, v, beta, g):
    B, H, NC, C, DK = q.shape
    DV = v.shape[-1]
    gam = jnp.cumsum(g, axis=-1)
    incl = jnp.tril(jnp.ones((C, C), dtype=bool))
    strict = jnp.tril(jnp.ones((C, C), dtype=bool), -1)
    diff = gam[..., :, None] - gam[..., None, :]
    Lmat = jnp.exp(jnp.where(incl, diff, -jnp.inf))
    kb = k * beta[..., None]
    A = jnp.where(strict, jnp.einsum('bhntk,bhnsk->bhnts', kb, k) * Lmat, 0.0)
    eye = jnp.eye(C, dtype=A.dtype)
    T = lax.linalg.triangular_solve(eye + A, jnp.broadcast_to(eye, A.shape),
                                    left_side=True, lower=True, unit_diagonal=True)
    u = jnp.matmul(T, v * beta[..., None])
    w = jnp.matmul(T, kb * jnp.exp(gam)[..., None])
    qk = jnp.einsum('bhntk,bhnsk->bhnts', q, k) * Lmat

    def step(S, inp):
        q_c, k_c, u_c, w_c, qk_c, gam_c = inp
        v_new = u_c - jnp.einsum('bhtk,bhkv->bhtv', w_c, S)
        o = (jnp.einsum('bhtk,bhkv->bhtv', q_c * jnp.exp(gam_c)[..., None], S)
             + jnp.einsum('bhts,bhsv->bhtv', qk_c, v_new))
        g_end = gam_c[..., -1]
        k_to_end = k_c * jnp.exp(g_end[..., None] - gam_c)[..., None]
        S = S * jnp.exp(g_end)[..., None, None] + jnp.einsum('bhsk,bhsv->bhkv', k_to_end, v_new)
        return S, o

    S0 = jnp.zeros((B, H, DK, DV), jnp.float32)
    xs = tuple(jnp.moveaxis(t, 2, 0) for t in (q, k, u, w, qk, gam))
    _, o = lax.scan(step, S0, xs)
    return jnp.moveaxis(o, 0, 2)


def hybrid_mixer(u, w_in, lb, hgrn_out_norm, conv_w, a_log, dt_bias, gdn_out_norm,
                 w_branch_hgrn, w_branch_gdn, w_out):
    B, S, _ = u.shape
    f32 = jnp.float32
    proj = (u @ w_in).astype(f32)
    offsets = np.cumsum(IN_SIZES)[:-1].tolist()
    (hq, hf, hi, hg, gq, gk, gv, ga, gb, gz, gate_h, gate_g) = jnp.split(proj, offsets, axis=-1)

    lb = lb.astype(f32)
    log_f = jnp.logaddexp(jnp.log(lb), jnp.log1p(-lb) + jax.nn.log_sigmoid(hf))
    k_h = -jnp.expm1(log_f)
    q_h = jax.nn.silu(hq) * HG_DK ** -0.5
    o_h = hgrn2_chunked(to_chunks(q_h, HG_HEADS), to_chunks(k_h, HG_HEADS),
                        to_chunks(hi, HG_HEADS), to_chunks(log_f, HG_HEADS))
    o_h = rmsnorm(from_chunks(o_h), hgrn_out_norm) * jax.nn.silu(hg).reshape(B, S, HG_HEADS, HG_DV)
    y_h = o_h.reshape(B, S, HG_WIDTH_V) @ w_branch_hgrn

    qkv = jax.nn.silu(causal_short_conv(jnp.concatenate([gq, gk, gv], axis=-1), conv_w))
    cq, ck, cv = jnp.split(qkv, [GDN_WIDTH_K, 2 * GDN_WIDTH_K], axis=-1)
    rep = GDN_V_HEADS // GDN_QK_HEADS
    q_g = l2norm(cq.reshape(B, S, GDN_QK_HEADS, GDN_DK)) * GDN_DK ** -0.5
    k_g = l2norm(ck.reshape(B, S, GDN_QK_HEADS, GDN_DK))
    q_g = jnp.repeat(q_g, rep, axis=2).reshape(B, S, GDN_V_HEADS * GDN_DK)
    k_g = jnp.repeat(k_g, rep, axis=2).reshape(B, S, GDN_V_HEADS * GDN_DK)
    beta = jax.nn.sigmoid(gb)
    g = -jnp.exp(a_log.astype(f32)) * jax.nn.softplus(ga + dt_bias)
    o_g = gated_delta_chunked(to_chunks(q_g, GDN_V_HEADS), to_chunks(k_g, GDN_V_HEADS),
                              to_chunks(cv, GDN_V_HEADS),
                              to_chunks(beta[..., None], GDN_V_HEADS)[..., 0],
                              to_chunks(g[..., None], GDN_V_HEADS)[..., 0])
    o_g = rmsnorm(from_chunks(o_g), gdn_out_norm) * jax.nn.silu(gz).reshape(B, S, GDN_V_HEADS, GDN_DV)
    y_g = o_g.reshape(B, S, GDN_WIDTH_V) @ w_branch_gdn

    y = jax.nn.sigmoid(gate_h) * y_h + jax.nn.sigmoid(gate_g) * y_g
    return (y @ w_out).astype(u.dtype)


def _fwd_setup_inputs(seed: int = 0) -> dict:
    key = jax.random.key(seed)
    ks = jax.random.split(key, 20)
    f32 = jnp.float32
    L = DEPTH

    def dense(k, shape):
        return jax.random.normal(k, shape, f32) * shape[-2] ** -0.5

    def gain(k, shape):
        return 1.0 + 0.05 * jax.random.normal(k, shape, f32)

    A = jax.random.uniform(ks[8], (L, GDN_V_HEADS), f32, 1.0, 16.0)
    dt = jnp.exp(jax.random.uniform(ks[9], (L, GDN_V_HEADS), f32, math.log(1e-3), math.log(1e-1)))
    dt_bias = dt + jnp.log(-jnp.expm1(-dt))
    return {
        "x": jax.random.normal(ks[0], (BATCH, SEQ, D_MODEL), f32),
        "ffn1_norm": gain(ks[1], (L, D_MODEL)),
        "ffn1_w_in": dense(ks[2], (L, D_MODEL, 2 * D_FF)),
        "ffn1_w_out": dense(ks[3], (L, D_FF, D_MODEL)),
        "mix_norm": gain(ks[4], (L, D_MODEL)),
        "w_in": dense(ks[5], (L, D_MODEL, IN_WIDTH)),
        "hgrn_lb_logits": 0.5 * jax.random.normal(ks[6], (L + 1, HG_WIDTH_K), f32),
        "hgrn_out_norm": gain(ks[7], (L, HG_DV)),
        "gdn_conv_w": 0.5 * jax.random.normal(ks[10], (L, CONV_K, 2 * GDN_WIDTH_K + GDN_WIDTH_V), f32),
        "gdn_a_log": jnp.log(A),
        "gdn_dt_bias": dt_bias,
        "gdn_out_norm": gain(ks[11], (L, GDN_DV)),
        "w_branch_hgrn": dense(ks[12], (L, HG_WIDTH_V, D_MODEL)),
        "w_branch_gdn": dense(ks[13], (L, GDN_WIDTH_V, D_MODEL)),
        "w_out": dense(ks[14], (L, D_MODEL, D_MODEL)),
        "ffn2_norm": gain(ks[15], (L, D_MODEL)),
        "ffn2_w_in": dense(ks[16], (L, D_MODEL, 2 * D_FF)),
        "ffn2_w_out": dense(ks[17], (L, D_FF, D_MODEL)),
        "final_norm": gain(ks[18], (D_MODEL,)),
    }


def _fwd_reference(x, ffn1_norm, ffn1_w_in, ffn1_w_out, mix_norm, w_in, hgrn_lb_logits,
              hgrn_out_norm, gdn_conv_w, gdn_a_log, gdn_dt_bias, gdn_out_norm,
              w_branch_hgrn, w_branch_gdn, w_out, ffn2_norm, ffn2_w_in, ffn2_w_out,
              final_norm):
    lb_all = jnp.cumsum(jax.nn.softmax(hgrn_lb_logits.astype(jnp.float32), axis=0), axis=0)
    h = x
    for l in range(DEPTH):
        h = h + 0.5 * swiglu(rmsnorm(h, ffn1_norm[l]), ffn1_w_in[l], ffn1_w_out[l])
        h = h + hybrid_mixer(rmsnorm(h, mix_norm[l]), w_in[l], lb_all[l], hgrn_out_norm[l],
                             gdn_conv_w[l], gdn_a_log[l], gdn_dt_bias[l], gdn_out_norm[l],
                             w_branch_hgrn[l], w_branch_gdn[l], w_out[l])
        h = h + 0.5 * swiglu(rmsnorm(h, ffn2_norm[l]), ffn2_w_in[l], ffn2_w_out[l])
    return rmsnorm(h, final_norm)


import jax as _jax
import jax.numpy as _jnp

TWIN_FORMAT = 'train_step'
FWD_PARAMS = ['x', 'ffn1_norm', 'ffn1_w_in', 'ffn1_w_out', 'mix_norm', 'w_in', 'hgrn_lb_logits', 'hgrn_out_norm', 'gdn_conv_w', 'gdn_a_log', 'gdn_dt_bias', 'gdn_out_norm', 'w_branch_hgrn', 'w_branch_gdn', 'w_out', 'ffn2_norm', 'ffn2_w_in', 'ffn2_w_out', 'final_norm']
TWIN_WEIGHTS = ['ffn1_norm', 'ffn1_w_in', 'ffn1_w_out', 'mix_norm', 'w_in', 'hgrn_lb_logits', 'hgrn_out_norm', 'gdn_conv_w', 'gdn_a_log', 'gdn_dt_bias', 'gdn_out_norm', 'w_branch_hgrn', 'w_branch_gdn', 'w_out', 'ffn2_norm', 'ffn2_w_in', 'ffn2_w_out', 'final_norm']
TWIN_DIFF_INPUT = 'x'
TWIN_INPUTS = ['x', 'ffn1_norm', 'ffn1_w_in', 'ffn1_w_out', 'mix_norm', 'w_in', 'hgrn_lb_logits', 'hgrn_out_norm', 'gdn_conv_w', 'gdn_a_log', 'gdn_dt_bias', 'gdn_out_norm', 'w_branch_hgrn', 'w_branch_gdn', 'w_out', 'ffn2_norm', 'ffn2_w_in', 'ffn2_w_out', 'final_norm', 'loss_target', 'm_ffn1_norm', 'm_ffn1_w_in', 'm_ffn1_w_out', 'm_mix_norm', 'm_w_in', 'm_hgrn_lb_logits', 'm_hgrn_out_norm', 'm_gdn_conv_w', 'm_gdn_a_log', 'm_gdn_dt_bias', 'm_gdn_out_norm', 'm_w_branch_hgrn', 'm_w_branch_gdn', 'm_w_out', 'm_ffn2_norm', 'm_ffn2_w_in', 'm_ffn2_w_out', 'm_final_norm', 'v_ffn1_norm', 'v_ffn1_w_in', 'v_ffn1_w_out', 'v_mix_norm', 'v_w_in', 'v_hgrn_lb_logits', 'v_hgrn_out_norm', 'v_gdn_conv_w', 'v_gdn_a_log', 'v_gdn_dt_bias', 'v_gdn_out_norm', 'v_w_branch_hgrn', 'v_w_branch_gdn', 'v_w_out', 'v_ffn2_norm', 'v_ffn2_w_in', 'v_ffn2_w_out', 'v_final_norm']
TWIN_OUTPUTS = ['loss', 'grad_x', 'grad_ffn1_norm', 'grad_ffn1_w_in', 'grad_ffn1_w_out', 'grad_mix_norm', 'grad_w_in', 'grad_hgrn_lb_logits', 'grad_hgrn_out_norm', 'grad_gdn_conv_w', 'grad_gdn_a_log', 'grad_gdn_dt_bias', 'grad_gdn_out_norm', 'grad_w_branch_hgrn', 'grad_w_branch_gdn', 'grad_w_out', 'grad_ffn2_norm', 'grad_ffn2_w_in', 'grad_ffn2_w_out', 'grad_final_norm', 'delta_ffn1_norm', 'delta_ffn1_w_in', 'delta_ffn1_w_out', 'delta_mix_norm', 'delta_w_in', 'delta_hgrn_lb_logits', 'delta_hgrn_out_norm', 'delta_gdn_conv_w', 'delta_gdn_a_log', 'delta_gdn_dt_bias', 'delta_gdn_out_norm', 'delta_w_branch_hgrn', 'delta_w_branch_gdn', 'delta_w_out', 'delta_ffn2_norm', 'delta_ffn2_w_in', 'delta_ffn2_w_out', 'delta_final_norm', 'new_m_ffn1_norm', 'new_m_ffn1_w_in', 'new_m_ffn1_w_out', 'new_m_mix_norm', 'new_m_w_in', 'new_m_hgrn_lb_logits', 'new_m_hgrn_out_norm', 'new_m_gdn_conv_w', 'new_m_gdn_a_log', 'new_m_gdn_dt_bias', 'new_m_gdn_out_norm', 'new_m_w_branch_hgrn', 'new_m_w_branch_gdn', 'new_m_w_out', 'new_m_ffn2_norm', 'new_m_ffn2_w_in', 'new_m_ffn2_w_out', 'new_m_final_norm', 'new_v_ffn1_norm', 'new_v_ffn1_w_in', 'new_v_ffn1_w_out', 'new_v_mix_norm', 'new_v_w_in', 'new_v_hgrn_lb_logits', 'new_v_hgrn_out_norm', 'new_v_gdn_conv_w', 'new_v_gdn_a_log', 'new_v_gdn_dt_bias', 'new_v_gdn_out_norm', 'new_v_w_branch_hgrn', 'new_v_w_branch_gdn', 'new_v_w_out', 'new_v_ffn2_norm', 'new_v_ffn2_w_in', 'new_v_ffn2_w_out', 'new_v_final_norm']
TWIN_LEAF_KINDS = {'loss': 'loss', 'grad_x': 'grad_x', 'grad_ffn1_norm': 'grad_w', 'grad_ffn1_w_in': 'grad_w', 'grad_ffn1_w_out': 'grad_w', 'grad_mix_norm': 'grad_w', 'grad_w_in': 'grad_w', 'grad_hgrn_lb_logits': 'grad_w', 'grad_hgrn_out_norm': 'grad_w', 'grad_gdn_conv_w': 'grad_w', 'grad_gdn_a_log': 'grad_w', 'grad_gdn_dt_bias': 'grad_w', 'grad_gdn_out_norm': 'grad_w', 'grad_w_branch_hgrn': 'grad_w', 'grad_w_branch_gdn': 'grad_w', 'grad_w_out': 'grad_w', 'grad_ffn2_norm': 'grad_w', 'grad_ffn2_w_in': 'grad_w', 'grad_ffn2_w_out': 'grad_w', 'grad_final_norm': 'grad_w', 'delta_ffn1_norm': 'delta_w', 'delta_ffn1_w_in': 'delta_w', 'delta_ffn1_w_out': 'delta_w', 'delta_mix_norm': 'delta_w', 'delta_w_in': 'delta_w', 'delta_hgrn_lb_logits': 'delta_w', 'delta_hgrn_out_norm': 'delta_w', 'delta_gdn_conv_w': 'delta_w', 'delta_gdn_a_log': 'delta_w', 'delta_gdn_dt_bias': 'delta_w', 'delta_gdn_out_norm': 'delta_w', 'delta_w_branch_hgrn': 'delta_w', 'delta_w_branch_gdn': 'delta_w', 'delta_w_out': 'delta_w', 'delta_ffn2_norm': 'delta_w', 'delta_ffn2_w_in': 'delta_w', 'delta_ffn2_w_out': 'delta_w', 'delta_final_norm': 'delta_w', 'new_m_ffn1_norm': 'new_m', 'new_m_ffn1_w_in': 'new_m', 'new_m_ffn1_w_out': 'new_m', 'new_m_mix_norm': 'new_m', 'new_m_w_in': 'new_m', 'new_m_hgrn_lb_logits': 'new_m', 'new_m_hgrn_out_norm': 'new_m', 'new_m_gdn_conv_w': 'new_m', 'new_m_gdn_a_log': 'new_m', 'new_m_gdn_dt_bias': 'new_m', 'new_m_gdn_out_norm': 'new_m', 'new_m_w_branch_hgrn': 'new_m', 'new_m_w_branch_gdn': 'new_m', 'new_m_w_out': 'new_m', 'new_m_ffn2_norm': 'new_m', 'new_m_ffn2_w_in': 'new_m', 'new_m_ffn2_w_out': 'new_m', 'new_m_final_norm': 'new_m', 'new_v_ffn1_norm': 'new_v', 'new_v_ffn1_w_in': 'new_v', 'new_v_ffn1_w_out': 'new_v', 'new_v_mix_norm': 'new_v', 'new_v_w_in': 'new_v', 'new_v_hgrn_lb_logits': 'new_v', 'new_v_hgrn_out_norm': 'new_v', 'new_v_gdn_conv_w': 'new_v', 'new_v_gdn_a_log': 'new_v', 'new_v_gdn_dt_bias': 'new_v', 'new_v_gdn_out_norm': 'new_v', 'new_v_w_branch_hgrn': 'new_v', 'new_v_w_branch_gdn': 'new_v', 'new_v_w_out': 'new_v', 'new_v_ffn2_norm': 'new_v', 'new_v_ffn2_w_in': 'new_v', 'new_v_ffn2_w_out': 'new_v', 'new_v_final_norm': 'new_v'}


def _forward(args):
    return _fwd_reference(*[args[k] for k in FWD_PARAMS])


def _output_shape():
    def fwd():
        inp = _fwd_setup_inputs(0)
        return _fwd_reference(*[inp[k] for k in FWD_PARAMS])
    out = _jax.eval_shape(fwd)
    return out.shape, out.dtype

N_MICROBATCH = 1
ADAM_LR = 0.001
ADAM_B1 = 0.9
ADAM_B2 = 0.999
ADAM_EPS = 1e-08
ADAM_WD = 0.01
ADAM_STEP = 10
PER_EXAMPLE_BATCH_AXIS = {'x': 0, 'loss_target': 0}
SHARED_INPUTS = []
_WEIGHT_DTYPES = {'ffn1_norm': _jnp.float32, 'ffn1_w_in': _jnp.float32, 'ffn1_w_out': _jnp.float32, 'mix_norm': _jnp.float32, 'w_in': _jnp.float32, 'hgrn_lb_logits': _jnp.float32, 'hgrn_out_norm': _jnp.float32, 'gdn_conv_w': _jnp.float32, 'gdn_a_log': _jnp.float32, 'gdn_dt_bias': _jnp.float32, 'gdn_out_norm': _jnp.float32, 'w_branch_hgrn': _jnp.float32, 'w_branch_gdn': _jnp.float32, 'w_out': _jnp.float32, 'ffn2_norm': _jnp.float32, 'ffn2_w_in': _jnp.float32, 'ffn2_w_out': _jnp.float32, 'final_norm': _jnp.float32}
MOMENT_SCALE = {'ffn1_norm': 8.787137e-02, 'ffn1_w_in': 3.633944e-02, 'ffn1_w_out': 5.934493e-02, 'mix_norm': 1.199565e-01, 'w_in': 3.424675e-02, 'hgrn_lb_logits': 4.608224e-03, 'hgrn_out_norm': 1.821501e-01, 'gdn_conv_w': 3.447323e-02, 'gdn_a_log': 1.806275e-01, 'gdn_dt_bias': 1.741890e-01, 'gdn_out_norm': 1.468806e-01, 'w_branch_hgrn': 5.230461e-02, 'w_branch_gdn': 5.243915e-02, 'w_out': 7.476218e-02, 'ffn2_norm': 6.966704e-02, 'ffn2_w_in': 2.789464e-02, 'ffn2_w_out': 4.582551e-02, 'final_norm': 3.207083e+01}


def _to_microbatches(a, axis):
    t = _jnp.moveaxis(a, axis, 0)
    t = t.reshape((N_MICROBATCH, t.shape[0] // N_MICROBATCH) + t.shape[1:])
    return _jnp.moveaxis(t, 1, axis + 1)


def setup_inputs(seed: int = 0) -> dict:
    inp = _fwd_setup_inputs(seed)
    key = _jax.random.fold_in(_jax.random.key(seed), 7919)
    shape, _ = _output_shape()
    out = dict(inp)
    out["loss_target"] = _jax.random.normal(_jax.random.fold_in(key, 0), shape, _jnp.float32)
    for i, name in enumerate(TWIN_WEIGHTS):
        w = inp[name].astype(_jnp.float32)
        if MOMENT_SCALE is None:
            s = _jnp.sqrt(_jnp.mean(_jnp.square(w)) + 1e-30)
        else:
            s = MOMENT_SCALE[name]
        km, kv = _jax.random.split(_jax.random.fold_in(key, i + 1))
        out[name] = w
        out["m_" + name] = s * _jax.random.normal(km, w.shape, _jnp.float32)
        out["v_" + name] = (s * s) * _jax.random.uniform(kv, w.shape, _jnp.float32, 0.5, 1.5)
    if N_MICROBATCH > 1:
        for name, axis in PER_EXAMPLE_BATCH_AXIS.items():
            out[name] = _to_microbatches(out[name], axis)
    return {'x': out['x'], 'ffn1_norm': out['ffn1_norm'], 'ffn1_w_in': out['ffn1_w_in'], 'ffn1_w_out': out['ffn1_w_out'], 'mix_norm': out['mix_norm'], 'w_in': out['w_in'], 'hgrn_lb_logits': out['hgrn_lb_logits'], 'hgrn_out_norm': out['hgrn_out_norm'], 'gdn_conv_w': out['gdn_conv_w'], 'gdn_a_log': out['gdn_a_log'], 'gdn_dt_bias': out['gdn_dt_bias'], 'gdn_out_norm': out['gdn_out_norm'], 'w_branch_hgrn': out['w_branch_hgrn'], 'w_branch_gdn': out['w_branch_gdn'], 'w_out': out['w_out'], 'ffn2_norm': out['ffn2_norm'], 'ffn2_w_in': out['ffn2_w_in'], 'ffn2_w_out': out['ffn2_w_out'], 'final_norm': out['final_norm'], 'loss_target': out['loss_target'], 'm_ffn1_norm': out['m_ffn1_norm'], 'm_ffn1_w_in': out['m_ffn1_w_in'], 'm_ffn1_w_out': out['m_ffn1_w_out'], 'm_mix_norm': out['m_mix_norm'], 'm_w_in': out['m_w_in'], 'm_hgrn_lb_logits': out['m_hgrn_lb_logits'], 'm_hgrn_out_norm': out['m_hgrn_out_norm'], 'm_gdn_conv_w': out['m_gdn_conv_w'], 'm_gdn_a_log': out['m_gdn_a_log'], 'm_gdn_dt_bias': out['m_gdn_dt_bias'], 'm_gdn_out_norm': out['m_gdn_out_norm'], 'm_w_branch_hgrn': out['m_w_branch_hgrn'], 'm_w_branch_gdn': out['m_w_branch_gdn'], 'm_w_out': out['m_w_out'], 'm_ffn2_norm': out['m_ffn2_norm'], 'm_ffn2_w_in': out['m_ffn2_w_in'], 'm_ffn2_w_out': out['m_ffn2_w_out'], 'm_final_norm': out['m_final_norm'], 'v_ffn1_norm': out['v_ffn1_norm'], 'v_ffn1_w_in': out['v_ffn1_w_in'], 'v_ffn1_w_out': out['v_ffn1_w_out'], 'v_mix_norm': out['v_mix_norm'], 'v_w_in': out['v_w_in'], 'v_hgrn_lb_logits': out['v_hgrn_lb_logits'], 'v_hgrn_out_norm': out['v_hgrn_out_norm'], 'v_gdn_conv_w': out['v_gdn_conv_w'], 'v_gdn_a_log': out['v_gdn_a_log'], 'v_gdn_dt_bias': out['v_gdn_dt_bias'], 'v_gdn_out_norm': out['v_gdn_out_norm'], 'v_w_branch_hgrn': out['v_w_branch_hgrn'], 'v_w_branch_gdn': out['v_w_branch_gdn'], 'v_w_out': out['v_w_out'], 'v_ffn2_norm': out['v_ffn2_norm'], 'v_ffn2_w_in': out['v_ffn2_w_in'], 'v_ffn2_w_out': out['v_ffn2_w_out'], 'v_final_norm': out['v_final_norm']}


def _loss(weights, diff, rest, loss_target):
    with _jax.named_scope("forward"):
        args = {**rest, TWIN_DIFF_INPUT: diff, **{k: w.astype(_WEIGHT_DTYPES[k]) for k, w in weights.items()}}
        y = _forward(args)
    with _jax.named_scope("loss_head"):
        err = _jnp.square(y.astype(_jnp.float32) - loss_target)
        return 0.5 * _jnp.sum(_jnp.mean(err, axis=-1)) if err.ndim else 0.5 * err


def _adamw(w, g, m, v):
    m = ADAM_B1 * m + (1.0 - ADAM_B1) * g
    v = ADAM_B2 * v + (1.0 - ADAM_B2) * _jnp.square(g)
    m_hat = m / (1.0 - ADAM_B1 ** ADAM_STEP)
    v_hat = v / (1.0 - ADAM_B2 ** ADAM_STEP)
    delta = -ADAM_LR * (m_hat / (_jnp.sqrt(v_hat) + ADAM_EPS) + ADAM_WD * w)
    return delta, m, v


def reference(x, ffn1_norm, ffn1_w_in, ffn1_w_out, mix_norm, w_in, hgrn_lb_logits, hgrn_out_norm, gdn_conv_w, gdn_a_log, gdn_dt_bias, gdn_out_norm, w_branch_hgrn, w_branch_gdn, w_out, ffn2_norm, ffn2_w_in, ffn2_w_out, final_norm, loss_target, m_ffn1_norm, m_ffn1_w_in, m_ffn1_w_out, m_mix_norm, m_w_in, m_hgrn_lb_logits, m_hgrn_out_norm, m_gdn_conv_w, m_gdn_a_log, m_gdn_dt_bias, m_gdn_out_norm, m_w_branch_hgrn, m_w_branch_gdn, m_w_out, m_ffn2_norm, m_ffn2_w_in, m_ffn2_w_out, m_final_norm, v_ffn1_norm, v_ffn1_w_in, v_ffn1_w_out, v_mix_norm, v_w_in, v_hgrn_lb_logits, v_hgrn_out_norm, v_gdn_conv_w, v_gdn_a_log, v_gdn_dt_bias, v_gdn_out_norm, v_w_branch_hgrn, v_w_branch_gdn, v_w_out, v_ffn2_norm, v_ffn2_w_in, v_ffn2_w_out, v_final_norm):
    given = dict(x=x, ffn1_norm=ffn1_norm, ffn1_w_in=ffn1_w_in, ffn1_w_out=ffn1_w_out, mix_norm=mix_norm, w_in=w_in, hgrn_lb_logits=hgrn_lb_logits, hgrn_out_norm=hgrn_out_norm, gdn_conv_w=gdn_conv_w, gdn_a_log=gdn_a_log, gdn_dt_bias=gdn_dt_bias, gdn_out_norm=gdn_out_norm, w_branch_hgrn=w_branch_hgrn, w_branch_gdn=w_branch_gdn, w_out=w_out, ffn2_norm=ffn2_norm, ffn2_w_in=ffn2_w_in, ffn2_w_out=ffn2_w_out, final_norm=final_norm, loss_target=loss_target, m_ffn1_norm=m_ffn1_norm, m_ffn1_w_in=m_ffn1_w_in, m_ffn1_w_out=m_ffn1_w_out, m_mix_norm=m_mix_norm, m_w_in=m_w_in, m_hgrn_lb_logits=m_hgrn_lb_logits, m_hgrn_out_norm=m_hgrn_out_norm, m_gdn_conv_w=m_gdn_conv_w, m_gdn_a_log=m_gdn_a_log, m_gdn_dt_bias=m_gdn_dt_bias, m_gdn_out_norm=m_gdn_out_norm, m_w_branch_hgrn=m_w_branch_hgrn, m_w_branch_gdn=m_w_branch_gdn, m_w_out=m_w_out, m_ffn2_norm=m_ffn2_norm, m_ffn2_w_in=m_ffn2_w_in, m_ffn2_w_out=m_ffn2_w_out, m_final_norm=m_final_norm, v_ffn1_norm=v_ffn1_norm, v_ffn1_w_in=v_ffn1_w_in, v_ffn1_w_out=v_ffn1_w_out, v_mix_norm=v_mix_norm, v_w_in=v_w_in, v_hgrn_lb_logits=v_hgrn_lb_logits, v_hgrn_out_norm=v_hgrn_out_norm, v_gdn_conv_w=v_gdn_conv_w, v_gdn_a_log=v_gdn_a_log, v_gdn_dt_bias=v_gdn_dt_bias, v_gdn_out_norm=v_gdn_out_norm, v_w_branch_hgrn=v_w_branch_hgrn, v_w_branch_gdn=v_w_branch_gdn, v_w_out=v_w_out, v_ffn2_norm=v_ffn2_norm, v_ffn2_w_in=v_ffn2_w_in, v_ffn2_w_out=v_ffn2_w_out, v_final_norm=v_final_norm)
    weights = {n: given[n] for n in TWIN_WEIGHTS}
    shared = {n: given[n] for n in SHARED_INPUTS}
    per_example = {n: given[n] for n in ['x']}
    grad_fn = _jax.value_and_grad(_loss, argnums=(0, 1))

    def one_microbatch(ex, loss_target):
        ex = dict(ex)
        diff = ex.pop(TWIN_DIFF_INPUT)
        return grad_fn(weights, diff, {**shared, **ex}, loss_target)

    if N_MICROBATCH == 1:
        loss, (grad_w, grad_x) = one_microbatch(per_example, given["loss_target"])
    else:
        def body(carry, xs):
            loss_sum, grad_sum = carry
            l_k, (gw_k, gx_k) = one_microbatch(xs[0], xs[1])
            with _jax.named_scope("update"):
                return (loss_sum + l_k, _jax.tree.map(_jnp.add, grad_sum, gw_k)), gx_k

        init = (_jnp.zeros((), _jnp.float32), _jax.tree.map(_jnp.zeros_like, weights))
        (loss, grad_w), grad_x = _jax.lax.scan(body, init, (per_example, given["loss_target"]))
    with _jax.named_scope("update"):
        delta_w, new_m, new_v = {}, {}, {}
        for n in TWIN_WEIGHTS:
            delta_w[n], new_m[n], new_v[n] = _adamw(weights[n], grad_w[n], given["m_" + n], given["v_" + n])
    return (loss, grad_x, *[grad_w[n] for n in TWIN_WEIGHTS], *[delta_w[n] for n in TWIN_WEIGHTS],
            *[new_m[n] for n in TWIN_WEIGHTS], *[new_v[n] for n in TWIN_WEIGHTS])
```

```python
import functools

import jax
import jax.numpy as jnp
from jax import lax
from jax.experimental import pallas as pl
from jax.experimental.pallas import tpu as pltpu

F32 = jnp.float32
BF16 = jnp.bfloat16
HIGHEST = lax.Precision.HIGHEST
MESH_IDS = pl.DeviceIdType.MESH

D_MODEL = 1024
D_FF = 2816
N_DEV = 8
EPS = 1e-6
HEAD = 128
HG_HEADS = 8
GDN_QK_HEADS = 8
GDN_V_HEADS = 16
GDN_CHUNK = 64
HG_CHUNK = 16
CONV_K = 4
FF_SHARD = 2 * D_FF // N_DEV
FF_SHARD_PAD = 768
FF_PAD = N_DEV * FF_SHARD_PAD // 2
IN_WIDTH = 12320
IN_MAIN = 12288
COL_HQ, COL_HF, COL_HI, COL_HG, COL_GQ, COL_GK, COL_GV, COL_GZ, COL_GATE_H, COL_GATE_G = 0, 8, 16, 24, 32, 40, 48, 64, 80, 88
VMEM_LIMIT = 56 * 1024 * 1024

ADAM_LR, ADAM_B1, ADAM_B2, ADAM_EPS, ADAM_WD, ADAM_STEP = 0.001, 0.9, 0.999, 1e-08, 0.01, 10

SDS = jax.ShapeDtypeStruct


def _params(n_axes):
    return pltpu.CompilerParams(dimension_semantics=("arbitrary",) * n_axes, vmem_limit_bytes=VMEM_LIMIT)


def _tile(n, candidates=(512, 384, 256, 128, 64, 32, 16, 8)):
    for c in candidates:
        if n % c == 0:
            return c
    return n


_DIMS = {"nn": ((1,), (0,)), "nt": ((1,), (1,)), "tn": ((0,), (0,))}


def _bdot_raw(a, b, dims):
    return lax.dot_general(a.astype(BF16), b.astype(BF16), (_DIMS[dims], ((), ())), preferred_element_type=F32)


@functools.partial(jax.custom_vjp, nondiff_argnums=(2,))
def _bdot(a, b, dims):
    return _bdot_raw(a, b, dims)


def _bdot_fwd(a, b, dims):
    return _bdot_raw(a, b, dims), (a, b)


def _bdot_bwd(dims, res, ct):
    a, b = res
    if dims == "nn":
        return _bdot_raw(ct, b, "nt"), _bdot_raw(a, ct, "tn")
    if dims == "nt":
        return _bdot_raw(ct, b, "nn"), _bdot_raw(ct, a, "tn")
    return _bdot_raw(b, ct, "nt"), _bdot_raw(a, ct, "nn")


_bdot.defvjp(_bdot_fwd, _bdot_bwd)


def _hdot_raw(a, b):
    return jnp.dot(a, b, precision=HIGHEST, preferred_element_type=F32)


@jax.custom_vjp
def _hdot(a, b):
    return _hdot_raw(a, b)


def _hdot_fwd(a, b):
    return _hdot_raw(a, b), (a, b)


def _hdot_bwd(res, ct):
    a, b = res
    return _hdot_raw(ct, b.T), _hdot_raw(a.T, ct)


_hdot.defvjp(_hdot_fwd, _hdot_bwd)


def _mm(a, b, dims, out_dtype, name, res=None, alpha=1.0):
    if dims == "nn":
        (m, k), (k2, n) = a.shape, b.shape
    elif dims == "nt":
        (m, k), (n, k2) = a.shape, b.shape
    else:
        (k, m), (k2, n) = a.shape, b.shape
    assert k == k2, (a.shape, b.shape, dims)
    tm, tn, tk = _tile(m), _tile(n), _tile(k, (1024, 512, 384, 256, 128, 64, 32, 16, 8))
    nk = k // tk
    a_spec = pl.BlockSpec((tk, tm), lambda i, j, kk: (kk, i)) if dims == "tn" else pl.BlockSpec((tm, tk), lambda i, j, kk: (i, kk))
    b_spec = pl.BlockSpec((tn, tk), lambda i, j, kk: (j, kk)) if dims == "nt" else pl.BlockSpec((tk, tn), lambda i, j, kk: (kk, j))
    o_spec = pl.BlockSpec((tm, tn), lambda i, j, kk: (i, j))
    has_res = res is not None

    def body(*refs):
        if has_res:
            a_ref, b_ref, r_ref, o_ref, acc_ref = refs
        else:
            a_ref, b_ref, o_ref, acc_ref = refs
        kk = pl.program_id(2)
        p = _bdot_raw(a_ref[...], b_ref[...], dims)

        @pl.when(kk == 0)
        def _():
            acc_ref[...] = p

        @pl.when(kk > 0)
        def _():
            acc_ref[...] += p

        @pl.when(kk == nk - 1)
        def _():
            out = acc_ref[...] * alpha if alpha != 1.0 else acc_ref[...]
            if has_res:
                out = r_ref[...].astype(F32) + out
            o_ref[...] = out.astype(o_ref.dtype)

    args = (a, b, res) if has_res else (a, b)
    in_specs = [a_spec, b_spec] + ([o_spec] if has_res else [])
    return pl.pallas_call(
        body, grid=(m // tm, n // tn, nk), in_specs=in_specs, out_specs=o_spec, out_shape=SDS((m, n), out_dtype),
        scratch_shapes=[pltpu.VMEM((tm, tn), F32)], name=name, compiler_params=_params(3),
    )(*args)


def _tmap(fn, grid, ins, outs, name):
    n_in = len(ins)
    n_ax = len(grid)

    def body(*refs):
        vals = fn(*[r[...] for r in refs[:n_in]])
        if not isinstance(vals, (tuple, list)):
            vals = (vals,)
        first_inner = pl.program_id(n_ax - 1) == 0
        first_all = first_inner
        for ax in range(n_ax - 1):
            first_all = jnp.logical_and(first_all, pl.program_id(ax) == 0)

        def put(ref, val, acc):
            val = val.astype(ref.dtype)
            if acc is None:
                ref[...] = val
                return
            first = first_inner if acc == "inner" else first_all

            @pl.when(first)
            def _():
                ref[...] = val

            @pl.when(jnp.logical_not(first))
            def _():
                ref[...] += val

        for ref, val, o in zip(refs[n_in:], vals, outs):
            put(ref, val, o[4])

    return pl.pallas_call(
        body, grid=grid,
        in_specs=[pl.BlockSpec(bs, im) for _, bs, im in ins],
        out_specs=[pl.BlockSpec(o[2], o[3]) for o in outs],
        out_shape=[SDS(o[0], o[1]) for o in outs],
        name=name, compiler_params=_params(n_ax),
    )(*[a for a, _, _ in ins])


def _rows(width, tt, off=0):
    return (tt, width), (lambda j, i: (i, off + j))


def _rms(x, g):
    x = x.astype(F32)
    return x * lax.rsqrt(jnp.mean(x * x, axis=-1, keepdims=True) + EPS) * g


def _silu(x):
    return x * jax.nn.sigmoid(x)


def _softplus(x):
    return jnp.maximum(x, 0.0) + jnp.log1p(jnp.exp(-jnp.abs(x)))


def _rms_fwd(x, g, name):
    t, d = x.shape
    tt = _tile(t, (256, 128))
    return _tmap(_rms, (1, t // tt), [(x, *_rows(d, tt)), (g, (1, d), lambda j, i: (0, 0))],
                 [((t, d), BF16, *_rows(d, tt), None)], name)[0]


def _rms_bwd(x, g, dn, dres, name):
    t, d = x.shape
    tt = _tile(t, (256, 128))

    def fn(x, g, dn, dres):
        _, vjp = jax.vjp(_rms, x, g)
        dx, dg = vjp(dn.astype(F32))
        return dres + dx, dg

    return _tmap(fn, (1, t // tt),
                 [(x, *_rows(d, tt)), (g, (1, d), lambda j, i: (0, 0)), (dn, *_rows(d, tt)), (dres, *_rows(d, tt))],
                 [((t, d), F32, *_rows(d, tt), None), ((1, d), F32, (1, d), lambda j, i: (0, 0), "inner")], name)


def _swiglu(ab):
    return _silu(ab[:, :FF_PAD]) * ab[:, FF_PAD:]


def _swiglu_fwd(ab, name):
    t = ab.shape[0]
    tt = _tile(t, (128,))
    return _tmap(_swiglu, (1, t // tt), [(ab, *_rows(2 * FF_PAD, tt))], [((t, FF_PAD), BF16, *_rows(FF_PAD, tt), None)], name)[0]


def _swiglu_bwd(ab, ds, name):
    t = ab.shape[0]
    tt = _tile(t, (128,))

    def fn(ab, ds):
        a, b = ab[:, :FF_PAD], ab[:, FF_PAD:]
        _, vjp = jax.vjp(lambda a, b: _silu(a) * b, a, b)
        da, db = vjp(ds.astype(F32))
        return jnp.concatenate([da, db], axis=1)

    return _tmap(fn, (1, t // tt), [(ab, *_rows(2 * FF_PAD, tt)), (ds, *_rows(FF_PAD, tt))],
                 [((t, 2 * FF_PAD), BF16, *_rows(2 * FF_PAD, tt), None)], name)[0]


def _ffn_fwd(h, g, w_in, w_out, tag):
    n = _rms_fwd(h, g, tag + "_norm")
    ab = _mm(n, w_in, "nn", F32, tag + "_in")
    s = _swiglu_fwd(ab, tag + "_act")
    out = _mm(s, w_out, "nn", F32, tag + "_out", res=h, alpha=0.5)
    return out, (n, ab, s)


def _ffn_bwd(h, g, w_in, w_out, saved, dout, tag):
    n, ab, s = saved
    dw_out = _mm(s, dout, "tn", BF16, tag + "_dw_out", alpha=0.5)
    ds = _mm(dout, w_out, "nt", F32, tag + "_ds", alpha=0.5)
    dab = _swiglu_bwd(ab, ds, tag + "_dact")
    dw_in = _mm(n, dab, "tn", BF16, tag + "_dw_in")
    dn = _mm(dab, w_in, "nt", F32, tag + "_dn")
    dh, dg = _rms_bwd(h, g, dn, dout, tag + "_dnorm")
    return dh, dg, dw_in, dw_out


def _chunk_sum_matrix(n, chunk, transpose=False):
    row = lax.broadcasted_iota(jnp.int32, (n, n), 0)
    col = lax.broadcasted_iota(jnp.int32, (n, n), 1)
    if transpose:
        row, col = col, row
    return jnp.where(jnp.logical_and(col <= row, row // chunk == col // chunk), 1.0, 0.0).astype(F32)


def _hgrn_gates(hq, hf, lbl):
    lb = jax.nn.sigmoid(lbl[0:1, :] - lbl[1:2, :])
    sg = jax.nn.sigmoid(hf)
    f = lb + (1.0 - lb) * sg
    q = _silu(hq) * HEAD ** -0.5
    k = (1.0 - lb) * (1.0 - sg)
    return q, k, jnp.log(f)


def _hgrn_prep_fwd(proj, lbl):
    t = proj.shape[0]
    tt, ft = _tile(t, (256, 128)), 512

    def fn(hq, hf, lbl):
        q, k, log_f = _hgrn_gates(hq, hf, lbl)
        return q, k, _hdot_raw(_chunk_sum_matrix(tt, HG_CHUNK), log_f)

    o = ((t, D_MODEL), F32, *_rows(ft, tt), None)
    return _tmap(fn, (D_MODEL // ft, t // tt),
                 [(proj, *_rows(ft, tt, COL_HQ * HEAD // ft)), (proj, *_rows(ft, tt, COL_HF * HEAD // ft)), (lbl, (2, ft), lambda j, i: (0, j))],
                 [o, o, o], "hgrn_prep")


def _hgrn_prep_bwd(proj, lbl, dq, dk, db):
    t = proj.shape[0]
    tt, ft = _tile(t, (256, 128)), 512

    def fn(hq, hf, lbl, dq, dk, db):
        dlog_f = _hdot_raw(_chunk_sum_matrix(tt, HG_CHUNK, transpose=True), db)
        _, vjp = jax.vjp(_hgrn_gates, hq, hf, lbl)
        return vjp((dq, dk, dlog_f))

    o = ((t, D_MODEL), BF16, *_rows(ft, tt), None)
    r = _rows(ft, tt)
    return _tmap(fn, (D_MODEL // ft, t // tt),
                 [(proj, *_rows(ft, tt, COL_HQ * HEAD // ft)), (proj, *_rows(ft, tt, COL_HF * HEAD // ft)), (lbl, (2, ft), lambda j, i: (0, j)),
                  (dq, *r), (dk, *r), (db, *r)],
                 [o, o, ((2, D_MODEL), F32, (2, ft), lambda j, i: (0, j), "inner")], "hgrn_prep_bwd")


def _hgrn_chunk(q, k, v, b, st):
    n = q.shape[0]
    srow = lax.broadcasted_iota(jnp.int32, (n, HEAD), 0)
    inter = _bdot(q * jnp.exp(b), st, "nt")
    rows = []
    for t in range(n):
        e = jnp.where(srow <= t, jnp.exp(jnp.minimum(b[t:t + 1, :] - b, 0.0)), 0.0)
        a = jnp.sum(q[t:t + 1, :] * k * e, axis=1, keepdims=True)
        rows.append(jnp.sum(a * v, axis=0, keepdims=True))
    o = inter + jnp.concatenate(rows, axis=0)
    bend = b[n - 1:n, :]
    st_new = st * jnp.exp(bend) + _bdot(v, k * jnp.exp(bend - b), "tn")
    return o, st_new


def _hgrn_blocks(t):
    per = GDN_CHUNK // HG_CHUNK
    blk = (GDN_CHUNK, HEAD)
    return per, blk


def _hgrn_rec_fwd(q, k, proj, b):
    t = q.shape[0]
    nc = t // GDN_CHUNK
    per, blk = _hgrn_blocks(t)
    im = lambda h, c: (c, h)

    def body(q_ref, k_ref, v_ref, b_ref, o_ref, hs_ref, st_ref):
        @pl.when(pl.program_id(1) == 0)
        def _():
            st_ref[...] = jnp.zeros_like(st_ref)

        for j in range(per):
            sl = pl.ds(HG_CHUNK * j, HG_CHUNK)
            st = st_ref[...]
            hs_ref[j] = st
            o, st_new = _hgrn_chunk(q_ref[sl, :], k_ref[sl, :], v_ref[sl, :], b_ref[sl, :], st)
            o_ref[sl, :] = o
            st_ref[...] = st_new

    return pl.pallas_call(
        body, grid=(HG_HEADS, nc),
        in_specs=[pl.BlockSpec(blk, im), pl.BlockSpec(blk, im), pl.BlockSpec(blk, lambda h, c: (c, COL_HI + h)), pl.BlockSpec(blk, im)],
        out_specs=[pl.BlockSpec(blk, im), pl.BlockSpec((None, per, HEAD, HEAD), lambda h, c: (h, c, 0, 0))],
        out_shape=[SDS((t, D_MODEL), F32), SDS((HG_HEADS, nc * per, HEAD, HEAD), F32)],
        scratch_shapes=[pltpu.VMEM((HEAD, HEAD), F32)], name="hgrn_rec", compiler_params=_params(2),
    )(q, k, proj, b)


def _hgrn_rec_bwd(q, k, proj, b, hs, do):
    t = q.shape[0]
    nc = t // GDN_CHUNK
    per, blk = _hgrn_blocks(t)
    im = lambda h, c: (nc - 1 - c, h)

    def body(q_ref, k_ref, v_ref, b_ref, hs_ref, do_ref, dq_ref, dk_ref, dv_ref, db_ref, dst_ref):
        @pl.when(pl.program_id(1) == 0)
        def _():
            dst_ref[...] = jnp.zeros_like(dst_ref)

        for j in reversed(range(per)):
            sl = pl.ds(HG_CHUNK * j, HG_CHUNK)
            _, vjp = jax.vjp(_hgrn_chunk, q_ref[sl, :], k_ref[sl, :], v_ref[sl, :], b_ref[sl, :], hs_ref[j])
            dq, dk, dv, db, dst = vjp((do_ref[sl, :], dst_ref[...]))
            dq_ref[sl, :] = dq
            dk_ref[sl, :] = dk
            dv_ref[sl, :] = dv.astype(dv_ref.dtype)
            db_ref[sl, :] = db
            dst_ref[...] = dst

    spec = pl.BlockSpec(blk, im)
    return pl.pallas_call(
        body, grid=(HG_HEADS, nc),
        in_specs=[spec, spec, pl.BlockSpec(blk, lambda h, c: (nc - 1 - c, COL_HI + h)), spec,
                  pl.BlockSpec((None, per, HEAD, HEAD), lambda h, c: (h, nc - 1 - c, 0, 0)), spec],
        out_specs=[spec, spec, spec, spec],
        out_shape=[SDS((t, D_MODEL), F32), SDS((t, D_MODEL), F32), SDS((t, D_MODEL), BF16), SDS((t, D_MODEL), F32)],
        scratch_shapes=[pltpu.VMEM((HEAD, HEAD), F32)], name="hgrn_rec_bwd", compiler_params=_params(2),
    )(q, k, proj, b, hs, do)


def _shift_down(x, d):
    if d == 0:
        return x
    row = lax.broadcasted_iota(jnp.int32, x.shape, 0)
    return jnp.where(row >= d, pltpu.roll(x, d, 0), 0.0)


def _shift_up(x, d):
    if d == 0:
        return x
    n = x.shape[0]
    row = lax.broadcasted_iota(jnp.int32, x.shape, 0)
    return jnp.where(row < n - d, pltpu.roll(x, n - d, 0), 0.0)


def _conv_fwd(proj, conv_w):
    t = proj.shape[0]
    width = 2 * D_MODEL + 2 * D_MODEL

    def body(x_ref, w_ref, c_ref):
        x, w = x_ref[...], w_ref[...]
        y = w[CONV_K - 1:CONV_K, :] * x
        for j in range(CONV_K - 1):
            y = y + w[j:j + 1, :] * _shift_down(x, CONV_K - 1 - j)
        c_ref[...] = _silu(y)

    return pl.pallas_call(
        body, grid=(width // HEAD,),
        in_specs=[pl.BlockSpec((t, HEAD), lambda j: (0, COL_GQ + j)), pl.BlockSpec((CONV_K, HEAD), lambda j: (0, j))],
        out_specs=pl.BlockSpec((t, HEAD), lambda j: (0, j)), out_shape=SDS((t, width), F32),
        name="gdn_conv", compiler_params=_params(1),
    )(proj, conv_w)


def _conv_bwd(proj, conv_w, dc):
    t = proj.shape[0]
    width = dc.shape[1]

    def body(x_ref, w_ref, dc_ref, dx_ref, dw_ref):
        x, w = x_ref[...], w_ref[...]
        xs = [_shift_down(x, CONV_K - 1 - j) for j in range(CONV_K)]
        y = w[0:1, :] * xs[0]
        for j in range(1, CONV_K):
            y = y + w[j:j + 1, :] * xs[j]
        sg = jax.nn.sigmoid(y)
        dy = dc_ref[...] * (sg * (1.0 + y * (1.0 - sg)))
        dx = w[CONV_K - 1:CONV_K, :] * dy
        for j in range(CONV_K - 1):
            dx = dx + w[j:j + 1, :] * _shift_up(dy, CONV_K - 1 - j)
        dx_ref[...] = dx.astype(dx_ref.dtype)
        dw_ref[...] = jnp.concatenate([jnp.sum(xs[j] * dy, axis=0, keepdims=True) for j in range(CONV_K)], axis=0)

    return pl.pallas_call(
        body, grid=(width // HEAD,),
        in_specs=[pl.BlockSpec((t, HEAD), lambda j: (0, COL_GQ + j)), pl.BlockSpec((CONV_K, HEAD), lambda j: (0, j)),
                  pl.BlockSpec((t, HEAD), lambda j: (0, j))],
        out_specs=[pl.BlockSpec((t, HEAD), lambda j: (0, j)), pl.BlockSpec((CONV_K, HEAD), lambda j: (0, j))],
        out_shape=[SDS((t, width), BF16), SDS((CONV_K, width), F32)],
        name="gdn_conv_bwd", compiler_params=_params(1),
    )(proj, conv_w, dc)


def _l2norm(x, scale):
    return x * lax.rsqrt(jnp.sum(x * x, axis=-1, keepdims=True) + EPS) * scale


def _l2_fwd(c, col_off, scale, name):
    t = c.shape[0]
    tt = _tile(t, (256, 128))
    return _tmap(lambda x: _l2norm(x, scale), (GDN_QK_HEADS, t // tt), [(c, *_rows(HEAD, tt, col_off))],
                 [((t, D_MODEL), F32, *_rows(HEAD, tt), None)], name)[0]


def _l2_bwd(c, col_off, scale, d_rep, name):
    t = c.shape[0]
    tt = _tile(t, (256, 128))

    def fn(x, d2):
        _, vjp = jax.vjp(lambda x: _l2norm(x, scale), x)
        return vjp(d2[:, :HEAD] + d2[:, HEAD:])[0]

    return _tmap(fn, (GDN_QK_HEADS, t // tt), [(c, *_rows(HEAD, tt, col_off)), (d_rep, *_rows(2 * HEAD, tt))],
                 [((t, D_MODEL), F32, *_rows(HEAD, tt), None)], name)[0]


def _gdn_gates(x, alog, dtb):
    return -jnp.exp(alog) * _softplus(x + dtb), jax.nn.sigmoid(x)


def _gates_fwd(pab, alog, dtb):
    t = pab.shape[0]
    tt = _tile(t, (256, 128))

    def fn(x, alog, dtb):
        g, beta = _gdn_gates(x, alog, dtb)
        lane = lax.broadcasted_iota(jnp.int32, g.shape, 1)
        return jnp.where(lane < GDN_V_HEADS, _hdot_raw(_chunk_sum_matrix(tt, GDN_CHUNK), g), beta)

    p = (alog, (1, HEAD), lambda j, i: (0, 0)), (dtb, (1, HEAD), lambda j, i: (0, 0))
    return _tmap(fn, (1, t // tt), [(pab, *_rows(HEAD, tt)), *p], [((t, HEAD), F32, *_rows(HEAD, tt), None)], "gdn_gates")[0]


def _gates_bwd(pab, alog, dtb, dout):
    t = pab.shape[0]
    tt = _tile(t, (256, 128))

    def fn(x, alog, dtb, dout):
        lane = lax.broadcasted_iota(jnp.int32, dout.shape, 1)
        dgam = jnp.where(lane < GDN_V_HEADS, dout, 0.0)
        dbeta = jnp.where(jnp.logical_and(lane >= GDN_V_HEADS, lane < 2 * GDN_V_HEADS), dout, 0.0)
        dg = _hdot_raw(_chunk_sum_matrix(tt, GDN_CHUNK, transpose=True), dgam)
        _, vjp = jax.vjp(_gdn_gates, x, alog, dtb)
        return vjp((dg, dbeta))

    p = (alog, (1, HEAD), lambda j, i: (0, 0)), (dtb, (1, HEAD), lambda j, i: (0, 0))
    acc = ((1, HEAD), F32, (1, HEAD), lambda j, i: (0, 0), "inner")
    return _tmap(fn, (1, t // tt), [(pab, *_rows(HEAD, tt)), *p, (dout, *_rows(HEAD, tt))],
                 [((t, HEAD), BF16, *_rows(HEAD, tt), None), acc, acc], "gdn_gates_bwd")


def _unit_lower_inverse_raw(a):
    n = a.shape[0]
    row = lax.broadcasted_iota(jnp.int32, (n, n), 0)
    col = lax.broadcasted_iota(jnp.int32, (n, n), 1)
    p = jnp.where(row == col, 1.0, 0.0).astype(F32) - a
    x = a
    m = 2
    while m < 2 * n:
        x = _hdot_raw(x, x)
        p = p + _hdot_raw(p, x)
        m *= 2
    return p


@jax.custom_vjp
def _unit_lower_inverse(a, known):
    return _unit_lower_inverse_raw(a) if known is None else known


def _uli_fwd(a, known):
    inv = _unit_lower_inverse(a, known)
    return inv, (inv, known)


def _uli_bwd(res, ct):
    inv, known = res
    da = -_hdot_raw(inv.T, _hdot_raw(ct, inv.T))
    return da, (None if known is None else jnp.zeros_like(known))


_unit_lower_inverse.defvjp(_uli_fwd, _uli_bwd)


def _gdn_chunk(q, k, v, beta, gam, gam_row, s, inv_known=None):
    n = q.shape[0]
    row = lax.broadcasted_iota(jnp.int32, (n, n), 0)
    col = lax.broadcasted_iota(jnp.int32, (n, n), 1)
    decay = jnp.where(row >= col, jnp.exp(jnp.minimum(gam - gam_row, 0.0)), 0.0)
    kb = k * beta
    a = jnp.where(row > col, _bdot(kb, k, "nt") * decay, 0.0)
    inv = _unit_lower_inverse(a, inv_known)
    eg = jnp.exp(gam)
    u = _bdot(inv, v * beta, "nn")
    w = _bdot(inv, kb * eg, "nn")
    qk = _bdot(q, k, "nt") * decay
    v_new = u - _bdot(w, s, "nn")
    o = _bdot(q * eg, s, "nn") + _bdot(qk, v_new, "nn")
    gend = gam[n - 1:n, :]
    s_new = s * jnp.exp(gend) + _bdot(k * jnp.exp(gend - gam), v_new, "tn")
    return o, s_new, inv


def _gdn_specs(nc, rev):
    cc = (lambda c: nc - 1 - c) if rev else (lambda c: c)
    blk = (GDN_CHUNK, HEAD)
    qk = pl.BlockSpec(blk, lambda h, c: (cc(c), h // 2))
    v = pl.BlockSpec(blk, lambda h, c: (cc(c), 2 * GDN_QK_HEADS + h))
    o = pl.BlockSpec(blk, lambda h, c: (cc(c), h))
    col = pl.BlockSpec((None, None, GDN_CHUNK, 1), lambda h, c: (h, cc(c), 0, 0))
    rw = pl.BlockSpec((None, None, 1, GDN_CHUNK), lambda h, c: (h, cc(c), 0, 0))
    st = pl.BlockSpec((None, None, HEAD, HEAD), lambda h, c: (h, cc(c), 0, 0))
    inv = pl.BlockSpec((None, None, GDN_CHUNK, GDN_CHUNK), lambda h, c: (h, cc(c), 0, 0))
    return qk, v, o, col, rw, st, inv


def _gdn_rec_fwd(qg, kg, c, beta_col, gam_col, gam_row):
    t = qg.shape[0]
    nc = t // GDN_CHUNK
    qk, v, o, col, rw, st, inv = _gdn_specs(nc, False)

    def body(q_ref, k_ref, v_ref, be_ref, g_ref, gr_ref, o_ref, ss_ref, inv_ref, s_ref):
        @pl.when(pl.program_id(1) == 0)
        def _():
            s_ref[...] = jnp.zeros_like(s_ref)

        s = s_ref[...]
        ss_ref[...] = s
        out, s_new, inv_c = _gdn_chunk(q_ref[...], k_ref[...], v_ref[...], be_ref[...], g_ref[...], gr_ref[...], s)
        o_ref[...] = out
        inv_ref[...] = inv_c
        s_ref[...] = s_new

    return pl.pallas_call(
        body, grid=(GDN_V_HEADS, nc), in_specs=[qk, qk, v, col, col, rw], out_specs=[o, st, inv],
        out_shape=[SDS((t, 2 * D_MODEL), F32), SDS((GDN_V_HEADS, nc, HEAD, HEAD), F32), SDS((GDN_V_HEADS, nc, GDN_CHUNK, GDN_CHUNK), F32)],
        scratch_shapes=[pltpu.VMEM((HEAD, HEAD), F32)], name="gdn_rec", compiler_params=_params(2),
    )(qg, kg, c, beta_col, gam_col, gam_row)


def _gdn_rec_bwd(qg, kg, c, beta_col, gam_col, gam_row, ss, invs, do):
    t = qg.shape[0]
    nc = t // GDN_CHUNK
    qk, v, o, col, rw, st, inv = _gdn_specs(nc, True)

    def body(q_ref, k_ref, v_ref, be_ref, g_ref, gr_ref, ss_ref, inv_ref, do_ref,
             dq_ref, dk_ref, dv_ref, dbe_ref, dg_ref, dgr_ref, ds_ref):
        @pl.when(pl.program_id(1) == 0)
        def _():
            ds_ref[...] = jnp.zeros_like(ds_ref)

        _, vjp = jax.vjp(_gdn_chunk, q_ref[...], k_ref[...], v_ref[...], be_ref[...], g_ref[...], gr_ref[...], ss_ref[...], inv_ref[...])
        dq, dk, dv, dbe, dg, dgr, ds, _ = vjp((do_ref[...], ds_ref[...], jnp.zeros((GDN_CHUNK, GDN_CHUNK), F32)))
        dq_ref[...] = dq
        dk_ref[...] = dk
        dv_ref[...] = dv
        dbe_ref[...] = dbe
        dg_ref[...] = dg
        dgr_ref[...] = dgr
        ds_ref[...] = ds

    wide = SDS((t, 2 * D_MODEL), F32)
    colshape = SDS((GDN_V_HEADS, nc, GDN_CHUNK, 1), F32)
    return pl.pallas_call(
        body, grid=(GDN_V_HEADS, nc), in_specs=[qk, qk, v, col, col, rw, st, inv, o], out_specs=[o, o, o, col, col, rw],
        out_shape=[wide, wide, wide, colshape, colshape, SDS((GDN_V_HEADS, nc, 1, GDN_CHUNK), F32)],
        scratch_shapes=[pltpu.VMEM((HEAD, HEAD), F32)], name="gdn_rec_bwd", compiler_params=_params(2),
    )(qg, kg, c, beta_col, gam_col, gam_row, ss, invs, do)


def _gated_norm(o, gate, w):
    return _rms(o, w) * _silu(gate)


def _post_fwd(o, proj, col_off, w, name):
    t, width = o.shape
    tt = _tile(t, (256, 128))
    return _tmap(_gated_norm, (width // HEAD, t // tt),
                 [(o, *_rows(HEAD, tt)), (proj, *_rows(HEAD, tt, col_off)), (w, (1, HEAD), lambda j, i: (0, 0))],
                 [((t, width), BF16, *_rows(HEAD, tt), None)], name)[0]


def _post_bwd(o, proj, col_off, w, dout, name):
    t, width = o.shape
    tt = _tile(t, (256, 128))

    def fn(o, gate, w, dout):
        _, vjp = jax.vjp(_gated_norm, o, gate, w)
        return vjp(dout)

    return _tmap(fn, (width // HEAD, t // tt),
                 [(o, *_rows(HEAD, tt)), (proj, *_rows(HEAD, tt, col_off)), (w, (1, HEAD), lambda j, i: (0, 0)), (dout, *_rows(HEAD, tt))],
                 [((t, width), F32, *_rows(HEAD, tt), None), ((t, width), BF16, *_rows(HEAD, tt), None),
                  ((1, HEAD), F32, (1, HEAD), lambda j, i: (0, 0), "all")], name)


def _merge(gate_h, gate_g, yh, yg):
    return jax.nn.sigmoid(gate_h) * yh + jax.nn.sigmoid(gate_g) * yg


def _merge_fwd(proj, yh, yg):
    t = yh.shape[0]
    tt, ft = _tile(t, (256, 128)), 512
    r = _rows(ft, tt)
    return _tmap(_merge, (D_MODEL // ft, t // tt),
                 [(proj, *_rows(ft, tt, COL_GATE_H * HEAD // ft)), (proj, *_rows(ft, tt, COL_GATE_G * HEAD // ft)), (yh, *r), (yg, *r)],
                 [((t, D_MODEL), BF16, *r, None)], "merge")[0]


def _merge_bwd(proj, yh, yg, dy):
    t = yh.shape[0]
    tt, ft = _tile(t, (256, 128)), 512
    r = _rows(ft, tt)

    def fn(gate_h, gate_g, yh, yg, dy):
        _, vjp = jax.vjp(_merge, gate_h, gate_g, yh, yg)
        return vjp(dy)

    o = ((t, D_MODEL), BF16, *r, None)
    return _tmap(fn, (D_MODEL // ft, t // tt),
                 [(proj, *_rows(ft, tt, COL_GATE_H * HEAD // ft)), (proj, *_rows(ft, tt, COL_GATE_G * HEAD // ft)), (yh, *r), (yg, *r), (dy, *r)],
                 [o, o, o, o], "merge_bwd")


def _loss_head(h, target, g):
    t, d = h.shape
    tt = _tile(t, (256, 128))

    def fn(h, target, g):
        def f(h, g):
            err = _rms(h, g) - target
            return 0.5 * jnp.sum(jnp.mean(err * err, axis=-1))

        loss, (dh, dg) = jax.value_and_grad(f, (0, 1))(h, g)
        return dh, dg, jnp.full((1, HEAD), loss, F32)

    return _tmap(fn, (1, t // tt), [(h, *_rows(d, tt)), (target, *_rows(d, tt)), (g, (1, d), lambda j, i: (0, 0))],
                 [((t, d), F32, *_rows(d, tt), None), ((1, d), F32, (1, d), lambda j, i: (0, 0), "inner"),
                  ((1, HEAD), F32, (1, HEAD), lambda j, i: (0, 0), "inner")], "loss_head")


def _heads_to_cols(a):
    t = a.shape[0]
    return a.T.reshape(GDN_V_HEADS, t // GDN_CHUNK, GDN_CHUNK, 1)


def _mixer_fwd(h, p, w):
    t = h.shape[0]
    nc = t // GDN_CHUNK
    u = _rms_fwd(h, p["mix_norm"], "mix_norm")
    proj = _mm(u, w["w_in_main"], "nn", F32, "mix_in")
    pab = _mm(u, w["w_in_ab"], "nn", F32, "mix_in_ab")
    qh, kh, bh = _hgrn_prep_fwd(proj, p["lbl"])
    oh, hs = _hgrn_rec_fwd(qh, kh, proj, bh)
    c = _conv_fwd(proj, w["conv_w"])
    qg = _l2_fwd(c, 0, HEAD ** -0.5, "gdn_qnorm")
    kg = _l2_fwd(c, GDN_QK_HEADS, 1.0, "gdn_knorm")
    gates = _gates_fwd(pab, p["alog"], p["dtb"])
    gam = gates[:, :GDN_V_HEADS]
    beta_col = _heads_to_cols(gates[:, GDN_V_HEADS:2 * GDN_V_HEADS])
    gam_col = _heads_to_cols(gam)
    gam_row = gam.T.reshape(GDN_V_HEADS, nc, 1, GDN_CHUNK)
    og, ss, invs = _gdn_rec_fwd(qg, kg, c, beta_col, gam_col, gam_row)
    ohn = _post_fwd(oh, proj, COL_HG, p["hgrn_out_norm"], "hgrn_out")
    ogn = _post_fwd(og, proj, COL_GZ, p["gdn_out_norm"], "gdn_out")
    yh = _mm(ohn, w["w_branch_hgrn"], "nn", F32, "branch_hgrn")
    yg = _mm(ogn, w["w_branch_gdn"], "nn", F32, "branch_gdn")
    y = _merge_fwd(proj, yh, yg)
    out = _mm(y, w["w_out"], "nn", F32, "mix_out", res=h)
    saved = (u, proj, pab, qh, kh, bh, oh, hs, c, qg, kg, beta_col, gam_col, gam_row, og, ss, invs, ohn, ogn, yh, yg, y)
    return out, saved


def _mixer_bwd(h, p, w, saved, dout):
    (u, proj, pab, qh, kh, bh, oh, hs, c, qg, kg, beta_col, gam_col, gam_row, og, ss, invs, ohn, ogn, yh, yg, y) = saved
    t = h.shape[0]
    grads = {}
    grads["w_out"] = _mm(y, dout, "tn", BF16, "mix_out_dw")
    dy = _mm(dout, w["w_out"], "nt", F32, "mix_out_dx")
    dgate_h, dgate_g, dyh, dyg = _merge_bwd(proj, yh, yg, dy)
    grads["w_branch_hgrn"] = _mm(ohn, dyh, "tn", BF16, "branch_hgrn_dw")
    grads["w_branch_gdn"] = _mm(ogn, dyg, "tn", BF16, "branch_gdn_dw")
    dohn = _mm(dyh, w["w_branch_hgrn"], "nt", F32, "branch_hgrn_dx")
    dogn = _mm(dyg, w["w_branch_gdn"], "nt", F32, "branch_gdn_dx")
    doh, dhg, grads["hgrn_out_norm"] = _post_bwd(oh, proj, COL_HG, p["hgrn_out_norm"], dohn, "hgrn_out_bwd")
    dog, dgz, grads["gdn_out_norm"] = _post_bwd(og, proj, COL_GZ, p["gdn_out_norm"], dogn, "gdn_out_bwd")
    dqh, dkh, dhi, dbh = _hgrn_rec_bwd(qh, kh, proj, bh, hs, doh)
    dhq, dhf, grads["lbl"] = _hgrn_prep_bwd(proj, p["lbl"], dqh, dkh, dbh)
    dqv, dkv, dcv, dbeta_col, dgam_col, dgam_row = _gdn_rec_bwd(qg, kg, c, beta_col, gam_col, gam_row, ss, invs, dog)
    dcq = _l2_bwd(c, 0, HEAD ** -0.5, dqv, "gdn_qnorm_bwd")
    dck = _l2_bwd(c, GDN_QK_HEADS, 1.0, dkv, "gdn_knorm_bwd")
    dxin, grads["conv_w"] = _conv_bwd(proj, w["conv_w"], jnp.concatenate([dcq, dck, dcv], axis=1))
    dgam = (dgam_col.reshape(GDN_V_HEADS, t) + dgam_row.reshape(GDN_V_HEADS, t)).T
    dbeta = dbeta_col.reshape(GDN_V_HEADS, t).T
    dgates = jnp.concatenate([dgam, dbeta, jnp.zeros((t, HEAD - 2 * GDN_V_HEADS), F32)], axis=1)
    dpab, grads["alog"], grads["dtb"] = _gates_bwd(pab, p["alog"], p["dtb"], dgates)
    dproj = jnp.concatenate([dhq, dhf, dhi, dhg, dxin, dgz, dgate_h, dgate_g], axis=1)
    grads["w_in_main"] = _mm(u, dproj, "tn", BF16, "mix_in_dw")
    grads["w_in_ab"] = _mm(u, dpab, "tn", BF16, "mix_in_ab_dw")
    du = _mm(dproj, w["w_in_main"], "nt", F32, "mix_in_dx")
    du = _mm(dpab, w["w_in_ab"], "nt", F32, "mix_in_ab_dx", res=du)
    dh, grads["mix_norm"] = _rms_bwd(h, p["mix_norm"], du, dout, "mix_norm_bwd")
    return dh, grads


def _local_step(x, target, p, w):
    h1, s1 = _ffn_fwd(x, p["ffn1_norm"], w["ffn1_w_in"], w["ffn1_w_out"], "ffn1")
    h2, sm = _mixer_fwd(h1, p, w)
    h3, s2 = _ffn_fwd(h2, p["ffn2_norm"], w["ffn2_w_in"], w["ffn2_w_out"], "ffn2")
    dh3, dfinal, loss = _loss_head(h3, target, p["final_norm"])
    g = {"final_norm": dfinal}
    dh2, g["ffn2_norm"], g["ffn2_w_in"], g["ffn2_w_out"] = _ffn_bwd(h2, p["ffn2_norm"], w["ffn2_w_in"], w["ffn2_w_out"], s2, dh3, "ffn2")
    dh1, gm = _mixer_bwd(h1, p, w, sm, dh2)
    g.update(gm)
    dx, g["ffn1_norm"], g["ffn1_w_in"], g["ffn1_w_out"] = _ffn_bwd(x, p["ffn1_norm"], w["ffn1_w_in"], w["ffn1_w_out"], s1, dh1, "ffn1")
    return loss, dx, g


def _exchange(items, name):
    n = len(items)
    out_shape = [SDS((N_DEV,) + a.shape if mode == "gather" else a.shape, a.dtype) for a, mode in items]

    def body(*refs):
        in_refs, out_refs = refs[:n], refs[n:2 * n]
        send_sems, recv_sems, local_sems = refs[2 * n:]
        x, y, c = lax.axis_index("x"), lax.axis_index("y"), lax.axis_index("c")
        me = 4 * x + 2 * y + c
        local = []
        for i, (_, mode) in enumerate(items):
            src = in_refs[i] if mode == "gather" else in_refs[i].at[me]
            cp = pltpu.make_async_copy(src, out_refs[i].at[me], local_sems.at[i])
            cp.start()
            local.append(cp)
        remote = []
        for rel in range(1, N_DEV):
            px = 1 - x if rel & 4 else x
            py = 1 - y if rel & 2 else y
            pc = 1 - c if rel & 1 else c
            peer = 4 * px + 2 * py + pc
            for i, (_, mode) in enumerate(items):
                src = in_refs[i] if mode == "gather" else in_refs[i].at[peer]
                cp = pltpu.make_async_remote_copy(src_ref=src, dst_ref=out_refs[i].at[me], send_sem=send_sems.at[i, rel - 1],
                                                  recv_sem=recv_sems.at[i, rel - 1], device_id=(px, py, pc), device_id_type=MESH_IDS)
                cp.start()
                remote.append(cp)
        for cp in remote:
            cp.wait()
        for cp in local:
            cp.wait()

    anyspace = pl.BlockSpec(memory_space=pl.ANY)
    return pl.pallas_call(
        body, in_specs=[anyspace] * n, out_specs=[anyspace] * n, out_shape=out_shape,
        scratch_shapes=[pltpu.SemaphoreType.DMA((n, N_DEV - 1)), pltpu.SemaphoreType.DMA((n, N_DEV - 1)), pltpu.SemaphoreType.DMA((n,))],
        name=name, compiler_params=pltpu.CompilerParams(has_side_effects=True),
    )(*[a for a, _ in items])


def _adam(parts, w, m, v, name):
    n_parts, r, c = parts.shape
    tr = _tile(r, (256, 176, 128, 64, 32, 16, 8))

    def body(p_ref, w_ref, m_ref, v_ref, g_ref, d_ref, mo_ref, vo_ref):
        g = p_ref[0].astype(F32)
        for i in range(1, n_parts):
            g = g + p_ref[i].astype(F32)
        m_new = ADAM_B1 * m_ref[...] + (1.0 - ADAM_B1) * g
        v_new = ADAM_B2 * v_ref[...] + (1.0 - ADAM_B2) * (g * g)
        m_hat = m_new / (1.0 - ADAM_B1 ** ADAM_STEP)
        v_hat = v_new / (1.0 - ADAM_B2 ** ADAM_STEP)
        g_ref[...] = g
        d_ref[...] = -ADAM_LR * (m_hat / (jnp.sqrt(v_hat) + ADAM_EPS) + ADAM_WD * w_ref[...])
        mo_ref[...] = m_new
        vo_ref[...] = v_new

    spec = pl.BlockSpec((tr, c), lambda i: (i, 0))
    return pl.pallas_call(
        body, grid=(r // tr,), in_specs=[pl.BlockSpec((n_parts, tr, c), lambda i: (0, i, 0)), spec, spec, spec],
        out_specs=[spec] * 4, out_shape=[SDS((r, c), F32)] * 4, name=name, compiler_params=_params(1),
    )(parts, w, m, v)


BIG = ("ffn1_w_in", "ffn1_w_out", "w_in", "w_branch_hgrn", "w_branch_gdn", "w_out", "ffn2_w_in", "ffn2_w_out")


PACK_ROW_TILE = 128


def _pad_rows(a, axis):
    extra = -a.shape[axis] % PACK_ROW_TILE
    return jnp.pad(a, [(0, extra if ax == axis else 0) for ax in range(a.ndim)])


def _pack_rows(arrays):
    flat = [a.reshape(-1, D_MODEL) for a in arrays]
    return _pad_rows(jnp.concatenate(flat, axis=0), 0), [f.shape[0] for f in flat]


def _unpack_rows(packed, counts, shapes, axis):
    out, start = [], 0
    for n, shape in zip(counts, shapes):
        piece = lax.slice_in_dim(packed, start, start + n, axis=axis)
        out.append(piece.reshape(packed.shape[:axis] + tuple(shape)))
        start += n
    return out


def _ffn_in_layout(g):
    return jnp.pad(g, ((0, 0), (0, 0), (0, FF_SHARD_PAD - FF_SHARD))).transpose(1, 0, 2).reshape(D_MODEL, 2 * FF_PAD)


def _ffn_in_shards(d):
    return d.reshape(D_MODEL, N_DEV, FF_SHARD_PAD)[:, :, :FF_SHARD].transpose(1, 0, 2)


def _ffn_out_layout(g):
    g = g.reshape(N_DEV // 2, FF_SHARD, D_MODEL)
    return jnp.pad(g, ((0, 0), (0, FF_SHARD_PAD - FF_SHARD), (0, 0))).reshape(FF_PAD, D_MODEL)


def _ffn_out_shards(d):
    return d.reshape(N_DEV // 2, FF_SHARD_PAD, D_MODEL)[:, :FF_SHARD].reshape(N_DEV, FF_SHARD // 2, D_MODEL)


SCALAR_COLS = 8192


def _w_in_layout(g):
    nat = g.transpose(1, 0, 2).reshape(D_MODEL, IN_WIDTH)
    main = jnp.concatenate([nat[:, :SCALAR_COLS], nat[:, SCALAR_COLS + 2 * GDN_V_HEADS:]], axis=1)
    ab = jnp.pad(nat[:, SCALAR_COLS:SCALAR_COLS + 2 * GDN_V_HEADS], ((0, 0), (0, HEAD - 2 * GDN_V_HEADS)))
    return main, ab


def _w_in_shards(dmain, dab):
    nat = jnp.concatenate([dmain[:, :SCALAR_COLS], dab[:, :2 * GDN_V_HEADS], dmain[:, SCALAR_COLS:]], axis=1)
    return nat.reshape(D_MODEL, N_DEV, IN_WIDTH // N_DEV).transpose(1, 0, 2)


def _pad_lanes(a, width=HEAD):
    return jnp.pad(a, ((0, 0), (0, width - a.shape[1])))


SMALL_ROWS = 24


def _pack_small(g, loss):
    row6 = jnp.concatenate([g["hgrn_out_norm"], g["gdn_out_norm"], g["alog"], g["dtb"], loss,
                            jnp.zeros((1, D_MODEL - 5 * HEAD), F32)], axis=1)
    return jnp.concatenate([g["ffn1_norm"], g["mix_norm"], g["lbl"], g["ffn2_norm"], g["final_norm"], row6,
                            jnp.zeros((1, D_MODEL), F32), g["conv_w"].reshape(4 * CONV_K, D_MODEL)], axis=0)


def _pack_small_state(a):
    row6 = jnp.concatenate([a["hgrn_out_norm"], a["gdn_out_norm"], _pad_lanes(a["gdn_a_log"]), _pad_lanes(a["gdn_dt_bias"]),
                            jnp.zeros((1, D_MODEL - 4 * HEAD), F32)], axis=1)
    return jnp.concatenate([a["ffn1_norm"], a["mix_norm"], a["hgrn_lb_logits"], a["ffn2_norm"], a["final_norm"].reshape(1, D_MODEL),
                            row6, jnp.zeros((1, D_MODEL), F32)], axis=0)


def _unpack_small(a):
    return {"ffn1_norm": a[0:1], "mix_norm": a[1:2], "hgrn_lb_logits": a[2:4], "ffn2_norm": a[4:5], "final_norm": a[5],
            "hgrn_out_norm": a[6:7, :HEAD], "gdn_out_norm": a[6:7, HEAD:2 * HEAD],
            "gdn_a_log": a[6:7, 2 * HEAD:2 * HEAD + GDN_V_HEADS], "gdn_dt_bias": a[6:7, 3 * HEAD:3 * HEAD + GDN_V_HEADS]}


NAMES = ("ffn1_norm", "ffn1_w_in", "ffn1_w_out", "mix_norm", "w_in", "hgrn_lb_logits", "hgrn_out_norm", "gdn_conv_w", "gdn_a_log",
         "gdn_dt_bias", "gdn_out_norm", "w_branch_hgrn", "w_branch_gdn", "w_out", "ffn2_norm", "ffn2_w_in", "ffn2_w_out", "final_norm")


def kernel(x, ffn1_norm, ffn1_w_in, ffn1_w_out, mix_norm, w_in, hgrn_lb_logits, hgrn_out_norm, gdn_conv_w, gdn_a_log, gdn_dt_bias, gdn_out_norm, w_branch_hgrn, w_branch_gdn, w_out, ffn2_norm, ffn2_w_in, ffn2_w_out, final_norm, loss_target, m_ffn1_norm, m_ffn1_w_in, m_ffn1_w_out, m_mix_norm, m_w_in, m_hgrn_lb_logits, m_hgrn_out_norm, m_gdn_conv_w, m_gdn_a_log, m_gdn_dt_bias, m_gdn_out_norm, m_w_branch_hgrn, m_w_branch_gdn, m_w_out, m_ffn2_norm, m_ffn2_w_in, m_ffn2_w_out, m_final_norm, v_ffn1_norm, v_ffn1_w_in, v_ffn1_w_out, v_mix_norm, v_w_in, v_hgrn_lb_logits, v_hgrn_out_norm, v_gdn_conv_w, v_gdn_a_log, v_gdn_dt_bias, v_gdn_out_norm, v_w_branch_hgrn, v_w_branch_gdn, v_w_out, v_ffn2_norm, v_ffn2_w_in, v_ffn2_w_out, v_final_norm):
    wts = dict(zip(NAMES, (ffn1_norm, ffn1_w_in, ffn1_w_out, mix_norm, w_in, hgrn_lb_logits, hgrn_out_norm, gdn_conv_w, gdn_a_log,
                           gdn_dt_bias, gdn_out_norm, w_branch_hgrn, w_branch_gdn, w_out, ffn2_norm, ffn2_w_in, ffn2_w_out, final_norm)))
    mom = dict(zip(NAMES, (m_ffn1_norm, m_ffn1_w_in, m_ffn1_w_out, m_mix_norm, m_w_in, m_hgrn_lb_logits, m_hgrn_out_norm, m_gdn_conv_w,
                           m_gdn_a_log, m_gdn_dt_bias, m_gdn_out_norm, m_w_branch_hgrn, m_w_branch_gdn, m_w_out, m_ffn2_norm, m_ffn2_w_in,
                           m_ffn2_w_out, m_final_norm)))
    var = dict(zip(NAMES, (v_ffn1_norm, v_ffn1_w_in, v_ffn1_w_out, v_mix_norm, v_w_in, v_hgrn_lb_logits, v_hgrn_out_norm, v_gdn_conv_w,
                           v_gdn_a_log, v_gdn_dt_bias, v_gdn_out_norm, v_w_branch_hgrn, v_w_branch_gdn, v_w_out, v_ffn2_norm, v_ffn2_w_in,
                           v_ffn2_w_out, v_final_norm)))
    me = 4 * lax.axis_index("x") + 2 * lax.axis_index("y") + lax.axis_index("c")

    shard_shapes = [wts[n][0].shape for n in BIG]
    packed, counts = _pack_rows([wts[n][0].astype(BF16) for n in BIG])
    conv_shard = wts["gdn_conv_w"][0]
    gathered, conv_all = _exchange([(packed, "gather"), (conv_shard.reshape(2, D_MODEL), "gather")], "gather_weights")
    full = dict(zip(BIG, _unpack_rows(gathered, counts, shard_shapes, 1)))
    w = {"ffn1_w_in": _ffn_in_layout(full["ffn1_w_in"]), "ffn1_w_out": _ffn_out_layout(full["ffn1_w_out"]),
         "ffn2_w_in": _ffn_in_layout(full["ffn2_w_in"]), "ffn2_w_out": _ffn_out_layout(full["ffn2_w_out"]),
         "w_branch_hgrn": full["w_branch_hgrn"].reshape(D_MODEL, D_MODEL), "w_branch_gdn": full["w_branch_gdn"].reshape(2 * D_MODEL, D_MODEL),
         "w_out": full["w_out"].reshape(D_MODEL, D_MODEL),
         "conv_w": conv_all.reshape(N_DEV, CONV_K, 4 * D_MODEL // N_DEV).transpose(1, 0, 2).reshape(CONV_K, 4 * D_MODEL)}
    w["w_in_main"], w["w_in_ab"] = _w_in_layout(full["w_in"])
    p = {"ffn1_norm": wts["ffn1_norm"], "mix_norm": wts["mix_norm"], "ffn2_norm": wts["ffn2_norm"], "final_norm": wts["final_norm"].reshape(1, D_MODEL),
         "lbl": wts["hgrn_lb_logits"], "hgrn_out_norm": wts["hgrn_out_norm"], "gdn_out_norm": wts["gdn_out_norm"],
         "alog": _pad_lanes(wts["gdn_a_log"]), "dtb": _pad_lanes(wts["gdn_dt_bias"])}

    loss, dx, g = _local_step(x[0], loss_target[0], p, w)

    dshards = [_ffn_in_shards(g["ffn1_w_in"]), _ffn_out_shards(g["ffn1_w_out"]), _w_in_shards(g["w_in_main"], g["w_in_ab"]),
               g["w_branch_hgrn"].reshape(N_DEV, -1, D_MODEL), g["w_branch_gdn"].reshape(N_DEV, -1, D_MODEL), g["w_out"].reshape(N_DEV, -1, D_MODEL),
               _ffn_in_shards(g["ffn2_w_in"]), _ffn_out_shards(g["ffn2_w_out"])]
    dpacked = _pad_rows(jnp.concatenate([d.reshape(N_DEV, -1, D_MODEL) for d in dshards], axis=1), 1)
    parts, small_parts = _exchange([(dpacked, "scatter"), (_pack_small(g, loss), "gather")], "scatter_grads")

    wp, _ = _pack_rows([wts[n][0] for n in BIG])
    mp, _ = _pack_rows([mom[n][0] for n in BIG])
    vp, _ = _pack_rows([var[n][0] for n in BIG])
    big = [dict(zip(BIG, _unpack_rows(o, counts, shard_shapes, 0))) for o in _adam(parts, wp, mp, vp, "adam_big")]
    n_vec = SMALL_ROWS - 4 * CONV_K
    small_raw = _adam(small_parts[:, :n_vec], _pack_small_state(wts), _pack_small_state(mom), _pack_small_state(var), "adam_small")
    small = [_unpack_small(o) for o in small_raw]
    loss_total = small_raw[0][6, 4 * HEAD]
    conv_parts = small_parts[:, n_vec:].reshape(N_DEV, CONV_K, 4 * D_MODEL)
    width = 4 * D_MODEL // N_DEV
    conv_mine = lax.dynamic_slice_in_dim(conv_parts, me * width, width, axis=2)
    conv = _adam(conv_mine, conv_shard, mom["gdn_conv_w"][0], var["gdn_conv_w"][0], "adam_conv")

    outs = []
    for kind in range(4):
        for n in NAMES:
            if n in BIG:
                outs.append(big[kind][n][None])
            elif n == "gdn_conv_w":
                outs.append(conv[kind][None])
            else:
                outs.append(small[kind][n])
    return (loss_total, dx[None], *outs)
```

```python
import functools

import jax
import jax.numpy as jnp
from jax import lax
from jax.experimental import pallas as pl
from jax.experimental.pallas import tpu as pltpu

F32 = jnp.float32
BF16 = jnp.bfloat16
HIGHEST = lax.Precision.HIGHEST
MESH_IDS = pl.DeviceIdType.MESH

D_MODEL = 1024
D_FF = 2816
N_DEV = 8
EPS = 1e-6
HEAD = 128
HG_HEADS = 8
GDN_QK_HEADS = 8
GDN_V_HEADS = 16
GDN_CHUNK = 64
HG_CHUNK = 16
CONV_K = 4
IN_WIDTH = 12320
IN_MAIN = 12288
COL_HQ, COL_HF, COL_HI, COL_HG, COL_GQ, COL_GK, COL_GV, COL_GZ, COL_GATE_H, COL_GATE_G = 0, 8, 16, 24, 32, 40, 48, 64, 80, 88
VMEM_LIMIT = 56 * 1024 * 1024

ADAM_LR, ADAM_B1, ADAM_B2, ADAM_EPS, ADAM_WD, ADAM_STEP = 0.001, 0.9, 0.999, 1e-08, 0.01, 10

SDS = jax.ShapeDtypeStruct


def _params(n_axes):
    return pltpu.CompilerParams(dimension_semantics=("arbitrary",) * n_axes, vmem_limit_bytes=VMEM_LIMIT)


def _tile(n, candidates=(512, 384, 256, 128, 64, 32, 16, 8)):
    for c in candidates:
        if n % c == 0:
            return c
    return n


_DIMS = {"nn": ((1,), (0,)), "nt": ((1,), (1,)), "tn": ((0,), (0,))}


def _bdot_raw(a, b, dims):
    return lax.dot_general(a.astype(BF16), b.astype(BF16), (_DIMS[dims], ((), ())), preferred_element_type=F32)


@functools.partial(jax.custom_vjp, nondiff_argnums=(2,))
def _bdot(a, b, dims):
    return _bdot_raw(a, b, dims)


def _bdot_fwd(a, b, dims):
    return _bdot_raw(a, b, dims), (a, b)


def _bdot_bwd(dims, res, ct):
    a, b = res
    if dims == "nn":
        return _bdot_raw(ct, b, "nt"), _bdot_raw(a, ct, "tn")
    if dims == "nt":
        return _bdot_raw(ct, b, "nn"), _bdot_raw(ct, a, "tn")
    return _bdot_raw(b, ct, "nt"), _bdot_raw(a, ct, "nn")


_bdot.defvjp(_bdot_fwd, _bdot_bwd)


def _hdot_raw(a, b):
    return jnp.dot(a, b, precision=HIGHEST, preferred_element_type=F32)


MM_VMEM_BUDGET = 30 * 1024 * 1024


def _mm_tiles(m, n, k, a_bytes, b_bytes, o_bytes, r_bytes):
    tm = _tile(m, (1024, 512, 256, 128, 64, 32, 16, 8))
    tn = _tile(n, (1408, 1024, 512, 256, 128))
    tk = _tile(k, (2048, 1408, 1024, 512, 256, 128, 64, 32, 16, 8))

    def need(tm, tn, tk):
        return 2 * (tm * tk * a_bytes + tk * tn * b_bytes + tm * tn * (o_bytes + r_bytes)) + tm * tn * 4

    while need(tm, tn, tk) > MM_VMEM_BUDGET:
        if tk > 512 and tk % 256 == 0:
            tk //= 2
        elif tn > 512 and tn % 256 == 0:
            tn //= 2
        elif tm > 256:
            tm //= 2
        else:
            break
    return tm, tn, tk


def _mm(a, b, dims, out_dtype, name, res=None, alpha=1.0):
    if dims == "nn":
        (m, k), (k2, n) = a.shape, b.shape
    elif dims == "nt":
        (m, k), (n, k2) = a.shape, b.shape
    else:
        (k, m), (k2, n) = a.shape, b.shape
    assert k == k2, (a.shape, b.shape, dims)
    has_res = res is not None
    tm, tn, tk = _mm_tiles(m, n, k, a.dtype.itemsize, b.dtype.itemsize, jnp.dtype(out_dtype).itemsize, res.dtype.itemsize if has_res else 0)
    nk = k // tk
    a_spec = pl.BlockSpec((tk, tm), lambda i, j, kk: (kk, i)) if dims == "tn" else pl.BlockSpec((tm, tk), lambda i, j, kk: (i, kk))
    b_spec = pl.BlockSpec((tn, tk), lambda i, j, kk: (j, kk)) if dims == "nt" else pl.BlockSpec((tk, tn), lambda i, j, kk: (kk, j))
    o_spec = pl.BlockSpec((tm, tn), lambda i, j, kk: (i, j))

    def finish(acc, r_ref, o_ref):
        out = acc * alpha if alpha != 1.0 else acc
        if has_res:
            out = r_ref[...].astype(F32) + out
        o_ref[...] = out.astype(o_ref.dtype)

    def body(*refs):
        a_ref, b_ref = refs[:2]
        r_ref = refs[2] if has_res else None
        o_ref = refs[3] if has_res else refs[2]
        p = _bdot_raw(a_ref[...], b_ref[...], dims)
        if nk == 1:
            finish(p, r_ref, o_ref)
            return
        acc_ref = refs[-1]
        kk = pl.program_id(2)

        @pl.when(kk == 0)
        def _():
            acc_ref[...] = p

        @pl.when(kk > 0)
        def _():
            acc_ref[...] += p

        @pl.when(kk == nk - 1)
        def _():
            finish(acc_ref[...], r_ref, o_ref)

    args = (a, b, res) if has_res else (a, b)
    in_specs = [a_spec, b_spec] + ([o_spec] if has_res else [])
    return pl.pallas_call(
        body, grid=(m // tm, n // tn, nk), in_specs=in_specs, out_specs=o_spec, out_shape=SDS((m, n), out_dtype),
        scratch_shapes=[pltpu.VMEM((tm, tn), F32)] if nk > 1 else [], name=name, compiler_params=_params(3),
    )(*args)


def _tmap(fn, grid, ins, outs, name):
    n_in = len(ins)
    n_ax = len(grid)

    def body(*refs):
        vals = fn(*[r[...] for r in refs[:n_in]])
        if not isinstance(vals, (tuple, list)):
            vals = (vals,)
        first_inner = pl.program_id(n_ax - 1) == 0
        first_all = first_inner
        for ax in range(n_ax - 1):
            first_all = jnp.logical_and(first_all, pl.program_id(ax) == 0)

        def put(ref, val, acc):
            val = val.astype(ref.dtype)
            if acc is None:
                ref[...] = val
                return
            first = first_inner if acc == "inner" else first_all

            @pl.when(first)
            def _():
                ref[...] = val

            @pl.when(jnp.logical_not(first))
            def _():
                ref[...] += val

        for ref, val, o in zip(refs[n_in:], vals, outs):
            put(ref, val, o[4])

    return pl.pallas_call(
        body, grid=grid,
        in_specs=[pl.BlockSpec(bs, im) for _, bs, im in ins],
        out_specs=[pl.BlockSpec(o[2], o[3]) for o in outs],
        out_shape=[SDS(o[0], o[1]) for o in outs],
        name=name, compiler_params=_params(n_ax),
    )(*[a for a, _, _ in ins])


def _rows(width, tt, off=0):
    return (tt, width), (lambda j, i: (i, off + j))


def _rms(x, g):
    x = x.astype(F32)
    return x * lax.rsqrt(jnp.mean(x * x, axis=-1, keepdims=True) + EPS) * g


def _silu(x):
    return x * jax.nn.sigmoid(x)


def _softplus(x):
    return jnp.maximum(x, 0.0) + jnp.log1p(jnp.exp(-jnp.abs(x)))


def _rms_fwd(x, g, name):
    t, d = x.shape
    tt = _tile(t, (256, 128))
    return _tmap(_rms, (1, t // tt), [(x, *_rows(d, tt)), (g, (1, d), lambda j, i: (0, 0))],
                 [((t, d), BF16, *_rows(d, tt), None)], name)[0]


def _rms_bwd(x, g, dn, dres, name):
    t, d = x.shape
    tt = _tile(t, (256, 128))

    def fn(x, g, dn, dres):
        _, vjp = jax.vjp(_rms, x, g)
        dx, dg = vjp(dn.astype(F32))
        return dres + dx, dg

    return _tmap(fn, (1, t // tt),
                 [(x, *_rows(d, tt)), (g, (1, d), lambda j, i: (0, 0)), (dn, *_rows(d, tt)), (dres, *_rows(d, tt))],
                 [((t, d), F32, *_rows(d, tt), None), ((1, d), F32, (1, d), lambda j, i: (0, 0), "inner")], name)


def _swiglu(ab):
    return _silu(ab[:, :D_FF]) * ab[:, D_FF:]


def _swiglu_fwd(ab, name):
    t = ab.shape[0]
    tt = _tile(t, (128,))
    return _tmap(_swiglu, (1, t // tt), [(ab, *_rows(2 * D_FF, tt))], [((t, D_FF), BF16, *_rows(D_FF, tt), None)], name)[0]


def _swiglu_bwd(ab, ds, name):
    t = ab.shape[0]
    tt = _tile(t, (128,))

    def fn(ab, ds):
        a, b = ab[:, :D_FF], ab[:, D_FF:]
        _, vjp = jax.vjp(lambda a, b: _silu(a) * b, a, b)
        da, db = vjp(ds.astype(F32))
        return jnp.concatenate([da, db], axis=1)

    return _tmap(fn, (1, t // tt), [(ab, *_rows(2 * D_FF, tt)), (ds, *_rows(D_FF, tt))],
                 [((t, 2 * D_FF), BF16, *_rows(2 * D_FF, tt), None)], name)[0]


def _ffn_fwd(h, g, w_in_t, w_out, tag):
    n = _rms_fwd(h, g, tag + "_norm")
    ab = _mm(n, w_in_t, "nt", F32, tag + "_in")
    s = _swiglu_fwd(ab, tag + "_act")
    out = _mm(s, w_out, "nn", F32, tag + "_out", res=h, alpha=0.5)
    return out, (n, ab, s)


def _ffn_bwd(h, g, w_in_t, w_out, saved, dout, tag):
    n, ab, s = saved
    dw_out = _mm(s, dout, "tn", BF16, tag + "_dw_out", alpha=0.5)
    ds = _mm(dout, w_out, "nt", F32, tag + "_ds", alpha=0.5)
    dab = _swiglu_bwd(ab, ds, tag + "_dact")
    dw_in_t = _mm(dab, n, "tn", BF16, tag + "_dw_in")
    dn = _mm(dab, w_in_t, "nn", F32, tag + "_dn")
    dh, dg = _rms_bwd(h, g, dn, dout, tag + "_dnorm")
    return dh, dg, dw_in_t, dw_out


def _chunk_sum_matrix(n, chunk, transpose=False):
    row = lax.broadcasted_iota(jnp.int32, (n, n), 0)
    col = lax.broadcasted_iota(jnp.int32, (n, n), 1)
    if transpose:
        row, col = col, row
    return jnp.where(jnp.logical_and(col <= row, row // chunk == col // chunk), 1.0, 0.0).astype(F32)


def _hgrn_gates(hq, hf, lbl):
    lb = jax.nn.sigmoid(lbl[0:1, :] - lbl[1:2, :])
    sg = jax.nn.sigmoid(hf)
    f = lb + (1.0 - lb) * sg
    q = _silu(hq) * HEAD ** -0.5
    k = (1.0 - lb) * (1.0 - sg)
    return q, k, jnp.log(f)


def _hgrn_prep_fwd(proj, lbl):
    t = proj.shape[0]
    tt, ft = _tile(t, (256, 128)), 512

    def fn(hq, hf, lbl):
        q, k, log_f = _hgrn_gates(hq, hf, lbl)
        return q, k, _hdot_raw(_chunk_sum_matrix(tt, HG_CHUNK), log_f)

    o = ((t, D_MODEL), F32, *_rows(ft, tt), None)
    return _tmap(fn, (D_MODEL // ft, t // tt),
                 [(proj, *_rows(ft, tt, COL_HQ * HEAD // ft)), (proj, *_rows(ft, tt, COL_HF * HEAD // ft)), (lbl, (2, ft), lambda j, i: (0, j))],
                 [o, o, o], "hgrn_prep")


def _hgrn_prep_bwd(proj, lbl, dq, dk, db):
    t = proj.shape[0]
    tt, ft = _tile(t, (256, 128)), 512

    def fn(hq, hf, lbl, dq, dk, db):
        dlog_f = _hdot_raw(_chunk_sum_matrix(tt, HG_CHUNK, transpose=True), db)
        _, vjp = jax.vjp(_hgrn_gates, hq, hf, lbl)
        return vjp((dq, dk, dlog_f))

    o = ((t, D_MODEL), BF16, *_rows(ft, tt), None)
    r = _rows(ft, tt)
    return _tmap(fn, (D_MODEL // ft, t // tt),
                 [(proj, *_rows(ft, tt, COL_HQ * HEAD // ft)), (proj, *_rows(ft, tt, COL_HF * HEAD // ft)), (lbl, (2, ft), lambda j, i: (0, j)),
                  (dq, *r), (dk, *r), (db, *r)],
                 [o, o, ((2, D_MODEL), F32, (2, ft), lambda j, i: (0, j), "inner")], "hgrn_prep_bwd")


def _hgrn_chunk(q, k, v, b, st):
    n = q.shape[0]
    srow = lax.broadcasted_iota(jnp.int32, (n, HEAD), 0)
    inter = _bdot(q * jnp.exp(b), st, "nt")
    rows = []
    for t in range(n):
        e = jnp.where(srow <= t, jnp.exp(jnp.minimum(b[t:t + 1, :] - b, 0.0)), 0.0)
        a = jnp.sum(q[t:t + 1, :] * k * e, axis=1, keepdims=True)
        rows.append(jnp.sum(a * v, axis=0, keepdims=True))
    o = inter + jnp.concatenate(rows, axis=0)
    bend = b[n - 1:n, :]
    st_new = st * jnp.exp(bend) + _bdot(v, k * jnp.exp(bend - b), "tn")
    return o, st_new


HG_GROUP = 4
HG_PER = GDN_CHUNK // HG_CHUNK


def _hgrn_rec_fwd(q, k, proj, b):
    t = q.shape[0]
    nc = t // GDN_CHUNK
    blk = (GDN_CHUNK, HG_GROUP * HEAD)
    im = lambda h, c: (c, h)

    def body(q_ref, k_ref, v_ref, b_ref, o_ref, hs_ref, st_ref):
        @pl.when(pl.program_id(1) == 0)
        def _():
            st_ref[...] = jnp.zeros_like(st_ref)

        for j in range(HG_PER):
            sl = pl.ds(HG_CHUNK * j, HG_CHUNK)
            for g in range(HG_GROUP):
                ln = _head_lanes(g)
                st = st_ref[g]
                hs_ref[g, j] = st
                o, st_new = _hgrn_chunk(q_ref[sl, ln], k_ref[sl, ln], v_ref[sl, ln], b_ref[sl, ln], st)
                o_ref[sl, ln] = o
                st_ref[g] = st_new

    return pl.pallas_call(
        body, grid=(HG_HEADS // HG_GROUP, nc),
        in_specs=[pl.BlockSpec(blk, im), pl.BlockSpec(blk, im), pl.BlockSpec(blk, lambda h, c: (c, COL_HI // HG_GROUP + h)), pl.BlockSpec(blk, im)],
        out_specs=[pl.BlockSpec(blk, im), pl.BlockSpec((HG_GROUP, HG_PER, HEAD, HEAD), lambda h, c: (h, c, 0, 0))],
        out_shape=[SDS((t, D_MODEL), F32), SDS((HG_HEADS, nc * HG_PER, HEAD, HEAD), F32)],
        scratch_shapes=[pltpu.VMEM((HG_GROUP, HEAD, HEAD), F32)], name="hgrn_rec", compiler_params=_params(2),
    )(q, k, proj, b)


def _hgrn_rec_bwd(q, k, proj, b, hs, do):
    t = q.shape[0]
    nc = t // GDN_CHUNK
    blk = (GDN_CHUNK, HG_GROUP * HEAD)
    im = lambda h, c: (nc - 1 - c, h)

    def body(q_ref, k_ref, v_ref, b_ref, hs_ref, do_ref, dq_ref, dk_ref, dv_ref, db_ref, dst_ref):
        @pl.when(pl.program_id(1) == 0)
        def _():
            dst_ref[...] = jnp.zeros_like(dst_ref)

        for j in reversed(range(HG_PER)):
            sl = pl.ds(HG_CHUNK * j, HG_CHUNK)
            for g in range(HG_GROUP):
                ln = _head_lanes(g)
                _, vjp = jax.vjp(_hgrn_chunk, q_ref[sl, ln], k_ref[sl, ln], v_ref[sl, ln], b_ref[sl, ln], hs_ref[g, j])
                dq, dk, dv, db, dst = vjp((do_ref[sl, ln], dst_ref[g]))
                dq_ref[sl, ln] = dq
                dk_ref[sl, ln] = dk
                dv_ref[sl, ln] = dv.astype(dv_ref.dtype)
                db_ref[sl, ln] = db
                dst_ref[g] = dst

    spec = pl.BlockSpec(blk, im)
    return pl.pallas_call(
        body, grid=(HG_HEADS // HG_GROUP, nc),
        in_specs=[spec, spec, pl.BlockSpec(blk, lambda h, c: (nc - 1 - c, COL_HI // HG_GROUP + h)), spec,
                  pl.BlockSpec((HG_GROUP, HG_PER, HEAD, HEAD), lambda h, c: (h, nc - 1 - c, 0, 0)), spec],
        out_specs=[spec, spec, spec, spec],
        out_shape=[SDS((t, D_MODEL), F32), SDS((t, D_MODEL), F32), SDS((t, D_MODEL), BF16), SDS((t, D_MODEL), F32)],
        scratch_shapes=[pltpu.VMEM((HG_GROUP, HEAD, HEAD), F32)], name="hgrn_rec_bwd", compiler_params=_params(2),
    )(q, k, proj, b, hs, do)


def _shift_down(x, d):
    if d == 0:
        return x
    row = lax.broadcasted_iota(jnp.int32, x.shape, 0)
    return jnp.where(row >= d, pltpu.roll(x, d, 0), 0.0)


def _shift_up(x, d):
    if d == 0:
        return x
    n = x.shape[0]
    row = lax.broadcasted_iota(jnp.int32, x.shape, 0)
    return jnp.where(row < n - d, pltpu.roll(x, n - d, 0), 0.0)


def _conv_fwd(proj, conv_w):
    t = proj.shape[0]
    width = 2 * D_MODEL + 2 * D_MODEL

    def body(x_ref, w_ref, c_ref):
        x, w = x_ref[...], w_ref[...]
        y = w[CONV_K - 1:CONV_K, :] * x
        for j in range(CONV_K - 1):
            y = y + w[j:j + 1, :] * _shift_down(x, CONV_K - 1 - j)
        c_ref[...] = _silu(y)

    return pl.pallas_call(
        body, grid=(width // HEAD,),
        in_specs=[pl.BlockSpec((t, HEAD), lambda j: (0, COL_GQ + j)), pl.BlockSpec((CONV_K, HEAD), lambda j: (0, j))],
        out_specs=pl.BlockSpec((t, HEAD), lambda j: (0, j)), out_shape=SDS((t, width), F32),
        name="gdn_conv", compiler_params=_params(1),
    )(proj, conv_w)


def _conv_bwd(proj, conv_w, dc):
    t = proj.shape[0]
    width = dc.shape[1]

    def body(x_ref, w_ref, dc_ref, dx_ref, dw_ref):
        x, w = x_ref[...], w_ref[...]
        xs = [_shift_down(x, CONV_K - 1 - j) for j in range(CONV_K)]
        y = w[0:1, :] * xs[0]
        for j in range(1, CONV_K):
            y = y + w[j:j + 1, :] * xs[j]
        sg = jax.nn.sigmoid(y)
        dy = dc_ref[...] * (sg * (1.0 + y * (1.0 - sg)))
        dx = w[CONV_K - 1:CONV_K, :] * dy
        for j in range(CONV_K - 1):
            dx = dx + w[j:j + 1, :] * _shift_up(dy, CONV_K - 1 - j)
        dx_ref[...] = dx.astype(dx_ref.dtype)
        dw_ref[...] = jnp.concatenate([jnp.sum(xs[j] * dy, axis=0, keepdims=True) for j in range(CONV_K)], axis=0)

    return pl.pallas_call(
        body, grid=(width // HEAD,),
        in_specs=[pl.BlockSpec((t, HEAD), lambda j: (0, COL_GQ + j)), pl.BlockSpec((CONV_K, HEAD), lambda j: (0, j)),
                  pl.BlockSpec((t, HEAD), lambda j: (0, j))],
        out_specs=[pl.BlockSpec((t, HEAD), lambda j: (0, j)), pl.BlockSpec((CONV_K, HEAD), lambda j: (0, j))],
        out_shape=[SDS((t, width), BF16), SDS((CONV_K, width), F32)],
        name="gdn_conv_bwd", compiler_params=_params(1),
    )(proj, conv_w, dc)


def _l2norm(x, scale):
    return x * lax.rsqrt(jnp.sum(x * x, axis=-1, keepdims=True) + EPS) * scale


def _l2_fwd(c, col_off, scale, name):
    t = c.shape[0]
    tt = _tile(t, (256, 128))
    return _tmap(lambda x: _l2norm(x, scale), (GDN_QK_HEADS, t // tt), [(c, *_rows(HEAD, tt, col_off))],
                 [((t, D_MODEL), F32, *_rows(HEAD, tt), None)], name)[0]


def _l2_bwd(c, col_off, scale, d_rep, name):
    t = c.shape[0]
    tt = _tile(t, (256, 128))

    def fn(x, d2):
        _, vjp = jax.vjp(lambda x: _l2norm(x, scale), x)
        return vjp(d2[:, :HEAD] + d2[:, HEAD:])[0]

    return _tmap(fn, (GDN_QK_HEADS, t // tt), [(c, *_rows(HEAD, tt, col_off)), (d_rep, *_rows(2 * HEAD, tt))],
                 [((t, D_MODEL), F32, *_rows(HEAD, tt), None)], name)[0]


def _gdn_gates(x, alog, dtb):
    return -jnp.exp(alog) * _softplus(x + dtb), jax.nn.sigmoid(x)


def _gates_fwd(pab, alog, dtb):
    t = pab.shape[0]
    tt = _tile(t, (256, 128))

    def fn(x, alog, dtb):
        g, beta = _gdn_gates(x, alog, dtb)
        lane = lax.broadcasted_iota(jnp.int32, g.shape, 1)
        return jnp.where(lane < GDN_V_HEADS, _hdot_raw(_chunk_sum_matrix(tt, GDN_CHUNK), g), beta)

    p = (alog, (1, HEAD), lambda j, i: (0, 0)), (dtb, (1, HEAD), lambda j, i: (0, 0))
    return _tmap(fn, (1, t // tt), [(pab, *_rows(HEAD, tt)), *p], [((t, HEAD), F32, *_rows(HEAD, tt), None)], "gdn_gates")[0]


def _gates_bwd(pab, alog, dtb, dout):
    t = pab.shape[0]
    tt = _tile(t, (256, 128))

    def fn(x, alog, dtb, dout):
        lane = lax.broadcasted_iota(jnp.int32, dout.shape, 1)
        dgam = jnp.where(lane < GDN_V_HEADS, dout, 0.0)
        dbeta = jnp.where(jnp.logical_and(lane >= GDN_V_HEADS, lane < 2 * GDN_V_HEADS), dout, 0.0)
        dg = _hdot_raw(_chunk_sum_matrix(tt, GDN_CHUNK, transpose=True), dgam)
        _, vjp = jax.vjp(_gdn_gates, x, alog, dtb)
        return vjp((dg, dbeta))

    p = (alog, (1, HEAD), lambda j, i: (0, 0)), (dtb, (1, HEAD), lambda j, i: (0, 0))
    acc = ((1, HEAD), F32, (1, HEAD), lambda j, i: (0, 0), "inner")
    return _tmap(fn, (1, t // tt), [(pab, *_rows(HEAD, tt)), *p, (dout, *_rows(HEAD, tt))],
                 [((t, HEAD), BF16, *_rows(HEAD, tt), None), acc, acc], "gdn_gates_bwd")


def _split_bf16(x):
    hi = x.astype(BF16)
    return hi, (x - hi.astype(F32)).astype(BF16)


def _dot3(a, b):
    (ah, al), (bh, bl) = a, b
    return _bdot_raw(ah, bh, "nn") + (_bdot_raw(ah, bl, "nn") + _bdot_raw(al, bh, "nn"))


def _each(fn, *lists):
    return tuple(fn(*xs) for xs in zip(*lists))


def _unit_lower_inverses_raw(a):
    n = a[0].shape[0]
    row = lax.broadcasted_iota(jnp.int32, (n, n), 0)
    col = lax.broadcasted_iota(jnp.int32, (n, n), 1)
    eye = jnp.where(row == col, 1.0, 0.0).astype(F32)
    p = _each(lambda a: eye - a, a)
    x = _each(_split_bf16, a)
    m = 2
    while m < 2 * n:
        x = _each(_split_bf16, _each(_dot3, x, x))
        p = _each(lambda p, x: p + _dot3(_split_bf16(p), x), p, x)
        m *= 2
    return p


@jax.custom_vjp
def _unit_lower_inverses(a, known):
    return _unit_lower_inverses_raw(a) if known is None else known


def _uli_fwd(a, known):
    inv = _unit_lower_inverses(a, known)
    return inv, (inv, known)


def _uli_bwd(res, ct):
    inv, known = res
    right = _each(lambda ct, inv: _bdot_raw(ct, inv, "nt"), ct, inv)
    da = _each(lambda inv, r: -_bdot_raw(inv, r, "tn"), inv, right)
    return da, (None if known is None else _each(jnp.zeros_like, known))


_unit_lower_inverses.defvjp(_uli_fwd, _uli_bwd)


def _gdn_chunks(q, k, v, beta, gam, gam_row, s, inv_known=None):
    n = q[0].shape[0]
    row = lax.broadcasted_iota(jnp.int32, (n, n), 0)
    col = lax.broadcasted_iota(jnp.int32, (n, n), 1)
    decay = _each(lambda gam, gam_row: jnp.where(row >= col, jnp.exp(jnp.minimum(gam - gam_row, 0.0)), 0.0), gam, gam_row)
    kb = _each(lambda k, beta: k * beta, k, beta)
    a = _each(lambda kb, k, decay: jnp.where(row > col, _bdot(kb, k, "nt") * decay, 0.0), kb, k, decay)
    inv = _unit_lower_inverses(a, inv_known)
    eg = _each(jnp.exp, gam)
    u = _each(lambda inv, v, beta: _bdot(inv, v * beta, "nn"), inv, v, beta)
    w = _each(lambda inv, kb, eg: _bdot(inv, kb * eg, "nn"), inv, kb, eg)
    qk = _each(lambda q, k, decay: _bdot(q, k, "nt") * decay, q, k, decay)
    v_new = _each(lambda u, w, s: u - _bdot(w, s, "nn"), u, w, s)
    o_state = _each(lambda q, eg, s: _bdot(q * eg, s, "nn"), q, eg, s)
    o = _each(lambda o_state, qk, v_new: o_state + _bdot(qk, v_new, "nn"), o_state, qk, v_new)
    gend = _each(lambda gam: gam[n - 1:n, :], gam)
    s_new = _each(lambda s, k, gam, gend, v_new: s * jnp.exp(gend) + _bdot(k * jnp.exp(gend - gam), v_new, "tn"), s, k, gam, gend, v_new)
    return o, s_new, inv


GDN_GROUP = 16


def _gdn_specs(nc, rev):
    cc = (lambda c: nc - 1 - c) if rev else (lambda c: c)
    grp = GDN_GROUP
    qk = pl.BlockSpec((GDN_CHUNK, grp // 2 * HEAD), lambda h, c: (cc(c), h))
    v = pl.BlockSpec((GDN_CHUNK, grp * HEAD), lambda h, c: (cc(c), 2 * GDN_QK_HEADS // grp + h))
    o = pl.BlockSpec((GDN_CHUNK, grp * HEAD), lambda h, c: (cc(c), h))
    col = pl.BlockSpec((grp, None, GDN_CHUNK, 1), lambda h, c: (h, cc(c), 0, 0))
    rw = pl.BlockSpec((grp, None, 1, GDN_CHUNK), lambda h, c: (h, cc(c), 0, 0))
    st = pl.BlockSpec((grp, None, HEAD, HEAD), lambda h, c: (h, cc(c), 0, 0))
    inv = pl.BlockSpec((grp, None, GDN_CHUNK, GDN_CHUNK), lambda h, c: (h, cc(c), 0, 0))
    return qk, v, o, col, rw, st, inv


def _head_lanes(g, per=1):
    return pl.ds((g // per) * HEAD, HEAD)


def _gdn_rec_fwd(qg, kg, c, beta_col, gam_col, gam_row):
    t = qg.shape[0]
    nc = t // GDN_CHUNK
    qk, v, o, col, rw, st, inv = _gdn_specs(nc, False)

    def body(q_ref, k_ref, v_ref, be_ref, g_ref, gr_ref, o_ref, ss_ref, inv_ref, s_ref):
        @pl.when(pl.program_id(1) == 0)
        def _():
            s_ref[...] = jnp.zeros_like(s_ref)

        heads = range(GDN_GROUP)
        s = tuple(s_ref[g] for g in heads)
        out, s_new, inv_c = _gdn_chunks(
            tuple(q_ref[:, _head_lanes(g, 2)] for g in heads), tuple(k_ref[:, _head_lanes(g, 2)] for g in heads),
            tuple(v_ref[:, _head_lanes(g)] for g in heads), tuple(be_ref[g] for g in heads), tuple(g_ref[g] for g in heads),
            tuple(gr_ref[g] for g in heads), s)
        for g in heads:
            ss_ref[g] = s[g]
            o_ref[:, _head_lanes(g)] = out[g]
            inv_ref[g] = inv_c[g]
            s_ref[g] = s_new[g]

    return pl.pallas_call(
        body, grid=(GDN_V_HEADS // GDN_GROUP, nc), in_specs=[qk, qk, v, col, col, rw], out_specs=[o, st, inv],
        out_shape=[SDS((t, 2 * D_MODEL), F32), SDS((GDN_V_HEADS, nc, HEAD, HEAD), F32), SDS((GDN_V_HEADS, nc, GDN_CHUNK, GDN_CHUNK), F32)],
        scratch_shapes=[pltpu.VMEM((GDN_GROUP, HEAD, HEAD), F32)], name="gdn_rec", compiler_params=_params(2),
    )(qg, kg, c, beta_col, gam_col, gam_row)


def _gdn_rec_bwd(qg, kg, c, beta_col, gam_col, gam_row, ss, invs, do):
    t = qg.shape[0]
    nc = t // GDN_CHUNK
    qk, v, o, col, rw, st, inv = _gdn_specs(nc, True)

    def body(q_ref, k_ref, v_ref, be_ref, g_ref, gr_ref, ss_ref, inv_ref, do_ref,
             dq_ref, dk_ref, dv_ref, dbe_ref, dg_ref, dgr_ref, ds_ref):
        @pl.when(pl.program_id(1) == 0)
        def _():
            ds_ref[...] = jnp.zeros_like(ds_ref)

        heads = range(GDN_GROUP)
        _, vjp = jax.vjp(
            _gdn_chunks,
            tuple(q_ref[:, _head_lanes(g, 2)] for g in heads), tuple(k_ref[:, _head_lanes(g, 2)] for g in heads),
            tuple(v_ref[:, _head_lanes(g)] for g in heads), tuple(be_ref[g] for g in heads), tuple(g_ref[g] for g in heads),
            tuple(gr_ref[g] for g in heads), tuple(ss_ref[g] for g in heads), tuple(inv_ref[g] for g in heads))
        no_inv_ct = tuple(jnp.zeros((GDN_CHUNK, GDN_CHUNK), F32) for g in heads)
        dq, dk, dv, dbe, dg, dgr, ds, _ = vjp((tuple(do_ref[:, _head_lanes(g)] for g in heads), tuple(ds_ref[g] for g in heads), no_inv_ct))
        for g in heads:
            dq_ref[:, _head_lanes(g)] = dq[g]
            dk_ref[:, _head_lanes(g)] = dk[g]
            dv_ref[:, _head_lanes(g)] = dv[g]
            dbe_ref[g] = dbe[g]
            dg_ref[g] = dg[g]
            dgr_ref[g] = dgr[g]
            ds_ref[g] = ds[g]

    wide = SDS((t, 2 * D_MODEL), F32)
    colshape = SDS((GDN_V_HEADS, nc, GDN_CHUNK, 1), F32)
    return pl.pallas_call(
        body, grid=(GDN_V_HEADS // GDN_GROUP, nc), in_specs=[qk, qk, v, col, col, rw, st, inv, o], out_specs=[o, o, o, col, col, rw],
        out_shape=[wide, wide, wide, colshape, colshape, SDS((GDN_V_HEADS, nc, 1, GDN_CHUNK), F32)],
        scratch_shapes=[pltpu.VMEM((GDN_GROUP, HEAD, HEAD), F32)], name="gdn_rec_bwd", compiler_params=_params(2),
    )(qg, kg, c, beta_col, gam_col, gam_row, ss, invs, do)


def _gated_norm(o, gate, w):
    return _rms(o, w) * _silu(gate)


def _post_fwd(o, proj, col_off, w, name):
    t, width = o.shape
    tt = _tile(t, (256, 128))
    return _tmap(_gated_norm, (width // HEAD, t // tt),
                 [(o, *_rows(HEAD, tt)), (proj, *_rows(HEAD, tt, col_off)), (w, (1, HEAD), lambda j, i: (0, 0))],
                 [((t, width), BF16, *_rows(HEAD, tt), None)], name)[0]


def _post_bwd(o, proj, col_off, w, dout, name):
    t, width = o.shape
    tt = _tile(t, (256, 128))

    def fn(o, gate, w, dout):
        _, vjp = jax.vjp(_gated_norm, o, gate, w)
        return vjp(dout)

    return _tmap(fn, (width // HEAD, t // tt),
                 [(o, *_rows(HEAD, tt)), (proj, *_rows(HEAD, tt, col_off)), (w, (1, HEAD), lambda j, i: (0, 0)), (dout, *_rows(HEAD, tt))],
                 [((t, width), F32, *_rows(HEAD, tt), None), ((t, width), BF16, *_rows(HEAD, tt), None),
                  ((1, HEAD), F32, (1, HEAD), lambda j, i: (0, 0), "all")], name)


def _merge(gate_h, gate_g, yh, yg):
    return jax.nn.sigmoid(gate_h) * yh + jax.nn.sigmoid(gate_g) * yg


def _merge_fwd(proj, yh, yg):
    t = yh.shape[0]
    tt, ft = _tile(t, (256, 128)), 512
    r = _rows(ft, tt)
    return _tmap(_merge, (D_MODEL // ft, t // tt),
                 [(proj, *_rows(ft, tt, COL_GATE_H * HEAD // ft)), (proj, *_rows(ft, tt, COL_GATE_G * HEAD // ft)), (yh, *r), (yg, *r)],
                 [((t, D_MODEL), BF16, *r, None)], "merge")[0]


def _merge_bwd(proj, yh, yg, dy):
    t = yh.shape[0]
    tt, ft = _tile(t, (256, 128)), 512
    r = _rows(ft, tt)

    def fn(gate_h, gate_g, yh, yg, dy):
        _, vjp = jax.vjp(_merge, gate_h, gate_g, yh, yg)
        return vjp(dy)

    o = ((t, D_MODEL), BF16, *r, None)
    return _tmap(fn, (D_MODEL // ft, t // tt),
                 [(proj, *_rows(ft, tt, COL_GATE_H * HEAD // ft)), (proj, *_rows(ft, tt, COL_GATE_G * HEAD // ft)), (yh, *r), (yg, *r), (dy, *r)],
                 [o, o, o, o], "merge_bwd")


def _loss_head(h, target, g):
    t, d = h.shape
    tt = _tile(t, (256, 128))

    def fn(h, target, g):
        def f(h, g):
            err = _rms(h, g) - target
            return 0.5 * jnp.sum(jnp.mean(err * err, axis=-1))

        loss, (dh, dg) = jax.value_and_grad(f, (0, 1))(h, g)
        return dh, dg, jnp.full((1, HEAD), loss, F32)

    return _tmap(fn, (1, t // tt), [(h, *_rows(d, tt)), (target, *_rows(d, tt)), (g, (1, d), lambda j, i: (0, 0))],
                 [((t, d), F32, *_rows(d, tt), None), ((1, d), F32, (1, d), lambda j, i: (0, 0), "inner"),
                  ((1, HEAD), F32, (1, HEAD), lambda j, i: (0, 0), "inner")], "loss_head")


def _heads_to_cols(a):
    t = a.shape[0]
    return a.T.reshape(GDN_V_HEADS, t // GDN_CHUNK, GDN_CHUNK, 1)


def _mixer_fwd(h, p, w):
    t = h.shape[0]
    nc = t // GDN_CHUNK
    u = _rms_fwd(h, p["mix_norm"], "mix_norm")
    proj = _mm(u, w["w_in_main_t"], "nt", F32, "mix_in")
    pab = _mm(u, w["w_in_ab_t"], "nt", F32, "mix_in_ab")
    qh, kh, bh = _hgrn_prep_fwd(proj, p["lbl"])
    oh, hs = _hgrn_rec_fwd(qh, kh, proj, bh)
    c = _conv_fwd(proj, w["conv_w"])
    qg = _l2_fwd(c, 0, HEAD ** -0.5, "gdn_qnorm")
    kg = _l2_fwd(c, GDN_QK_HEADS, 1.0, "gdn_knorm")
    gates = _gates_fwd(pab, p["alog"], p["dtb"])
    gam = gates[:, :GDN_V_HEADS]
    beta_col = _heads_to_cols(gates[:, GDN_V_HEADS:2 * GDN_V_HEADS])
    gam_col = _heads_to_cols(gam)
    gam_row = gam.T.reshape(GDN_V_HEADS, nc, 1, GDN_CHUNK)
    og, ss, invs = _gdn_rec_fwd(qg, kg, c, beta_col, gam_col, gam_row)
    ohn = _post_fwd(oh, proj, COL_HG, p["hgrn_out_norm"], "hgrn_out")
    ogn = _post_fwd(og, proj, COL_GZ, p["gdn_out_norm"], "gdn_out")
    yh = _mm(ohn, w["w_branch_hgrn"], "nn", F32, "branch_hgrn")
    yg = _mm(ogn, w["w_branch_gdn"], "nn", F32, "branch_gdn")
    y = _merge_fwd(proj, yh, yg)
    out = _mm(y, w["w_out"], "nn", F32, "mix_out", res=h)
    saved = (u, proj, pab, qh, kh, bh, oh, hs, c, qg, kg, beta_col, gam_col, gam_row, og, ss, invs, ohn, ogn, yh, yg, y)
    return out, saved


def _mixer_bwd(h, p, w, saved, dout):
    (u, proj, pab, qh, kh, bh, oh, hs, c, qg, kg, beta_col, gam_col, gam_row, og, ss, invs, ohn, ogn, yh, yg, y) = saved
    t = h.shape[0]
    grads = {}
    grads["w_out"] = _mm(y, dout, "tn", BF16, "mix_out_dw")
    dy = _mm(dout, w["w_out"], "nt", F32, "mix_out_dx")
    dgate_h, dgate_g, dyh, dyg = _merge_bwd(proj, yh, yg, dy)
    grads["w_branch_hgrn"] = _mm(ohn, dyh, "tn", BF16, "branch_hgrn_dw")
    grads["w_branch_gdn"] = _mm(ogn, dyg, "tn", BF16, "branch_gdn_dw")
    dohn = _mm(dyh, w["w_branch_hgrn"], "nt", F32, "branch_hgrn_dx")
    dogn = _mm(dyg, w["w_branch_gdn"], "nt", F32, "branch_gdn_dx")
    doh, dhg, grads["hgrn_out_norm"] = _post_bwd(oh, proj, COL_HG, p["hgrn_out_norm"], dohn, "hgrn_out_bwd")
    dog, dgz, grads["gdn_out_norm"] = _post_bwd(og, proj, COL_GZ, p["gdn_out_norm"], dogn, "gdn_out_bwd")
    dqh, dkh, dhi, dbh = _hgrn_rec_bwd(qh, kh, proj, bh, hs, doh)
    dhq, dhf, grads["lbl"] = _hgrn_prep_bwd(proj, p["lbl"], dqh, dkh, dbh)
    dqv, dkv, dcv, dbeta_col, dgam_col, dgam_row = _gdn_rec_bwd(qg, kg, c, beta_col, gam_col, gam_row, ss, invs, dog)
    dcq = _l2_bwd(c, 0, HEAD ** -0.5, dqv, "gdn_qnorm_bwd")
    dck = _l2_bwd(c, GDN_QK_HEADS, 1.0, dkv, "gdn_knorm_bwd")
    dxin, grads["conv_w"] = _conv_bwd(proj, w["conv_w"], jnp.concatenate([dcq, dck, dcv], axis=1))
    dgam = (dgam_col.reshape(GDN_V_HEADS, t) + dgam_row.reshape(GDN_V_HEADS, t)).T
    dbeta = dbeta_col.reshape(GDN_V_HEADS, t).T
    dgates = jnp.concatenate([dgam, dbeta, jnp.zeros((t, HEAD - 2 * GDN_V_HEADS), F32)], axis=1)
    dpab, grads["alog"], grads["dtb"] = _gates_bwd(pab, p["alog"], p["dtb"], dgates)
    dproj = jnp.concatenate([dhq, dhf, dhi, dhg, dxin, dgz, dgate_h, dgate_g], axis=1)
    grads["w_in_main_t"] = _mm(dproj, u, "tn", BF16, "mix_in_dw")
    grads["w_in_ab_t"] = _mm(dpab, u, "tn", BF16, "mix_in_ab_dw")
    du = _mm(dproj, w["w_in_main_t"], "nn", F32, "mix_in_dx")
    du = _mm(dpab, w["w_in_ab_t"], "nn", F32, "mix_in_ab_dx", res=du)
    dh, grads["mix_norm"] = _rms_bwd(h, p["mix_norm"], du, dout, "mix_norm_bwd")
    return dh, grads


def _local_step(x, target, p, w):
    h1, s1 = _ffn_fwd(x, p["ffn1_norm"], w["ffn1_w_in"], w["ffn1_w_out"], "ffn1")
    h2, sm = _mixer_fwd(h1, p, w)
    h3, s2 = _ffn_fwd(h2, p["ffn2_norm"], w["ffn2_w_in"], w["ffn2_w_out"], "ffn2")
    dh3, dfinal, loss = _loss_head(h3, target, p["final_norm"])
    g = {"final_norm": dfinal}
    dh2, g["ffn2_norm"], g["ffn2_w_in"], g["ffn2_w_out"] = _ffn_bwd(h2, p["ffn2_norm"], w["ffn2_w_in"], w["ffn2_w_out"], s2, dh3, "ffn2")
    dh1, gm = _mixer_bwd(h1, p, w, sm, dh2)
    g.update(gm)
    dx, g["ffn1_norm"], g["ffn1_w_in"], g["ffn1_w_out"] = _ffn_bwd(x, p["ffn1_norm"], w["ffn1_w_in"], w["ffn1_w_out"], s1, dh1, "ffn1")
    return loss, dx, g


def _exchange(items, name):
    n = len(items)
    out_shape = [SDS((N_DEV,) + a.shape if mode == "gather" else a.shape, a.dtype) for a, mode in items]

    def body(*refs):
        in_refs, out_refs = refs[:n], refs[n:2 * n]
        send_sems, recv_sems, local_sems = refs[2 * n:]
        x, y, c = lax.axis_index("x"), lax.axis_index("y"), lax.axis_index("c")
        me = 4 * x + 2 * y + c
        local = []
        for i, (_, mode) in enumerate(items):
            src = in_refs[i] if mode == "gather" else in_refs[i].at[me]
            cp = pltpu.make_async_copy(src, out_refs[i].at[me], local_sems.at[i])
            cp.start()
            local.append(cp)
        remote = []
        for rel in range(1, N_DEV):
            px = 1 - x if rel & 4 else x
            py = 1 - y if rel & 2 else y
            pc = 1 - c if rel & 1 else c
            peer = 4 * px + 2 * py + pc
            for i, (_, mode) in enumerate(items):
                src = in_refs[i] if mode == "gather" else in_refs[i].at[peer]
                cp = pltpu.make_async_remote_copy(src_ref=src, dst_ref=out_refs[i].at[me], send_sem=send_sems.at[i, rel - 1],
                                                  recv_sem=recv_sems.at[i, rel - 1], device_id=(px, py, pc), device_id_type=MESH_IDS)
                cp.start()
                remote.append(cp)
        for cp in remote:
            cp.wait()
        for cp in local:
            cp.wait()

    anyspace = pl.BlockSpec(memory_space=pl.ANY)
    return pl.pallas_call(
        body, in_specs=[anyspace] * n, out_specs=[anyspace] * n, out_shape=out_shape,
        scratch_shapes=[pltpu.SemaphoreType.DMA((n, N_DEV - 1)), pltpu.SemaphoreType.DMA((n, N_DEV - 1)), pltpu.SemaphoreType.DMA((n,))],
        name=name, compiler_params=pltpu.CompilerParams(has_side_effects=True),
    )(*[a for a, _ in items])


def _adam(parts, w, m, v, name):
    n_parts, r, c = parts.shape
    tc = c if c <= 512 else (256 if r > 1024 else 512)

    def body(p_ref, w_ref, m_ref, v_ref, g_ref, d_ref, mo_ref, vo_ref):
        g = p_ref[0].astype(F32)
        for i in range(1, n_parts):
            g = g + p_ref[i].astype(F32)
        m_new = ADAM_B1 * m_ref[...] + (1.0 - ADAM_B1) * g
        v_new = ADAM_B2 * v_ref[...] + (1.0 - ADAM_B2) * (g * g)
        m_hat = m_new / (1.0 - ADAM_B1 ** ADAM_STEP)
        v_hat = v_new / (1.0 - ADAM_B2 ** ADAM_STEP)
        g_ref[...] = g
        d_ref[...] = -ADAM_LR * (m_hat / (jnp.sqrt(v_hat) + ADAM_EPS) + ADAM_WD * w_ref[...])
        mo_ref[...] = m_new
        vo_ref[...] = v_new

    spec = pl.BlockSpec((r, tc), lambda j: (0, j))
    return pl.pallas_call(
        body, grid=(c // tc,), in_specs=[pl.BlockSpec((n_parts, r, tc), lambda j: (0, 0, j)), spec, spec, spec],
        out_specs=[spec] * 4, out_shape=[SDS((r, c), F32)] * 4, name=name, compiler_params=_params(1),
    )(parts, w, m, v)


BIG = ("ffn1_w_in", "ffn1_w_out", "w_in", "w_branch_hgrn", "w_branch_gdn", "w_out", "ffn2_w_in", "ffn2_w_out")


TRANSPOSED = ("ffn1_w_in", "w_in", "ffn2_w_in")


def _shard_rows(name, shard):
    return shard.T if name in TRANSPOSED else shard


SCALAR_ROWS = 8192
N_SCALAR = 2 * GDN_V_HEADS


def _w_in_split(w_in_t):
    main = jnp.concatenate([w_in_t[:SCALAR_ROWS], w_in_t[SCALAR_ROWS + N_SCALAR:]], axis=0)
    ab = jnp.pad(w_in_t[SCALAR_ROWS:SCALAR_ROWS + N_SCALAR], ((0, HEAD - N_SCALAR), (0, 0)))
    return main, ab


def _w_in_join(dmain_t, dab_t):
    return jnp.concatenate([dmain_t[:SCALAR_ROWS], dab_t[:N_SCALAR], dmain_t[SCALAR_ROWS:]], axis=0)


def _pad_lanes(a, width=HEAD):
    return jnp.pad(a, ((0, 0), (0, width - a.shape[1])))


SMALL_ROWS = 24


def _pack_small(g, loss):
    row6 = jnp.concatenate([g["hgrn_out_norm"], g["gdn_out_norm"], g["alog"], g["dtb"], loss,
                            jnp.zeros((1, D_MODEL - 5 * HEAD), F32)], axis=1)
    return jnp.concatenate([g["ffn1_norm"], g["mix_norm"], g["lbl"], g["ffn2_norm"], g["final_norm"], row6,
                            jnp.zeros((1, D_MODEL), F32), g["conv_w"].reshape(4 * CONV_K, D_MODEL)], axis=0)


def _pack_small_state(a):
    row6 = jnp.concatenate([a["hgrn_out_norm"], a["gdn_out_norm"], _pad_lanes(a["gdn_a_log"]), _pad_lanes(a["gdn_dt_bias"]),
                            jnp.zeros((1, D_MODEL - 4 * HEAD), F32)], axis=1)
    return jnp.concatenate([a["ffn1_norm"], a["mix_norm"], a["hgrn_lb_logits"], a["ffn2_norm"], a["final_norm"].reshape(1, D_MODEL),
                            row6, jnp.zeros((1, D_MODEL), F32)], axis=0)


def _unpack_small(a):
    return {"ffn1_norm": a[0:1], "mix_norm": a[1:2], "hgrn_lb_logits": a[2:4], "ffn2_norm": a[4:5], "final_norm": a[5],
            "hgrn_out_norm": a[6:7, :HEAD], "gdn_out_norm": a[6:7, HEAD:2 * HEAD],
            "gdn_a_log": a[6:7, 2 * HEAD:2 * HEAD + GDN_V_HEADS], "gdn_dt_bias": a[6:7, 3 * HEAD:3 * HEAD + GDN_V_HEADS]}


NAMES = ("ffn1_norm", "ffn1_w_in", "ffn1_w_out", "mix_norm", "w_in", "hgrn_lb_logits", "hgrn_out_norm", "gdn_conv_w", "gdn_a_log",
         "gdn_dt_bias", "gdn_out_norm", "w_branch_hgrn", "w_branch_gdn", "w_out", "ffn2_norm", "ffn2_w_in", "ffn2_w_out", "final_norm")


def kernel(x, ffn1_norm, ffn1_w_in, ffn1_w_out, mix_norm, w_in, hgrn_lb_logits, hgrn_out_norm, gdn_conv_w, gdn_a_log, gdn_dt_bias, gdn_out_norm, w_branch_hgrn, w_branch_gdn, w_out, ffn2_norm, ffn2_w_in, ffn2_w_out, final_norm, loss_target, m_ffn1_norm, m_ffn1_w_in, m_ffn1_w_out, m_mix_norm, m_w_in, m_hgrn_lb_logits, m_hgrn_out_norm, m_gdn_conv_w, m_gdn_a_log, m_gdn_dt_bias, m_gdn_out_norm, m_w_branch_hgrn, m_w_branch_gdn, m_w_out, m_ffn2_norm, m_ffn2_w_in, m_ffn2_w_out, m_final_norm, v_ffn1_norm, v_ffn1_w_in, v_ffn1_w_out, v_mix_norm, v_w_in, v_hgrn_lb_logits, v_hgrn_out_norm, v_gdn_conv_w, v_gdn_a_log, v_gdn_dt_bias, v_gdn_out_norm, v_w_branch_hgrn, v_w_branch_gdn, v_w_out, v_ffn2_norm, v_ffn2_w_in, v_ffn2_w_out, v_final_norm):
    wts = dict(zip(NAMES, (ffn1_norm, ffn1_w_in, ffn1_w_out, mix_norm, w_in, hgrn_lb_logits, hgrn_out_norm, gdn_conv_w, gdn_a_log,
                           gdn_dt_bias, gdn_out_norm, w_branch_hgrn, w_branch_gdn, w_out, ffn2_norm, ffn2_w_in, ffn2_w_out, final_norm)))
    mom = dict(zip(NAMES, (m_ffn1_norm, m_ffn1_w_in, m_ffn1_w_out, m_mix_norm, m_w_in, m_hgrn_lb_logits, m_hgrn_out_norm, m_gdn_conv_w,
                           m_gdn_a_log, m_gdn_dt_bias, m_gdn_out_norm, m_w_branch_hgrn, m_w_branch_gdn, m_w_out, m_ffn2_norm, m_ffn2_w_in,
                           m_ffn2_w_out, m_final_norm)))
    var = dict(zip(NAMES, (v_ffn1_norm, v_ffn1_w_in, v_ffn1_w_out, v_mix_norm, v_w_in, v_hgrn_lb_logits, v_hgrn_out_norm, v_gdn_conv_w,
                           v_gdn_a_log, v_gdn_dt_bias, v_gdn_out_norm, v_w_branch_hgrn, v_w_branch_gdn, v_w_out, v_ffn2_norm, v_ffn2_w_in,
                           v_ffn2_w_out, v_final_norm)))
    me = 4 * lax.axis_index("x") + 2 * lax.axis_index("y") + lax.axis_index("c")

    conv_shard = wts["gdn_conv_w"][0]
    gathered = _exchange([(_shard_rows(n, wts[n][0]).astype(BF16), "gather") for n in BIG] + [(conv_shard.reshape(2, D_MODEL), "gather")],
                         "gather_weights")
    w = {n: g.reshape(-1, D_MODEL) for n, g in zip(BIG, gathered)}
    w["conv_w"] = gathered[-1].reshape(N_DEV, CONV_K, 4 * D_MODEL // N_DEV).transpose(1, 0, 2).reshape(CONV_K, 4 * D_MODEL)
    w["w_in_main_t"], w["w_in_ab_t"] = _w_in_split(w.pop("w_in"))
    p = {"ffn1_norm": wts["ffn1_norm"], "mix_norm": wts["mix_norm"], "ffn2_norm": wts["ffn2_norm"], "final_norm": wts["final_norm"].reshape(1, D_MODEL),
         "lbl": wts["hgrn_lb_logits"], "hgrn_out_norm": wts["hgrn_out_norm"], "gdn_out_norm": wts["gdn_out_norm"],
         "alog": _pad_lanes(wts["gdn_a_log"]), "dtb": _pad_lanes(wts["gdn_dt_bias"])}

    loss, dx, g = _local_step(x[0], loss_target[0], p, w)

    g["w_in"] = _w_in_join(g.pop("w_in_main_t"), g.pop("w_in_ab_t"))
    landed = _exchange([(g[n].reshape(N_DEV, -1, D_MODEL), "scatter") for n in BIG] + [(_pack_small(g, loss), "gather")], "scatter_grads")
    small_parts = landed[-1]

    big = [{} for _ in range(4)]
    for n, parts in zip(BIG, landed):
        res = _adam(parts, _shard_rows(n, wts[n][0]), _shard_rows(n, mom[n][0]), _shard_rows(n, var[n][0]), "adam_" + n)
        for kind in range(4):
            big[kind][n] = _shard_rows(n, res[kind])
    n_vec = SMALL_ROWS - 4 * CONV_K
    small_raw = _adam(small_parts[:, :n_vec], _pack_small_state(wts), _pack_small_state(mom), _pack_small_state(var), "adam_small")
    small = [_unpack_small(o) for o in small_raw]
    loss_total = small_raw[0][6, 4 * HEAD]
    conv_parts = small_parts[:, n_vec:].reshape(N_DEV, CONV_K, 4 * D_MODEL)
    width = 4 * D_MODEL // N_DEV
    conv_mine = lax.dynamic_slice_in_dim(conv_parts, me * width, width, axis=2)
    conv = _adam(conv_mine, conv_shard, mom["gdn_conv_w"][0], var["gdn_conv_w"][0], "adam_conv")

    outs = []
    for kind in range(4):
        for n in NAMES:
            if n in BIG:
                outs.append(big[kind][n][None])
            elif n == "gdn_conv_w":
                outs.append(conv[kind][None])
            else:
                outs.append(small[kind][n])
    return (loss_total, dx[None], *outs)
```

```python
import functools

import jax
import jax.numpy as jnp
from jax import lax
from jax.experimental import pallas as pl
from jax.experimental.pallas import tpu as pltpu

F32 = jnp.float32
BF16 = jnp.bfloat16
HIGHEST = lax.Precision.HIGHEST
MESH_IDS = pl.DeviceIdType.MESH

D_MODEL = 1024
D_FF = 2816
N_DEV = 8
EPS = 1e-6
HEAD = 128
HG_HEADS = 8
GDN_QK_HEADS = 8
GDN_V_HEADS = 16
GDN_CHUNK = 64
HG_CHUNK = 16
CONV_K = 4
IN_WIDTH = 12320
IN_MAIN = 12288
COL_HQ, COL_HF, COL_HI, COL_HG, COL_GQ, COL_GK, COL_GV, COL_GZ, COL_GATE_H, COL_GATE_G = 0, 8, 16, 24, 32, 40, 48, 64, 80, 88
VMEM_LIMIT = 56 * 1024 * 1024

ADAM_LR, ADAM_B1, ADAM_B2, ADAM_EPS, ADAM_WD, ADAM_STEP = 0.001, 0.9, 0.999, 1e-08, 0.01, 10

SDS = jax.ShapeDtypeStruct


def _params(n_axes):
    return pltpu.CompilerParams(dimension_semantics=("arbitrary",) * n_axes, vmem_limit_bytes=VMEM_LIMIT)


def _tile(n, candidates=(512, 384, 256, 128, 64, 32, 16, 8)):
    for c in candidates:
        if n % c == 0:
            return c
    return n


_DIMS = {"nn": ((1,), (0,)), "nt": ((1,), (1,)), "tn": ((0,), (0,))}


def _bdot_raw(a, b, dims):
    return lax.dot_general(a.astype(BF16), b.astype(BF16), (_DIMS[dims], ((), ())), preferred_element_type=F32)


@functools.partial(jax.custom_vjp, nondiff_argnums=(2,))
def _bdot(a, b, dims):
    return _bdot_raw(a, b, dims)


def _bdot_fwd(a, b, dims):
    return _bdot_raw(a, b, dims), (a, b)


def _bdot_bwd(dims, res, ct):
    a, b = res
    if dims == "nn":
        return _bdot_raw(ct, b, "nt"), _bdot_raw(a, ct, "tn")
    if dims == "nt":
        return _bdot_raw(ct, b, "nn"), _bdot_raw(ct, a, "tn")
    return _bdot_raw(b, ct, "nt"), _bdot_raw(a, ct, "nn")


_bdot.defvjp(_bdot_fwd, _bdot_bwd)


def _hdot_raw(a, b):
    return jnp.dot(a, b, precision=HIGHEST, preferred_element_type=F32)


MM_VMEM_BUDGET = 30 * 1024 * 1024


def _mm_tiles(m, n, k, a_bytes, b_bytes, o_bytes, r_bytes):
    tm = _tile(m, (1024, 512, 256, 128, 64, 32, 16, 8))
    tn = _tile(n, (1408, 1024, 512, 256, 128))
    tk = _tile(k, (2048, 1408, 1024, 512, 256, 128, 64, 32, 16, 8))

    def need(tm, tn, tk):
        return 2 * (tm * tk * a_bytes + tk * tn * b_bytes + tm * tn * (o_bytes + r_bytes)) + tm * tn * 4

    while need(tm, tn, tk) > MM_VMEM_BUDGET:
        if tk > 512 and tk % 256 == 0:
            tk //= 2
        elif tn > 512 and tn % 256 == 0:
            tn //= 2
        elif tm > 256:
            tm //= 2
        else:
            break
    return tm, tn, tk


def _mm(a, b, dims, out_dtype, name, res=None, alpha=1.0):
    if dims == "nn":
        (m, k), (k2, n) = a.shape, b.shape
    elif dims == "nt":
        (m, k), (n, k2) = a.shape, b.shape
    else:
        (k, m), (k2, n) = a.shape, b.shape
    assert k == k2, (a.shape, b.shape, dims)
    has_res = res is not None
    tm, tn, tk = _mm_tiles(m, n, k, a.dtype.itemsize, b.dtype.itemsize, jnp.dtype(out_dtype).itemsize, res.dtype.itemsize if has_res else 0)
    nk = k // tk
    a_spec = pl.BlockSpec((tk, tm), lambda i, j, kk: (kk, i)) if dims == "tn" else pl.BlockSpec((tm, tk), lambda i, j, kk: (i, kk))
    b_spec = pl.BlockSpec((tn, tk), lambda i, j, kk: (j, kk)) if dims == "nt" else pl.BlockSpec((tk, tn), lambda i, j, kk: (kk, j))
    o_spec = pl.BlockSpec((tm, tn), lambda i, j, kk: (i, j))

    def finish(acc, r_ref, o_ref):
        out = acc * alpha if alpha != 1.0 else acc
        if has_res:
            out = r_ref[...].astype(F32) + out
        o_ref[...] = out.astype(o_ref.dtype)

    def body(*refs):
        a_ref, b_ref = refs[:2]
        r_ref = refs[2] if has_res else None
        o_ref = refs[3] if has_res else refs[2]
        p = _bdot_raw(a_ref[...], b_ref[...], dims)
        if nk == 1:
            finish(p, r_ref, o_ref)
            return
        acc_ref = refs[-1]
        kk = pl.program_id(2)

        @pl.when(kk == 0)
        def _():
            acc_ref[...] = p

        @pl.when(kk > 0)
        def _():
            acc_ref[...] += p

        @pl.when(kk == nk - 1)
        def _():
            finish(acc_ref[...], r_ref, o_ref)

    args = (a, b, res) if has_res else (a, b)
    in_specs = [a_spec, b_spec] + ([o_spec] if has_res else [])
    return pl.pallas_call(
        body, grid=(m // tm, n // tn, nk), in_specs=in_specs, out_specs=o_spec, out_shape=SDS((m, n), out_dtype),
        scratch_shapes=[pltpu.VMEM((tm, tn), F32)] if nk > 1 else [], name=name, compiler_params=_params(3),
    )(*args)


def _tmap(fn, grid, ins, outs, name):
    n_in = len(ins)
    n_ax = len(grid)

    def body(*refs):
        vals = fn(*[r[...] for r in refs[:n_in]])
        if not isinstance(vals, (tuple, list)):
            vals = (vals,)
        first_inner = pl.program_id(n_ax - 1) == 0
        first_all = first_inner
        for ax in range(n_ax - 1):
            first_all = jnp.logical_and(first_all, pl.program_id(ax) == 0)

        def put(ref, val, acc):
            val = val.astype(ref.dtype)
            if acc is None:
                ref[...] = val
                return
            first = first_inner if acc == "inner" else first_all

            @pl.when(first)
            def _():
                ref[...] = val

            @pl.when(jnp.logical_not(first))
            def _():
                ref[...] += val

        for ref, val, o in zip(refs[n_in:], vals, outs):
            put(ref, val, o[4])

    return pl.pallas_call(
        body, grid=grid,
        in_specs=[pl.BlockSpec(bs, im) for _, bs, im in ins],
        out_specs=[pl.BlockSpec(o[2], o[3]) for o in outs],
        out_shape=[SDS(o[0], o[1]) for o in outs],
        name=name, compiler_params=_params(n_ax),
    )(*[a for a, _, _ in ins])


def _rows(width, tt, off=0):
    return (tt, width), (lambda j, i: (i, off + j))


def _rms(x, g):
    x = x.astype(F32)
    return x * lax.rsqrt(jnp.mean(x * x, axis=-1, keepdims=True) + EPS) * g


def _silu(x):
    return x * jax.nn.sigmoid(x)


def _softplus(x):
    return jnp.maximum(x, 0.0) + jnp.log1p(jnp.exp(-jnp.abs(x)))


def _rms_fwd(x, g, name):
    t, d = x.shape
    tt = _tile(t, (256, 128))
    return _tmap(_rms, (1, t // tt), [(x, *_rows(d, tt)), (g, (1, d), lambda j, i: (0, 0))],
                 [((t, d), BF16, *_rows(d, tt), None)], name)[0]


def _rms_bwd(x, g, dn, dres, name):
    t, d = x.shape
    tt = _tile(t, (256, 128))

    def fn(x, g, dn, dres):
        _, vjp = jax.vjp(_rms, x, g)
        dx, dg = vjp(dn.astype(F32))
        return dres + dx, dg

    return _tmap(fn, (1, t // tt),
                 [(x, *_rows(d, tt)), (g, (1, d), lambda j, i: (0, 0)), (dn, *_rows(d, tt)), (dres, *_rows(d, tt))],
                 [((t, d), F32, *_rows(d, tt), None), ((1, d), F32, (1, d), lambda j, i: (0, 0), "inner")], name)


def _swiglu(ab):
    return _silu(ab[:, :D_FF]) * ab[:, D_FF:]


def _swiglu_fwd(ab, name):
    t = ab.shape[0]
    tt = _tile(t, (128,))
    return _tmap(_swiglu, (1, t // tt), [(ab, *_rows(2 * D_FF, tt))], [((t, D_FF), BF16, *_rows(D_FF, tt), None)], name)[0]


def _swiglu_bwd(ab, ds, name):
    t = ab.shape[0]
    tt = _tile(t, (128,))

    def fn(ab, ds):
        a, b = ab[:, :D_FF], ab[:, D_FF:]
        _, vjp = jax.vjp(lambda a, b: _silu(a) * b, a, b)
        da, db = vjp(ds.astype(F32))
        return jnp.concatenate([da, db], axis=1)

    return _tmap(fn, (1, t // tt), [(ab, *_rows(2 * D_FF, tt)), (ds, *_rows(D_FF, tt))],
                 [((t, 2 * D_FF), BF16, *_rows(2 * D_FF, tt), None)], name)[0]


def _ffn_fwd(h, g, w_in_t, w_out, tag):
    n = _rms_fwd(h, g, tag + "_norm")
    ab = _mm(n, w_in_t, "nt", F32, tag + "_in")
    s = _swiglu_fwd(ab, tag + "_act")
    out = _mm(s, w_out, "nn", F32, tag + "_out", res=h, alpha=0.5)
    return out, (n, ab, s)


def _ffn_bwd(h, g, w_in_t, w_out, saved, dout, tag, links):
    n, ab, s = saved
    links.send({tag + "_w_out": _mm(s, dout, "tn", BF16, tag + "_dw_out", alpha=0.5)})
    ds = _mm(dout, w_out, "nt", F32, tag + "_ds", alpha=0.5)
    dab = _swiglu_bwd(ab, ds, tag + "_dact")
    sent = links.send({tag + "_w_in": _mm(dab, n, "tn", BF16, tag + "_dw_in")})
    dn = _mm(dab, w_in_t, "nn", F32, tag + "_dn")
    dh, dg = _rms_bwd(h, g + sent, dn, dout, tag + "_dnorm")
    return dh, dg, sent


def _chunk_sum_matrix(n, chunk, transpose=False):
    row = lax.broadcasted_iota(jnp.int32, (n, n), 0)
    col = lax.broadcasted_iota(jnp.int32, (n, n), 1)
    if transpose:
        row, col = col, row
    return jnp.where(jnp.logical_and(col <= row, row // chunk == col // chunk), 1.0, 0.0).astype(F32)


def _hgrn_gates(hq, hf, lbl):
    lb = jax.nn.sigmoid(lbl[0:1, :] - lbl[1:2, :])
    sg = jax.nn.sigmoid(hf)
    f = lb + (1.0 - lb) * sg
    q = _silu(hq) * HEAD ** -0.5
    k = (1.0 - lb) * (1.0 - sg)
    return q, k, jnp.log(f)


def _hgrn_prep_fwd(proj, lbl):
    t = proj.shape[0]
    tt, ft = _tile(t, (256, 128)), 512

    def fn(hq, hf, lbl):
        q, k, log_f = _hgrn_gates(hq, hf, lbl)
        return q, k, _hdot_raw(_chunk_sum_matrix(tt, HG_CHUNK), log_f)

    o = ((t, D_MODEL), F32, *_rows(ft, tt), None)
    return _tmap(fn, (D_MODEL // ft, t // tt),
                 [(proj, *_rows(ft, tt, COL_HQ * HEAD // ft)), (proj, *_rows(ft, tt, COL_HF * HEAD // ft)), (lbl, (2, ft), lambda j, i: (0, j))],
                 [o, o, o], "hgrn_prep")


def _hgrn_prep_bwd(proj, lbl, dq, dk, db):
    t = proj.shape[0]
    tt, ft = _tile(t, (256, 128)), 512

    def fn(hq, hf, lbl, dq, dk, db):
        dlog_f = _hdot_raw(_chunk_sum_matrix(tt, HG_CHUNK, transpose=True), db)
        _, vjp = jax.vjp(_hgrn_gates, hq, hf, lbl)
        return vjp((dq, dk, dlog_f))

    o = ((t, D_MODEL), BF16, *_rows(ft, tt), None)
    r = _rows(ft, tt)
    return _tmap(fn, (D_MODEL // ft, t // tt),
                 [(proj, *_rows(ft, tt, COL_HQ * HEAD // ft)), (proj, *_rows(ft, tt, COL_HF * HEAD // ft)), (lbl, (2, ft), lambda j, i: (0, j)),
                  (dq, *r), (dk, *r), (db, *r)],
                 [o, o, ((2, D_MODEL), F32, (2, ft), lambda j, i: (0, j), "inner")], "hgrn_prep_bwd")


def _hgrn_chunk(q, k, v, b, st):
    n = q.shape[0]
    srow = lax.broadcasted_iota(jnp.int32, (n, HEAD), 0)
    inter = _bdot(q * jnp.exp(b), st, "nt")
    rows = []
    for t in range(n):
        e = jnp.where(srow <= t, jnp.exp(jnp.minimum(b[t:t + 1, :] - b, 0.0)), 0.0)
        a = jnp.sum(q[t:t + 1, :] * k * e, axis=1, keepdims=True)
        rows.append(jnp.sum(a * v, axis=0, keepdims=True))
    o = inter + jnp.concatenate(rows, axis=0)
    bend = b[n - 1:n, :]
    st_new = st * jnp.exp(bend) + _bdot(v, k * jnp.exp(bend - b), "tn")
    return o, st_new


HG_GROUP = 4
HG_PER = GDN_CHUNK // HG_CHUNK


def _hgrn_rec_fwd(q, k, proj, b):
    t = q.shape[0]
    nc = t // GDN_CHUNK
    blk = (GDN_CHUNK, HG_GROUP * HEAD)
    im = lambda h, c: (c, h)

    def body(q_ref, k_ref, v_ref, b_ref, o_ref, hs_ref, st_ref):
        @pl.when(pl.program_id(1) == 0)
        def _():
            st_ref[...] = jnp.zeros_like(st_ref)

        for j in range(HG_PER):
            sl = pl.ds(HG_CHUNK * j, HG_CHUNK)
            for g in range(HG_GROUP):
                ln = _head_lanes(g)
                st = st_ref[g]
                hs_ref[g, j] = st
                o, st_new = _hgrn_chunk(q_ref[sl, ln], k_ref[sl, ln], v_ref[sl, ln], b_ref[sl, ln], st)
                o_ref[sl, ln] = o
                st_ref[g] = st_new

    return pl.pallas_call(
        body, grid=(HG_HEADS // HG_GROUP, nc),
        in_specs=[pl.BlockSpec(blk, im), pl.BlockSpec(blk, im), pl.BlockSpec(blk, lambda h, c: (c, COL_HI // HG_GROUP + h)), pl.BlockSpec(blk, im)],
        out_specs=[pl.BlockSpec(blk, im), pl.BlockSpec((HG_GROUP, HG_PER, HEAD, HEAD), lambda h, c: (h, c, 0, 0))],
        out_shape=[SDS((t, D_MODEL), F32), SDS((HG_HEADS, nc * HG_PER, HEAD, HEAD), F32)],
        scratch_shapes=[pltpu.VMEM((HG_GROUP, HEAD, HEAD), F32)], name="hgrn_rec", compiler_params=_params(2),
    )(q, k, proj, b)


def _hgrn_rec_bwd(q, k, proj, b, hs, do):
    t = q.shape[0]
    nc = t // GDN_CHUNK
    blk = (GDN_CHUNK, HG_GROUP * HEAD)
    im = lambda h, c: (nc - 1 - c, h)

    def body(q_ref, k_ref, v_ref, b_ref, hs_ref, do_ref, dq_ref, dk_ref, dv_ref, db_ref, dst_ref):
        @pl.when(pl.program_id(1) == 0)
        def _():
            dst_ref[...] = jnp.zeros_like(dst_ref)

        for j in reversed(range(HG_PER)):
            sl = pl.ds(HG_CHUNK * j, HG_CHUNK)
            for g in range(HG_GROUP):
                ln = _head_lanes(g)
                _, vjp = jax.vjp(_hgrn_chunk, q_ref[sl, ln], k_ref[sl, ln], v_ref[sl, ln], b_ref[sl, ln], hs_ref[g, j])
                dq, dk, dv, db, dst = vjp((do_ref[sl, ln], dst_ref[g]))
                dq_ref[sl, ln] = dq
                dk_ref[sl, ln] = dk
                dv_ref[sl, ln] = dv.astype(dv_ref.dtype)
                db_ref[sl, ln] = db
                dst_ref[g] = dst

    spec = pl.BlockSpec(blk, im)
    return pl.pallas_call(
        body, grid=(HG_HEADS // HG_GROUP, nc),
        in_specs=[spec, spec, pl.BlockSpec(blk, lambda h, c: (nc - 1 - c, COL_HI // HG_GROUP + h)), spec,
                  pl.BlockSpec((HG_GROUP, HG_PER, HEAD, HEAD), lambda h, c: (h, nc - 1 - c, 0, 0)), spec],
        out_specs=[spec, spec, spec, spec],
        out_shape=[SDS((t, D_MODEL), F32), SDS((t, D_MODEL), F32), SDS((t, D_MODEL), BF16), SDS((t, D_MODEL), F32)],
        scratch_shapes=[pltpu.VMEM((HG_GROUP, HEAD, HEAD), F32)], name="hgrn_rec_bwd", compiler_params=_params(2),
    )(q, k, proj, b, hs, do)


def _shift_down(x, d):
    if d == 0:
        return x
    row = lax.broadcasted_iota(jnp.int32, x.shape, 0)
    return jnp.where(row >= d, pltpu.roll(x, d, 0), 0.0)


def _shift_up(x, d):
    if d == 0:
        return x
    n = x.shape[0]
    row = lax.broadcasted_iota(jnp.int32, x.shape, 0)
    return jnp.where(row < n - d, pltpu.roll(x, n - d, 0), 0.0)


def _conv_fwd(proj, conv_w):
    t = proj.shape[0]
    width = 2 * D_MODEL + 2 * D_MODEL

    def body(x_ref, w_ref, c_ref):
        x, w = x_ref[...], w_ref[...]
        y = w[CONV_K - 1:CONV_K, :] * x
        for j in range(CONV_K - 1):
            y = y + w[j:j + 1, :] * _shift_down(x, CONV_K - 1 - j)
        c_ref[...] = _silu(y)

    return pl.pallas_call(
        body, grid=(width // HEAD,),
        in_specs=[pl.BlockSpec((t, HEAD), lambda j: (0, COL_GQ + j)), pl.BlockSpec((CONV_K, HEAD), lambda j: (0, j))],
        out_specs=pl.BlockSpec((t, HEAD), lambda j: (0, j)), out_shape=SDS((t, width), F32),
        name="gdn_conv", compiler_params=_params(1),
    )(proj, conv_w)


def _conv_bwd(proj, conv_w, dc):
    t = proj.shape[0]
    width = dc.shape[1]

    def body(x_ref, w_ref, dc_ref, dx_ref, dw_ref):
        x, w = x_ref[...], w_ref[...]
        xs = [_shift_down(x, CONV_K - 1 - j) for j in range(CONV_K)]
        y = w[0:1, :] * xs[0]
        for j in range(1, CONV_K):
            y = y + w[j:j + 1, :] * xs[j]
        sg = jax.nn.sigmoid(y)
        dy = dc_ref[...] * (sg * (1.0 + y * (1.0 - sg)))
        dx = w[CONV_K - 1:CONV_K, :] * dy
        for j in range(CONV_K - 1):
            dx = dx + w[j:j + 1, :] * _shift_up(dy, CONV_K - 1 - j)
        dx_ref[...] = dx.astype(dx_ref.dtype)
        dw_ref[...] = jnp.concatenate([jnp.sum(xs[j] * dy, axis=0, keepdims=True) for j in range(CONV_K)], axis=0)

    return pl.pallas_call(
        body, grid=(width // HEAD,),
        in_specs=[pl.BlockSpec((t, HEAD), lambda j: (0, COL_GQ + j)), pl.BlockSpec((CONV_K, HEAD), lambda j: (0, j)),
                  pl.BlockSpec((t, HEAD), lambda j: (0, j))],
        out_specs=[pl.BlockSpec((t, HEAD), lambda j: (0, j)), pl.BlockSpec((CONV_K, HEAD), lambda j: (0, j))],
        out_shape=[SDS((t, width), BF16), SDS((CONV_K, width), F32)],
        name="gdn_conv_bwd", compiler_params=_params(1),
    )(proj, conv_w, dc)


def _l2norm(x, scale):
    return x * lax.rsqrt(jnp.sum(x * x, axis=-1, keepdims=True) + EPS) * scale


def _l2_fwd(c, col_off, scale, name):
    t = c.shape[0]
    tt = _tile(t, (256, 128))
    return _tmap(lambda x: _l2norm(x, scale), (GDN_QK_HEADS, t // tt), [(c, *_rows(HEAD, tt, col_off))],
                 [((t, D_MODEL), F32, *_rows(HEAD, tt), None)], name)[0]


def _l2_bwd(c, col_off, scale, d_rep, name):
    t = c.shape[0]
    tt = _tile(t, (256, 128))

    def fn(x, d2):
        _, vjp = jax.vjp(lambda x: _l2norm(x, scale), x)
        return vjp(d2[:, :HEAD] + d2[:, HEAD:])[0]

    return _tmap(fn, (GDN_QK_HEADS, t // tt), [(c, *_rows(HEAD, tt, col_off)), (d_rep, *_rows(2 * HEAD, tt))],
                 [((t, D_MODEL), F32, *_rows(HEAD, tt), None)], name)[0]


def _gdn_gates(x, alog, dtb):
    return -jnp.exp(alog) * _softplus(x + dtb), jax.nn.sigmoid(x)


def _gates_fwd(pab, alog, dtb):
    t = pab.shape[0]
    tt = _tile(t, (256, 128))

    def fn(x, alog, dtb):
        g, beta = _gdn_gates(x, alog, dtb)
        lane = lax.broadcasted_iota(jnp.int32, g.shape, 1)
        return jnp.where(lane < GDN_V_HEADS, _hdot_raw(_chunk_sum_matrix(tt, GDN_CHUNK), g), beta)

    p = (alog, (1, HEAD), lambda j, i: (0, 0)), (dtb, (1, HEAD), lambda j, i: (0, 0))
    return _tmap(fn, (1, t // tt), [(pab, *_rows(HEAD, tt)), *p], [((t, HEAD), F32, *_rows(HEAD, tt), None)], "gdn_gates")[0]


def _gates_bwd(pab, alog, dtb, dout):
    t = pab.shape[0]
    tt = _tile(t, (256, 128))

    def fn(x, alog, dtb, dout):
        lane = lax.broadcasted_iota(jnp.int32, dout.shape, 1)
        dgam = jnp.where(lane < GDN_V_HEADS, dout, 0.0)
        dbeta = jnp.where(jnp.logical_and(lane >= GDN_V_HEADS, lane < 2 * GDN_V_HEADS), dout, 0.0)
        dg = _hdot_raw(_chunk_sum_matrix(tt, GDN_CHUNK, transpose=True), dgam)
        _, vjp = jax.vjp(_gdn_gates, x, alog, dtb)
        return vjp((dg, dbeta))

    p = (alog, (1, HEAD), lambda j, i: (0, 0)), (dtb, (1, HEAD), lambda j, i: (0, 0))
    acc = ((1, HEAD), F32, (1, HEAD), lambda j, i: (0, 0), "inner")
    return _tmap(fn, (1, t // tt), [(pab, *_rows(HEAD, tt)), *p, (dout, *_rows(HEAD, tt))],
                 [((t, HEAD), BF16, *_rows(HEAD, tt), None), acc, acc], "gdn_gates_bwd")


def _split_bf16(x):
    hi = x.astype(BF16)
    return hi, (x - hi.astype(F32)).astype(BF16)


def _dot3(a, b):
    (ah, al), (bh, bl) = a, b
    return _bdot_raw(ah, bh, "nn") + (_bdot_raw(ah, bl, "nn") + _bdot_raw(al, bh, "nn"))


def _each(fn, *lists):
    return tuple(fn(*xs) for xs in zip(*lists))


def _unit_lower_inverses_raw(a):
    n = a[0].shape[0]
    row = lax.broadcasted_iota(jnp.int32, (n, n), 0)
    col = lax.broadcasted_iota(jnp.int32, (n, n), 1)
    eye = jnp.where(row == col, 1.0, 0.0).astype(F32)
    p = _each(lambda a: eye - a, a)
    x = _each(_split_bf16, a)
    m = 2
    while m < 2 * n:
        x = _each(_split_bf16, _each(_dot3, x, x))
        p = _each(lambda p, x: p + _dot3(_split_bf16(p), x), p, x)
        m *= 2
    return p


@jax.custom_vjp
def _unit_lower_inverses(a, known):
    return _unit_lower_inverses_raw(a) if known is None else known


def _uli_fwd(a, known):
    inv = _unit_lower_inverses(a, known)
    return inv, (inv, known)


def _uli_bwd(res, ct):
    inv, known = res
    right = _each(lambda ct, inv: _bdot_raw(ct, inv, "nt"), ct, inv)
    da = _each(lambda inv, r: -_bdot_raw(inv, r, "tn"), inv, right)
    return da, (None if known is None else _each(jnp.zeros_like, known))


_unit_lower_inverses.defvjp(_uli_fwd, _uli_bwd)


def _gdn_chunks(q, k, v, beta, gam, gam_row, s, inv_known=None):
    n = q[0].shape[0]
    row = lax.broadcasted_iota(jnp.int32, (n, n), 0)
    col = lax.broadcasted_iota(jnp.int32, (n, n), 1)
    decay = _each(lambda gam, gam_row: jnp.where(row >= col, jnp.exp(jnp.minimum(gam - gam_row, 0.0)), 0.0), gam, gam_row)
    kb = _each(lambda k, beta: k * beta, k, beta)
    a = _each(lambda kb, k, decay: jnp.where(row > col, _bdot(kb, k, "nt") * decay, 0.0), kb, k, decay)
    inv = _unit_lower_inverses(a, inv_known)
    eg = _each(jnp.exp, gam)
    u = _each(lambda inv, v, beta: _bdot(inv, v * beta, "nn"), inv, v, beta)
    w = _each(lambda inv, kb, eg: _bdot(inv, kb * eg, "nn"), inv, kb, eg)
    qk = _each(lambda q, k, decay: _bdot(q, k, "nt") * decay, q, k, decay)
    v_new = _each(lambda u, w, s: u - _bdot(w, s, "nn"), u, w, s)
    o_state = _each(lambda q, eg, s: _bdot(q * eg, s, "nn"), q, eg, s)
    o = _each(lambda o_state, qk, v_new: o_state + _bdot(qk, v_new, "nn"), o_state, qk, v_new)
    gend = _each(lambda gam: gam[n - 1:n, :], gam)
    s_new = _each(lambda s, k, gam, gend, v_new: s * jnp.exp(gend) + _bdot(k * jnp.exp(gend - gam), v_new, "tn"), s, k, gam, gend, v_new)
    return o, s_new, inv


GDN_GROUP = 16


def _gdn_specs(nc, rev):
    cc = (lambda c: nc - 1 - c) if rev else (lambda c: c)
    grp = GDN_GROUP
    qk = pl.BlockSpec((GDN_CHUNK, grp // 2 * HEAD), lambda h, c: (cc(c), h))
    v = pl.BlockSpec((GDN_CHUNK, grp * HEAD), lambda h, c: (cc(c), 2 * GDN_QK_HEADS // grp + h))
    o = pl.BlockSpec((GDN_CHUNK, grp * HEAD), lambda h, c: (cc(c), h))
    col = pl.BlockSpec((grp, None, GDN_CHUNK, 1), lambda h, c: (h, cc(c), 0, 0))
    rw = pl.BlockSpec((grp, None, 1, GDN_CHUNK), lambda h, c: (h, cc(c), 0, 0))
    st = pl.BlockSpec((grp, None, HEAD, HEAD), lambda h, c: (h, cc(c), 0, 0))
    inv = pl.BlockSpec((grp, None, GDN_CHUNK, GDN_CHUNK), lambda h, c: (h, cc(c), 0, 0))
    return qk, v, o, col, rw, st, inv


def _head_lanes(g, per=1):
    return pl.ds((g // per) * HEAD, HEAD)


def _gdn_rec_fwd(qg, kg, c, beta_col, gam_col, gam_row):
    t = qg.shape[0]
    nc = t // GDN_CHUNK
    qk, v, o, col, rw, st, inv = _gdn_specs(nc, False)

    def body(q_ref, k_ref, v_ref, be_ref, g_ref, gr_ref, o_ref, ss_ref, inv_ref, s_ref):
        @pl.when(pl.program_id(1) == 0)
        def _():
            s_ref[...] = jnp.zeros_like(s_ref)

        heads = range(GDN_GROUP)
        s = tuple(s_ref[g] for g in heads)
        out, s_new, inv_c = _gdn_chunks(
            tuple(q_ref[:, _head_lanes(g, 2)] for g in heads), tuple(k_ref[:, _head_lanes(g, 2)] for g in heads),
            tuple(v_ref[:, _head_lanes(g)] for g in heads), tuple(be_ref[g] for g in heads), tuple(g_ref[g] for g in heads),
            tuple(gr_ref[g] for g in heads), s)
        for g in heads:
            ss_ref[g] = s[g]
            o_ref[:, _head_lanes(g)] = out[g]
            inv_ref[g] = inv_c[g]
            s_ref[g] = s_new[g]

    return pl.pallas_call(
        body, grid=(GDN_V_HEADS // GDN_GROUP, nc), in_specs=[qk, qk, v, col, col, rw], out_specs=[o, st, inv],
        out_shape=[SDS((t, 2 * D_MODEL), F32), SDS((GDN_V_HEADS, nc, HEAD, HEAD), F32), SDS((GDN_V_HEADS, nc, GDN_CHUNK, GDN_CHUNK), F32)],
        scratch_shapes=[pltpu.VMEM((GDN_GROUP, HEAD, HEAD), F32)], name="gdn_rec", compiler_params=_params(2),
    )(qg, kg, c, beta_col, gam_col, gam_row)


def _gdn_rec_bwd(qg, kg, c, beta_col, gam_col, gam_row, ss, invs, do):
    t = qg.shape[0]
    nc = t // GDN_CHUNK
    qk, v, o, col, rw, st, inv = _gdn_specs(nc, True)

    def body(q_ref, k_ref, v_ref, be_ref, g_ref, gr_ref, ss_ref, inv_ref, do_ref,
             dq_ref, dk_ref, dv_ref, dbe_ref, dg_ref, dgr_ref, ds_ref):
        @pl.when(pl.program_id(1) == 0)
        def _():
            ds_ref[...] = jnp.zeros_like(ds_ref)

        heads = range(GDN_GROUP)
        _, vjp = jax.vjp(
            _gdn_chunks,
            tuple(q_ref[:, _head_lanes(g, 2)] for g in heads), tuple(k_ref[:, _head_lanes(g, 2)] for g in heads),
            tuple(v_ref[:, _head_lanes(g)] for g in heads), tuple(be_ref[g] for g in heads), tuple(g_ref[g] for g in heads),
            tuple(gr_ref[g] for g in heads), tuple(ss_ref[g] for g in heads), tuple(inv_ref[g] for g in heads))
        no_inv_ct = tuple(jnp.zeros((GDN_CHUNK, GDN_CHUNK), F32) for g in heads)
        dq, dk, dv, dbe, dg, dgr, ds, _ = vjp((tuple(do_ref[:, _head_lanes(g)] for g in heads), tuple(ds_ref[g] for g in heads), no_inv_ct))
        for g in heads:
            dq_ref[:, _head_lanes(g)] = dq[g]
            dk_ref[:, _head_lanes(g)] = dk[g]
            dv_ref[:, _head_lanes(g)] = dv[g]
            dbe_ref[g] = dbe[g]
            dg_ref[g] = dg[g]
            dgr_ref[g] = dgr[g]
            ds_ref[g] = ds[g]

    wide = SDS((t, 2 * D_MODEL), F32)
    colshape = SDS((GDN_V_HEADS, nc, GDN_CHUNK, 1), F32)
    return pl.pallas_call(
        body, grid=(GDN_V_HEADS // GDN_GROUP, nc), in_specs=[qk, qk, v, col, col, rw, st, inv, o], out_specs=[o, o, o, col, col, rw],
        out_shape=[wide, wide, wide, colshape, colshape, SDS((GDN_V_HEADS, nc, 1, GDN_CHUNK), F32)],
        scratch_shapes=[pltpu.VMEM((GDN_GROUP, HEAD, HEAD), F32)], name="gdn_rec_bwd", compiler_params=_params(2),
    )(qg, kg, c, beta_col, gam_col, gam_row, ss, invs, do)


def _gated_norm(o, gate, w):
    return _rms(o, w) * _silu(gate)


def _post_fwd(o, proj, col_off, w, name):
    t, width = o.shape
    tt = _tile(t, (256, 128))
    return _tmap(_gated_norm, (width // HEAD, t // tt),
                 [(o, *_rows(HEAD, tt)), (proj, *_rows(HEAD, tt, col_off)), (w, (1, HEAD), lambda j, i: (0, 0))],
                 [((t, width), BF16, *_rows(HEAD, tt), None)], name)[0]


def _post_bwd(o, proj, col_off, w, dout, name):
    t, width = o.shape
    tt = _tile(t, (256, 128))

    def fn(o, gate, w, dout):
        _, vjp = jax.vjp(_gated_norm, o, gate, w)
        return vjp(dout)

    return _tmap(fn, (width // HEAD, t // tt),
                 [(o, *_rows(HEAD, tt)), (proj, *_rows(HEAD, tt, col_off)), (w, (1, HEAD), lambda j, i: (0, 0)), (dout, *_rows(HEAD, tt))],
                 [((t, width), F32, *_rows(HEAD, tt), None), ((t, width), BF16, *_rows(HEAD, tt), None),
                  ((1, HEAD), F32, (1, HEAD), lambda j, i: (0, 0), "all")], name)


def _merge(gate_h, gate_g, yh, yg):
    return jax.nn.sigmoid(gate_h) * yh + jax.nn.sigmoid(gate_g) * yg


def _merge_fwd(proj, yh, yg):
    t = yh.shape[0]
    tt, ft = _tile(t, (256, 128)), 512
    r = _rows(ft, tt)
    return _tmap(_merge, (D_MODEL // ft, t // tt),
                 [(proj, *_rows(ft, tt, COL_GATE_H * HEAD // ft)), (proj, *_rows(ft, tt, COL_GATE_G * HEAD // ft)), (yh, *r), (yg, *r)],
                 [((t, D_MODEL), BF16, *r, None)], "merge")[0]


def _merge_bwd(proj, yh, yg, dy):
    t = yh.shape[0]
    tt, ft = _tile(t, (256, 128)), 512
    r = _rows(ft, tt)

    def fn(gate_h, gate_g, yh, yg, dy):
        _, vjp = jax.vjp(_merge, gate_h, gate_g, yh, yg)
        return vjp(dy)

    o = ((t, D_MODEL), BF16, *r, None)
    return _tmap(fn, (D_MODEL // ft, t // tt),
                 [(proj, *_rows(ft, tt, COL_GATE_H * HEAD // ft)), (proj, *_rows(ft, tt, COL_GATE_G * HEAD // ft)), (yh, *r), (yg, *r), (dy, *r)],
                 [o, o, o, o], "merge_bwd")


def _loss_head(h, target, g):
    t, d = h.shape
    tt = _tile(t, (256, 128))

    def fn(h, target, g):
        def f(h, g):
            err = _rms(h, g) - target
            return 0.5 * jnp.sum(jnp.mean(err * err, axis=-1))

        loss, (dh, dg) = jax.value_and_grad(f, (0, 1))(h, g)
        return dh, dg, jnp.full((1, HEAD), loss, F32)

    return _tmap(fn, (1, t // tt), [(h, *_rows(d, tt)), (target, *_rows(d, tt)), (g, (1, d), lambda j, i: (0, 0))],
                 [((t, d), F32, *_rows(d, tt), None), ((1, d), F32, (1, d), lambda j, i: (0, 0), "inner"),
                  ((1, HEAD), F32, (1, HEAD), lambda j, i: (0, 0), "inner")], "loss_head")


def _heads_to_cols(a):
    t = a.shape[0]
    return a.T.reshape(GDN_V_HEADS, t // GDN_CHUNK, GDN_CHUNK, 1)


def _mixer_fwd(h, p, links):
    t = h.shape[0]
    nc = t // GDN_CHUNK
    u = _rms_fwd(h, p["mix_norm"], "mix_norm")
    w = {n: links.weight(n, h) for n in ("w_in_main_t", "w_in_ab_t", "conv_w")}
    proj = _mm(u, w["w_in_main_t"], "nt", F32, "mix_in")
    pab = _mm(u, w["w_in_ab_t"], "nt", F32, "mix_in_ab")
    qh, kh, bh = _hgrn_prep_fwd(proj, p["lbl"])
    oh, hs = _hgrn_rec_fwd(qh, kh, proj, bh)
    c = _conv_fwd(proj, w["conv_w"])
    qg = _l2_fwd(c, 0, HEAD ** -0.5, "gdn_qnorm")
    kg = _l2_fwd(c, GDN_QK_HEADS, 1.0, "gdn_knorm")
    gates = _gates_fwd(pab, p["alog"], p["dtb"])
    gam = gates[:, :GDN_V_HEADS]
    beta_col = _heads_to_cols(gates[:, GDN_V_HEADS:2 * GDN_V_HEADS])
    gam_col = _heads_to_cols(gam)
    gam_row = gam.T.reshape(GDN_V_HEADS, nc, 1, GDN_CHUNK)
    og, ss, invs = _gdn_rec_fwd(qg, kg, c, beta_col, gam_col, gam_row)
    ohn = _post_fwd(oh, proj, COL_HG, p["hgrn_out_norm"], "hgrn_out")
    ogn = _post_fwd(og, proj, COL_GZ, p["gdn_out_norm"], "gdn_out")
    w.update({n: links.weight(n, ogn) for n in ("w_branch_hgrn", "w_branch_gdn", "w_out")})
    yh = _mm(ohn, w["w_branch_hgrn"], "nn", F32, "branch_hgrn")
    yg = _mm(ogn, w["w_branch_gdn"], "nn", F32, "branch_gdn")
    y = _merge_fwd(proj, yh, yg)
    out = _mm(y, w["w_out"], "nn", F32, "mix_out", res=h)
    saved = (w, u, proj, pab, qh, kh, bh, oh, hs, c, qg, kg, beta_col, gam_col, gam_row, og, ss, invs, ohn, ogn, yh, yg, y)
    return out, saved


def _mixer_bwd(h, p, links, saved, dout):
    (w, u, proj, pab, qh, kh, bh, oh, hs, c, qg, kg, beta_col, gam_col, gam_row, og, ss, invs, ohn, ogn, yh, yg, y) = saved
    t = h.shape[0]
    grads = {}
    dw_out = _mm(y, dout, "tn", BF16, "mix_out_dw")
    dy = _mm(dout, w["w_out"], "nt", F32, "mix_out_dx")
    dgate_h, dgate_g, dyh, dyg = _merge_bwd(proj, yh, yg, dy)
    dw_bh = _mm(ohn, dyh, "tn", BF16, "branch_hgrn_dw")
    dw_bg = _mm(ogn, dyg, "tn", BF16, "branch_gdn_dw")
    sent = links.send({"w_out": dw_out, "w_branch_hgrn": dw_bh, "w_branch_gdn": dw_bg})
    beta_col = beta_col + sent
    dohn = _mm(dyh, w["w_branch_hgrn"], "nt", F32, "branch_hgrn_dx")
    dogn = _mm(dyg, w["w_branch_gdn"], "nt", F32, "branch_gdn_dx")
    doh, dhg, grads["hgrn_out_norm"] = _post_bwd(oh, proj, COL_HG, p["hgrn_out_norm"], dohn, "hgrn_out_bwd")
    dog, dgz, grads["gdn_out_norm"] = _post_bwd(og, proj, COL_GZ, p["gdn_out_norm"], dogn, "gdn_out_bwd")
    dqh, dkh, dhi, dbh = _hgrn_rec_bwd(qh, kh, proj, bh, hs, doh)
    dhq, dhf, grads["lbl"] = _hgrn_prep_bwd(proj, p["lbl"], dqh, dkh, dbh)
    dqv, dkv, dcv, dbeta_col, dgam_col, dgam_row = _gdn_rec_bwd(qg, kg, c, beta_col, gam_col, gam_row, ss, invs, dog)
    dcq = _l2_bwd(c, 0, HEAD ** -0.5, dqv, "gdn_qnorm_bwd")
    dck = _l2_bwd(c, GDN_QK_HEADS, 1.0, dkv, "gdn_knorm_bwd")
    dxin, grads["conv_w"] = _conv_bwd(proj, w["conv_w"], jnp.concatenate([dcq, dck, dcv], axis=1))
    dgam = (dgam_col.reshape(GDN_V_HEADS, t) + dgam_row.reshape(GDN_V_HEADS, t)).T
    dbeta = dbeta_col.reshape(GDN_V_HEADS, t).T
    dgates = jnp.concatenate([dgam, dbeta, jnp.zeros((t, HEAD - 2 * GDN_V_HEADS), F32)], axis=1)
    dpab, grads["alog"], grads["dtb"] = _gates_bwd(pab, p["alog"], p["dtb"], dgates)
    dproj = jnp.concatenate([dhq, dhf, dhi, dhg, dxin, dgz, dgate_h, dgate_g], axis=1)
    dw_main_t = _mm(dproj, u, "tn", BF16, "mix_in_dw")
    dw_ab_t = _mm(dpab, u, "tn", BF16, "mix_in_ab_dw")
    sent = links.send({"w_in": _w_in_join(dw_main_t, dw_ab_t)})
    du = _mm(dproj, w["w_in_main_t"], "nn", F32, "mix_in_dx")
    du = _mm(dpab, w["w_in_ab_t"] + sent.astype(BF16), "nn", F32, "mix_in_ab_dx", res=du)
    dh, grads["mix_norm"] = _rms_bwd(h, p["mix_norm"], du, dout, "mix_norm_bwd")
    return dh, grads


def _local_step(x, target, p, links):
    w1 = {n: links.weight(n, x) for n in ("ffn1_w_in", "ffn1_w_out")}
    h1, s1 = _ffn_fwd(x, p["ffn1_norm"], w1["ffn1_w_in"], w1["ffn1_w_out"], "ffn1")
    h2, sm = _mixer_fwd(h1, p, links)
    w2 = {n: links.weight(n, h2) for n in ("ffn2_w_in", "ffn2_w_out")}
    h3, s2 = _ffn_fwd(h2, p["ffn2_norm"], w2["ffn2_w_in"], w2["ffn2_w_out"], "ffn2")
    dh3, dfinal, loss = _loss_head(h3, target, p["final_norm"])
    g = {"final_norm": dfinal}
    dh2, g["ffn2_norm"], _ = _ffn_bwd(h2, p["ffn2_norm"], w2["ffn2_w_in"], w2["ffn2_w_out"], s2, dh3, "ffn2", links)
    dh1, gm = _mixer_bwd(h1, p, links, sm, dh2)
    g.update(gm)
    dx, g["ffn1_norm"], _ = _ffn_bwd(x, p["ffn1_norm"], w1["ffn1_w_in"], w1["ffn1_w_out"], s1, dh1, "ffn1", links)
    return loss, dx, g


def _exchange(items, name):
    n = len(items)
    out_shape = [SDS((N_DEV,) + a.shape if mode == "gather" else a.shape, a.dtype) for a, mode in items]

    def body(*refs):
        in_refs, out_refs = refs[:n], refs[n:2 * n]
        send_sems, recv_sems, local_sems = refs[2 * n:]
        x, y, c = lax.axis_index("x"), lax.axis_index("y"), lax.axis_index("c")
        me = 4 * x + 2 * y + c
        local = []
        for i, (_, mode) in enumerate(items):
            src = in_refs[i] if mode == "gather" else in_refs[i].at[me]
            cp = pltpu.make_async_copy(src, out_refs[i].at[me], local_sems.at[i])
            cp.start()
            local.append(cp)
        remote = []
        for rel in range(1, N_DEV):
            px = 1 - x if rel & 4 else x
            py = 1 - y if rel & 2 else y
            pc = 1 - c if rel & 1 else c
            peer = 4 * px + 2 * py + pc
            for i, (_, mode) in enumerate(items):
                src = in_refs[i] if mode == "gather" else in_refs[i].at[peer]
                cp = pltpu.make_async_remote_copy(src_ref=src, dst_ref=out_refs[i].at[me], send_sem=send_sems.at[i, rel - 1],
                                                  recv_sem=recv_sems.at[i, rel - 1], device_id=(px, py, pc), device_id_type=MESH_IDS)
                cp.start()
                remote.append(cp)
        for cp in remote:
            cp.wait()
        for cp in local:
            cp.wait()

    anyspace = pl.BlockSpec(memory_space=pl.ANY)
    return pl.pallas_call(
        body, in_specs=[anyspace] * n, out_specs=[anyspace] * n, out_shape=out_shape,
        scratch_shapes=[pltpu.SemaphoreType.DMA((n, N_DEV - 1)), pltpu.SemaphoreType.DMA((n, N_DEV - 1)), pltpu.SemaphoreType.DMA((n,))],
        name=name, compiler_params=pltpu.CompilerParams(has_side_effects=True),
    )(*[a for a, _ in items])


HBM_SPEC = pl.BlockSpec(memory_space=pltpu.HBM)
SEM_SPEC = pl.BlockSpec(memory_space=pltpu.SEMAPHORE)
DATAFLOW = pltpu.SideEffectType.DATAFLOW_SIDE_EFFECTING


def _position():
    x, y, c = lax.axis_index("x"), lax.axis_index("y"), lax.axis_index("c")
    return x, y, c, 4 * x + 2 * y + c


def _relations(x, y, c):
    for rel in range(1, N_DEV):
        px = 1 - x if rel & 4 else x
        py = 1 - y if rel & 2 else y
        pc = 1 - c if rel & 1 else c
        yield rel, (px, py, pc), 4 * px + 2 * py + pc


def _sem_index(item, rel):
    return item * (N_DEV - 1) + rel - 1


def _landing(a, mode):
    return lax.empty((N_DEV,) + a.shape if mode == "gather" else a.shape, a.dtype)


def _copies_start(groups, name):
    flat = [item for grp in groups for item in grp]
    n, ng = len(flat), len(groups)
    lands = [_landing(a, mode) for a, mode in flat]

    def body(*refs):
        src_refs, land_refs, sems, token = refs[:n], refs[n:2 * n], refs[2 * n:2 * n + 2 * ng], refs[-1]
        x, y, c, me = _position()
        for rel, where, peer in _relations(x, y, c):
            k = 0
            for gi, grp in enumerate(groups):
                for li, (_, mode) in enumerate(grp):
                    src = src_refs[k] if mode == "gather" else src_refs[k].at[peer]
                    pltpu.make_async_remote_copy(src_ref=src, dst_ref=land_refs[k].at[me], send_sem=sems[2 * gi].at[_sem_index(li, rel)],
                                                 recv_sem=sems[2 * gi + 1].at[_sem_index(li, rel)], device_id=where, device_id_type=MESH_IDS).start()
                    k += 1
        token[...] = jnp.zeros_like(token)

    sem_shapes = [pltpu.SemaphoreType.DMA((len(grp) * (N_DEV - 1),)) for grp in groups for _ in range(2)]
    thru = [pltpu.HBM(a.shape, a.dtype) for a, _ in flat] + [pltpu.HBM(l.shape, l.dtype) for l in lands]
    outs = pl.pallas_call(
        body, name=name, out_shape=(*sem_shapes, *thru, SDS((8, HEAD), F32)),
        in_specs=[HBM_SPEC] * (2 * n), out_specs=(*[SEM_SPEC] * (2 * ng), *[HBM_SPEC] * (2 * n), pl.BlockSpec(memory_space=pltpu.VMEM)),
        input_output_aliases={i: 2 * ng + i for i in range(2 * n)}, compiler_params=pltpu.CompilerParams(has_side_effects=DATAFLOW),
    )(*[pltpu.with_memory_space_constraint(a, pltpu.HBM) for a, _ in flat], *[pltpu.with_memory_space_constraint(l, pltpu.HBM) for l in lands])
    sems, srcs, landed, token = outs[:2 * ng], outs[2 * ng:2 * ng + n], outs[2 * ng + n:2 * ng + 2 * n], outs[-1]
    result, k = [], 0
    for gi, grp in enumerate(groups):
        result.append((sems[2 * gi], sems[2 * gi + 1], srcs[k:k + len(grp)], landed[k:k + len(grp)]))
        k += len(grp)
    return result, token[0, 0]


def _copies_wait(started, modes, after, name):
    send_sems, recv_sems, srcs, lands = started
    n = len(srcs)

    def body(*refs):
        src_refs, land_refs, ssem, rsem = refs[:n], refs[n:2 * n], refs[2 * n], refs[2 * n + 1]
        x, y, c, _ = _position()
        for rel in range(1, N_DEV):
            for i, mode in enumerate(modes):
                src = src_refs[i] if mode == "gather" else src_refs[i].at[0]
                cp = pltpu.make_async_remote_copy(src_ref=src, dst_ref=land_refs[i].at[0], send_sem=ssem.at[_sem_index(i, rel)],
                                                  recv_sem=rsem.at[_sem_index(i, rel)], device_id=(x, y, c), device_id_type=MESH_IDS)
                cp.wait_send()
                cp.wait_recv()

    outs = pl.pallas_call(
        body, name=name, out_shape=[pltpu.HBM(a.shape, a.dtype) for a in (*srcs, *lands)],
        in_specs=[HBM_SPEC] * (2 * n) + [SEM_SPEC, SEM_SPEC, pl.BlockSpec(memory_space=pl.ANY)], out_specs=[HBM_SPEC] * (2 * n),
        input_output_aliases={i: i for i in range(2 * n)}, compiler_params=pltpu.CompilerParams(has_side_effects=DATAFLOW),
    )(*srcs, *lands, send_sems, recv_sems, after)
    return outs[:n], outs[n:]


WEIGHT_GROUPS = (("ffn1_w_in", "ffn1_w_out", "gdn_conv_w"), ("w_in",), ("w_branch_hgrn", "w_branch_gdn", "w_out", "ffn2_w_in", "ffn2_w_out"))


class _Links:
    def __init__(self, shards, me):
        self.me = me
        self.shards = shards
        groups = [[(shards[n], "gather") for n in grp] for grp in WEIGHT_GROUPS]
        self.gathers, self.token = _copies_start(groups, "gather_start")
        self.weights = {}
        self.sends = []

    def weight(self, name, after):
        if name not in self.weights:
            gi = [i for i, grp in enumerate(WEIGHT_GROUPS) if {"w_in_main_t": "w_in", "w_in_ab_t": "w_in", "conv_w": "gdn_conv_w"}.get(name, name) in grp][0]
            srcs, lands = _copies_wait(self.gathers[gi], ["gather"] * len(WEIGHT_GROUPS[gi]), after, "gather_wait_%d" % gi)
            for n, src, land in zip(WEIGHT_GROUPS[gi], srcs, lands):
                full = lax.dynamic_update_index_in_dim(land, src, self.me, 0)
                if n == "gdn_conv_w":
                    self.weights["conv_w"] = full.reshape(N_DEV, CONV_K, 4 * D_MODEL // N_DEV).transpose(1, 0, 2).reshape(CONV_K, 4 * D_MODEL)
                elif n == "w_in":
                    self.weights["w_in_main_t"], self.weights["w_in_ab_t"] = _w_in_split(full.reshape(-1, D_MODEL))
                else:
                    self.weights[n] = full.reshape(-1, D_MODEL)
        return self.weights[name]

    def send(self, grads):
        names = list(grads)
        blocks = [grads[n].reshape(N_DEV, -1, D_MODEL) for n in names]
        started, token = _copies_start([[(b, "scatter") for b in blocks]], "send_" + names[0])
        self.sends.append((names, started[0]))
        return token

    def landed(self, after):
        out = {}
        for names, started in self.sends:
            srcs, lands = _copies_wait(started, ["scatter"] * len(names), after, "landed_" + names[0])
            for n, src, land in zip(names, srcs, lands):
                out[n] = lax.dynamic_update_index_in_dim(land, lax.dynamic_index_in_dim(src, self.me, 0, keepdims=False), self.me, 0)
        return out


def _adam(parts, w, m, v, name):
    n_parts, r, c = parts.shape
    tc = c if c <= 512 else (256 if r > 1024 else 512)

    def body(p_ref, w_ref, m_ref, v_ref, g_ref, d_ref, mo_ref, vo_ref):
        g = p_ref[0].astype(F32)
        for i in range(1, n_parts):
            g = g + p_ref[i].astype(F32)
        m_new = ADAM_B1 * m_ref[...] + (1.0 - ADAM_B1) * g
        v_new = ADAM_B2 * v_ref[...] + (1.0 - ADAM_B2) * (g * g)
        m_hat = m_new / (1.0 - ADAM_B1 ** ADAM_STEP)
        v_hat = v_new / (1.0 - ADAM_B2 ** ADAM_STEP)
        g_ref[...] = g
        d_ref[...] = -ADAM_LR * (m_hat / (jnp.sqrt(v_hat) + ADAM_EPS) + ADAM_WD * w_ref[...])
        mo_ref[...] = m_new
        vo_ref[...] = v_new

    spec = pl.BlockSpec((r, tc), lambda j: (0, j))
    return pl.pallas_call(
        body, grid=(c // tc,), in_specs=[pl.BlockSpec((n_parts, r, tc), lambda j: (0, 0, j)), spec, spec, spec],
        out_specs=[spec] * 4, out_shape=[SDS((r, c), F32)] * 4, name=name, compiler_params=_params(1),
    )(parts, w, m, v)


BIG = ("ffn1_w_in", "ffn1_w_out", "w_in", "w_branch_hgrn", "w_branch_gdn", "w_out", "ffn2_w_in", "ffn2_w_out")


TRANSPOSED = ("ffn1_w_in", "w_in", "ffn2_w_in")


def _shard_rows(name, shard):
    return shard.T if name in TRANSPOSED else shard


SCALAR_ROWS = 8192
N_SCALAR = 2 * GDN_V_HEADS


def _w_in_split(w_in_t):
    main = jnp.concatenate([w_in_t[:SCALAR_ROWS], w_in_t[SCALAR_ROWS + N_SCALAR:]], axis=0)
    ab = jnp.pad(w_in_t[SCALAR_ROWS:SCALAR_ROWS + N_SCALAR], ((0, HEAD - N_SCALAR), (0, 0)))
    return main, ab


def _w_in_join(dmain_t, dab_t):
    return jnp.concatenate([dmain_t[:SCALAR_ROWS], dab_t[:N_SCALAR], dmain_t[SCALAR_ROWS:]], axis=0)


def _pad_lanes(a, width=HEAD):
    return jnp.pad(a, ((0, 0), (0, width - a.shape[1])))


SMALL_ROWS = 24


def _pack_small(g, loss):
    row6 = jnp.concatenate([g["hgrn_out_norm"], g["gdn_out_norm"], g["alog"], g["dtb"], loss,
                            jnp.zeros((1, D_MODEL - 5 * HEAD), F32)], axis=1)
    return jnp.concatenate([g["ffn1_norm"], g["mix_norm"], g["lbl"], g["ffn2_norm"], g["final_norm"], row6,
                            jnp.zeros((1, D_MODEL), F32), g["conv_w"].reshape(4 * CONV_K, D_MODEL)], axis=0)


def _pack_small_state(a):
    row6 = jnp.concatenate([a["hgrn_out_norm"], a["gdn_out_norm"], _pad_lanes(a["gdn_a_log"]), _pad_lanes(a["gdn_dt_bias"]),
                            jnp.zeros((1, D_MODEL - 4 * HEAD), F32)], axis=1)
    return jnp.concatenate([a["ffn1_norm"], a["mix_norm"], a["hgrn_lb_logits"], a["ffn2_norm"], a["final_norm"].reshape(1, D_MODEL),
                            row6, jnp.zeros((1, D_MODEL), F32)], axis=0)


def _unpack_small(a):
    return {"ffn1_norm": a[0:1], "mix_norm": a[1:2], "hgrn_lb_logits": a[2:4], "ffn2_norm": a[4:5], "final_norm": a[5],
            "hgrn_out_norm": a[6:7, :HEAD], "gdn_out_norm": a[6:7, HEAD:2 * HEAD],
            "gdn_a_log": a[6:7, 2 * HEAD:2 * HEAD + GDN_V_HEADS], "gdn_dt_bias": a[6:7, 3 * HEAD:3 * HEAD + GDN_V_HEADS]}


NAMES = ("ffn1_norm", "ffn1_w_in", "ffn1_w_out", "mix_norm", "w_in", "hgrn_lb_logits", "hgrn_out_norm", "gdn_conv_w", "gdn_a_log",
         "gdn_dt_bias", "gdn_out_norm", "w_branch_hgrn", "w_branch_gdn", "w_out", "ffn2_norm", "ffn2_w_in", "ffn2_w_out", "final_norm")


def kernel(x, ffn1_norm, ffn1_w_in, ffn1_w_out, mix_norm, w_in, hgrn_lb_logits, hgrn_out_norm, gdn_conv_w, gdn_a_log, gdn_dt_bias, gdn_out_norm, w_branch_hgrn, w_branch_gdn, w_out, ffn2_norm, ffn2_w_in, ffn2_w_out, final_norm, loss_target, m_ffn1_norm, m_ffn1_w_in, m_ffn1_w_out, m_mix_norm, m_w_in, m_hgrn_lb_logits, m_hgrn_out_norm, m_gdn_conv_w, m_gdn_a_log, m_gdn_dt_bias, m_gdn_out_norm, m_w_branch_hgrn, m_w_branch_gdn, m_w_out, m_ffn2_norm, m_ffn2_w_in, m_ffn2_w_out, m_final_norm, v_ffn1_norm, v_ffn1_w_in, v_ffn1_w_out, v_mix_norm, v_w_in, v_hgrn_lb_logits, v_hgrn_out_norm, v_gdn_conv_w, v_gdn_a_log, v_gdn_dt_bias, v_gdn_out_norm, v_w_branch_hgrn, v_w_branch_gdn, v_w_out, v_ffn2_norm, v_ffn2_w_in, v_ffn2_w_out, v_final_norm):
    wts = dict(zip(NAMES, (ffn1_norm, ffn1_w_in, ffn1_w_out, mix_norm, w_in, hgrn_lb_logits, hgrn_out_norm, gdn_conv_w, gdn_a_log,
                           gdn_dt_bias, gdn_out_norm, w_branch_hgrn, w_branch_gdn, w_out, ffn2_norm, ffn2_w_in, ffn2_w_out, final_norm)))
    mom = dict(zip(NAMES, (m_ffn1_norm, m_ffn1_w_in, m_ffn1_w_out, m_mix_norm, m_w_in, m_hgrn_lb_logits, m_hgrn_out_norm, m_gdn_conv_w,
                           m_gdn_a_log, m_gdn_dt_bias, m_gdn_out_norm, m_w_branch_hgrn, m_w_branch_gdn, m_w_out, m_ffn2_norm, m_ffn2_w_in,
                           m_ffn2_w_out, m_final_norm)))
    var = dict(zip(NAMES, (v_ffn1_norm, v_ffn1_w_in, v_ffn1_w_out, v_mix_norm, v_w_in, v_hgrn_lb_logits, v_hgrn_out_norm, v_gdn_conv_w,
                           v_gdn_a_log, v_gdn_dt_bias, v_gdn_out_norm, v_w_branch_hgrn, v_w_branch_gdn, v_w_out, v_ffn2_norm, v_ffn2_w_in,
                           v_ffn2_w_out, v_final_norm)))
    me = 4 * lax.axis_index("x") + 2 * lax.axis_index("y") + lax.axis_index("c")

    conv_shard = wts["gdn_conv_w"][0]
    shards = {n: _shard_rows(n, wts[n][0]).astype(BF16) for n in BIG}
    shards["gdn_conv_w"] = conv_shard.reshape(2, D_MODEL)
    links = _Links(shards, me)
    p = {"ffn1_norm": wts["ffn1_norm"], "mix_norm": wts["mix_norm"], "ffn2_norm": wts["ffn2_norm"], "final_norm": wts["final_norm"].reshape(1, D_MODEL),
         "lbl": wts["hgrn_lb_logits"], "hgrn_out_norm": wts["hgrn_out_norm"], "gdn_out_norm": wts["gdn_out_norm"],
         "alog": _pad_lanes(wts["gdn_a_log"]), "dtb": _pad_lanes(wts["gdn_dt_bias"])}

    loss, dx, g = _local_step(x[0], loss_target[0], p, links)

    small_parts = _exchange([(_pack_small(g, loss), "gather")], "gather_small")[0]
    landed = links.landed(small_parts)

    big = [{} for _ in range(4)]
    for n in BIG:
        res = _adam(landed[n], _shard_rows(n, wts[n][0]), _shard_rows(n, mom[n][0]), _shard_rows(n, var[n][0]), "adam_" + n)
        for kind in range(4):
            big[kind][n] = _shard_rows(n, res[kind])
    n_vec = SMALL_ROWS - 4 * CONV_K
    small_raw = _adam(small_parts[:, :n_vec], _pack_small_state(wts), _pack_small_state(mom), _pack_small_state(var), "adam_small")
    small = [_unpack_small(o) for o in small_raw]
    loss_total = small_raw[0][6, 4 * HEAD]
    conv_parts = small_parts[:, n_vec:].reshape(N_DEV, CONV_K, 4 * D_MODEL)
    width = 4 * D_MODEL // N_DEV
    conv_mine = lax.dynamic_slice_in_dim(conv_parts, me * width, width, axis=2)
    conv = _adam(conv_mine, conv_shard, mom["gdn_conv_w"][0], var["gdn_conv_w"][0], "adam_conv")

    outs = []
    for kind in range(4):
        for n in NAMES:
            if n in BIG:
                outs.append(big[kind][n][None])
            elif n == "gdn_conv_w":
                outs.append(conv[kind][None])
            else:
                outs.append(small[kind][n])
    return (loss_total, dx[None], *outs)
```

```python
import functools

import jax
import jax.numpy as jnp
from jax import lax
from jax.experimental import pallas as pl
from jax.experimental.pallas import tpu as pltpu

F32 = jnp.float32
BF16 = jnp.bfloat16
HIGHEST = lax.Precision.HIGHEST
MESH_IDS = pl.DeviceIdType.MESH

D_MODEL = 1024
D_FF = 2816
N_DEV = 8
EPS = 1e-6
HEAD = 128
HG_HEADS = 8
GDN_QK_HEADS = 8
GDN_V_HEADS = 16
GDN_CHUNK = 64
HG_CHUNK = 16
CONV_K = 4
IN_WIDTH = 12320
IN_MAIN = 12288
COL_HQ, COL_HF, COL_HI, COL_HG, COL_GQ, COL_GK, COL_GV, COL_GZ, COL_GATE_H, COL_GATE_G = 0, 8, 16, 24, 32, 40, 48, 64, 80, 88
VMEM_LIMIT = 56 * 1024 * 1024

ADAM_LR, ADAM_B1, ADAM_B2, ADAM_EPS, ADAM_WD, ADAM_STEP = 0.001, 0.9, 0.999, 1e-08, 0.01, 10

SDS = jax.ShapeDtypeStruct


def _params(n_axes):
    return pltpu.CompilerParams(dimension_semantics=("arbitrary",) * n_axes, vmem_limit_bytes=VMEM_LIMIT)


def _tile(n, candidates=(512, 384, 256, 128, 64, 32, 16, 8)):
    for c in candidates:
        if n % c == 0:
            return c
    return n


_DIMS = {"nn": ((1,), (0,)), "nt": ((1,), (1,)), "tn": ((0,), (0,))}


def _bdot_raw(a, b, dims):
    return lax.dot_general(a.astype(BF16), b.astype(BF16), (_DIMS[dims], ((), ())), preferred_element_type=F32)


@functools.partial(jax.custom_vjp, nondiff_argnums=(2,))
def _bdot(a, b, dims):
    return _bdot_raw(a, b, dims)


def _bdot_fwd(a, b, dims):
    return _bdot_raw(a, b, dims), (a, b)


def _bdot_bwd(dims, res, ct):
    a, b = res
    if dims == "nn":
        return _bdot_raw(ct, b, "nt"), _bdot_raw(a, ct, "tn")
    if dims == "nt":
        return _bdot_raw(ct, b, "nn"), _bdot_raw(ct, a, "tn")
    return _bdot_raw(b, ct, "nt"), _bdot_raw(a, ct, "nn")


_bdot.defvjp(_bdot_fwd, _bdot_bwd)


def _hdot_raw(a, b):
    return jnp.dot(a, b, precision=HIGHEST, preferred_element_type=F32)


MM_VMEM_BUDGET = 30 * 1024 * 1024


def _mm_tiles(m, n, k, a_bytes, b_bytes, o_bytes, r_bytes):
    tm = _tile(m, (1024, 512, 256, 128, 64, 32, 16, 8))
    tn = _tile(n, (1408, 1024, 512, 256, 128))
    tk = _tile(k, (2048, 1408, 1024, 512, 256, 128, 64, 32, 16, 8))

    def need(tm, tn, tk):
        return 2 * (tm * tk * a_bytes + tk * tn * b_bytes + tm * tn * (o_bytes + r_bytes)) + tm * tn * 4

    while need(tm, tn, tk) > MM_VMEM_BUDGET:
        if tk > 512 and tk % 256 == 0:
            tk //= 2
        elif tn > 512 and tn % 256 == 0:
            tn //= 2
        elif tm > 256:
            tm //= 2
        else:
            break
    return tm, tn, tk


def _mm(a, b, dims, out_dtype, name, res=None, alpha=1.0):
    if dims == "nn":
        (m, k), (k2, n) = a.shape, b.shape
    elif dims == "nt":
        (m, k), (n, k2) = a.shape, b.shape
    else:
        (k, m), (k2, n) = a.shape, b.shape
    assert k == k2, (a.shape, b.shape, dims)
    has_res = res is not None
    tm, tn, tk = _mm_tiles(m, n, k, a.dtype.itemsize, b.dtype.itemsize, jnp.dtype(out_dtype).itemsize, res.dtype.itemsize if has_res else 0)
    nk = k // tk
    a_spec = pl.BlockSpec((tk, tm), lambda i, j, kk: (kk, i)) if dims == "tn" else pl.BlockSpec((tm, tk), lambda i, j, kk: (i, kk))
    b_spec = pl.BlockSpec((tn, tk), lambda i, j, kk: (j, kk)) if dims == "nt" else pl.BlockSpec((tk, tn), lambda i, j, kk: (kk, j))
    o_spec = pl.BlockSpec((tm, tn), lambda i, j, kk: (i, j))

    def finish(acc, r_ref, o_ref):
        out = acc * alpha if alpha != 1.0 else acc
        if has_res:
            out = r_ref[...].astype(F32) + out
        o_ref[...] = out.astype(o_ref.dtype)

    def body(*refs):
        a_ref, b_ref = refs[:2]
        r_ref = refs[2] if has_res else None
        o_ref = refs[3] if has_res else refs[2]
        p = _bdot_raw(a_ref[...], b_ref[...], dims)
        if nk == 1:
            finish(p, r_ref, o_ref)
            return
        acc_ref = refs[-1]
        kk = pl.program_id(2)

        @pl.when(kk == 0)
        def _():
            acc_ref[...] = p

        @pl.when(kk > 0)
        def _():
            acc_ref[...] += p

        @pl.when(kk == nk - 1)
        def _():
            finish(acc_ref[...], r_ref, o_ref)

    args = (a, b, res) if has_res else (a, b)
    in_specs = [a_spec, b_spec] + ([o_spec] if has_res else [])
    return pl.pallas_call(
        body, grid=(m // tm, n // tn, nk), in_specs=in_specs, out_specs=o_spec, out_shape=SDS((m, n), out_dtype),
        scratch_shapes=[pltpu.VMEM((tm, tn), F32)] if nk > 1 else [], name=name, compiler_params=_params(3),
    )(*args)


def _tmap(fn, grid, ins, outs, name):
    n_in = len(ins)
    n_ax = len(grid)

    def body(*refs):
        vals = fn(*[r[...] for r in refs[:n_in]])
        if not isinstance(vals, (tuple, list)):
            vals = (vals,)
        first_inner = pl.program_id(n_ax - 1) == 0
        first_all = first_inner
        for ax in range(n_ax - 1):
            first_all = jnp.logical_and(first_all, pl.program_id(ax) == 0)

        def put(ref, val, acc):
            val = val.astype(ref.dtype)
            if acc is None:
                ref[...] = val
                return
            first = first_inner if acc == "inner" else first_all

            @pl.when(first)
            def _():
                ref[...] = val

            @pl.when(jnp.logical_not(first))
            def _():
                ref[...] += val

        for ref, val, o in zip(refs[n_in:], vals, outs):
            put(ref, val, o[4])

    return pl.pallas_call(
        body, grid=grid,
        in_specs=[pl.BlockSpec(bs, im) for _, bs, im in ins],
        out_specs=[pl.BlockSpec(o[2], o[3]) for o in outs],
        out_shape=[SDS(o[0], o[1]) for o in outs],
        name=name, compiler_params=_params(n_ax),
    )(*[a for a, _, _ in ins])


def _rows(width, tt, off=0):
    return (tt, width), (lambda j, i: (i, off + j))


def _rms(x, g):
    x = x.astype(F32)
    return x * lax.rsqrt(jnp.mean(x * x, axis=-1, keepdims=True) + EPS) * g


def _silu(x):
    return x * jax.nn.sigmoid(x)


def _softplus(x):
    return jnp.maximum(x, 0.0) + jnp.log1p(jnp.exp(-jnp.abs(x)))


def _rms_fwd(x, g, name):
    t, d = x.shape
    tt = _tile(t, (256, 128))
    return _tmap(_rms, (1, t // tt), [(x, *_rows(d, tt)), (g, (1, d), lambda j, i: (0, 0))],
                 [((t, d), BF16, *_rows(d, tt), None)], name)[0]


def _rms_bwd(x, g, dn, dres, name):
    t, d = x.shape
    tt = _tile(t, (256, 128))

    def fn(x, g, dn, dres):
        _, vjp = jax.vjp(_rms, x, g)
        dx, dg = vjp(dn.astype(F32))
        return dres + dx, dg

    return _tmap(fn, (1, t // tt),
                 [(x, *_rows(d, tt)), (g, (1, d), lambda j, i: (0, 0)), (dn, *_rows(d, tt)), (dres, *_rows(d, tt))],
                 [((t, d), F32, *_rows(d, tt), None), ((1, d), F32, (1, d), lambda j, i: (0, 0), "inner")], name)


def _swiglu(ab):
    return _silu(ab[:, :D_FF]) * ab[:, D_FF:]


def _swiglu_fwd(ab, name):
    t = ab.shape[0]
    tt = _tile(t, (128,))
    return _tmap(_swiglu, (1, t // tt), [(ab, *_rows(2 * D_FF, tt))], [((t, D_FF), BF16, *_rows(D_FF, tt), None)], name)[0]


def _swiglu_bwd(ab, ds, name):
    t = ab.shape[0]
    tt = _tile(t, (128,))

    def fn(ab, ds):
        a, b = ab[:, :D_FF], ab[:, D_FF:]
        _, vjp = jax.vjp(lambda a, b: _silu(a) * b, a, b)
        da, db = vjp(ds.astype(F32))
        return jnp.concatenate([da, db], axis=1)

    return _tmap(fn, (1, t // tt), [(ab, *_rows(2 * D_FF, tt)), (ds, *_rows(D_FF, tt))],
                 [((t, 2 * D_FF), BF16, *_rows(2 * D_FF, tt), None)], name)[0]


def _ffn_fwd(h, g, w_in_t, w_out, tag):
    n = _rms_fwd(h, g, tag + "_norm")
    ab = _mm(n, w_in_t, "nt", F32, tag + "_in")
    s = _swiglu_fwd(ab, tag + "_act")
    out = _mm(s, w_out, "nn", F32, tag + "_out", res=h, alpha=0.5)
    return out, (n, ab, s)


def _ffn_bwd(h, g, w_in_t, w_out, saved, dout, tag, links):
    n, ab, s = saved
    links.send({tag + "_w_out": _mm(s, dout, "tn", BF16, tag + "_dw_out", alpha=0.5)})
    ds = _mm(dout, w_out, "nt", F32, tag + "_ds", alpha=0.5)
    dab = _swiglu_bwd(ab, ds, tag + "_dact")
    sent = links.send({tag + "_w_in": _mm(dab, n, "tn", BF16, tag + "_dw_in")})
    dn = _mm(dab, w_in_t, "nn", F32, tag + "_dn")
    dh, dg = _rms_bwd(h, g + sent, dn, dout, tag + "_dnorm")
    return dh, dg, sent


def _chunk_sum_matrix(n, chunk, transpose=False):
    row = lax.broadcasted_iota(jnp.int32, (n, n), 0)
    col = lax.broadcasted_iota(jnp.int32, (n, n), 1)
    if transpose:
        row, col = col, row
    return jnp.where(jnp.logical_and(col <= row, row // chunk == col // chunk), 1.0, 0.0).astype(F32)


def _hgrn_gates(hq, hf, lbl):
    lb = jax.nn.sigmoid(lbl[0:1, :] - lbl[1:2, :])
    sg = jax.nn.sigmoid(hf)
    f = lb + (1.0 - lb) * sg
    q = _silu(hq) * HEAD ** -0.5
    k = (1.0 - lb) * (1.0 - sg)
    return q, k, jnp.log(f)


def _hgrn_prep_fwd(proj, lbl):
    t = proj.shape[0]
    tt, ft = _tile(t, (256, 128)), 512

    def fn(hq, hf, lbl):
        q, k, log_f = _hgrn_gates(hq, hf, lbl)
        return q, k, _hdot_raw(_chunk_sum_matrix(tt, HG_CHUNK), log_f)

    o = ((t, D_MODEL), F32, *_rows(ft, tt), None)
    return _tmap(fn, (D_MODEL // ft, t // tt),
                 [(proj, *_rows(ft, tt, COL_HQ * HEAD // ft)), (proj, *_rows(ft, tt, COL_HF * HEAD // ft)), (lbl, (2, ft), lambda j, i: (0, j))],
                 [o, o, o], "hgrn_prep")


def _hgrn_prep_bwd(proj, lbl, dq, dk, db):
    t = proj.shape[0]
    tt, ft = _tile(t, (256, 128)), 512

    def fn(hq, hf, lbl, dq, dk, db):
        dlog_f = _hdot_raw(_chunk_sum_matrix(tt, HG_CHUNK, transpose=True), db)
        _, vjp = jax.vjp(_hgrn_gates, hq, hf, lbl)
        return vjp((dq, dk, dlog_f))

    o = ((t, D_MODEL), BF16, *_rows(ft, tt), None)
    r = _rows(ft, tt)
    return _tmap(fn, (D_MODEL // ft, t // tt),
                 [(proj, *_rows(ft, tt, COL_HQ * HEAD // ft)), (proj, *_rows(ft, tt, COL_HF * HEAD // ft)), (lbl, (2, ft), lambda j, i: (0, j)),
                  (dq, *r), (dk, *r), (db, *r)],
                 [o, o, ((2, D_MODEL), F32, (2, ft), lambda j, i: (0, j), "inner")], "hgrn_prep_bwd")


def _hgrn_chunk(q, k, v, b, st):
    n = q.shape[0]
    srow = lax.broadcasted_iota(jnp.int32, (n, HEAD), 0)
    inter = _bdot(q * jnp.exp(b), st, "nt")
    rows = []
    for t in range(n):
        e = jnp.where(srow <= t, jnp.exp(jnp.minimum(b[t:t + 1, :] - b, 0.0)), 0.0)
        a = jnp.sum(q[t:t + 1, :] * k * e, axis=1, keepdims=True)
        rows.append(jnp.sum(a * v, axis=0, keepdims=True))
    o = inter + jnp.concatenate(rows, axis=0)
    bend = b[n - 1:n, :]
    st_new = st * jnp.exp(bend) + _bdot(v, k * jnp.exp(bend - b), "tn")
    return o, st_new


HG_GROUP = 4
HG_PER = GDN_CHUNK // HG_CHUNK


def _hgrn_rec_fwd(q, k, proj, b):
    t = q.shape[0]
    nc = t // GDN_CHUNK
    blk = (GDN_CHUNK, HG_GROUP * HEAD)
    im = lambda h, c: (c, h)

    def body(q_ref, k_ref, v_ref, b_ref, o_ref, hs_ref, st_ref):
        @pl.when(pl.program_id(1) == 0)
        def _():
            st_ref[...] = jnp.zeros_like(st_ref)

        for j in range(HG_PER):
            sl = pl.ds(HG_CHUNK * j, HG_CHUNK)
            for g in range(HG_GROUP):
                ln = _head_lanes(g)
                st = st_ref[g]
                hs_ref[g, j] = st
                o, st_new = _hgrn_chunk(q_ref[sl, ln], k_ref[sl, ln], v_ref[sl, ln], b_ref[sl, ln], st)
                o_ref[sl, ln] = o
                st_ref[g] = st_new

    return pl.pallas_call(
        body, grid=(HG_HEADS // HG_GROUP, nc),
        in_specs=[pl.BlockSpec(blk, im), pl.BlockSpec(blk, im), pl.BlockSpec(blk, lambda h, c: (c, COL_HI // HG_GROUP + h)), pl.BlockSpec(blk, im)],
        out_specs=[pl.BlockSpec(blk, im), pl.BlockSpec((HG_GROUP, HG_PER, HEAD, HEAD), lambda h, c: (h, c, 0, 0))],
        out_shape=[SDS((t, D_MODEL), F32), SDS((HG_HEADS, nc * HG_PER, HEAD, HEAD), F32)],
        scratch_shapes=[pltpu.VMEM((HG_GROUP, HEAD, HEAD), F32)], name="hgrn_rec", compiler_params=_params(2),
    )(q, k, proj, b)


def _hgrn_rec_bwd(q, k, proj, b, hs, do):
    t = q.shape[0]
    nc = t // GDN_CHUNK
    blk = (GDN_CHUNK, HG_GROUP * HEAD)
    im = lambda h, c: (nc - 1 - c, h)

    def body(q_ref, k_ref, v_ref, b_ref, hs_ref, do_ref, dq_ref, dk_ref, dv_ref, db_ref, dst_ref):
        @pl.when(pl.program_id(1) == 0)
        def _():
            dst_ref[...] = jnp.zeros_like(dst_ref)

        for j in reversed(range(HG_PER)):
            sl = pl.ds(HG_CHUNK * j, HG_CHUNK)
            for g in range(HG_GROUP):
                ln = _head_lanes(g)
                _, vjp = jax.vjp(_hgrn_chunk, q_ref[sl, ln], k_ref[sl, ln], v_ref[sl, ln], b_ref[sl, ln], hs_ref[g, j])
                dq, dk, dv, db, dst = vjp((do_ref[sl, ln], dst_ref[g]))
                dq_ref[sl, ln] = dq
                dk_ref[sl, ln] = dk
                dv_ref[sl, ln] = dv.astype(dv_ref.dtype)
                db_ref[sl, ln] = db
                dst_ref[g] = dst

    spec = pl.BlockSpec(blk, im)
    return pl.pallas_call(
        body, grid=(HG_HEADS // HG_GROUP, nc),
        in_specs=[spec, spec, pl.BlockSpec(blk, lambda h, c: (nc - 1 - c, COL_HI // HG_GROUP + h)), spec,
                  pl.BlockSpec((HG_GROUP, HG_PER, HEAD, HEAD), lambda h, c: (h, nc - 1 - c, 0, 0)), spec],
        out_specs=[spec, spec, spec, spec],
        out_shape=[SDS((t, D_MODEL), F32), SDS((t, D_MODEL), F32), SDS((t, D_MODEL), BF16), SDS((t, D_MODEL), F32)],
        scratch_shapes=[pltpu.VMEM((HG_GROUP, HEAD, HEAD), F32)], name="hgrn_rec_bwd", compiler_params=_params(2),
    )(q, k, proj, b, hs, do)


def _shift_down(x, d):
    if d == 0:
        return x
    row = lax.broadcasted_iota(jnp.int32, x.shape, 0)
    return jnp.where(row >= d, pltpu.roll(x, d, 0), 0.0)


def _shift_up(x, d):
    if d == 0:
        return x
    n = x.shape[0]
    row = lax.broadcasted_iota(jnp.int32, x.shape, 0)
    return jnp.where(row < n - d, pltpu.roll(x, n - d, 0), 0.0)


def _conv_fwd(proj, conv_w):
    t = proj.shape[0]
    width = 2 * D_MODEL + 2 * D_MODEL

    def body(x_ref, w_ref, c_ref):
        x, w = x_ref[...], w_ref[...]
        y = w[CONV_K - 1:CONV_K, :] * x
        for j in range(CONV_K - 1):
            y = y + w[j:j + 1, :] * _shift_down(x, CONV_K - 1 - j)
        c_ref[...] = _silu(y)

    return pl.pallas_call(
        body, grid=(width // HEAD,),
        in_specs=[pl.BlockSpec((t, HEAD), lambda j: (0, COL_GQ + j)), pl.BlockSpec((CONV_K, HEAD), lambda j: (0, j))],
        out_specs=pl.BlockSpec((t, HEAD), lambda j: (0, j)), out_shape=SDS((t, width), F32),
        name="gdn_conv", compiler_params=_params(1),
    )(proj, conv_w)


def _conv_bwd(proj, conv_w, dc_qk, dc_v):
    t = proj.shape[0]
    n_qk = dc_qk.shape[1] // HEAD
    width = dc_qk.shape[1] + dc_v.shape[1]

    def body(x_ref, w_ref, dqk_ref, dv_ref, dx_ref, dw_ref):
        x, w = x_ref[...], w_ref[...]
        xs = [_shift_down(x, CONV_K - 1 - j) for j in range(CONV_K)]
        y = w[0:1, :] * xs[0]
        for j in range(1, CONV_K):
            y = y + w[j:j + 1, :] * xs[j]
        sg = jax.nn.sigmoid(y)
        dc = jnp.where(pl.program_id(0) < n_qk, dqk_ref[...], dv_ref[...])
        dy = dc * (sg * (1.0 + y * (1.0 - sg)))
        dx = w[CONV_K - 1:CONV_K, :] * dy
        for j in range(CONV_K - 1):
            dx = dx + w[j:j + 1, :] * _shift_up(dy, CONV_K - 1 - j)
        dx_ref[...] = dx.astype(dx_ref.dtype)
        dw_ref[...] = jnp.concatenate([jnp.sum(xs[j] * dy, axis=0, keepdims=True) for j in range(CONV_K)], axis=0)

    return pl.pallas_call(
        body, grid=(width // HEAD,),
        in_specs=[pl.BlockSpec((t, HEAD), lambda j: (0, COL_GQ + j)), pl.BlockSpec((CONV_K, HEAD), lambda j: (0, j)),
                  pl.BlockSpec((t, HEAD), lambda j: (0, jnp.minimum(j, n_qk - 1))), pl.BlockSpec((t, HEAD), lambda j: (0, jnp.maximum(j - n_qk, 0)))],
        out_specs=[pl.BlockSpec((t, HEAD), lambda j: (0, j)), pl.BlockSpec((CONV_K, HEAD), lambda j: (0, j))],
        out_shape=[SDS((t, width), BF16), SDS((CONV_K, width), F32)],
        name="gdn_conv_bwd", compiler_params=_params(1),
    )(proj, conv_w, dc_qk, dc_v)


def _l2norm(x, scale):
    return x * lax.rsqrt(jnp.sum(x * x, axis=-1, keepdims=True) + EPS) * scale


def _head(a, h):
    return a[:, h * HEAD:(h + 1) * HEAD]


def _qk_scale(h):
    return HEAD ** -0.5 if h < GDN_QK_HEADS else 1.0


def _qk_norm_fwd(c):
    t = c.shape[0]
    tt = _tile(t, (256, 128))
    width = 2 * D_MODEL

    def fn(x):
        return jnp.concatenate([_l2norm(_head(x, h), _qk_scale(h)) for h in range(2 * GDN_QK_HEADS)], axis=1)

    return _tmap(fn, (1, t // tt), [(c, *_rows(width, tt))], [((t, width), F32, *_rows(width, tt), None)], "gdn_qk_norm")[0]


def _qk_norm_bwd(c, dq_rep, dk_rep):
    t = c.shape[0]
    tt = _tile(t, (256, 128))
    width = 2 * D_MODEL

    def fn(x, dq2, dk2):
        out = []
        for h in range(2 * GDN_QK_HEADS):
            d2, hh = (dq2, h) if h < GDN_QK_HEADS else (dk2, h - GDN_QK_HEADS)
            _, vjp = jax.vjp(lambda x: _l2norm(x, _qk_scale(h)), _head(x, h))
            out.append(vjp(_head(d2, 2 * hh) + _head(d2, 2 * hh + 1))[0])
        return jnp.concatenate(out, axis=1)

    r = _rows(width, tt)
    return _tmap(fn, (1, t // tt), [(c, *r), (dq_rep, *r), (dk_rep, *r)], [((t, width), F32, *r, None)], "gdn_qk_norm_bwd")[0]


def _gdn_gates(x, alog, dtb):
    return -jnp.exp(alog) * _softplus(x + dtb), jax.nn.sigmoid(x)


def _gates_fwd(pab, alog, dtb):
    t = pab.shape[0]
    tt = _tile(t, (256, 128))

    def fn(x, alog, dtb):
        g, beta = _gdn_gates(x, alog, dtb)
        lane = lax.broadcasted_iota(jnp.int32, g.shape, 1)
        return jnp.where(lane < GDN_V_HEADS, _hdot_raw(_chunk_sum_matrix(tt, GDN_CHUNK), g), beta)

    p = (alog, (1, HEAD), lambda j, i: (0, 0)), (dtb, (1, HEAD), lambda j, i: (0, 0))
    return _tmap(fn, (1, t // tt), [(pab, *_rows(HEAD, tt)), *p], [((t, HEAD), F32, *_rows(HEAD, tt), None)], "gdn_gates")[0]


def _gates_bwd(pab, alog, dtb, dout):
    t = pab.shape[0]
    tt = _tile(t, (256, 128))

    def fn(x, alog, dtb, dout):
        lane = lax.broadcasted_iota(jnp.int32, dout.shape, 1)
        dgam = jnp.where(lane < GDN_V_HEADS, dout, 0.0)
        dbeta = jnp.where(jnp.logical_and(lane >= GDN_V_HEADS, lane < 2 * GDN_V_HEADS), dout, 0.0)
        dg = _hdot_raw(_chunk_sum_matrix(tt, GDN_CHUNK, transpose=True), dgam)
        _, vjp = jax.vjp(_gdn_gates, x, alog, dtb)
        return vjp((dg, dbeta))

    p = (alog, (1, HEAD), lambda j, i: (0, 0)), (dtb, (1, HEAD), lambda j, i: (0, 0))
    acc = ((1, HEAD), F32, (1, HEAD), lambda j, i: (0, 0), "inner")
    return _tmap(fn, (1, t // tt), [(pab, *_rows(HEAD, tt)), *p, (dout, *_rows(HEAD, tt))],
                 [((t, HEAD), BF16, *_rows(HEAD, tt), None), acc, acc], "gdn_gates_bwd")


def _split_bf16(x):
    hi = x.astype(BF16)
    return hi, (x - hi.astype(F32)).astype(BF16)


def _dot3(a, b):
    (ah, al), (bh, bl) = a, b
    return _bdot_raw(ah, bh, "nn") + (_bdot_raw(ah, bl, "nn") + _bdot_raw(al, bh, "nn"))


def _each(fn, *lists):
    return tuple(fn(*xs) for xs in zip(*lists))


def _unit_lower_inverses_raw(a):
    n = a[0].shape[0]
    row = lax.broadcasted_iota(jnp.int32, (n, n), 0)
    col = lax.broadcasted_iota(jnp.int32, (n, n), 1)
    eye = jnp.where(row == col, 1.0, 0.0).astype(F32)
    p = _each(lambda a: eye - a, a)
    x = _each(_split_bf16, a)
    m = 2
    while m < 2 * n:
        x = _each(_split_bf16, _each(_dot3, x, x))
        p = _each(lambda p, x: p + _dot3(_split_bf16(p), x), p, x)
        m *= 2
    return p


@jax.custom_vjp
def _unit_lower_inverses(a, known):
    return _unit_lower_inverses_raw(a) if known is None else known


def _uli_fwd(a, known):
    inv = _unit_lower_inverses(a, known)
    return inv, (inv, known)


def _uli_bwd(res, ct):
    inv, known = res
    right = _each(lambda ct, inv: _bdot_raw(ct, inv, "nt"), ct, inv)
    da = _each(lambda inv, r: -_bdot_raw(inv, r, "tn"), inv, right)
    return da, (None if known is None else _each(jnp.zeros_like, known))


_unit_lower_inverses.defvjp(_uli_fwd, _uli_bwd)


def _gdn_chunks(q, k, v, beta, gam, gam_row, s, inv_known=None):
    n = q[0].shape[0]
    row = lax.broadcasted_iota(jnp.int32, (n, n), 0)
    col = lax.broadcasted_iota(jnp.int32, (n, n), 1)
    decay = _each(lambda gam, gam_row: jnp.where(row >= col, jnp.exp(jnp.minimum(gam - gam_row, 0.0)), 0.0), gam, gam_row)
    kb = _each(lambda k, beta: k * beta, k, beta)
    a = _each(lambda kb, k, decay: jnp.where(row > col, _bdot(kb, k, "nt") * decay, 0.0), kb, k, decay)
    inv = _unit_lower_inverses(a, inv_known)
    eg = _each(jnp.exp, gam)
    u = _each(lambda inv, v, beta: _bdot(inv, v * beta, "nn"), inv, v, beta)
    w = _each(lambda inv, kb, eg: _bdot(inv, kb * eg, "nn"), inv, kb, eg)
    qk = _each(lambda q, k, decay: _bdot(q, k, "nt") * decay, q, k, decay)
    v_new = _each(lambda u, w, s: u - _bdot(w, s, "nn"), u, w, s)
    o_state = _each(lambda q, eg, s: _bdot(q * eg, s, "nn"), q, eg, s)
    o = _each(lambda o_state, qk, v_new: o_state + _bdot(qk, v_new, "nn"), o_state, qk, v_new)
    gend = _each(lambda gam: gam[n - 1:n, :], gam)
    s_new = _each(lambda s, k, gam, gend, v_new: s * jnp.exp(gend) + _bdot(k * jnp.exp(gend - gam), v_new, "tn"), s, k, gam, gend, v_new)
    return o, s_new, inv


GDN_GROUP = 16


def _gdn_specs(nc, rev):
    cc = (lambda c: nc - 1 - c) if rev else (lambda c: c)
    grp = GDN_GROUP
    q = pl.BlockSpec((GDN_CHUNK, grp // 2 * HEAD), lambda h, c: (cc(c), h))
    k = pl.BlockSpec((GDN_CHUNK, grp // 2 * HEAD), lambda h, c: (cc(c), 2 * GDN_QK_HEADS // grp + h))
    v = pl.BlockSpec((GDN_CHUNK, grp * HEAD), lambda h, c: (cc(c), 2 * GDN_QK_HEADS // grp + h))
    o = pl.BlockSpec((GDN_CHUNK, grp * HEAD), lambda h, c: (cc(c), h))
    col = pl.BlockSpec((grp, None, GDN_CHUNK, 1), lambda h, c: (h, cc(c), 0, 0))
    rw = pl.BlockSpec((grp, None, 1, GDN_CHUNK), lambda h, c: (h, cc(c), 0, 0))
    st = pl.BlockSpec((grp, None, HEAD, HEAD), lambda h, c: (h, cc(c), 0, 0))
    inv = pl.BlockSpec((grp, None, GDN_CHUNK, GDN_CHUNK), lambda h, c: (h, cc(c), 0, 0))
    return q, k, v, o, col, rw, st, inv


def _head_lanes(g, per=1):
    return pl.ds((g // per) * HEAD, HEAD)


def _gdn_rec_fwd(qk, c, beta_col, gam_col, gam_row):
    t = qk.shape[0]
    nc = t // GDN_CHUNK
    q, k, v, o, col, rw, st, inv = _gdn_specs(nc, False)

    def body(q_ref, k_ref, v_ref, be_ref, g_ref, gr_ref, o_ref, ss_ref, inv_ref, s_ref):
        @pl.when(pl.program_id(1) == 0)
        def _():
            s_ref[...] = jnp.zeros_like(s_ref)

        heads = range(GDN_GROUP)
        s = tuple(s_ref[g] for g in heads)
        out, s_new, inv_c = _gdn_chunks(
            tuple(q_ref[:, _head_lanes(g, 2)] for g in heads), tuple(k_ref[:, _head_lanes(g, 2)] for g in heads),
            tuple(v_ref[:, _head_lanes(g)] for g in heads), tuple(be_ref[g] for g in heads), tuple(g_ref[g] for g in heads),
            tuple(gr_ref[g] for g in heads), s)
        for g in heads:
            ss_ref[g] = s[g]
            o_ref[:, _head_lanes(g)] = out[g]
            inv_ref[g] = inv_c[g]
            s_ref[g] = s_new[g]

    return pl.pallas_call(
        body, grid=(GDN_V_HEADS // GDN_GROUP, nc), in_specs=[q, k, v, col, col, rw], out_specs=[o, st, inv],
        out_shape=[SDS((t, 2 * D_MODEL), F32), SDS((GDN_V_HEADS, nc, HEAD, HEAD), F32), SDS((GDN_V_HEADS, nc, GDN_CHUNK, GDN_CHUNK), F32)],
        scratch_shapes=[pltpu.VMEM((GDN_GROUP, HEAD, HEAD), F32)], name="gdn_rec", compiler_params=_params(2),
    )(qk, qk, c, beta_col, gam_col, gam_row)


def _gdn_rec_bwd(qk, c, beta_col, gam_col, gam_row, ss, invs, do):
    t = qk.shape[0]
    nc = t // GDN_CHUNK
    q, k, v, o, col, rw, st, inv = _gdn_specs(nc, True)

    def body(q_ref, k_ref, v_ref, be_ref, g_ref, gr_ref, ss_ref, inv_ref, do_ref,
             dq_ref, dk_ref, dv_ref, dbe_ref, dg_ref, dgr_ref, ds_ref):
        @pl.when(pl.program_id(1) == 0)
        def _():
            ds_ref[...] = jnp.zeros_like(ds_ref)

        heads = range(GDN_GROUP)
        _, vjp = jax.vjp(
            _gdn_chunks,
            tuple(q_ref[:, _head_lanes(g, 2)] for g in heads), tuple(k_ref[:, _head_lanes(g, 2)] for g in heads),
            tuple(v_ref[:, _head_lanes(g)] for g in heads), tuple(be_ref[g] for g in heads), tuple(g_ref[g] for g in heads),
            tuple(gr_ref[g] for g in heads), tuple(ss_ref[g] for g in heads), tuple(inv_ref[g] for g in heads))
        no_inv_ct = tuple(jnp.zeros((GDN_CHUNK, GDN_CHUNK), F32) for g in heads)
        dq, dk, dv, dbe, dg, dgr, ds, _ = vjp((tuple(do_ref[:, _head_lanes(g)] for g in heads), tuple(ds_ref[g] for g in heads), no_inv_ct))
        for g in heads:
            dq_ref[:, _head_lanes(g)] = dq[g]
            dk_ref[:, _head_lanes(g)] = dk[g]
            dv_ref[:, _head_lanes(g)] = dv[g]
            dbe_ref[g] = dbe[g]
            dg_ref[g] = dg[g]
            dgr_ref[g] = dgr[g]
            ds_ref[g] = ds[g]

    wide = SDS((t, 2 * D_MODEL), F32)
    colshape = SDS((GDN_V_HEADS, nc, GDN_CHUNK, 1), F32)
    return pl.pallas_call(
        body, grid=(GDN_V_HEADS // GDN_GROUP, nc), in_specs=[q, k, v, col, col, rw, st, inv, o], out_specs=[o, o, o, col, col, rw],
        out_shape=[wide, wide, wide, colshape, colshape, SDS((GDN_V_HEADS, nc, 1, GDN_CHUNK), F32)],
        scratch_shapes=[pltpu.VMEM((GDN_GROUP, HEAD, HEAD), F32)], name="gdn_rec_bwd", compiler_params=_params(2),
    )(qk, qk, c, beta_col, gam_col, gam_row, ss, invs, do)


def _gated_norm(o, gate, w):
    return _rms(o, w) * _silu(gate)


def _post_fwd(o, proj, col_off, w, name):
    t, width = o.shape
    tt = _tile(t, (256, 128))

    def fn(o, gate, w):
        return jnp.concatenate([_gated_norm(_head(o, h), _head(gate, h), w) for h in range(width // HEAD)], axis=1)

    return _tmap(fn, (1, t // tt),
                 [(o, *_rows(width, tt)), (proj, *_rows(width, tt, col_off * HEAD // width)), (w, (1, HEAD), lambda j, i: (0, 0))],
                 [((t, width), BF16, *_rows(width, tt), None)], name)[0]


def _post_bwd(o, proj, col_off, w, dout, name):
    t, width = o.shape
    tt = _tile(t, (256, 128))

    def fn(o, gate, w, dout):
        do, dgate, dw = [], [], jnp.zeros((1, HEAD), F32)
        for h in range(width // HEAD):
            _, vjp = jax.vjp(_gated_norm, _head(o, h), _head(gate, h), w)
            a, b, c = vjp(_head(dout, h))
            do.append(a)
            dgate.append(b)
            dw = dw + c
        return jnp.concatenate(do, axis=1), jnp.concatenate(dgate, axis=1), dw

    r = _rows(width, tt)
    return _tmap(fn, (1, t // tt),
                 [(o, *r), (proj, *_rows(width, tt, col_off * HEAD // width)), (w, (1, HEAD), lambda j, i: (0, 0)), (dout, *r)],
                 [((t, width), F32, *r, None), ((t, width), BF16, *r, None), ((1, HEAD), F32, (1, HEAD), lambda j, i: (0, 0), "inner")], name)


def _merge(gate_h, gate_g, yh, yg):
    return jax.nn.sigmoid(gate_h) * yh + jax.nn.sigmoid(gate_g) * yg


def _merge_fwd(proj, yh, yg):
    t = yh.shape[0]
    tt, ft = _tile(t, (256, 128)), 512
    r = _rows(ft, tt)
    return _tmap(_merge, (D_MODEL // ft, t // tt),
                 [(proj, *_rows(ft, tt, COL_GATE_H * HEAD // ft)), (proj, *_rows(ft, tt, COL_GATE_G * HEAD // ft)), (yh, *r), (yg, *r)],
                 [((t, D_MODEL), BF16, *r, None)], "merge")[0]


def _merge_bwd(proj, yh, yg, dy):
    t = yh.shape[0]
    tt, ft = _tile(t, (256, 128)), 512
    r = _rows(ft, tt)

    def fn(gate_h, gate_g, yh, yg, dy):
        _, vjp = jax.vjp(_merge, gate_h, gate_g, yh, yg)
        return vjp(dy)

    o = ((t, D_MODEL), BF16, *r, None)
    return _tmap(fn, (D_MODEL // ft, t // tt),
                 [(proj, *_rows(ft, tt, COL_GATE_H * HEAD // ft)), (proj, *_rows(ft, tt, COL_GATE_G * HEAD // ft)), (yh, *r), (yg, *r), (dy, *r)],
                 [o, o, o, o], "merge_bwd")


def _loss_head(h, target, g):
    t, d = h.shape
    tt = _tile(t, (256, 128))

    def fn(h, target, g):
        def f(h, g):
            err = _rms(h, g) - target
            return 0.5 * jnp.sum(jnp.mean(err * err, axis=-1))

        loss, (dh, dg) = jax.value_and_grad(f, (0, 1))(h, g)
        return dh, dg, jnp.full((1, HEAD), loss, F32)

    return _tmap(fn, (1, t // tt), [(h, *_rows(d, tt)), (target, *_rows(d, tt)), (g, (1, d), lambda j, i: (0, 0))],
                 [((t, d), F32, *_rows(d, tt), None), ((1, d), F32, (1, d), lambda j, i: (0, 0), "inner"),
                  ((1, HEAD), F32, (1, HEAD), lambda j, i: (0, 0), "inner")], "loss_head")


def _heads_to_cols(a):
    t = a.shape[0]
    return a.T.reshape(GDN_V_HEADS, t // GDN_CHUNK, GDN_CHUNK, 1)


def _mixer_fwd(h, p, links):
    t = h.shape[0]
    nc = t // GDN_CHUNK
    u = _rms_fwd(h, p["mix_norm"], "mix_norm")
    w = {n: links.weight(n, h) for n in ("w_in_main_t", "w_in_ab_t", "conv_w")}
    proj = _mm(u, w["w_in_main_t"], "nt", F32, "mix_in")
    pab = _mm(u, w["w_in_ab_t"], "nt", F32, "mix_in_ab")
    qh, kh, bh = _hgrn_prep_fwd(proj, p["lbl"])
    oh, hs = _hgrn_rec_fwd(qh, kh, proj, bh)
    c = _conv_fwd(proj, w["conv_w"])
    qk = _qk_norm_fwd(c)
    gates = _gates_fwd(pab, p["alog"], p["dtb"])
    gam = gates[:, :GDN_V_HEADS]
    beta_col = _heads_to_cols(gates[:, GDN_V_HEADS:2 * GDN_V_HEADS])
    gam_col = _heads_to_cols(gam)
    gam_row = gam.T.reshape(GDN_V_HEADS, nc, 1, GDN_CHUNK)
    og, ss, invs = _gdn_rec_fwd(qk, c, beta_col, gam_col, gam_row)
    ohn = _post_fwd(oh, proj, COL_HG, p["hgrn_out_norm"], "hgrn_out")
    ogn = _post_fwd(og, proj, COL_GZ, p["gdn_out_norm"], "gdn_out")
    w.update({n: links.weight(n, ogn) for n in ("w_branch_hgrn", "w_branch_gdn", "w_out")})
    yh = _mm(ohn, w["w_branch_hgrn"], "nn", F32, "branch_hgrn")
    yg = _mm(ogn, w["w_branch_gdn"], "nn", F32, "branch_gdn")
    y = _merge_fwd(proj, yh, yg)
    out = _mm(y, w["w_out"], "nn", F32, "mix_out", res=h)
    saved = (w, u, proj, pab, qh, kh, bh, oh, hs, c, qk, beta_col, gam_col, gam_row, og, ss, invs, ohn, ogn, yh, yg, y)
    return out, saved


def _mixer_bwd(h, p, links, saved, dout):
    (w, u, proj, pab, qh, kh, bh, oh, hs, c, qk, beta_col, gam_col, gam_row, og, ss, invs, ohn, ogn, yh, yg, y) = saved
    t = h.shape[0]
    grads = {}
    dw_out = _mm(y, dout, "tn", BF16, "mix_out_dw")
    dy = _mm(dout, w["w_out"], "nt", F32, "mix_out_dx")
    dgate_h, dgate_g, dyh, dyg = _merge_bwd(proj, yh, yg, dy)
    dw_bh = _mm(ohn, dyh, "tn", BF16, "branch_hgrn_dw")
    dw_bg = _mm(ogn, dyg, "tn", BF16, "branch_gdn_dw")
    sent = links.send({"w_out": dw_out, "w_branch_hgrn": dw_bh, "w_branch_gdn": dw_bg})
    beta_col = beta_col + sent
    dohn = _mm(dyh, w["w_branch_hgrn"], "nt", F32, "branch_hgrn_dx")
    dogn = _mm(dyg, w["w_branch_gdn"], "nt", F32, "branch_gdn_dx")
    doh, dhg, grads["hgrn_out_norm"] = _post_bwd(oh, proj, COL_HG, p["hgrn_out_norm"], dohn, "hgrn_out_bwd")
    dog, dgz, grads["gdn_out_norm"] = _post_bwd(og, proj, COL_GZ, p["gdn_out_norm"], dogn, "gdn_out_bwd")
    dqh, dkh, dhi, dbh = _hgrn_rec_bwd(qh, kh, proj, bh, hs, doh)
    dhq, dhf, grads["lbl"] = _hgrn_prep_bwd(proj, p["lbl"], dqh, dkh, dbh)
    dqv, dkv, dcv, dbeta_col, dgam_col, dgam_row = _gdn_rec_bwd(qk, c, beta_col, gam_col, gam_row, ss, invs, dog)
    dcqk = _qk_norm_bwd(c, dqv, dkv)
    dxin, grads["conv_w"] = _conv_bwd(proj, w["conv_w"], dcqk, dcv)
    dgam = (dgam_col.reshape(GDN_V_HEADS, t) + dgam_row.reshape(GDN_V_HEADS, t)).T
    dbeta = dbeta_col.reshape(GDN_V_HEADS, t).T
    dgates = jnp.concatenate([dgam, dbeta, jnp.zeros((t, HEAD - 2 * GDN_V_HEADS), F32)], axis=1)
    dpab, grads["alog"], grads["dtb"] = _gates_bwd(pab, p["alog"], p["dtb"], dgates)
    dproj = jnp.concatenate([dhq, dhf, dhi, dhg, dxin, dgz, dgate_h, dgate_g], axis=1)
    dw_main_t = _mm(dproj, u, "tn", BF16, "mix_in_dw")
    dw_ab_t = _mm(dpab, u, "tn", BF16, "mix_in_ab_dw")
    sent = links.send({"w_in": _w_in_join(dw_main_t, dw_ab_t)})
    du = _mm(dproj, w["w_in_main_t"], "nn", F32, "mix_in_dx")
    du = _mm(dpab, w["w_in_ab_t"] + sent.astype(BF16), "nn", F32, "mix_in_ab_dx", res=du)
    dh, grads["mix_norm"] = _rms_bwd(h, p["mix_norm"], du, dout, "mix_norm_bwd")
    return dh, grads


def _local_step(x, target, p, links):
    w1 = {n: links.weight(n, x) for n in ("ffn1_w_in", "ffn1_w_out")}
    h1, s1 = _ffn_fwd(x, p["ffn1_norm"], w1["ffn1_w_in"], w1["ffn1_w_out"], "ffn1")
    h2, sm = _mixer_fwd(h1, p, links)
    w2 = {n: links.weight(n, h2) for n in ("ffn2_w_in", "ffn2_w_out")}
    h3, s2 = _ffn_fwd(h2, p["ffn2_norm"], w2["ffn2_w_in"], w2["ffn2_w_out"], "ffn2")
    dh3, dfinal, loss = _loss_head(h3, target, p["final_norm"])
    g = {"final_norm": dfinal}
    dh2, g["ffn2_norm"], _ = _ffn_bwd(h2, p["ffn2_norm"], w2["ffn2_w_in"], w2["ffn2_w_out"], s2, dh3, "ffn2", links)
    dh1, gm = _mixer_bwd(h1, p, links, sm, dh2)
    g.update(gm)
    dx, g["ffn1_norm"], _ = _ffn_bwd(x, p["ffn1_norm"], w1["ffn1_w_in"], w1["ffn1_w_out"], s1, dh1, "ffn1", links)
    return loss, dx, g


def _exchange(items, name):
    n = len(items)
    out_shape = [SDS((N_DEV,) + a.shape if mode == "gather" else a.shape, a.dtype) for a, mode in items]

    def body(*refs):
        in_refs, out_refs = refs[:n], refs[n:2 * n]
        send_sems, recv_sems, local_sems = refs[2 * n:]
        x, y, c = lax.axis_index("x"), lax.axis_index("y"), lax.axis_index("c")
        me = 4 * x + 2 * y + c
        local = []
        for i, (_, mode) in enumerate(items):
            src = in_refs[i] if mode == "gather" else in_refs[i].at[me]
            cp = pltpu.make_async_copy(src, out_refs[i].at[me], local_sems.at[i])
            cp.start()
            local.append(cp)
        remote = []
        for rel in range(1, N_DEV):
            px = 1 - x if rel & 4 else x
            py = 1 - y if rel & 2 else y
            pc = 1 - c if rel & 1 else c
            peer = 4 * px + 2 * py + pc
            for i, (_, mode) in enumerate(items):
                src = in_refs[i] if mode == "gather" else in_refs[i].at[peer]
                cp = pltpu.make_async_remote_copy(src_ref=src, dst_ref=out_refs[i].at[me], send_sem=send_sems.at[i, rel - 1],
                                                  recv_sem=recv_sems.at[i, rel - 1], device_id=(px, py, pc), device_id_type=MESH_IDS)
                cp.start()
                remote.append(cp)
        for cp in remote:
            cp.wait()
        for cp in local:
            cp.wait()

    anyspace = pl.BlockSpec(memory_space=pl.ANY)
    return pl.pallas_call(
        body, in_specs=[anyspace] * n, out_specs=[anyspace] * n, out_shape=out_shape,
        scratch_shapes=[pltpu.SemaphoreType.DMA((n, N_DEV - 1)), pltpu.SemaphoreType.DMA((n, N_DEV - 1)), pltpu.SemaphoreType.DMA((n,))],
        name=name, compiler_params=pltpu.CompilerParams(has_side_effects=True),
    )(*[a for a, _ in items])


HBM_SPEC = pl.BlockSpec(memory_space=pltpu.HBM)
SEM_SPEC = pl.BlockSpec(memory_space=pltpu.SEMAPHORE)
DATAFLOW = pltpu.SideEffectType.DATAFLOW_SIDE_EFFECTING


def _position():
    x, y, c = lax.axis_index("x"), lax.axis_index("y"), lax.axis_index("c")
    return x, y, c, 4 * x + 2 * y + c


def _relations(x, y, c):
    for rel in range(1, N_DEV):
        px = 1 - x if rel & 4 else x
        py = 1 - y if rel & 2 else y
        pc = 1 - c if rel & 1 else c
        yield rel, (px, py, pc), 4 * px + 2 * py + pc


def _sem_index(item, rel):
    return item * (N_DEV - 1) + rel - 1


def _landing(a, mode):
    return lax.empty((N_DEV,) + a.shape if mode == "gather" else a.shape, a.dtype)


def _copies_start(groups, name):
    flat = [item for grp in groups for item in grp]
    n, ng = len(flat), len(groups)
    lands = [_landing(a, mode) for a, mode in flat]

    def body(*refs):
        src_refs, land_refs, sems, token = refs[:n], refs[n:2 * n], refs[2 * n:2 * n + 2 * ng], refs[-1]
        x, y, c, me = _position()
        for rel, where, peer in _relations(x, y, c):
            k = 0
            for gi, grp in enumerate(groups):
                for li, (_, mode) in enumerate(grp):
                    src = src_refs[k] if mode == "gather" else src_refs[k].at[peer]
                    pltpu.make_async_remote_copy(src_ref=src, dst_ref=land_refs[k].at[me], send_sem=sems[2 * gi].at[_sem_index(li, rel)],
                                                 recv_sem=sems[2 * gi + 1].at[_sem_index(li, rel)], device_id=where, device_id_type=MESH_IDS).start()
                    k += 1
        token[...] = jnp.zeros_like(token)

    sem_shapes = [pltpu.SemaphoreType.DMA((len(grp) * (N_DEV - 1),)) for grp in groups for _ in range(2)]
    thru = [pltpu.HBM(a.shape, a.dtype) for a, _ in flat] + [pltpu.HBM(l.shape, l.dtype) for l in lands]
    outs = pl.pallas_call(
        body, name=name, out_shape=(*sem_shapes, *thru, SDS((8, HEAD), F32)),
        in_specs=[HBM_SPEC] * (2 * n), out_specs=(*[SEM_SPEC] * (2 * ng), *[HBM_SPEC] * (2 * n), pl.BlockSpec(memory_space=pltpu.VMEM)),
        input_output_aliases={i: 2 * ng + i for i in range(2 * n)}, compiler_params=pltpu.CompilerParams(has_side_effects=DATAFLOW),
    )(*[pltpu.with_memory_space_constraint(a, pltpu.HBM) for a, _ in flat], *[pltpu.with_memory_space_constraint(l, pltpu.HBM) for l in lands])
    sems, srcs, landed, token = outs[:2 * ng], outs[2 * ng:2 * ng + n], outs[2 * ng + n:2 * ng + 2 * n], outs[-1]
    result, k = [], 0
    for gi, grp in enumerate(groups):
        result.append((sems[2 * gi], sems[2 * gi + 1], srcs[k:k + len(grp)], landed[k:k + len(grp)]))
        k += len(grp)
    return result, token[0, 0]


def _copies_wait(started, modes, after, name):
    send_sems, recv_sems, srcs, lands = started
    n = len(srcs)

    def body(*refs):
        src_refs, land_refs, ssem, rsem, token = refs[:n], refs[n:2 * n], refs[2 * n], refs[2 * n + 1], refs[-1]
        x, y, c, _ = _position()
        for rel in range(1, N_DEV):
            for i, mode in enumerate(modes):
                src = src_refs[i] if mode == "gather" else src_refs[i].at[0]
                cp = pltpu.make_async_remote_copy(src_ref=src, dst_ref=land_refs[i].at[0], send_sem=ssem.at[_sem_index(i, rel)],
                                                  recv_sem=rsem.at[_sem_index(i, rel)], device_id=(x, y, c), device_id_type=MESH_IDS)
                cp.wait_send()
                cp.wait_recv()
        token[...] = jnp.zeros_like(token)

    outs = pl.pallas_call(
        body, name=name, out_shape=[pltpu.HBM(a.shape, a.dtype) for a in (*srcs, *lands)] + [SDS((8, HEAD), F32)],
        in_specs=[HBM_SPEC] * (2 * n) + [SEM_SPEC, SEM_SPEC, pl.BlockSpec(memory_space=pl.ANY)],
        out_specs=[HBM_SPEC] * (2 * n) + [pl.BlockSpec(memory_space=pltpu.VMEM)],
        input_output_aliases={i: i for i in range(2 * n)}, compiler_params=pltpu.CompilerParams(has_side_effects=DATAFLOW),
    )(*srcs, *lands, send_sems, recv_sems, after)
    return outs[:n], outs[n:2 * n], outs[-1][0, 0]


WEIGHT_GROUPS = (("ffn1_w_in", "ffn1_w_out", "gdn_conv_w"), ("w_in",), ("w_branch_hgrn", "w_branch_gdn", "w_out", "ffn2_w_in", "ffn2_w_out"))


class _Links:
    def __init__(self, shards, me):
        self.me = me
        self.shards = shards
        self.weights = {}
        self.sends = []
        self.gathers = {}
        self._start_gather(0, None)

    def _start_gather(self, gi, zero):
        if gi < len(WEIGHT_GROUPS):
            items = [(self.shards[n] if zero is None else self.shards[n] + zero.astype(self.shards[n].dtype), "gather") for n in WEIGHT_GROUPS[gi]]
            self.gathers[gi] = _copies_start([items], "gather_start_%d" % gi)[0][0]

    def weight(self, name, after):
        if name not in self.weights:
            gi = [i for i, grp in enumerate(WEIGHT_GROUPS) if {"w_in_main_t": "w_in", "w_in_ab_t": "w_in", "conv_w": "gdn_conv_w"}.get(name, name) in grp][0]
            assert gi in self.gathers, "weight groups are asked for in order"
            srcs, lands, zero = _copies_wait(self.gathers[gi], ["gather"] * len(WEIGHT_GROUPS[gi]), after, "gather_wait_%d" % gi)
            self._start_gather(gi + 1, zero)
            for n, src, land in zip(WEIGHT_GROUPS[gi], srcs, lands):
                full = lax.dynamic_update_index_in_dim(land, src, self.me, 0)
                if n == "gdn_conv_w":
                    self.weights["conv_w"] = full.reshape(N_DEV, CONV_K, 4 * D_MODEL // N_DEV).transpose(1, 0, 2).reshape(CONV_K, 4 * D_MODEL)
                elif n == "w_in":
                    self.weights["w_in_main_t"], self.weights["w_in_ab_t"] = _w_in_split(full.reshape(-1, D_MODEL))
                else:
                    self.weights[n] = full.reshape(-1, D_MODEL)
        return self.weights[name]

    def send(self, grads):
        names = list(grads)
        blocks = [grads[n].reshape(N_DEV, -1, D_MODEL) for n in names]
        started, token = _copies_start([[(b, "scatter") for b in blocks]], "send_" + names[0])
        self.sends.append((names, started[0]))
        return token

    def landed(self, after):
        out = {}
        for names, started in self.sends:
            srcs, lands, _ = _copies_wait(started, ["scatter"] * len(names), after, "landed_" + names[0])
            for n, src, land in zip(names, srcs, lands):
                out[n] = lax.dynamic_update_index_in_dim(land, lax.dynamic_index_in_dim(src, self.me, 0, keepdims=False), self.me, 0)
        return out


def _adam(parts, w, m, v, name):
    n_parts, r, c = parts.shape
    tc = c if c <= 512 else (256 if r > 1024 else 512)

    def body(p_ref, w_ref, m_ref, v_ref, g_ref, d_ref, mo_ref, vo_ref):
        g = p_ref[0].astype(F32)
        for i in range(1, n_parts):
            g = g + p_ref[i].astype(F32)
        m_new = ADAM_B1 * m_ref[...] + (1.0 - ADAM_B1) * g
        v_new = ADAM_B2 * v_ref[...] + (1.0 - ADAM_B2) * (g * g)
        m_hat = m_new / (1.0 - ADAM_B1 ** ADAM_STEP)
        v_hat = v_new / (1.0 - ADAM_B2 ** ADAM_STEP)
        g_ref[...] = g
        d_ref[...] = -ADAM_LR * (m_hat / (jnp.sqrt(v_hat) + ADAM_EPS) + ADAM_WD * w_ref[...])
        mo_ref[...] = m_new
        vo_ref[...] = v_new

    spec = pl.BlockSpec((r, tc), lambda j: (0, j))
    return pl.pallas_call(
        body, grid=(c // tc,), in_specs=[pl.BlockSpec((n_parts, r, tc), lambda j: (0, 0, j)), spec, spec, spec],
        out_specs=[spec] * 4, out_shape=[SDS((r, c), F32)] * 4, name=name, compiler_params=_params(1),
    )(parts, w, m, v)


BIG = ("ffn1_w_in", "ffn1_w_out", "w_in", "w_branch_hgrn", "w_branch_gdn", "w_out", "ffn2_w_in", "ffn2_w_out")


TRANSPOSED = ("ffn1_w_in", "w_in", "ffn2_w_in")


def _shard_rows(name, shard):
    return shard.T if name in TRANSPOSED else shard


SCALAR_ROWS = 8192
N_SCALAR = 2 * GDN_V_HEADS


def _w_in_split(w_in_t):
    main = jnp.concatenate([w_in_t[:SCALAR_ROWS], w_in_t[SCALAR_ROWS + N_SCALAR:]], axis=0)
    ab = jnp.pad(w_in_t[SCALAR_ROWS:SCALAR_ROWS + N_SCALAR], ((0, HEAD - N_SCALAR), (0, 0)))
    return main, ab


def _w_in_join(dmain_t, dab_t):
    return jnp.concatenate([dmain_t[:SCALAR_ROWS], dab_t[:N_SCALAR], dmain_t[SCALAR_ROWS:]], axis=0)


def _pad_lanes(a, width=HEAD):
    return jnp.pad(a, ((0, 0), (0, width - a.shape[1])))


SMALL_ROWS = 24


def _pack_small(g, loss):
    row6 = jnp.concatenate([g["hgrn_out_norm"], g["gdn_out_norm"], g["alog"], g["dtb"], loss,
                            jnp.zeros((1, D_MODEL - 5 * HEAD), F32)], axis=1)
    return jnp.concatenate([g["ffn1_norm"], g["mix_norm"], g["lbl"], g["ffn2_norm"], g["final_norm"], row6,
                            jnp.zeros((1, D_MODEL), F32), g["conv_w"].reshape(4 * CONV_K, D_MODEL)], axis=0)


def _pack_small_state(a):
    row6 = jnp.concatenate([a["hgrn_out_norm"], a["gdn_out_norm"], _pad_lanes(a["gdn_a_log"]), _pad_lanes(a["gdn_dt_bias"]),
                            jnp.zeros((1, D_MODEL - 4 * HEAD), F32)], axis=1)
    return jnp.concatenate([a["ffn1_norm"], a["mix_norm"], a["hgrn_lb_logits"], a["ffn2_norm"], a["final_norm"].reshape(1, D_MODEL),
                            row6, jnp.zeros((1, D_MODEL), F32)], axis=0)


def _unpack_small(a):
    return {"ffn1_norm": a[0:1], "mix_norm": a[1:2], "hgrn_lb_logits": a[2:4], "ffn2_norm": a[4:5], "final_norm": a[5],
            "hgrn_out_norm": a[6:7, :HEAD], "gdn_out_norm": a[6:7, HEAD:2 * HEAD],
            "gdn_a_log": a[6:7, 2 * HEAD:2 * HEAD + GDN_V_HEADS], "gdn_dt_bias": a[6:7, 3 * HEAD:3 * HEAD + GDN_V_HEADS]}


NAMES = ("ffn1_norm", "ffn1_w_in", "ffn1_w_out", "mix_norm", "w_in", "hgrn_lb_logits", "hgrn_out_norm", "gdn_conv_w", "gdn_a_log",
         "gdn_dt_bias", "gdn_out_norm", "w_branch_hgrn", "w_branch_gdn", "w_out", "ffn2_norm", "ffn2_w_in", "ffn2_w_out", "final_norm")


def kernel(x, ffn1_norm, ffn1_w_in, ffn1_w_out, mix_norm, w_in, hgrn_lb_logits, hgrn_out_norm, gdn_conv_w, gdn_a_log, gdn_dt_bias, gdn_out_norm, w_branch_hgrn, w_branch_gdn, w_out, ffn2_norm, ffn2_w_in, ffn2_w_out, final_norm, loss_target, m_ffn1_norm, m_ffn1_w_in, m_ffn1_w_out, m_mix_norm, m_w_in, m_hgrn_lb_logits, m_hgrn_out_norm, m_gdn_conv_w, m_gdn_a_log, m_gdn_dt_bias, m_gdn_out_norm, m_w_branch_hgrn, m_w_branch_gdn, m_w_out, m_ffn2_norm, m_ffn2_w_in, m_ffn2_w_out, m_final_norm, v_ffn1_norm, v_ffn1_w_in, v_ffn1_w_out, v_mix_norm, v_w_in, v_hgrn_lb_logits, v_hgrn_out_norm, v_gdn_conv_w, v_gdn_a_log, v_gdn_dt_bias, v_gdn_out_norm, v_w_branch_hgrn, v_w_branch_gdn, v_w_out, v_ffn2_norm, v_ffn2_w_in, v_ffn2_w_out, v_final_norm):
    wts = dict(zip(NAMES, (ffn1_norm, ffn1_w_in, ffn1_w_out, mix_norm, w_in, hgrn_lb_logits, hgrn_out_norm, gdn_conv_w, gdn_a_log,
                           gdn_dt_bias, gdn_out_norm, w_branch_hgrn, w_branch_gdn, w_out, ffn2_norm, ffn2_w_in, ffn2_w_out, final_norm)))
    mom = dict(zip(NAMES, (m_ffn1_norm, m_ffn1_w_in, m_ffn1_w_out, m_mix_norm, m_w_in, m_hgrn_lb_logits, m_hgrn_out_norm, m_gdn_conv_w,
                           m_gdn_a_log, m_gdn_dt_bias, m_gdn_out_norm, m_w_branch_hgrn, m_w_branch_gdn, m_w_out, m_ffn2_norm, m_ffn2_w_in,
                           m_ffn2_w_out, m_final_norm)))
    var = dict(zip(NAMES, (v_ffn1_norm, v_ffn1_w_in, v_ffn1_w_out, v_mix_norm, v_w_in, v_hgrn_lb_logits, v_hgrn_out_norm, v_gdn_conv_w,
                           v_gdn_a_log, v_gdn_dt_bias, v_gdn_out_norm, v_w_branch_hgrn, v_w_branch_gdn, v_w_out, v_ffn2_norm, v_ffn2_w_in,
                           v_ffn2_w_out, v_final_norm)))
    me = 4 * lax.axis_index("x") + 2 * lax.axis_index("y") + lax.axis_index("c")

    conv_shard = wts["gdn_conv_w"][0]
    shards = {n: _shard_rows(n, wts[n][0]).astype(BF16) for n in BIG}
    shards["gdn_conv_w"] = conv_shard.reshape(2, D_MODEL)
    links = _Links(shards, me)
    p = {"ffn1_norm": wts["ffn1_norm"], "mix_norm": wts["mix_norm"], "ffn2_norm": wts["ffn2_norm"], "final_norm": wts["final_norm"].reshape(1, D_MODEL),
         "lbl": wts["hgrn_lb_logits"], "hgrn_out_norm": wts["hgrn_out_norm"], "gdn_out_norm": wts["gdn_out_norm"],
         "alog": _pad_lanes(wts["gdn_a_log"]), "dtb": _pad_lanes(wts["gdn_dt_bias"])}

    loss, dx, g = _local_step(x[0], loss_target[0], p, links)

    small_parts = _exchange([(_pack_small(g, loss), "gather")], "gather_small")[0]
    landed = links.landed(small_parts)

    big = [{} for _ in range(4)]
    for n in BIG:
        res = _adam(landed[n], _shard_rows(n, wts[n][0]), _shard_rows(n, mom[n][0]), _shard_rows(n, var[n][0]), "adam_" + n)
        for kind in range(4):
            big[kind][n] = _shard_rows(n, res[kind])
    n_vec = SMALL_ROWS - 4 * CONV_K
    small_raw = _adam(small_parts[:, :n_vec], _pack_small_state(wts), _pack_small_state(mom), _pack_small_state(var), "adam_small")
    small = [_unpack_small(o) for o in small_raw]
    loss_total = small_raw[0][6, 4 * HEAD]
    conv_parts = small_parts[:, n_vec:].reshape(N_DEV, CONV_K, 4 * D_MODEL)
    width = 4 * D_MODEL // N_DEV
    conv_mine = lax.dynamic_slice_in_dim(conv_parts, me * width, width, axis=2)
    conv = _adam(conv_mine, conv_shard, mom["gdn_conv_w"][0], var["gdn_conv_w"][0], "adam_conv")

    outs = []
    for kind in range(4):
        for n in NAMES:
            if n in BIG:
                outs.append(big[kind][n][None])
            elif n == "gdn_conv_w":
                outs.append(conv[kind][None])
            else:
                outs.append(small[kind][n])
    return (loss_total, dx[None], *outs)
```

```python
import functools

import jax
import jax.numpy as jnp
from jax import lax
from jax.experimental import pallas as pl
from jax.experimental.pallas import tpu as pltpu

F32 = jnp.float32
BF16 = jnp.bfloat16
HIGHEST = lax.Precision.HIGHEST
MESH_IDS = pl.DeviceIdType.MESH

D_MODEL = 1024
D_FF = 2816
N_DEV = 8
EPS = 1e-6
HEAD = 128
HG_HEADS = 8
GDN_QK_HEADS = 8
GDN_V_HEADS = 16
GDN_CHUNK = 64
HG_CHUNK = 16
CONV_K = 4
IN_WIDTH = 12320
IN_MAIN = 12288
COL_HQ, COL_HF, COL_HI, COL_HG, COL_GQ, COL_GK, COL_GV, COL_GZ, COL_GATE_H, COL_GATE_G = 0, 8, 16, 24, 32, 40, 48, 64, 80, 88
VMEM_LIMIT = 56 * 1024 * 1024

ADAM_LR, ADAM_B1, ADAM_B2, ADAM_EPS, ADAM_WD, ADAM_STEP = 0.001, 0.9, 0.999, 1e-08, 0.01, 10

SDS = jax.ShapeDtypeStruct


def _params(n_axes):
    return pltpu.CompilerParams(dimension_semantics=("arbitrary",) * n_axes, vmem_limit_bytes=VMEM_LIMIT)


def _tile(n, candidates=(512, 384, 256, 128, 64, 32, 16, 8)):
    for c in candidates:
        if n % c == 0:
            return c
    return n


_DIMS = {"nn": ((1,), (0,)), "nt": ((1,), (1,)), "tn": ((0,), (0,))}


def _bdot_raw(a, b, dims):
    return lax.dot_general(a.astype(BF16), b.astype(BF16), (_DIMS[dims], ((), ())), preferred_element_type=F32)


@functools.partial(jax.custom_vjp, nondiff_argnums=(2,))
def _bdot(a, b, dims):
    return _bdot_raw(a, b, dims)


def _bdot_fwd(a, b, dims):
    return _bdot_raw(a, b, dims), (a, b)


def _bdot_bwd(dims, res, ct):
    a, b = res
    if dims == "nn":
        return _bdot_raw(ct, b, "nt"), _bdot_raw(a, ct, "tn")
    if dims == "nt":
        return _bdot_raw(ct, b, "nn"), _bdot_raw(ct, a, "tn")
    return _bdot_raw(b, ct, "nt"), _bdot_raw(a, ct, "nn")


_bdot.defvjp(_bdot_fwd, _bdot_bwd)


def _hdot_raw(a, b):
    return jnp.dot(a, b, precision=HIGHEST, preferred_element_type=F32)


MM_VMEM_BUDGET = 30 * 1024 * 1024
TOKEN = (8, HEAD)


def _mm_tiles(m, n, k, a_bytes, b_bytes, o_bytes, r_bytes):
    tm = _tile(m, (1024, 512, 256, 128, 64, 32, 16, 8))
    tn = _tile(n, (1408, 1024, 512, 256, 128))
    tk = _tile(k, (2048, 1408, 1024, 512, 256, 128, 64, 32, 16, 8))

    def need(tm, tn, tk):
        return 2 * (tm * tk * a_bytes + tk * tn * b_bytes + tm * tn * (o_bytes + r_bytes)) + tm * tn * 4

    while need(tm, tn, tk) > MM_VMEM_BUDGET:
        if tk > 512 and tk % 256 == 0:
            tk //= 2
        elif tn > 512 and tn % 256 == 0:
            tn //= 2
        elif tm > 256:
            tm //= 2
        else:
            break
    return tm, tn, tk


def _mm(a, b, dims, out_dtype, name, res=None, alpha=1.0, after=None):
    if dims == "nn":
        (m, k), (k2, n) = a.shape, b.shape
    elif dims == "nt":
        (m, k), (n, k2) = a.shape, b.shape
    else:
        (k, m), (k2, n) = a.shape, b.shape
    assert k == k2, (a.shape, b.shape, dims)
    has_res = res is not None
    tm, tn, tk = _mm_tiles(m, n, k, a.dtype.itemsize, b.dtype.itemsize, jnp.dtype(out_dtype).itemsize, res.dtype.itemsize if has_res else 0)
    nk = k // tk
    a_spec = pl.BlockSpec((tk, tm), lambda i, j, kk: (kk, i)) if dims == "tn" else pl.BlockSpec((tm, tk), lambda i, j, kk: (i, kk))
    b_spec = pl.BlockSpec((tn, tk), lambda i, j, kk: (j, kk)) if dims == "nt" else pl.BlockSpec((tk, tn), lambda i, j, kk: (kk, j))
    o_spec = pl.BlockSpec((tm, tn), lambda i, j, kk: (i, j))

    def finish(acc, r_ref, o_ref):
        out = acc * alpha if alpha != 1.0 else acc
        if has_res:
            out = r_ref[...].astype(F32) + out
        o_ref[...] = out.astype(o_ref.dtype)

    n_in = 2 + has_res + (after is not None)

    def body(*refs):
        a_ref, b_ref = refs[:2]
        r_ref = refs[2] if has_res else None
        o_ref = refs[n_in]
        p = _bdot_raw(a_ref[...], b_ref[...], dims)
        if nk == 1:
            finish(p, r_ref, o_ref)
            return
        acc_ref = refs[-1]
        kk = pl.program_id(2)

        @pl.when(kk == 0)
        def _():
            acc_ref[...] = p

        @pl.when(kk > 0)
        def _():
            acc_ref[...] += p

        @pl.when(kk == nk - 1)
        def _():
            finish(acc_ref[...], r_ref, o_ref)

    args = (a, b) + ((res,) if has_res else ()) + ((after,) if after is not None else ())
    in_specs = [a_spec, b_spec] + ([o_spec] if has_res else []) + ([pl.BlockSpec(TOKEN, lambda i, j, kk: (0, 0))] if after is not None else [])
    return pl.pallas_call(
        body, grid=(m // tm, n // tn, nk), in_specs=in_specs, out_specs=o_spec, out_shape=SDS((m, n), out_dtype),
        scratch_shapes=[pltpu.VMEM((tm, tn), F32)] if nk > 1 else [], name=name, compiler_params=_params(3),
    )(*args)


def _tmap(fn, grid, ins, outs, name):
    n_in = len(ins)
    n_ax = len(grid)

    def body(*refs):
        vals = fn(*[r[...] for r in refs[:n_in]])
        if not isinstance(vals, (tuple, list)):
            vals = (vals,)
        first_inner = pl.program_id(n_ax - 1) == 0
        first_all = first_inner
        for ax in range(n_ax - 1):
            first_all = jnp.logical_and(first_all, pl.program_id(ax) == 0)

        def put(ref, val, acc):
            val = val.astype(ref.dtype)
            if acc is None:
                ref[...] = val
                return
            first = first_inner if acc == "inner" else first_all

            @pl.when(first)
            def _():
                ref[...] = val

            @pl.when(jnp.logical_not(first))
            def _():
                ref[...] += val

        for ref, val, o in zip(refs[n_in:], vals, outs):
            put(ref, val, o[4])

    return pl.pallas_call(
        body, grid=grid,
        in_specs=[pl.BlockSpec(bs, im) for _, bs, im in ins],
        out_specs=[pl.BlockSpec(o[2], o[3]) for o in outs],
        out_shape=[SDS(o[0], o[1]) for o in outs],
        name=name, compiler_params=_params(n_ax),
    )(*[a for a, _, _ in ins])


def _rows(width, tt, off=0):
    return (tt, width), (lambda j, i: (i, off + j))


def _rms(x, g):
    x = x.astype(F32)
    return x * lax.rsqrt(jnp.mean(x * x, axis=-1, keepdims=True) + EPS) * g


def _silu(x):
    return x * jax.nn.sigmoid(x)


def _softplus(x):
    return jnp.maximum(x, 0.0) + jnp.log1p(jnp.exp(-jnp.abs(x)))


def _rms_fwd(x, g, name):
    t, d = x.shape
    tt = _tile(t, (256, 128))
    return _tmap(_rms, (1, t // tt), [(x, *_rows(d, tt)), (g, (1, d), lambda j, i: (0, 0))],
                 [((t, d), BF16, *_rows(d, tt), None)], name)[0]


def _rms_bwd(x, g, dn, dres, name):
    t, d = x.shape
    tt = _tile(t, (256, 128))

    def fn(x, g, dn, dres):
        _, vjp = jax.vjp(_rms, x, g)
        dx, dg = vjp(dn.astype(F32))
        return dres + dx, dg

    return _tmap(fn, (1, t // tt),
                 [(x, *_rows(d, tt)), (g, (1, d), lambda j, i: (0, 0)), (dn, *_rows(d, tt)), (dres, *_rows(d, tt))],
                 [((t, d), F32, *_rows(d, tt), None), ((1, d), F32, (1, d), lambda j, i: (0, 0), "inner")], name)


def _swiglu(ab):
    return _silu(ab[:, :D_FF]) * ab[:, D_FF:]


def _swiglu_fwd(ab, name):
    t = ab.shape[0]
    tt = _tile(t, (128,))
    return _tmap(_swiglu, (1, t // tt), [(ab, *_rows(2 * D_FF, tt))], [((t, D_FF), BF16, *_rows(D_FF, tt), None)], name)[0]


def _swiglu_bwd(ab, ds, name):
    t = ab.shape[0]
    tt = _tile(t, (128,))

    def fn(ab, ds):
        a, b = ab[:, :D_FF], ab[:, D_FF:]
        _, vjp = jax.vjp(lambda a, b: _silu(a) * b, a, b)
        da, db = vjp(ds.astype(F32))
        return jnp.concatenate([da, db], axis=1)

    return _tmap(fn, (1, t // tt), [(ab, *_rows(2 * D_FF, tt)), (ds, *_rows(D_FF, tt))],
                 [((t, 2 * D_FF), BF16, *_rows(2 * D_FF, tt), None)], name)[0]


def _ffn_fwd(h, g, w_in_t, w_out, tag, after):
    n = _rms_fwd(h, g, tag + "_norm")
    ab = _mm(n, w_in_t, "nt", F32, tag + "_in", after=after)
    s = _swiglu_fwd(ab, tag + "_act")
    out = _mm(s, w_out, "nn", F32, tag + "_out", res=h, alpha=0.5)
    return out, (n, ab, s)


def _ffn_bwd(h, g, w_in_t, w_out, saved, dout, tag, links):
    n, ab, s = saved
    sent = links.send({tag + "_w_out": _mm(s, dout, "tn", BF16, tag + "_dw_out", alpha=0.5)})
    ds = _mm(dout, w_out, "nt", F32, tag + "_ds", alpha=0.5, after=sent)
    dab = _swiglu_bwd(ab, ds, tag + "_dact")
    sent = links.send({tag + "_w_in": _mm(dab, n, "tn", BF16, tag + "_dw_in")})
    dn = _mm(dab, w_in_t, "nn", F32, tag + "_dn", after=sent)
    return _rms_bwd(h, g, dn, dout, tag + "_dnorm")


def _chunk_sum_matrix(n, chunk, transpose=False):
    row = lax.broadcasted_iota(jnp.int32, (n, n), 0)
    col = lax.broadcasted_iota(jnp.int32, (n, n), 1)
    if transpose:
        row, col = col, row
    return jnp.where(jnp.logical_and(col <= row, row // chunk == col // chunk), 1.0, 0.0).astype(F32)


def _hgrn_gates(hq, hf, lbl):
    lb = jax.nn.sigmoid(lbl[0:1, :] - lbl[1:2, :])
    sg = jax.nn.sigmoid(hf)
    f = lb + (1.0 - lb) * sg
    q = _silu(hq) * HEAD ** -0.5
    k = (1.0 - lb) * (1.0 - sg)
    return q, k, jnp.log(f)


def _hgrn_prep_fwd(proj, lbl):
    t = proj.shape[0]
    tt, ft = _tile(t, (256, 128)), 512

    def fn(hq, hf, lbl):
        q, k, log_f = _hgrn_gates(hq, hf, lbl)
        return q, k, _hdot_raw(_chunk_sum_matrix(tt, HG_CHUNK), log_f)

    o = ((t, D_MODEL), F32, *_rows(ft, tt), None)
    return _tmap(fn, (D_MODEL // ft, t // tt),
                 [(proj, *_rows(ft, tt, COL_HQ * HEAD // ft)), (proj, *_rows(ft, tt, COL_HF * HEAD // ft)), (lbl, (2, ft), lambda j, i: (0, j))],
                 [o, o, o], "hgrn_prep")


def _hgrn_prep_bwd(proj, lbl, dq, dk, db):
    t = proj.shape[0]
    tt, ft = _tile(t, (256, 128)), 512

    def fn(hq, hf, lbl, dq, dk, db):
        dlog_f = _hdot_raw(_chunk_sum_matrix(tt, HG_CHUNK, transpose=True), db)
        _, vjp = jax.vjp(_hgrn_gates, hq, hf, lbl)
        return vjp((dq, dk, dlog_f))

    o = ((t, D_MODEL), BF16, *_rows(ft, tt), None)
    r = _rows(ft, tt)
    return _tmap(fn, (D_MODEL // ft, t // tt),
                 [(proj, *_rows(ft, tt, COL_HQ * HEAD // ft)), (proj, *_rows(ft, tt, COL_HF * HEAD // ft)), (lbl, (2, ft), lambda j, i: (0, j)),
                  (dq, *r), (dk, *r), (db, *r)],
                 [o, o, ((2, D_MODEL), F32, (2, ft), lambda j, i: (0, j), "inner")], "hgrn_prep_bwd")


def _hgrn_chunk(q, k, v, b, st):
    n = q.shape[0]
    srow = lax.broadcasted_iota(jnp.int32, (n, HEAD), 0)
    inter = _bdot(q * jnp.exp(b), st, "nt")
    rows = []
    for t in range(n):
        e = jnp.where(srow <= t, jnp.exp(jnp.minimum(b[t:t + 1, :] - b, 0.0)), 0.0)
        a = jnp.sum(q[t:t + 1, :] * k * e, axis=1, keepdims=True)
        rows.append(jnp.sum(a * v, axis=0, keepdims=True))
    o = inter + jnp.concatenate(rows, axis=0)
    bend = b[n - 1:n, :]
    st_new = st * jnp.exp(bend) + _bdot(v, k * jnp.exp(bend - b), "tn")
    return o, st_new


HG_GROUP = 4
HG_PER = GDN_CHUNK // HG_CHUNK


def _hgrn_rec_fwd(q, k, proj, b):
    t = q.shape[0]
    nc = t // GDN_CHUNK
    blk = (GDN_CHUNK, HG_GROUP * HEAD)
    im = lambda h, c: (c, h)

    def body(q_ref, k_ref, v_ref, b_ref, o_ref, hs_ref, st_ref):
        @pl.when(pl.program_id(1) == 0)
        def _():
            st_ref[...] = jnp.zeros_like(st_ref)

        for j in range(HG_PER):
            sl = pl.ds(HG_CHUNK * j, HG_CHUNK)
            for g in range(HG_GROUP):
                ln = _head_lanes(g)
                st = st_ref[g]
                hs_ref[g, j] = st
                o, st_new = _hgrn_chunk(q_ref[sl, ln], k_ref[sl, ln], v_ref[sl, ln], b_ref[sl, ln], st)
                o_ref[sl, ln] = o
                st_ref[g] = st_new

    return pl.pallas_call(
        body, grid=(HG_HEADS // HG_GROUP, nc),
        in_specs=[pl.BlockSpec(blk, im), pl.BlockSpec(blk, im), pl.BlockSpec(blk, lambda h, c: (c, COL_HI // HG_GROUP + h)), pl.BlockSpec(blk, im)],
        out_specs=[pl.BlockSpec(blk, im), pl.BlockSpec((HG_GROUP, HG_PER, HEAD, HEAD), lambda h, c: (h, c, 0, 0))],
        out_shape=[SDS((t, D_MODEL), F32), SDS((HG_HEADS, nc * HG_PER, HEAD, HEAD), F32)],
        scratch_shapes=[pltpu.VMEM((HG_GROUP, HEAD, HEAD), F32)], name="hgrn_rec", compiler_params=_params(2),
    )(q, k, proj, b)


def _hgrn_rec_bwd(q, k, proj, b, hs, do):
    t = q.shape[0]
    nc = t // GDN_CHUNK
    blk = (GDN_CHUNK, HG_GROUP * HEAD)
    im = lambda h, c: (nc - 1 - c, h)

    def body(q_ref, k_ref, v_ref, b_ref, hs_ref, do_ref, dq_ref, dk_ref, dv_ref, db_ref, dst_ref):
        @pl.when(pl.program_id(1) == 0)
        def _():
            dst_ref[...] = jnp.zeros_like(dst_ref)

        for j in reversed(range(HG_PER)):
            sl = pl.ds(HG_CHUNK * j, HG_CHUNK)
            for g in range(HG_GROUP):
                ln = _head_lanes(g)
                _, vjp = jax.vjp(_hgrn_chunk, q_ref[sl, ln], k_ref[sl, ln], v_ref[sl, ln], b_ref[sl, ln], hs_ref[g, j])
                dq, dk, dv, db, dst = vjp((do_ref[sl, ln], dst_ref[g]))
                dq_ref[sl, ln] = dq
                dk_ref[sl, ln] = dk
                dv_ref[sl, ln] = dv.astype(dv_ref.dtype)
                db_ref[sl, ln] = db
                dst_ref[g] = dst

    spec = pl.BlockSpec(blk, im)
    return pl.pallas_call(
        body, grid=(HG_HEADS // HG_GROUP, nc),
        in_specs=[spec, spec, pl.BlockSpec(blk, lambda h, c: (nc - 1 - c, COL_HI // HG_GROUP + h)), spec,
                  pl.BlockSpec((HG_GROUP, HG_PER, HEAD, HEAD), lambda h, c: (h, nc - 1 - c, 0, 0)), spec],
        out_specs=[spec, spec, spec, spec],
        out_shape=[SDS((t, D_MODEL), F32), SDS((t, D_MODEL), F32), SDS((t, D_MODEL), BF16), SDS((t, D_MODEL), F32)],
        scratch_shapes=[pltpu.VMEM((HG_GROUP, HEAD, HEAD), F32)], name="hgrn_rec_bwd", compiler_params=_params(2),
    )(q, k, proj, b, hs, do)


def _shift_down(x, d):
    if d == 0:
        return x
    row = lax.broadcasted_iota(jnp.int32, x.shape, 0)
    return jnp.where(row >= d, pltpu.roll(x, d, 0), 0.0)


def _shift_up(x, d):
    if d == 0:
        return x
    n = x.shape[0]
    row = lax.broadcasted_iota(jnp.int32, x.shape, 0)
    return jnp.where(row < n - d, pltpu.roll(x, n - d, 0), 0.0)


def _conv_fwd(proj, conv_w):
    t = proj.shape[0]
    width = 2 * D_MODEL + 2 * D_MODEL

    def body(x_ref, w_ref, c_ref):
        x, w = x_ref[...], w_ref[...]
        y = w[CONV_K - 1:CONV_K, :] * x
        for j in range(CONV_K - 1):
            y = y + w[j:j + 1, :] * _shift_down(x, CONV_K - 1 - j)
        c_ref[...] = _silu(y)

    return pl.pallas_call(
        body, grid=(width // HEAD,),
        in_specs=[pl.BlockSpec((t, HEAD), lambda j: (0, COL_GQ + j)), pl.BlockSpec((CONV_K, HEAD), lambda j: (0, j))],
        out_specs=pl.BlockSpec((t, HEAD), lambda j: (0, j)), out_shape=SDS((t, width), F32),
        name="gdn_conv", compiler_params=_params(1),
    )(proj, conv_w)


def _conv_bwd(proj, conv_w, dc_qk, dc_v):
    t = proj.shape[0]
    n_qk = dc_qk.shape[1] // HEAD
    width = dc_qk.shape[1] + dc_v.shape[1]

    def body(x_ref, w_ref, dqk_ref, dv_ref, dx_ref, dw_ref):
        x, w = x_ref[...], w_ref[...]
        xs = [_shift_down(x, CONV_K - 1 - j) for j in range(CONV_K)]
        y = w[0:1, :] * xs[0]
        for j in range(1, CONV_K):
            y = y + w[j:j + 1, :] * xs[j]
        sg = jax.nn.sigmoid(y)
        dc = jnp.where(pl.program_id(0) < n_qk, dqk_ref[...], dv_ref[...])
        dy = dc * (sg * (1.0 + y * (1.0 - sg)))
        dx = w[CONV_K - 1:CONV_K, :] * dy
        for j in range(CONV_K - 1):
            dx = dx + w[j:j + 1, :] * _shift_up(dy, CONV_K - 1 - j)
        dx_ref[...] = dx.astype(dx_ref.dtype)
        dw_ref[...] = jnp.concatenate([jnp.sum(xs[j] * dy, axis=0, keepdims=True) for j in range(CONV_K)], axis=0)

    return pl.pallas_call(
        body, grid=(width // HEAD,),
        in_specs=[pl.BlockSpec((t, HEAD), lambda j: (0, COL_GQ + j)), pl.BlockSpec((CONV_K, HEAD), lambda j: (0, j)),
                  pl.BlockSpec((t, HEAD), lambda j: (0, jnp.minimum(j, n_qk - 1))), pl.BlockSpec((t, HEAD), lambda j: (0, jnp.maximum(j - n_qk, 0)))],
        out_specs=[pl.BlockSpec((t, HEAD), lambda j: (0, j)), pl.BlockSpec((CONV_K, HEAD), lambda j: (0, j))],
        out_shape=[SDS((t, width), BF16), SDS((CONV_K, width), F32)],
        name="gdn_conv_bwd", compiler_params=_params(1),
    )(proj, conv_w, dc_qk, dc_v)


def _l2norm(x, scale):
    return x * lax.rsqrt(jnp.sum(x * x, axis=-1, keepdims=True) + EPS) * scale


def _head(a, h):
    return a[:, h * HEAD:(h + 1) * HEAD]


def _qk_scale(h):
    return HEAD ** -0.5 if h < GDN_QK_HEADS else 1.0


def _qk_norm_fwd(c):
    t = c.shape[0]
    tt = _tile(t, (256, 128))
    width = 2 * D_MODEL

    def fn(x):
        return jnp.concatenate([_l2norm(_head(x, h), _qk_scale(h)) for h in range(2 * GDN_QK_HEADS)], axis=1)

    return _tmap(fn, (1, t // tt), [(c, *_rows(width, tt))], [((t, width), F32, *_rows(width, tt), None)], "gdn_qk_norm")[0]


def _qk_norm_bwd(c, dq_rep, dk_rep):
    t = c.shape[0]
    tt = _tile(t, (256, 128))
    width = 2 * D_MODEL

    def fn(x, dq2, dk2):
        out = []
        for h in range(2 * GDN_QK_HEADS):
            d2, hh = (dq2, h) if h < GDN_QK_HEADS else (dk2, h - GDN_QK_HEADS)
            _, vjp = jax.vjp(lambda x: _l2norm(x, _qk_scale(h)), _head(x, h))
            out.append(vjp(_head(d2, 2 * hh) + _head(d2, 2 * hh + 1))[0])
        return jnp.concatenate(out, axis=1)

    r = _rows(width, tt)
    return _tmap(fn, (1, t // tt), [(c, *r), (dq_rep, *r), (dk_rep, *r)], [((t, width), F32, *r, None)], "gdn_qk_norm_bwd")[0]


def _gdn_gates(x, alog, dtb):
    return -jnp.exp(alog) * _softplus(x + dtb), jax.nn.sigmoid(x)


def _gates_fwd(pab, alog, dtb):
    t = pab.shape[0]
    tt = _tile(t, (256, 128))

    def fn(x, alog, dtb):
        g, beta = _gdn_gates(x, alog, dtb)
        lane = lax.broadcasted_iota(jnp.int32, g.shape, 1)
        return jnp.where(lane < GDN_V_HEADS, _hdot_raw(_chunk_sum_matrix(tt, GDN_CHUNK), g), beta)

    p = (alog, (1, HEAD), lambda j, i: (0, 0)), (dtb, (1, HEAD), lambda j, i: (0, 0))
    return _tmap(fn, (1, t // tt), [(pab, *_rows(HEAD, tt)), *p], [((t, HEAD), F32, *_rows(HEAD, tt), None)], "gdn_gates")[0]


def _gates_bwd(pab, alog, dtb, dout):
    t = pab.shape[0]
    tt = _tile(t, (256, 128))

    def fn(x, alog, dtb, dout):
        lane = lax.broadcasted_iota(jnp.int32, dout.shape, 1)
        dgam = jnp.where(lane < GDN_V_HEADS, dout, 0.0)
        dbeta = jnp.where(jnp.logical_and(lane >= GDN_V_HEADS, lane < 2 * GDN_V_HEADS), dout, 0.0)
        dg = _hdot_raw(_chunk_sum_matrix(tt, GDN_CHUNK, transpose=True), dgam)
        _, vjp = jax.vjp(_gdn_gates, x, alog, dtb)
        return vjp((dg, dbeta))

    p = (alog, (1, HEAD), lambda j, i: (0, 0)), (dtb, (1, HEAD), lambda j, i: (0, 0))
    acc = ((1, HEAD), F32, (1, HEAD), lambda j, i: (0, 0), "inner")
    return _tmap(fn, (1, t // tt), [(pab, *_rows(HEAD, tt)), *p, (dout, *_rows(HEAD, tt))],
                 [((t, HEAD), BF16, *_rows(HEAD, tt), None), acc, acc], "gdn_gates_bwd")


def _split_bf16(x):
    hi = x.astype(BF16)
    return hi, (x - hi.astype(F32)).astype(BF16)


def _dot3(a, b):
    (ah, al), (bh, bl) = a, b
    return _bdot_raw(ah, bh, "nn") + (_bdot_raw(ah, bl, "nn") + _bdot_raw(al, bh, "nn"))


def _each(fn, *lists):
    return tuple(fn(*xs) for xs in zip(*lists))


def _unit_lower_inverses_raw(a):
    n = a[0].shape[0]
    row = lax.broadcasted_iota(jnp.int32, (n, n), 0)
    col = lax.broadcasted_iota(jnp.int32, (n, n), 1)
    eye = jnp.where(row == col, 1.0, 0.0).astype(F32)
    p = _each(lambda a: eye - a, a)
    x = _each(_split_bf16, a)
    m = 2
    while m < 2 * n:
        x = _each(_split_bf16, _each(_dot3, x, x))
        p = _each(lambda p, x: p + _dot3(_split_bf16(p), x), p, x)
        m *= 2
    return p


@jax.custom_vjp
def _unit_lower_inverses(a, known):
    return _unit_lower_inverses_raw(a) if known is None else known


def _uli_fwd(a, known):
    inv = _unit_lower_inverses(a, known)
    return inv, (inv, known)


def _uli_bwd(res, ct):
    inv, known = res
    right = _each(lambda ct, inv: _bdot_raw(ct, inv, "nt"), ct, inv)
    da = _each(lambda inv, r: -_bdot_raw(inv, r, "tn"), inv, right)
    return da, (None if known is None else _each(jnp.zeros_like, known))


_unit_lower_inverses.defvjp(_uli_fwd, _uli_bwd)


def _gdn_chunks(q, k, v, beta, gam, gam_row, s, inv_known=None):
    n = q[0].shape[0]
    row = lax.broadcasted_iota(jnp.int32, (n, n), 0)
    col = lax.broadcasted_iota(jnp.int32, (n, n), 1)
    decay = _each(lambda gam, gam_row: jnp.where(row >= col, jnp.exp(jnp.minimum(gam - gam_row, 0.0)), 0.0), gam, gam_row)
    kb = _each(lambda k, beta: k * beta, k, beta)
    a = _each(lambda kb, k, decay: jnp.where(row > col, _bdot(kb, k, "nt") * decay, 0.0), kb, k, decay)
    inv = _unit_lower_inverses(a, inv_known)
    eg = _each(jnp.exp, gam)
    u = _each(lambda inv, v, beta: _bdot(inv, v * beta, "nn"), inv, v, beta)
    w = _each(lambda inv, kb, eg: _bdot(inv, kb * eg, "nn"), inv, kb, eg)
    qk = _each(lambda q, k, decay: _bdot(q, k, "nt") * decay, q, k, decay)
    v_new = _each(lambda u, w, s: u - _bdot(w, s, "nn"), u, w, s)
    o_state = _each(lambda q, eg, s: _bdot(q * eg, s, "nn"), q, eg, s)
    o = _each(lambda o_state, qk, v_new: o_state + _bdot(qk, v_new, "nn"), o_state, qk, v_new)
    gend = _each(lambda gam: gam[n - 1:n, :], gam)
    s_new = _each(lambda s, k, gam, gend, v_new: s * jnp.exp(gend) + _bdot(k * jnp.exp(gend - gam), v_new, "tn"), s, k, gam, gend, v_new)
    return o, s_new, inv


GDN_GROUP = 16


def _gdn_specs(nc, rev):
    cc = (lambda c: nc - 1 - c) if rev else (lambda c: c)
    grp = GDN_GROUP
    q = pl.BlockSpec((GDN_CHUNK, grp // 2 * HEAD), lambda h, c: (cc(c), h))
    k = pl.BlockSpec((GDN_CHUNK, grp // 2 * HEAD), lambda h, c: (cc(c), 2 * GDN_QK_HEADS // grp + h))
    v = pl.BlockSpec((GDN_CHUNK, grp * HEAD), lambda h, c: (cc(c), 2 * GDN_QK_HEADS // grp + h))
    o = pl.BlockSpec((GDN_CHUNK, grp * HEAD), lambda h, c: (cc(c), h))
    col = pl.BlockSpec((grp, None, GDN_CHUNK, 1), lambda h, c: (h, cc(c), 0, 0))
    rw = pl.BlockSpec((grp, None, 1, GDN_CHUNK), lambda h, c: (h, cc(c), 0, 0))
    st = pl.BlockSpec((grp, None, HEAD, HEAD), lambda h, c: (h, cc(c), 0, 0))
    inv = pl.BlockSpec((grp, None, GDN_CHUNK, GDN_CHUNK), lambda h, c: (h, cc(c), 0, 0))
    return q, k, v, o, col, rw, st, inv


def _head_lanes(g, per=1):
    return pl.ds((g // per) * HEAD, HEAD)


def _gdn_rec_fwd(qk, c, beta_col, gam_col, gam_row):
    t = qk.shape[0]
    nc = t // GDN_CHUNK
    q, k, v, o, col, rw, st, inv = _gdn_specs(nc, False)

    def body(q_ref, k_ref, v_ref, be_ref, g_ref, gr_ref, o_ref, ss_ref, inv_ref, s_ref):
        @pl.when(pl.program_id(1) == 0)
        def _():
            s_ref[...] = jnp.zeros_like(s_ref)

        heads = range(GDN_GROUP)
        s = tuple(s_ref[g] for g in heads)
        out, s_new, inv_c = _gdn_chunks(
            tuple(q_ref[:, _head_lanes(g, 2)] for g in heads), tuple(k_ref[:, _head_lanes(g, 2)] for g in heads),
            tuple(v_ref[:, _head_lanes(g)] for g in heads), tuple(be_ref[g] for g in heads), tuple(g_ref[g] for g in heads),
            tuple(gr_ref[g] for g in heads), s)
        for g in heads:
            ss_ref[g] = s[g]
            o_ref[:, _head_lanes(g)] = out[g]
            inv_ref[g] = inv_c[g]
            s_ref[g] = s_new[g]

    return pl.pallas_call(
        body, grid=(GDN_V_HEADS // GDN_GROUP, nc), in_specs=[q, k, v, col, col, rw], out_specs=[o, st, inv],
        out_shape=[SDS((t, 2 * D_MODEL), F32), SDS((GDN_V_HEADS, nc, HEAD, HEAD), F32), SDS((GDN_V_HEADS, nc, GDN_CHUNK, GDN_CHUNK), F32)],
        scratch_shapes=[pltpu.VMEM((GDN_GROUP, HEAD, HEAD), F32)], name="gdn_rec", compiler_params=_params(2),
    )(qk, qk, c, beta_col, gam_col, gam_row)


def _gdn_rec_bwd(qk, c, beta_col, gam_col, gam_row, ss, invs, do):
    t = qk.shape[0]
    nc = t // GDN_CHUNK
    q, k, v, o, col, rw, st, inv = _gdn_specs(nc, True)

    def body(q_ref, k_ref, v_ref, be_ref, g_ref, gr_ref, ss_ref, inv_ref, do_ref,
             dq_ref, dk_ref, dv_ref, dbe_ref, dg_ref, dgr_ref, ds_ref):
        @pl.when(pl.program_id(1) == 0)
        def _():
            ds_ref[...] = jnp.zeros_like(ds_ref)

        heads = range(GDN_GROUP)
        _, vjp = jax.vjp(
            _gdn_chunks,
            tuple(q_ref[:, _head_lanes(g, 2)] for g in heads), tuple(k_ref[:, _head_lanes(g, 2)] for g in heads),
            tuple(v_ref[:, _head_lanes(g)] for g in heads), tuple(be_ref[g] for g in heads), tuple(g_ref[g] for g in heads),
            tuple(gr_ref[g] for g in heads), tuple(ss_ref[g] for g in heads), tuple(inv_ref[g] for g in heads))
        no_inv_ct = tuple(jnp.zeros((GDN_CHUNK, GDN_CHUNK), F32) for g in heads)
        dq, dk, dv, dbe, dg, dgr, ds, _ = vjp((tuple(do_ref[:, _head_lanes(g)] for g in heads), tuple(ds_ref[g] for g in heads), no_inv_ct))
        for g in heads:
            dq_ref[:, _head_lanes(g)] = dq[g]
            dk_ref[:, _head_lanes(g)] = dk[g]
            dv_ref[:, _head_lanes(g)] = dv[g]
            dbe_ref[g] = dbe[g]
            dg_ref[g] = dg[g]
            dgr_ref[g] = dgr[g]
            ds_ref[g] = ds[g]

    wide = SDS((t, 2 * D_MODEL), F32)
    colshape = SDS((GDN_V_HEADS, nc, GDN_CHUNK, 1), F32)
    return pl.pallas_call(
        body, grid=(GDN_V_HEADS // GDN_GROUP, nc), in_specs=[q, k, v, col, col, rw, st, inv, o], out_specs=[o, o, o, col, col, rw],
        out_shape=[wide, wide, wide, colshape, colshape, SDS((GDN_V_HEADS, nc, 1, GDN_CHUNK), F32)],
        scratch_shapes=[pltpu.VMEM((GDN_GROUP, HEAD, HEAD), F32)], name="gdn_rec_bwd", compiler_params=_params(2),
    )(qk, qk, c, beta_col, gam_col, gam_row, ss, invs, do)


def _gated_norm(o, gate, w):
    return _rms(o, w) * _silu(gate)


def _post_fwd(o, proj, col_off, w, name):
    t, width = o.shape
    tt = _tile(t, (256, 128))

    def fn(o, gate, w):
        return jnp.concatenate([_gated_norm(_head(o, h), _head(gate, h), w) for h in range(width // HEAD)], axis=1)

    return _tmap(fn, (1, t // tt),
                 [(o, *_rows(width, tt)), (proj, *_rows(width, tt, col_off * HEAD // width)), (w, (1, HEAD), lambda j, i: (0, 0))],
                 [((t, width), BF16, *_rows(width, tt), None)], name)[0]


def _post_bwd(o, proj, col_off, w, dout, name):
    t, width = o.shape
    tt = _tile(t, (256, 128))

    def fn(o, gate, w, dout):
        do, dgate, dw = [], [], jnp.zeros((1, HEAD), F32)
        for h in range(width // HEAD):
            _, vjp = jax.vjp(_gated_norm, _head(o, h), _head(gate, h), w)
            a, b, c = vjp(_head(dout, h))
            do.append(a)
            dgate.append(b)
            dw = dw + c
        return jnp.concatenate(do, axis=1), jnp.concatenate(dgate, axis=1), dw

    r = _rows(width, tt)
    return _tmap(fn, (1, t // tt),
                 [(o, *r), (proj, *_rows(width, tt, col_off * HEAD // width)), (w, (1, HEAD), lambda j, i: (0, 0)), (dout, *r)],
                 [((t, width), F32, *r, None), ((t, width), BF16, *r, None), ((1, HEAD), F32, (1, HEAD), lambda j, i: (0, 0), "inner")], name)


def _merge(gate_h, gate_g, yh, yg):
    return jax.nn.sigmoid(gate_h) * yh + jax.nn.sigmoid(gate_g) * yg


def _merge_fwd(proj, yh, yg):
    t = yh.shape[0]
    tt, ft = _tile(t, (256, 128)), 512
    r = _rows(ft, tt)
    return _tmap(_merge, (D_MODEL // ft, t // tt),
                 [(proj, *_rows(ft, tt, COL_GATE_H * HEAD // ft)), (proj, *_rows(ft, tt, COL_GATE_G * HEAD // ft)), (yh, *r), (yg, *r)],
                 [((t, D_MODEL), BF16, *r, None)], "merge")[0]


def _merge_bwd(proj, yh, yg, dy):
    t = yh.shape[0]
    tt, ft = _tile(t, (256, 128)), 512
    r = _rows(ft, tt)

    def fn(gate_h, gate_g, yh, yg, dy):
        _, vjp = jax.vjp(_merge, gate_h, gate_g, yh, yg)
        return vjp(dy)

    o = ((t, D_MODEL), BF16, *r, None)
    return _tmap(fn, (D_MODEL // ft, t // tt),
                 [(proj, *_rows(ft, tt, COL_GATE_H * HEAD // ft)), (proj, *_rows(ft, tt, COL_GATE_G * HEAD // ft)), (yh, *r), (yg, *r), (dy, *r)],
                 [o, o, o, o], "merge_bwd")


def _loss_head(h, target, g):
    t, d = h.shape
    tt = _tile(t, (256, 128))

    def fn(h, target, g):
        def f(h, g):
            err = _rms(h, g) - target
            return 0.5 * jnp.sum(jnp.mean(err * err, axis=-1))

        loss, (dh, dg) = jax.value_and_grad(f, (0, 1))(h, g)
        return dh, dg, jnp.full((1, HEAD), loss, F32)

    return _tmap(fn, (1, t // tt), [(h, *_rows(d, tt)), (target, *_rows(d, tt)), (g, (1, d), lambda j, i: (0, 0))],
                 [((t, d), F32, *_rows(d, tt), None), ((1, d), F32, (1, d), lambda j, i: (0, 0), "inner"),
                  ((1, HEAD), F32, (1, HEAD), lambda j, i: (0, 0), "inner")], "loss_head")


def _heads_to_cols(a):
    t = a.shape[0]
    return a.T.reshape(GDN_V_HEADS, t // GDN_CHUNK, GDN_CHUNK, 1)


def _mixer_fwd(h, p, links):
    t = h.shape[0]
    nc = t // GDN_CHUNK
    u = _rms_fwd(h, p["mix_norm"], "mix_norm")
    w = {n: links.weight(n, h) for n in ("w_in_main_t", "w_in_ab_t", "conv_w")}
    proj = _mm(u, w["w_in_main_t"], "nt", F32, "mix_in", after=links.started)
    pab = _mm(u, w["w_in_ab_t"], "nt", F32, "mix_in_ab")
    qh, kh, bh = _hgrn_prep_fwd(proj, p["lbl"])
    oh, hs = _hgrn_rec_fwd(qh, kh, proj, bh)
    c = _conv_fwd(proj, w["conv_w"])
    qk = _qk_norm_fwd(c)
    gates = _gates_fwd(pab, p["alog"], p["dtb"])
    gam = gates[:, :GDN_V_HEADS]
    beta_col = _heads_to_cols(gates[:, GDN_V_HEADS:2 * GDN_V_HEADS])
    gam_col = _heads_to_cols(gam)
    gam_row = gam.T.reshape(GDN_V_HEADS, nc, 1, GDN_CHUNK)
    og, ss, invs = _gdn_rec_fwd(qk, c, beta_col, gam_col, gam_row)
    ohn = _post_fwd(oh, proj, COL_HG, p["hgrn_out_norm"], "hgrn_out")
    ogn = _post_fwd(og, proj, COL_GZ, p["gdn_out_norm"], "gdn_out")
    w.update({n: links.weight(n, ogn) for n in ("w_branch_hgrn", "w_branch_gdn", "w_out")})
    yh = _mm(ohn, w["w_branch_hgrn"], "nn", F32, "branch_hgrn")
    yg = _mm(ogn, w["w_branch_gdn"], "nn", F32, "branch_gdn")
    y = _merge_fwd(proj, yh, yg)
    out = _mm(y, w["w_out"], "nn", F32, "mix_out", res=h)
    saved = (w, u, proj, pab, qh, kh, bh, oh, hs, c, qk, beta_col, gam_col, gam_row, og, ss, invs, ohn, ogn, yh, yg, y)
    return out, saved


def _mixer_bwd(h, p, links, saved, dout):
    (w, u, proj, pab, qh, kh, bh, oh, hs, c, qk, beta_col, gam_col, gam_row, og, ss, invs, ohn, ogn, yh, yg, y) = saved
    t = h.shape[0]
    grads = {}
    dw_out = _mm(y, dout, "tn", BF16, "mix_out_dw")
    dy = _mm(dout, w["w_out"], "nt", F32, "mix_out_dx")
    dgate_h, dgate_g, dyh, dyg = _merge_bwd(proj, yh, yg, dy)
    dw_bh = _mm(ohn, dyh, "tn", BF16, "branch_hgrn_dw")
    dw_bg = _mm(ogn, dyg, "tn", BF16, "branch_gdn_dw")
    sent = links.send({"w_out": dw_out, "w_branch_hgrn": dw_bh, "w_branch_gdn": dw_bg})
    dohn = _mm(dyh, w["w_branch_hgrn"], "nt", F32, "branch_hgrn_dx", after=sent)
    dogn = _mm(dyg, w["w_branch_gdn"], "nt", F32, "branch_gdn_dx")
    doh, dhg, grads["hgrn_out_norm"] = _post_bwd(oh, proj, COL_HG, p["hgrn_out_norm"], dohn, "hgrn_out_bwd")
    dog, dgz, grads["gdn_out_norm"] = _post_bwd(og, proj, COL_GZ, p["gdn_out_norm"], dogn, "gdn_out_bwd")
    dqh, dkh, dhi, dbh = _hgrn_rec_bwd(qh, kh, proj, bh, hs, doh)
    dhq, dhf, grads["lbl"] = _hgrn_prep_bwd(proj, p["lbl"], dqh, dkh, dbh)
    dqv, dkv, dcv, dbeta_col, dgam_col, dgam_row = _gdn_rec_bwd(qk, c, beta_col, gam_col, gam_row, ss, invs, dog)
    dcqk = _qk_norm_bwd(c, dqv, dkv)
    dxin, grads["conv_w"] = _conv_bwd(proj, w["conv_w"], dcqk, dcv)
    dgam = (dgam_col.reshape(GDN_V_HEADS, t) + dgam_row.reshape(GDN_V_HEADS, t)).T
    dbeta = dbeta_col.reshape(GDN_V_HEADS, t).T
    dgates = jnp.concatenate([dgam, dbeta, jnp.zeros((t, HEAD - 2 * GDN_V_HEADS), F32)], axis=1)
    dpab, grads["alog"], grads["dtb"] = _gates_bwd(pab, p["alog"], p["dtb"], dgates)
    dproj = jnp.concatenate([dhq, dhf, dhi, dhg, dxin, dgz, dgate_h, dgate_g], axis=1)
    dw_main_t = _mm(dproj, u, "tn", BF16, "mix_in_dw")
    dw_ab_t = _mm(dpab, u, "tn", BF16, "mix_in_ab_dw")
    sent = links.send({"w_in": _w_in_join(dw_main_t, dw_ab_t)})
    du = _mm(dproj, w["w_in_main_t"], "nn", F32, "mix_in_dx", after=sent)
    du = _mm(dpab, w["w_in_ab_t"], "nn", F32, "mix_in_ab_dx", res=du)
    dh, grads["mix_norm"] = _rms_bwd(h, p["mix_norm"], du, dout, "mix_norm_bwd")
    return dh, grads


def _local_step(x, target, p, links):
    w1 = {n: links.weight(n, x) for n in ("ffn1_w_in", "ffn1_w_out")}
    h1, s1 = _ffn_fwd(x, p["ffn1_norm"], w1["ffn1_w_in"], w1["ffn1_w_out"], "ffn1", links.started)
    h2, sm = _mixer_fwd(h1, p, links)
    w2 = {n: links.weight(n, h2) for n in ("ffn2_w_in", "ffn2_w_out")}
    h3, s2 = _ffn_fwd(h2, p["ffn2_norm"], w2["ffn2_w_in"], w2["ffn2_w_out"], "ffn2", None)
    dh3, dfinal, loss = _loss_head(h3, target, p["final_norm"])
    g = {"final_norm": dfinal}
    dh2, g["ffn2_norm"] = _ffn_bwd(h2, p["ffn2_norm"], w2["ffn2_w_in"], w2["ffn2_w_out"], s2, dh3, "ffn2", links)
    dh1, gm = _mixer_bwd(h1, p, links, sm, dh2)
    g.update(gm)
    dx, g["ffn1_norm"] = _ffn_bwd(x, p["ffn1_norm"], w1["ffn1_w_in"], w1["ffn1_w_out"], s1, dh1, "ffn1", links)
    return loss, dx, g


def _exchange(items, name):
    n = len(items)
    out_shape = [SDS((N_DEV,) + a.shape if mode == "gather" else a.shape, a.dtype) for a, mode in items]

    def body(*refs):
        in_refs, out_refs = refs[:n], refs[n:2 * n]
        send_sems, recv_sems, local_sems = refs[2 * n:]
        x, y, c = lax.axis_index("x"), lax.axis_index("y"), lax.axis_index("c")
        me = 4 * x + 2 * y + c
        local = []
        for i, (_, mode) in enumerate(items):
            src = in_refs[i] if mode == "gather" else in_refs[i].at[me]
            cp = pltpu.make_async_copy(src, out_refs[i].at[me], local_sems.at[i])
            cp.start()
            local.append(cp)
        remote = []
        for rel in range(1, N_DEV):
            px = 1 - x if rel & 4 else x
            py = 1 - y if rel & 2 else y
            pc = 1 - c if rel & 1 else c
            peer = 4 * px + 2 * py + pc
            for i, (_, mode) in enumerate(items):
                src = in_refs[i] if mode == "gather" else in_refs[i].at[peer]
                cp = pltpu.make_async_remote_copy(src_ref=src, dst_ref=out_refs[i].at[me], send_sem=send_sems.at[i, rel - 1],
                                                  recv_sem=recv_sems.at[i, rel - 1], device_id=(px, py, pc), device_id_type=MESH_IDS)
                cp.start()
                remote.append(cp)
        for cp in remote:
            cp.wait()
        for cp in local:
            cp.wait()

    anyspace = pl.BlockSpec(memory_space=pl.ANY)
    return pl.pallas_call(
        body, in_specs=[anyspace] * n, out_specs=[anyspace] * n, out_shape=out_shape,
        scratch_shapes=[pltpu.SemaphoreType.DMA((n, N_DEV - 1)), pltpu.SemaphoreType.DMA((n, N_DEV - 1)), pltpu.SemaphoreType.DMA((n,))],
        name=name, compiler_params=pltpu.CompilerParams(has_side_effects=True),
    )(*[a for a, _ in items])


HBM_SPEC = pl.BlockSpec(memory_space=pltpu.HBM)
SEM_SPEC = pl.BlockSpec(memory_space=pltpu.SEMAPHORE)
DATAFLOW = pltpu.SideEffectType.DATAFLOW_SIDE_EFFECTING


def _position():
    x, y, c = lax.axis_index("x"), lax.axis_index("y"), lax.axis_index("c")
    return x, y, c, 4 * x + 2 * y + c


def _relations(x, y, c):
    for rel in range(1, N_DEV):
        px = 1 - x if rel & 4 else x
        py = 1 - y if rel & 2 else y
        pc = 1 - c if rel & 1 else c
        yield rel, (px, py, pc), 4 * px + 2 * py + pc


def _sem_index(item, rel):
    return item * (N_DEV - 1) + rel - 1


def _landing(a, mode):
    return lax.empty((N_DEV,) + a.shape if mode == "gather" else a.shape, a.dtype)


def _copies_start(groups, name):
    flat = [item for grp in groups for item in grp]
    n, ng = len(flat), len(groups)
    lands = [_landing(a, mode) for a, mode in flat]

    def body(*refs):
        src_refs, land_refs, sems, token = refs[:n], refs[n:2 * n], refs[2 * n:2 * n + 2 * ng], refs[-1]
        x, y, c, me = _position()
        for rel, where, peer in _relations(x, y, c):
            k = 0
            for gi, grp in enumerate(groups):
                for li, (_, mode) in enumerate(grp):
                    src = src_refs[k] if mode == "gather" else src_refs[k].at[peer]
                    pltpu.make_async_remote_copy(src_ref=src, dst_ref=land_refs[k].at[me], send_sem=sems[2 * gi].at[_sem_index(li, rel)],
                                                 recv_sem=sems[2 * gi + 1].at[_sem_index(li, rel)], device_id=where, device_id_type=MESH_IDS).start()
                    k += 1
        token[...] = jnp.zeros_like(token)

    sem_shapes = [pltpu.SemaphoreType.DMA((len(grp) * (N_DEV - 1),)) for grp in groups for _ in range(2)]
    thru = [pltpu.HBM(a.shape, a.dtype) for a, _ in flat] + [pltpu.HBM(l.shape, l.dtype) for l in lands]
    outs = pl.pallas_call(
        body, name=name, out_shape=(*sem_shapes, *thru, SDS((8, HEAD), F32)),
        in_specs=[HBM_SPEC] * (2 * n), out_specs=(*[SEM_SPEC] * (2 * ng), *[HBM_SPEC] * (2 * n), pl.BlockSpec(memory_space=pltpu.VMEM)),
        input_output_aliases={i: 2 * ng + i for i in range(2 * n)}, compiler_params=pltpu.CompilerParams(has_side_effects=DATAFLOW),
    )(*[pltpu.with_memory_space_constraint(a, pltpu.HBM) for a, _ in flat], *[pltpu.with_memory_space_constraint(l, pltpu.HBM) for l in lands])
    sems, srcs, landed, token = outs[:2 * ng], outs[2 * ng:2 * ng + n], outs[2 * ng + n:2 * ng + 2 * n], outs[-1]
    result, k = [], 0
    for gi, grp in enumerate(groups):
        result.append((sems[2 * gi], sems[2 * gi + 1], srcs[k:k + len(grp)], landed[k:k + len(grp)]))
        k += len(grp)
    return result, token


def _copies_wait(started, modes, after, name):
    send_sems, recv_sems, srcs, lands = started
    n = len(srcs)

    def body(*refs):
        src_refs, land_refs, ssem, rsem, token = refs[:n], refs[n:2 * n], refs[2 * n], refs[2 * n + 1], refs[-1]
        x, y, c, _ = _position()
        for rel in range(1, N_DEV):
            for i, mode in enumerate(modes):
                src = src_refs[i] if mode == "gather" else src_refs[i].at[0]
                cp = pltpu.make_async_remote_copy(src_ref=src, dst_ref=land_refs[i].at[0], send_sem=ssem.at[_sem_index(i, rel)],
                                                  recv_sem=rsem.at[_sem_index(i, rel)], device_id=(x, y, c), device_id_type=MESH_IDS)
                cp.wait_send()
                cp.wait_recv()
        token[...] = jnp.zeros_like(token)

    outs = pl.pallas_call(
        body, name=name, out_shape=[pltpu.HBM(a.shape, a.dtype) for a in (*srcs, *lands)] + [SDS((8, HEAD), F32)],
        in_specs=[HBM_SPEC] * (2 * n) + [SEM_SPEC, SEM_SPEC, pl.BlockSpec(memory_space=pl.ANY)],
        out_specs=[HBM_SPEC] * (2 * n) + [pl.BlockSpec(memory_space=pltpu.VMEM)],
        input_output_aliases={i: i for i in range(2 * n)}, compiler_params=pltpu.CompilerParams(has_side_effects=DATAFLOW),
    )(*srcs, *lands, send_sems, recv_sems, after)
    return outs[:n], outs[n:2 * n], outs[-1]


WEIGHT_GROUPS = (("ffn1_w_in", "ffn1_w_out", "gdn_conv_w"), ("w_in",), ("w_branch_hgrn", "w_branch_gdn", "w_out", "ffn2_w_in", "ffn2_w_out"))


class _Links:
    def __init__(self, shards, me):
        self.me = me
        self.shards = shards
        self.weights = {}
        self.sends = []
        self.gathers = {}
        self.started = None
        self._start_gather(0, None)

    def _start_gather(self, gi, zeros):
        if gi < len(WEIGHT_GROUPS):
            items = [(self.shards[n] if zeros is None else self.shards[n] + zeros[0, 0].astype(self.shards[n].dtype), "gather")
                     for n in WEIGHT_GROUPS[gi]]
            started, self.started = _copies_start([items], "gather_start_%d" % gi)
            self.gathers[gi] = started[0]

    def weight(self, name, after):
        if name not in self.weights:
            gi = [i for i, grp in enumerate(WEIGHT_GROUPS) if {"w_in_main_t": "w_in", "w_in_ab_t": "w_in", "conv_w": "gdn_conv_w"}.get(name, name) in grp][0]
            assert gi in self.gathers, "weight groups are asked for in order"
            srcs, lands, zero = _copies_wait(self.gathers[gi], ["gather"] * len(WEIGHT_GROUPS[gi]), after, "gather_wait_%d" % gi)
            self._start_gather(gi + 1, zero)
            for n, src, land in zip(WEIGHT_GROUPS[gi], srcs, lands):
                full = lax.dynamic_update_index_in_dim(land, src, self.me, 0)
                if n == "gdn_conv_w":
                    self.weights["conv_w"] = full.reshape(N_DEV, CONV_K, 4 * D_MODEL // N_DEV).transpose(1, 0, 2).reshape(CONV_K, 4 * D_MODEL)
                elif n == "w_in":
                    self.weights["w_in_main_t"], self.weights["w_in_ab_t"] = _w_in_split(full.reshape(-1, D_MODEL))
                else:
                    self.weights[n] = full.reshape(-1, D_MODEL)
        return self.weights[name]

    def send(self, grads):
        names = list(grads)
        blocks = [grads[n].reshape(N_DEV, -1, D_MODEL) for n in names]
        started, token = _copies_start([[(b, "scatter") for b in blocks]], "send_" + names[0])
        self.sends.append((names, started[0]))
        return token

    def landed(self, after):
        out = {}
        for names, started in self.sends:
            srcs, lands, _ = _copies_wait(started, ["scatter"] * len(names), after, "landed_" + names[0])
            for n, src, land in zip(names, srcs, lands):
                out[n] = lax.dynamic_update_index_in_dim(land, lax.dynamic_index_in_dim(src, self.me, 0, keepdims=False), self.me, 0)
        return out


def _adam(parts, w, m, v, name):
    n_parts, r, c = parts.shape
    tc = c if c <= 512 else (256 if r > 1024 else 512)

    def body(p_ref, w_ref, m_ref, v_ref, g_ref, d_ref, mo_ref, vo_ref):
        g = p_ref[0].astype(F32)
        for i in range(1, n_parts):
            g = g + p_ref[i].astype(F32)
        m_new = ADAM_B1 * m_ref[...] + (1.0 - ADAM_B1) * g
        v_new = ADAM_B2 * v_ref[...] + (1.0 - ADAM_B2) * (g * g)
        m_hat = m_new / (1.0 - ADAM_B1 ** ADAM_STEP)
        v_hat = v_new / (1.0 - ADAM_B2 ** ADAM_STEP)
        g_ref[...] = g
        d_ref[...] = -ADAM_LR * (m_hat / (jnp.sqrt(v_hat) + ADAM_EPS) + ADAM_WD * w_ref[...])
        mo_ref[...] = m_new
        vo_ref[...] = v_new

    spec = pl.BlockSpec((r, tc), lambda j: (0, j))
    return pl.pallas_call(
        body, grid=(c // tc,), in_specs=[pl.BlockSpec((n_parts, r, tc), lambda j: (0, 0, j)), spec, spec, spec],
        out_specs=[spec] * 4, out_shape=[SDS((r, c), F32)] * 4, name=name, compiler_params=_params(1),
    )(parts, w, m, v)


BIG = ("ffn1_w_in", "ffn1_w_out", "w_in", "w_branch_hgrn", "w_branch_gdn", "w_out", "ffn2_w_in", "ffn2_w_out")


TRANSPOSED = ("ffn1_w_in", "w_in", "ffn2_w_in")


def _shard_rows(name, shard):
    return shard.T if name in TRANSPOSED else shard


SCALAR_ROWS = 8192
N_SCALAR = 2 * GDN_V_HEADS


def _w_in_split(w_in_t):
    main = jnp.concatenate([w_in_t[:SCALAR_ROWS], w_in_t[SCALAR_ROWS + N_SCALAR:]], axis=0)
    ab = jnp.pad(w_in_t[SCALAR_ROWS:SCALAR_ROWS + N_SCALAR], ((0, HEAD - N_SCALAR), (0, 0)))
    return main, ab


def _w_in_join(dmain_t, dab_t):
    return jnp.concatenate([dmain_t[:SCALAR_ROWS], dab_t[:N_SCALAR], dmain_t[SCALAR_ROWS:]], axis=0)


def _pad_lanes(a, width=HEAD):
    return jnp.pad(a, ((0, 0), (0, width - a.shape[1])))


SMALL_ROWS = 24


def _pack_small(g, loss):
    row6 = jnp.concatenate([g["hgrn_out_norm"], g["gdn_out_norm"], g["alog"], g["dtb"], loss,
                            jnp.zeros((1, D_MODEL - 5 * HEAD), F32)], axis=1)
    return jnp.concatenate([g["ffn1_norm"], g["mix_norm"], g["lbl"], g["ffn2_norm"], g["final_norm"], row6,
                            jnp.zeros((1, D_MODEL), F32), g["conv_w"].reshape(4 * CONV_K, D_MODEL)], axis=0)


def _pack_small_state(a):
    row6 = jnp.concatenate([a["hgrn_out_norm"], a["gdn_out_norm"], _pad_lanes(a["gdn_a_log"]), _pad_lanes(a["gdn_dt_bias"]),
                            jnp.zeros((1, D_MODEL - 4 * HEAD), F32)], axis=1)
    return jnp.concatenate([a["ffn1_norm"], a["mix_norm"], a["hgrn_lb_logits"], a["ffn2_norm"], a["final_norm"].reshape(1, D_MODEL),
                            row6, jnp.zeros((1, D_MODEL), F32)], axis=0)


def _unpack_small(a):
    return {"ffn1_norm": a[0:1], "mix_norm": a[1:2], "hgrn_lb_logits": a[2:4], "ffn2_norm": a[4:5], "final_norm": a[5],
            "hgrn_out_norm": a[6:7, :HEAD], "gdn_out_norm": a[6:7, HEAD:2 * HEAD],
            "gdn_a_log": a[6:7, 2 * HEAD:2 * HEAD + GDN_V_HEADS], "gdn_dt_bias": a[6:7, 3 * HEAD:3 * HEAD + GDN_V_HEADS]}


NAMES = ("ffn1_norm", "ffn1_w_in", "ffn1_w_out", "mix_norm", "w_in", "hgrn_lb_logits", "hgrn_out_norm", "gdn_conv_w", "gdn_a_log",
         "gdn_dt_bias", "gdn_out_norm", "w_branch_hgrn", "w_branch_gdn", "w_out", "ffn2_norm", "ffn2_w_in", "ffn2_w_out", "final_norm")


def kernel(x, ffn1_norm, ffn1_w_in, ffn1_w_out, mix_norm, w_in, hgrn_lb_logits, hgrn_out_norm, gdn_conv_w, gdn_a_log, gdn_dt_bias, gdn_out_norm, w_branch_hgrn, w_branch_gdn, w_out, ffn2_norm, ffn2_w_in, ffn2_w_out, final_norm, loss_target, m_ffn1_norm, m_ffn1_w_in, m_ffn1_w_out, m_mix_norm, m_w_in, m_hgrn_lb_logits, m_hgrn_out_norm, m_gdn_conv_w, m_gdn_a_log, m_gdn_dt_bias, m_gdn_out_norm, m_w_branch_hgrn, m_w_branch_gdn, m_w_out, m_ffn2_norm, m_ffn2_w_in, m_ffn2_w_out, m_final_norm, v_ffn1_norm, v_ffn1_w_in, v_ffn1_w_out, v_mix_norm, v_w_in, v_hgrn_lb_logits, v_hgrn_out_norm, v_gdn_conv_w, v_gdn_a_log, v_gdn_dt_bias, v_gdn_out_norm, v_w_branch_hgrn, v_w_branch_gdn, v_w_out, v_ffn2_norm, v_ffn2_w_in, v_ffn2_w_out, v_final_norm):
    wts = dict(zip(NAMES, (ffn1_norm, ffn1_w_in, ffn1_w_out, mix_norm, w_in, hgrn_lb_logits, hgrn_out_norm, gdn_conv_w, gdn_a_log,
                           gdn_dt_bias, gdn_out_norm, w_branch_hgrn, w_branch_gdn, w_out, ffn2_norm, ffn2_w_in, ffn2_w_out, final_norm)))
    mom = dict(zip(NAMES, (m_ffn1_norm, m_ffn1_w_in, m_ffn1_w_out, m_mix_norm, m_w_in, m_hgrn_lb_logits, m_hgrn_out_norm, m_gdn_conv_w,
                           m_gdn_a_log, m_gdn_dt_bias, m_gdn_out_norm, m_w_branch_hgrn, m_w_branch_gdn, m_w_out, m_ffn2_norm, m_ffn2_w_in,
                           m_ffn2_w_out, m_final_norm)))
    var = dict(zip(NAMES, (v_ffn1_norm, v_ffn1_w_in, v_ffn1_w_out, v_mix_norm, v_w_in, v_hgrn_lb_logits, v_hgrn_out_norm, v_gdn_conv_w,
                           v_gdn_a_log, v_gdn_dt_bias, v_gdn_out_norm, v_w_branch_hgrn, v_w_branch_gdn, v_w_out, v_ffn2_norm, v_ffn2_w_in,
                           v_ffn2_w_out, v_final_norm)))
    me = 4 * lax.axis_index("x") + 2 * lax.axis_index("y") + lax.axis_index("c")

    conv_shard = wts["gdn_conv_w"][0]
    shards = {n: _shard_rows(n, wts[n][0]).astype(BF16) for n in BIG}
    shards["gdn_conv_w"] = conv_shard.reshape(2, D_MODEL)
    links = _Links(shards, me)
    p = {"ffn1_norm": wts["ffn1_norm"], "mix_norm": wts["mix_norm"], "ffn2_norm": wts["ffn2_norm"], "final_norm": wts["final_norm"].reshape(1, D_MODEL),
         "lbl": wts["hgrn_lb_logits"], "hgrn_out_norm": wts["hgrn_out_norm"], "gdn_out_norm": wts["gdn_out_norm"],
         "alog": _pad_lanes(wts["gdn_a_log"]), "dtb": _pad_lanes(wts["gdn_dt_bias"])}

    loss, dx, g = _local_step(x[0], loss_target[0], p, links)

    small_parts = _exchange([(_pack_small(g, loss), "gather")], "gather_small")[0]
    landed = links.landed(small_parts)

    big = [{} for _ in range(4)]
    for n in BIG:
        res = _adam(landed[n], _shard_rows(n, wts[n][0]), _shard_rows(n, mom[n][0]), _shard_rows(n, var[n][0]), "adam_" + n)
        for kind in range(4):
            big[kind][n] = _shard_rows(n, res[kind])
    n_vec = SMALL_ROWS - 4 * CONV_K
    small_raw = _adam(small_parts[:, :n_vec], _pack_small_state(wts), _pack_small_state(mom), _pack_small_state(var), "adam_small")
    small = [_unpack_small(o) for o in small_raw]
    loss_total = small_raw[0][6, 4 * HEAD]
    conv_parts = small_parts[:, n_vec:].reshape(N_DEV, CONV_K, 4 * D_MODEL)
    width = 4 * D_MODEL // N_DEV
    conv_mine = lax.dynamic_slice_in_dim(conv_parts, me * width, width, axis=2)
    conv = _adam(conv_mine, conv_shard, mom["gdn_conv_w"][0], var["gdn_conv_w"][0], "adam_conv")

    outs = []
    for kind in range(4):
        for n in NAMES:
            if n in BIG:
                outs.append(big[kind][n][None])
            elif n == "gdn_conv_w":
                outs.append(conv[kind][None])
            else:
                outs.append(small[kind][n])
    return (loss_total, dx[None], *outs)
```

```python
import functools

import jax
import jax.numpy as jnp
from jax import lax
from jax.experimental import pallas as pl
from jax.experimental.pallas import tpu as pltpu

F32 = jnp.float32
BF16 = jnp.bfloat16
HIGHEST = lax.Precision.HIGHEST
MESH_IDS = pl.DeviceIdType.MESH

D_MODEL = 1024
D_FF = 2816
N_DEV = 8
EPS = 1e-6
HEAD = 128
HG_HEADS = 8
GDN_QK_HEADS = 8
GDN_V_HEADS = 16
GDN_CHUNK = 64
HG_CHUNK = 16
CONV_K = 4
IN_WIDTH = 12320
IN_MAIN = 12288
COL_HQ, COL_HF, COL_HI, COL_HG, COL_GQ, COL_GK, COL_GV, COL_GZ, COL_GATE_H, COL_GATE_G = 0, 8, 16, 24, 32, 40, 48, 64, 80, 88
VMEM_LIMIT = 56 * 1024 * 1024

ADAM_LR, ADAM_B1, ADAM_B2, ADAM_EPS, ADAM_WD, ADAM_STEP = 0.001, 0.9, 0.999, 1e-08, 0.01, 10

SDS = jax.ShapeDtypeStruct


def _params(n_axes):
    return pltpu.CompilerParams(dimension_semantics=("arbitrary",) * n_axes, vmem_limit_bytes=VMEM_LIMIT)


def _tile(n, candidates=(512, 384, 256, 128, 64, 32, 16, 8)):
    for c in candidates:
        if n % c == 0:
            return c
    return n


_DIMS = {"nn": ((1,), (0,)), "nt": ((1,), (1,)), "tn": ((0,), (0,))}


def _bdot_raw(a, b, dims):
    return lax.dot_general(a.astype(BF16), b.astype(BF16), (_DIMS[dims], ((), ())), preferred_element_type=F32)


@functools.partial(jax.custom_vjp, nondiff_argnums=(2,))
def _bdot(a, b, dims):
    return _bdot_raw(a, b, dims)


def _bdot_fwd(a, b, dims):
    return _bdot_raw(a, b, dims), (a, b)


def _bdot_bwd(dims, res, ct):
    a, b = res
    if dims == "nn":
        return _bdot_raw(ct, b, "nt"), _bdot_raw(a, ct, "tn")
    if dims == "nt":
        return _bdot_raw(ct, b, "nn"), _bdot_raw(ct, a, "tn")
    return _bdot_raw(b, ct, "nt"), _bdot_raw(a, ct, "nn")


_bdot.defvjp(_bdot_fwd, _bdot_bwd)


def _hdot_raw(a, b):
    return jnp.dot(a, b, precision=HIGHEST, preferred_element_type=F32)


MM_VMEM_BUDGET = 30 * 1024 * 1024
TOKEN = (8, HEAD)


def _mm_tiles(m, n, k, a_bytes, b_bytes, o_bytes, r_bytes):
    tm = _tile(m, (1408, 1024, 704, 512, 256, 128, 64, 32, 16, 8))
    tn = _tile(n, (1408, 1024, 512, 256, 128))
    tk = _tile(k, (2048, 1408, 1024, 512, 256, 128, 64, 32, 16, 8))

    def need(tm, tn, tk):
        return 2 * (tm * tk * a_bytes + tk * tn * b_bytes + tm * tn * (o_bytes + r_bytes)) + tm * tn * 4

    while need(tm, tn, tk) > MM_VMEM_BUDGET:
        if tk > 512 and tk % 256 == 0:
            tk //= 2
        elif tn > 512 and tn % 256 == 0:
            tn //= 2
        elif tm > 256:
            tm //= 2
        else:
            break
    return tm, tn, tk


def _mm(a, b, dims, out_dtype, name, res=None, alpha=1.0, after=None):
    if dims == "nn":
        (m, k), (k2, n) = a.shape, b.shape
    elif dims == "nt":
        (m, k), (n, k2) = a.shape, b.shape
    else:
        (k, m), (k2, n) = a.shape, b.shape
    assert k == k2, (a.shape, b.shape, dims)
    has_res = res is not None
    tm, tn, tk = _mm_tiles(m, n, k, a.dtype.itemsize, b.dtype.itemsize, jnp.dtype(out_dtype).itemsize, res.dtype.itemsize if has_res else 0)
    nk = k // tk
    a_spec = pl.BlockSpec((tk, tm), lambda i, j, kk: (kk, i)) if dims == "tn" else pl.BlockSpec((tm, tk), lambda i, j, kk: (i, kk))
    b_spec = pl.BlockSpec((tn, tk), lambda i, j, kk: (j, kk)) if dims == "nt" else pl.BlockSpec((tk, tn), lambda i, j, kk: (kk, j))
    o_spec = pl.BlockSpec((tm, tn), lambda i, j, kk: (i, j))

    def finish(acc, r_ref, o_ref):
        out = acc * alpha if alpha != 1.0 else acc
        if has_res:
            out = r_ref[...].astype(F32) + out
        o_ref[...] = out.astype(o_ref.dtype)

    n_in = 2 + has_res + (after is not None)

    def body(*refs):
        a_ref, b_ref = refs[:2]
        r_ref = refs[2] if has_res else None
        o_ref = refs[n_in]
        p = _bdot_raw(a_ref[...], b_ref[...], dims)
        if nk == 1:
            finish(p, r_ref, o_ref)
            return
        acc_ref = refs[-1]
        kk = pl.program_id(2)

        @pl.when(kk == 0)
        def _():
            acc_ref[...] = p

        @pl.when(kk > 0)
        def _():
            acc_ref[...] += p

        @pl.when(kk == nk - 1)
        def _():
            finish(acc_ref[...], r_ref, o_ref)

    args = (a, b) + ((res,) if has_res else ()) + ((after,) if after is not None else ())
    in_specs = [a_spec, b_spec] + ([o_spec] if has_res else []) + ([pl.BlockSpec(TOKEN, lambda i, j, kk: (0, 0))] if after is not None else [])
    return pl.pallas_call(
        body, grid=(m // tm, n // tn, nk), in_specs=in_specs, out_specs=o_spec, out_shape=SDS((m, n), out_dtype),
        scratch_shapes=[pltpu.VMEM((tm, tn), F32)] if nk > 1 else [], name=name, compiler_params=_params(3),
    )(*args)


def _tmap(fn, grid, ins, outs, name):
    n_in = len(ins)
    n_ax = len(grid)

    def body(*refs):
        vals = fn(*[r[...] for r in refs[:n_in]])
        if not isinstance(vals, (tuple, list)):
            vals = (vals,)
        first_inner = pl.program_id(n_ax - 1) == 0
        first_all = first_inner
        for ax in range(n_ax - 1):
            first_all = jnp.logical_and(first_all, pl.program_id(ax) == 0)

        def put(ref, val, acc):
            val = val.astype(ref.dtype)
            if acc is None:
                ref[...] = val
                return
            first = first_inner if acc == "inner" else first_all

            @pl.when(first)
            def _():
                ref[...] = val

            @pl.when(jnp.logical_not(first))
            def _():
                ref[...] += val

        for ref, val, o in zip(refs[n_in:], vals, outs):
            put(ref, val, o[4])

    return pl.pallas_call(
        body, grid=grid,
        in_specs=[pl.BlockSpec(bs, im) for _, bs, im in ins],
        out_specs=[pl.BlockSpec(o[2], o[3]) for o in outs],
        out_shape=[SDS(o[0], o[1]) for o in outs],
        name=name, compiler_params=_params(n_ax),
    )(*[a for a, _, _ in ins])


def _rows(width, tt, off=0):
    return (tt, width), (lambda j, i: (i, off + j))


def _rms(x, g):
    x = x.astype(F32)
    return x * lax.rsqrt(jnp.mean(x * x, axis=-1, keepdims=True) + EPS) * g


def _silu(x):
    return x * jax.nn.sigmoid(x)


def _softplus(x):
    return jnp.maximum(x, 0.0) + jnp.log1p(jnp.exp(-jnp.abs(x)))


def _rms_fwd(x, g, name):
    t, d = x.shape
    tt = _tile(t, (256, 128))
    return _tmap(_rms, (1, t // tt), [(x, *_rows(d, tt)), (g, (1, d), lambda j, i: (0, 0))],
                 [((t, d), BF16, *_rows(d, tt), None)], name)[0]


def _rms_bwd(x, g, dn, dres, name):
    t, d = x.shape
    tt = _tile(t, (256, 128))

    def fn(x, g, dn, dres):
        _, vjp = jax.vjp(_rms, x, g)
        dx, dg = vjp(dn.astype(F32))
        return dres + dx, dg

    return _tmap(fn, (1, t // tt),
                 [(x, *_rows(d, tt)), (g, (1, d), lambda j, i: (0, 0)), (dn, *_rows(d, tt)), (dres, *_rows(d, tt))],
                 [((t, d), F32, *_rows(d, tt), None), ((1, d), F32, (1, d), lambda j, i: (0, 0), "inner")], name)


def _swiglu(ab):
    return _silu(ab[:, :D_FF]) * ab[:, D_FF:]


def _swiglu_fwd(ab, name):
    t = ab.shape[0]
    tt = _tile(t, (128,))
    return _tmap(_swiglu, (1, t // tt), [(ab, *_rows(2 * D_FF, tt))], [((t, D_FF), BF16, *_rows(D_FF, tt), None)], name)[0]


def _swiglu_bwd(ab, ds, name):
    t = ab.shape[0]
    tt = _tile(t, (128,))

    def fn(ab, ds):
        a, b = ab[:, :D_FF], ab[:, D_FF:]
        _, vjp = jax.vjp(lambda a, b: _silu(a) * b, a, b)
        da, db = vjp(ds.astype(F32))
        return jnp.concatenate([da, db], axis=1)

    return _tmap(fn, (1, t // tt), [(ab, *_rows(2 * D_FF, tt)), (ds, *_rows(D_FF, tt))],
                 [((t, 2 * D_FF), BF16, *_rows(2 * D_FF, tt), None)], name)[0]


def _ffn_fwd(h, g, w_in_t, w_out, tag, after):
    n = _rms_fwd(h, g, tag + "_norm")
    ab = _mm(n, w_in_t, "nt", F32, tag + "_in", after=after)
    s = _swiglu_fwd(ab, tag + "_act")
    out = _mm(s, w_out, "nn", F32, tag + "_out", res=h, alpha=0.5)
    return out, (n, ab, s)


def _ffn_bwd(h, g, w_in_t, w_out, saved, dout, tag, links):
    n, ab, s = saved
    sent = links.send({tag + "_w_out": _mm(s, dout, "tn", BF16, tag + "_dw_out", alpha=0.5)})
    ds = _mm(dout, w_out, "nt", F32, tag + "_ds", alpha=0.5, after=sent)
    dab = _swiglu_bwd(ab, ds, tag + "_dact")
    sent = links.send({tag + "_w_in": _mm(dab, n, "tn", BF16, tag + "_dw_in")})
    dn = _mm(dab, w_in_t, "nn", F32, tag + "_dn", after=sent)
    return _rms_bwd(h, g, dn, dout, tag + "_dnorm")


def _chunk_sum_matrix(n, chunk, transpose=False):
    row = lax.broadcasted_iota(jnp.int32, (n, n), 0)
    col = lax.broadcasted_iota(jnp.int32, (n, n), 1)
    if transpose:
        row, col = col, row
    return jnp.where(jnp.logical_and(col <= row, row // chunk == col // chunk), 1.0, 0.0).astype(F32)


def _hgrn_gates(hq, hf, lbl):
    lb = jax.nn.sigmoid(lbl[0:1, :] - lbl[1:2, :])
    sg = jax.nn.sigmoid(hf)
    f = lb + (1.0 - lb) * sg
    q = _silu(hq) * HEAD ** -0.5
    k = (1.0 - lb) * (1.0 - sg)
    return q, k, jnp.log(f)


def _hgrn_prep_fwd(proj, lbl):
    t = proj.shape[0]
    tt, ft = _tile(t, (256, 128)), 512

    def fn(hq, hf, lbl):
        q, k, log_f = _hgrn_gates(hq, hf, lbl)
        return q, k, _hdot_raw(_chunk_sum_matrix(tt, HG_CHUNK), log_f)

    o = ((t, D_MODEL), F32, *_rows(ft, tt), None)
    return _tmap(fn, (D_MODEL // ft, t // tt),
                 [(proj, *_rows(ft, tt, COL_HQ * HEAD // ft)), (proj, *_rows(ft, tt, COL_HF * HEAD // ft)), (lbl, (2, ft), lambda j, i: (0, j))],
                 [o, o, o], "hgrn_prep")


def _hgrn_prep_bwd(proj, lbl, dq, dk, db):
    t = proj.shape[0]
    tt, ft = _tile(t, (256, 128)), 512

    def fn(hq, hf, lbl, dq, dk, db):
        dlog_f = _hdot_raw(_chunk_sum_matrix(tt, HG_CHUNK, transpose=True), db)
        _, vjp = jax.vjp(_hgrn_gates, hq, hf, lbl)
        return vjp((dq, dk, dlog_f))

    o = ((t, D_MODEL), BF16, *_rows(ft, tt), None)
    r = _rows(ft, tt)
    return _tmap(fn, (D_MODEL // ft, t // tt),
                 [(proj, *_rows(ft, tt, COL_HQ * HEAD // ft)), (proj, *_rows(ft, tt, COL_HF * HEAD // ft)), (lbl, (2, ft), lambda j, i: (0, j)),
                  (dq, *r), (dk, *r), (db, *r)],
                 [o, o, ((2, D_MODEL), F32, (2, ft), lambda j, i: (0, j), "inner")], "hgrn_prep_bwd")


def _hgrn_chunks(q, k, v, b, st):
    n = q[0].shape[0]
    half = n // 2
    srow = lax.broadcasted_iota(jnp.int32, (half, HEAD), 0)
    inter = _each(lambda q, b, st: _bdot(q * jnp.exp(b), st, "nt"), q, b, st)

    def below_scores(q, k, b):
        ref = b[half:half + 1, :]
        return _bdot(q[half:] * jnp.exp(jnp.minimum(b[half:] - ref, 0.0)), k[:half] * jnp.exp(jnp.minimum(ref - b[:half], 0.0)), "nt")

    below = _each(lambda a, v: _bdot(a, v[:half], "nn"), _each(below_scores, q, k, b), v)

    def diagonal(q, k, v, b):
        rows = []
        for lo in (0, half):
            qb, kb, vb, bb = (a[lo:lo + half] for a in (q, k, v, b))
            for t in range(half):
                e = jnp.where(srow <= t, jnp.exp(jnp.minimum(bb[t:t + 1, :] - bb, 0.0)), 0.0)
                a = jnp.sum(qb[t:t + 1, :] * kb * e, axis=1, keepdims=True)
                rows.append(jnp.sum(a * vb, axis=0, keepdims=True))
        return jnp.concatenate(rows, axis=0)

    diag = _each(diagonal, q, k, v, b)
    o = _each(lambda inter, diag, below: inter + diag + jnp.concatenate([jnp.zeros_like(below), below], axis=0), inter, diag, below)

    def new_state(k, v, b, st):
        bend = b[n - 1:n, :]
        return st * jnp.exp(bend) + _bdot(v, k * jnp.exp(bend - b), "tn")

    return o, _each(new_state, k, v, b, st)


HG_GROUP = 4
HG_PER = GDN_CHUNK // HG_CHUNK


def _hgrn_rec_fwd(q, k, proj, b):
    t = q.shape[0]
    nc = t // GDN_CHUNK
    blk = (GDN_CHUNK, HG_GROUP * HEAD)
    im = lambda h, c: (c, h)

    def body(q_ref, k_ref, v_ref, b_ref, o_ref, hs_ref, st_ref):
        @pl.when(pl.program_id(1) == 0)
        def _():
            st_ref[...] = jnp.zeros_like(st_ref)

        heads = range(HG_GROUP)
        for j in range(HG_PER):
            sl = pl.ds(HG_CHUNK * j, HG_CHUNK)
            st = tuple(st_ref[g] for g in heads)
            o, st_new = _hgrn_chunks(*[tuple(r[sl, _head_lanes(g)] for g in heads) for r in (q_ref, k_ref, v_ref, b_ref)], st)
            for g in heads:
                hs_ref[g, j] = st[g]
                o_ref[sl, _head_lanes(g)] = o[g]
                st_ref[g] = st_new[g]

    return pl.pallas_call(
        body, grid=(HG_HEADS // HG_GROUP, nc),
        in_specs=[pl.BlockSpec(blk, im), pl.BlockSpec(blk, im), pl.BlockSpec(blk, lambda h, c: (c, COL_HI // HG_GROUP + h)), pl.BlockSpec(blk, im)],
        out_specs=[pl.BlockSpec(blk, im), pl.BlockSpec((HG_GROUP, HG_PER, HEAD, HEAD), lambda h, c: (h, c, 0, 0))],
        out_shape=[SDS((t, D_MODEL), F32), SDS((HG_HEADS, nc * HG_PER, HEAD, HEAD), F32)],
        scratch_shapes=[pltpu.VMEM((HG_GROUP, HEAD, HEAD), F32)], name="hgrn_rec", compiler_params=_params(2),
    )(q, k, proj, b)


def _hgrn_rec_bwd(q, k, proj, b, hs, do):
    t = q.shape[0]
    nc = t // GDN_CHUNK
    blk = (GDN_CHUNK, HG_GROUP * HEAD)
    im = lambda h, c: (nc - 1 - c, h)

    def body(q_ref, k_ref, v_ref, b_ref, hs_ref, do_ref, dq_ref, dk_ref, dv_ref, db_ref, dst_ref):
        @pl.when(pl.program_id(1) == 0)
        def _():
            dst_ref[...] = jnp.zeros_like(dst_ref)

        heads = range(HG_GROUP)
        for j in reversed(range(HG_PER)):
            sl = pl.ds(HG_CHUNK * j, HG_CHUNK)
            _, vjp = jax.vjp(_hgrn_chunks, *[tuple(r[sl, _head_lanes(g)] for g in heads) for r in (q_ref, k_ref, v_ref, b_ref)],
                             tuple(hs_ref[g, j] for g in heads))
            dq, dk, dv, db, dst = vjp((tuple(do_ref[sl, _head_lanes(g)] for g in heads), tuple(dst_ref[g] for g in heads)))
            for g in heads:
                ln = _head_lanes(g)
                dq_ref[sl, ln] = dq[g]
                dk_ref[sl, ln] = dk[g]
                dv_ref[sl, ln] = dv[g].astype(dv_ref.dtype)
                db_ref[sl, ln] = db[g]
                dst_ref[g] = dst[g]

    spec = pl.BlockSpec(blk, im)
    return pl.pallas_call(
        body, grid=(HG_HEADS // HG_GROUP, nc),
        in_specs=[spec, spec, pl.BlockSpec(blk, lambda h, c: (nc - 1 - c, COL_HI // HG_GROUP + h)), spec,
                  pl.BlockSpec((HG_GROUP, HG_PER, HEAD, HEAD), lambda h, c: (h, nc - 1 - c, 0, 0)), spec],
        out_specs=[spec, spec, spec, spec],
        out_shape=[SDS((t, D_MODEL), F32), SDS((t, D_MODEL), F32), SDS((t, D_MODEL), BF16), SDS((t, D_MODEL), F32)],
        scratch_shapes=[pltpu.VMEM((HG_GROUP, HEAD, HEAD), F32)], name="hgrn_rec_bwd", compiler_params=_params(2),
    )(q, k, proj, b, hs, do)


def _shift_down(x, d):
    if d == 0:
        return x
    row = lax.broadcasted_iota(jnp.int32, x.shape, 0)
    return jnp.where(row >= d, pltpu.roll(x, d, 0), 0.0)


def _shift_up(x, d):
    if d == 0:
        return x
    n = x.shape[0]
    row = lax.broadcasted_iota(jnp.int32, x.shape, 0)
    return jnp.where(row < n - d, pltpu.roll(x, n - d, 0), 0.0)


def _conv_fwd(proj, conv_w):
    t = proj.shape[0]
    width = 2 * D_MODEL + 2 * D_MODEL

    def body(x_ref, w_ref, c_ref):
        x, w = x_ref[...], w_ref[...]
        y = w[CONV_K - 1:CONV_K, :] * x
        for j in range(CONV_K - 1):
            y = y + w[j:j + 1, :] * _shift_down(x, CONV_K - 1 - j)
        c_ref[...] = _silu(y)

    return pl.pallas_call(
        body, grid=(width // HEAD,),
        in_specs=[pl.BlockSpec((t, HEAD), lambda j: (0, COL_GQ + j)), pl.BlockSpec((CONV_K, HEAD), lambda j: (0, j))],
        out_specs=pl.BlockSpec((t, HEAD), lambda j: (0, j)), out_shape=SDS((t, width), F32),
        name="gdn_conv", compiler_params=_params(1),
    )(proj, conv_w)


def _conv_bwd(proj, conv_w, dc_qk, dc_v):
    t = proj.shape[0]
    n_qk = dc_qk.shape[1] // HEAD
    width = dc_qk.shape[1] + dc_v.shape[1]

    def body(x_ref, w_ref, dqk_ref, dv_ref, dx_ref, dw_ref):
        x, w = x_ref[...], w_ref[...]
        xs = [_shift_down(x, CONV_K - 1 - j) for j in range(CONV_K)]
        y = w[0:1, :] * xs[0]
        for j in range(1, CONV_K):
            y = y + w[j:j + 1, :] * xs[j]
        sg = jax.nn.sigmoid(y)
        dc = jnp.where(pl.program_id(0) < n_qk, dqk_ref[...], dv_ref[...])
        dy = dc * (sg * (1.0 + y * (1.0 - sg)))
        dx = w[CONV_K - 1:CONV_K, :] * dy
        for j in range(CONV_K - 1):
            dx = dx + w[j:j + 1, :] * _shift_up(dy, CONV_K - 1 - j)
        dx_ref[...] = dx.astype(dx_ref.dtype)
        dw_ref[...] = jnp.concatenate([jnp.sum(xs[j] * dy, axis=0, keepdims=True) for j in range(CONV_K)], axis=0)

    return pl.pallas_call(
        body, grid=(width // HEAD,),
        in_specs=[pl.BlockSpec((t, HEAD), lambda j: (0, COL_GQ + j)), pl.BlockSpec((CONV_K, HEAD), lambda j: (0, j)),
                  pl.BlockSpec((t, HEAD), lambda j: (0, jnp.minimum(j, n_qk - 1))), pl.BlockSpec((t, HEAD), lambda j: (0, jnp.maximum(j - n_qk, 0)))],
        out_specs=[pl.BlockSpec((t, HEAD), lambda j: (0, j)), pl.BlockSpec((CONV_K, HEAD), lambda j: (0, j))],
        out_shape=[SDS((t, width), BF16), SDS((CONV_K, width), F32)],
        name="gdn_conv_bwd", compiler_params=_params(1),
    )(proj, conv_w, dc_qk, dc_v)


def _l2norm(x, scale):
    return x * lax.rsqrt(jnp.sum(x * x, axis=-1, keepdims=True) + EPS) * scale


def _head(a, h):
    return a[:, h * HEAD:(h + 1) * HEAD]


def _qk_scale(h):
    return HEAD ** -0.5 if h < GDN_QK_HEADS else 1.0


def _qk_norm_fwd(c):
    t = c.shape[0]
    tt = _tile(t, (256, 128))
    width = 2 * D_MODEL

    def fn(x):
        return jnp.concatenate([_l2norm(_head(x, h), _qk_scale(h)) for h in range(2 * GDN_QK_HEADS)], axis=1)

    return _tmap(fn, (1, t // tt), [(c, *_rows(width, tt))], [((t, width), F32, *_rows(width, tt), None)], "gdn_qk_norm")[0]


def _qk_norm_bwd(c, dq_rep, dk_rep):
    t = c.shape[0]
    tt = _tile(t, (256, 128))
    width = 2 * D_MODEL

    def fn(x, dq2, dk2):
        out = []
        for h in range(2 * GDN_QK_HEADS):
            d2, hh = (dq2, h) if h < GDN_QK_HEADS else (dk2, h - GDN_QK_HEADS)
            _, vjp = jax.vjp(lambda x: _l2norm(x, _qk_scale(h)), _head(x, h))
            out.append(vjp(_head(d2, 2 * hh) + _head(d2, 2 * hh + 1))[0])
        return jnp.concatenate(out, axis=1)

    r = _rows(width, tt)
    return _tmap(fn, (1, t // tt), [(c, *r), (dq_rep, *r), (dk_rep, *r)], [((t, width), F32, *r, None)], "gdn_qk_norm_bwd")[0]


def _gdn_gates(x, alog, dtb):
    return -jnp.exp(alog) * _softplus(x + dtb), jax.nn.sigmoid(x)


def _gates_fwd(pab, alog, dtb):
    t = pab.shape[0]
    tt = _tile(t, (256, 128))

    def fn(x, alog, dtb):
        g, beta = _gdn_gates(x, alog, dtb)
        lane = lax.broadcasted_iota(jnp.int32, g.shape, 1)
        return jnp.where(lane < GDN_V_HEADS, _hdot_raw(_chunk_sum_matrix(tt, GDN_CHUNK), g), beta).T

    p = (alog, (1, HEAD), lambda j, i: (0, 0)), (dtb, (1, HEAD), lambda j, i: (0, 0))
    return _tmap(fn, (1, t // tt), [(pab, *_rows(HEAD, tt)), *p], [((HEAD, t), F32, (HEAD, tt), lambda j, i: (0, i), None)], "gdn_gates")[0]


def _gates_bwd(pab, alog, dtb, dout_t):
    t = pab.shape[0]
    tt = _tile(t, (256, 128))

    def fn(x, alog, dtb, dout_t):
        dout = dout_t.T
        lane = lax.broadcasted_iota(jnp.int32, dout.shape, 1)
        dgam = jnp.where(lane < GDN_V_HEADS, dout, 0.0)
        dbeta = jnp.where(jnp.logical_and(lane >= GDN_V_HEADS, lane < 2 * GDN_V_HEADS), dout, 0.0)
        dg = _hdot_raw(_chunk_sum_matrix(tt, GDN_CHUNK, transpose=True), dgam)
        _, vjp = jax.vjp(_gdn_gates, x, alog, dtb)
        return vjp((dg, dbeta))

    p = (alog, (1, HEAD), lambda j, i: (0, 0)), (dtb, (1, HEAD), lambda j, i: (0, 0))
    acc = ((1, HEAD), F32, (1, HEAD), lambda j, i: (0, 0), "inner")
    return _tmap(fn, (1, t // tt), [(pab, *_rows(HEAD, tt)), *p, (dout_t, (HEAD, tt), lambda j, i: (0, i))],
                 [((t, HEAD), BF16, *_rows(HEAD, tt), None), acc, acc], "gdn_gates_bwd")


def _split_bf16(x):
    hi = x.astype(BF16)
    return hi, (x - hi.astype(F32)).astype(BF16)


def _dot3(a, b):
    (ah, al), (bh, bl) = a, b
    return _bdot_raw(ah, bh, "nn") + (_bdot_raw(ah, bl, "nn") + _bdot_raw(al, bh, "nn"))


def _each(fn, *lists):
    return tuple(fn(*xs) for xs in zip(*lists))


def _unit_lower_inverses_raw(a):
    n = a[0].shape[0]
    row = lax.broadcasted_iota(jnp.int32, (n, n), 0)
    col = lax.broadcasted_iota(jnp.int32, (n, n), 1)
    eye = jnp.where(row == col, 1.0, 0.0).astype(F32)
    p = _each(lambda a: eye - a, a)
    x = _each(_split_bf16, a)
    m = 2
    while m < 2 * n:
        x = _each(_split_bf16, _each(_dot3, x, x))
        p = _each(lambda p, x: p + _dot3(_split_bf16(p), x), p, x)
        m *= 2
    return p


@jax.custom_vjp
def _unit_lower_inverses(a, known):
    return _unit_lower_inverses_raw(a) if known is None else known


def _uli_fwd(a, known):
    inv = _unit_lower_inverses(a, known)
    return inv, (inv, known)


def _uli_bwd(res, ct):
    inv, known = res
    right = _each(lambda ct, inv: _bdot_raw(ct, inv, "nt"), ct, inv)
    da = _each(lambda inv, r: -_bdot_raw(inv, r, "tn"), inv, right)
    return da, (None if known is None else _each(jnp.zeros_like, known))


_unit_lower_inverses.defvjp(_uli_fwd, _uli_bwd)


def _gdn_chunks(q, k, v, beta_row, gam_row, s, inv_known=None):
    n = q[0].shape[0]
    row = lax.broadcasted_iota(jnp.int32, (n, n), 0)
    col = lax.broadcasted_iota(jnp.int32, (n, n), 1)
    beta = _each(lambda r: r.T, beta_row)
    gam = _each(lambda r: r.T, gam_row)
    decay = _each(lambda gam, gam_row: jnp.where(row >= col, jnp.exp(jnp.minimum(gam - gam_row, 0.0)), 0.0), gam, gam_row)
    kb = _each(lambda k, beta: k * beta, k, beta)
    a = _each(lambda kb, k, decay: jnp.where(row > col, _bdot(kb, k, "nt") * decay, 0.0), kb, k, decay)
    inv = _unit_lower_inverses(a, inv_known)
    eg = _each(jnp.exp, gam)
    u = _each(lambda inv, v, beta: _bdot(inv, v * beta, "nn"), inv, v, beta)
    w = _each(lambda inv, kb, eg: _bdot(inv, kb * eg, "nn"), inv, kb, eg)
    qk = _each(lambda q, k, decay: _bdot(q, k, "nt") * decay, q, k, decay)
    v_new = _each(lambda u, w, s: u - _bdot(w, s, "nn"), u, w, s)
    o_state = _each(lambda q, eg, s: _bdot(q * eg, s, "nn"), q, eg, s)
    o = _each(lambda o_state, qk, v_new: o_state + _bdot(qk, v_new, "nn"), o_state, qk, v_new)
    gend = _each(lambda gam: gam[n - 1:n, :], gam)
    s_new = _each(lambda s, k, gam, gend, v_new: s * jnp.exp(gend) + _bdot(k * jnp.exp(gend - gam), v_new, "tn"), s, k, gam, gend, v_new)
    return o, s_new, inv


GDN_GROUP = 16


def _gdn_specs(nc, rev):
    cc = (lambda c: nc - 1 - c) if rev else (lambda c: c)
    grp = GDN_GROUP
    q = pl.BlockSpec((GDN_CHUNK, grp // 2 * HEAD), lambda h, c: (cc(c), h))
    k = pl.BlockSpec((GDN_CHUNK, grp // 2 * HEAD), lambda h, c: (cc(c), 2 * GDN_QK_HEADS // grp + h))
    v = pl.BlockSpec((GDN_CHUNK, grp * HEAD), lambda h, c: (cc(c), 2 * GDN_QK_HEADS // grp + h))
    o = pl.BlockSpec((GDN_CHUNK, grp * HEAD), lambda h, c: (cc(c), h))
    rw = pl.BlockSpec((grp, None, 1, GDN_CHUNK), lambda h, c: (h, cc(c), 0, 0))
    st = pl.BlockSpec((grp, None, HEAD, HEAD), lambda h, c: (h, cc(c), 0, 0))
    inv = pl.BlockSpec((grp, None, GDN_CHUNK, GDN_CHUNK), lambda h, c: (h, cc(c), 0, 0))
    return q, k, v, o, rw, st, inv


def _head_lanes(g, per=1):
    return pl.ds((g // per) * HEAD, HEAD)


def _gdn_rec_fwd(qk, c, beta_row, gam_row):
    t = qk.shape[0]
    nc = t // GDN_CHUNK
    q, k, v, o, rw, st, inv = _gdn_specs(nc, False)

    def body(q_ref, k_ref, v_ref, be_ref, gr_ref, o_ref, ss_ref, inv_ref, s_ref):
        @pl.when(pl.program_id(1) == 0)
        def _():
            s_ref[...] = jnp.zeros_like(s_ref)

        heads = range(GDN_GROUP)
        s = tuple(s_ref[g] for g in heads)
        out, s_new, inv_c = _gdn_chunks(
            tuple(q_ref[:, _head_lanes(g, 2)] for g in heads), tuple(k_ref[:, _head_lanes(g, 2)] for g in heads),
            tuple(v_ref[:, _head_lanes(g)] for g in heads), tuple(be_ref[g] for g in heads), tuple(gr_ref[g] for g in heads), s)
        for g in heads:
            ss_ref[g] = s[g]
            o_ref[:, _head_lanes(g)] = out[g]
            inv_ref[g] = inv_c[g]
            s_ref[g] = s_new[g]

    return pl.pallas_call(
        body, grid=(GDN_V_HEADS // GDN_GROUP, nc), in_specs=[q, k, v, rw, rw], out_specs=[o, st, inv],
        out_shape=[SDS((t, 2 * D_MODEL), F32), SDS((GDN_V_HEADS, nc, HEAD, HEAD), F32), SDS((GDN_V_HEADS, nc, GDN_CHUNK, GDN_CHUNK), F32)],
        scratch_shapes=[pltpu.VMEM((GDN_GROUP, HEAD, HEAD), F32)], name="gdn_rec", compiler_params=_params(2),
    )(qk, qk, c, beta_row, gam_row)


def _gdn_rec_bwd(qk, c, beta_row, gam_row, ss, invs, do):
    t = qk.shape[0]
    nc = t // GDN_CHUNK
    q, k, v, o, rw, st, inv = _gdn_specs(nc, True)

    def body(q_ref, k_ref, v_ref, be_ref, gr_ref, ss_ref, inv_ref, do_ref,
             dq_ref, dk_ref, dv_ref, dbe_ref, dgr_ref, ds_ref):
        @pl.when(pl.program_id(1) == 0)
        def _():
            ds_ref[...] = jnp.zeros_like(ds_ref)

        heads = range(GDN_GROUP)
        _, vjp = jax.vjp(
            _gdn_chunks,
            tuple(q_ref[:, _head_lanes(g, 2)] for g in heads), tuple(k_ref[:, _head_lanes(g, 2)] for g in heads),
            tuple(v_ref[:, _head_lanes(g)] for g in heads), tuple(be_ref[g] for g in heads), tuple(gr_ref[g] for g in heads),
            tuple(ss_ref[g] for g in heads), tuple(inv_ref[g] for g in heads))
        no_inv_ct = tuple(jnp.zeros((GDN_CHUNK, GDN_CHUNK), F32) for g in heads)
        dq, dk, dv, dbe, dgr, ds, _ = vjp((tuple(do_ref[:, _head_lanes(g)] for g in heads), tuple(ds_ref[g] for g in heads), no_inv_ct))
        for g in heads:
            dq_ref[:, _head_lanes(g)] = dq[g]
            dk_ref[:, _head_lanes(g)] = dk[g]
            dv_ref[:, _head_lanes(g)] = dv[g]
            dbe_ref[g] = dbe[g]
            dgr_ref[g] = dgr[g]
            ds_ref[g] = ds[g]

    wide = SDS((t, 2 * D_MODEL), F32)
    rowshape = SDS((GDN_V_HEADS, nc, 1, GDN_CHUNK), F32)
    return pl.pallas_call(
        body, grid=(GDN_V_HEADS // GDN_GROUP, nc), in_specs=[q, k, v, rw, rw, st, inv, o], out_specs=[o, o, o, rw, rw],
        out_shape=[wide, wide, wide, rowshape, rowshape],
        scratch_shapes=[pltpu.VMEM((GDN_GROUP, HEAD, HEAD), F32)], name="gdn_rec_bwd", compiler_params=_params(2),
    )(qk, qk, c, beta_row, gam_row, ss, invs, do)


def _gated_norm(o, gate, w):
    return _rms(o, w) * _silu(gate)


def _post_fwd(o, proj, col_off, w, name):
    t, width = o.shape
    tt = _tile(t, (256, 128))

    def fn(o, gate, w):
        return jnp.concatenate([_gated_norm(_head(o, h), _head(gate, h), w) for h in range(width // HEAD)], axis=1)

    return _tmap(fn, (1, t // tt),
                 [(o, *_rows(width, tt)), (proj, *_rows(width, tt, col_off * HEAD // width)), (w, (1, HEAD), lambda j, i: (0, 0))],
                 [((t, width), BF16, *_rows(width, tt), None)], name)[0]


def _post_bwd(o, proj, col_off, w, dout, name):
    t, width = o.shape
    tt = _tile(t, (256, 128))

    def fn(o, gate, w, dout):
        do, dgate, dw = [], [], jnp.zeros((1, HEAD), F32)
        for h in range(width // HEAD):
            _, vjp = jax.vjp(_gated_norm, _head(o, h), _head(gate, h), w)
            a, b, c = vjp(_head(dout, h))
            do.append(a)
            dgate.append(b)
            dw = dw + c
        return jnp.concatenate(do, axis=1), jnp.concatenate(dgate, axis=1), dw

    r = _rows(width, tt)
    return _tmap(fn, (1, t // tt),
                 [(o, *r), (proj, *_rows(width, tt, col_off * HEAD // width)), (w, (1, HEAD), lambda j, i: (0, 0)), (dout, *r)],
                 [((t, width), F32, *r, None), ((t, width), BF16, *r, None), ((1, HEAD), F32, (1, HEAD), lambda j, i: (0, 0), "inner")], name)


def _merge(gate_h, gate_g, yh, yg):
    return jax.nn.sigmoid(gate_h) * yh + jax.nn.sigmoid(gate_g) * yg


def _merge_fwd(proj, yh, yg):
    t = yh.shape[0]
    tt, ft = _tile(t, (256, 128)), 512
    r = _rows(ft, tt)
    return _tmap(_merge, (D_MODEL // ft, t // tt),
                 [(proj, *_rows(ft, tt, COL_GATE_H * HEAD // ft)), (proj, *_rows(ft, tt, COL_GATE_G * HEAD // ft)), (yh, *r), (yg, *r)],
                 [((t, D_MODEL), BF16, *r, None)], "merge")[0]


def _merge_bwd(proj, yh, yg, dy):
    t = yh.shape[0]
    tt, ft = _tile(t, (256, 128)), 512
    r = _rows(ft, tt)

    def fn(gate_h, gate_g, yh, yg, dy):
        _, vjp = jax.vjp(_merge, gate_h, gate_g, yh, yg)
        return vjp(dy)

    o = ((t, D_MODEL), BF16, *r, None)
    return _tmap(fn, (D_MODEL // ft, t // tt),
                 [(proj, *_rows(ft, tt, COL_GATE_H * HEAD // ft)), (proj, *_rows(ft, tt, COL_GATE_G * HEAD // ft)), (yh, *r), (yg, *r), (dy, *r)],
                 [o, o, o, o], "merge_bwd")


def _loss_head(h, target, g):
    t, d = h.shape
    tt = _tile(t, (256, 128))

    def fn(h, target, g):
        def f(h, g):
            err = _rms(h, g) - target
            return 0.5 * jnp.sum(jnp.mean(err * err, axis=-1))

        loss, (dh, dg) = jax.value_and_grad(f, (0, 1))(h, g)
        return dh, dg, jnp.full((1, HEAD), loss, F32)

    return _tmap(fn, (1, t // tt), [(h, *_rows(d, tt)), (target, *_rows(d, tt)), (g, (1, d), lambda j, i: (0, 0))],
                 [((t, d), F32, *_rows(d, tt), None), ((1, d), F32, (1, d), lambda j, i: (0, 0), "inner"),
                  ((1, HEAD), F32, (1, HEAD), lambda j, i: (0, 0), "inner")], "loss_head")


def _mixer_fwd(h, p, links):
    t = h.shape[0]
    nc = t // GDN_CHUNK
    u = _rms_fwd(h, p["mix_norm"], "mix_norm")
    w = {n: links.weight(n, h) for n in ("w_in_main_t", "w_in_ab_t", "conv_w")}
    proj = _mm(u, w["w_in_main_t"], "nt", F32, "mix_in", after=links.started)
    pab = _mm(u, w["w_in_ab_t"], "nt", F32, "mix_in_ab")
    qh, kh, bh = _hgrn_prep_fwd(proj, p["lbl"])
    oh, hs = _hgrn_rec_fwd(qh, kh, proj, bh)
    c = _conv_fwd(proj, w["conv_w"])
    qk = _qk_norm_fwd(c)
    gates_t = _gates_fwd(pab, p["alog"], p["dtb"])
    gam_row = gates_t[:GDN_V_HEADS].reshape(GDN_V_HEADS, nc, 1, GDN_CHUNK)
    beta_row = gates_t[GDN_V_HEADS:2 * GDN_V_HEADS].reshape(GDN_V_HEADS, nc, 1, GDN_CHUNK)
    og, ss, invs = _gdn_rec_fwd(qk, c, beta_row, gam_row)
    ohn = _post_fwd(oh, proj, COL_HG, p["hgrn_out_norm"], "hgrn_out")
    ogn = _post_fwd(og, proj, COL_GZ, p["gdn_out_norm"], "gdn_out")
    w.update({n: links.weight(n, ogn) for n in ("w_branch_hgrn", "w_branch_gdn", "w_out")})
    yh = _mm(ohn, w["w_branch_hgrn"], "nn", F32, "branch_hgrn")
    yg = _mm(ogn, w["w_branch_gdn"], "nn", F32, "branch_gdn")
    y = _merge_fwd(proj, yh, yg)
    out = _mm(y, w["w_out"], "nn", F32, "mix_out", res=h)
    saved = (w, u, proj, pab, qh, kh, bh, oh, hs, c, qk, beta_row, gam_row, og, ss, invs, ohn, ogn, yh, yg, y)
    return out, saved


def _mixer_bwd(h, p, links, saved, dout):
    (w, u, proj, pab, qh, kh, bh, oh, hs, c, qk, beta_row, gam_row, og, ss, invs, ohn, ogn, yh, yg, y) = saved
    t = h.shape[0]
    grads = {}
    dw_out = _mm(y, dout, "tn", BF16, "mix_out_dw")
    dy = _mm(dout, w["w_out"], "nt", F32, "mix_out_dx")
    dgate_h, dgate_g, dyh, dyg = _merge_bwd(proj, yh, yg, dy)
    dw_bh = _mm(ohn, dyh, "tn", BF16, "branch_hgrn_dw")
    dw_bg = _mm(ogn, dyg, "tn", BF16, "branch_gdn_dw")
    sent = links.send({"w_out": dw_out, "w_branch_hgrn": dw_bh, "w_branch_gdn": dw_bg})
    dohn = _mm(dyh, w["w_branch_hgrn"], "nt", F32, "branch_hgrn_dx", after=sent)
    dogn = _mm(dyg, w["w_branch_gdn"], "nt", F32, "branch_gdn_dx")
    doh, dhg, grads["hgrn_out_norm"] = _post_bwd(oh, proj, COL_HG, p["hgrn_out_norm"], dohn, "hgrn_out_bwd")
    dog, dgz, grads["gdn_out_norm"] = _post_bwd(og, proj, COL_GZ, p["gdn_out_norm"], dogn, "gdn_out_bwd")
    dqh, dkh, dhi, dbh = _hgrn_rec_bwd(qh, kh, proj, bh, hs, doh)
    dhq, dhf, grads["lbl"] = _hgrn_prep_bwd(proj, p["lbl"], dqh, dkh, dbh)
    dqv, dkv, dcv, dbeta_row, dgam_row = _gdn_rec_bwd(qk, c, beta_row, gam_row, ss, invs, dog)
    dcqk = _qk_norm_bwd(c, dqv, dkv)
    dxin, grads["conv_w"] = _conv_bwd(proj, w["conv_w"], dcqk, dcv)
    dgates_t = jnp.concatenate([dgam_row.reshape(GDN_V_HEADS, t), dbeta_row.reshape(GDN_V_HEADS, t),
                                jnp.zeros((HEAD - 2 * GDN_V_HEADS, t), F32)], axis=0)
    dpab, grads["alog"], grads["dtb"] = _gates_bwd(pab, p["alog"], p["dtb"], dgates_t)
    dproj = jnp.concatenate([dhq, dhf, dhi, dhg, dxin, dgz, dgate_h, dgate_g], axis=1)
    dw_main_t = _mm(dproj, u, "tn", BF16, "mix_in_dw")
    dw_ab_t = _mm(dpab, u, "tn", BF16, "mix_in_ab_dw")
    sent = links.send({"w_in": _w_in_join(dw_main_t, dw_ab_t)})
    du = _mm(dproj, w["w_in_main_t"], "nn", F32, "mix_in_dx", after=sent)
    du = _mm(dpab, w["w_in_ab_t"], "nn", F32, "mix_in_ab_dx", res=du)
    dh, grads["mix_norm"] = _rms_bwd(h, p["mix_norm"], du, dout, "mix_norm_bwd")
    return dh, grads


def _local_step(x, target, p, links):
    w1 = {n: links.weight(n, x) for n in ("ffn1_w_in", "ffn1_w_out")}
    h1, s1 = _ffn_fwd(x, p["ffn1_norm"], w1["ffn1_w_in"], w1["ffn1_w_out"], "ffn1", links.started)
    h2, sm = _mixer_fwd(h1, p, links)
    w2 = {n: links.weight(n, h2) for n in ("ffn2_w_in", "ffn2_w_out")}
    h3, s2 = _ffn_fwd(h2, p["ffn2_norm"], w2["ffn2_w_in"], w2["ffn2_w_out"], "ffn2", None)
    dh3, dfinal, loss = _loss_head(h3, target, p["final_norm"])
    g = {"final_norm": dfinal}
    dh2, g["ffn2_norm"] = _ffn_bwd(h2, p["ffn2_norm"], w2["ffn2_w_in"], w2["ffn2_w_out"], s2, dh3, "ffn2", links)
    dh1, gm = _mixer_bwd(h1, p, links, sm, dh2)
    g.update(gm)
    dx, g["ffn1_norm"] = _ffn_bwd(x, p["ffn1_norm"], w1["ffn1_w_in"], w1["ffn1_w_out"], s1, dh1, "ffn1", links)
    return loss, dx, g


def _exchange(items, name):
    n = len(items)
    out_shape = [SDS((N_DEV,) + a.shape if mode == "gather" else a.shape, a.dtype) for a, mode in items]

    def body(*refs):
        in_refs, out_refs = refs[:n], refs[n:2 * n]
        send_sems, recv_sems, local_sems = refs[2 * n:]
        x, y, c = lax.axis_index("x"), lax.axis_index("y"), lax.axis_index("c")
        me = 4 * x + 2 * y + c
        local = []
        for i, (_, mode) in enumerate(items):
            src = in_refs[i] if mode == "gather" else in_refs[i].at[me]
            cp = pltpu.make_async_copy(src, out_refs[i].at[me], local_sems.at[i])
            cp.start()
            local.append(cp)
        remote = []
        for rel in range(1, N_DEV):
            px = 1 - x if rel & 4 else x
            py = 1 - y if rel & 2 else y
            pc = 1 - c if rel & 1 else c
            peer = 4 * px + 2 * py + pc
            for i, (_, mode) in enumerate(items):
                src = in_refs[i] if mode == "gather" else in_refs[i].at[peer]
                cp = pltpu.make_async_remote_copy(src_ref=src, dst_ref=out_refs[i].at[me], send_sem=send_sems.at[i, rel - 1],
                                                  recv_sem=recv_sems.at[i, rel - 1], device_id=(px, py, pc), device_id_type=MESH_IDS)
                cp.start()
                remote.append(cp)
        for cp in remote:
            cp.wait()
        for cp in local:
            cp.wait()

    anyspace = pl.BlockSpec(memory_space=pl.ANY)
    return pl.pallas_call(
        body, in_specs=[anyspace] * n, out_specs=[anyspace] * n, out_shape=out_shape,
        scratch_shapes=[pltpu.SemaphoreType.DMA((n, N_DEV - 1)), pltpu.SemaphoreType.DMA((n, N_DEV - 1)), pltpu.SemaphoreType.DMA((n,))],
        name=name, compiler_params=pltpu.CompilerParams(has_side_effects=True),
    )(*[a for a, _ in items])


HBM_SPEC = pl.BlockSpec(memory_space=pltpu.HBM)
SEM_SPEC = pl.BlockSpec(memory_space=pltpu.SEMAPHORE)
DATAFLOW = pltpu.SideEffectType.DATAFLOW_SIDE_EFFECTING


def _position():
    x, y, c = lax.axis_index("x"), lax.axis_index("y"), lax.axis_index("c")
    return x, y, c, 4 * x + 2 * y + c


def _relations(x, y, c):
    for rel in range(1, N_DEV):
        px = 1 - x if rel & 4 else x
        py = 1 - y if rel & 2 else y
        pc = 1 - c if rel & 1 else c
        yield rel, (px, py, pc), 4 * px + 2 * py + pc


def _sem_index(item, rel):
    return item * (N_DEV - 1) + rel - 1


def _landing(a, mode):
    return lax.empty((N_DEV,) + a.shape if mode == "gather" else a.shape, a.dtype)


def _copies_start(groups, name):
    flat = [item for grp in groups for item in grp]
    n, ng = len(flat), len(groups)
    lands = [_landing(a, mode) for a, mode in flat]

    def body(*refs):
        src_refs, land_refs, sems, token = refs[:n], refs[n:2 * n], refs[2 * n:2 * n + 2 * ng], refs[-1]
        x, y, c, me = _position()
        for rel, where, peer in _relations(x, y, c):
            k = 0
            for gi, grp in enumerate(groups):
                for li, (_, mode) in enumerate(grp):
                    src = src_refs[k] if mode == "gather" else src_refs[k].at[peer]
                    pltpu.make_async_remote_copy(src_ref=src, dst_ref=land_refs[k].at[me], send_sem=sems[2 * gi].at[_sem_index(li, rel)],
                                                 recv_sem=sems[2 * gi + 1].at[_sem_index(li, rel)], device_id=where, device_id_type=MESH_IDS).start()
                    k += 1
        token[...] = jnp.zeros_like(token)

    sem_shapes = [pltpu.SemaphoreType.DMA((len(grp) * (N_DEV - 1),)) for grp in groups for _ in range(2)]
    thru = [pltpu.HBM(a.shape, a.dtype) for a, _ in flat] + [pltpu.HBM(l.shape, l.dtype) for l in lands]
    outs = pl.pallas_call(
        body, name=name, out_shape=(*sem_shapes, *thru, SDS((8, HEAD), F32)),
        in_specs=[HBM_SPEC] * (2 * n), out_specs=(*[SEM_SPEC] * (2 * ng), *[HBM_SPEC] * (2 * n), pl.BlockSpec(memory_space=pltpu.VMEM)),
        input_output_aliases={i: 2 * ng + i for i in range(2 * n)}, compiler_params=pltpu.CompilerParams(has_side_effects=DATAFLOW),
    )(*[pltpu.with_memory_space_constraint(a, pltpu.HBM) for a, _ in flat], *[pltpu.with_memory_space_constraint(l, pltpu.HBM) for l in lands])
    sems, srcs, landed, token = outs[:2 * ng], outs[2 * ng:2 * ng + n], outs[2 * ng + n:2 * ng + 2 * n], outs[-1]
    result, k = [], 0
    for gi, grp in enumerate(groups):
        result.append((sems[2 * gi], sems[2 * gi + 1], srcs[k:k + len(grp)], landed[k:k + len(grp)]))
        k += len(grp)
    return result, token


def _copies_wait(started, modes, after, name):
    send_sems, recv_sems, srcs, lands = started
    n = len(srcs)

    def body(*refs):
        src_refs, land_refs, ssem, rsem, token = refs[:n], refs[n:2 * n], refs[2 * n], refs[2 * n + 1], refs[-1]
        x, y, c, _ = _position()
        for rel in range(1, N_DEV):
            for i, mode in enumerate(modes):
                src = src_refs[i] if mode == "gather" else src_refs[i].at[0]
                cp = pltpu.make_async_remote_copy(src_ref=src, dst_ref=land_refs[i].at[0], send_sem=ssem.at[_sem_index(i, rel)],
                                                  recv_sem=rsem.at[_sem_index(i, rel)], device_id=(x, y, c), device_id_type=MESH_IDS)
                cp.wait_send()
                cp.wait_recv()
        token[...] = jnp.zeros_like(token)

    outs = pl.pallas_call(
        body, name=name, out_shape=[pltpu.HBM(a.shape, a.dtype) for a in (*srcs, *lands)] + [SDS((8, HEAD), F32)],
        in_specs=[HBM_SPEC] * (2 * n) + [SEM_SPEC, SEM_SPEC, pl.BlockSpec(memory_space=pl.ANY)],
        out_specs=[HBM_SPEC] * (2 * n) + [pl.BlockSpec(memory_space=pltpu.VMEM)],
        input_output_aliases={i: i for i in range(2 * n)}, compiler_params=pltpu.CompilerParams(has_side_effects=DATAFLOW),
    )(*srcs, *lands, send_sems, recv_sems, after)
    return outs[:n], outs[n:2 * n], outs[-1]


WEIGHT_GROUPS = (("ffn1_w_in", "ffn1_w_out", "gdn_conv_w"), ("w_in",), ("w_branch_hgrn", "w_branch_gdn", "w_out", "ffn2_w_in", "ffn2_w_out"))


class _Links:
    def __init__(self, shards, me):
        self.me = me
        self.shards = shards
        self.weights = {}
        self.sends = []
        self.gathers = {}
        self.started = None
        self._start_gather(0, None)

    def _start_gather(self, gi, zeros):
        if gi < len(WEIGHT_GROUPS):
            items = [(self.shards[n] if zeros is None else self.shards[n] + zeros[0, 0].astype(self.shards[n].dtype), "gather")
                     for n in WEIGHT_GROUPS[gi]]
            started, self.started = _copies_start([items], "gather_start_%d" % gi)
            self.gathers[gi] = started[0]

    def weight(self, name, after):
        if name not in self.weights:
            gi = [i for i, grp in enumerate(WEIGHT_GROUPS) if {"w_in_main_t": "w_in", "w_in_ab_t": "w_in", "conv_w": "gdn_conv_w"}.get(name, name) in grp][0]
            assert gi in self.gathers, "weight groups are asked for in order"
            srcs, lands, zero = _copies_wait(self.gathers[gi], ["gather"] * len(WEIGHT_GROUPS[gi]), after, "gather_wait_%d" % gi)
            self._start_gather(gi + 1, zero)
            for n, src, land in zip(WEIGHT_GROUPS[gi], srcs, lands):
                full = lax.dynamic_update_index_in_dim(land, src, self.me, 0)
                if n == "gdn_conv_w":
                    self.weights["conv_w"] = full.reshape(N_DEV, CONV_K, 4 * D_MODEL // N_DEV).transpose(1, 0, 2).reshape(CONV_K, 4 * D_MODEL)
                elif n == "w_in":
                    self.weights["w_in_main_t"], self.weights["w_in_ab_t"] = _w_in_split(full.reshape(-1, D_MODEL))
                else:
                    self.weights[n] = full.reshape(-1, D_MODEL)
        return self.weights[name]

    def send(self, grads):
        names = list(grads)
        blocks = [grads[n].reshape(N_DEV, -1, D_MODEL) for n in names]
        started, token = _copies_start([[(b, "scatter") for b in blocks]], "send_" + names[0])
        self.sends.append((names, started[0]))
        return token

    def landed(self, after):
        out = {}
        for names, started in self.sends:
            srcs, lands, _ = _copies_wait(started, ["scatter"] * len(names), after, "landed_" + names[0])
            for n, src, land in zip(names, srcs, lands):
                out[n] = lax.dynamic_update_index_in_dim(land, lax.dynamic_index_in_dim(src, self.me, 0, keepdims=False), self.me, 0)
        return out


def _adam(parts, w, m, v, name):
    n_parts, r, c = parts.shape
    tc = c if c <= 512 else (256 if r > 1024 else 512)

    def body(p_ref, w_ref, m_ref, v_ref, g_ref, d_ref, mo_ref, vo_ref):
        g = p_ref[0].astype(F32)
        for i in range(1, n_parts):
            g = g + p_ref[i].astype(F32)
        m_new = ADAM_B1 * m_ref[...] + (1.0 - ADAM_B1) * g
        v_new = ADAM_B2 * v_ref[...] + (1.0 - ADAM_B2) * (g * g)
        m_hat = m_new / (1.0 - ADAM_B1 ** ADAM_STEP)
        v_hat = v_new / (1.0 - ADAM_B2 ** ADAM_STEP)
        g_ref[...] = g
        d_ref[...] = -ADAM_LR * (m_hat / (jnp.sqrt(v_hat) + ADAM_EPS) + ADAM_WD * w_ref[...])
        mo_ref[...] = m_new
        vo_ref[...] = v_new

    spec = pl.BlockSpec((r, tc), lambda j: (0, j))
    return pl.pallas_call(
        body, grid=(c // tc,), in_specs=[pl.BlockSpec((n_parts, r, tc), lambda j: (0, 0, j)), spec, spec, spec],
        out_specs=[spec] * 4, out_shape=[SDS((r, c), F32)] * 4, name=name, compiler_params=_params(1),
    )(parts, w, m, v)


BIG = ("ffn1_w_in", "ffn1_w_out", "w_in", "w_branch_hgrn", "w_branch_gdn", "w_out", "ffn2_w_in", "ffn2_w_out")


TRANSPOSED = ("ffn1_w_in", "w_in", "ffn2_w_in")


def _shard_rows(name, shard):
    return shard.T if name in TRANSPOSED else shard


SCALAR_ROWS = 8192
N_SCALAR = 2 * GDN_V_HEADS


def _w_in_split(w_in_t):
    main = jnp.concatenate([w_in_t[:SCALAR_ROWS], w_in_t[SCALAR_ROWS + N_SCALAR:]], axis=0)
    ab = jnp.pad(w_in_t[SCALAR_ROWS:SCALAR_ROWS + N_SCALAR], ((0, HEAD - N_SCALAR), (0, 0)))
    return main, ab


def _w_in_join(dmain_t, dab_t):
    return jnp.concatenate([dmain_t[:SCALAR_ROWS], dab_t[:N_SCALAR], dmain_t[SCALAR_ROWS:]], axis=0)


def _pad_lanes(a, width=HEAD):
    return jnp.pad(a, ((0, 0), (0, width - a.shape[1])))


SMALL_ROWS = 24


def _pack_small(g, loss):
    row6 = jnp.concatenate([g["hgrn_out_norm"], g["gdn_out_norm"], g["alog"], g["dtb"], loss,
                            jnp.zeros((1, D_MODEL - 5 * HEAD), F32)], axis=1)
    return jnp.concatenate([g["ffn1_norm"], g["mix_norm"], g["lbl"], g["ffn2_norm"], g["final_norm"], row6,
                            jnp.zeros((1, D_MODEL), F32), g["conv_w"].reshape(4 * CONV_K, D_MODEL)], axis=0)


def _pack_small_state(a):
    row6 = jnp.concatenate([a["hgrn_out_norm"], a["gdn_out_norm"], _pad_lanes(a["gdn_a_log"]), _pad_lanes(a["gdn_dt_bias"]),
                            jnp.zeros((1, D_MODEL - 4 * HEAD), F32)], axis=1)
    return jnp.concatenate([a["ffn1_norm"], a["mix_norm"], a["hgrn_lb_logits"], a["ffn2_norm"], a["final_norm"].reshape(1, D_MODEL),
                            row6, jnp.zeros((1, D_MODEL), F32)], axis=0)


def _unpack_small(a):
    return {"ffn1_norm": a[0:1], "mix_norm": a[1:2], "hgrn_lb_logits": a[2:4], "ffn2_norm": a[4:5], "final_norm": a[5],
            "hgrn_out_norm": a[6:7, :HEAD], "gdn_out_norm": a[6:7, HEAD:2 * HEAD],
            "gdn_a_log": a[6:7, 2 * HEAD:2 * HEAD + GDN_V_HEADS], "gdn_dt_bias": a[6:7, 3 * HEAD:3 * HEAD + GDN_V_HEADS]}


NAMES = ("ffn1_norm", "ffn1_w_in", "ffn1_w_out", "mix_norm", "w_in", "hgrn_lb_logits", "hgrn_out_norm", "gdn_conv_w", "gdn_a_log",
         "gdn_dt_bias", "gdn_out_norm", "w_branch_hgrn", "w_branch_gdn", "w_out", "ffn2_norm", "ffn2_w_in", "ffn2_w_out", "final_norm")


def kernel(x, ffn1_norm, ffn1_w_in, ffn1_w_out, mix_norm, w_in, hgrn_lb_logits, hgrn_out_norm, gdn_conv_w, gdn_a_log, gdn_dt_bias, gdn_out_norm, w_branch_hgrn, w_branch_gdn, w_out, ffn2_norm, ffn2_w_in, ffn2_w_out, final_norm, loss_target, m_ffn1_norm, m_ffn1_w_in, m_ffn1_w_out, m_mix_norm, m_w_in, m_hgrn_lb_logits, m_hgrn_out_norm, m_gdn_conv_w, m_gdn_a_log, m_gdn_dt_bias, m_gdn_out_norm, m_w_branch_hgrn, m_w_branch_gdn, m_w_out, m_ffn2_norm, m_ffn2_w_in, m_ffn2_w_out, m_final_norm, v_ffn1_norm, v_ffn1_w_in, v_ffn1_w_out, v_mix_norm, v_w_in, v_hgrn_lb_logits, v_hgrn_out_norm, v_gdn_conv_w, v_gdn_a_log, v_gdn_dt_bias, v_gdn_out_norm, v_w_branch_hgrn, v_w_branch_gdn, v_w_out, v_ffn2_norm, v_ffn2_w_in, v_ffn2_w_out, v_final_norm):
    wts = dict(zip(NAMES, (ffn1_norm, ffn1_w_in, ffn1_w_out, mix_norm, w_in, hgrn_lb_logits, hgrn_out_norm, gdn_conv_w, gdn_a_log,
                           gdn_dt_bias, gdn_out_norm, w_branch_hgrn, w_branch_gdn, w_out, ffn2_norm, ffn2_w_in, ffn2_w_out, final_norm)))
    mom = dict(zip(NAMES, (m_ffn1_norm, m_ffn1_w_in, m_ffn1_w_out, m_mix_norm, m_w_in, m_hgrn_lb_logits, m_hgrn_out_norm, m_gdn_conv_w,
                           m_gdn_a_log, m_gdn_dt_bias, m_gdn_out_norm, m_w_branch_hgrn, m_w_branch_gdn, m_w_out, m_ffn2_norm, m_ffn2_w_in,
                           m_ffn2_w_out, m_final_norm)))
    var = dict(zip(NAMES, (v_ffn1_norm, v_ffn1_w_in, v_ffn1_w_out, v_mix_norm, v_w_in, v_hgrn_lb_logits, v_hgrn_out_norm, v_gdn_conv_w,
                           v_gdn_a_log, v_gdn_dt_bias, v_gdn_out_norm, v_w_branch_hgrn, v_w_branch_gdn, v_w_out, v_ffn2_norm, v_ffn2_w_in,
                           v_ffn2_w_out, v_final_norm)))
    me = 4 * lax.axis_index("x") + 2 * lax.axis_index("y") + lax.axis_index("c")

    conv_shard = wts["gdn_conv_w"][0]
    shards = {n: _shard_rows(n, wts[n][0]).astype(BF16) for n in BIG}
    shards["gdn_conv_w"] = conv_shard.reshape(2, D_MODEL)
    links = _Links(shards, me)
    p = {"ffn1_norm": wts["ffn1_norm"], "mix_norm": wts["mix_norm"], "ffn2_norm": wts["ffn2_norm"], "final_norm": wts["final_norm"].reshape(1, D_MODEL),
         "lbl": wts["hgrn_lb_logits"], "hgrn_out_norm": wts["hgrn_out_norm"], "gdn_out_norm": wts["gdn_out_norm"],
         "alog": _pad_lanes(wts["gdn_a_log"]), "dtb": _pad_lanes(wts["gdn_dt_bias"])}

    loss, dx, g = _local_step(x[0], loss_target[0], p, links)

    small_parts = _exchange([(_pack_small(g, loss), "gather")], "gather_small")[0]
    landed = links.landed(small_parts)

    big = [{} for _ in range(4)]
    for n in BIG:
        res = _adam(landed[n], _shard_rows(n, wts[n][0]), _shard_rows(n, mom[n][0]), _shard_rows(n, var[n][0]), "adam_" + n)
        for kind in range(4):
            big[kind][n] = _shard_rows(n, res[kind])
    n_vec = SMALL_ROWS - 4 * CONV_K
    small_raw = _adam(small_parts[:, :n_vec], _pack_small_state(wts), _pack_small_state(mom), _pack_small_state(var), "adam_small")
    small = [_unpack_small(o) for o in small_raw]
    loss_total = small_raw[0][6, 4 * HEAD]
    conv_parts = small_parts[:, n_vec:].reshape(N_DEV, CONV_K, 4 * D_MODEL)
    width = 4 * D_MODEL // N_DEV
    conv_mine = lax.dynamic_slice_in_dim(conv_parts, me * width, width, axis=2)
    conv = _adam(conv_mine, conv_shard, mom["gdn_conv_w"][0], var["gdn_conv_w"][0], "adam_conv")

    outs = []
    for kind in range(4):
        for n in NAMES:
            if n in BIG:
                outs.append(big[kind][n][None])
            elif n == "gdn_conv_w":
                outs.append(conv[kind][None])
            else:
                outs.append(small[kind][n])
    return (loss_total, dx[None], *outs)
```

```python
import functools

import jax
import jax.numpy as jnp
from jax import lax
from jax.experimental import pallas as pl
from jax.experimental.pallas import tpu as pltpu

F32 = jnp.float32
BF16 = jnp.bfloat16
HIGHEST = lax.Precision.HIGHEST
MESH_IDS = pl.DeviceIdType.MESH

D_MODEL = 1024
D_FF = 2816
N_DEV = 8
EPS = 1e-6
HEAD = 128
HG_HEADS = 8
GDN_QK_HEADS = 8
GDN_V_HEADS = 16
GDN_CHUNK = 64
HG_CHUNK = 16
CONV_K = 4
IN_WIDTH = 12320
COL_HQ, COL_HF, COL_HI, COL_HG, COL_GQ, COL_GK, COL_GV = 0, 8, 16, 24, 32, 40, 48
COL_GZ, COL_GATE_H, COL_GATE_G = 0, 16, 24
VMEM_LIMIT = 56 * 1024 * 1024

ADAM_LR, ADAM_B1, ADAM_B2, ADAM_EPS, ADAM_WD, ADAM_STEP = 0.001, 0.9, 0.999, 1e-08, 0.01, 10

SDS = jax.ShapeDtypeStruct


def _params(n_axes):
    return pltpu.CompilerParams(dimension_semantics=("arbitrary",) * n_axes, vmem_limit_bytes=VMEM_LIMIT)


def _tile(n, candidates=(512, 384, 256, 128, 64, 32, 16, 8)):
    for c in candidates:
        if n % c == 0:
            return c
    return n


_DIMS = {"nn": ((1,), (0,)), "nt": ((1,), (1,)), "tn": ((0,), (0,))}


def _bdot_raw(a, b, dims):
    return lax.dot_general(a.astype(BF16), b.astype(BF16), (_DIMS[dims], ((), ())), preferred_element_type=F32)


@functools.partial(jax.custom_vjp, nondiff_argnums=(2,))
def _bdot(a, b, dims):
    return _bdot_raw(a, b, dims)


def _bdot_fwd(a, b, dims):
    return _bdot_raw(a, b, dims), (a, b)


def _bdot_bwd(dims, res, ct):
    a, b = res
    if dims == "nn":
        return _bdot_raw(ct, b, "nt"), _bdot_raw(a, ct, "tn")
    if dims == "nt":
        return _bdot_raw(ct, b, "nn"), _bdot_raw(ct, a, "tn")
    return _bdot_raw(b, ct, "nt"), _bdot_raw(a, ct, "nn")


_bdot.defvjp(_bdot_fwd, _bdot_bwd)


def _hdot_raw(a, b):
    return jnp.dot(a, b, precision=HIGHEST, preferred_element_type=F32)


MM_VMEM_BUDGET = 30 * 1024 * 1024
TOKEN = (8, HEAD)


def _mm_tiles(m, n, k, a_bytes, b_bytes, o_bytes, r_bytes):
    tm = _tile(m, (1408, 1024, 704, 512, 256, 128, 64, 32, 16, 8))
    tn = _tile(n, (1408, 1024, 512, 256, 128))
    tk = _tile(k, (2048, 1408, 1024, 512, 256, 128, 64, 32, 16, 8))

    def need(tm, tn, tk):
        return 2 * (tm * tk * a_bytes + tk * tn * b_bytes + tm * tn * (o_bytes + r_bytes)) + tm * tn * 4

    while need(tm, tn, tk) > MM_VMEM_BUDGET:
        if tk > 512 and tk % 256 == 0:
            tk //= 2
        elif tn > 512 and tn % 256 == 0:
            tn //= 2
        elif tm > 256:
            tm //= 2
        else:
            break
    return tm, tn, tk


def _mm(a, b, dims, out_dtype, name, res=None, alpha=1.0, after=None, b_rows=None):
    b_shape = b.shape if b_rows is None else (b_rows, b.shape[1])
    if dims == "nn":
        (m, k), (k2, n) = a.shape, b_shape
    elif dims == "nt":
        (m, k), (n, k2) = a.shape, b_shape
    else:
        (k, m), (k2, n) = a.shape, b_shape
    assert k == k2, (a.shape, b.shape, dims)
    has_res = res is not None
    tm, tn, tk = _mm_tiles(m, n, k, a.dtype.itemsize, b.dtype.itemsize, jnp.dtype(out_dtype).itemsize, res.dtype.itemsize if has_res else 0)
    nk = k // tk
    a_spec = pl.BlockSpec((tk, tm), lambda i, j, kk: (kk, i)) if dims == "tn" else pl.BlockSpec((tm, tk), lambda i, j, kk: (i, kk))
    b_spec = pl.BlockSpec((tn, tk), lambda i, j, kk: (j, kk)) if dims == "nt" else pl.BlockSpec((tk, tn), lambda i, j, kk: (kk, j))
    o_spec = pl.BlockSpec((tm, tn), lambda i, j, kk: (i, j))

    def finish(acc, r_ref, o_ref):
        out = acc * alpha if alpha != 1.0 else acc
        if has_res:
            out = r_ref[...].astype(F32) + out
        o_ref[...] = out.astype(o_ref.dtype)

    n_in = 2 + has_res + (after is not None)

    def body(*refs):
        a_ref, b_ref = refs[:2]
        r_ref = refs[2] if has_res else None
        o_ref = refs[n_in]
        p = _bdot_raw(a_ref[...], b_ref[...], dims)
        if nk == 1:
            finish(p, r_ref, o_ref)
            return
        acc_ref = refs[-1]
        kk = pl.program_id(2)

        @pl.when(kk == 0)
        def _():
            acc_ref[...] = p

        @pl.when(kk > 0)
        def _():
            acc_ref[...] += p

        @pl.when(kk == nk - 1)
        def _():
            finish(acc_ref[...], r_ref, o_ref)

    args = (a, b) + ((res,) if has_res else ()) + ((after,) if after is not None else ())
    in_specs = [a_spec, b_spec] + ([o_spec] if has_res else []) + ([pl.BlockSpec(TOKEN, lambda i, j, kk: (0, 0))] if after is not None else [])
    return pl.pallas_call(
        body, grid=(m // tm, n // tn, nk), in_specs=in_specs, out_specs=o_spec, out_shape=SDS((m, n), out_dtype),
        scratch_shapes=[pltpu.VMEM((tm, tn), F32)] if nk > 1 else [], name=name, compiler_params=_params(3),
    )(*args)


def _tmap(fn, grid, ins, outs, name):
    n_in = len(ins)
    n_ax = len(grid)

    def body(*refs):
        vals = fn(*[r[...] for r in refs[:n_in]])
        if not isinstance(vals, (tuple, list)):
            vals = (vals,)
        first_inner = pl.program_id(n_ax - 1) == 0
        first_all = first_inner
        for ax in range(n_ax - 1):
            first_all = jnp.logical_and(first_all, pl.program_id(ax) == 0)

        def put(ref, val, acc):
            val = val.astype(ref.dtype)
            if acc is None:
                ref[...] = val
                return
            first = first_inner if acc == "inner" else first_all

            @pl.when(first)
            def _():
                ref[...] = val

            @pl.when(jnp.logical_not(first))
            def _():
                ref[...] += val

        for ref, val, o in zip(refs[n_in:], vals, outs):
            put(ref, val, o[4])

    return pl.pallas_call(
        body, grid=grid,
        in_specs=[pl.BlockSpec(bs, im) for _, bs, im in ins],
        out_specs=[pl.BlockSpec(o[2], o[3]) for o in outs],
        out_shape=[SDS(o[0], o[1]) for o in outs],
        name=name, compiler_params=_params(n_ax),
    )(*[a for a, _, _ in ins])


def _rows(width, tt, off=0):
    return (tt, width), (lambda j, i: (i, off + j))


def _rms(x, g):
    x = x.astype(F32)
    return x * lax.rsqrt(jnp.mean(x * x, axis=-1, keepdims=True) + EPS) * g


def _silu(x):
    return x * jax.nn.sigmoid(x)


def _softplus(x):
    return jnp.maximum(x, 0.0) + jnp.log1p(jnp.exp(-jnp.abs(x)))


def _rms_fwd(x, g, name):
    t, d = x.shape
    tt = _tile(t, (256, 128))
    return _tmap(_rms, (1, t // tt), [(x, *_rows(d, tt)), (g, (1, d), lambda j, i: (0, 0))],
                 [((t, d), BF16, *_rows(d, tt), None)], name)[0]


def _rms_bwd(x, g, dn, dres, name):
    t, d = x.shape
    tt = _tile(t, (256, 128))

    def fn(x, g, dn, dres):
        _, vjp = jax.vjp(_rms, x, g)
        dx, dg = vjp(dn.astype(F32))
        return dres + dx, dg

    return _tmap(fn, (1, t // tt),
                 [(x, *_rows(d, tt)), (g, (1, d), lambda j, i: (0, 0)), (dn, *_rows(d, tt)), (dres, *_rows(d, tt))],
                 [((t, d), F32, *_rows(d, tt), None), ((1, d), F32, (1, d), lambda j, i: (0, 0), "inner")], name)


def _swiglu(ab):
    return _silu(ab[:, :D_FF]) * ab[:, D_FF:]


def _swiglu_fwd(ab, name):
    t = ab.shape[0]
    tt = _tile(t, (128,))
    return _tmap(_swiglu, (1, t // tt), [(ab, *_rows(2 * D_FF, tt))], [((t, D_FF), BF16, *_rows(D_FF, tt), None)], name)[0]


def _swiglu_bwd(ab, ds, name):
    t = ab.shape[0]
    tt = _tile(t, (128,))

    def fn(ab, ds):
        a, b = ab[:, :D_FF], ab[:, D_FF:]
        _, vjp = jax.vjp(lambda a, b: _silu(a) * b, a, b)
        da, db = vjp(ds.astype(F32))
        return jnp.concatenate([da, db], axis=1)

    return _tmap(fn, (1, t // tt), [(ab, *_rows(2 * D_FF, tt)), (ds, *_rows(D_FF, tt))],
                 [((t, 2 * D_FF), BF16, *_rows(2 * D_FF, tt), None)], name)[0]


def _ffn_fwd(h, g, w_in_t, w_out, tag, after):
    n = _rms_fwd(h, g, tag + "_norm")
    ab = _mm(n, w_in_t, "nt", F32, tag + "_in", after=after)
    s = _swiglu_fwd(ab, tag + "_act")
    out = _mm(s, w_out, "nn", F32, tag + "_out", res=h, alpha=0.5)
    return out, (n, ab, s)


def _ffn_bwd(h, g, w_in_t, w_out, saved, dout, tag, links):
    n, ab, s = saved
    sent = links.send({tag + "_w_out": _mm(s, dout, "tn", BF16, tag + "_dw_out", alpha=0.5)})
    ds = _mm(dout, w_out, "nt", F32, tag + "_ds", alpha=0.5, after=sent)
    dab = _swiglu_bwd(ab, ds, tag + "_dact")
    sent = links.send({tag + "_w_in": _mm(dab, n, "tn", BF16, tag + "_dw_in")})
    dn = _mm(dab, w_in_t, "nn", F32, tag + "_dn", after=sent)
    return _rms_bwd(h, g, dn, dout, tag + "_dnorm")


def _chunk_sum_matrix(n, chunk, transpose=False):
    row = lax.broadcasted_iota(jnp.int32, (n, n), 0)
    col = lax.broadcasted_iota(jnp.int32, (n, n), 1)
    if transpose:
        row, col = col, row
    return jnp.where(jnp.logical_and(col <= row, row // chunk == col // chunk), 1.0, 0.0).astype(F32)


def _hgrn_gates(hq, hf, lbl):
    lb = jax.nn.sigmoid(lbl[0:1, :] - lbl[1:2, :])
    sg = jax.nn.sigmoid(hf)
    f = lb + (1.0 - lb) * sg
    q = _silu(hq) * HEAD ** -0.5
    k = (1.0 - lb) * (1.0 - sg)
    return q, k, jnp.log(f)


def _hgrn_prep_fwd(proj, lbl):
    t = proj.shape[0]
    tt, ft = _tile(t, (256, 128)), 512

    def fn(hq, hf, lbl):
        q, k, log_f = _hgrn_gates(hq, hf, lbl)
        return q, k, _hdot_raw(_chunk_sum_matrix(tt, HG_CHUNK), log_f)

    o = ((t, D_MODEL), F32, *_rows(ft, tt), None)
    return _tmap(fn, (D_MODEL // ft, t // tt),
                 [(proj, *_rows(ft, tt, COL_HQ * HEAD // ft)), (proj, *_rows(ft, tt, COL_HF * HEAD // ft)), (lbl, (2, ft), lambda j, i: (0, j))],
                 [o, o, o], "hgrn_prep")


def _hgrn_prep_bwd(proj, lbl, dq, dk, db):
    t = proj.shape[0]
    tt, ft = _tile(t, (256, 128)), 512

    def fn(hq, hf, lbl, dq, dk, db):
        dlog_f = _hdot_raw(_chunk_sum_matrix(tt, HG_CHUNK, transpose=True), db)
        _, vjp = jax.vjp(_hgrn_gates, hq, hf, lbl)
        return vjp((dq, dk, dlog_f))

    o = ((t, D_MODEL), BF16, *_rows(ft, tt), None)
    r = _rows(ft, tt)
    return _tmap(fn, (D_MODEL // ft, t // tt),
                 [(proj, *_rows(ft, tt, COL_HQ * HEAD // ft)), (proj, *_rows(ft, tt, COL_HF * HEAD // ft)), (lbl, (2, ft), lambda j, i: (0, j)),
                  (dq, *r), (dk, *r), (db, *r)],
                 [o, o, ((2, D_MODEL), F32, (2, ft), lambda j, i: (0, j), "inner")], "hgrn_prep_bwd")


def _hgrn_chunks(q, k, v, b, st):
    n = q[0].shape[0]
    half = n // 2
    srow = lax.broadcasted_iota(jnp.int32, (half, HEAD), 0)
    inter = _each(lambda q, b, st: _bdot(q * jnp.exp(b), st, "nt"), q, b, st)

    def below_scores(q, k, b):
        ref = b[half:half + 1, :]
        return _bdot(q[half:] * jnp.exp(jnp.minimum(b[half:] - ref, 0.0)), k[:half] * jnp.exp(jnp.minimum(ref - b[:half], 0.0)), "nt")

    below = _each(lambda a, v: _bdot(a, v[:half], "nn"), _each(below_scores, q, k, b), v)

    def diagonal(q, k, v, b):
        rows = []
        for lo in (0, half):
            qb, kb, vb, bb = (a[lo:lo + half] for a in (q, k, v, b))
            for t in range(half):
                e = jnp.where(srow <= t, jnp.exp(jnp.minimum(bb[t:t + 1, :] - bb, 0.0)), 0.0)
                a = jnp.sum(qb[t:t + 1, :] * kb * e, axis=1, keepdims=True)
                rows.append(jnp.sum(a * vb, axis=0, keepdims=True))
        return jnp.concatenate(rows, axis=0)

    diag = _each(diagonal, q, k, v, b)
    o = _each(lambda inter, diag, below: inter + diag + jnp.concatenate([jnp.zeros_like(below), below], axis=0), inter, diag, below)

    def new_state(k, v, b, st):
        bend = b[n - 1:n, :]
        return st * jnp.exp(bend) + _bdot(v, k * jnp.exp(bend - b), "tn")

    return o, _each(new_state, k, v, b, st)


HG_GROUP = 4
HG_PER = GDN_CHUNK // HG_CHUNK


def _hgrn_rec_fwd(q, k, proj, b):
    t = q.shape[0]
    nc = t // GDN_CHUNK
    blk = (GDN_CHUNK, HG_GROUP * HEAD)
    im = lambda h, c: (c, h)

    def body(q_ref, k_ref, v_ref, b_ref, o_ref, hs_ref, st_ref):
        @pl.when(pl.program_id(1) == 0)
        def _():
            st_ref[...] = jnp.zeros_like(st_ref)

        heads = range(HG_GROUP)
        for j in range(HG_PER):
            sl = pl.ds(HG_CHUNK * j, HG_CHUNK)
            st = tuple(st_ref[g] for g in heads)
            o, st_new = _hgrn_chunks(*[tuple(r[sl, _head_lanes(g)] for g in heads) for r in (q_ref, k_ref, v_ref, b_ref)], st)
            for g in heads:
                hs_ref[g, j] = st[g]
                o_ref[sl, _head_lanes(g)] = o[g]
                st_ref[g] = st_new[g]

    return pl.pallas_call(
        body, grid=(HG_HEADS // HG_GROUP, nc),
        in_specs=[pl.BlockSpec(blk, im), pl.BlockSpec(blk, im), pl.BlockSpec(blk, lambda h, c: (c, COL_HI // HG_GROUP + h)), pl.BlockSpec(blk, im)],
        out_specs=[pl.BlockSpec(blk, im), pl.BlockSpec((HG_GROUP, HG_PER, HEAD, HEAD), lambda h, c: (h, c, 0, 0))],
        out_shape=[SDS((t, D_MODEL), F32), SDS((HG_HEADS, nc * HG_PER, HEAD, HEAD), F32)],
        scratch_shapes=[pltpu.VMEM((HG_GROUP, HEAD, HEAD), F32)], name="hgrn_rec", compiler_params=_params(2),
    )(q, k, proj, b)


def _hgrn_rec_bwd(q, k, proj, b, hs, do):
    t = q.shape[0]
    nc = t // GDN_CHUNK
    blk = (GDN_CHUNK, HG_GROUP * HEAD)
    im = lambda h, c: (nc - 1 - c, h)

    def body(q_ref, k_ref, v_ref, b_ref, hs_ref, do_ref, dq_ref, dk_ref, dv_ref, db_ref, dst_ref):
        @pl.when(pl.program_id(1) == 0)
        def _():
            dst_ref[...] = jnp.zeros_like(dst_ref)

        heads = range(HG_GROUP)
        for j in reversed(range(HG_PER)):
            sl = pl.ds(HG_CHUNK * j, HG_CHUNK)
            _, vjp = jax.vjp(_hgrn_chunks, *[tuple(r[sl, _head_lanes(g)] for g in heads) for r in (q_ref, k_ref, v_ref, b_ref)],
                             tuple(hs_ref[g, j] for g in heads))
            dq, dk, dv, db, dst = vjp((tuple(do_ref[sl, _head_lanes(g)] for g in heads), tuple(dst_ref[g] for g in heads)))
            for g in heads:
                ln = _head_lanes(g)
                dq_ref[sl, ln] = dq[g]
                dk_ref[sl, ln] = dk[g]
                dv_ref[sl, ln] = dv[g].astype(dv_ref.dtype)
                db_ref[sl, ln] = db[g]
                dst_ref[g] = dst[g]

    spec = pl.BlockSpec(blk, im)
    return pl.pallas_call(
        body, grid=(HG_HEADS // HG_GROUP, nc),
        in_specs=[spec, spec, pl.BlockSpec(blk, lambda h, c: (nc - 1 - c, COL_HI // HG_GROUP + h)), spec,
                  pl.BlockSpec((HG_GROUP, HG_PER, HEAD, HEAD), lambda h, c: (h, nc - 1 - c, 0, 0)), spec],
        out_specs=[spec, spec, spec, spec],
        out_shape=[SDS((t, D_MODEL), F32), SDS((t, D_MODEL), F32), SDS((t, D_MODEL), BF16), SDS((t, D_MODEL), F32)],
        scratch_shapes=[pltpu.VMEM((HG_GROUP, HEAD, HEAD), F32)], name="hgrn_rec_bwd", compiler_params=_params(2),
    )(q, k, proj, b, hs, do)


def _shift_down(x, d):
    if d == 0:
        return x
    row = lax.broadcasted_iota(jnp.int32, x.shape, 0)
    return jnp.where(row >= d, pltpu.roll(x, d, 0), 0.0)


def _shift_up(x, d):
    if d == 0:
        return x
    n = x.shape[0]
    row = lax.broadcasted_iota(jnp.int32, x.shape, 0)
    return jnp.where(row < n - d, pltpu.roll(x, n - d, 0), 0.0)


def _conv_fwd(proj, conv_w):
    t = proj.shape[0]
    width = 2 * D_MODEL + 2 * D_MODEL

    def body(x_ref, w_ref, c_ref):
        x, w = x_ref[...], w_ref[...]
        y = w[CONV_K - 1:CONV_K, :] * x
        for j in range(CONV_K - 1):
            y = y + w[j:j + 1, :] * _shift_down(x, CONV_K - 1 - j)
        c_ref[...] = _silu(y)

    return pl.pallas_call(
        body, grid=(width // HEAD,),
        in_specs=[pl.BlockSpec((t, HEAD), lambda j: (0, COL_GQ + j)), pl.BlockSpec((CONV_K, HEAD), lambda j: (0, j))],
        out_specs=pl.BlockSpec((t, HEAD), lambda j: (0, j)), out_shape=SDS((t, width), F32),
        name="gdn_conv", compiler_params=_params(1),
    )(proj, conv_w)


def _conv_bwd(proj, conv_w, dc_qk, dc_v):
    t = proj.shape[0]
    n_qk = dc_qk.shape[1] // HEAD
    width = dc_qk.shape[1] + dc_v.shape[1]

    def body(x_ref, w_ref, dqk_ref, dv_ref, dx_ref, dw_ref):
        x, w = x_ref[...], w_ref[...]
        xs = [_shift_down(x, CONV_K - 1 - j) for j in range(CONV_K)]
        y = w[0:1, :] * xs[0]
        for j in range(1, CONV_K):
            y = y + w[j:j + 1, :] * xs[j]
        sg = jax.nn.sigmoid(y)
        dc = jnp.where(pl.program_id(0) < n_qk, dqk_ref[...], dv_ref[...])
        dy = dc * (sg * (1.0 + y * (1.0 - sg)))
        dx = w[CONV_K - 1:CONV_K, :] * dy
        for j in range(CONV_K - 1):
            dx = dx + w[j:j + 1, :] * _shift_up(dy, CONV_K - 1 - j)
        dx_ref[...] = dx.astype(dx_ref.dtype)
        dw_ref[...] = jnp.concatenate([jnp.sum(xs[j] * dy, axis=0, keepdims=True) for j in range(CONV_K)], axis=0)

    return pl.pallas_call(
        body, grid=(width // HEAD,),
        in_specs=[pl.BlockSpec((t, HEAD), lambda j: (0, COL_GQ + j)), pl.BlockSpec((CONV_K, HEAD), lambda j: (0, j)),
                  pl.BlockSpec((t, HEAD), lambda j: (0, jnp.minimum(j, n_qk - 1))), pl.BlockSpec((t, HEAD), lambda j: (0, jnp.maximum(j - n_qk, 0)))],
        out_specs=[pl.BlockSpec((t, HEAD), lambda j: (0, j)), pl.BlockSpec((CONV_K, HEAD), lambda j: (0, j))],
        out_shape=[SDS((t, width), BF16), SDS((CONV_K, width), F32)],
        name="gdn_conv_bwd", compiler_params=_params(1),
    )(proj, conv_w, dc_qk, dc_v)


def _l2norm(x, scale):
    return x * lax.rsqrt(jnp.sum(x * x, axis=-1, keepdims=True) + EPS) * scale


def _head(a, h):
    return a[:, h * HEAD:(h + 1) * HEAD]


def _qk_scale(h):
    return HEAD ** -0.5 if h < GDN_QK_HEADS else 1.0


def _qk_norm_fwd(c):
    t = c.shape[0]
    tt = _tile(t, (256, 128))
    width = 2 * D_MODEL

    def fn(x):
        return jnp.concatenate([_l2norm(_head(x, h), _qk_scale(h)) for h in range(2 * GDN_QK_HEADS)], axis=1)

    return _tmap(fn, (1, t // tt), [(c, *_rows(width, tt))], [((t, width), F32, *_rows(width, tt), None)], "gdn_qk_norm")[0]


def _qk_norm_bwd(c, dq_rep, dk_rep):
    t = c.shape[0]
    tt = _tile(t, (256, 128))
    width = 2 * D_MODEL

    def fn(x, dq2, dk2):
        out = []
        for h in range(2 * GDN_QK_HEADS):
            d2, hh = (dq2, h) if h < GDN_QK_HEADS else (dk2, h - GDN_QK_HEADS)
            _, vjp = jax.vjp(lambda x: _l2norm(x, _qk_scale(h)), _head(x, h))
            out.append(vjp(_head(d2, 2 * hh) + _head(d2, 2 * hh + 1))[0])
        return jnp.concatenate(out, axis=1)

    r = _rows(width, tt)
    return _tmap(fn, (1, t // tt), [(c, *r), (dq_rep, *r), (dk_rep, *r)], [((t, width), F32, *r, None)], "gdn_qk_norm_bwd")[0]


def _gdn_gates(x, alog, dtb):
    return -jnp.exp(alog) * _softplus(x + dtb), jax.nn.sigmoid(x)


def _gates_fwd(pab, alog, dtb):
    t = pab.shape[0]
    tt = _tile(t, (256, 128))

    def fn(x, alog, dtb):
        g, beta = _gdn_gates(x, alog, dtb)
        lane = lax.broadcasted_iota(jnp.int32, g.shape, 1)
        return jnp.where(lane < GDN_V_HEADS, _hdot_raw(_chunk_sum_matrix(tt, GDN_CHUNK), g), beta).T

    p = (alog, (1, HEAD), lambda j, i: (0, 0)), (dtb, (1, HEAD), lambda j, i: (0, 0))
    return _tmap(fn, (1, t // tt), [(pab, *_rows(HEAD, tt)), *p], [((HEAD, t), F32, (HEAD, tt), lambda j, i: (0, i), None)], "gdn_gates")[0]


def _gates_bwd(pab, alog, dtb, dout_t):
    t = pab.shape[0]
    tt = _tile(t, (256, 128))

    def fn(x, alog, dtb, dout_t):
        dout = dout_t.T
        lane = lax.broadcasted_iota(jnp.int32, dout.shape, 1)
        dgam = jnp.where(lane < GDN_V_HEADS, dout, 0.0)
        dbeta = jnp.where(jnp.logical_and(lane >= GDN_V_HEADS, lane < 2 * GDN_V_HEADS), dout, 0.0)
        dg = _hdot_raw(_chunk_sum_matrix(tt, GDN_CHUNK, transpose=True), dgam)
        _, vjp = jax.vjp(_gdn_gates, x, alog, dtb)
        return vjp((dg, dbeta))

    p = (alog, (1, HEAD), lambda j, i: (0, 0)), (dtb, (1, HEAD), lambda j, i: (0, 0))
    acc = ((1, HEAD), F32, (1, HEAD), lambda j, i: (0, 0), "inner")
    return _tmap(fn, (1, t // tt), [(pab, *_rows(HEAD, tt)), *p, (dout_t, (HEAD, tt), lambda j, i: (0, i))],
                 [((t, HEAD), BF16, *_rows(HEAD, tt), None), acc, acc], "gdn_gates_bwd")


def _split_bf16(x):
    hi = x.astype(BF16)
    return hi, (x - hi.astype(F32)).astype(BF16)


def _dot3(a, b):
    (ah, al), (bh, bl) = a, b
    return _bdot_raw(ah, bh, "nn") + (_bdot_raw(ah, bl, "nn") + _bdot_raw(al, bh, "nn"))


def _each(fn, *lists):
    return tuple(fn(*xs) for xs in zip(*lists))


def _unit_lower_inverses_raw(a):
    n = a[0].shape[0]
    row = lax.broadcasted_iota(jnp.int32, (n, n), 0)
    col = lax.broadcasted_iota(jnp.int32, (n, n), 1)
    eye = jnp.where(row == col, 1.0, 0.0).astype(F32)
    p = _each(lambda a: eye - a, a)
    x = _each(_split_bf16, a)
    m = 2
    while m < 2 * n:
        x = _each(_split_bf16, _each(_dot3, x, x))
        p = _each(lambda p, x: p + _dot3(_split_bf16(p), x), p, x)
        m *= 2
    return p


@jax.custom_vjp
def _unit_lower_inverses(a, known):
    return _unit_lower_inverses_raw(a) if known is None else known


def _uli_fwd(a, known):
    inv = _unit_lower_inverses(a, known)
    return inv, (inv, known)


def _uli_bwd(res, ct):
    inv, known = res
    right = _each(lambda ct, inv: _bdot_raw(ct, inv, "nt"), ct, inv)
    da = _each(lambda inv, r: -_bdot_raw(inv, r, "tn"), inv, right)
    return da, (None if known is None else _each(jnp.zeros_like, known))


_unit_lower_inverses.defvjp(_uli_fwd, _uli_bwd)


def _gdn_chunks(q, k, v, beta_rows, gam_rows, s, inv_known=None):
    n = q[0].shape[0]
    heads = range(len(q))
    row = lax.broadcasted_iota(jnp.int32, (n, n), 0)
    col = lax.broadcasted_iota(jnp.int32, (n, n), 1)
    beta_cols, gam_cols = beta_rows.T, gam_rows.T
    beta = tuple(beta_cols[:, g:g + 1] for g in heads)
    gam = tuple(gam_cols[:, g:g + 1] for g in heads)
    gam_row = tuple(gam_rows[g:g + 1, :] for g in heads)
    decay = _each(lambda gam, gam_row: jnp.where(row >= col, jnp.exp(jnp.minimum(gam - gam_row, 0.0)), 0.0), gam, gam_row)
    kb = _each(lambda k, beta: k * beta, k, beta)
    a = _each(lambda kb, k, decay: jnp.where(row > col, _bdot(kb, k, "nt") * decay, 0.0), kb, k, decay)
    inv = _unit_lower_inverses(a, inv_known)
    eg = _each(jnp.exp, gam)
    u = _each(lambda inv, v, beta: _bdot(inv, v * beta, "nn"), inv, v, beta)
    w = _each(lambda inv, kb, eg: _bdot(inv, kb * eg, "nn"), inv, kb, eg)
    qk = _each(lambda q, k, decay: _bdot(q, k, "nt") * decay, q, k, decay)
    v_new = _each(lambda u, w, s: u - _bdot(w, s, "nn"), u, w, s)
    o_state = _each(lambda q, eg, s: _bdot(q * eg, s, "nn"), q, eg, s)
    o = _each(lambda o_state, qk, v_new: o_state + _bdot(qk, v_new, "nn"), o_state, qk, v_new)
    gend = _each(lambda gam: gam[n - 1:n, :], gam)
    s_new = _each(lambda s, k, gam, gend, v_new: s * jnp.exp(gend) + _bdot(k * jnp.exp(gend - gam), v_new, "tn"), s, k, gam, gend, v_new)
    return o, s_new, inv


GDN_GROUP = 16


def _gdn_specs(nc, rev):
    cc = (lambda c: nc - 1 - c) if rev else (lambda c: c)
    grp = GDN_GROUP
    q = pl.BlockSpec((GDN_CHUNK, grp // 2 * HEAD), lambda h, c: (cc(c), h))
    k = pl.BlockSpec((GDN_CHUNK, grp // 2 * HEAD), lambda h, c: (cc(c), 2 * GDN_QK_HEADS // grp + h))
    v = pl.BlockSpec((GDN_CHUNK, grp * HEAD), lambda h, c: (cc(c), 2 * GDN_QK_HEADS // grp + h))
    o = pl.BlockSpec((GDN_CHUNK, grp * HEAD), lambda h, c: (cc(c), h))
    rw = pl.BlockSpec((grp, None, 1, GDN_CHUNK), lambda h, c: (h, cc(c), 0, 0))
    st = pl.BlockSpec((grp, None, HEAD, HEAD), lambda h, c: (h, cc(c), 0, 0))
    inv = pl.BlockSpec((grp, None, GDN_CHUNK, GDN_CHUNK), lambda h, c: (h, cc(c), 0, 0))
    return q, k, v, o, rw, st, inv


def _head_lanes(g, per=1):
    return pl.ds((g // per) * HEAD, HEAD)


def _gdn_rec_fwd(qk, c, beta_row, gam_row):
    t = qk.shape[0]
    nc = t // GDN_CHUNK
    q, k, v, o, rw, st, inv = _gdn_specs(nc, False)

    def body(q_ref, k_ref, v_ref, be_ref, gr_ref, o_ref, ss_ref, inv_ref, s_ref):
        @pl.when(pl.program_id(1) == 0)
        def _():
            s_ref[...] = jnp.zeros_like(s_ref)

        heads = range(GDN_GROUP)
        s = tuple(s_ref[g] for g in heads)
        out, s_new, inv_c = _gdn_chunks(
            tuple(q_ref[:, _head_lanes(g, 2)] for g in heads), tuple(k_ref[:, _head_lanes(g, 2)] for g in heads),
            tuple(v_ref[:, _head_lanes(g)] for g in heads), be_ref[:, 0, :], gr_ref[:, 0, :], s)
        for g in heads:
            ss_ref[g] = s[g]
            o_ref[:, _head_lanes(g)] = out[g]
            inv_ref[g] = inv_c[g]
            s_ref[g] = s_new[g]

    return pl.pallas_call(
        body, grid=(GDN_V_HEADS // GDN_GROUP, nc), in_specs=[q, k, v, rw, rw], out_specs=[o, st, inv],
        out_shape=[SDS((t, 2 * D_MODEL), F32), SDS((GDN_V_HEADS, nc, HEAD, HEAD), F32), SDS((GDN_V_HEADS, nc, GDN_CHUNK, GDN_CHUNK), F32)],
        scratch_shapes=[pltpu.VMEM((GDN_GROUP, HEAD, HEAD), F32)], name="gdn_rec", compiler_params=_params(2),
    )(qk, qk, c, beta_row, gam_row)


def _gdn_rec_bwd(qk, c, beta_row, gam_row, ss, invs, do):
    t = qk.shape[0]
    nc = t // GDN_CHUNK
    q, k, v, o, rw, st, inv = _gdn_specs(nc, True)

    def body(q_ref, k_ref, v_ref, be_ref, gr_ref, ss_ref, inv_ref, do_ref,
             dq_ref, dk_ref, dv_ref, dbe_ref, dgr_ref, ds_ref):
        @pl.when(pl.program_id(1) == 0)
        def _():
            ds_ref[...] = jnp.zeros_like(ds_ref)

        heads = range(GDN_GROUP)
        _, vjp = jax.vjp(
            _gdn_chunks,
            tuple(q_ref[:, _head_lanes(g, 2)] for g in heads), tuple(k_ref[:, _head_lanes(g, 2)] for g in heads),
            tuple(v_ref[:, _head_lanes(g)] for g in heads), be_ref[:, 0, :], gr_ref[:, 0, :],
            tuple(ss_ref[g] for g in heads), tuple(inv_ref[g] for g in heads))
        no_inv_ct = tuple(jnp.zeros((GDN_CHUNK, GDN_CHUNK), F32) for g in heads)
        dq, dk, dv, dbe, dgr, ds, _ = vjp((tuple(do_ref[:, _head_lanes(g)] for g in heads), tuple(ds_ref[g] for g in heads), no_inv_ct))
        for g in heads:
            dq_ref[:, _head_lanes(g)] = dq[g]
            dk_ref[:, _head_lanes(g)] = dk[g]
            dv_ref[:, _head_lanes(g)] = dv[g]
            ds_ref[g] = ds[g]
        dbe_ref[:, 0, :] = dbe
        dgr_ref[:, 0, :] = dgr

    wide = SDS((t, 2 * D_MODEL), F32)
    rowshape = SDS((GDN_V_HEADS, nc, 1, GDN_CHUNK), F32)
    return pl.pallas_call(
        body, grid=(GDN_V_HEADS // GDN_GROUP, nc), in_specs=[q, k, v, rw, rw, st, inv, o], out_specs=[o, o, o, rw, rw],
        out_shape=[wide, wide, wide, rowshape, rowshape],
        scratch_shapes=[pltpu.VMEM((GDN_GROUP, HEAD, HEAD), F32)], name="gdn_rec_bwd", compiler_params=_params(2),
    )(qk, qk, c, beta_row, gam_row, ss, invs, do)


def _gated_norm(o, gate, w):
    return _rms(o, w) * _silu(gate)


def _post_fwd(o, proj, col_off, w, name):
    t, width = o.shape
    tt = _tile(t, (256, 128))

    def fn(o, gate, w):
        return jnp.concatenate([_gated_norm(_head(o, h), _head(gate, h), w) for h in range(width // HEAD)], axis=1)

    return _tmap(fn, (1, t // tt),
                 [(o, *_rows(width, tt)), (proj, *_rows(width, tt, col_off * HEAD // width)), (w, (1, HEAD), lambda j, i: (0, 0))],
                 [((t, width), BF16, *_rows(width, tt), None)], name)[0]


def _post_bwd(o, proj, col_off, w, dout, name):
    t, width = o.shape
    tt = _tile(t, (256, 128))

    def fn(o, gate, w, dout):
        do, dgate, dw = [], [], jnp.zeros((1, HEAD), F32)
        for h in range(width // HEAD):
            _, vjp = jax.vjp(_gated_norm, _head(o, h), _head(gate, h), w)
            a, b, c = vjp(_head(dout, h))
            do.append(a)
            dgate.append(b)
            dw = dw + c
        return jnp.concatenate(do, axis=1), jnp.concatenate(dgate, axis=1), dw

    r = _rows(width, tt)
    return _tmap(fn, (1, t // tt),
                 [(o, *r), (proj, *_rows(width, tt, col_off * HEAD // width)), (w, (1, HEAD), lambda j, i: (0, 0)), (dout, *r)],
                 [((t, width), F32, *r, None), ((t, width), BF16, *r, None), ((1, HEAD), F32, (1, HEAD), lambda j, i: (0, 0), "inner")], name)


def _merge(gate_h, gate_g, yh, yg):
    return jax.nn.sigmoid(gate_h) * yh + jax.nn.sigmoid(gate_g) * yg


def _merge_fwd(proj, yh, yg):
    t = yh.shape[0]
    tt, ft = _tile(t, (256, 128)), 512
    r = _rows(ft, tt)
    return _tmap(_merge, (D_MODEL // ft, t // tt),
                 [(proj, *_rows(ft, tt, COL_GATE_H * HEAD // ft)), (proj, *_rows(ft, tt, COL_GATE_G * HEAD // ft)), (yh, *r), (yg, *r)],
                 [((t, D_MODEL), BF16, *r, None)], "merge")[0]


def _merge_bwd(proj, yh, yg, dy):
    t = yh.shape[0]
    tt, ft = _tile(t, (256, 128)), 512
    r = _rows(ft, tt)

    def fn(gate_h, gate_g, yh, yg, dy):
        _, vjp = jax.vjp(_merge, gate_h, gate_g, yh, yg)
        return vjp(dy)

    o = ((t, D_MODEL), BF16, *r, None)
    return _tmap(fn, (D_MODEL // ft, t // tt),
                 [(proj, *_rows(ft, tt, COL_GATE_H * HEAD // ft)), (proj, *_rows(ft, tt, COL_GATE_G * HEAD // ft)), (yh, *r), (yg, *r), (dy, *r)],
                 [o, o, o, o], "merge_bwd")


def _loss_head(h, target, g):
    t, d = h.shape
    tt = _tile(t, (256, 128))

    def fn(h, target, g):
        def f(h, g):
            err = _rms(h, g) - target
            return 0.5 * jnp.sum(jnp.mean(err * err, axis=-1))

        loss, (dh, dg) = jax.value_and_grad(f, (0, 1))(h, g)
        return dh, dg, jnp.full((1, HEAD), loss, F32)

    return _tmap(fn, (1, t // tt), [(h, *_rows(d, tt)), (target, *_rows(d, tt)), (g, (1, d), lambda j, i: (0, 0))],
                 [((t, d), F32, *_rows(d, tt), None), ((1, d), F32, (1, d), lambda j, i: (0, 0), "inner"),
                  ((1, HEAD), F32, (1, HEAD), lambda j, i: (0, 0), "inner")], "loss_head")


def _mixer_fwd(h, p, links):
    t = h.shape[0]
    nc = t // GDN_CHUNK
    u = _rms_fwd(h, p["mix_norm"], "mix_norm")
    w = {n: links.weight(n, h) for n in ("w_in_t", "w_in_b_t", "w_in_ab_t", "conv_w")}
    proj = _mm(u, w["w_in_t"], "nt", F32, "mix_in", after=links.started, b_rows=SCALAR_ROWS)
    proj_b = _mm(u, w["w_in_b_t"], "nt", F32, "mix_in_b")
    pab = _mm(u, w["w_in_ab_t"], "nt", F32, "mix_in_ab")
    qh, kh, bh = _hgrn_prep_fwd(proj, p["lbl"])
    oh, hs = _hgrn_rec_fwd(qh, kh, proj, bh)
    c = _conv_fwd(proj, w["conv_w"])
    qk = _qk_norm_fwd(c)
    gates_t = _gates_fwd(pab, p["alog"], p["dtb"])
    gam_row = gates_t[:GDN_V_HEADS].reshape(GDN_V_HEADS, nc, 1, GDN_CHUNK)
    beta_row = gates_t[GDN_V_HEADS:2 * GDN_V_HEADS].reshape(GDN_V_HEADS, nc, 1, GDN_CHUNK)
    og, ss, invs = _gdn_rec_fwd(qk, c, beta_row, gam_row)
    ohn = _post_fwd(oh, proj, COL_HG, p["hgrn_out_norm"], "hgrn_out")
    ogn = _post_fwd(og, proj_b, COL_GZ, p["gdn_out_norm"], "gdn_out")
    w.update({n: links.weight(n, ogn) for n in ("w_branch_hgrn", "w_branch_gdn", "w_out")})
    yh = _mm(ohn, w["w_branch_hgrn"], "nn", F32, "branch_hgrn")
    yg = _mm(ogn, w["w_branch_gdn"], "nn", F32, "branch_gdn")
    y = _merge_fwd(proj_b, yh, yg)
    out = _mm(y, w["w_out"], "nn", F32, "mix_out", res=h)
    saved = (w, u, proj, proj_b, pab, qh, kh, bh, oh, hs, c, qk, beta_row, gam_row, og, ss, invs, ohn, ogn, yh, yg, y)
    return out, saved


def _mixer_bwd(h, p, links, saved, dout):
    (w, u, proj, proj_b, pab, qh, kh, bh, oh, hs, c, qk, beta_row, gam_row, og, ss, invs, ohn, ogn, yh, yg, y) = saved
    t = h.shape[0]
    grads = {}
    dw_out = _mm(y, dout, "tn", BF16, "mix_out_dw")
    dy = _mm(dout, w["w_out"], "nt", F32, "mix_out_dx")
    dgate_h, dgate_g, dyh, dyg = _merge_bwd(proj_b, yh, yg, dy)
    dw_bh = _mm(ohn, dyh, "tn", BF16, "branch_hgrn_dw")
    dw_bg = _mm(ogn, dyg, "tn", BF16, "branch_gdn_dw")
    sent = links.send({"w_out": dw_out, "w_branch_hgrn": dw_bh, "w_branch_gdn": dw_bg})
    dohn = _mm(dyh, w["w_branch_hgrn"], "nt", F32, "branch_hgrn_dx", after=sent)
    dogn = _mm(dyg, w["w_branch_gdn"], "nt", F32, "branch_gdn_dx")
    doh, dhg, grads["hgrn_out_norm"] = _post_bwd(oh, proj, COL_HG, p["hgrn_out_norm"], dohn, "hgrn_out_bwd")
    dog, dgz, grads["gdn_out_norm"] = _post_bwd(og, proj_b, COL_GZ, p["gdn_out_norm"], dogn, "gdn_out_bwd")
    dqh, dkh, dhi, dbh = _hgrn_rec_bwd(qh, kh, proj, bh, hs, doh)
    dhq, dhf, grads["lbl"] = _hgrn_prep_bwd(proj, p["lbl"], dqh, dkh, dbh)
    dqv, dkv, dcv, dbeta_row, dgam_row = _gdn_rec_bwd(qk, c, beta_row, gam_row, ss, invs, dog)
    dcqk = _qk_norm_bwd(c, dqv, dkv)
    dxin, grads["conv_w"] = _conv_bwd(proj, w["conv_w"], dcqk, dcv)
    dgates_t = jnp.concatenate([dgam_row.reshape(GDN_V_HEADS, t), dbeta_row.reshape(GDN_V_HEADS, t),
                                jnp.zeros((HEAD - 2 * GDN_V_HEADS, t), F32)], axis=0)
    dpab, grads["alog"], grads["dtb"] = _gates_bwd(pab, p["alog"], p["dtb"], dgates_t)
    dproj = jnp.concatenate([dhq, dhf, dhi, dhg, dxin], axis=1)
    dproj_b = jnp.concatenate([dgz, dgate_h, dgate_g], axis=1)
    dw_t = _mm(dproj, u, "tn", BF16, "mix_in_dw")
    dw_b_t = _mm(dproj_b, u, "tn", BF16, "mix_in_b_dw")
    dw_ab_t = _mm(dpab, u, "tn", BF16, "mix_in_ab_dw")
    sent = links.send({"w_in": jnp.concatenate([dw_t, dw_ab_t[:N_SCALAR], dw_b_t], axis=0)})
    du = _mm(dproj, w["w_in_t"], "nn", F32, "mix_in_dx", after=sent, b_rows=SCALAR_ROWS)
    du = _mm(dproj_b, w["w_in_b_t"], "nn", F32, "mix_in_b_dx", res=du)
    du = _mm(dpab, w["w_in_ab_t"], "nn", F32, "mix_in_ab_dx", res=du)
    dh, grads["mix_norm"] = _rms_bwd(h, p["mix_norm"], du, dout, "mix_norm_bwd")
    return dh, grads


def _local_step(x, target, p, links):
    w1 = {n: links.weight(n, x) for n in ("ffn1_w_in", "ffn1_w_out")}
    h1, s1 = _ffn_fwd(x, p["ffn1_norm"], w1["ffn1_w_in"], w1["ffn1_w_out"], "ffn1", links.started)
    h2, sm = _mixer_fwd(h1, p, links)
    w2 = {n: links.weight(n, h2) for n in ("ffn2_w_in", "ffn2_w_out")}
    h3, s2 = _ffn_fwd(h2, p["ffn2_norm"], w2["ffn2_w_in"], w2["ffn2_w_out"], "ffn2", None)
    dh3, dfinal, loss = _loss_head(h3, target, p["final_norm"])
    g = {"final_norm": dfinal}
    dh2, g["ffn2_norm"] = _ffn_bwd(h2, p["ffn2_norm"], w2["ffn2_w_in"], w2["ffn2_w_out"], s2, dh3, "ffn2", links)
    dh1, gm = _mixer_bwd(h1, p, links, sm, dh2)
    g.update(gm)
    dx, g["ffn1_norm"] = _ffn_bwd(x, p["ffn1_norm"], w1["ffn1_w_in"], w1["ffn1_w_out"], s1, dh1, "ffn1", links)
    return loss, dx, g


def _exchange(items, name):
    n = len(items)
    out_shape = [SDS((N_DEV,) + a.shape if mode == "gather" else a.shape, a.dtype) for a, mode in items]

    def body(*refs):
        in_refs, out_refs = refs[:n], refs[n:2 * n]
        send_sems, recv_sems, local_sems = refs[2 * n:]
        x, y, c = lax.axis_index("x"), lax.axis_index("y"), lax.axis_index("c")
        me = 4 * x + 2 * y + c
        local = []
        for i, (_, mode) in enumerate(items):
            src = in_refs[i] if mode == "gather" else in_refs[i].at[me]
            cp = pltpu.make_async_copy(src, out_refs[i].at[me], local_sems.at[i])
            cp.start()
            local.append(cp)
        remote = []
        for rel in range(1, N_DEV):
            px = 1 - x if rel & 4 else x
            py = 1 - y if rel & 2 else y
            pc = 1 - c if rel & 1 else c
            peer = 4 * px + 2 * py + pc
            for i, (_, mode) in enumerate(items):
                src = in_refs[i] if mode == "gather" else in_refs[i].at[peer]
                cp = pltpu.make_async_remote_copy(src_ref=src, dst_ref=out_refs[i].at[me], send_sem=send_sems.at[i, rel - 1],
                                                  recv_sem=recv_sems.at[i, rel - 1], device_id=(px, py, pc), device_id_type=MESH_IDS)
                cp.start()
                remote.append(cp)
        for cp in remote:
            cp.wait()
        for cp in local:
            cp.wait()

    anyspace = pl.BlockSpec(memory_space=pl.ANY)
    return pl.pallas_call(
        body, in_specs=[anyspace] * n, out_specs=[anyspace] * n, out_shape=out_shape,
        scratch_shapes=[pltpu.SemaphoreType.DMA((n, N_DEV - 1)), pltpu.SemaphoreType.DMA((n, N_DEV - 1)), pltpu.SemaphoreType.DMA((n,))],
        name=name, compiler_params=pltpu.CompilerParams(has_side_effects=True),
    )(*[a for a, _ in items])


HBM_SPEC = pl.BlockSpec(memory_space=pltpu.HBM)
SEM_SPEC = pl.BlockSpec(memory_space=pltpu.SEMAPHORE)
DATAFLOW = pltpu.SideEffectType.DATAFLOW_SIDE_EFFECTING


def _position():
    x, y, c = lax.axis_index("x"), lax.axis_index("y"), lax.axis_index("c")
    return x, y, c, 4 * x + 2 * y + c


def _relations(x, y, c):
    for rel in range(1, N_DEV):
        px = 1 - x if rel & 4 else x
        py = 1 - y if rel & 2 else y
        pc = 1 - c if rel & 1 else c
        yield rel, (px, py, pc), 4 * px + 2 * py + pc


def _sem_index(item, rel):
    return item * (N_DEV - 1) + rel - 1


def _landing(a, mode):
    return lax.empty((N_DEV,) + a.shape if mode == "gather" else a.shape, a.dtype)


ALL_PEERS = tuple(range(1, N_DEV))
ONE_PER_CHIP = (1, 2, 4, 6)


def _copies_start(groups, name, rels=ALL_PEERS):
    flat = [item for grp in groups for item in grp]
    n, ng = len(flat), len(groups)
    lands = [_landing(a, mode) for a, mode in flat]

    def body(*refs):
        src_refs, land_refs, sems, token = refs[:n], refs[n:2 * n], refs[2 * n:2 * n + 2 * ng], refs[-1]
        x, y, c, me = _position()
        for rel, where, peer in _relations(x, y, c):
            if rel not in rels:
                continue
            k = 0
            for gi, grp in enumerate(groups):
                for li, (_, mode) in enumerate(grp):
                    src = src_refs[k] if mode == "gather" else src_refs[k].at[peer]
                    pltpu.make_async_remote_copy(src_ref=src, dst_ref=land_refs[k].at[me], send_sem=sems[2 * gi].at[_sem_index(li, rel)],
                                                 recv_sem=sems[2 * gi + 1].at[_sem_index(li, rel)], device_id=where, device_id_type=MESH_IDS).start()
                    k += 1
        token[...] = jnp.zeros_like(token)

    sem_shapes = [pltpu.SemaphoreType.DMA((len(grp) * (N_DEV - 1),)) for grp in groups for _ in range(2)]
    thru = [pltpu.HBM(a.shape, a.dtype) for a, _ in flat] + [pltpu.HBM(l.shape, l.dtype) for l in lands]
    outs = pl.pallas_call(
        body, name=name, out_shape=(*sem_shapes, *thru, SDS((8, HEAD), F32)),
        in_specs=[HBM_SPEC] * (2 * n), out_specs=(*[SEM_SPEC] * (2 * ng), *[HBM_SPEC] * (2 * n), pl.BlockSpec(memory_space=pltpu.VMEM)),
        input_output_aliases={i: 2 * ng + i for i in range(2 * n)}, compiler_params=pltpu.CompilerParams(has_side_effects=DATAFLOW),
    )(*[pltpu.with_memory_space_constraint(a, pltpu.HBM) for a, _ in flat], *[pltpu.with_memory_space_constraint(l, pltpu.HBM) for l in lands])
    sems, srcs, landed, token = outs[:2 * ng], outs[2 * ng:2 * ng + n], outs[2 * ng + n:2 * ng + 2 * n], outs[-1]
    result, k = [], 0
    for gi, grp in enumerate(groups):
        result.append((sems[2 * gi], sems[2 * gi + 1], srcs[k:k + len(grp)], landed[k:k + len(grp)]))
        k += len(grp)
    return result, token


def _copies_wait(started, modes, after, name, rels=ALL_PEERS):
    send_sems, recv_sems, srcs, lands = started
    n = len(srcs)

    def body(*refs):
        src_refs, land_refs, ssem, rsem, token = refs[:n], refs[n:2 * n], refs[2 * n], refs[2 * n + 1], refs[-1]
        x, y, c, _ = _position()
        for rel in rels:
            for i, mode in enumerate(modes):
                src = src_refs[i] if mode == "gather" else src_refs[i].at[0]
                cp = pltpu.make_async_remote_copy(src_ref=src, dst_ref=land_refs[i].at[0], send_sem=ssem.at[_sem_index(i, rel)],
                                                  recv_sem=rsem.at[_sem_index(i, rel)], device_id=(x, y, c), device_id_type=MESH_IDS)
                cp.wait_send()
                cp.wait_recv()
        token[...] = jnp.zeros_like(token)

    outs = pl.pallas_call(
        body, name=name, out_shape=[pltpu.HBM(a.shape, a.dtype) for a in (*srcs, *lands)] + [SDS((8, HEAD), F32)],
        in_specs=[HBM_SPEC] * (2 * n) + [SEM_SPEC, SEM_SPEC, pl.BlockSpec(memory_space=pl.ANY)],
        out_specs=[HBM_SPEC] * (2 * n) + [pl.BlockSpec(memory_space=pltpu.VMEM)],
        input_output_aliases={i: i for i in range(2 * n)}, compiler_params=pltpu.CompilerParams(has_side_effects=DATAFLOW),
    )(*srcs, *lands, send_sems, recv_sems, after)
    return outs[:n], outs[n:2 * n], outs[-1]


OTHER_CHIPS = ((1, 0), (0, 1), (1, 1))


def _pass_on_start(lands, name):
    n = len(lands)

    def body(*refs):
        land_refs, ssem, rsem, token = refs[:n], refs[n], refs[n + 1], refs[-1]
        x, y, c, _ = _position()
        for j, (fx, fy) in enumerate(OTHER_CHIPS):
            slot = 4 * (1 - x if fx else x) + 2 * (1 - y if fy else y) + c
            for i in range(n):
                pltpu.make_async_remote_copy(src_ref=land_refs[i].at[slot], dst_ref=land_refs[i].at[slot], send_sem=ssem.at[i * len(OTHER_CHIPS) + j],
                                             recv_sem=rsem.at[i * len(OTHER_CHIPS) + j], device_id=(x, y, 1 - c), device_id_type=MESH_IDS).start()
        token[...] = jnp.zeros_like(token)

    sems = pltpu.SemaphoreType.DMA((n * len(OTHER_CHIPS),))
    outs = pl.pallas_call(
        body, name=name, out_shape=(sems, sems, *[pltpu.HBM(l.shape, l.dtype) for l in lands], SDS(TOKEN, F32)),
        in_specs=[HBM_SPEC] * n, out_specs=(SEM_SPEC, SEM_SPEC, *[HBM_SPEC] * n, pl.BlockSpec(memory_space=pltpu.VMEM)),
        input_output_aliases={i: 2 + i for i in range(n)}, compiler_params=pltpu.CompilerParams(has_side_effects=DATAFLOW),
    )(*lands)
    return (outs[0], outs[1], outs[2:2 + n]), outs[-1]


def _pass_on_wait(started, after, name):
    send_sems, recv_sems, lands = started
    n = len(lands)

    def body(*refs):
        land_refs, ssem, rsem = refs[:n], refs[n], refs[n + 1]
        x, y, c, _ = _position()
        for j in range(len(OTHER_CHIPS)):
            for i in range(n):
                cp = pltpu.make_async_remote_copy(src_ref=land_refs[i].at[0], dst_ref=land_refs[i].at[0], send_sem=ssem.at[i * len(OTHER_CHIPS) + j],
                                                  recv_sem=rsem.at[i * len(OTHER_CHIPS) + j], device_id=(x, y, c), device_id_type=MESH_IDS)
                cp.wait_send()
                cp.wait_recv()

    return pl.pallas_call(
        body, name=name, out_shape=[pltpu.HBM(l.shape, l.dtype) for l in lands],
        in_specs=[HBM_SPEC] * n + [SEM_SPEC, SEM_SPEC, pl.BlockSpec(memory_space=pl.ANY)], out_specs=[HBM_SPEC] * n,
        input_output_aliases={i: i for i in range(n)}, compiler_params=pltpu.CompilerParams(has_side_effects=DATAFLOW),
    )(*lands, send_sems, recv_sems, after)


WEIGHT_GROUPS = (("ffn1_w_in", "ffn1_w_out", "gdn_conv_w"), ("w_in",), ("w_branch_hgrn", "w_branch_gdn", "w_out", "ffn2_w_in", "ffn2_w_out"))
GROUP_RELS = (ONE_PER_CHIP, ONE_PER_CHIP, ALL_PEERS)


class _Links:
    def __init__(self, shards, me):
        self.me = me
        self.shards = shards
        self.weights = {}
        self.sends = []
        self.gathers = {}
        self.started = None
        self._start_gather(0, None)

    def _start_gather(self, gi, zeros):
        if gi < len(WEIGHT_GROUPS):
            items = [(self.shards[n] if zeros is None else self.shards[n] + zeros[0, 0].astype(self.shards[n].dtype), "gather")
                     for n in WEIGHT_GROUPS[gi]]
            started, self.started = _copies_start([items], "gather_start_%d" % gi, GROUP_RELS[gi])
            self.gathers[gi] = started[0]

    def weight(self, name, after):
        if name not in self.weights:
            source = {"w_in_t": "w_in", "w_in_b_t": "w_in", "w_in_ab_t": "w_in", "conv_w": "gdn_conv_w"}.get(name, name)
            gi = [i for i, grp in enumerate(WEIGHT_GROUPS) if source in grp][0]
            assert gi in self.gathers, "weight groups are asked for in order"
            srcs, lands, zero = _copies_wait(self.gathers[gi], ["gather"] * len(WEIGHT_GROUPS[gi]), after, "gather_wait_%d" % gi, GROUP_RELS[gi])
            if GROUP_RELS[gi] == ONE_PER_CHIP:
                passing, zero = _pass_on_start(lands, "gather_pass_%d" % gi)
                self._start_gather(gi + 1, zero)
                lands = _pass_on_wait(passing, self.started, "gather_passed_%d" % gi)
            else:
                self._start_gather(gi + 1, zero)
            for n, src, land in zip(WEIGHT_GROUPS[gi], srcs, lands):
                full = lax.dynamic_update_index_in_dim(land, src, self.me, 0)
                if n == "gdn_conv_w":
                    self.weights["conv_w"] = full.reshape(N_DEV, CONV_K, 4 * D_MODEL // N_DEV).transpose(1, 0, 2).reshape(CONV_K, 4 * D_MODEL)
                elif n == "w_in":
                    self.weights.update(_w_in_pieces(full.reshape(-1, D_MODEL)))
                else:
                    self.weights[n] = full.reshape(-1, D_MODEL)
        return self.weights[name]

    def send(self, grads):
        names = list(grads)
        blocks = [grads[n].reshape(N_DEV, -1, D_MODEL) for n in names]
        started, token = _copies_start([[(b, "scatter") for b in blocks]], "send_" + names[0])
        self.sends.append((names, started[0]))
        return token

    def landed(self, after):
        out = {}
        for names, started in self.sends:
            srcs, lands, _ = _copies_wait(started, ["scatter"] * len(names), after, "landed_" + names[0])
            for n, src, land in zip(names, srcs, lands):
                out[n] = lax.dynamic_update_index_in_dim(land, lax.dynamic_index_in_dim(src, self.me, 0, keepdims=False), self.me, 0)
        return out


def _adam(parts, w, m, v, name):
    n_parts, r, c = parts.shape
    tc = c if c <= 512 else (256 if r > 1024 else 512)

    def body(p_ref, w_ref, m_ref, v_ref, g_ref, d_ref, mo_ref, vo_ref):
        g = p_ref[0].astype(F32)
        for i in range(1, n_parts):
            g = g + p_ref[i].astype(F32)
        m_new = ADAM_B1 * m_ref[...] + (1.0 - ADAM_B1) * g
        v_new = ADAM_B2 * v_ref[...] + (1.0 - ADAM_B2) * (g * g)
        m_hat = m_new / (1.0 - ADAM_B1 ** ADAM_STEP)
        v_hat = v_new / (1.0 - ADAM_B2 ** ADAM_STEP)
        g_ref[...] = g
        d_ref[...] = -ADAM_LR * (m_hat / (jnp.sqrt(v_hat) + ADAM_EPS) + ADAM_WD * w_ref[...])
        mo_ref[...] = m_new
        vo_ref[...] = v_new

    spec = pl.BlockSpec((r, tc), lambda j: (0, j))
    return pl.pallas_call(
        body, grid=(c // tc,), in_specs=[pl.BlockSpec((n_parts, r, tc), lambda j: (0, 0, j)), spec, spec, spec],
        out_specs=[spec] * 4, out_shape=[SDS((r, c), F32)] * 4, name=name, compiler_params=_params(1),
    )(parts, w, m, v)


BIG = ("ffn1_w_in", "ffn1_w_out", "w_in", "w_branch_hgrn", "w_branch_gdn", "w_out", "ffn2_w_in", "ffn2_w_out")


TRANSPOSED = ("ffn1_w_in", "w_in", "ffn2_w_in")


def _shard_rows(name, shard):
    return shard.T if name in TRANSPOSED else shard


SCALAR_ROWS = 8192
N_SCALAR = 2 * GDN_V_HEADS


def _w_in_pieces(w_in_t):
    return {"w_in_t": w_in_t, "w_in_b_t": w_in_t[SCALAR_ROWS + N_SCALAR:],
            "w_in_ab_t": jnp.pad(w_in_t[SCALAR_ROWS:SCALAR_ROWS + N_SCALAR], ((0, HEAD - N_SCALAR), (0, 0)))}


def _pad_lanes(a, width=HEAD):
    return jnp.pad(a, ((0, 0), (0, width - a.shape[1])))


SMALL_ROWS = 24


def _pack_small(g, loss):
    row6 = jnp.concatenate([g["hgrn_out_norm"], g["gdn_out_norm"], g["alog"], g["dtb"], loss,
                            jnp.zeros((1, D_MODEL - 5 * HEAD), F32)], axis=1)
    return jnp.concatenate([g["ffn1_norm"], g["mix_norm"], g["lbl"], g["ffn2_norm"], g["final_norm"], row6,
                            jnp.zeros((1, D_MODEL), F32), g["conv_w"].reshape(4 * CONV_K, D_MODEL)], axis=0)


def _pack_small_state(a):
    row6 = jnp.concatenate([a["hgrn_out_norm"], a["gdn_out_norm"], _pad_lanes(a["gdn_a_log"]), _pad_lanes(a["gdn_dt_bias"]),
                            jnp.zeros((1, D_MODEL - 4 * HEAD), F32)], axis=1)
    return jnp.concatenate([a["ffn1_norm"], a["mix_norm"], a["hgrn_lb_logits"], a["ffn2_norm"], a["final_norm"].reshape(1, D_MODEL),
                            row6, jnp.zeros((1, D_MODEL), F32)], axis=0)


def _unpack_small(a):
    return {"ffn1_norm": a[0:1], "mix_norm": a[1:2], "hgrn_lb_logits": a[2:4], "ffn2_norm": a[4:5], "final_norm": a[5],
            "hgrn_out_norm": a[6:7, :HEAD], "gdn_out_norm": a[6:7, HEAD:2 * HEAD],
            "gdn_a_log": a[6:7, 2 * HEAD:2 * HEAD + GDN_V_HEADS], "gdn_dt_bias": a[6:7, 3 * HEAD:3 * HEAD + GDN_V_HEADS]}


NAMES = ("ffn1_norm", "ffn1_w_in", "ffn1_w_out", "mix_norm", "w_in", "hgrn_lb_logits", "hgrn_out_norm", "gdn_conv_w", "gdn_a_log",
         "gdn_dt_bias", "gdn_out_norm", "w_branch_hgrn", "w_branch_gdn", "w_out", "ffn2_norm", "ffn2_w_in", "ffn2_w_out", "final_norm")


def kernel(x, ffn1_norm, ffn1_w_in, ffn1_w_out, mix_norm, w_in, hgrn_lb_logits, hgrn_out_norm, gdn_conv_w, gdn_a_log, gdn_dt_bias, gdn_out_norm, w_branch_hgrn, w_branch_gdn, w_out, ffn2_norm, ffn2_w_in, ffn2_w_out, final_norm, loss_target, m_ffn1_norm, m_ffn1_w_in, m_ffn1_w_out, m_mix_norm, m_w_in, m_hgrn_lb_logits, m_hgrn_out_norm, m_gdn_conv_w, m_gdn_a_log, m_gdn_dt_bias, m_gdn_out_norm, m_w_branch_hgrn, m_w_branch_gdn, m_w_out, m_ffn2_norm, m_ffn2_w_in, m_ffn2_w_out, m_final_norm, v_ffn1_norm, v_ffn1_w_in, v_ffn1_w_out, v_mix_norm, v_w_in, v_hgrn_lb_logits, v_hgrn_out_norm, v_gdn_conv_w, v_gdn_a_log, v_gdn_dt_bias, v_gdn_out_norm, v_w_branch_hgrn, v_w_branch_gdn, v_w_out, v_ffn2_norm, v_ffn2_w_in, v_ffn2_w_out, v_final_norm):
    wts = dict(zip(NAMES, (ffn1_norm, ffn1_w_in, ffn1_w_out, mix_norm, w_in, hgrn_lb_logits, hgrn_out_norm, gdn_conv_w, gdn_a_log,
                           gdn_dt_bias, gdn_out_norm, w_branch_hgrn, w_branch_gdn, w_out, ffn2_norm, ffn2_w_in, ffn2_w_out, final_norm)))
    mom = dict(zip(NAMES, (m_ffn1_norm, m_ffn1_w_in, m_ffn1_w_out, m_mix_norm, m_w_in, m_hgrn_lb_logits, m_hgrn_out_norm, m_gdn_conv_w,
                           m_gdn_a_log, m_gdn_dt_bias, m_gdn_out_norm, m_w_branch_hgrn, m_w_branch_gdn, m_w_out, m_ffn2_norm, m_ffn2_w_in,
                           m_ffn2_w_out, m_final_norm)))
    var = dict(zip(NAMES, (v_ffn1_norm, v_ffn1_w_in, v_ffn1_w_out, v_mix_norm, v_w_in, v_hgrn_lb_logits, v_hgrn_out_norm, v_gdn_conv_w,
                           v_gdn_a_log, v_gdn_dt_bias, v_gdn_out_norm, v_w_branch_hgrn, v_w_branch_gdn, v_w_out, v_ffn2_norm, v_ffn2_w_in,
                           v_ffn2_w_out, v_final_norm)))
    me = 4 * lax.axis_index("x") + 2 * lax.axis_index("y") + lax.axis_index("c")

    conv_shard = wts["gdn_conv_w"][0]
    shards = {n: _shard_rows(n, wts[n][0]).astype(BF16) for n in BIG}
    shards["gdn_conv_w"] = conv_shard.reshape(2, D_MODEL)
    links = _Links(shards, me)
    p = {"ffn1_norm": wts["ffn1_norm"], "mix_norm": wts["mix_norm"], "ffn2_norm": wts["ffn2_norm"], "final_norm": wts["final_norm"].reshape(1, D_MODEL),
         "lbl": wts["hgrn_lb_logits"], "hgrn_out_norm": wts["hgrn_out_norm"], "gdn_out_norm": wts["gdn_out_norm"],
         "alog": _pad_lanes(wts["gdn_a_log"]), "dtb": _pad_lanes(wts["gdn_dt_bias"])}

    loss, dx, g = _local_step(x[0], loss_target[0], p, links)

    small_parts = _exchange([(_pack_small(g, loss), "gather")], "gather_small")[0]
    landed = links.landed(small_parts)

    big = [{} for _ in range(4)]
    for n in BIG:
        res = _adam(landed[n], _shard_rows(n, wts[n][0]), _shard_rows(n, mom[n][0]), _shard_rows(n, var[n][0]), "adam_" + n)
        for kind in range(4):
            big[kind][n] = _shard_rows(n, res[kind])
    n_vec = SMALL_ROWS - 4 * CONV_K
    small_raw = _adam(small_parts[:, :n_vec], _pack_small_state(wts), _pack_small_state(mom), _pack_small_state(var), "adam_small")
    small = [_unpack_small(o) for o in small_raw]
    loss_total = small_raw[0][6, 4 * HEAD]
    conv_parts = small_parts[:, n_vec:].reshape(N_DEV, CONV_K, 4 * D_MODEL)
    width = 4 * D_MODEL // N_DEV
    conv_mine = lax.dynamic_slice_in_dim(conv_parts, me * width, width, axis=2)
    conv = _adam(conv_mine, conv_shard, mom["gdn_conv_w"][0], var["gdn_conv_w"][0], "adam_conv")

    outs = []
    for kind in range(4):
        for n in NAMES:
            if n in BIG:
                outs.append(big[kind][n][None])
            elif n == "gdn_conv_w":
                outs.append(conv[kind][None])
            else:
                outs.append(small[kind][n])
    return (loss_total, dx[None], *outs)
```

```python
import functools

import jax
import jax.numpy as jnp
from jax import lax
from jax.experimental import pallas as pl
from jax.experimental.pallas import tpu as pltpu

F32 = jnp.float32
BF16 = jnp.bfloat16
HIGHEST = lax.Precision.HIGHEST
MESH_IDS = pl.DeviceIdType.MESH

D_MODEL = 1024
D_FF = 2816
N_DEV = 8
EPS = 1e-6
HEAD = 128
HG_HEADS = 8
GDN_QK_HEADS = 8
GDN_V_HEADS = 16
GDN_CHUNK = 64
HG_CHUNK = 16
CONV_K = 4
IN_WIDTH = 12320
COL_HQ, COL_HF, COL_HI, COL_HG, COL_GQ, COL_GK, COL_GV = 0, 8, 16, 24, 32, 40, 48
COL_GZ, COL_GATE_H, COL_GATE_G = 0, 16, 24
VMEM_LIMIT = 56 * 1024 * 1024

ADAM_LR, ADAM_B1, ADAM_B2, ADAM_EPS, ADAM_WD, ADAM_STEP = 0.001, 0.9, 0.999, 1e-08, 0.01, 10

SDS = jax.ShapeDtypeStruct


def _params(n_axes):
    return pltpu.CompilerParams(dimension_semantics=("arbitrary",) * n_axes, vmem_limit_bytes=VMEM_LIMIT)


def _tile(n, candidates=(512, 384, 256, 128, 64, 32, 16, 8)):
    for c in candidates:
        if n % c == 0:
            return c
    return n


_DIMS = {"nn": ((1,), (0,)), "nt": ((1,), (1,)), "tn": ((0,), (0,))}


def _bdot_raw(a, b, dims):
    return lax.dot_general(a.astype(BF16), b.astype(BF16), (_DIMS[dims], ((), ())), preferred_element_type=F32)


@functools.partial(jax.custom_vjp, nondiff_argnums=(2,))
def _bdot(a, b, dims):
    return _bdot_raw(a, b, dims)


def _bdot_fwd(a, b, dims):
    return _bdot_raw(a, b, dims), (a, b)


def _bdot_bwd(dims, res, ct):
    a, b = res
    if dims == "nn":
        return _bdot_raw(ct, b, "nt"), _bdot_raw(a, ct, "tn")
    if dims == "nt":
        return _bdot_raw(ct, b, "nn"), _bdot_raw(ct, a, "tn")
    return _bdot_raw(b, ct, "nt"), _bdot_raw(a, ct, "nn")


_bdot.defvjp(_bdot_fwd, _bdot_bwd)


def _hdot_raw(a, b):
    return jnp.dot(a, b, precision=HIGHEST, preferred_element_type=F32)


MM_VMEM_BUDGET = 38 * 1024 * 1024
TOKEN = (8, HEAD)


def _mm_tiles(m, n, k, a_bytes, b_bytes, o_bytes, r_bytes, m_align=8):
    def need(tm, tn, tk):
        return 2 * (tm * tk * a_bytes + tk * tn * b_bytes + tm * tn * (o_bytes + r_bytes)) + (tm * tn * 4 if tk < k else 0)

    def shrink(tm, tn, tk, floor_m, floor_n):
        while need(tm, tn, tk) > MM_VMEM_BUDGET:
            if tn > floor_n and tn % 256 == 0 and tn >= tm:
                tn //= 2
            elif tm > floor_m and tm % (2 * m_align) == 0:
                tm //= 2
            elif tn > floor_n and tn % 256 == 0:
                tn //= 2
            else:
                return None
        return tm, tn, tk

    tm = _tile(m, (1408, 1024, 704, 512, 256, 128, 64, 32, 16, 8))
    tn = _tile(n, (1408, 1024, 512, 256, 128))
    whole = shrink(tm, tn, k, min(tm, 512), min(tn, 512))
    if whole is not None:
        return whole
    tk = _tile(k, (2048, 1408, 1024, 512, 256, 128, 64, 32, 16, 8))
    while True:
        fit = shrink(tm, tn, tk, min(tm, 256), min(tn, 512))
        if fit is not None or tk <= 512 or tk % 256:
            return fit if fit is not None else (tm, tn, tk)
        tk //= 2


def _mm(a, b, dims, out_dtype, name, res=None, alpha=1.0, after=None, b_rows=None):
    b_shape = b.shape if b_rows is None else (b_rows, b.shape[1])
    if dims == "nn":
        (m, k), (k2, n) = a.shape, b_shape
    elif dims == "nt":
        (m, k), (n, k2) = a.shape, b_shape
    else:
        (k, m), (k2, n) = a.shape, b_shape
    assert k == k2, (a.shape, b.shape, dims)
    has_res = res is not None
    tm, tn, tk = _mm_tiles(m, n, k, a.dtype.itemsize, b.dtype.itemsize, jnp.dtype(out_dtype).itemsize, res.dtype.itemsize if has_res else 0,
                           m_align=HEAD if dims == "tn" else 8)
    nk = k // tk
    a_spec = pl.BlockSpec((tk, tm), lambda i, j, kk: (kk, i)) if dims == "tn" else pl.BlockSpec((tm, tk), lambda i, j, kk: (i, kk))
    b_spec = pl.BlockSpec((tn, tk), lambda i, j, kk: (j, kk)) if dims == "nt" else pl.BlockSpec((tk, tn), lambda i, j, kk: (kk, j))
    o_spec = pl.BlockSpec((tm, tn), lambda i, j, kk: (i, j))

    def finish(acc, r_ref, o_ref):
        out = acc * alpha if alpha != 1.0 else acc
        if has_res:
            out = r_ref[...].astype(F32) + out
        o_ref[...] = out.astype(o_ref.dtype)

    n_in = 2 + has_res + (after is not None)

    def body(*refs):
        a_ref, b_ref = refs[:2]
        r_ref = refs[2] if has_res else None
        o_ref = refs[n_in]
        p = _bdot_raw(a_ref[...], b_ref[...], dims)
        if nk == 1:
            finish(p, r_ref, o_ref)
            return
        acc_ref = refs[-1]
        kk = pl.program_id(2)

        @pl.when(kk == 0)
        def _():
            acc_ref[...] = p

        @pl.when(kk > 0)
        def _():
            acc_ref[...] += p

        @pl.when(kk == nk - 1)
        def _():
            finish(acc_ref[...], r_ref, o_ref)

    args = (a, b) + ((res,) if has_res else ()) + ((after,) if after is not None else ())
    in_specs = [a_spec, b_spec] + ([o_spec] if has_res else []) + ([pl.BlockSpec(TOKEN, lambda i, j, kk: (0, 0))] if after is not None else [])
    return pl.pallas_call(
        body, grid=(m // tm, n // tn, nk), in_specs=in_specs, out_specs=o_spec, out_shape=SDS((m, n), out_dtype),
        scratch_shapes=[pltpu.VMEM((tm, tn), F32)] if nk > 1 else [], name=name, compiler_params=_params(3),
    )(*args)


def _tmap(fn, grid, ins, outs, name):
    n_in = len(ins)
    n_ax = len(grid)

    def body(*refs):
        vals = fn(*[r[...] for r in refs[:n_in]])
        if not isinstance(vals, (tuple, list)):
            vals = (vals,)
        first_inner = pl.program_id(n_ax - 1) == 0
        first_all = first_inner
        for ax in range(n_ax - 1):
            first_all = jnp.logical_and(first_all, pl.program_id(ax) == 0)

        def put(ref, val, acc):
            val = val.astype(ref.dtype)
            if acc is None:
                ref[...] = val
                return
            first = first_inner if acc == "inner" else first_all

            @pl.when(first)
            def _():
                ref[...] = val

            @pl.when(jnp.logical_not(first))
            def _():
                ref[...] += val

        for ref, val, o in zip(refs[n_in:], vals, outs):
            put(ref, val, o[4])

    return pl.pallas_call(
        body, grid=grid,
        in_specs=[pl.BlockSpec(bs, im) for _, bs, im in ins],
        out_specs=[pl.BlockSpec(o[2], o[3]) for o in outs],
        out_shape=[SDS(o[0], o[1]) for o in outs],
        name=name, compiler_params=_params(n_ax),
    )(*[a for a, _, _ in ins])


def _rows(width, tt, off=0):
    return (tt, width), (lambda j, i: (i, off + j))


def _rms(x, g):
    x = x.astype(F32)
    return x * lax.rsqrt(jnp.mean(x * x, axis=-1, keepdims=True) + EPS) * g


def _silu(x):
    return x * jax.nn.sigmoid(x)


def _softplus(x):
    return jnp.maximum(x, 0.0) + jnp.log1p(jnp.exp(-jnp.abs(x)))


def _rms_fwd(x, g, name):
    t, d = x.shape
    tt = _tile(t, (256, 128))
    return _tmap(_rms, (1, t // tt), [(x, *_rows(d, tt)), (g, (1, d), lambda j, i: (0, 0))],
                 [((t, d), BF16, *_rows(d, tt), None)], name)[0]


def _rms_bwd(x, g, dn, dres, name):
    t, d = x.shape
    tt = _tile(t, (256, 128))

    def fn(x, g, dn, dres):
        _, vjp = jax.vjp(_rms, x, g)
        dx, dg = vjp(dn.astype(F32))
        return dres + dx, dg

    return _tmap(fn, (1, t // tt),
                 [(x, *_rows(d, tt)), (g, (1, d), lambda j, i: (0, 0)), (dn, *_rows(d, tt)), (dres, *_rows(d, tt))],
                 [((t, d), F32, *_rows(d, tt), None), ((1, d), F32, (1, d), lambda j, i: (0, 0), "inner")], name)


def _swiglu(ab):
    return _silu(ab[:, :D_FF]) * ab[:, D_FF:]


def _swiglu_fwd(ab, name):
    t = ab.shape[0]
    tt = _tile(t, (128,))
    return _tmap(_swiglu, (1, t // tt), [(ab, *_rows(2 * D_FF, tt))], [((t, D_FF), BF16, *_rows(D_FF, tt), None)], name)[0]


def _swiglu_bwd(ab, ds, name):
    t = ab.shape[0]
    tt = _tile(t, (128,))

    def fn(ab, ds):
        a, b = ab[:, :D_FF], ab[:, D_FF:]
        _, vjp = jax.vjp(lambda a, b: _silu(a) * b, a, b)
        da, db = vjp(ds.astype(F32))
        return jnp.concatenate([da, db], axis=1)

    return _tmap(fn, (1, t // tt), [(ab, *_rows(2 * D_FF, tt)), (ds, *_rows(D_FF, tt))],
                 [((t, 2 * D_FF), BF16, *_rows(2 * D_FF, tt), None)], name)[0]


def _ffn_fwd(h, g, w_in_t, w_out, tag, after):
    n = _rms_fwd(h, g, tag + "_norm")
    ab = _mm(n, w_in_t, "nt", F32, tag + "_in", after=after)
    s = _swiglu_fwd(ab, tag + "_act")
    out = _mm(s, w_out, "nn", F32, tag + "_out", res=h, alpha=0.5)
    return out, (n, ab, s)


def _ffn_bwd(h, g, w_in_t, w_out, saved, dout, tag, links):
    n, ab, s = saved
    sent = links.send({tag + "_w_out": _mm(s, dout, "tn", BF16, tag + "_dw_out", alpha=0.5)})
    ds = _mm(dout, w_out, "nt", F32, tag + "_ds", alpha=0.5, after=sent)
    dab = _swiglu_bwd(ab, ds, tag + "_dact")
    sent = links.send({tag + "_w_in": _mm(dab, n, "tn", BF16, tag + "_dw_in")})
    dn = _mm(dab, w_in_t, "nn", F32, tag + "_dn", after=sent)
    return _rms_bwd(h, g, dn, dout, tag + "_dnorm")


def _chunk_sum_matrix(n, chunk, transpose=False):
    row = lax.broadcasted_iota(jnp.int32, (n, n), 0)
    col = lax.broadcasted_iota(jnp.int32, (n, n), 1)
    if transpose:
        row, col = col, row
    return jnp.where(jnp.logical_and(col <= row, row // chunk == col // chunk), 1.0, 0.0).astype(F32)


def _hgrn_gates(hq, hf, lbl):
    lb = jax.nn.sigmoid(lbl[0:1, :] - lbl[1:2, :])
    sg = jax.nn.sigmoid(hf)
    f = lb + (1.0 - lb) * sg
    q = _silu(hq) * HEAD ** -0.5
    k = (1.0 - lb) * (1.0 - sg)
    return q, k, jnp.log(f)


def _hgrn_prep_fwd(proj, lbl):
    t = proj.shape[0]
    tt, ft = _tile(t, (256, 128)), 512

    def fn(hq, hf, lbl):
        q, k, log_f = _hgrn_gates(hq, hf, lbl)
        return q, k, _hdot_raw(_chunk_sum_matrix(tt, HG_CHUNK), log_f)

    o = ((t, D_MODEL), F32, *_rows(ft, tt), None)
    return _tmap(fn, (D_MODEL // ft, t // tt),
                 [(proj, *_rows(ft, tt, COL_HQ * HEAD // ft)), (proj, *_rows(ft, tt, COL_HF * HEAD // ft)), (lbl, (2, ft), lambda j, i: (0, j))],
                 [o, o, o], "hgrn_prep")


def _hgrn_prep_bwd(proj, lbl, dq, dk, db):
    t = proj.shape[0]
    tt, ft = _tile(t, (256, 128)), 512

    def fn(hq, hf, lbl, dq, dk, db):
        dlog_f = _hdot_raw(_chunk_sum_matrix(tt, HG_CHUNK, transpose=True), db)
        _, vjp = jax.vjp(_hgrn_gates, hq, hf, lbl)
        return vjp((dq, dk, dlog_f))

    o = ((t, D_MODEL), BF16, *_rows(ft, tt), None)
    r = _rows(ft, tt)
    return _tmap(fn, (D_MODEL // ft, t // tt),
                 [(proj, *_rows(ft, tt, COL_HQ * HEAD // ft)), (proj, *_rows(ft, tt, COL_HF * HEAD // ft)), (lbl, (2, ft), lambda j, i: (0, j)),
                  (dq, *r), (dk, *r), (db, *r)],
                 [o, o, ((2, D_MODEL), F32, (2, ft), lambda j, i: (0, j), "inner")], "hgrn_prep_bwd")


def _hgrn_chunks(q, k, v, b, st):
    n = q[0].shape[0]
    half = n // 2
    srow = lax.broadcasted_iota(jnp.int32, (half, HEAD), 0)
    inter = _each(lambda q, b, st: _bdot(q * jnp.exp(b), st, "nt"), q, b, st)

    def below_scores(q, k, b):
        ref = b[half:half + 1, :]
        return _bdot(q[half:] * jnp.exp(jnp.minimum(b[half:] - ref, 0.0)), k[:half] * jnp.exp(jnp.minimum(ref - b[:half], 0.0)), "nt")

    below = _each(lambda a, v: _bdot(a, v[:half], "nn"), _each(below_scores, q, k, b), v)

    def diagonal(q, k, v, b):
        rows = []
        for lo in (0, half):
            qb, kb, vb, bb = (a[lo:lo + half] for a in (q, k, v, b))
            for t in range(half):
                e = jnp.where(srow <= t, jnp.exp(jnp.minimum(bb[t:t + 1, :] - bb, 0.0)), 0.0)
                a = jnp.sum(qb[t:t + 1, :] * kb * e, axis=1, keepdims=True)
                rows.append(jnp.sum(a * vb, axis=0, keepdims=True))
        return jnp.concatenate(rows, axis=0)

    diag = _each(diagonal, q, k, v, b)
    o = _each(lambda inter, diag, below: inter + diag + jnp.concatenate([jnp.zeros_like(below), below], axis=0), inter, diag, below)

    def new_state(k, v, b, st):
        bend = b[n - 1:n, :]
        return st * jnp.exp(bend) + _bdot(v, k * jnp.exp(bend - b), "tn")

    return o, _each(new_state, k, v, b, st)


HG_GROUP = 4
HG_PER = GDN_CHUNK // HG_CHUNK


def _hgrn_rec_fwd(q, k, proj, b):
    t = q.shape[0]
    nc = t // GDN_CHUNK
    blk = (GDN_CHUNK, HG_GROUP * HEAD)
    im = lambda h, c: (c, h)

    def body(q_ref, k_ref, v_ref, b_ref, o_ref, hs_ref, st_ref):
        @pl.when(pl.program_id(1) == 0)
        def _():
            st_ref[...] = jnp.zeros_like(st_ref)

        heads = range(HG_GROUP)
        for j in range(HG_PER):
            sl = pl.ds(HG_CHUNK * j, HG_CHUNK)
            st = tuple(st_ref[g] for g in heads)
            o, st_new = _hgrn_chunks(*[tuple(r[sl, _head_lanes(g)] for g in heads) for r in (q_ref, k_ref, v_ref, b_ref)], st)
            for g in heads:
                hs_ref[g, j] = st[g]
                o_ref[sl, _head_lanes(g)] = o[g]
                st_ref[g] = st_new[g]

    return pl.pallas_call(
        body, grid=(HG_HEADS // HG_GROUP, nc),
        in_specs=[pl.BlockSpec(blk, im), pl.BlockSpec(blk, im), pl.BlockSpec(blk, lambda h, c: (c, COL_HI // HG_GROUP + h)), pl.BlockSpec(blk, im)],
        out_specs=[pl.BlockSpec(blk, im), pl.BlockSpec((HG_GROUP, HG_PER, HEAD, HEAD), lambda h, c: (h, c, 0, 0))],
        out_shape=[SDS((t, D_MODEL), F32), SDS((HG_HEADS, nc * HG_PER, HEAD, HEAD), F32)],
        scratch_shapes=[pltpu.VMEM((HG_GROUP, HEAD, HEAD), F32)], name="hgrn_rec", compiler_params=_params(2),
    )(q, k, proj, b)


def _hgrn_rec_bwd(q, k, proj, b, hs, do):
    t = q.shape[0]
    nc = t // GDN_CHUNK
    blk = (GDN_CHUNK, HG_GROUP * HEAD)
    im = lambda h, c: (nc - 1 - c, h)

    def body(q_ref, k_ref, v_ref, b_ref, hs_ref, do_ref, dq_ref, dk_ref, dv_ref, db_ref, dst_ref):
        @pl.when(pl.program_id(1) == 0)
        def _():
            dst_ref[...] = jnp.zeros_like(dst_ref)

        heads = range(HG_GROUP)
        for j in reversed(range(HG_PER)):
            sl = pl.ds(HG_CHUNK * j, HG_CHUNK)
            _, vjp = jax.vjp(_hgrn_chunks, *[tuple(r[sl, _head_lanes(g)] for g in heads) for r in (q_ref, k_ref, v_ref, b_ref)],
                             tuple(hs_ref[g, j] for g in heads))
            dq, dk, dv, db, dst = vjp((tuple(do_ref[sl, _head_lanes(g)] for g in heads), tuple(dst_ref[g] for g in heads)))
            for g in heads:
                ln = _head_lanes(g)
                dq_ref[sl, ln] = dq[g]
                dk_ref[sl, ln] = dk[g]
                dv_ref[sl, ln] = dv[g].astype(dv_ref.dtype)
                db_ref[sl, ln] = db[g]
                dst_ref[g] = dst[g]

    spec = pl.BlockSpec(blk, im)
    return pl.pallas_call(
        body, grid=(HG_HEADS // HG_GROUP, nc),
        in_specs=[spec, spec, pl.BlockSpec(blk, lambda h, c: (nc - 1 - c, COL_HI // HG_GROUP + h)), spec,
                  pl.BlockSpec((HG_GROUP, HG_PER, HEAD, HEAD), lambda h, c: (h, nc - 1 - c, 0, 0)), spec],
        out_specs=[spec, spec, spec, spec],
        out_shape=[SDS((t, D_MODEL), F32), SDS((t, D_MODEL), F32), SDS((t, D_MODEL), BF16), SDS((t, D_MODEL), F32)],
        scratch_shapes=[pltpu.VMEM((HG_GROUP, HEAD, HEAD), F32)], name="hgrn_rec_bwd", compiler_params=_params(2),
    )(q, k, proj, b, hs, do)


def _shift_down(x, d):
    if d == 0:
        return x
    row = lax.broadcasted_iota(jnp.int32, x.shape, 0)
    return jnp.where(row >= d, pltpu.roll(x, d, 0), 0.0)


def _shift_up(x, d):
    if d == 0:
        return x
    n = x.shape[0]
    row = lax.broadcasted_iota(jnp.int32, x.shape, 0)
    return jnp.where(row < n - d, pltpu.roll(x, n - d, 0), 0.0)


def _conv_fwd(proj, conv_w):
    t = proj.shape[0]
    width = 2 * D_MODEL + 2 * D_MODEL

    def body(x_ref, w_ref, c_ref):
        x, w = x_ref[...], w_ref[...]
        y = w[CONV_K - 1:CONV_K, :] * x
        for j in range(CONV_K - 1):
            y = y + w[j:j + 1, :] * _shift_down(x, CONV_K - 1 - j)
        c_ref[...] = _silu(y)

    return pl.pallas_call(
        body, grid=(width // HEAD,),
        in_specs=[pl.BlockSpec((t, HEAD), lambda j: (0, COL_GQ + j)), pl.BlockSpec((CONV_K, HEAD), lambda j: (0, j))],
        out_specs=pl.BlockSpec((t, HEAD), lambda j: (0, j)), out_shape=SDS((t, width), F32),
        name="gdn_conv", compiler_params=_params(1),
    )(proj, conv_w)


def _conv_bwd(proj, conv_w, dc_qk, dc_v):
    t = proj.shape[0]
    n_qk = dc_qk.shape[1] // HEAD
    width = dc_qk.shape[1] + dc_v.shape[1]

    def body(x_ref, w_ref, dqk_ref, dv_ref, dx_ref, dw_ref):
        x, w = x_ref[...], w_ref[...]
        xs = [_shift_down(x, CONV_K - 1 - j) for j in range(CONV_K)]
        y = w[0:1, :] * xs[0]
        for j in range(1, CONV_K):
            y = y + w[j:j + 1, :] * xs[j]
        sg = jax.nn.sigmoid(y)
        dc = jnp.where(pl.program_id(0) < n_qk, dqk_ref[...], dv_ref[...])
        dy = dc * (sg * (1.0 + y * (1.0 - sg)))
        dx = w[CONV_K - 1:CONV_K, :] * dy
        for j in range(CONV_K - 1):
            dx = dx + w[j:j + 1, :] * _shift_up(dy, CONV_K - 1 - j)
        dx_ref[...] = dx.astype(dx_ref.dtype)
        dw_ref[...] = jnp.concatenate([jnp.sum(xs[j] * dy, axis=0, keepdims=True) for j in range(CONV_K)], axis=0)

    return pl.pallas_call(
        body, grid=(width // HEAD,),
        in_specs=[pl.BlockSpec((t, HEAD), lambda j: (0, COL_GQ + j)), pl.BlockSpec((CONV_K, HEAD), lambda j: (0, j)),
                  pl.BlockSpec((t, HEAD), lambda j: (0, jnp.minimum(j, n_qk - 1))), pl.BlockSpec((t, HEAD), lambda j: (0, jnp.maximum(j - n_qk, 0)))],
        out_specs=[pl.BlockSpec((t, HEAD), lambda j: (0, j)), pl.BlockSpec((CONV_K, HEAD), lambda j: (0, j))],
        out_shape=[SDS((t, width), BF16), SDS((CONV_K, width), F32)],
        name="gdn_conv_bwd", compiler_params=_params(1),
    )(proj, conv_w, dc_qk, dc_v)


def _l2norm(x, scale):
    return x * lax.rsqrt(jnp.sum(x * x, axis=-1, keepdims=True) + EPS) * scale


def _head(a, h):
    return a[:, h * HEAD:(h + 1) * HEAD]


def _qk_scale(h):
    return HEAD ** -0.5 if h < GDN_QK_HEADS else 1.0


def _qk_norm_fwd(c):
    t = c.shape[0]
    tt = _tile(t, (256, 128))
    width = 2 * D_MODEL

    def fn(x):
        return jnp.concatenate([_l2norm(_head(x, h), _qk_scale(h)) for h in range(2 * GDN_QK_HEADS)], axis=1)

    return _tmap(fn, (1, t // tt), [(c, *_rows(width, tt))], [((t, width), F32, *_rows(width, tt), None)], "gdn_qk_norm")[0]


def _qk_norm_bwd(c, dq_rep, dk_rep):
    t = c.shape[0]
    tt = _tile(t, (256, 128))
    width = 2 * D_MODEL

    def fn(x, dq2, dk2):
        out = []
        for h in range(2 * GDN_QK_HEADS):
            d2, hh = (dq2, h) if h < GDN_QK_HEADS else (dk2, h - GDN_QK_HEADS)
            _, vjp = jax.vjp(lambda x: _l2norm(x, _qk_scale(h)), _head(x, h))
            out.append(vjp(_head(d2, 2 * hh) + _head(d2, 2 * hh + 1))[0])
        return jnp.concatenate(out, axis=1)

    r = _rows(width, tt)
    return _tmap(fn, (1, t // tt), [(c, *r), (dq_rep, *r), (dk_rep, *r)], [((t, width), F32, *r, None)], "gdn_qk_norm_bwd")[0]


def _gdn_gates(x, alog, dtb):
    return -jnp.exp(alog) * _softplus(x + dtb), jax.nn.sigmoid(x)


def _gates_fwd(pab, alog, dtb):
    t = pab.shape[0]
    tt = _tile(t, (256, 128))

    def fn(x, alog, dtb):
        g, beta = _gdn_gates(x, alog, dtb)
        lane = lax.broadcasted_iota(jnp.int32, g.shape, 1)
        return jnp.where(lane < GDN_V_HEADS, _hdot_raw(_chunk_sum_matrix(tt, GDN_CHUNK), g), beta).T

    p = (alog, (1, HEAD), lambda j, i: (0, 0)), (dtb, (1, HEAD), lambda j, i: (0, 0))
    return _tmap(fn, (1, t // tt), [(pab, *_rows(HEAD, tt)), *p], [((HEAD, t), F32, (HEAD, tt), lambda j, i: (0, i), None)], "gdn_gates")[0]


def _gates_bwd(pab, alog, dtb, dout_t):
    t = pab.shape[0]
    tt = _tile(t, (256, 128))

    def fn(x, alog, dtb, dout_t):
        dout = dout_t.T
        lane = lax.broadcasted_iota(jnp.int32, dout.shape, 1)
        dgam = jnp.where(lane < GDN_V_HEADS, dout, 0.0)
        dbeta = jnp.where(jnp.logical_and(lane >= GDN_V_HEADS, lane < 2 * GDN_V_HEADS), dout, 0.0)
        dg = _hdot_raw(_chunk_sum_matrix(tt, GDN_CHUNK, transpose=True), dgam)
        _, vjp = jax.vjp(_gdn_gates, x, alog, dtb)
        return vjp((dg, dbeta))

    p = (alog, (1, HEAD), lambda j, i: (0, 0)), (dtb, (1, HEAD), lambda j, i: (0, 0))
    acc = ((1, HEAD), F32, (1, HEAD), lambda j, i: (0, 0), "inner")
    return _tmap(fn, (1, t // tt), [(pab, *_rows(HEAD, tt)), *p, (dout_t, (HEAD, tt), lambda j, i: (0, i))],
                 [((t, HEAD), BF16, *_rows(HEAD, tt), None), acc, acc], "gdn_gates_bwd")


def _split_bf16(x):
    hi = x.astype(BF16)
    return hi, (x - hi.astype(F32)).astype(BF16)


def _dot3(a, b):
    (ah, al), (bh, bl) = a, b
    return _bdot_raw(ah, bh, "nn") + (_bdot_raw(ah, bl, "nn") + _bdot_raw(al, bh, "nn"))


def _each(fn, *lists):
    return tuple(fn(*xs) for xs in zip(*lists))


def _unit_lower_inverses_raw(a):
    n = a[0].shape[0]
    row = lax.broadcasted_iota(jnp.int32, (n, n), 0)
    col = lax.broadcasted_iota(jnp.int32, (n, n), 1)
    eye = jnp.where(row == col, 1.0, 0.0).astype(F32)
    p = _each(lambda a: eye - a, a)
    x = _each(_split_bf16, a)
    m = 2
    while m < 2 * n:
        x = _each(_split_bf16, _each(_dot3, x, x))
        p = _each(lambda p, x: p + _dot3(_split_bf16(p), x), p, x)
        m *= 2
    return p


@jax.custom_vjp
def _unit_lower_inverses(a, known):
    return _unit_lower_inverses_raw(a) if known is None else known


def _uli_fwd(a, known):
    inv = _unit_lower_inverses(a, known)
    return inv, (inv, known)


def _uli_bwd(res, ct):
    inv, known = res
    right = _each(lambda ct, inv: _bdot_raw(ct, inv, "nt"), ct, inv)
    da = _each(lambda inv, r: -_bdot_raw(inv, r, "tn"), inv, right)
    return da, (None if known is None else _each(jnp.zeros_like, known))


_unit_lower_inverses.defvjp(_uli_fwd, _uli_bwd)


def _gdn_chunks(q, k, v, beta_rows, gam_rows, s, inv_known=None):
    n = q[0].shape[0]
    heads = range(len(q))
    row = lax.broadcasted_iota(jnp.int32, (n, n), 0)
    col = lax.broadcasted_iota(jnp.int32, (n, n), 1)
    beta_cols, gam_cols = beta_rows.T, gam_rows.T
    beta = tuple(beta_cols[:, g:g + 1] for g in heads)
    gam = tuple(gam_cols[:, g:g + 1] for g in heads)
    gam_row = tuple(gam_rows[g:g + 1, :] for g in heads)
    decay = _each(lambda gam, gam_row: jnp.where(row >= col, jnp.exp(jnp.minimum(gam - gam_row, 0.0)), 0.0), gam, gam_row)
    kb = _each(lambda k, beta: k * beta, k, beta)
    a = _each(lambda kb, k, decay: jnp.where(row > col, _bdot(kb, k, "nt") * decay, 0.0), kb, k, decay)
    inv = _unit_lower_inverses(a, inv_known)
    eg = _each(jnp.exp, gam)
    u = _each(lambda inv, v, beta: _bdot(inv, v * beta, "nn"), inv, v, beta)
    w = _each(lambda inv, kb, eg: _bdot(inv, kb * eg, "nn"), inv, kb, eg)
    qk = _each(lambda q, k, decay: _bdot(q, k, "nt") * decay, q, k, decay)
    v_new = _each(lambda u, w, s: u - _bdot(w, s, "nn"), u, w, s)
    o_state = _each(lambda q, eg, s: _bdot(q * eg, s, "nn"), q, eg, s)
    o = _each(lambda o_state, qk, v_new: o_state + _bdot(qk, v_new, "nn"), o_state, qk, v_new)
    gend = _each(lambda gam: gam[n - 1:n, :], gam)
    s_new = _each(lambda s, k, gam, gend, v_new: s * jnp.exp(gend) + _bdot(k * jnp.exp(gend - gam), v_new, "tn"), s, k, gam, gend, v_new)
    return o, s_new, inv


GDN_GROUP = 16


def _gdn_specs(nc, rev):
    cc = (lambda c: nc - 1 - c) if rev else (lambda c: c)
    grp = GDN_GROUP
    q = pl.BlockSpec((GDN_CHUNK, grp // 2 * HEAD), lambda h, c: (cc(c), h))
    k = pl.BlockSpec((GDN_CHUNK, grp // 2 * HEAD), lambda h, c: (cc(c), 2 * GDN_QK_HEADS // grp + h))
    v = pl.BlockSpec((GDN_CHUNK, grp * HEAD), lambda h, c: (cc(c), 2 * GDN_QK_HEADS // grp + h))
    o = pl.BlockSpec((GDN_CHUNK, grp * HEAD), lambda h, c: (cc(c), h))
    rw = pl.BlockSpec((grp, None, 1, GDN_CHUNK), lambda h, c: (h, cc(c), 0, 0))
    st = pl.BlockSpec((grp, None, HEAD, HEAD), lambda h, c: (h, cc(c), 0, 0))
    inv = pl.BlockSpec((grp, None, GDN_CHUNK, GDN_CHUNK), lambda h, c: (h, cc(c), 0, 0))
    return q, k, v, o, rw, st, inv


def _head_lanes(g, per=1):
    return pl.ds((g // per) * HEAD, HEAD)


def _gdn_rec_fwd(qk, c, beta_row, gam_row):
    t = qk.shape[0]
    nc = t // GDN_CHUNK
    q, k, v, o, rw, st, inv = _gdn_specs(nc, False)

    def body(q_ref, k_ref, v_ref, be_ref, gr_ref, o_ref, ss_ref, inv_ref, s_ref):
        @pl.when(pl.program_id(1) == 0)
        def _():
            s_ref[...] = jnp.zeros_like(s_ref)

        heads = range(GDN_GROUP)
        s = tuple(s_ref[g] for g in heads)
        out, s_new, inv_c = _gdn_chunks(
            tuple(q_ref[:, _head_lanes(g, 2)] for g in heads), tuple(k_ref[:, _head_lanes(g, 2)] for g in heads),
            tuple(v_ref[:, _head_lanes(g)] for g in heads), be_ref[:, 0, :], gr_ref[:, 0, :], s)
        for g in heads:
            ss_ref[g] = s[g]
            o_ref[:, _head_lanes(g)] = out[g]
            inv_ref[g] = inv_c[g]
            s_ref[g] = s_new[g]

    return pl.pallas_call(
        body, grid=(GDN_V_HEADS // GDN_GROUP, nc), in_specs=[q, k, v, rw, rw], out_specs=[o, st, inv],
        out_shape=[SDS((t, 2 * D_MODEL), F32), SDS((GDN_V_HEADS, nc, HEAD, HEAD), F32), SDS((GDN_V_HEADS, nc, GDN_CHUNK, GDN_CHUNK), F32)],
        scratch_shapes=[pltpu.VMEM((GDN_GROUP, HEAD, HEAD), F32)], name="gdn_rec", compiler_params=_params(2),
    )(qk, qk, c, beta_row, gam_row)


def _gdn_rec_bwd(qk, c, beta_row, gam_row, ss, invs, do):
    t = qk.shape[0]
    nc = t // GDN_CHUNK
    q, k, v, o, rw, st, inv = _gdn_specs(nc, True)

    def body(q_ref, k_ref, v_ref, be_ref, gr_ref, ss_ref, inv_ref, do_ref,
             dq_ref, dk_ref, dv_ref, dbe_ref, dgr_ref, ds_ref):
        @pl.when(pl.program_id(1) == 0)
        def _():
            ds_ref[...] = jnp.zeros_like(ds_ref)

        heads = range(GDN_GROUP)
        _, vjp = jax.vjp(
            _gdn_chunks,
            tuple(q_ref[:, _head_lanes(g, 2)] for g in heads), tuple(k_ref[:, _head_lanes(g, 2)] for g in heads),
            tuple(v_ref[:, _head_lanes(g)] for g in heads), be_ref[:, 0, :], gr_ref[:, 0, :],
            tuple(ss_ref[g] for g in heads), tuple(inv_ref[g] for g in heads))
        no_inv_ct = tuple(jnp.zeros((GDN_CHUNK, GDN_CHUNK), F32) for g in heads)
        dq, dk, dv, dbe, dgr, ds, _ = vjp((tuple(do_ref[:, _head_lanes(g)] for g in heads), tuple(ds_ref[g] for g in heads), no_inv_ct))
        for g in heads:
            dq_ref[:, _head_lanes(g)] = dq[g]
            dk_ref[:, _head_lanes(g)] = dk[g]
            dv_ref[:, _head_lanes(g)] = dv[g]
            ds_ref[g] = ds[g]
        dbe_ref[:, 0, :] = dbe
        dgr_ref[:, 0, :] = dgr

    wide = SDS((t, 2 * D_MODEL), F32)
    rowshape = SDS((GDN_V_HEADS, nc, 1, GDN_CHUNK), F32)
    return pl.pallas_call(
        body, grid=(GDN_V_HEADS // GDN_GROUP, nc), in_specs=[q, k, v, rw, rw, st, inv, o], out_specs=[o, o, o, rw, rw],
        out_shape=[wide, wide, wide, rowshape, rowshape],
        scratch_shapes=[pltpu.VMEM((GDN_GROUP, HEAD, HEAD), F32)], name="gdn_rec_bwd", compiler_params=_params(2),
    )(qk, qk, c, beta_row, gam_row, ss, invs, do)


def _gated_norm(o, gate, w):
    return _rms(o, w) * _silu(gate)


def _post_fwd(o, proj, col_off, w, name):
    t, width = o.shape
    tt = _tile(t, (256, 128))

    def fn(o, gate, w):
        return jnp.concatenate([_gated_norm(_head(o, h), _head(gate, h), w) for h in range(width // HEAD)], axis=1)

    return _tmap(fn, (1, t // tt),
                 [(o, *_rows(width, tt)), (proj, *_rows(width, tt, col_off * HEAD // width)), (w, (1, HEAD), lambda j, i: (0, 0))],
                 [((t, width), BF16, *_rows(width, tt), None)], name)[0]


def _post_bwd(o, proj, col_off, w, dout, name):
    t, width = o.shape
    tt = _tile(t, (256, 128))

    def fn(o, gate, w, dout):
        do, dgate, dw = [], [], jnp.zeros((1, HEAD), F32)
        for h in range(width // HEAD):
            _, vjp = jax.vjp(_gated_norm, _head(o, h), _head(gate, h), w)
            a, b, c = vjp(_head(dout, h))
            do.append(a)
            dgate.append(b)
            dw = dw + c
        return jnp.concatenate(do, axis=1), jnp.concatenate(dgate, axis=1), dw

    r = _rows(width, tt)
    return _tmap(fn, (1, t // tt),
                 [(o, *r), (proj, *_rows(width, tt, col_off * HEAD // width)), (w, (1, HEAD), lambda j, i: (0, 0)), (dout, *r)],
                 [((t, width), F32, *r, None), ((t, width), BF16, *r, None), ((1, HEAD), F32, (1, HEAD), lambda j, i: (0, 0), "inner")], name)


def _merge(gate_h, gate_g, yh, yg):
    return jax.nn.sigmoid(gate_h) * yh + jax.nn.sigmoid(gate_g) * yg


def _merge_fwd(proj, yh, yg):
    t = yh.shape[0]
    tt, ft = _tile(t, (256, 128)), 512
    r = _rows(ft, tt)
    return _tmap(_merge, (D_MODEL // ft, t // tt),
                 [(proj, *_rows(ft, tt, COL_GATE_H * HEAD // ft)), (proj, *_rows(ft, tt, COL_GATE_G * HEAD // ft)), (yh, *r), (yg, *r)],
                 [((t, D_MODEL), BF16, *r, None)], "merge")[0]


def _merge_bwd(proj, yh, yg, dy):
    t = yh.shape[0]
    tt, ft = _tile(t, (256, 128)), 512
    r = _rows(ft, tt)

    def fn(gate_h, gate_g, yh, yg, dy):
        _, vjp = jax.vjp(_merge, gate_h, gate_g, yh, yg)
        return vjp(dy)

    o = ((t, D_MODEL), BF16, *r, None)
    return _tmap(fn, (D_MODEL // ft, t // tt),
                 [(proj, *_rows(ft, tt, COL_GATE_H * HEAD // ft)), (proj, *_rows(ft, tt, COL_GATE_G * HEAD // ft)), (yh, *r), (yg, *r), (dy, *r)],
                 [o, o, o, o], "merge_bwd")


def _loss_head(h, target, g):
    t, d = h.shape
    tt = _tile(t, (256, 128))

    def fn(h, target, g):
        def f(h, g):
            err = _rms(h, g) - target
            return 0.5 * jnp.sum(jnp.mean(err * err, axis=-1))

        loss, (dh, dg) = jax.value_and_grad(f, (0, 1))(h, g)
        return dh, dg, jnp.full((1, HEAD), loss, F32)

    return _tmap(fn, (1, t // tt), [(h, *_rows(d, tt)), (target, *_rows(d, tt)), (g, (1, d), lambda j, i: (0, 0))],
                 [((t, d), F32, *_rows(d, tt), None), ((1, d), F32, (1, d), lambda j, i: (0, 0), "inner"),
                  ((1, HEAD), F32, (1, HEAD), lambda j, i: (0, 0), "inner")], "loss_head")


def _mixer_fwd(h, p, links):
    t = h.shape[0]
    nc = t // GDN_CHUNK
    u = _rms_fwd(h, p["mix_norm"], "mix_norm")
    w = {n: links.weight(n, h) for n in ("w_in_t", "w_in_b_t", "w_in_ab_t", "conv_w")}
    proj = _mm(u, w["w_in_t"], "nt", F32, "mix_in", after=links.started, b_rows=SCALAR_ROWS)
    proj_b = _mm(u, w["w_in_b_t"], "nt", F32, "mix_in_b")
    pab = _mm(u, w["w_in_ab_t"], "nt", F32, "mix_in_ab")
    qh, kh, bh = _hgrn_prep_fwd(proj, p["lbl"])
    oh, hs = _hgrn_rec_fwd(qh, kh, proj, bh)
    c = _conv_fwd(proj, w["conv_w"])
    qk = _qk_norm_fwd(c)
    gates_t = _gates_fwd(pab, p["alog"], p["dtb"])
    gam_row = gates_t[:GDN_V_HEADS].reshape(GDN_V_HEADS, nc, 1, GDN_CHUNK)
    beta_row = gates_t[GDN_V_HEADS:2 * GDN_V_HEADS].reshape(GDN_V_HEADS, nc, 1, GDN_CHUNK)
    og, ss, invs = _gdn_rec_fwd(qk, c, beta_row, gam_row)
    ohn = _post_fwd(oh, proj, COL_HG, p["hgrn_out_norm"], "hgrn_out")
    ogn = _post_fwd(og, proj_b, COL_GZ, p["gdn_out_norm"], "gdn_out")
    w.update({n: links.weight(n, ogn) for n in ("w_branch_hgrn", "w_branch_gdn", "w_out")})
    yh = _mm(ohn, w["w_branch_hgrn"], "nn", F32, "branch_hgrn")
    yg = _mm(ogn, w["w_branch_gdn"], "nn", F32, "branch_gdn")
    y = _merge_fwd(proj_b, yh, yg)
    out = _mm(y, w["w_out"], "nn", F32, "mix_out", res=h)
    saved = (w, u, proj, proj_b, pab, qh, kh, bh, oh, hs, c, qk, beta_row, gam_row, og, ss, invs, ohn, ogn, yh, yg, y)
    return out, saved


def _mixer_bwd(h, p, links, saved, dout):
    (w, u, proj, proj_b, pab, qh, kh, bh, oh, hs, c, qk, beta_row, gam_row, og, ss, invs, ohn, ogn, yh, yg, y) = saved
    t = h.shape[0]
    grads = {}
    dw_out = _mm(y, dout, "tn", BF16, "mix_out_dw")
    dy = _mm(dout, w["w_out"], "nt", F32, "mix_out_dx")
    dgate_h, dgate_g, dyh, dyg = _merge_bwd(proj_b, yh, yg, dy)
    dw_bh = _mm(ohn, dyh, "tn", BF16, "branch_hgrn_dw")
    dw_bg = _mm(ogn, dyg, "tn", BF16, "branch_gdn_dw")
    sent = links.send({"w_out": dw_out, "w_branch_hgrn": dw_bh, "w_branch_gdn": dw_bg})
    dohn = _mm(dyh, w["w_branch_hgrn"], "nt", F32, "branch_hgrn_dx", after=sent)
    dogn = _mm(dyg, w["w_branch_gdn"], "nt", F32, "branch_gdn_dx")
    doh, dhg, grads["hgrn_out_norm"] = _post_bwd(oh, proj, COL_HG, p["hgrn_out_norm"], dohn, "hgrn_out_bwd")
    dog, dgz, grads["gdn_out_norm"] = _post_bwd(og, proj_b, COL_GZ, p["gdn_out_norm"], dogn, "gdn_out_bwd")
    dqh, dkh, dhi, dbh = _hgrn_rec_bwd(qh, kh, proj, bh, hs, doh)
    dhq, dhf, grads["lbl"] = _hgrn_prep_bwd(proj, p["lbl"], dqh, dkh, dbh)
    dqv, dkv, dcv, dbeta_row, dgam_row = _gdn_rec_bwd(qk, c, beta_row, gam_row, ss, invs, dog)
    dcqk = _qk_norm_bwd(c, dqv, dkv)
    dxin, grads["conv_w"] = _conv_bwd(proj, w["conv_w"], dcqk, dcv)
    dgates_t = jnp.concatenate([dgam_row.reshape(GDN_V_HEADS, t), dbeta_row.reshape(GDN_V_HEADS, t),
                                jnp.zeros((HEAD - 2 * GDN_V_HEADS, t), F32)], axis=0)
    dpab, grads["alog"], grads["dtb"] = _gates_bwd(pab, p["alog"], p["dtb"], dgates_t)
    dproj = jnp.concatenate([dhq, dhf, dhi, dhg, dxin], axis=1)
    dproj_b = jnp.concatenate([dgz, dgate_h, dgate_g], axis=1)
    dw_t = _mm(dproj, u, "tn", BF16, "mix_in_dw")
    dw_b_t = _mm(dproj_b, u, "tn", BF16, "mix_in_b_dw")
    dw_ab_t = _mm(dpab, u, "tn", BF16, "mix_in_ab_dw")
    sent = links.send({"w_in": jnp.concatenate([dw_t, dw_ab_t[:N_SCALAR], dw_b_t], axis=0)})
    du = _mm(dproj, w["w_in_t"], "nn", F32, "mix_in_dx", after=sent, b_rows=SCALAR_ROWS)
    du = _mm(dproj_b, w["w_in_b_t"], "nn", F32, "mix_in_b_dx", res=du)
    du = _mm(dpab, w["w_in_ab_t"], "nn", F32, "mix_in_ab_dx", res=du)
    dh, grads["mix_norm"] = _rms_bwd(h, p["mix_norm"], du, dout, "mix_norm_bwd")
    return dh, grads


def _local_step(x, target, p, links):
    w1 = {n: links.weight(n, x) for n in ("ffn1_w_in", "ffn1_w_out")}
    h1, s1 = _ffn_fwd(x, p["ffn1_norm"], w1["ffn1_w_in"], w1["ffn1_w_out"], "ffn1", links.started)
    h2, sm = _mixer_fwd(h1, p, links)
    w2 = {n: links.weight(n, h2) for n in ("ffn2_w_in", "ffn2_w_out")}
    h3, s2 = _ffn_fwd(h2, p["ffn2_norm"], w2["ffn2_w_in"], w2["ffn2_w_out"], "ffn2", None)
    dh3, dfinal, loss = _loss_head(h3, target, p["final_norm"])
    g = {"final_norm": dfinal}
    dh2, g["ffn2_norm"] = _ffn_bwd(h2, p["ffn2_norm"], w2["ffn2_w_in"], w2["ffn2_w_out"], s2, dh3, "ffn2", links)
    dh1, gm = _mixer_bwd(h1, p, links, sm, dh2)
    g.update(gm)
    dx, g["ffn1_norm"] = _ffn_bwd(x, p["ffn1_norm"], w1["ffn1_w_in"], w1["ffn1_w_out"], s1, dh1, "ffn1", links)
    return loss, dx, g


HBM_SPEC = pl.BlockSpec(memory_space=pltpu.HBM)
SEM_SPEC = pl.BlockSpec(memory_space=pltpu.SEMAPHORE)
DATAFLOW = pltpu.SideEffectType.DATAFLOW_SIDE_EFFECTING


def _position():
    x, y, c = lax.axis_index("x"), lax.axis_index("y"), lax.axis_index("c")
    return x, y, c, 4 * x + 2 * y + c


def _relations(x, y, c):
    for rel in range(1, N_DEV):
        px = 1 - x if rel & 4 else x
        py = 1 - y if rel & 2 else y
        pc = 1 - c if rel & 1 else c
        yield rel, (px, py, pc), 4 * px + 2 * py + pc


def _sem_index(item, rel):
    return item * (N_DEV - 1) + rel - 1


def _landing(a, mode):
    return lax.empty((N_DEV,) + a.shape if mode == "gather" else a.shape, a.dtype)


ALL_PEERS = tuple(range(1, N_DEV))
ONE_PER_CHIP = (1, 2, 4, 6)


def _copies_start(groups, name, rels=ALL_PEERS):
    flat = [item for grp in groups for item in grp]
    n, ng = len(flat), len(groups)
    lands = [_landing(a, mode) for a, mode in flat]

    def body(*refs):
        src_refs, land_refs, sems, token = refs[:n], refs[n:2 * n], refs[2 * n:2 * n + 2 * ng], refs[-1]
        x, y, c, me = _position()
        for rel, where, peer in _relations(x, y, c):
            if rel not in rels:
                continue
            k = 0
            for gi, grp in enumerate(groups):
                for li, (_, mode) in enumerate(grp):
                    src = src_refs[k] if mode == "gather" else src_refs[k].at[peer]
                    pltpu.make_async_remote_copy(src_ref=src, dst_ref=land_refs[k].at[me], send_sem=sems[2 * gi].at[_sem_index(li, rel)],
                                                 recv_sem=sems[2 * gi + 1].at[_sem_index(li, rel)], device_id=where, device_id_type=MESH_IDS).start()
                    k += 1
        token[...] = jnp.zeros_like(token)

    sem_shapes = [pltpu.SemaphoreType.DMA((len(grp) * (N_DEV - 1),)) for grp in groups for _ in range(2)]
    thru = [pltpu.HBM(a.shape, a.dtype) for a, _ in flat] + [pltpu.HBM(l.shape, l.dtype) for l in lands]
    outs = pl.pallas_call(
        body, name=name, out_shape=(*sem_shapes, *thru, SDS((8, HEAD), F32)),
        in_specs=[HBM_SPEC] * (2 * n), out_specs=(*[SEM_SPEC] * (2 * ng), *[HBM_SPEC] * (2 * n), pl.BlockSpec(memory_space=pltpu.VMEM)),
        input_output_aliases={i: 2 * ng + i for i in range(2 * n)}, compiler_params=pltpu.CompilerParams(has_side_effects=DATAFLOW),
    )(*[pltpu.with_memory_space_constraint(a, pltpu.HBM) for a, _ in flat], *[pltpu.with_memory_space_constraint(l, pltpu.HBM) for l in lands])
    sems, srcs, landed, token = outs[:2 * ng], outs[2 * ng:2 * ng + n], outs[2 * ng + n:2 * ng + 2 * n], outs[-1]
    result, k = [], 0
    for gi, grp in enumerate(groups):
        result.append((sems[2 * gi], sems[2 * gi + 1], srcs[k:k + len(grp)], landed[k:k + len(grp)]))
        k += len(grp)
    return result, token


def _copies_wait(started, modes, after, name, rels=ALL_PEERS):
    send_sems, recv_sems, srcs, lands = started
    n = len(srcs)

    def body(*refs):
        src_refs, land_refs, ssem, rsem, token = refs[:n], refs[n:2 * n], refs[2 * n], refs[2 * n + 1], refs[-1]
        x, y, c, _ = _position()
        for rel in rels:
            for i, mode in enumerate(modes):
                src = src_refs[i] if mode == "gather" else src_refs[i].at[0]
                cp = pltpu.make_async_remote_copy(src_ref=src, dst_ref=land_refs[i].at[0], send_sem=ssem.at[_sem_index(i, rel)],
                                                  recv_sem=rsem.at[_sem_index(i, rel)], device_id=(x, y, c), device_id_type=MESH_IDS)
                cp.wait_send()
                cp.wait_recv()
        token[...] = jnp.zeros_like(token)

    outs = pl.pallas_call(
        body, name=name, out_shape=[pltpu.HBM(a.shape, a.dtype) for a in (*srcs, *lands)] + [SDS((8, HEAD), F32)],
        in_specs=[HBM_SPEC] * (2 * n) + [SEM_SPEC, SEM_SPEC, pl.BlockSpec(memory_space=pl.ANY)],
        out_specs=[HBM_SPEC] * (2 * n) + [pl.BlockSpec(memory_space=pltpu.VMEM)],
        input_output_aliases={i: i for i in range(2 * n)}, compiler_params=pltpu.CompilerParams(has_side_effects=DATAFLOW),
    )(*srcs, *lands, send_sems, recv_sems, after)
    return outs[:n], outs[n:2 * n], outs[-1]


OTHER_CHIPS = ((1, 0), (0, 1), (1, 1))


def _pass_on_start(lands, name):
    n = len(lands)

    def body(*refs):
        land_refs, ssem, rsem, token = refs[:n], refs[n], refs[n + 1], refs[-1]
        x, y, c, _ = _position()
        for j, (fx, fy) in enumerate(OTHER_CHIPS):
            slot = 4 * (1 - x if fx else x) + 2 * (1 - y if fy else y) + c
            for i in range(n):
                pltpu.make_async_remote_copy(src_ref=land_refs[i].at[slot], dst_ref=land_refs[i].at[slot], send_sem=ssem.at[i * len(OTHER_CHIPS) + j],
                                             recv_sem=rsem.at[i * len(OTHER_CHIPS) + j], device_id=(x, y, 1 - c), device_id_type=MESH_IDS).start()
        token[...] = jnp.zeros_like(token)

    sems = pltpu.SemaphoreType.DMA((n * len(OTHER_CHIPS),))
    outs = pl.pallas_call(
        body, name=name, out_shape=(sems, sems, *[pltpu.HBM(l.shape, l.dtype) for l in lands], SDS(TOKEN, F32)),
        in_specs=[HBM_SPEC] * n, out_specs=(SEM_SPEC, SEM_SPEC, *[HBM_SPEC] * n, pl.BlockSpec(memory_space=pltpu.VMEM)),
        input_output_aliases={i: 2 + i for i in range(n)}, compiler_params=pltpu.CompilerParams(has_side_effects=DATAFLOW),
    )(*lands)
    return (outs[0], outs[1], outs[2:2 + n]), outs[-1]


def _pass_on_wait(started, after, name):
    send_sems, recv_sems, lands = started
    n = len(lands)

    def body(*refs):
        land_refs, ssem, rsem = refs[:n], refs[n], refs[n + 1]
        x, y, c, _ = _position()
        for j in range(len(OTHER_CHIPS)):
            for i in range(n):
                cp = pltpu.make_async_remote_copy(src_ref=land_refs[i].at[0], dst_ref=land_refs[i].at[0], send_sem=ssem.at[i * len(OTHER_CHIPS) + j],
                                                  recv_sem=rsem.at[i * len(OTHER_CHIPS) + j], device_id=(x, y, c), device_id_type=MESH_IDS)
                cp.wait_send()
                cp.wait_recv()

    return pl.pallas_call(
        body, name=name, out_shape=[pltpu.HBM(l.shape, l.dtype) for l in lands],
        in_specs=[HBM_SPEC] * n + [SEM_SPEC, SEM_SPEC, pl.BlockSpec(memory_space=pl.ANY)], out_specs=[HBM_SPEC] * n,
        input_output_aliases={i: i for i in range(n)}, compiler_params=pltpu.CompilerParams(has_side_effects=DATAFLOW),
    )(*lands, send_sems, recv_sems, after)


WEIGHT_GROUPS = (("ffn1_w_in", "ffn1_w_out", "gdn_conv_w"), ("w_in",), ("w_branch_hgrn", "w_branch_gdn", "w_out", "ffn2_w_in", "ffn2_w_out"))
GROUP_RELS = (ONE_PER_CHIP, ONE_PER_CHIP, ALL_PEERS)


class _Links:
    def __init__(self, shards, me):
        self.me = me
        self.shards = shards
        self.weights = {}
        self.sends = []
        self.gathers = {}
        self.started = None
        self._start_gather(0, None)

    def _start_gather(self, gi, zeros):
        if gi < len(WEIGHT_GROUPS):
            items = [(self.shards[n] if zeros is None else self.shards[n] + zeros[0, 0].astype(self.shards[n].dtype), "gather")
                     for n in WEIGHT_GROUPS[gi]]
            started, self.started = _copies_start([items], "gather_start_%d" % gi, GROUP_RELS[gi])
            self.gathers[gi] = started[0]

    def weight(self, name, after):
        if name not in self.weights:
            source = {"w_in_t": "w_in", "w_in_b_t": "w_in", "w_in_ab_t": "w_in", "conv_w": "gdn_conv_w"}.get(name, name)
            gi = [i for i, grp in enumerate(WEIGHT_GROUPS) if source in grp][0]
            assert gi in self.gathers, "weight groups are asked for in order"
            srcs, lands, zero = _copies_wait(self.gathers[gi], ["gather"] * len(WEIGHT_GROUPS[gi]), after, "gather_wait_%d" % gi, GROUP_RELS[gi])
            if GROUP_RELS[gi] == ONE_PER_CHIP:
                passing, zero = _pass_on_start(lands, "gather_pass_%d" % gi)
                self._start_gather(gi + 1, zero)
                lands = _pass_on_wait(passing, self.started, "gather_passed_%d" % gi)
            else:
                self._start_gather(gi + 1, zero)
            for n, src, land in zip(WEIGHT_GROUPS[gi], srcs, lands):
                full = lax.dynamic_update_index_in_dim(land, src, self.me, 0)
                if n == "gdn_conv_w":
                    self.weights["conv_w"] = full.reshape(N_DEV, CONV_K, 4 * D_MODEL // N_DEV).transpose(1, 0, 2).reshape(CONV_K, 4 * D_MODEL)
                elif n == "w_in":
                    self.weights.update(_w_in_pieces(full.reshape(-1, D_MODEL)))
                else:
                    self.weights[n] = full.reshape(-1, D_MODEL)
        return self.weights[name]

    def send(self, grads):
        names = list(grads)
        blocks = [grads[n].reshape(N_DEV, -1, D_MODEL) for n in names]
        started, token = _copies_start([[(b, "scatter") for b in blocks]], "send_" + names[0])
        self.sends.append((names, started[0]))
        return token

    def landed(self, after):
        out = {}
        for names, started in self.sends:
            srcs, lands, _ = _copies_wait(started, ["scatter"] * len(names), after, "landed_" + names[0])
            for n, src, land in zip(names, srcs, lands):
                out[n] = lax.dynamic_update_index_in_dim(land, lax.dynamic_index_in_dim(src, self.me, 0, keepdims=False), self.me, 0)
        return out


def _adam(parts, w, m, v, name):
    n_parts, r, c = parts.shape
    tc = c if c <= 512 else (256 if r > 1024 else 512)

    def body(p_ref, w_ref, m_ref, v_ref, g_ref, d_ref, mo_ref, vo_ref):
        g = p_ref[0].astype(F32)
        for i in range(1, n_parts):
            g = g + p_ref[i].astype(F32)
        m_new = ADAM_B1 * m_ref[...] + (1.0 - ADAM_B1) * g
        v_new = ADAM_B2 * v_ref[...] + (1.0 - ADAM_B2) * (g * g)
        m_hat = m_new / (1.0 - ADAM_B1 ** ADAM_STEP)
        v_hat = v_new / (1.0 - ADAM_B2 ** ADAM_STEP)
        g_ref[...] = g
        d_ref[...] = -ADAM_LR * (m_hat / (jnp.sqrt(v_hat) + ADAM_EPS) + ADAM_WD * w_ref[...])
        mo_ref[...] = m_new
        vo_ref[...] = v_new

    spec = pl.BlockSpec((r, tc), lambda j: (0, j))
    return pl.pallas_call(
        body, grid=(c // tc,), in_specs=[pl.BlockSpec((n_parts, r, tc), lambda j: (0, 0, j)), spec, spec, spec],
        out_specs=[spec] * 4, out_shape=[SDS((r, c), F32)] * 4, name=name, compiler_params=_params(1),
    )(parts, w, m, v)


BIG = ("ffn1_w_in", "ffn1_w_out", "w_in", "w_branch_hgrn", "w_branch_gdn", "w_out", "ffn2_w_in", "ffn2_w_out")


TRANSPOSED = ("ffn1_w_in", "w_in", "ffn2_w_in")


def _shard_rows(name, shard):
    return shard.T if name in TRANSPOSED else shard


SCALAR_ROWS = 8192
N_SCALAR = 2 * GDN_V_HEADS


def _w_in_pieces(w_in_t):
    return {"w_in_t": w_in_t, "w_in_b_t": w_in_t[SCALAR_ROWS + N_SCALAR:],
            "w_in_ab_t": jnp.pad(w_in_t[SCALAR_ROWS:SCALAR_ROWS + N_SCALAR], ((0, HEAD - N_SCALAR), (0, 0)))}


def _pad_lanes(a, width=HEAD):
    return jnp.pad(a, ((0, 0), (0, width - a.shape[1])))


SMALL_ROWS = 24


def _pack_small(g, loss):
    row6 = jnp.concatenate([g["hgrn_out_norm"], g["gdn_out_norm"], g["alog"], g["dtb"], loss,
                            jnp.zeros((1, D_MODEL - 5 * HEAD), F32)], axis=1)
    return jnp.concatenate([g["ffn1_norm"], g["mix_norm"], g["lbl"], g["ffn2_norm"], g["final_norm"], row6,
                            jnp.zeros((1, D_MODEL), F32), g["conv_w"].reshape(4 * CONV_K, D_MODEL)], axis=0)


def _pack_small_state(a):
    row6 = jnp.concatenate([a["hgrn_out_norm"], a["gdn_out_norm"], _pad_lanes(a["gdn_a_log"]), _pad_lanes(a["gdn_dt_bias"]),
                            jnp.zeros((1, D_MODEL - 4 * HEAD), F32)], axis=1)
    return jnp.concatenate([a["ffn1_norm"], a["mix_norm"], a["hgrn_lb_logits"], a["ffn2_norm"], a["final_norm"].reshape(1, D_MODEL),
                            row6, jnp.zeros((1, D_MODEL), F32)], axis=0)


def _unpack_small(a):
    return {"ffn1_norm": a[0:1], "mix_norm": a[1:2], "hgrn_lb_logits": a[2:4], "ffn2_norm": a[4:5], "final_norm": a[5],
            "hgrn_out_norm": a[6:7, :HEAD], "gdn_out_norm": a[6:7, HEAD:2 * HEAD],
            "gdn_a_log": a[6:7, 2 * HEAD:2 * HEAD + GDN_V_HEADS], "gdn_dt_bias": a[6:7, 3 * HEAD:3 * HEAD + GDN_V_HEADS]}


NAMES = ("ffn1_norm", "ffn1_w_in", "ffn1_w_out", "mix_norm", "w_in", "hgrn_lb_logits", "hgrn_out_norm", "gdn_conv_w", "gdn_a_log",
         "gdn_dt_bias", "gdn_out_norm", "w_branch_hgrn", "w_branch_gdn", "w_out", "ffn2_norm", "ffn2_w_in", "ffn2_w_out", "final_norm")


def kernel(x, ffn1_norm, ffn1_w_in, ffn1_w_out, mix_norm, w_in, hgrn_lb_logits, hgrn_out_norm, gdn_conv_w, gdn_a_log, gdn_dt_bias, gdn_out_norm, w_branch_hgrn, w_branch_gdn, w_out, ffn2_norm, ffn2_w_in, ffn2_w_out, final_norm, loss_target, m_ffn1_norm, m_ffn1_w_in, m_ffn1_w_out, m_mix_norm, m_w_in, m_hgrn_lb_logits, m_hgrn_out_norm, m_gdn_conv_w, m_gdn_a_log, m_gdn_dt_bias, m_gdn_out_norm, m_w_branch_hgrn, m_w_branch_gdn, m_w_out, m_ffn2_norm, m_ffn2_w_in, m_ffn2_w_out, m_final_norm, v_ffn1_norm, v_ffn1_w_in, v_ffn1_w_out, v_mix_norm, v_w_in, v_hgrn_lb_logits, v_hgrn_out_norm, v_gdn_conv_w, v_gdn_a_log, v_gdn_dt_bias, v_gdn_out_norm, v_w_branch_hgrn, v_w_branch_gdn, v_w_out, v_ffn2_norm, v_ffn2_w_in, v_ffn2_w_out, v_final_norm):
    wts = dict(zip(NAMES, (ffn1_norm, ffn1_w_in, ffn1_w_out, mix_norm, w_in, hgrn_lb_logits, hgrn_out_norm, gdn_conv_w, gdn_a_log,
                           gdn_dt_bias, gdn_out_norm, w_branch_hgrn, w_branch_gdn, w_out, ffn2_norm, ffn2_w_in, ffn2_w_out, final_norm)))
    mom = dict(zip(NAMES, (m_ffn1_norm, m_ffn1_w_in, m_ffn1_w_out, m_mix_norm, m_w_in, m_hgrn_lb_logits, m_hgrn_out_norm, m_gdn_conv_w,
                           m_gdn_a_log, m_gdn_dt_bias, m_gdn_out_norm, m_w_branch_hgrn, m_w_branch_gdn, m_w_out, m_ffn2_norm, m_ffn2_w_in,
                           m_ffn2_w_out, m_final_norm)))
    var = dict(zip(NAMES, (v_ffn1_norm, v_ffn1_w_in, v_ffn1_w_out, v_mix_norm, v_w_in, v_hgrn_lb_logits, v_hgrn_out_norm, v_gdn_conv_w,
                           v_gdn_a_log, v_gdn_dt_bias, v_gdn_out_norm, v_w_branch_hgrn, v_w_branch_gdn, v_w_out, v_ffn2_norm, v_ffn2_w_in,
                           v_ffn2_w_out, v_final_norm)))
    me = 4 * lax.axis_index("x") + 2 * lax.axis_index("y") + lax.axis_index("c")

    conv_shard = wts["gdn_conv_w"][0]
    shards = {n: _shard_rows(n, wts[n][0]).astype(BF16) for n in BIG}
    shards["gdn_conv_w"] = conv_shard.reshape(2, D_MODEL)
    links = _Links(shards, me)
    p = {"ffn1_norm": wts["ffn1_norm"], "mix_norm": wts["mix_norm"], "ffn2_norm": wts["ffn2_norm"], "final_norm": wts["final_norm"].reshape(1, D_MODEL),
         "lbl": wts["hgrn_lb_logits"], "hgrn_out_norm": wts["hgrn_out_norm"], "gdn_out_norm": wts["gdn_out_norm"],
         "alog": _pad_lanes(wts["gdn_a_log"]), "dtb": _pad_lanes(wts["gdn_dt_bias"])}

    loss, dx, g = _local_step(x[0], loss_target[0], p, links)

    small_started, small_token = _copies_start([[(_pack_small(g, loss), "gather")]], "small_start")
    landed = links.landed(small_token)

    big = [{} for _ in range(4)]
    for n in BIG:
        res = _adam(landed[n], _shard_rows(n, wts[n][0]), _shard_rows(n, mom[n][0]), _shard_rows(n, var[n][0]), "adam_" + n)
        for kind in range(4):
            big[kind][n] = _shard_rows(n, res[kind])
    small_srcs, small_lands, _ = _copies_wait(small_started[0], ["gather"], res[0], "small_wait")
    small_parts = lax.dynamic_update_index_in_dim(small_lands[0], small_srcs[0], me, 0)
    n_vec = SMALL_ROWS - 4 * CONV_K
    small_raw = _adam(small_parts[:, :n_vec], _pack_small_state(wts), _pack_small_state(mom), _pack_small_state(var), "adam_small")
    small = [_unpack_small(o) for o in small_raw]
    loss_total = small_raw[0][6, 4 * HEAD]
    conv_parts = small_parts[:, n_vec:].reshape(N_DEV, CONV_K, 4 * D_MODEL)
    width = 4 * D_MODEL // N_DEV
    conv_mine = lax.dynamic_slice_in_dim(conv_parts, me * width, width, axis=2)
    conv = _adam(conv_mine, conv_shard, mom["gdn_conv_w"][0], var["gdn_conv_w"][0], "adam_conv")

    outs = []
    for kind in range(4):
        for n in NAMES:
            if n in BIG:
                outs.append(big[kind][n][None])
            elif n == "gdn_conv_w":
                outs.append(conv[kind][None])
            else:
                outs.append(small[kind][n])
    return (loss_total, dx[None], *outs)
```

```python
import functools

import jax
import jax.numpy as jnp
from jax import lax
from jax.experimental import pallas as pl
from jax.experimental.pallas import tpu as pltpu

F32 = jnp.float32
BF16 = jnp.bfloat16
HIGHEST = lax.Precision.HIGHEST
MESH_IDS = pl.DeviceIdType.MESH

D_MODEL = 1024
D_FF = 2816
N_DEV = 8
EPS = 1e-6
HEAD = 128
HG_HEADS = 8
GDN_QK_HEADS = 8
GDN_V_HEADS = 16
GDN_CHUNK = 64
HG_CHUNK = 16
CONV_K = 4
IN_WIDTH = 12320
COL_HQ, COL_HF, COL_HI, COL_HG, COL_GQ, COL_GK, COL_GV = 0, 8, 16, 24, 32, 40, 48
COL_GZ, COL_GATE_H, COL_GATE_G = 0, 16, 24
VMEM_LIMIT = 56 * 1024 * 1024

ADAM_LR, ADAM_B1, ADAM_B2, ADAM_EPS, ADAM_WD, ADAM_STEP = 0.001, 0.9, 0.999, 1e-08, 0.01, 10

SDS = jax.ShapeDtypeStruct


def _params(n_axes):
    return pltpu.CompilerParams(dimension_semantics=("arbitrary",) * n_axes, vmem_limit_bytes=VMEM_LIMIT)


def _tile(n, candidates=(512, 384, 256, 128, 64, 32, 16, 8)):
    for c in candidates:
        if n % c == 0:
            return c
    return n


_DIMS = {"nn": ((1,), (0,)), "nt": ((1,), (1,)), "tn": ((0,), (0,))}


def _bdot_raw(a, b, dims):
    return lax.dot_general(a.astype(BF16), b.astype(BF16), (_DIMS[dims], ((), ())), preferred_element_type=F32)


@functools.partial(jax.custom_vjp, nondiff_argnums=(2,))
def _bdot(a, b, dims):
    return _bdot_raw(a, b, dims)


def _bdot_fwd(a, b, dims):
    return _bdot_raw(a, b, dims), (a, b)


def _bdot_bwd(dims, res, ct):
    a, b = res
    if dims == "nn":
        return _bdot_raw(ct, b, "nt"), _bdot_raw(a, ct, "tn")
    if dims == "nt":
        return _bdot_raw(ct, b, "nn"), _bdot_raw(ct, a, "tn")
    return _bdot_raw(b, ct, "nt"), _bdot_raw(a, ct, "nn")


_bdot.defvjp(_bdot_fwd, _bdot_bwd)


def _hdot_raw(a, b):
    return jnp.dot(a, b, precision=HIGHEST, preferred_element_type=F32)


MM_VMEM_BUDGET = 38 * 1024 * 1024
TOKEN = (8, HEAD)


def _mm_tiles(m, n, k, a_bytes, b_bytes, o_bytes, r_bytes, m_align=8):
    def need(tm, tn, tk):
        return 2 * (tm * tk * a_bytes + tk * tn * b_bytes + tm * tn * (o_bytes + r_bytes)) + (tm * tn * 4 if tk < k else 0)

    def shrink(tm, tn, tk, floor_m, floor_n):
        while need(tm, tn, tk) > MM_VMEM_BUDGET:
            if tn > floor_n and tn % 256 == 0 and tn >= tm:
                tn //= 2
            elif tm > floor_m and tm % (2 * m_align) == 0:
                tm //= 2
            elif tn > floor_n and tn % 256 == 0:
                tn //= 2
            else:
                return None
        return tm, tn, tk

    tm = _tile(m, (1408, 1024, 704, 512, 256, 128, 64, 32, 16, 8))
    tn = _tile(n, (1408, 1024, 512, 256, 128))
    whole = shrink(tm, tn, k, min(tm, 1024), min(tn, 512))
    if whole is not None:
        return whole
    tk = _tile(k, (2048, 1408, 1024, 512, 256, 128, 64, 32, 16, 8))
    while True:
        fit = shrink(tm, tn, tk, min(tm, 256), min(tn, 512))
        if fit is not None or tk <= 512 or tk % 256:
            return fit if fit is not None else (tm, tn, tk)
        tk //= 2


def _mm(a, b, dims, out_dtype, name, res=None, alpha=1.0, after=None, b_rows=None):
    b_shape = b.shape if b_rows is None else (b_rows, b.shape[1])
    if dims == "nn":
        (m, k), (k2, n) = a.shape, b_shape
    elif dims == "nt":
        (m, k), (n, k2) = a.shape, b_shape
    else:
        (k, m), (k2, n) = a.shape, b_shape
    assert k == k2, (a.shape, b.shape, dims)
    has_res = res is not None
    tm, tn, tk = _mm_tiles(m, n, k, a.dtype.itemsize, b.dtype.itemsize, jnp.dtype(out_dtype).itemsize, res.dtype.itemsize if has_res else 0,
                           m_align=HEAD if dims == "tn" else 8)
    nk = k // tk
    a_spec = pl.BlockSpec((tk, tm), lambda i, j, kk: (kk, i)) if dims == "tn" else pl.BlockSpec((tm, tk), lambda i, j, kk: (i, kk))
    b_spec = pl.BlockSpec((tn, tk), lambda i, j, kk: (j, kk)) if dims == "nt" else pl.BlockSpec((tk, tn), lambda i, j, kk: (kk, j))
    o_spec = pl.BlockSpec((tm, tn), lambda i, j, kk: (i, j))

    def finish(acc, r_ref, o_ref):
        out = acc * alpha if alpha != 1.0 else acc
        if has_res:
            out = r_ref[...].astype(F32) + out
        o_ref[...] = out.astype(o_ref.dtype)

    n_in = 2 + has_res + (after is not None)

    def body(*refs):
        a_ref, b_ref = refs[:2]
        r_ref = refs[2] if has_res else None
        o_ref = refs[n_in]
        p = _bdot_raw(a_ref[...], b_ref[...], dims)
        if nk == 1:
            finish(p, r_ref, o_ref)
            return
        acc_ref = refs[-1]
        kk = pl.program_id(2)

        @pl.when(kk == 0)
        def _():
            acc_ref[...] = p

        @pl.when(kk > 0)
        def _():
            acc_ref[...] += p

        @pl.when(kk == nk - 1)
        def _():
            finish(acc_ref[...], r_ref, o_ref)

    args = (a, b) + ((res,) if has_res else ()) + ((after,) if after is not None else ())
    in_specs = [a_spec, b_spec] + ([o_spec] if has_res else []) + ([pl.BlockSpec(TOKEN, lambda i, j, kk: (0, 0))] if after is not None else [])
    return pl.pallas_call(
        body, grid=(m // tm, n // tn, nk), in_specs=in_specs, out_specs=o_spec, out_shape=SDS((m, n), out_dtype),
        scratch_shapes=[pltpu.VMEM((tm, tn), F32)] if nk > 1 else [], name=name, compiler_params=_params(3),
    )(*args)


def _tmap(fn, grid, ins, outs, name):
    n_in = len(ins)
    n_ax = len(grid)

    def body(*refs):
        vals = fn(*[r[...] for r in refs[:n_in]])
        if not isinstance(vals, (tuple, list)):
            vals = (vals,)
        first_inner = pl.program_id(n_ax - 1) == 0
        first_all = first_inner
        for ax in range(n_ax - 1):
            first_all = jnp.logical_and(first_all, pl.program_id(ax) == 0)

        def put(ref, val, acc):
            val = val.astype(ref.dtype)
            if acc is None:
                ref[...] = val
                return
            first = first_inner if acc == "inner" else first_all

            @pl.when(first)
            def _():
                ref[...] = val

            @pl.when(jnp.logical_not(first))
            def _():
                ref[...] += val

        for ref, val, o in zip(refs[n_in:], vals, outs):
            put(ref, val, o[4])

    return pl.pallas_call(
        body, grid=grid,
        in_specs=[pl.BlockSpec(bs, im) for _, bs, im in ins],
        out_specs=[pl.BlockSpec(o[2], o[3]) for o in outs],
        out_shape=[SDS(o[0], o[1]) for o in outs],
        name=name, compiler_params=_params(n_ax),
    )(*[a for a, _, _ in ins])


def _rows(width, tt, off=0):
    return (tt, width), (lambda j, i: (i, off + j))


def _rms(x, g):
    x = x.astype(F32)
    return x * lax.rsqrt(jnp.mean(x * x, axis=-1, keepdims=True) + EPS) * g


def _silu(x):
    return x * jax.nn.sigmoid(x)


def _softplus(x):
    return jnp.maximum(x, 0.0) + jnp.log1p(jnp.exp(-jnp.abs(x)))


def _rms_fwd(x, g, name):
    t, d = x.shape
    tt = _tile(t, (256, 128))
    return _tmap(_rms, (1, t // tt), [(x, *_rows(d, tt)), (g, (1, d), lambda j, i: (0, 0))],
                 [((t, d), BF16, *_rows(d, tt), None)], name)[0]


def _rms_bwd(x, g, dn, dres, name):
    t, d = x.shape
    tt = _tile(t, (256, 128))

    def fn(x, g, dn, dres):
        _, vjp = jax.vjp(_rms, x, g)
        dx, dg = vjp(dn.astype(F32))
        return dres + dx, dg

    return _tmap(fn, (1, t // tt),
                 [(x, *_rows(d, tt)), (g, (1, d), lambda j, i: (0, 0)), (dn, *_rows(d, tt)), (dres, *_rows(d, tt))],
                 [((t, d), F32, *_rows(d, tt), None), ((1, d), F32, (1, d), lambda j, i: (0, 0), "inner")], name)


def _swiglu(ab):
    return _silu(ab[:, :D_FF].astype(F32)) * ab[:, D_FF:].astype(F32)


def _swiglu_fwd(ab, name):
    t = ab.shape[0]
    tt = _tile(t, (128,))
    return _tmap(_swiglu, (1, t // tt), [(ab, *_rows(2 * D_FF, tt))], [((t, D_FF), BF16, *_rows(D_FF, tt), None)], name)[0]


def _swiglu_bwd(ab, ds, name):
    t = ab.shape[0]
    tt = _tile(t, (128,))

    def fn(ab, ds):
        a, b = ab[:, :D_FF].astype(F32), ab[:, D_FF:].astype(F32)
        _, vjp = jax.vjp(lambda a, b: _silu(a) * b, a, b)
        da, db = vjp(ds.astype(F32))
        return jnp.concatenate([da, db], axis=1)

    return _tmap(fn, (1, t // tt), [(ab, *_rows(2 * D_FF, tt)), (ds, *_rows(D_FF, tt))],
                 [((t, 2 * D_FF), BF16, *_rows(2 * D_FF, tt), None)], name)[0]


def _ffn_fwd(h, g, w_in_t, w_out, tag, after):
    n = _rms_fwd(h, g, tag + "_norm")
    ab = _mm(n, w_in_t, "nt", BF16, tag + "_in", after=after)
    s = _swiglu_fwd(ab, tag + "_act")
    out = _mm(s, w_out, "nn", F32, tag + "_out", res=h, alpha=0.5)
    return out, (n, ab, s)


def _ffn_bwd(h, g, w_in_t, w_out, saved, dout, tag, links):
    n, ab, s = saved
    sent = links.send({tag + "_w_out": _mm(s, dout, "tn", BF16, tag + "_dw_out", alpha=0.5)})
    ds = _mm(dout, w_out, "nt", BF16, tag + "_ds", alpha=0.5, after=sent)
    dab = _swiglu_bwd(ab, ds, tag + "_dact")
    sent = links.send({tag + "_w_in": _mm(dab, n, "tn", BF16, tag + "_dw_in")})
    dn = _mm(dab, w_in_t, "nn", F32, tag + "_dn", after=sent)
    return _rms_bwd(h, g, dn, dout, tag + "_dnorm")


def _chunk_sum_matrix(n, chunk, transpose=False):
    row = lax.broadcasted_iota(jnp.int32, (n, n), 0)
    col = lax.broadcasted_iota(jnp.int32, (n, n), 1)
    if transpose:
        row, col = col, row
    return jnp.where(jnp.logical_and(col <= row, row // chunk == col // chunk), 1.0, 0.0).astype(F32)


def _hgrn_gates(hq, hf, lbl):
    lb = jax.nn.sigmoid(lbl[0:1, :] - lbl[1:2, :])
    sg = jax.nn.sigmoid(hf)
    f = lb + (1.0 - lb) * sg
    q = _silu(hq) * HEAD ** -0.5
    k = (1.0 - lb) * (1.0 - sg)
    return q, k, jnp.log(f)


def _hgrn_prep_fwd(proj, lbl):
    t = proj.shape[0]
    tt, ft = _tile(t, (256, 128)), 512

    def fn(hq, hf, lbl):
        q, k, log_f = _hgrn_gates(hq, hf, lbl)
        return q, k, _hdot_raw(_chunk_sum_matrix(tt, HG_CHUNK), log_f)

    o = ((t, D_MODEL), F32, *_rows(ft, tt), None)
    return _tmap(fn, (D_MODEL // ft, t // tt),
                 [(proj, *_rows(ft, tt, COL_HQ * HEAD // ft)), (proj, *_rows(ft, tt, COL_HF * HEAD // ft)), (lbl, (2, ft), lambda j, i: (0, j))],
                 [o, o, o], "hgrn_prep")


def _hgrn_prep_bwd(proj, lbl, dq, dk, db):
    t = proj.shape[0]
    tt, ft = _tile(t, (256, 128)), 512

    def fn(hq, hf, lbl, dq, dk, db):
        dlog_f = _hdot_raw(_chunk_sum_matrix(tt, HG_CHUNK, transpose=True), db)
        _, vjp = jax.vjp(_hgrn_gates, hq, hf, lbl)
        return vjp((dq, dk, dlog_f))

    o = ((t, D_MODEL), BF16, *_rows(ft, tt), None)
    r = _rows(ft, tt)
    return _tmap(fn, (D_MODEL // ft, t // tt),
                 [(proj, *_rows(ft, tt, COL_HQ * HEAD // ft)), (proj, *_rows(ft, tt, COL_HF * HEAD // ft)), (lbl, (2, ft), lambda j, i: (0, j)),
                  (dq, *r), (dk, *r), (db, *r)],
                 [o, o, ((2, D_MODEL), F32, (2, ft), lambda j, i: (0, j), "inner")], "hgrn_prep_bwd")


def _hgrn_chunks(q, k, v, b, st):
    n = q[0].shape[0]
    half = n // 2
    srow = lax.broadcasted_iota(jnp.int32, (half, HEAD), 0)
    inter = _each(lambda q, b, st: _bdot(q * jnp.exp(b), st, "nt"), q, b, st)

    def below_scores(q, k, b):
        ref = b[half:half + 1, :]
        return _bdot(q[half:] * jnp.exp(jnp.minimum(b[half:] - ref, 0.0)), k[:half] * jnp.exp(jnp.minimum(ref - b[:half], 0.0)), "nt")

    below = _each(lambda a, v: _bdot(a, v[:half], "nn"), _each(below_scores, q, k, b), v)

    def diagonal(q, k, v, b):
        rows = []
        for lo in (0, half):
            qb, kb, vb, bb = (a[lo:lo + half] for a in (q, k, v, b))
            for t in range(half):
                e = jnp.where(srow <= t, jnp.exp(jnp.minimum(bb[t:t + 1, :] - bb, 0.0)), 0.0)
                a = jnp.sum(qb[t:t + 1, :] * kb * e, axis=1, keepdims=True)
                rows.append(jnp.sum(a * vb, axis=0, keepdims=True))
        return jnp.concatenate(rows, axis=0)

    diag = _each(diagonal, q, k, v, b)
    o = _each(lambda inter, diag, below: inter + diag + jnp.concatenate([jnp.zeros_like(below), below], axis=0), inter, diag, below)

    def new_state(k, v, b, st):
        bend = b[n - 1:n, :]
        return st * jnp.exp(bend) + _bdot(v, k * jnp.exp(bend - b), "tn")

    return o, _each(new_state, k, v, b, st)


HG_GROUP = 4
HG_PER = GDN_CHUNK // HG_CHUNK


def _hgrn_rec_fwd(q, k, proj, b):
    t = q.shape[0]
    nc = t // GDN_CHUNK
    blk = (GDN_CHUNK, HG_GROUP * HEAD)
    im = lambda h, c: (c, h)

    def body(q_ref, k_ref, v_ref, b_ref, o_ref, hs_ref, st_ref):
        @pl.when(pl.program_id(1) == 0)
        def _():
            st_ref[...] = jnp.zeros_like(st_ref)

        heads = range(HG_GROUP)
        for j in range(HG_PER):
            sl = pl.ds(HG_CHUNK * j, HG_CHUNK)
            st = tuple(st_ref[g] for g in heads)
            o, st_new = _hgrn_chunks(*[tuple(r[sl, _head_lanes(g)] for g in heads) for r in (q_ref, k_ref, v_ref, b_ref)], st)
            for g in heads:
                hs_ref[g, j] = st[g]
                o_ref[sl, _head_lanes(g)] = o[g]
                st_ref[g] = st_new[g]

    return pl.pallas_call(
        body, grid=(HG_HEADS // HG_GROUP, nc),
        in_specs=[pl.BlockSpec(blk, im), pl.BlockSpec(blk, im), pl.BlockSpec(blk, lambda h, c: (c, COL_HI // HG_GROUP + h)), pl.BlockSpec(blk, im)],
        out_specs=[pl.BlockSpec(blk, im), pl.BlockSpec((HG_GROUP, HG_PER, HEAD, HEAD), lambda h, c: (h, c, 0, 0))],
        out_shape=[SDS((t, D_MODEL), F32), SDS((HG_HEADS, nc * HG_PER, HEAD, HEAD), F32)],
        scratch_shapes=[pltpu.VMEM((HG_GROUP, HEAD, HEAD), F32)], name="hgrn_rec", compiler_params=_params(2),
    )(q, k, proj, b)


def _hgrn_rec_bwd(q, k, proj, b, hs, do):
    t = q.shape[0]
    nc = t // GDN_CHUNK
    blk = (GDN_CHUNK, HG_GROUP * HEAD)
    im = lambda h, c: (nc - 1 - c, h)

    def body(q_ref, k_ref, v_ref, b_ref, hs_ref, do_ref, dq_ref, dk_ref, dv_ref, db_ref, dst_ref):
        @pl.when(pl.program_id(1) == 0)
        def _():
            dst_ref[...] = jnp.zeros_like(dst_ref)

        heads = range(HG_GROUP)
        for j in reversed(range(HG_PER)):
            sl = pl.ds(HG_CHUNK * j, HG_CHUNK)
            _, vjp = jax.vjp(_hgrn_chunks, *[tuple(r[sl, _head_lanes(g)] for g in heads) for r in (q_ref, k_ref, v_ref, b_ref)],
                             tuple(hs_ref[g, j] for g in heads))
            dq, dk, dv, db, dst = vjp((tuple(do_ref[sl, _head_lanes(g)] for g in heads), tuple(dst_ref[g] for g in heads)))
            for g in heads:
                ln = _head_lanes(g)
                dq_ref[sl, ln] = dq[g]
                dk_ref[sl, ln] = dk[g]
                dv_ref[sl, ln] = dv[g].astype(dv_ref.dtype)
                db_ref[sl, ln] = db[g]
                dst_ref[g] = dst[g]

    spec = pl.BlockSpec(blk, im)
    return pl.pallas_call(
        body, grid=(HG_HEADS // HG_GROUP, nc),
        in_specs=[spec, spec, pl.BlockSpec(blk, lambda h, c: (nc - 1 - c, COL_HI // HG_GROUP + h)), spec,
                  pl.BlockSpec((HG_GROUP, HG_PER, HEAD, HEAD), lambda h, c: (h, nc - 1 - c, 0, 0)), spec],
        out_specs=[spec, spec, spec, spec],
        out_shape=[SDS((t, D_MODEL), F32), SDS((t, D_MODEL), F32), SDS((t, D_MODEL), BF16), SDS((t, D_MODEL), F32)],
        scratch_shapes=[pltpu.VMEM((HG_GROUP, HEAD, HEAD), F32)], name="hgrn_rec_bwd", compiler_params=_params(2),
    )(q, k, proj, b, hs, do)


def _shift_down(x, d):
    if d == 0:
        return x
    row = lax.broadcasted_iota(jnp.int32, x.shape, 0)
    return jnp.where(row >= d, pltpu.roll(x, d, 0), 0.0)


def _shift_up(x, d):
    if d == 0:
        return x
    n = x.shape[0]
    row = lax.broadcasted_iota(jnp.int32, x.shape, 0)
    return jnp.where(row < n - d, pltpu.roll(x, n - d, 0), 0.0)


def _conv_fwd(proj, conv_w):
    t = proj.shape[0]
    width = 2 * D_MODEL + 2 * D_MODEL

    def body(x_ref, w_ref, c_ref):
        x, w = x_ref[...], w_ref[...]
        y = w[CONV_K - 1:CONV_K, :] * x
        for j in range(CONV_K - 1):
            y = y + w[j:j + 1, :] * _shift_down(x, CONV_K - 1 - j)
        c_ref[...] = _silu(y)

    return pl.pallas_call(
        body, grid=(width // HEAD,),
        in_specs=[pl.BlockSpec((t, HEAD), lambda j: (0, COL_GQ + j)), pl.BlockSpec((CONV_K, HEAD), lambda j: (0, j))],
        out_specs=pl.BlockSpec((t, HEAD), lambda j: (0, j)), out_shape=SDS((t, width), F32),
        name="gdn_conv", compiler_params=_params(1),
    )(proj, conv_w)


def _conv_bwd(proj, conv_w, dc_qk, dc_v):
    t = proj.shape[0]
    n_qk = dc_qk.shape[1] // HEAD
    width = dc_qk.shape[1] + dc_v.shape[1]

    def body(x_ref, w_ref, dqk_ref, dv_ref, dx_ref, dw_ref):
        x, w = x_ref[...], w_ref[...]
        xs = [_shift_down(x, CONV_K - 1 - j) for j in range(CONV_K)]
        y = w[0:1, :] * xs[0]
        for j in range(1, CONV_K):
            y = y + w[j:j + 1, :] * xs[j]
        sg = jax.nn.sigmoid(y)
        dc = jnp.where(pl.program_id(0) < n_qk, dqk_ref[...], dv_ref[...])
        dy = dc * (sg * (1.0 + y * (1.0 - sg)))
        dx = w[CONV_K - 1:CONV_K, :] * dy
        for j in range(CONV_K - 1):
            dx = dx + w[j:j + 1, :] * _shift_up(dy, CONV_K - 1 - j)
        dx_ref[...] = dx.astype(dx_ref.dtype)
        dw_ref[...] = jnp.concatenate([jnp.sum(xs[j] * dy, axis=0, keepdims=True) for j in range(CONV_K)], axis=0)

    return pl.pallas_call(
        body, grid=(width // HEAD,),
        in_specs=[pl.BlockSpec((t, HEAD), lambda j: (0, COL_GQ + j)), pl.BlockSpec((CONV_K, HEAD), lambda j: (0, j)),
                  pl.BlockSpec((t, HEAD), lambda j: (0, jnp.minimum(j, n_qk - 1))), pl.BlockSpec((t, HEAD), lambda j: (0, jnp.maximum(j - n_qk, 0)))],
        out_specs=[pl.BlockSpec((t, HEAD), lambda j: (0, j)), pl.BlockSpec((CONV_K, HEAD), lambda j: (0, j))],
        out_shape=[SDS((t, width), BF16), SDS((CONV_K, width), F32)],
        name="gdn_conv_bwd", compiler_params=_params(1),
    )(proj, conv_w, dc_qk, dc_v)


def _l2norm(x, scale):
    return x * lax.rsqrt(jnp.sum(x * x, axis=-1, keepdims=True) + EPS) * scale


def _head(a, h):
    return a[:, h * HEAD:(h + 1) * HEAD]


def _qk_scale(h):
    return HEAD ** -0.5 if h < GDN_QK_HEADS else 1.0


def _qk_norm_fwd(c):
    t = c.shape[0]
    tt = _tile(t, (256, 128))
    width = 2 * D_MODEL

    def fn(x):
        return jnp.concatenate([_l2norm(_head(x, h), _qk_scale(h)) for h in range(2 * GDN_QK_HEADS)], axis=1)

    return _tmap(fn, (1, t // tt), [(c, *_rows(width, tt))], [((t, width), F32, *_rows(width, tt), None)], "gdn_qk_norm")[0]


def _qk_norm_bwd(c, dq_rep, dk_rep):
    t = c.shape[0]
    tt = _tile(t, (256, 128))
    width = 2 * D_MODEL

    def fn(x, dq2, dk2):
        out = []
        for h in range(2 * GDN_QK_HEADS):
            d2, hh = (dq2, h) if h < GDN_QK_HEADS else (dk2, h - GDN_QK_HEADS)
            _, vjp = jax.vjp(lambda x: _l2norm(x, _qk_scale(h)), _head(x, h))
            out.append(vjp(_head(d2, 2 * hh) + _head(d2, 2 * hh + 1))[0])
        return jnp.concatenate(out, axis=1)

    r = _rows(width, tt)
    return _tmap(fn, (1, t // tt), [(c, *r), (dq_rep, *r), (dk_rep, *r)], [((t, width), F32, *r, None)], "gdn_qk_norm_bwd")[0]


def _gdn_gates(x, alog, dtb):
    return -jnp.exp(alog) * _softplus(x + dtb), jax.nn.sigmoid(x)


def _gates_fwd(pab, alog, dtb):
    t = pab.shape[0]
    tt = _tile(t, (256, 128))

    def fn(x, alog, dtb):
        g, beta = _gdn_gates(x, alog, dtb)
        lane = lax.broadcasted_iota(jnp.int32, g.shape, 1)
        return jnp.where(lane < GDN_V_HEADS, _hdot_raw(_chunk_sum_matrix(tt, GDN_CHUNK), g), beta).T

    p = (alog, (1, HEAD), lambda j, i: (0, 0)), (dtb, (1, HEAD), lambda j, i: (0, 0))
    return _tmap(fn, (1, t // tt), [(pab, *_rows(HEAD, tt)), *p], [((HEAD, t), F32, (HEAD, tt), lambda j, i: (0, i), None)], "gdn_gates")[0]


def _gates_bwd(pab, alog, dtb, dout_t):
    t = pab.shape[0]
    tt = _tile(t, (256, 128))

    def fn(x, alog, dtb, dout_t):
        dout = dout_t.T
        lane = lax.broadcasted_iota(jnp.int32, dout.shape, 1)
        dgam = jnp.where(lane < GDN_V_HEADS, dout, 0.0)
        dbeta = jnp.where(jnp.logical_and(lane >= GDN_V_HEADS, lane < 2 * GDN_V_HEADS), dout, 0.0)
        dg = _hdot_raw(_chunk_sum_matrix(tt, GDN_CHUNK, transpose=True), dgam)
        _, vjp = jax.vjp(_gdn_gates, x, alog, dtb)
        return vjp((dg, dbeta))

    p = (alog, (1, HEAD), lambda j, i: (0, 0)), (dtb, (1, HEAD), lambda j, i: (0, 0))
    acc = ((1, HEAD), F32, (1, HEAD), lambda j, i: (0, 0), "inner")
    return _tmap(fn, (1, t // tt), [(pab, *_rows(HEAD, tt)), *p, (dout_t, (HEAD, tt), lambda j, i: (0, i))],
                 [((t, HEAD), BF16, *_rows(HEAD, tt), None), acc, acc], "gdn_gates_bwd")


def _split_bf16(x):
    hi = x.astype(BF16)
    return hi, (x - hi.astype(F32)).astype(BF16)


def _dot3(a, b):
    (ah, al), (bh, bl) = a, b
    return _bdot_raw(ah, bh, "nn") + (_bdot_raw(ah, bl, "nn") + _bdot_raw(al, bh, "nn"))


def _each(fn, *lists):
    return tuple(fn(*xs) for xs in zip(*lists))


def _unit_lower_inverses_raw(a):
    n = a[0].shape[0]
    row = lax.broadcasted_iota(jnp.int32, (n, n), 0)
    col = lax.broadcasted_iota(jnp.int32, (n, n), 1)
    eye = jnp.where(row == col, 1.0, 0.0).astype(F32)
    p = _each(lambda a: eye - a, a)
    x = _each(_split_bf16, a)
    m = 2
    while m < 2 * n:
        x = _each(_split_bf16, _each(_dot3, x, x))
        p = _each(lambda p, x: p + _dot3(_split_bf16(p), x), p, x)
        m *= 2
    return p


@jax.custom_vjp
def _unit_lower_inverses(a, known):
    return _unit_lower_inverses_raw(a) if known is None else known


def _uli_fwd(a, known):
    inv = _unit_lower_inverses(a, known)
    return inv, (inv, known)


def _uli_bwd(res, ct):
    inv, known = res
    right = _each(lambda ct, inv: _bdot_raw(ct, inv, "nt"), ct, inv)
    da = _each(lambda inv, r: -_bdot_raw(inv, r, "tn"), inv, right)
    return da, (None if known is None else _each(jnp.zeros_like, known))


_unit_lower_inverses.defvjp(_uli_fwd, _uli_bwd)


def _gdn_chunks(q, k, v, beta_rows, gam_rows, s, inv_known=None):
    n = q[0].shape[0]
    heads = range(len(q))
    row = lax.broadcasted_iota(jnp.int32, (n, n), 0)
    col = lax.broadcasted_iota(jnp.int32, (n, n), 1)
    beta_cols, gam_cols = beta_rows.T, gam_rows.T
    beta = tuple(beta_cols[:, g:g + 1] for g in heads)
    gam = tuple(gam_cols[:, g:g + 1] for g in heads)
    gam_row = tuple(gam_rows[g:g + 1, :] for g in heads)
    decay = _each(lambda gam, gam_row: jnp.where(row >= col, jnp.exp(jnp.minimum(gam - gam_row, 0.0)), 0.0), gam, gam_row)
    kb = _each(lambda k, beta: k * beta, k, beta)
    a = _each(lambda kb, k, decay: jnp.where(row > col, _bdot(kb, k, "nt") * decay, 0.0), kb, k, decay)
    inv = _unit_lower_inverses(a, inv_known)
    eg = _each(jnp.exp, gam)
    u = _each(lambda inv, v, beta: _bdot(inv, v * beta, "nn"), inv, v, beta)
    w = _each(lambda inv, kb, eg: _bdot(inv, kb * eg, "nn"), inv, kb, eg)
    qk = _each(lambda q, k, decay: _bdot(q, k, "nt") * decay, q, k, decay)
    v_new = _each(lambda u, w, s: u - _bdot(w, s, "nn"), u, w, s)
    o_state = _each(lambda q, eg, s: _bdot(q * eg, s, "nn"), q, eg, s)
    o = _each(lambda o_state, qk, v_new: o_state + _bdot(qk, v_new, "nn"), o_state, qk, v_new)
    gend = _each(lambda gam: gam[n - 1:n, :], gam)
    s_new = _each(lambda s, k, gam, gend, v_new: s * jnp.exp(gend) + _bdot(k * jnp.exp(gend - gam), v_new, "tn"), s, k, gam, gend, v_new)
    return o, s_new, inv


GDN_GROUP = 16


def _gdn_specs(nc, rev):
    cc = (lambda c: nc - 1 - c) if rev else (lambda c: c)
    grp = GDN_GROUP
    q = pl.BlockSpec((GDN_CHUNK, grp // 2 * HEAD), lambda h, c: (cc(c), h))
    k = pl.BlockSpec((GDN_CHUNK, grp // 2 * HEAD), lambda h, c: (cc(c), 2 * GDN_QK_HEADS // grp + h))
    v = pl.BlockSpec((GDN_CHUNK, grp * HEAD), lambda h, c: (cc(c), 2 * GDN_QK_HEADS // grp + h))
    o = pl.BlockSpec((GDN_CHUNK, grp * HEAD), lambda h, c: (cc(c), h))
    rw = pl.BlockSpec((grp, None, 1, GDN_CHUNK), lambda h, c: (h, cc(c), 0, 0))
    st = pl.BlockSpec((grp, None, HEAD, HEAD), lambda h, c: (h, cc(c), 0, 0))
    inv = pl.BlockSpec((grp, None, GDN_CHUNK, GDN_CHUNK), lambda h, c: (h, cc(c), 0, 0))
    return q, k, v, o, rw, st, inv


def _head_lanes(g, per=1):
    return pl.ds((g // per) * HEAD, HEAD)


def _gdn_rec_fwd(qk, c, beta_row, gam_row):
    t = qk.shape[0]
    nc = t // GDN_CHUNK
    q, k, v, o, rw, st, inv = _gdn_specs(nc, False)

    def body(q_ref, k_ref, v_ref, be_ref, gr_ref, o_ref, ss_ref, inv_ref, s_ref):
        @pl.when(pl.program_id(1) == 0)
        def _():
            s_ref[...] = jnp.zeros_like(s_ref)

        heads = range(GDN_GROUP)
        s = tuple(s_ref[g] for g in heads)
        out, s_new, inv_c = _gdn_chunks(
            tuple(q_ref[:, _head_lanes(g, 2)] for g in heads), tuple(k_ref[:, _head_lanes(g, 2)] for g in heads),
            tuple(v_ref[:, _head_lanes(g)] for g in heads), be_ref[:, 0, :], gr_ref[:, 0, :], s)
        for g in heads:
            ss_ref[g] = s[g]
            o_ref[:, _head_lanes(g)] = out[g]
            inv_ref[g] = inv_c[g]
            s_ref[g] = s_new[g]

    return pl.pallas_call(
        body, grid=(GDN_V_HEADS // GDN_GROUP, nc), in_specs=[q, k, v, rw, rw], out_specs=[o, st, inv],
        out_shape=[SDS((t, 2 * D_MODEL), F32), SDS((GDN_V_HEADS, nc, HEAD, HEAD), F32), SDS((GDN_V_HEADS, nc, GDN_CHUNK, GDN_CHUNK), F32)],
        scratch_shapes=[pltpu.VMEM((GDN_GROUP, HEAD, HEAD), F32)], name="gdn_rec", compiler_params=_params(2),
    )(qk, qk, c, beta_row, gam_row)


def _gdn_rec_bwd(qk, c, beta_row, gam_row, ss, invs, do):
    t = qk.shape[0]
    nc = t // GDN_CHUNK
    q, k, v, o, rw, st, inv = _gdn_specs(nc, True)

    def body(q_ref, k_ref, v_ref, be_ref, gr_ref, ss_ref, inv_ref, do_ref,
             dq_ref, dk_ref, dv_ref, dbe_ref, dgr_ref, ds_ref):
        @pl.when(pl.program_id(1) == 0)
        def _():
            ds_ref[...] = jnp.zeros_like(ds_ref)

        heads = range(GDN_GROUP)
        _, vjp = jax.vjp(
            _gdn_chunks,
            tuple(q_ref[:, _head_lanes(g, 2)] for g in heads), tuple(k_ref[:, _head_lanes(g, 2)] for g in heads),
            tuple(v_ref[:, _head_lanes(g)] for g in heads), be_ref[:, 0, :], gr_ref[:, 0, :],
            tuple(ss_ref[g] for g in heads), tuple(inv_ref[g] for g in heads))
        no_inv_ct = tuple(jnp.zeros((GDN_CHUNK, GDN_CHUNK), F32) for g in heads)
        dq, dk, dv, dbe, dgr, ds, _ = vjp((tuple(do_ref[:, _head_lanes(g)] for g in heads), tuple(ds_ref[g] for g in heads), no_inv_ct))
        for g in heads:
            dq_ref[:, _head_lanes(g)] = dq[g]
            dk_ref[:, _head_lanes(g)] = dk[g]
            dv_ref[:, _head_lanes(g)] = dv[g]
            ds_ref[g] = ds[g]
        dbe_ref[:, 0, :] = dbe
        dgr_ref[:, 0, :] = dgr

    wide = SDS((t, 2 * D_MODEL), F32)
    rowshape = SDS((GDN_V_HEADS, nc, 1, GDN_CHUNK), F32)
    return pl.pallas_call(
        body, grid=(GDN_V_HEADS // GDN_GROUP, nc), in_specs=[q, k, v, rw, rw, st, inv, o], out_specs=[o, o, o, rw, rw],
        out_shape=[wide, wide, wide, rowshape, rowshape],
        scratch_shapes=[pltpu.VMEM((GDN_GROUP, HEAD, HEAD), F32)], name="gdn_rec_bwd", compiler_params=_params(2),
    )(qk, qk, c, beta_row, gam_row, ss, invs, do)


def _gated_norm(o, gate, w):
    return _rms(o, w) * _silu(gate)


def _post_fwd(o, proj, col_off, w, name):
    t, width = o.shape
    tt = _tile(t, (256, 128))

    def fn(o, gate, w):
        return jnp.concatenate([_gated_norm(_head(o, h), _head(gate, h), w) for h in range(width // HEAD)], axis=1)

    return _tmap(fn, (1, t // tt),
                 [(o, *_rows(width, tt)), (proj, *_rows(width, tt, col_off * HEAD // width)), (w, (1, HEAD), lambda j, i: (0, 0))],
                 [((t, width), BF16, *_rows(width, tt), None)], name)[0]


def _post_bwd(o, proj, col_off, w, dout, name):
    t, width = o.shape
    tt = _tile(t, (256, 128))

    def fn(o, gate, w, dout):
        do, dgate, dw = [], [], jnp.zeros((1, HEAD), F32)
        for h in range(width // HEAD):
            _, vjp = jax.vjp(_gated_norm, _head(o, h), _head(gate, h), w)
            a, b, c = vjp(_head(dout, h))
            do.append(a)
            dgate.append(b)
            dw = dw + c
        return jnp.concatenate(do, axis=1), jnp.concatenate(dgate, axis=1), dw

    r = _rows(width, tt)
    return _tmap(fn, (1, t // tt),
                 [(o, *r), (proj, *_rows(width, tt, col_off * HEAD // width)), (w, (1, HEAD), lambda j, i: (0, 0)), (dout, *r)],
                 [((t, width), F32, *r, None), ((t, width), BF16, *r, None), ((1, HEAD), F32, (1, HEAD), lambda j, i: (0, 0), "inner")], name)


def _merge(gate_h, gate_g, yh, yg):
    return jax.nn.sigmoid(gate_h) * yh + jax.nn.sigmoid(gate_g) * yg


def _merge_fwd(proj, yh, yg):
    t = yh.shape[0]
    tt, ft = _tile(t, (256, 128)), 512
    r = _rows(ft, tt)
    return _tmap(_merge, (D_MODEL // ft, t // tt),
                 [(proj, *_rows(ft, tt, COL_GATE_H * HEAD // ft)), (proj, *_rows(ft, tt, COL_GATE_G * HEAD // ft)), (yh, *r), (yg, *r)],
                 [((t, D_MODEL), BF16, *r, None)], "merge")[0]


def _merge_bwd(proj, yh, yg, dy):
    t = yh.shape[0]
    tt, ft = _tile(t, (256, 128)), 512
    r = _rows(ft, tt)

    def fn(gate_h, gate_g, yh, yg, dy):
        _, vjp = jax.vjp(_merge, gate_h, gate_g, yh, yg)
        return vjp(dy)

    o = ((t, D_MODEL), BF16, *r, None)
    return _tmap(fn, (D_MODEL // ft, t // tt),
                 [(proj, *_rows(ft, tt, COL_GATE_H * HEAD // ft)), (proj, *_rows(ft, tt, COL_GATE_G * HEAD // ft)), (yh, *r), (yg, *r), (dy, *r)],
                 [o, o, o, o], "merge_bwd")


def _loss_head(h, target, g):
    t, d = h.shape
    tt = _tile(t, (256, 128))

    def fn(h, target, g):
        def f(h, g):
            err = _rms(h, g) - target
            return 0.5 * jnp.sum(jnp.mean(err * err, axis=-1))

        loss, (dh, dg) = jax.value_and_grad(f, (0, 1))(h, g)
        return dh, dg, jnp.full((1, HEAD), loss, F32)

    return _tmap(fn, (1, t // tt), [(h, *_rows(d, tt)), (target, *_rows(d, tt)), (g, (1, d), lambda j, i: (0, 0))],
                 [((t, d), F32, *_rows(d, tt), None), ((1, d), F32, (1, d), lambda j, i: (0, 0), "inner"),
                  ((1, HEAD), F32, (1, HEAD), lambda j, i: (0, 0), "inner")], "loss_head")


def _mixer_fwd(h, p, links):
    t = h.shape[0]
    nc = t // GDN_CHUNK
    u = _rms_fwd(h, p["mix_norm"], "mix_norm")
    w = {n: links.weight(n, h) for n in ("w_in_t", "w_in_b_t", "w_in_ab_t", "conv_w")}
    proj = _mm(u, w["w_in_t"], "nt", F32, "mix_in", after=links.started, b_rows=SCALAR_ROWS)
    proj_b = _mm(u, w["w_in_b_t"], "nt", F32, "mix_in_b")
    pab = _mm(u, w["w_in_ab_t"], "nt", F32, "mix_in_ab")
    qh, kh, bh = _hgrn_prep_fwd(proj, p["lbl"])
    oh, hs = _hgrn_rec_fwd(qh, kh, proj, bh)
    c = _conv_fwd(proj, w["conv_w"])
    qk = _qk_norm_fwd(c)
    gates_t = _gates_fwd(pab, p["alog"], p["dtb"])
    gam_row = gates_t[:GDN_V_HEADS].reshape(GDN_V_HEADS, nc, 1, GDN_CHUNK)
    beta_row = gates_t[GDN_V_HEADS:2 * GDN_V_HEADS].reshape(GDN_V_HEADS, nc, 1, GDN_CHUNK)
    og, ss, invs = _gdn_rec_fwd(qk, c, beta_row, gam_row)
    ohn = _post_fwd(oh, proj, COL_HG, p["hgrn_out_norm"], "hgrn_out")
    ogn = _post_fwd(og, proj_b, COL_GZ, p["gdn_out_norm"], "gdn_out")
    w.update({n: links.weight(n, ogn) for n in ("w_branch_hgrn", "w_branch_gdn", "w_out")})
    yh = _mm(ohn, w["w_branch_hgrn"], "nn", BF16, "branch_hgrn")
    yg = _mm(ogn, w["w_branch_gdn"], "nn", BF16, "branch_gdn")
    y = _merge_fwd(proj_b, yh, yg)
    out = _mm(y, w["w_out"], "nn", F32, "mix_out", res=h)
    saved = (w, u, proj, proj_b, pab, qh, kh, bh, oh, hs, c, qk, beta_row, gam_row, og, ss, invs, ohn, ogn, yh, yg, y)
    return out, saved


def _mixer_bwd(h, p, links, saved, dout):
    (w, u, proj, proj_b, pab, qh, kh, bh, oh, hs, c, qk, beta_row, gam_row, og, ss, invs, ohn, ogn, yh, yg, y) = saved
    t = h.shape[0]
    grads = {}
    dw_out = _mm(y, dout, "tn", BF16, "mix_out_dw")
    dy = _mm(dout, w["w_out"], "nt", F32, "mix_out_dx")
    dgate_h, dgate_g, dyh, dyg = _merge_bwd(proj_b, yh, yg, dy)
    dw_bh = _mm(ohn, dyh, "tn", BF16, "branch_hgrn_dw")
    dw_bg = _mm(ogn, dyg, "tn", BF16, "branch_gdn_dw")
    sent = links.send({"w_out": dw_out, "w_branch_hgrn": dw_bh, "w_branch_gdn": dw_bg})
    dohn = _mm(dyh, w["w_branch_hgrn"], "nt", F32, "branch_hgrn_dx", after=sent)
    dogn = _mm(dyg, w["w_branch_gdn"], "nt", F32, "branch_gdn_dx")
    doh, dhg, grads["hgrn_out_norm"] = _post_bwd(oh, proj, COL_HG, p["hgrn_out_norm"], dohn, "hgrn_out_bwd")
    dog, dgz, grads["gdn_out_norm"] = _post_bwd(og, proj_b, COL_GZ, p["gdn_out_norm"], dogn, "gdn_out_bwd")
    dqh, dkh, dhi, dbh = _hgrn_rec_bwd(qh, kh, proj, bh, hs, doh)
    dhq, dhf, grads["lbl"] = _hgrn_prep_bwd(proj, p["lbl"], dqh, dkh, dbh)
    dqv, dkv, dcv, dbeta_row, dgam_row = _gdn_rec_bwd(qk, c, beta_row, gam_row, ss, invs, dog)
    dcqk = _qk_norm_bwd(c, dqv, dkv)
    dxin, grads["conv_w"] = _conv_bwd(proj, w["conv_w"], dcqk, dcv)
    dgates_t = jnp.concatenate([dgam_row.reshape(GDN_V_HEADS, t), dbeta_row.reshape(GDN_V_HEADS, t),
                                jnp.zeros((HEAD - 2 * GDN_V_HEADS, t), F32)], axis=0)
    dpab, grads["alog"], grads["dtb"] = _gates_bwd(pab, p["alog"], p["dtb"], dgates_t)
    dproj = jnp.concatenate([dhq, dhf, dhi, dhg, dxin], axis=1)
    dproj_b = jnp.concatenate([dgz, dgate_h, dgate_g], axis=1)
    dw_t = _mm(dproj, u, "tn", BF16, "mix_in_dw")
    dw_b_t = _mm(dproj_b, u, "tn", BF16, "mix_in_b_dw")
    dw_ab_t = _mm(dpab, u, "tn", BF16, "mix_in_ab_dw")
    sent = links.send({"w_in": jnp.concatenate([dw_t, dw_ab_t[:N_SCALAR], dw_b_t], axis=0)})
    du = _mm(dproj, w["w_in_t"], "nn", F32, "mix_in_dx", after=sent, b_rows=SCALAR_ROWS)
    du = _mm(dproj_b, w["w_in_b_t"], "nn", F32, "mix_in_b_dx", res=du)
    du = _mm(dpab, w["w_in_ab_t"], "nn", F32, "mix_in_ab_dx", res=du)
    dh, grads["mix_norm"] = _rms_bwd(h, p["mix_norm"], du, dout, "mix_norm_bwd")
    return dh, grads


def _local_step(x, target, p, links):
    w1 = {n: links.weight(n, x) for n in ("ffn1_w_in", "ffn1_w_out")}
    h1, s1 = _ffn_fwd(x, p["ffn1_norm"], w1["ffn1_w_in"], w1["ffn1_w_out"], "ffn1", links.started)
    h2, sm = _mixer_fwd(h1, p, links)
    w2 = {n: links.weight(n, h2) for n in ("ffn2_w_in", "ffn2_w_out")}
    h3, s2 = _ffn_fwd(h2, p["ffn2_norm"], w2["ffn2_w_in"], w2["ffn2_w_out"], "ffn2", None)
    dh3, dfinal, loss = _loss_head(h3, target, p["final_norm"])
    g = {"final_norm": dfinal}
    dh2, g["ffn2_norm"] = _ffn_bwd(h2, p["ffn2_norm"], w2["ffn2_w_in"], w2["ffn2_w_out"], s2, dh3, "ffn2", links)
    dh1, gm = _mixer_bwd(h1, p, links, sm, dh2)
    g.update(gm)
    dx, g["ffn1_norm"] = _ffn_bwd(x, p["ffn1_norm"], w1["ffn1_w_in"], w1["ffn1_w_out"], s1, dh1, "ffn1", links)
    return loss, dx, g


HBM_SPEC = pl.BlockSpec(memory_space=pltpu.HBM)
SEM_SPEC = pl.BlockSpec(memory_space=pltpu.SEMAPHORE)
DATAFLOW = pltpu.SideEffectType.DATAFLOW_SIDE_EFFECTING


def _position():
    x, y, c = lax.axis_index("x"), lax.axis_index("y"), lax.axis_index("c")
    return x, y, c, 4 * x + 2 * y + c


def _relations(x, y, c):
    for rel in range(1, N_DEV):
        px = 1 - x if rel & 4 else x
        py = 1 - y if rel & 2 else y
        pc = 1 - c if rel & 1 else c
        yield rel, (px, py, pc), 4 * px + 2 * py + pc


def _sem_index(item, rel):
    return item * (N_DEV - 1) + rel - 1


def _landing(a, mode):
    return lax.empty((N_DEV,) + a.shape if mode == "gather" else a.shape, a.dtype)


ALL_PEERS = tuple(range(1, N_DEV))
ONE_PER_CHIP = (1, 2, 4, 6)


def _copies_start(groups, name, rels=ALL_PEERS):
    flat = [item for grp in groups for item in grp]
    n, ng = len(flat), len(groups)
    lands = [_landing(a, mode) for a, mode in flat]

    def body(*refs):
        src_refs, land_refs, sems, token = refs[:n], refs[n:2 * n], refs[2 * n:2 * n + 2 * ng], refs[-1]
        x, y, c, me = _position()
        for rel, where, peer in _relations(x, y, c):
            if rel not in rels:
                continue
            k = 0
            for gi, grp in enumerate(groups):
                for li, (_, mode) in enumerate(grp):
                    src = src_refs[k] if mode == "gather" else src_refs[k].at[peer]
                    pltpu.make_async_remote_copy(src_ref=src, dst_ref=land_refs[k].at[me], send_sem=sems[2 * gi].at[_sem_index(li, rel)],
                                                 recv_sem=sems[2 * gi + 1].at[_sem_index(li, rel)], device_id=where, device_id_type=MESH_IDS).start()
                    k += 1
        token[...] = jnp.zeros_like(token)

    sem_shapes = [pltpu.SemaphoreType.DMA((len(grp) * (N_DEV - 1),)) for grp in groups for _ in range(2)]
    thru = [pltpu.HBM(a.shape, a.dtype) for a, _ in flat] + [pltpu.HBM(l.shape, l.dtype) for l in lands]
    outs = pl.pallas_call(
        body, name=name, out_shape=(*sem_shapes, *thru, SDS((8, HEAD), F32)),
        in_specs=[HBM_SPEC] * (2 * n), out_specs=(*[SEM_SPEC] * (2 * ng), *[HBM_SPEC] * (2 * n), pl.BlockSpec(memory_space=pltpu.VMEM)),
        input_output_aliases={i: 2 * ng + i for i in range(2 * n)}, compiler_params=pltpu.CompilerParams(has_side_effects=DATAFLOW),
    )(*[pltpu.with_memory_space_constraint(a, pltpu.HBM) for a, _ in flat], *[pltpu.with_memory_space_constraint(l, pltpu.HBM) for l in lands])
    sems, srcs, landed, token = outs[:2 * ng], outs[2 * ng:2 * ng + n], outs[2 * ng + n:2 * ng + 2 * n], outs[-1]
    result, k = [], 0
    for gi, grp in enumerate(groups):
        result.append((sems[2 * gi], sems[2 * gi + 1], srcs[k:k + len(grp)], landed[k:k + len(grp)]))
        k += len(grp)
    return result, token


def _copies_wait(started, modes, after, name, rels=ALL_PEERS):
    send_sems, recv_sems, srcs, lands = started
    n = len(srcs)

    def body(*refs):
        src_refs, land_refs, ssem, rsem, token = refs[:n], refs[n:2 * n], refs[2 * n], refs[2 * n + 1], refs[-1]
        x, y, c, _ = _position()
        for rel in rels:
            for i, mode in enumerate(modes):
                src = src_refs[i] if mode == "gather" else src_refs[i].at[0]
                cp = pltpu.make_async_remote_copy(src_ref=src, dst_ref=land_refs[i].at[0], send_sem=ssem.at[_sem_index(i, rel)],
                                                  recv_sem=rsem.at[_sem_index(i, rel)], device_id=(x, y, c), device_id_type=MESH_IDS)
                cp.wait_send()
                cp.wait_recv()
        token[...] = jnp.zeros_like(token)

    outs = pl.pallas_call(
        body, name=name, out_shape=[pltpu.HBM(a.shape, a.dtype) for a in (*srcs, *lands)] + [SDS((8, HEAD), F32)],
        in_specs=[HBM_SPEC] * (2 * n) + [SEM_SPEC, SEM_SPEC, pl.BlockSpec(memory_space=pl.ANY)],
        out_specs=[HBM_SPEC] * (2 * n) + [pl.BlockSpec(memory_space=pltpu.VMEM)],
        input_output_aliases={i: i for i in range(2 * n)}, compiler_params=pltpu.CompilerParams(has_side_effects=DATAFLOW),
    )(*srcs, *lands, send_sems, recv_sems, after)
    return outs[:n], outs[n:2 * n], outs[-1]


OTHER_CHIPS = ((1, 0), (0, 1), (1, 1))


def _pass_on_start(lands, name):
    n = len(lands)

    def body(*refs):
        land_refs, ssem, rsem, token = refs[:n], refs[n], refs[n + 1], refs[-1]
        x, y, c, _ = _position()
        for j, (fx, fy) in enumerate(OTHER_CHIPS):
            slot = 4 * (1 - x if fx else x) + 2 * (1 - y if fy else y) + c
            for i in range(n):
                pltpu.make_async_remote_copy(src_ref=land_refs[i].at[slot], dst_ref=land_refs[i].at[slot], send_sem=ssem.at[i * len(OTHER_CHIPS) + j],
                                             recv_sem=rsem.at[i * len(OTHER_CHIPS) + j], device_id=(x, y, 1 - c), device_id_type=MESH_IDS).start()
        token[...] = jnp.zeros_like(token)

    sems = pltpu.SemaphoreType.DMA((n * len(OTHER_CHIPS),))
    outs = pl.pallas_call(
        body, name=name, out_shape=(sems, sems, *[pltpu.HBM(l.shape, l.dtype) for l in lands], SDS(TOKEN, F32)),
        in_specs=[HBM_SPEC] * n, out_specs=(SEM_SPEC, SEM_SPEC, *[HBM_SPEC] * n, pl.BlockSpec(memory_space=pltpu.VMEM)),
        input_output_aliases={i: 2 + i for i in range(n)}, compiler_params=pltpu.CompilerParams(has_side_effects=DATAFLOW),
    )(*lands)
    return (outs[0], outs[1], outs[2:2 + n]), outs[-1]


def _pass_on_wait(started, after, name):
    send_sems, recv_sems, lands = started
    n = len(lands)

    def body(*refs):
        land_refs, ssem, rsem = refs[:n], refs[n], refs[n + 1]
        x, y, c, _ = _position()
        for j in range(len(OTHER_CHIPS)):
            for i in range(n):
                cp = pltpu.make_async_remote_copy(src_ref=land_refs[i].at[0], dst_ref=land_refs[i].at[0], send_sem=ssem.at[i * len(OTHER_CHIPS) + j],
                                                  recv_sem=rsem.at[i * len(OTHER_CHIPS) + j], device_id=(x, y, c), device_id_type=MESH_IDS)
                cp.wait_send()
                cp.wait_recv()

    return pl.pallas_call(
        body, name=name, out_shape=[pltpu.HBM(l.shape, l.dtype) for l in lands],
        in_specs=[HBM_SPEC] * n + [SEM_SPEC, SEM_SPEC, pl.BlockSpec(memory_space=pl.ANY)], out_specs=[HBM_SPEC] * n,
        input_output_aliases={i: i for i in range(n)}, compiler_params=pltpu.CompilerParams(has_side_effects=DATAFLOW),
    )(*lands, send_sems, recv_sems, after)


WEIGHT_GROUPS = (("ffn1_w_in", "ffn1_w_out", "gdn_conv_w"), ("w_in",), ("w_branch_hgrn", "w_branch_gdn", "w_out", "ffn2_w_in", "ffn2_w_out"))
GROUP_RELS = (ONE_PER_CHIP, ONE_PER_CHIP, ALL_PEERS)


class _Links:
    def __init__(self, shards, me):
        self.me = me
        self.shards = shards
        self.weights = {}
        self.sends = []
        self.gathers = {}
        self.started = None
        self._start_gather(0, None)

    def _start_gather(self, gi, zeros):
        if gi < len(WEIGHT_GROUPS):
            items = [(self.shards[n] if zeros is None else self.shards[n] + zeros[0, 0].astype(self.shards[n].dtype), "gather")
                     for n in WEIGHT_GROUPS[gi]]
            started, self.started = _copies_start([items], "gather_start_%d" % gi, GROUP_RELS[gi])
            self.gathers[gi] = started[0]

    def weight(self, name, after):
        if name not in self.weights:
            source = {"w_in_t": "w_in", "w_in_b_t": "w_in", "w_in_ab_t": "w_in", "conv_w": "gdn_conv_w"}.get(name, name)
            gi = [i for i, grp in enumerate(WEIGHT_GROUPS) if source in grp][0]
            assert gi in self.gathers, "weight groups are asked for in order"
            srcs, lands, zero = _copies_wait(self.gathers[gi], ["gather"] * len(WEIGHT_GROUPS[gi]), after, "gather_wait_%d" % gi, GROUP_RELS[gi])
            if GROUP_RELS[gi] == ONE_PER_CHIP:
                passing, zero = _pass_on_start(lands, "gather_pass_%d" % gi)
                self._start_gather(gi + 1, zero)
                lands = _pass_on_wait(passing, self.started, "gather_passed_%d" % gi)
            else:
                self._start_gather(gi + 1, zero)
            for n, src, land in zip(WEIGHT_GROUPS[gi], srcs, lands):
                full = lax.dynamic_update_index_in_dim(land, src, self.me, 0)
                if n == "gdn_conv_w":
                    self.weights["conv_w"] = full.reshape(N_DEV, CONV_K, 4 * D_MODEL // N_DEV).transpose(1, 0, 2).reshape(CONV_K, 4 * D_MODEL)
                elif n == "w_in":
                    self.weights.update(_w_in_pieces(full.reshape(-1, D_MODEL)))
                else:
                    self.weights[n] = full.reshape(-1, D_MODEL)
        return self.weights[name]

    def send(self, grads):
        names = list(grads)
        blocks = [grads[n].reshape(N_DEV, -1, D_MODEL) for n in names]
        started, token = _copies_start([[(b, "scatter") for b in blocks]], "send_" + names[0])
        self.sends.append((names, started[0]))
        return token

    def landed(self, after):
        out = {}
        for names, started in self.sends:
            srcs, lands, _ = _copies_wait(started, ["scatter"] * len(names), after, "landed_" + names[0])
            for n, src, land in zip(names, srcs, lands):
                out[n] = lax.dynamic_update_index_in_dim(land, lax.dynamic_index_in_dim(src, self.me, 0, keepdims=False), self.me, 0)
        return out


def _adam(parts, w, m, v, name):
    n_parts, r, c = parts.shape
    tc = c if c <= 512 else (256 if r > 1024 else 512)

    def body(p_ref, w_ref, m_ref, v_ref, g_ref, d_ref, mo_ref, vo_ref):
        g = p_ref[0].astype(F32)
        for i in range(1, n_parts):
            g = g + p_ref[i].astype(F32)
        m_new = ADAM_B1 * m_ref[...] + (1.0 - ADAM_B1) * g
        v_new = ADAM_B2 * v_ref[...] + (1.0 - ADAM_B2) * (g * g)
        m_hat = m_new / (1.0 - ADAM_B1 ** ADAM_STEP)
        v_hat = v_new / (1.0 - ADAM_B2 ** ADAM_STEP)
        g_ref[...] = g
        d_ref[...] = -ADAM_LR * (m_hat / (jnp.sqrt(v_hat) + ADAM_EPS) + ADAM_WD * w_ref[...])
        mo_ref[...] = m_new
        vo_ref[...] = v_new

    spec = pl.BlockSpec((r, tc), lambda j: (0, j))
    return pl.pallas_call(
        body, grid=(c // tc,), in_specs=[pl.BlockSpec((n_parts, r, tc), lambda j: (0, 0, j)), spec, spec, spec],
        out_specs=[spec] * 4, out_shape=[SDS((r, c), F32)] * 4, name=name, compiler_params=_params(1),
    )(parts, w, m, v)


BIG = ("ffn1_w_in", "ffn1_w_out", "w_in", "w_branch_hgrn", "w_branch_gdn", "w_out", "ffn2_w_in", "ffn2_w_out")


TRANSPOSED = ("ffn1_w_in", "w_in", "ffn2_w_in")


def _shard_rows(name, shard):
    return shard.T if name in TRANSPOSED else shard


SCALAR_ROWS = 8192
N_SCALAR = 2 * GDN_V_HEADS


def _w_in_pieces(w_in_t):
    return {"w_in_t": w_in_t, "w_in_b_t": w_in_t[SCALAR_ROWS + N_SCALAR:],
            "w_in_ab_t": jnp.pad(w_in_t[SCALAR_ROWS:SCALAR_ROWS + N_SCALAR], ((0, HEAD - N_SCALAR), (0, 0)))}


def _pad_lanes(a, width=HEAD):
    return jnp.pad(a, ((0, 0), (0, width - a.shape[1])))


SMALL_ROWS = 24


def _pack_small(g, loss):
    row6 = jnp.concatenate([g["hgrn_out_norm"], g["gdn_out_norm"], g["alog"], g["dtb"], loss,
                            jnp.zeros((1, D_MODEL - 5 * HEAD), F32)], axis=1)
    return jnp.concatenate([g["ffn1_norm"], g["mix_norm"], g["lbl"], g["ffn2_norm"], g["final_norm"], row6,
                            jnp.zeros((1, D_MODEL), F32), g["conv_w"].reshape(4 * CONV_K, D_MODEL)], axis=0)


def _pack_small_state(a):
    row6 = jnp.concatenate([a["hgrn_out_norm"], a["gdn_out_norm"], _pad_lanes(a["gdn_a_log"]), _pad_lanes(a["gdn_dt_bias"]),
                            jnp.zeros((1, D_MODEL - 4 * HEAD), F32)], axis=1)
    return jnp.concatenate([a["ffn1_norm"], a["mix_norm"], a["hgrn_lb_logits"], a["ffn2_norm"], a["final_norm"].reshape(1, D_MODEL),
                            row6, jnp.zeros((1, D_MODEL), F32)], axis=0)


def _unpack_small(a):
    return {"ffn1_norm": a[0:1], "mix_norm": a[1:2], "hgrn_lb_logits": a[2:4], "ffn2_norm": a[4:5], "final_norm": a[5],
            "hgrn_out_norm": a[6:7, :HEAD], "gdn_out_norm": a[6:7, HEAD:2 * HEAD],
            "gdn_a_log": a[6:7, 2 * HEAD:2 * HEAD + GDN_V_HEADS], "gdn_dt_bias": a[6:7, 3 * HEAD:3 * HEAD + GDN_V_HEADS]}


NAMES = ("ffn1_norm", "ffn1_w_in", "ffn1_w_out", "mix_norm", "w_in", "hgrn_lb_logits", "hgrn_out_norm", "gdn_conv_w", "gdn_a_log",
         "gdn_dt_bias", "gdn_out_norm", "w_branch_hgrn", "w_branch_gdn", "w_out", "ffn2_norm", "ffn2_w_in", "ffn2_w_out", "final_norm")


def kernel(x, ffn1_norm, ffn1_w_in, ffn1_w_out, mix_norm, w_in, hgrn_lb_logits, hgrn_out_norm, gdn_conv_w, gdn_a_log, gdn_dt_bias, gdn_out_norm, w_branch_hgrn, w_branch_gdn, w_out, ffn2_norm, ffn2_w_in, ffn2_w_out, final_norm, loss_target, m_ffn1_norm, m_ffn1_w_in, m_ffn1_w_out, m_mix_norm, m_w_in, m_hgrn_lb_logits, m_hgrn_out_norm, m_gdn_conv_w, m_gdn_a_log, m_gdn_dt_bias, m_gdn_out_norm, m_w_branch_hgrn, m_w_branch_gdn, m_w_out, m_ffn2_norm, m_ffn2_w_in, m_ffn2_w_out, m_final_norm, v_ffn1_norm, v_ffn1_w_in, v_ffn1_w_out, v_mix_norm, v_w_in, v_hgrn_lb_logits, v_hgrn_out_norm, v_gdn_conv_w, v_gdn_a_log, v_gdn_dt_bias, v_gdn_out_norm, v_w_branch_hgrn, v_w_branch_gdn, v_w_out, v_ffn2_norm, v_ffn2_w_in, v_ffn2_w_out, v_final_norm):
    wts = dict(zip(NAMES, (ffn1_norm, ffn1_w_in, ffn1_w_out, mix_norm, w_in, hgrn_lb_logits, hgrn_out_norm, gdn_conv_w, gdn_a_log,
                           gdn_dt_bias, gdn_out_norm, w_branch_hgrn, w_branch_gdn, w_out, ffn2_norm, ffn2_w_in, ffn2_w_out, final_norm)))
    mom = dict(zip(NAMES, (m_ffn1_norm, m_ffn1_w_in, m_ffn1_w_out, m_mix_norm, m_w_in, m_hgrn_lb_logits, m_hgrn_out_norm, m_gdn_conv_w,
                           m_gdn_a_log, m_gdn_dt_bias, m_gdn_out_norm, m_w_branch_hgrn, m_w_branch_gdn, m_w_out, m_ffn2_norm, m_ffn2_w_in,
                           m_ffn2_w_out, m_final_norm)))
    var = dict(zip(NAMES, (v_ffn1_norm, v_ffn1_w_in, v_ffn1_w_out, v_mix_norm, v_w_in, v_hgrn_lb_logits, v_hgrn_out_norm, v_gdn_conv_w,
                           v_gdn_a_log, v_gdn_dt_bias, v_gdn_out_norm, v_w_branch_hgrn, v_w_branch_gdn, v_w_out, v_ffn2_norm, v_ffn2_w_in,
                           v_ffn2_w_out, v_final_norm)))
    me = 4 * lax.axis_index("x") + 2 * lax.axis_index("y") + lax.axis_index("c")

    conv_shard = wts["gdn_conv_w"][0]
    shards = {n: _shard_rows(n, wts[n][0]).astype(BF16) for n in BIG}
    shards["gdn_conv_w"] = conv_shard.reshape(2, D_MODEL)
    links = _Links(shards, me)
    p = {"ffn1_norm": wts["ffn1_norm"], "mix_norm": wts["mix_norm"], "ffn2_norm": wts["ffn2_norm"], "final_norm": wts["final_norm"].reshape(1, D_MODEL),
         "lbl": wts["hgrn_lb_logits"], "hgrn_out_norm": wts["hgrn_out_norm"], "gdn_out_norm": wts["gdn_out_norm"],
         "alog": _pad_lanes(wts["gdn_a_log"]), "dtb": _pad_lanes(wts["gdn_dt_bias"])}

    loss, dx, g = _local_step(x[0], loss_target[0], p, links)

    small_started, small_token = _copies_start([[(_pack_small(g, loss), "gather")]], "small_start")
    landed = links.landed(small_token)

    big = [{} for _ in range(4)]
    for n in BIG:
        res = _adam(landed[n], _shard_rows(n, wts[n][0]), _shard_rows(n, mom[n][0]), _shard_rows(n, var[n][0]), "adam_" + n)
        for kind in range(4):
            big[kind][n] = _shard_rows(n, res[kind])
    small_srcs, small_lands, _ = _copies_wait(small_started[0], ["gather"], res[0], "small_wait")
    small_parts = lax.dynamic_update_index_in_dim(small_lands[0], small_srcs[0], me, 0)
    n_vec = SMALL_ROWS - 4 * CONV_K
    small_raw = _adam(small_parts[:, :n_vec], _pack_small_state(wts), _pack_small_state(mom), _pack_small_state(var), "adam_small")
    small = [_unpack_small(o) for o in small_raw]
    loss_total = small_raw[0][6, 4 * HEAD]
    conv_parts = small_parts[:, n_vec:].reshape(N_DEV, CONV_K, 4 * D_MODEL)
    width = 4 * D_MODEL // N_DEV
    conv_mine = lax.dynamic_slice_in_dim(conv_parts, me * width, width, axis=2)
    conv = _adam(conv_mine, conv_shard, mom["gdn_conv_w"][0], var["gdn_conv_w"][0], "adam_conv")

    outs = []
    for kind in range(4):
        for n in NAMES:
            if n in BIG:
                outs.append(big[kind][n][None])
            elif n == "gdn_conv_w":
                outs.append(conv[kind][None])
            else:
                outs.append(small[kind][n])
    return (loss_total, dx[None], *outs)
```

```python
import functools

import jax
import jax.numpy as jnp
from jax import lax
from jax.experimental import pallas as pl
from jax.experimental.pallas import tpu as pltpu

F32 = jnp.float32
BF16 = jnp.bfloat16
HIGHEST = lax.Precision.HIGHEST
MESH_IDS = pl.DeviceIdType.MESH

D_MODEL = 1024
D_FF = 2816
N_DEV = 8
EPS = 1e-6
HEAD = 128
HG_HEADS = 8
GDN_QK_HEADS = 8
GDN_V_HEADS = 16
GDN_CHUNK = 64
HG_CHUNK = 16
CONV_K = 4
IN_WIDTH = 12320
COL_HQ, COL_HF, COL_HI, COL_HG, COL_GQ, COL_GK, COL_GV = 0, 8, 16, 24, 32, 40, 48
COL_GZ, COL_GATE_H, COL_GATE_G = 0, 16, 24
VMEM_LIMIT = 56 * 1024 * 1024

ADAM_LR, ADAM_B1, ADAM_B2, ADAM_EPS, ADAM_WD, ADAM_STEP = 0.001, 0.9, 0.999, 1e-08, 0.01, 10

SDS = jax.ShapeDtypeStruct


def _params(n_axes):
    return pltpu.CompilerParams(dimension_semantics=("arbitrary",) * n_axes, vmem_limit_bytes=VMEM_LIMIT)


def _tile(n, candidates=(512, 384, 256, 128, 64, 32, 16, 8)):
    for c in candidates:
        if n % c == 0:
            return c
    return n


_DIMS = {"nn": ((1,), (0,)), "nt": ((1,), (1,)), "tn": ((0,), (0,))}


def _bdot_raw(a, b, dims):
    return lax.dot_general(a.astype(BF16), b.astype(BF16), (_DIMS[dims], ((), ())), preferred_element_type=F32)


@functools.partial(jax.custom_vjp, nondiff_argnums=(2,))
def _bdot(a, b, dims):
    return _bdot_raw(a, b, dims)


def _bdot_fwd(a, b, dims):
    return _bdot_raw(a, b, dims), (a, b)


def _bdot_bwd(dims, res, ct):
    a, b = res
    if dims == "nn":
        return _bdot_raw(ct, b, "nt"), _bdot_raw(a, ct, "tn")
    if dims == "nt":
        return _bdot_raw(ct, b, "nn"), _bdot_raw(ct, a, "tn")
    return _bdot_raw(b, ct, "nt"), _bdot_raw(a, ct, "nn")


_bdot.defvjp(_bdot_fwd, _bdot_bwd)


def _hdot_raw(a, b):
    return jnp.dot(a, b, precision=HIGHEST, preferred_element_type=F32)


MM_VMEM_BUDGET = 38 * 1024 * 1024
TOKEN = (8, HEAD)


def _mm_tiles(m, n, k, a_bytes, b_bytes, o_bytes, r_bytes, m_align=8):
    def need(tm, tn, tk):
        return 2 * (tm * tk * a_bytes + tk * tn * b_bytes + tm * tn * (o_bytes + r_bytes)) + (tm * tn * 4 if tk < k else 0)

    def shrink(tm, tn, tk, floor_m, floor_n):
        while need(tm, tn, tk) > MM_VMEM_BUDGET:
            if tn > floor_n and tn % 256 == 0 and tn >= tm:
                tn //= 2
            elif tm > floor_m and tm % (2 * m_align) == 0:
                tm //= 2
            elif tn > floor_n and tn % 256 == 0:
                tn //= 2
            else:
                return None
        return tm, tn, tk

    tm = _tile(m, (1408, 1024, 704, 512, 256, 128, 64, 32, 16, 8))
    tn = _tile(n, (1408, 1024, 512, 256, 128))
    whole = shrink(tm, tn, k, min(tm, 1024), min(tn, 512))
    if whole is not None:
        return whole
    tk = _tile(k, (2048, 1408, 1024, 512, 256, 128, 64, 32, 16, 8))
    while True:
        fit = shrink(tm, tn, tk, min(tm, 256), min(tn, 512))
        if fit is not None or tk <= 512 or tk % 256:
            return fit if fit is not None else (tm, tn, tk)
        tk //= 2


def _mm(a, b, dims, out_dtype, name, res=None, alpha=1.0, after=None, b_rows=None):
    b_shape = b.shape if b_rows is None else (b_rows, b.shape[1])
    if dims == "nn":
        (m, k), (k2, n) = a.shape, b_shape
    elif dims == "nt":
        (m, k), (n, k2) = a.shape, b_shape
    else:
        (k, m), (k2, n) = a.shape, b_shape
    assert k == k2, (a.shape, b.shape, dims)
    has_res = res is not None
    tm, tn, tk = _mm_tiles(m, n, k, a.dtype.itemsize, b.dtype.itemsize, jnp.dtype(out_dtype).itemsize, res.dtype.itemsize if has_res else 0,
                           m_align=HEAD if dims == "tn" else 8)
    nk = k // tk
    a_spec = pl.BlockSpec((tk, tm), lambda i, j, kk: (kk, i)) if dims == "tn" else pl.BlockSpec((tm, tk), lambda i, j, kk: (i, kk))
    b_spec = pl.BlockSpec((tn, tk), lambda i, j, kk: (j, kk)) if dims == "nt" else pl.BlockSpec((tk, tn), lambda i, j, kk: (kk, j))
    o_spec = pl.BlockSpec((tm, tn), lambda i, j, kk: (i, j))

    def finish(acc, r_ref, o_ref):
        out = acc * alpha if alpha != 1.0 else acc
        if has_res:
            out = r_ref[...].astype(F32) + out
        o_ref[...] = out.astype(o_ref.dtype)

    n_in = 2 + has_res + (after is not None)

    def body(*refs):
        a_ref, b_ref = refs[:2]
        r_ref = refs[2] if has_res else None
        o_ref = refs[n_in]
        p = _bdot_raw(a_ref[...], b_ref[...], dims)
        if nk == 1:
            finish(p, r_ref, o_ref)
            return
        acc_ref = refs[-1]
        kk = pl.program_id(2)

        @pl.when(kk == 0)
        def _():
            acc_ref[...] = p

        @pl.when(kk > 0)
        def _():
            acc_ref[...] += p

        @pl.when(kk == nk - 1)
        def _():
            finish(acc_ref[...], r_ref, o_ref)

    args = (a, b) + ((res,) if has_res else ()) + ((after,) if after is not None else ())
    in_specs = [a_spec, b_spec] + ([o_spec] if has_res else []) + ([pl.BlockSpec(TOKEN, lambda i, j, kk: (0, 0))] if after is not None else [])
    return pl.pallas_call(
        body, grid=(m // tm, n // tn, nk), in_specs=in_specs, out_specs=o_spec, out_shape=SDS((m, n), out_dtype),
        scratch_shapes=[pltpu.VMEM((tm, tn), F32)] if nk > 1 else [], name=name, compiler_params=_params(3),
    )(*args)


def _tmap(fn, grid, ins, outs, name):
    n_in = len(ins)
    n_ax = len(grid)

    def body(*refs):
        vals = fn(*[r[...] for r in refs[:n_in]])
        if not isinstance(vals, (tuple, list)):
            vals = (vals,)
        first_inner = pl.program_id(n_ax - 1) == 0
        first_all = first_inner
        for ax in range(n_ax - 1):
            first_all = jnp.logical_and(first_all, pl.program_id(ax) == 0)

        def put(ref, val, acc):
            val = val.astype(ref.dtype)
            if acc is None:
                ref[...] = val
                return
            first = first_inner if acc == "inner" else first_all

            @pl.when(first)
            def _():
                ref[...] = val

            @pl.when(jnp.logical_not(first))
            def _():
                ref[...] += val

        for ref, val, o in zip(refs[n_in:], vals, outs):
            put(ref, val, o[4])

    return pl.pallas_call(
        body, grid=grid,
        in_specs=[pl.BlockSpec(bs, im) for _, bs, im in ins],
        out_specs=[pl.BlockSpec(o[2], o[3]) for o in outs],
        out_shape=[SDS(o[0], o[1]) for o in outs],
        name=name, compiler_params=_params(n_ax),
    )(*[a for a, _, _ in ins])


def _rows(width, tt, off=0):
    return (tt, width), (lambda j, i: (i, off + j))


def _rms(x, g):
    x = x.astype(F32)
    return x * lax.rsqrt(jnp.mean(x * x, axis=-1, keepdims=True) + EPS) * g


def _sigmoid(x):
    return 0.5 * jnp.tanh(0.5 * x) + 0.5


def _silu(x):
    return x * _sigmoid(x)


def _softplus(x):
    return jnp.maximum(x, 0.0) + jnp.log1p(jnp.exp(-jnp.abs(x)))


def _rms_fwd(x, g, name):
    t, d = x.shape
    tt = _tile(t, (256, 128))
    return _tmap(_rms, (1, t // tt), [(x, *_rows(d, tt)), (g, (1, d), lambda j, i: (0, 0))],
                 [((t, d), BF16, *_rows(d, tt), None)], name)[0]


def _rms_bwd(x, g, dn, dres, name):
    t, d = x.shape
    tt = _tile(t, (256, 128))

    def fn(x, g, dn, dres):
        _, vjp = jax.vjp(_rms, x, g)
        dx, dg = vjp(dn.astype(F32))
        return dres + dx, dg

    return _tmap(fn, (1, t // tt),
                 [(x, *_rows(d, tt)), (g, (1, d), lambda j, i: (0, 0)), (dn, *_rows(d, tt)), (dres, *_rows(d, tt))],
                 [((t, d), F32, *_rows(d, tt), None), ((1, d), F32, (1, d), lambda j, i: (0, 0), "inner")], name)


def _swiglu(ab):
    return _silu(ab[:, :D_FF].astype(F32)) * ab[:, D_FF:].astype(F32)


def _swiglu_fwd(ab, name):
    t = ab.shape[0]
    tt = _tile(t, (128,))
    return _tmap(_swiglu, (1, t // tt), [(ab, *_rows(2 * D_FF, tt))], [((t, D_FF), BF16, *_rows(D_FF, tt), None)], name)[0]


def _swiglu_bwd(ab, ds, name):
    t = ab.shape[0]
    tt = _tile(t, (128,))

    def fn(ab, ds):
        a, b = ab[:, :D_FF].astype(F32), ab[:, D_FF:].astype(F32)
        _, vjp = jax.vjp(lambda a, b: _silu(a) * b, a, b)
        da, db = vjp(ds.astype(F32))
        return jnp.concatenate([da, db], axis=1)

    return _tmap(fn, (1, t // tt), [(ab, *_rows(2 * D_FF, tt)), (ds, *_rows(D_FF, tt))],
                 [((t, 2 * D_FF), BF16, *_rows(2 * D_FF, tt), None)], name)[0]


def _ffn_fwd(h, g, w_in_t, w_out, tag, after):
    n = _rms_fwd(h, g, tag + "_norm")
    ab = _mm(n, w_in_t, "nt", BF16, tag + "_in", after=after)
    s = _swiglu_fwd(ab, tag + "_act")
    out = _mm(s, w_out, "nn", F32, tag + "_out", res=h, alpha=0.5)
    return out, (n, ab, s)


def _ffn_bwd(h, g, w_in_t, w_out, saved, dout, tag, links):
    n, ab, s = saved
    sent = links.send({tag + "_w_out": _mm(s, dout, "tn", BF16, tag + "_dw_out", alpha=0.5)})
    ds = _mm(dout, w_out, "nt", BF16, tag + "_ds", alpha=0.5, after=sent)
    dab = _swiglu_bwd(ab, ds, tag + "_dact")
    sent = links.send({tag + "_w_in": _mm(dab, n, "tn", BF16, tag + "_dw_in")})
    dn = _mm(dab, w_in_t, "nn", F32, tag + "_dn", after=sent)
    return _rms_bwd(h, g, dn, dout, tag + "_dnorm")


def _chunk_sum_matrix(n, chunk, transpose=False):
    row = lax.broadcasted_iota(jnp.int32, (n, n), 0)
    col = lax.broadcasted_iota(jnp.int32, (n, n), 1)
    if transpose:
        row, col = col, row
    return jnp.where(jnp.logical_and(col <= row, row // chunk == col // chunk), 1.0, 0.0).astype(F32)


def _hgrn_gates(hq, hf, lbl):
    lb = _sigmoid(lbl[0:1, :] - lbl[1:2, :])
    sg = _sigmoid(hf)
    f = lb + (1.0 - lb) * sg
    q = _silu(hq) * HEAD ** -0.5
    k = (1.0 - lb) * (1.0 - sg)
    return q, k, jnp.log(f)


def _hgrn_prep_fwd(proj, lbl):
    t = proj.shape[0]
    tt, ft = _tile(t, (256, 128)), 512

    def fn(hq, hf, lbl):
        q, k, log_f = _hgrn_gates(hq, hf, lbl)
        return q, k, _hdot_raw(_chunk_sum_matrix(tt, HG_CHUNK), log_f)

    o = ((t, D_MODEL), F32, *_rows(ft, tt), None)
    return _tmap(fn, (D_MODEL // ft, t // tt),
                 [(proj, *_rows(ft, tt, COL_HQ * HEAD // ft)), (proj, *_rows(ft, tt, COL_HF * HEAD // ft)), (lbl, (2, ft), lambda j, i: (0, j))],
                 [o, o, o], "hgrn_prep")


def _hgrn_prep_bwd(proj, lbl, dq, dk, db):
    t = proj.shape[0]
    tt, ft = _tile(t, (256, 128)), 512

    def fn(hq, hf, lbl, dq, dk, db):
        dlog_f = _hdot_raw(_chunk_sum_matrix(tt, HG_CHUNK, transpose=True), db)
        _, vjp = jax.vjp(_hgrn_gates, hq, hf, lbl)
        return vjp((dq, dk, dlog_f))

    o = ((t, D_MODEL), BF16, *_rows(ft, tt), None)
    r = _rows(ft, tt)
    return _tmap(fn, (D_MODEL // ft, t // tt),
                 [(proj, *_rows(ft, tt, COL_HQ * HEAD // ft)), (proj, *_rows(ft, tt, COL_HF * HEAD // ft)), (lbl, (2, ft), lambda j, i: (0, j)),
                  (dq, *r), (dk, *r), (db, *r)],
                 [o, o, ((2, D_MODEL), F32, (2, ft), lambda j, i: (0, j), "inner")], "hgrn_prep_bwd")


def _hgrn_chunks(q, k, v, b, st):
    n = q[0].shape[0]
    half = n // 2
    srow = lax.broadcasted_iota(jnp.int32, (half, HEAD), 0)
    inter = _each(lambda q, b, st: _bdot(q * jnp.exp(b), st, "nt"), q, b, st)

    def below_scores(q, k, b):
        ref = b[half:half + 1, :]
        return _bdot(q[half:] * jnp.exp(jnp.minimum(b[half:] - ref, 0.0)), k[:half] * jnp.exp(jnp.minimum(ref - b[:half], 0.0)), "nt")

    below = _each(lambda a, v: _bdot(a, v[:half], "nn"), _each(below_scores, q, k, b), v)

    def diagonal(q, k, v, b):
        rows = []
        for lo in (0, half):
            qb, kb, vb, bb = (a[lo:lo + half] for a in (q, k, v, b))
            for t in range(half):
                e = jnp.where(srow <= t, jnp.exp(jnp.minimum(bb[t:t + 1, :] - bb, 0.0)), 0.0)
                a = jnp.sum(qb[t:t + 1, :] * kb * e, axis=1, keepdims=True)
                rows.append(jnp.sum(a * vb, axis=0, keepdims=True))
        return jnp.concatenate(rows, axis=0)

    diag = _each(diagonal, q, k, v, b)
    o = _each(lambda inter, diag, below: inter + diag + jnp.concatenate([jnp.zeros_like(below), below], axis=0), inter, diag, below)

    def new_state(k, v, b, st):
        bend = b[n - 1:n, :]
        return st * jnp.exp(bend) + _bdot(v, k * jnp.exp(bend - b), "tn")

    return o, _each(new_state, k, v, b, st)


HG_GROUP = 4
HG_PER = GDN_CHUNK // HG_CHUNK


def _hgrn_rec_fwd(q, k, proj, b):
    t = q.shape[0]
    nc = t // GDN_CHUNK
    blk = (GDN_CHUNK, HG_GROUP * HEAD)
    im = lambda h, c: (c, h)

    def body(q_ref, k_ref, v_ref, b_ref, o_ref, hs_ref, st_ref):
        @pl.when(pl.program_id(1) == 0)
        def _():
            st_ref[...] = jnp.zeros_like(st_ref)

        heads = range(HG_GROUP)
        for j in range(HG_PER):
            sl = pl.ds(HG_CHUNK * j, HG_CHUNK)
            st = tuple(st_ref[g] for g in heads)
            o, st_new = _hgrn_chunks(*[tuple(r[sl, _head_lanes(g)] for g in heads) for r in (q_ref, k_ref, v_ref, b_ref)], st)
            for g in heads:
                hs_ref[g, j] = st[g]
                o_ref[sl, _head_lanes(g)] = o[g]
                st_ref[g] = st_new[g]

    return pl.pallas_call(
        body, grid=(HG_HEADS // HG_GROUP, nc),
        in_specs=[pl.BlockSpec(blk, im), pl.BlockSpec(blk, im), pl.BlockSpec(blk, lambda h, c: (c, COL_HI // HG_GROUP + h)), pl.BlockSpec(blk, im)],
        out_specs=[pl.BlockSpec(blk, im), pl.BlockSpec((HG_GROUP, HG_PER, HEAD, HEAD), lambda h, c: (h, c, 0, 0))],
        out_shape=[SDS((t, D_MODEL), F32), SDS((HG_HEADS, nc * HG_PER, HEAD, HEAD), F32)],
        scratch_shapes=[pltpu.VMEM((HG_GROUP, HEAD, HEAD), F32)], name="hgrn_rec", compiler_params=_params(2),
    )(q, k, proj, b)


def _hgrn_rec_bwd(q, k, proj, b, hs, do):
    t = q.shape[0]
    nc = t // GDN_CHUNK
    blk = (GDN_CHUNK, HG_GROUP * HEAD)
    im = lambda h, c: (nc - 1 - c, h)

    def body(q_ref, k_ref, v_ref, b_ref, hs_ref, do_ref, dq_ref, dk_ref, dv_ref, db_ref, dst_ref):
        @pl.when(pl.program_id(1) == 0)
        def _():
            dst_ref[...] = jnp.zeros_like(dst_ref)

        heads = range(HG_GROUP)
        for j in reversed(range(HG_PER)):
            sl = pl.ds(HG_CHUNK * j, HG_CHUNK)
            _, vjp = jax.vjp(_hgrn_chunks, *[tuple(r[sl, _head_lanes(g)] for g in heads) for r in (q_ref, k_ref, v_ref, b_ref)],
                             tuple(hs_ref[g, j] for g in heads))
            dq, dk, dv, db, dst = vjp((tuple(do_ref[sl, _head_lanes(g)] for g in heads), tuple(dst_ref[g] for g in heads)))
            for g in heads:
                ln = _head_lanes(g)
                dq_ref[sl, ln] = dq[g]
                dk_ref[sl, ln] = dk[g]
                dv_ref[sl, ln] = dv[g].astype(dv_ref.dtype)
                db_ref[sl, ln] = db[g]
                dst_ref[g] = dst[g]

    spec = pl.BlockSpec(blk, im)
    return pl.pallas_call(
        body, grid=(HG_HEADS // HG_GROUP, nc),
        in_specs=[spec, spec, pl.BlockSpec(blk, lambda h, c: (nc - 1 - c, COL_HI // HG_GROUP + h)), spec,
                  pl.BlockSpec((HG_GROUP, HG_PER, HEAD, HEAD), lambda h, c: (h, nc - 1 - c, 0, 0)), spec],
        out_specs=[spec, spec, spec, spec],
        out_shape=[SDS((t, D_MODEL), F32), SDS((t, D_MODEL), F32), SDS((t, D_MODEL), BF16), SDS((t, D_MODEL), F32)],
        scratch_shapes=[pltpu.VMEM((HG_GROUP, HEAD, HEAD), F32)], name="hgrn_rec_bwd", compiler_params=_params(2),
    )(q, k, proj, b, hs, do)


def _shift_down(x, d):
    if d == 0:
        return x
    row = lax.broadcasted_iota(jnp.int32, x.shape, 0)
    return jnp.where(row >= d, pltpu.roll(x, d, 0), 0.0)


def _shift_up(x, d):
    if d == 0:
        return x
    n = x.shape[0]
    row = lax.broadcasted_iota(jnp.int32, x.shape, 0)
    return jnp.where(row < n - d, pltpu.roll(x, n - d, 0), 0.0)


def _conv_fwd(proj, conv_w):
    t = proj.shape[0]
    width = 2 * D_MODEL + 2 * D_MODEL

    def body(x_ref, w_ref, c_ref, y_ref):
        x, w = x_ref[...], w_ref[...]
        y = w[CONV_K - 1:CONV_K, :] * x
        for j in range(CONV_K - 1):
            y = y + w[j:j + 1, :] * _shift_down(x, CONV_K - 1 - j)
        y_ref[...] = y
        c_ref[...] = _silu(y)

    out = pl.BlockSpec((t, HEAD), lambda j: (0, j))
    return pl.pallas_call(
        body, grid=(width // HEAD,),
        in_specs=[pl.BlockSpec((t, HEAD), lambda j: (0, COL_GQ + j)), pl.BlockSpec((CONV_K, HEAD), lambda j: (0, j))],
        out_specs=[out, out], out_shape=[SDS((t, width), F32), SDS((t, width), F32)],
        name="gdn_conv", compiler_params=_params(1),
    )(proj, conv_w)


def _conv_bwd(proj, conv_w, y, dc_qk, dc_v):
    t = proj.shape[0]
    n_qk = dc_qk.shape[1] // HEAD
    width = dc_qk.shape[1] + dc_v.shape[1]

    def body(x_ref, w_ref, y_ref, dqk_ref, dv_ref, dx_ref, dw_ref):
        x, w, y = x_ref[...], w_ref[...], y_ref[...]
        sg = _sigmoid(y)
        dc = jnp.where(pl.program_id(0) < n_qk, dqk_ref[...], dv_ref[...])
        dy = dc * (sg * (1.0 + y * (1.0 - sg)))
        ahead = [_shift_up(dy, CONV_K - 1 - j) for j in range(CONV_K)]
        dx = w[0:1, :] * ahead[0]
        for j in range(1, CONV_K):
            dx = dx + w[j:j + 1, :] * ahead[j]
        dx_ref[...] = dx.astype(dx_ref.dtype)
        dw_ref[...] = jnp.concatenate([jnp.sum(x * ahead[j], axis=0, keepdims=True) for j in range(CONV_K)], axis=0)

    blk = pl.BlockSpec((t, HEAD), lambda j: (0, j))
    return pl.pallas_call(
        body, grid=(width // HEAD,),
        in_specs=[pl.BlockSpec((t, HEAD), lambda j: (0, COL_GQ + j)), pl.BlockSpec((CONV_K, HEAD), lambda j: (0, j)), blk,
                  pl.BlockSpec((t, HEAD), lambda j: (0, jnp.minimum(j, n_qk - 1))), pl.BlockSpec((t, HEAD), lambda j: (0, jnp.maximum(j - n_qk, 0)))],
        out_specs=[blk, pl.BlockSpec((CONV_K, HEAD), lambda j: (0, j))],
        out_shape=[SDS((t, width), BF16), SDS((CONV_K, width), F32)],
        name="gdn_conv_bwd", compiler_params=_params(1),
    )(proj, conv_w, y, dc_qk, dc_v)


def _l2norm(x, scale):
    return x * lax.rsqrt(jnp.sum(x * x, axis=-1, keepdims=True) + EPS) * scale


def _head(a, h):
    return a[:, h * HEAD:(h + 1) * HEAD]


def _qk_scale(h):
    return HEAD ** -0.5 if h < GDN_QK_HEADS else 1.0


def _qk_norm_fwd(c):
    t = c.shape[0]
    tt = _tile(t, (256, 128))
    width = 2 * D_MODEL

    def fn(x):
        return jnp.concatenate([_l2norm(_head(x, h), _qk_scale(h)) for h in range(2 * GDN_QK_HEADS)], axis=1)

    return _tmap(fn, (1, t // tt), [(c, *_rows(width, tt))], [((t, width), F32, *_rows(width, tt), None)], "gdn_qk_norm")[0]


def _qk_norm_bwd(c, dq_rep, dk_rep):
    t = c.shape[0]
    tt = _tile(t, (256, 128))
    width = 2 * D_MODEL

    def fn(x, dq2, dk2):
        out = []
        for h in range(2 * GDN_QK_HEADS):
            d2, hh = (dq2, h) if h < GDN_QK_HEADS else (dk2, h - GDN_QK_HEADS)
            _, vjp = jax.vjp(lambda x: _l2norm(x, _qk_scale(h)), _head(x, h))
            out.append(vjp(_head(d2, 2 * hh) + _head(d2, 2 * hh + 1))[0])
        return jnp.concatenate(out, axis=1)

    r = _rows(width, tt)
    return _tmap(fn, (1, t // tt), [(c, *r), (dq_rep, *r), (dk_rep, *r)], [((t, width), F32, *r, None)], "gdn_qk_norm_bwd")[0]


def _gdn_gates(x, alog, dtb):
    return -jnp.exp(alog) * _softplus(x + dtb), _sigmoid(x)


def _gates_fwd(pab, alog, dtb):
    t = pab.shape[0]
    tt = _tile(t, (256, 128))

    def fn(x, alog, dtb):
        g, beta = _gdn_gates(x, alog, dtb)
        lane = lax.broadcasted_iota(jnp.int32, g.shape, 1)
        return jnp.where(lane < GDN_V_HEADS, _hdot_raw(_chunk_sum_matrix(tt, GDN_CHUNK), g), beta).T

    p = (alog, (1, HEAD), lambda j, i: (0, 0)), (dtb, (1, HEAD), lambda j, i: (0, 0))
    return _tmap(fn, (1, t // tt), [(pab, *_rows(HEAD, tt)), *p], [((HEAD, t), F32, (HEAD, tt), lambda j, i: (0, i), None)], "gdn_gates")[0]


def _gates_bwd(pab, alog, dtb, dout_t):
    t = pab.shape[0]
    tt = _tile(t, (256, 128))

    def fn(x, alog, dtb, dout_t):
        dout = dout_t.T
        lane = lax.broadcasted_iota(jnp.int32, dout.shape, 1)
        dgam = jnp.where(lane < GDN_V_HEADS, dout, 0.0)
        dbeta = jnp.where(jnp.logical_and(lane >= GDN_V_HEADS, lane < 2 * GDN_V_HEADS), dout, 0.0)
        dg = _hdot_raw(_chunk_sum_matrix(tt, GDN_CHUNK, transpose=True), dgam)
        _, vjp = jax.vjp(_gdn_gates, x, alog, dtb)
        return vjp((dg, dbeta))

    p = (alog, (1, HEAD), lambda j, i: (0, 0)), (dtb, (1, HEAD), lambda j, i: (0, 0))
    acc = ((1, HEAD), F32, (1, HEAD), lambda j, i: (0, 0), "inner")
    return _tmap(fn, (1, t // tt), [(pab, *_rows(HEAD, tt)), *p, (dout_t, (HEAD, tt), lambda j, i: (0, i))],
                 [((t, HEAD), BF16, *_rows(HEAD, tt), None), acc, acc], "gdn_gates_bwd")


def _split_bf16(x):
    hi = x.astype(BF16)
    return hi, (x - hi.astype(F32)).astype(BF16)


def _dot3(a, b):
    (ah, al), (bh, bl) = a, b
    return _bdot_raw(ah, bh, "nn") + (_bdot_raw(ah, bl, "nn") + _bdot_raw(al, bh, "nn"))


def _each(fn, *lists):
    return tuple(fn(*xs) for xs in zip(*lists))


def _unit_lower_inverses_raw(a):
    n = a[0].shape[0]
    row = lax.broadcasted_iota(jnp.int32, (n, n), 0)
    col = lax.broadcasted_iota(jnp.int32, (n, n), 1)
    eye = jnp.where(row == col, 1.0, 0.0).astype(F32)
    p = _each(lambda a: eye - a, a)
    x = _each(_split_bf16, a)
    m = 2
    while m < 2 * n:
        x = _each(_split_bf16, _each(_dot3, x, x))
        p = _each(lambda p, x: p + _dot3(_split_bf16(p), x), p, x)
        m *= 2
    return p


@jax.custom_vjp
def _unit_lower_inverses(a, known):
    return _unit_lower_inverses_raw(a) if known is None else known


def _uli_fwd(a, known):
    inv = _unit_lower_inverses(a, known)
    return inv, (inv, known)


def _uli_bwd(res, ct):
    inv, known = res
    right = _each(lambda ct, inv: _bdot_raw(ct, inv, "nt"), ct, inv)
    da = _each(lambda inv, r: -_bdot_raw(inv, r, "tn"), inv, right)
    return da, (None if known is None else _each(jnp.zeros_like, known))


_unit_lower_inverses.defvjp(_uli_fwd, _uli_bwd)


def _gdn_chunks(q, k, v, beta_rows, gam_rows, s, inv_known=None):
    n = q[0].shape[0]
    heads = range(len(q))
    row = lax.broadcasted_iota(jnp.int32, (n, n), 0)
    col = lax.broadcasted_iota(jnp.int32, (n, n), 1)
    beta_cols, gam_cols = beta_rows.T, gam_rows.T
    beta = tuple(beta_cols[:, g:g + 1] for g in heads)
    gam = tuple(gam_cols[:, g:g + 1] for g in heads)
    gam_row = tuple(gam_rows[g:g + 1, :] for g in heads)
    decay = _each(lambda gam, gam_row: jnp.where(row >= col, jnp.exp(jnp.minimum(gam - gam_row, 0.0)), 0.0), gam, gam_row)
    kb = _each(lambda k, beta: k * beta, k, beta)
    a = _each(lambda kb, k, decay: jnp.where(row > col, _bdot(kb, k, "nt") * decay, 0.0), kb, k, decay)
    inv = _unit_lower_inverses(a, inv_known)
    eg = _each(jnp.exp, gam)
    u = _each(lambda inv, v, beta: _bdot(inv, v * beta, "nn"), inv, v, beta)
    w = _each(lambda inv, kb, eg: _bdot(inv, kb * eg, "nn"), inv, kb, eg)
    qk = _each(lambda q, k, decay: _bdot(q, k, "nt") * decay, q, k, decay)
    v_new = _each(lambda u, w, s: u - _bdot(w, s, "nn"), u, w, s)
    o_state = _each(lambda q, eg, s: _bdot(q * eg, s, "nn"), q, eg, s)
    o = _each(lambda o_state, qk, v_new: o_state + _bdot(qk, v_new, "nn"), o_state, qk, v_new)
    gend = _each(lambda gam: gam[n - 1:n, :], gam)
    s_new = _each(lambda s, k, gam, gend, v_new: s * jnp.exp(gend) + _bdot(k * jnp.exp(gend - gam), v_new, "tn"), s, k, gam, gend, v_new)
    return o, s_new, inv


GDN_GROUP = 16


def _gdn_specs(nc, rev):
    cc = (lambda c: nc - 1 - c) if rev else (lambda c: c)
    grp = GDN_GROUP
    q = pl.BlockSpec((GDN_CHUNK, grp // 2 * HEAD), lambda h, c: (cc(c), h))
    k = pl.BlockSpec((GDN_CHUNK, grp // 2 * HEAD), lambda h, c: (cc(c), 2 * GDN_QK_HEADS // grp + h))
    v = pl.BlockSpec((GDN_CHUNK, grp * HEAD), lambda h, c: (cc(c), 2 * GDN_QK_HEADS // grp + h))
    o = pl.BlockSpec((GDN_CHUNK, grp * HEAD), lambda h, c: (cc(c), h))
    rw = pl.BlockSpec((grp, None, 1, GDN_CHUNK), lambda h, c: (h, cc(c), 0, 0))
    st = pl.BlockSpec((grp, None, HEAD, HEAD), lambda h, c: (h, cc(c), 0, 0))
    inv = pl.BlockSpec((grp, None, GDN_CHUNK, GDN_CHUNK), lambda h, c: (h, cc(c), 0, 0))
    return q, k, v, o, rw, st, inv


def _head_lanes(g, per=1):
    return pl.ds((g // per) * HEAD, HEAD)


def _gdn_rec_fwd(qk, c, beta_row, gam_row):
    t = qk.shape[0]
    nc = t // GDN_CHUNK
    q, k, v, o, rw, st, inv = _gdn_specs(nc, False)

    def body(q_ref, k_ref, v_ref, be_ref, gr_ref, o_ref, ss_ref, inv_ref, s_ref):
        @pl.when(pl.program_id(1) == 0)
        def _():
            s_ref[...] = jnp.zeros_like(s_ref)

        heads = range(GDN_GROUP)
        s = tuple(s_ref[g] for g in heads)
        out, s_new, inv_c = _gdn_chunks(
            tuple(q_ref[:, _head_lanes(g, 2)] for g in heads), tuple(k_ref[:, _head_lanes(g, 2)] for g in heads),
            tuple(v_ref[:, _head_lanes(g)] for g in heads), be_ref[:, 0, :], gr_ref[:, 0, :], s)
        for g in heads:
            ss_ref[g] = s[g]
            o_ref[:, _head_lanes(g)] = out[g]
            inv_ref[g] = inv_c[g]
            s_ref[g] = s_new[g]

    return pl.pallas_call(
        body, grid=(GDN_V_HEADS // GDN_GROUP, nc), in_specs=[q, k, v, rw, rw], out_specs=[o, st, inv],
        out_shape=[SDS((t, 2 * D_MODEL), F32), SDS((GDN_V_HEADS, nc, HEAD, HEAD), F32), SDS((GDN_V_HEADS, nc, GDN_CHUNK, GDN_CHUNK), F32)],
        scratch_shapes=[pltpu.VMEM((GDN_GROUP, HEAD, HEAD), F32)], name="gdn_rec", compiler_params=_params(2),
    )(qk, qk, c, beta_row, gam_row)


def _gdn_rec_bwd(qk, c, beta_row, gam_row, ss, invs, do):
    t = qk.shape[0]
    nc = t // GDN_CHUNK
    q, k, v, o, rw, st, inv = _gdn_specs(nc, True)

    def body(q_ref, k_ref, v_ref, be_ref, gr_ref, ss_ref, inv_ref, do_ref,
             dq_ref, dk_ref, dv_ref, dbe_ref, dgr_ref, ds_ref):
        @pl.when(pl.program_id(1) == 0)
        def _():
            ds_ref[...] = jnp.zeros_like(ds_ref)

        heads = range(GDN_GROUP)
        _, vjp = jax.vjp(
            _gdn_chunks,
            tuple(q_ref[:, _head_lanes(g, 2)] for g in heads), tuple(k_ref[:, _head_lanes(g, 2)] for g in heads),
            tuple(v_ref[:, _head_lanes(g)] for g in heads), be_ref[:, 0, :], gr_ref[:, 0, :],
            tuple(ss_ref[g] for g in heads), tuple(inv_ref[g] for g in heads))
        no_inv_ct = tuple(jnp.zeros((GDN_CHUNK, GDN_CHUNK), F32) for g in heads)
        dq, dk, dv, dbe, dgr, ds, _ = vjp((tuple(do_ref[:, _head_lanes(g)] for g in heads), tuple(ds_ref[g] for g in heads), no_inv_ct))
        for g in heads:
            dq_ref[:, _head_lanes(g)] = dq[g]
            dk_ref[:, _head_lanes(g)] = dk[g]
            dv_ref[:, _head_lanes(g)] = dv[g]
            ds_ref[g] = ds[g]
        dbe_ref[:, 0, :] = dbe
        dgr_ref[:, 0, :] = dgr

    wide = SDS((t, 2 * D_MODEL), F32)
    rowshape = SDS((GDN_V_HEADS, nc, 1, GDN_CHUNK), F32)
    return pl.pallas_call(
        body, grid=(GDN_V_HEADS // GDN_GROUP, nc), in_specs=[q, k, v, rw, rw, st, inv, o], out_specs=[o, o, o, rw, rw],
        out_shape=[wide, wide, wide, rowshape, rowshape],
        scratch_shapes=[pltpu.VMEM((GDN_GROUP, HEAD, HEAD), F32)], name="gdn_rec_bwd", compiler_params=_params(2),
    )(qk, qk, c, beta_row, gam_row, ss, invs, do)


def _gated_norm(o, gate, w):
    return _rms(o, w) * _silu(gate)


def _post_fwd(o, proj, col_off, w, name):
    t, width = o.shape
    tt = _tile(t, (256, 128))

    def fn(o, gate, w):
        return jnp.concatenate([_gated_norm(_head(o, h), _head(gate, h), w) for h in range(width // HEAD)], axis=1)

    return _tmap(fn, (1, t // tt),
                 [(o, *_rows(width, tt)), (proj, *_rows(width, tt, col_off * HEAD // width)), (w, (1, HEAD), lambda j, i: (0, 0))],
                 [((t, width), BF16, *_rows(width, tt), None)], name)[0]


def _post_bwd(o, proj, col_off, w, dout, name):
    t, width = o.shape
    tt = _tile(t, (256, 128))

    def fn(o, gate, w, dout):
        do, dgate, dw = [], [], jnp.zeros((1, HEAD), F32)
        for h in range(width // HEAD):
            _, vjp = jax.vjp(_gated_norm, _head(o, h), _head(gate, h), w)
            a, b, c = vjp(_head(dout, h))
            do.append(a)
            dgate.append(b)
            dw = dw + c
        return jnp.concatenate(do, axis=1), jnp.concatenate(dgate, axis=1), dw

    r = _rows(width, tt)
    return _tmap(fn, (1, t // tt),
                 [(o, *r), (proj, *_rows(width, tt, col_off * HEAD // width)), (w, (1, HEAD), lambda j, i: (0, 0)), (dout, *r)],
                 [((t, width), F32, *r, None), ((t, width), BF16, *r, None), ((1, HEAD), F32, (1, HEAD), lambda j, i: (0, 0), "inner")], name)


def _merge(gate_h, gate_g, yh, yg):
    return _sigmoid(gate_h) * yh + _sigmoid(gate_g) * yg


def _merge_fwd(proj, yh, yg):
    t = yh.shape[0]
    tt, ft = _tile(t, (256, 128)), 512
    r = _rows(ft, tt)
    return _tmap(_merge, (D_MODEL // ft, t // tt),
                 [(proj, *_rows(ft, tt, COL_GATE_H * HEAD // ft)), (proj, *_rows(ft, tt, COL_GATE_G * HEAD // ft)), (yh, *r), (yg, *r)],
                 [((t, D_MODEL), BF16, *r, None)], "merge")[0]


def _merge_bwd(proj, yh, yg, dy):
    t = yh.shape[0]
    tt, ft = _tile(t, (256, 128)), 512
    r = _rows(ft, tt)

    def fn(gate_h, gate_g, yh, yg, dy):
        _, vjp = jax.vjp(_merge, gate_h, gate_g, yh, yg)
        return vjp(dy)

    o = ((t, D_MODEL), BF16, *r, None)
    return _tmap(fn, (D_MODEL // ft, t // tt),
                 [(proj, *_rows(ft, tt, COL_GATE_H * HEAD // ft)), (proj, *_rows(ft, tt, COL_GATE_G * HEAD // ft)), (yh, *r), (yg, *r), (dy, *r)],
                 [o, o, o, o], "merge_bwd")


def _loss_head(h, target, g):
    t, d = h.shape
    tt = _tile(t, (256, 128))

    def fn(h, target, g):
        def f(h, g):
            err = _rms(h, g) - target
            return 0.5 * jnp.sum(jnp.mean(err * err, axis=-1))

        loss, (dh, dg) = jax.value_and_grad(f, (0, 1))(h, g)
        return dh, dg, jnp.full((1, HEAD), loss, F32)

    return _tmap(fn, (1, t // tt), [(h, *_rows(d, tt)), (target, *_rows(d, tt)), (g, (1, d), lambda j, i: (0, 0))],
                 [((t, d), F32, *_rows(d, tt), None), ((1, d), F32, (1, d), lambda j, i: (0, 0), "inner"),
                  ((1, HEAD), F32, (1, HEAD), lambda j, i: (0, 0), "inner")], "loss_head")


def _mixer_fwd(h, p, links):
    t = h.shape[0]
    nc = t // GDN_CHUNK
    u = _rms_fwd(h, p["mix_norm"], "mix_norm")
    w = {n: links.weight(n, h) for n in ("w_in_t", "w_in_b_t", "w_in_ab_t", "conv_w")}
    proj = _mm(u, w["w_in_t"], "nt", F32, "mix_in", after=links.started, b_rows=SCALAR_ROWS)
    proj_b = _mm(u, w["w_in_b_t"], "nt", F32, "mix_in_b")
    pab = _mm(u, w["w_in_ab_t"], "nt", F32, "mix_in_ab")
    qh, kh, bh = _hgrn_prep_fwd(proj, p["lbl"])
    oh, hs = _hgrn_rec_fwd(qh, kh, proj, bh)
    c, conv_y = _conv_fwd(proj, w["conv_w"])
    qk = _qk_norm_fwd(c)
    gates_t = _gates_fwd(pab, p["alog"], p["dtb"])
    gam_row = gates_t[:GDN_V_HEADS].reshape(GDN_V_HEADS, nc, 1, GDN_CHUNK)
    beta_row = gates_t[GDN_V_HEADS:2 * GDN_V_HEADS].reshape(GDN_V_HEADS, nc, 1, GDN_CHUNK)
    og, ss, invs = _gdn_rec_fwd(qk, c, beta_row, gam_row)
    ohn = _post_fwd(oh, proj, COL_HG, p["hgrn_out_norm"], "hgrn_out")
    ogn = _post_fwd(og, proj_b, COL_GZ, p["gdn_out_norm"], "gdn_out")
    w.update({n: links.weight(n, ogn) for n in ("w_branch_hgrn", "w_branch_gdn", "w_out")})
    yh = _mm(ohn, w["w_branch_hgrn"], "nn", BF16, "branch_hgrn")
    yg = _mm(ogn, w["w_branch_gdn"], "nn", BF16, "branch_gdn")
    y = _merge_fwd(proj_b, yh, yg)
    out = _mm(y, w["w_out"], "nn", F32, "mix_out", res=h)
    saved = (w, u, proj, proj_b, pab, qh, kh, bh, oh, hs, c, conv_y, qk, beta_row, gam_row, og, ss, invs, ohn, ogn, yh, yg, y)
    return out, saved


def _mixer_bwd(h, p, links, saved, dout):
    (w, u, proj, proj_b, pab, qh, kh, bh, oh, hs, c, conv_y, qk, beta_row, gam_row, og, ss, invs, ohn, ogn, yh, yg, y) = saved
    t = h.shape[0]
    grads = {}
    dw_out = _mm(y, dout, "tn", BF16, "mix_out_dw")
    dy = _mm(dout, w["w_out"], "nt", F32, "mix_out_dx")
    dgate_h, dgate_g, dyh, dyg = _merge_bwd(proj_b, yh, yg, dy)
    dw_bh = _mm(ohn, dyh, "tn", BF16, "branch_hgrn_dw")
    dw_bg = _mm(ogn, dyg, "tn", BF16, "branch_gdn_dw")
    sent = links.send({"w_out": dw_out, "w_branch_hgrn": dw_bh, "w_branch_gdn": dw_bg})
    dohn = _mm(dyh, w["w_branch_hgrn"], "nt", F32, "branch_hgrn_dx", after=sent)
    dogn = _mm(dyg, w["w_branch_gdn"], "nt", F32, "branch_gdn_dx")
    doh, dhg, grads["hgrn_out_norm"] = _post_bwd(oh, proj, COL_HG, p["hgrn_out_norm"], dohn, "hgrn_out_bwd")
    dog, dgz, grads["gdn_out_norm"] = _post_bwd(og, proj_b, COL_GZ, p["gdn_out_norm"], dogn, "gdn_out_bwd")
    dqh, dkh, dhi, dbh = _hgrn_rec_bwd(qh, kh, proj, bh, hs, doh)
    dhq, dhf, grads["lbl"] = _hgrn_prep_bwd(proj, p["lbl"], dqh, dkh, dbh)
    dqv, dkv, dcv, dbeta_row, dgam_row = _gdn_rec_bwd(qk, c, beta_row, gam_row, ss, invs, dog)
    dcqk = _qk_norm_bwd(c, dqv, dkv)
    dxin, grads["conv_w"] = _conv_bwd(proj, w["conv_w"], conv_y, dcqk, dcv)
    dgates_t = jnp.concatenate([dgam_row.reshape(GDN_V_HEADS, t), dbeta_row.reshape(GDN_V_HEADS, t),
                                jnp.zeros((HEAD - 2 * GDN_V_HEADS, t), F32)], axis=0)
    dpab, grads["alog"], grads["dtb"] = _gates_bwd(pab, p["alog"], p["dtb"], dgates_t)
    dproj = jnp.concatenate([dhq, dhf, dhi, dhg, dxin], axis=1)
    dproj_b = jnp.concatenate([dgz, dgate_h, dgate_g], axis=1)
    dw_t = _mm(dproj, u, "tn", BF16, "mix_in_dw")
    dw_b_t = _mm(dproj_b, u, "tn", BF16, "mix_in_b_dw")
    dw_ab_t = _mm(dpab, u, "tn", BF16, "mix_in_ab_dw")
    sent = links.send({"w_in": jnp.concatenate([dw_t, dw_ab_t[:N_SCALAR], dw_b_t], axis=0)})
    du = _mm(dproj, w["w_in_t"], "nn", F32, "mix_in_dx", after=sent, b_rows=SCALAR_ROWS)
    du = _mm(dproj_b, w["w_in_b_t"], "nn", F32, "mix_in_b_dx", res=du)
    du = _mm(dpab, w["w_in_ab_t"], "nn", F32, "mix_in_ab_dx", res=du)
    dh, grads["mix_norm"] = _rms_bwd(h, p["mix_norm"], du, dout, "mix_norm_bwd")
    return dh, grads


def _local_step(x, target, p, links):
    w1 = {n: links.weight(n, x) for n in ("ffn1_w_in", "ffn1_w_out")}
    h1, s1 = _ffn_fwd(x, p["ffn1_norm"], w1["ffn1_w_in"], w1["ffn1_w_out"], "ffn1", links.started)
    h2, sm = _mixer_fwd(h1, p, links)
    w2 = {n: links.weight(n, h2) for n in ("ffn2_w_in", "ffn2_w_out")}
    h3, s2 = _ffn_fwd(h2, p["ffn2_norm"], w2["ffn2_w_in"], w2["ffn2_w_out"], "ffn2", None)
    dh3, dfinal, loss = _loss_head(h3, target, p["final_norm"])
    g = {"final_norm": dfinal}
    dh2, g["ffn2_norm"] = _ffn_bwd(h2, p["ffn2_norm"], w2["ffn2_w_in"], w2["ffn2_w_out"], s2, dh3, "ffn2", links)
    dh1, gm = _mixer_bwd(h1, p, links, sm, dh2)
    g.update(gm)
    dx, g["ffn1_norm"] = _ffn_bwd(x, p["ffn1_norm"], w1["ffn1_w_in"], w1["ffn1_w_out"], s1, dh1, "ffn1", links)
    return loss, dx, g


HBM_SPEC = pl.BlockSpec(memory_space=pltpu.HBM)
SEM_SPEC = pl.BlockSpec(memory_space=pltpu.SEMAPHORE)
DATAFLOW = pltpu.SideEffectType.DATAFLOW_SIDE_EFFECTING


def _position():
    x, y, c = lax.axis_index("x"), lax.axis_index("y"), lax.axis_index("c")
    return x, y, c, 4 * x + 2 * y + c


def _relations(x, y, c):
    for rel in range(1, N_DEV):
        px = 1 - x if rel & 4 else x
        py = 1 - y if rel & 2 else y
        pc = 1 - c if rel & 1 else c
        yield rel, (px, py, pc), 4 * px + 2 * py + pc


def _sem_index(item, rel):
    return item * (N_DEV - 1) + rel - 1


def _landing(a, mode):
    return lax.empty((N_DEV,) + a.shape if mode == "gather" else a.shape, a.dtype)


ALL_PEERS = tuple(range(1, N_DEV))
ONE_PER_CHIP = (1, 2, 4, 6)


def _copies_start(groups, name, rels=ALL_PEERS):
    flat = [item for grp in groups for item in grp]
    n, ng = len(flat), len(groups)
    lands = [_landing(a, mode) for a, mode in flat]

    def body(*refs):
        src_refs, land_refs, sems, token = refs[:n], refs[n:2 * n], refs[2 * n:2 * n + 2 * ng], refs[-1]
        x, y, c, me = _position()
        for rel, where, peer in _relations(x, y, c):
            if rel not in rels:
                continue
            k = 0
            for gi, grp in enumerate(groups):
                for li, (_, mode) in enumerate(grp):
                    src = src_refs[k] if mode == "gather" else src_refs[k].at[peer]
                    pltpu.make_async_remote_copy(src_ref=src, dst_ref=land_refs[k].at[me], send_sem=sems[2 * gi].at[_sem_index(li, rel)],
                                                 recv_sem=sems[2 * gi + 1].at[_sem_index(li, rel)], device_id=where, device_id_type=MESH_IDS).start()
                    k += 1
        token[...] = jnp.zeros_like(token)

    sem_shapes = [pltpu.SemaphoreType.DMA((len(grp) * (N_DEV - 1),)) for grp in groups for _ in range(2)]
    thru = [pltpu.HBM(a.shape, a.dtype) for a, _ in flat] + [pltpu.HBM(l.shape, l.dtype) for l in lands]
    outs = pl.pallas_call(
        body, name=name, out_shape=(*sem_shapes, *thru, SDS((8, HEAD), F32)),
        in_specs=[HBM_SPEC] * (2 * n), out_specs=(*[SEM_SPEC] * (2 * ng), *[HBM_SPEC] * (2 * n), pl.BlockSpec(memory_space=pltpu.VMEM)),
        input_output_aliases={i: 2 * ng + i for i in range(2 * n)}, compiler_params=pltpu.CompilerParams(has_side_effects=DATAFLOW),
    )(*[pltpu.with_memory_space_constraint(a, pltpu.HBM) for a, _ in flat], *[pltpu.with_memory_space_constraint(l, pltpu.HBM) for l in lands])
    sems, srcs, landed, token = outs[:2 * ng], outs[2 * ng:2 * ng + n], outs[2 * ng + n:2 * ng + 2 * n], outs[-1]
    result, k = [], 0
    for gi, grp in enumerate(groups):
        result.append((sems[2 * gi], sems[2 * gi + 1], srcs[k:k + len(grp)], landed[k:k + len(grp)]))
        k += len(grp)
    return result, token


def _copies_wait(started, modes, after, name, rels=ALL_PEERS):
    send_sems, recv_sems, srcs, lands = started
    n = len(srcs)

    def body(*refs):
        src_refs, land_refs, ssem, rsem, token = refs[:n], refs[n:2 * n], refs[2 * n], refs[2 * n + 1], refs[-1]
        x, y, c, _ = _position()
        for rel in rels:
            for i, mode in enumerate(modes):
                src = src_refs[i] if mode == "gather" else src_refs[i].at[0]
                cp = pltpu.make_async_remote_copy(src_ref=src, dst_ref=land_refs[i].at[0], send_sem=ssem.at[_sem_index(i, rel)],
                                                  recv_sem=rsem.at[_sem_index(i, rel)], device_id=(x, y, c), device_id_type=MESH_IDS)
                cp.wait_send()
                cp.wait_recv()
        token[...] = jnp.zeros_like(token)

    outs = pl.pallas_call(
        body, name=name, out_shape=[pltpu.HBM(a.shape, a.dtype) for a in (*srcs, *lands)] + [SDS((8, HEAD), F32)],
        in_specs=[HBM_SPEC] * (2 * n) + [SEM_SPEC, SEM_SPEC, pl.BlockSpec(memory_space=pl.ANY)],
        out_specs=[HBM_SPEC] * (2 * n) + [pl.BlockSpec(memory_space=pltpu.VMEM)],
        input_output_aliases={i: i for i in range(2 * n)}, compiler_params=pltpu.CompilerParams(has_side_effects=DATAFLOW),
    )(*srcs, *lands, send_sems, recv_sems, after)
    return outs[:n], outs[n:2 * n], outs[-1]


OTHER_CHIPS = ((1, 0), (0, 1), (1, 1))


def _pass_on_start(lands, name):
    n = len(lands)

    def body(*refs):
        land_refs, ssem, rsem, token = refs[:n], refs[n], refs[n + 1], refs[-1]
        x, y, c, _ = _position()
        for j, (fx, fy) in enumerate(OTHER_CHIPS):
            slot = 4 * (1 - x if fx else x) + 2 * (1 - y if fy else y) + c
            for i in range(n):
                pltpu.make_async_remote_copy(src_ref=land_refs[i].at[slot], dst_ref=land_refs[i].at[slot], send_sem=ssem.at[i * len(OTHER_CHIPS) + j],
                                             recv_sem=rsem.at[i * len(OTHER_CHIPS) + j], device_id=(x, y, 1 - c), device_id_type=MESH_IDS).start()
        token[...] = jnp.zeros_like(token)

    sems = pltpu.SemaphoreType.DMA((n * len(OTHER_CHIPS),))
    outs = pl.pallas_call(
        body, name=name, out_shape=(sems, sems, *[pltpu.HBM(l.shape, l.dtype) for l in lands], SDS(TOKEN, F32)),
        in_specs=[HBM_SPEC] * n, out_specs=(SEM_SPEC, SEM_SPEC, *[HBM_SPEC] * n, pl.BlockSpec(memory_space=pltpu.VMEM)),
        input_output_aliases={i: 2 + i for i in range(n)}, compiler_params=pltpu.CompilerParams(has_side_effects=DATAFLOW),
    )(*lands)
    return (outs[0], outs[1], outs[2:2 + n]), outs[-1]


def _pass_on_wait(started, after, name):
    send_sems, recv_sems, lands = started
    n = len(lands)

    def body(*refs):
        land_refs, ssem, rsem = refs[:n], refs[n], refs[n + 1]
        x, y, c, _ = _position()
        for j in range(len(OTHER_CHIPS)):
            for i in range(n):
                cp = pltpu.make_async_remote_copy(src_ref=land_refs[i].at[0], dst_ref=land_refs[i].at[0], send_sem=ssem.at[i * len(OTHER_CHIPS) + j],
                                                  recv_sem=rsem.at[i * len(OTHER_CHIPS) + j], device_id=(x, y, c), device_id_type=MESH_IDS)
                cp.wait_send()
                cp.wait_recv()

    return pl.pallas_call(
        body, name=name, out_shape=[pltpu.HBM(l.shape, l.dtype) for l in lands],
        in_specs=[HBM_SPEC] * n + [SEM_SPEC, SEM_SPEC, pl.BlockSpec(memory_space=pl.ANY)], out_specs=[HBM_SPEC] * n,
        input_output_aliases={i: i for i in range(n)}, compiler_params=pltpu.CompilerParams(has_side_effects=DATAFLOW),
    )(*lands, send_sems, recv_sems, after)


WEIGHT_GROUPS = (("ffn1_w_in", "ffn1_w_out", "gdn_conv_w"), ("w_in",), ("w_branch_hgrn", "w_branch_gdn", "w_out", "ffn2_w_in", "ffn2_w_out"))
GROUP_RELS = (ONE_PER_CHIP, ONE_PER_CHIP, ALL_PEERS)


class _Links:
    def __init__(self, shards, me):
        self.me = me
        self.shards = shards
        self.weights = {}
        self.sends = []
        self.gathers = {}
        self.started = None
        self._start_gather(0, None)

    def _start_gather(self, gi, zeros):
        if gi < len(WEIGHT_GROUPS):
            items = [(self.shards[n] if zeros is None else self.shards[n] + zeros[0, 0].astype(self.shards[n].dtype), "gather")
                     for n in WEIGHT_GROUPS[gi]]
            started, self.started = _copies_start([items], "gather_start_%d" % gi, GROUP_RELS[gi])
            self.gathers[gi] = started[0]

    def weight(self, name, after):
        if name not in self.weights:
            source = {"w_in_t": "w_in", "w_in_b_t": "w_in", "w_in_ab_t": "w_in", "conv_w": "gdn_conv_w"}.get(name, name)
            gi = [i for i, grp in enumerate(WEIGHT_GROUPS) if source in grp][0]
            assert gi in self.gathers, "weight groups are asked for in order"
            srcs, lands, zero = _copies_wait(self.gathers[gi], ["gather"] * len(WEIGHT_GROUPS[gi]), after, "gather_wait_%d" % gi, GROUP_RELS[gi])
            if GROUP_RELS[gi] == ONE_PER_CHIP:
                passing, zero = _pass_on_start(lands, "gather_pass_%d" % gi)
                self._start_gather(gi + 1, zero)
                lands = _pass_on_wait(passing, self.started, "gather_passed_%d" % gi)
            else:
                self._start_gather(gi + 1, zero)
            for n, src, land in zip(WEIGHT_GROUPS[gi], srcs, lands):
                full = lax.dynamic_update_index_in_dim(land, src, self.me, 0)
                if n == "gdn_conv_w":
                    self.weights["conv_w"] = full.reshape(N_DEV, CONV_K, 4 * D_MODEL // N_DEV).transpose(1, 0, 2).reshape(CONV_K, 4 * D_MODEL)
                elif n == "w_in":
                    self.weights.update(_w_in_pieces(full.reshape(-1, D_MODEL)))
                else:
                    self.weights[n] = full.reshape(-1, D_MODEL)
        return self.weights[name]

    def send(self, grads):
        names = list(grads)
        blocks = [grads[n].reshape(N_DEV, -1, D_MODEL) for n in names]
        started, token = _copies_start([[(b, "scatter") for b in blocks]], "send_" + names[0])
        self.sends.append((names, started[0]))
        return token

    def landed(self, after):
        out = {}
        for names, started in self.sends:
            srcs, lands, _ = _copies_wait(started, ["scatter"] * len(names), after, "landed_" + names[0])
            for n, src, land in zip(names, srcs, lands):
                out[n] = lax.dynamic_update_index_in_dim(land, lax.dynamic_index_in_dim(src, self.me, 0, keepdims=False), self.me, 0)
        return out


def _adam(parts, w, m, v, name):
    n_parts, r, c = parts.shape
    tc = c if c <= 512 else (256 if r > 1024 else 512)

    def body(p_ref, w_ref, m_ref, v_ref, g_ref, d_ref, mo_ref, vo_ref):
        g = p_ref[0].astype(F32)
        for i in range(1, n_parts):
            g = g + p_ref[i].astype(F32)
        m_new = ADAM_B1 * m_ref[...] + (1.0 - ADAM_B1) * g
        v_new = ADAM_B2 * v_ref[...] + (1.0 - ADAM_B2) * (g * g)
        m_hat = m_new / (1.0 - ADAM_B1 ** ADAM_STEP)
        v_hat = v_new / (1.0 - ADAM_B2 ** ADAM_STEP)
        g_ref[...] = g
        d_ref[...] = -ADAM_LR * (m_hat / (jnp.sqrt(v_hat) + ADAM_EPS) + ADAM_WD * w_ref[...])
        mo_ref[...] = m_new
        vo_ref[...] = v_new

    spec = pl.BlockSpec((r, tc), lambda j: (0, j))
    return pl.pallas_call(
        body, grid=(c // tc,), in_specs=[pl.BlockSpec((n_parts, r, tc), lambda j: (0, 0, j)), spec, spec, spec],
        out_specs=[spec] * 4, out_shape=[SDS((r, c), F32)] * 4, name=name, compiler_params=_params(1),
    )(parts, w, m, v)


BIG = ("ffn1_w_in", "ffn1_w_out", "w_in", "w_branch_hgrn", "w_branch_gdn", "w_out", "ffn2_w_in", "ffn2_w_out")


TRANSPOSED = ("ffn1_w_in", "w_in", "ffn2_w_in")


def _shard_rows(name, shard):
    return shard.T if name in TRANSPOSED else shard


SCALAR_ROWS = 8192
N_SCALAR = 2 * GDN_V_HEADS


def _w_in_pieces(w_in_t):
    return {"w_in_t": w_in_t, "w_in_b_t": w_in_t[SCALAR_ROWS + N_SCALAR:],
            "w_in_ab_t": jnp.pad(w_in_t[SCALAR_ROWS:SCALAR_ROWS + N_SCALAR], ((0, HEAD - N_SCALAR), (0, 0)))}


def _pad_lanes(a, width=HEAD):
    return jnp.pad(a, ((0, 0), (0, width - a.shape[1])))


SMALL_ROWS = 24


def _pack_small(g, loss):
    row6 = jnp.concatenate([g["hgrn_out_norm"], g["gdn_out_norm"], g["alog"], g["dtb"], loss,
                            jnp.zeros((1, D_MODEL - 5 * HEAD), F32)], axis=1)
    return jnp.concatenate([g["ffn1_norm"], g["mix_norm"], g["lbl"], g["ffn2_norm"], g["final_norm"], row6,
                            jnp.zeros((1, D_MODEL), F32), g["conv_w"].reshape(4 * CONV_K, D_MODEL)], axis=0)


def _pack_small_state(a):
    row6 = jnp.concatenate([a["hgrn_out_norm"], a["gdn_out_norm"], _pad_lanes(a["gdn_a_log"]), _pad_lanes(a["gdn_dt_bias"]),
                            jnp.zeros((1, D_MODEL - 4 * HEAD), F32)], axis=1)
    return jnp.concatenate([a["ffn1_norm"], a["mix_norm"], a["hgrn_lb_logits"], a["ffn2_norm"], a["final_norm"].reshape(1, D_MODEL),
                            row6, jnp.zeros((1, D_MODEL), F32)], axis=0)


def _unpack_small(a):
    return {"ffn1_norm": a[0:1], "mix_norm": a[1:2], "hgrn_lb_logits": a[2:4], "ffn2_norm": a[4:5], "final_norm": a[5],
            "hgrn_out_norm": a[6:7, :HEAD], "gdn_out_norm": a[6:7, HEAD:2 * HEAD],
            "gdn_a_log": a[6:7, 2 * HEAD:2 * HEAD + GDN_V_HEADS], "gdn_dt_bias": a[6:7, 3 * HEAD:3 * HEAD + GDN_V_HEADS]}


NAMES = ("ffn1_norm", "ffn1_w_in", "ffn1_w_out", "mix_norm", "w_in", "hgrn_lb_logits", "hgrn_out_norm", "gdn_conv_w", "gdn_a_log",
         "gdn_dt_bias", "gdn_out_norm", "w_branch_hgrn", "w_branch_gdn", "w_out", "ffn2_norm", "ffn2_w_in", "ffn2_w_out", "final_norm")


def kernel(x, ffn1_norm, ffn1_w_in, ffn1_w_out, mix_norm, w_in, hgrn_lb_logits, hgrn_out_norm, gdn_conv_w, gdn_a_log, gdn_dt_bias, gdn_out_norm, w_branch_hgrn, w_branch_gdn, w_out, ffn2_norm, ffn2_w_in, ffn2_w_out, final_norm, loss_target, m_ffn1_norm, m_ffn1_w_in, m_ffn1_w_out, m_mix_norm, m_w_in, m_hgrn_lb_logits, m_hgrn_out_norm, m_gdn_conv_w, m_gdn_a_log, m_gdn_dt_bias, m_gdn_out_norm, m_w_branch_hgrn, m_w_branch_gdn, m_w_out, m_ffn2_norm, m_ffn2_w_in, m_ffn2_w_out, m_final_norm, v_ffn1_norm, v_ffn1_w_in, v_ffn1_w_out, v_mix_norm, v_w_in, v_hgrn_lb_logits, v_hgrn_out_norm, v_gdn_conv_w, v_gdn_a_log, v_gdn_dt_bias, v_gdn_out_norm, v_w_branch_hgrn, v_w_branch_gdn, v_w_out, v_ffn2_norm, v_ffn2_w_in, v_ffn2_w_out, v_final_norm):
    wts = dict(zip(NAMES, (ffn1_norm, ffn1_w_in, ffn1_w_out, mix_norm, w_in, hgrn_lb_logits, hgrn_out_norm, gdn_conv_w, gdn_a_log,
                           gdn_dt_bias, gdn_out_norm, w_branch_hgrn, w_branch_gdn, w_out, ffn2_norm, ffn2_w_in, ffn2_w_out, final_norm)))
    mom = dict(zip(NAMES, (m_ffn1_norm, m_ffn1_w_in, m_ffn1_w_out, m_mix_norm, m_w_in, m_hgrn_lb_logits, m_hgrn_out_norm, m_gdn_conv_w,
                           m_gdn_a_log, m_gdn_dt_bias, m_gdn_out_norm, m_w_branch_hgrn, m_w_branch_gdn, m_w_out, m_ffn2_norm, m_ffn2_w_in,
                           m_ffn2_w_out, m_final_norm)))
    var = dict(zip(NAMES, (v_ffn1_norm, v_ffn1_w_in, v_ffn1_w_out, v_mix_norm, v_w_in, v_hgrn_lb_logits, v_hgrn_out_norm, v_gdn_conv_w,
                           v_gdn_a_log, v_gdn_dt_bias, v_gdn_out_norm, v_w_branch_hgrn, v_w_branch_gdn, v_w_out, v_ffn2_norm, v_ffn2_w_in,
                           v_ffn2_w_out, v_final_norm)))
    me = 4 * lax.axis_index("x") + 2 * lax.axis_index("y") + lax.axis_index("c")

    conv_shard = wts["gdn_conv_w"][0]
    shards = {n: _shard_rows(n, wts[n][0]).astype(BF16) for n in BIG}
    shards["gdn_conv_w"] = conv_shard.reshape(2, D_MODEL)
    links = _Links(shards, me)
    p = {"ffn1_norm": wts["ffn1_norm"], "mix_norm": wts["mix_norm"], "ffn2_norm": wts["ffn2_norm"], "final_norm": wts["final_norm"].reshape(1, D_MODEL),
         "lbl": wts["hgrn_lb_logits"], "hgrn_out_norm": wts["hgrn_out_norm"], "gdn_out_norm": wts["gdn_out_norm"],
         "alog": _pad_lanes(wts["gdn_a_log"]), "dtb": _pad_lanes(wts["gdn_dt_bias"])}

    loss, dx, g = _local_step(x[0], loss_target[0], p, links)

    small_started, small_token = _copies_start([[(_pack_small(g, loss), "gather")]], "small_start")
    landed = links.landed(small_token)

    big = [{} for _ in range(4)]
    for n in BIG:
        res = _adam(landed[n], _shard_rows(n, wts[n][0]), _shard_rows(n, mom[n][0]), _shard_rows(n, var[n][0]), "adam_" + n)
        for kind in range(4):
            big[kind][n] = _shard_rows(n, res[kind])
    small_srcs, small_lands, _ = _copies_wait(small_started[0], ["gather"], res[0], "small_wait")
    small_parts = lax.dynamic_update_index_in_dim(small_lands[0], small_srcs[0], me, 0)
    n_vec = SMALL_ROWS - 4 * CONV_K
    small_raw = _adam(small_parts[:, :n_vec], _pack_small_state(wts), _pack_small_state(mom), _pack_small_state(var), "adam_small")
    small = [_unpack_small(o) for o in small_raw]
    loss_total = small_raw[0][6, 4 * HEAD]
    conv_parts = small_parts[:, n_vec:].reshape(N_DEV, CONV_K, 4 * D_MODEL)
    width = 4 * D_MODEL // N_DEV
    conv_mine = lax.dynamic_slice_in_dim(conv_parts, me * width, width, axis=2)
    conv = _adam(conv_mine, conv_shard, mom["gdn_conv_w"][0], var["gdn_conv_w"][0], "adam_conv")

    outs = []
    for kind in range(4):
        for n in NAMES:
            if n in BIG:
                outs.append(big[kind][n][None])
            elif n == "gdn_conv_w":
                outs.append(conv[kind][None])
            else:
                outs.append(small[kind][n])
    return (loss_total, dx[None], *outs)
```

```python
import functools

import jax
import jax.numpy as jnp
from jax import lax
from jax.experimental import pallas as pl
from jax.experimental.pallas import tpu as pltpu

F32 = jnp.float32
BF16 = jnp.bfloat16
HIGHEST = lax.Precision.HIGHEST
MESH_IDS = pl.DeviceIdType.MESH

D_MODEL = 1024
D_FF = 2816
N_DEV = 8
EPS = 1e-6
HEAD = 128
HG_HEADS = 8
GDN_QK_HEADS = 8
GDN_V_HEADS = 16
GDN_CHUNK = 64
HG_CHUNK = 16
CONV_K = 4
IN_WIDTH = 12320
COL_HQ, COL_HF, COL_HI, COL_HG, COL_GQ, COL_GK, COL_GV = 0, 8, 16, 24, 32, 40, 48
COL_GZ, COL_GATE_H, COL_GATE_G = 0, 16, 24
VMEM_LIMIT = 56 * 1024 * 1024

ADAM_LR, ADAM_B1, ADAM_B2, ADAM_EPS, ADAM_WD, ADAM_STEP = 0.001, 0.9, 0.999, 1e-08, 0.01, 10

SDS = jax.ShapeDtypeStruct


def _params(n_axes):
    return pltpu.CompilerParams(dimension_semantics=("arbitrary",) * n_axes, vmem_limit_bytes=VMEM_LIMIT)


def _tile(n, candidates=(512, 384, 256, 128, 64, 32, 16, 8)):
    for c in candidates:
        if n % c == 0:
            return c
    return n


_DIMS = {"nn": ((1,), (0,)), "nt": ((1,), (1,)), "tn": ((0,), (0,))}


def _bdot_raw(a, b, dims):
    return lax.dot_general(a.astype(BF16), b.astype(BF16), (_DIMS[dims], ((), ())), preferred_element_type=F32)


@functools.partial(jax.custom_vjp, nondiff_argnums=(2,))
def _bdot(a, b, dims):
    return _bdot_raw(a, b, dims)


def _bdot_fwd(a, b, dims):
    return _bdot_raw(a, b, dims), (a, b)


def _bdot_bwd(dims, res, ct):
    a, b = res
    if dims == "nn":
        return _bdot_raw(ct, b, "nt"), _bdot_raw(a, ct, "tn")
    if dims == "nt":
        return _bdot_raw(ct, b, "nn"), _bdot_raw(ct, a, "tn")
    return _bdot_raw(b, ct, "nt"), _bdot_raw(a, ct, "nn")


_bdot.defvjp(_bdot_fwd, _bdot_bwd)


def _hdot_raw(a, b):
    return jnp.dot(a, b, precision=HIGHEST, preferred_element_type=F32)


MM_VMEM_BUDGET = 38 * 1024 * 1024
TOKEN = (8, HEAD)


def _mm_tiles(m, n, k, a_bytes, b_bytes, o_bytes, r_bytes, m_align=8):
    def need(tm, tn, tk):
        return 2 * (tm * tk * a_bytes + tk * tn * b_bytes + tm * tn * (o_bytes + r_bytes)) + (tm * tn * 4 if tk < k else 0)

    def shrink(tm, tn, tk, floor_m, floor_n):
        while need(tm, tn, tk) > MM_VMEM_BUDGET:
            if tn > floor_n and tn % 256 == 0 and tn >= tm:
                tn //= 2
            elif tm > floor_m and tm % (2 * m_align) == 0:
                tm //= 2
            elif tn > floor_n and tn % 256 == 0:
                tn //= 2
            else:
                return None
        return tm, tn, tk

    tm = _tile(m, (1408, 1024, 704, 512, 256, 128, 64, 32, 16, 8))
    tn = _tile(n, (1408, 1024, 512, 256, 128))
    whole = shrink(tm, tn, k, min(tm, 1024), min(tn, 512))
    if whole is not None:
        return whole
    tk = _tile(k, (2048, 1408, 1024, 512, 256, 128, 64, 32, 16, 8))
    while True:
        fit = shrink(tm, tn, tk, min(tm, 256), min(tn, 512))
        if fit is not None or tk <= 512 or tk % 256:
            return fit if fit is not None else (tm, tn, tk)
        tk //= 2


def _mm(a, b, dims, out_dtype, name, res=None, alpha=1.0, after=None, b_rows=None):
    b_shape = b.shape if b_rows is None else (b_rows, b.shape[1])
    if dims == "nn":
        (m, k), (k2, n) = a.shape, b_shape
    elif dims == "nt":
        (m, k), (n, k2) = a.shape, b_shape
    else:
        (k, m), (k2, n) = a.shape, b_shape
    assert k == k2, (a.shape, b.shape, dims)
    has_res = res is not None
    tm, tn, tk = _mm_tiles(m, n, k, a.dtype.itemsize, b.dtype.itemsize, jnp.dtype(out_dtype).itemsize, res.dtype.itemsize if has_res else 0,
                           m_align=HEAD if dims == "tn" else 8)
    nk = k // tk
    a_spec = pl.BlockSpec((tk, tm), lambda i, j, kk: (kk, i)) if dims == "tn" else pl.BlockSpec((tm, tk), lambda i, j, kk: (i, kk))
    b_spec = pl.BlockSpec((tn, tk), lambda i, j, kk: (j, kk)) if dims == "nt" else pl.BlockSpec((tk, tn), lambda i, j, kk: (kk, j))
    o_spec = pl.BlockSpec((tm, tn), lambda i, j, kk: (i, j))

    def finish(acc, r_ref, o_ref):
        out = acc * alpha if alpha != 1.0 else acc
        if has_res:
            out = r_ref[...].astype(F32) + out
        o_ref[...] = out.astype(o_ref.dtype)

    n_in = 2 + has_res + (after is not None)

    def body(*refs):
        a_ref, b_ref = refs[:2]
        r_ref = refs[2] if has_res else None
        o_ref = refs[n_in]
        p = _bdot_raw(a_ref[...], b_ref[...], dims)
        if nk == 1:
            finish(p, r_ref, o_ref)
            return
        acc_ref = refs[-1]
        kk = pl.program_id(2)

        @pl.when(kk == 0)
        def _():
            acc_ref[...] = p

        @pl.when(kk > 0)
        def _():
            acc_ref[...] += p

        @pl.when(kk == nk - 1)
        def _():
            finish(acc_ref[...], r_ref, o_ref)

    args = (a, b) + ((res,) if has_res else ()) + ((after,) if after is not None else ())
    in_specs = [a_spec, b_spec] + ([o_spec] if has_res else []) + ([pl.BlockSpec(TOKEN, lambda i, j, kk: (0, 0))] if after is not None else [])
    return pl.pallas_call(
        body, grid=(m // tm, n // tn, nk), in_specs=in_specs, out_specs=o_spec, out_shape=SDS((m, n), out_dtype),
        scratch_shapes=[pltpu.VMEM((tm, tn), F32)] if nk > 1 else [], name=name, compiler_params=_params(3),
    )(*args)


def _tmap(fn, grid, ins, outs, name):
    n_in = len(ins)
    n_ax = len(grid)

    def body(*refs):
        vals = fn(*[r[...] for r in refs[:n_in]])
        if not isinstance(vals, (tuple, list)):
            vals = (vals,)
        first_inner = pl.program_id(n_ax - 1) == 0
        first_all = first_inner
        for ax in range(n_ax - 1):
            first_all = jnp.logical_and(first_all, pl.program_id(ax) == 0)

        def put(ref, val, acc):
            val = val.astype(ref.dtype)
            if acc is None:
                ref[...] = val
                return
            first = first_inner if acc == "inner" else first_all

            @pl.when(first)
            def _():
                ref[...] = val

            @pl.when(jnp.logical_not(first))
            def _():
                ref[...] += val

        for ref, val, o in zip(refs[n_in:], vals, outs):
            put(ref, val, o[4])

    return pl.pallas_call(
        body, grid=grid,
        in_specs=[pl.BlockSpec(bs, im) for _, bs, im in ins],
        out_specs=[pl.BlockSpec(o[2], o[3]) for o in outs],
        out_shape=[SDS(o[0], o[1]) for o in outs],
        name=name, compiler_params=_params(n_ax),
    )(*[a for a, _, _ in ins])


def _rows(width, tt, off=0):
    return (tt, width), (lambda j, i: (i, off + j))


def _rms(x, g):
    x = x.astype(F32)
    return x * lax.rsqrt(jnp.mean(x * x, axis=-1, keepdims=True) + EPS) * g


def _sigmoid(x):
    return jax.nn.sigmoid(x)


def _silu(x):
    return x * _sigmoid(x)


def _softplus(x):
    return jnp.maximum(x, 0.0) + jnp.log1p(jnp.exp(-jnp.abs(x)))


def _rms_fwd(x, g, name):
    t, d = x.shape
    tt = _tile(t, (256, 128))
    return _tmap(_rms, (1, t // tt), [(x, *_rows(d, tt)), (g, (1, d), lambda j, i: (0, 0))],
                 [((t, d), BF16, *_rows(d, tt), None)], name)[0]


def _rms_bwd(x, g, dn, dres, name):
    t, d = x.shape
    tt = _tile(t, (256, 128))

    def fn(x, g, dn, dres):
        _, vjp = jax.vjp(_rms, x, g)
        dx, dg = vjp(dn.astype(F32))
        return dres + dx, dg

    return _tmap(fn, (1, t // tt),
                 [(x, *_rows(d, tt)), (g, (1, d), lambda j, i: (0, 0)), (dn, *_rows(d, tt)), (dres, *_rows(d, tt))],
                 [((t, d), F32, *_rows(d, tt), None), ((1, d), F32, (1, d), lambda j, i: (0, 0), "inner")], name)


def _swiglu(ab):
    return _silu(ab[:, :D_FF].astype(F32)) * ab[:, D_FF:].astype(F32)


def _swiglu_fwd(ab, name):
    t = ab.shape[0]
    tt = _tile(t, (128,))
    return _tmap(_swiglu, (1, t // tt), [(ab, *_rows(2 * D_FF, tt))], [((t, D_FF), BF16, *_rows(D_FF, tt), None)], name)[0]


def _swiglu_bwd(ab, ds, name):
    t = ab.shape[0]
    tt = _tile(t, (128,))

    def fn(ab, ds):
        a, b = ab[:, :D_FF].astype(F32), ab[:, D_FF:].astype(F32)
        _, vjp = jax.vjp(lambda a, b: _silu(a) * b, a, b)
        da, db = vjp(ds.astype(F32))
        return jnp.concatenate([da, db], axis=1)

    return _tmap(fn, (1, t // tt), [(ab, *_rows(2 * D_FF, tt)), (ds, *_rows(D_FF, tt))],
                 [((t, 2 * D_FF), BF16, *_rows(2 * D_FF, tt), None)], name)[0]


def _ffn_fwd(h, g, weights, tag):
    n = _rms_fwd(h, g, tag + "_norm")
    w_in_t, w_out, after = weights(n)
    ab = _mm(n, w_in_t, "nt", BF16, tag + "_in", after=after)
    s = _swiglu_fwd(ab, tag + "_act")
    out = _mm(s, w_out, "nn", F32, tag + "_out", res=h, alpha=0.5)
    return out, (n, ab, s, w_in_t, w_out)


def _ffn_bwd(h, g, saved, dout, tag, links):
    n, ab, s, w_in_t, w_out = saved
    sent = links.send({tag + "_w_out": _mm(s, dout, "tn", BF16, tag + "_dw_out", alpha=0.5)})
    ds = _mm(dout, w_out, "nt", BF16, tag + "_ds", alpha=0.5, after=sent)
    dab = _swiglu_bwd(ab, ds, tag + "_dact")
    sent = links.send({tag + "_w_in": _mm(dab, n, "tn", BF16, tag + "_dw_in")})
    dn = _mm(dab, w_in_t, "nn", F32, tag + "_dn", after=sent)
    return _rms_bwd(h, g, dn, dout, tag + "_dnorm")


def _chunk_sum_matrix(n, chunk, transpose=False):
    row = lax.broadcasted_iota(jnp.int32, (n, n), 0)
    col = lax.broadcasted_iota(jnp.int32, (n, n), 1)
    if transpose:
        row, col = col, row
    return jnp.where(jnp.logical_and(col <= row, row // chunk == col // chunk), 1.0, 0.0).astype(F32)


def _hgrn_gates(hq, hf, lbl):
    lb = _sigmoid(lbl[0:1, :] - lbl[1:2, :])
    sg = _sigmoid(hf)
    f = lb + (1.0 - lb) * sg
    q = _silu(hq) * HEAD ** -0.5
    k = (1.0 - lb) * (1.0 - sg)
    return q, k, jnp.log(f)


def _hgrn_prep_fwd(proj, lbl):
    t = proj.shape[0]
    tt, ft = _tile(t, (256, 128)), 512

    def fn(hq, hf, lbl):
        q, k, log_f = _hgrn_gates(hq, hf, lbl)
        return q, k, _hdot_raw(_chunk_sum_matrix(tt, HG_CHUNK), log_f)

    o = ((t, D_MODEL), F32, *_rows(ft, tt), None)
    return _tmap(fn, (D_MODEL // ft, t // tt),
                 [(proj, *_rows(ft, tt, COL_HQ * HEAD // ft)), (proj, *_rows(ft, tt, COL_HF * HEAD // ft)), (lbl, (2, ft), lambda j, i: (0, j))],
                 [o, o, o], "hgrn_prep")


def _hgrn_prep_bwd(proj, lbl, dq, dk, db):
    t = proj.shape[0]
    tt, ft = _tile(t, (256, 128)), 512

    def fn(hq, hf, lbl, dq, dk, db):
        dlog_f = _hdot_raw(_chunk_sum_matrix(tt, HG_CHUNK, transpose=True), db)
        _, vjp = jax.vjp(_hgrn_gates, hq, hf, lbl)
        return vjp((dq, dk, dlog_f))

    o = ((t, D_MODEL), BF16, *_rows(ft, tt), None)
    r = _rows(ft, tt)
    return _tmap(fn, (D_MODEL // ft, t // tt),
                 [(proj, *_rows(ft, tt, COL_HQ * HEAD // ft)), (proj, *_rows(ft, tt, COL_HF * HEAD // ft)), (lbl, (2, ft), lambda j, i: (0, j)),
                  (dq, *r), (dk, *r), (db, *r)],
                 [o, o, ((2, D_MODEL), F32, (2, ft), lambda j, i: (0, j), "inner")], "hgrn_prep_bwd")


def _hgrn_chunks(q, k, v, b, st):
    n = q[0].shape[0]
    half = n // 2
    srow = lax.broadcasted_iota(jnp.int32, (half, HEAD), 0)
    inter = _each(lambda q, b, st: _bdot(q * jnp.exp(b), st, "nt"), q, b, st)

    def below_scores(q, k, b):
        ref = b[half:half + 1, :]
        return _bdot(q[half:] * jnp.exp(jnp.minimum(b[half:] - ref, 0.0)), k[:half] * jnp.exp(jnp.minimum(ref - b[:half], 0.0)), "nt")

    below = _each(lambda a, v: _bdot(a, v[:half], "nn"), _each(below_scores, q, k, b), v)

    def diagonal(q, k, v, b):
        rows = []
        for lo in (0, half):
            qb, kb, vb, bb = (a[lo:lo + half] for a in (q, k, v, b))
            for t in range(half):
                e = jnp.where(srow <= t, jnp.exp(jnp.minimum(bb[t:t + 1, :] - bb, 0.0)), 0.0)
                a = jnp.sum(qb[t:t + 1, :] * kb * e, axis=1, keepdims=True)
                rows.append(jnp.sum(a * vb, axis=0, keepdims=True))
        return jnp.concatenate(rows, axis=0)

    diag = _each(diagonal, q, k, v, b)
    o = _each(lambda inter, diag, below: inter + diag + jnp.concatenate([jnp.zeros_like(below), below], axis=0), inter, diag, below)

    def new_state(k, v, b, st):
        bend = b[n - 1:n, :]
        return st * jnp.exp(bend) + _bdot(v, k * jnp.exp(bend - b), "tn")

    return o, _each(new_state, k, v, b, st)


HG_GROUP = 4
HG_PER = GDN_CHUNK // HG_CHUNK


def _hgrn_rec_fwd(q, k, proj, b):
    t = q.shape[0]
    nc = t // GDN_CHUNK
    blk = (GDN_CHUNK, HG_GROUP * HEAD)
    im = lambda h, c: (c, h)

    def body(q_ref, k_ref, v_ref, b_ref, o_ref, hs_ref, st_ref):
        @pl.when(pl.program_id(1) == 0)
        def _():
            st_ref[...] = jnp.zeros_like(st_ref)

        heads = range(HG_GROUP)
        for j in range(HG_PER):
            sl = pl.ds(HG_CHUNK * j, HG_CHUNK)
            st = tuple(st_ref[g] for g in heads)
            o, st_new = _hgrn_chunks(*[tuple(r[sl, _head_lanes(g)] for g in heads) for r in (q_ref, k_ref, v_ref, b_ref)], st)
            for g in heads:
                hs_ref[g, j] = st[g]
                o_ref[sl, _head_lanes(g)] = o[g]
                st_ref[g] = st_new[g]

    return pl.pallas_call(
        body, grid=(HG_HEADS // HG_GROUP, nc),
        in_specs=[pl.BlockSpec(blk, im), pl.BlockSpec(blk, im), pl.BlockSpec(blk, lambda h, c: (c, COL_HI // HG_GROUP + h)), pl.BlockSpec(blk, im)],
        out_specs=[pl.BlockSpec(blk, im), pl.BlockSpec((HG_GROUP, HG_PER, HEAD, HEAD), lambda h, c: (h, c, 0, 0))],
        out_shape=[SDS((t, D_MODEL), F32), SDS((HG_HEADS, nc * HG_PER, HEAD, HEAD), F32)],
        scratch_shapes=[pltpu.VMEM((HG_GROUP, HEAD, HEAD), F32)], name="hgrn_rec", compiler_params=_params(2),
    )(q, k, proj, b)


def _hgrn_rec_bwd(q, k, proj, b, hs, do):
    t = q.shape[0]
    nc = t // GDN_CHUNK
    blk = (GDN_CHUNK, HG_GROUP * HEAD)
    im = lambda h, c: (nc - 1 - c, h)

    def body(q_ref, k_ref, v_ref, b_ref, hs_ref, do_ref, dq_ref, dk_ref, dv_ref, db_ref, dst_ref):
        @pl.when(pl.program_id(1) == 0)
        def _():
            dst_ref[...] = jnp.zeros_like(dst_ref)

        heads = range(HG_GROUP)
        for j in reversed(range(HG_PER)):
            sl = pl.ds(HG_CHUNK * j, HG_CHUNK)
            _, vjp = jax.vjp(_hgrn_chunks, *[tuple(r[sl, _head_lanes(g)] for g in heads) for r in (q_ref, k_ref, v_ref, b_ref)],
                             tuple(hs_ref[g, j] for g in heads))
            dq, dk, dv, db, dst = vjp((tuple(do_ref[sl, _head_lanes(g)] for g in heads), tuple(dst_ref[g] for g in heads)))
            for g in heads:
                ln = _head_lanes(g)
                dq_ref[sl, ln] = dq[g]
                dk_ref[sl, ln] = dk[g]
                dv_ref[sl, ln] = dv[g].astype(dv_ref.dtype)
                db_ref[sl, ln] = db[g]
                dst_ref[g] = dst[g]

    spec = pl.BlockSpec(blk, im)
    return pl.pallas_call(
        body, grid=(HG_HEADS // HG_GROUP, nc),
        in_specs=[spec, spec, pl.BlockSpec(blk, lambda h, c: (nc - 1 - c, COL_HI // HG_GROUP + h)), spec,
                  pl.BlockSpec((HG_GROUP, HG_PER, HEAD, HEAD), lambda h, c: (h, nc - 1 - c, 0, 0)), spec],
        out_specs=[spec, spec, spec, spec],
        out_shape=[SDS((t, D_MODEL), F32), SDS((t, D_MODEL), F32), SDS((t, D_MODEL), BF16), SDS((t, D_MODEL), F32)],
        scratch_shapes=[pltpu.VMEM((HG_GROUP, HEAD, HEAD), F32)], name="hgrn_rec_bwd", compiler_params=_params(2),
    )(q, k, proj, b, hs, do)


def _shift_down(x, d):
    if d == 0:
        return x
    row = lax.broadcasted_iota(jnp.int32, x.shape, 0)
    return jnp.where(row >= d, pltpu.roll(x, d, 0), 0.0)


def _shift_up(x, d):
    if d == 0:
        return x
    n = x.shape[0]
    row = lax.broadcasted_iota(jnp.int32, x.shape, 0)
    return jnp.where(row < n - d, pltpu.roll(x, n - d, 0), 0.0)


def _conv_fwd(proj, conv_w):
    t = proj.shape[0]
    width = 2 * D_MODEL + 2 * D_MODEL

    def body(x_ref, w_ref, c_ref, y_ref):
        x, w = x_ref[...], w_ref[...]
        y = w[CONV_K - 1:CONV_K, :] * x
        for j in range(CONV_K - 1):
            y = y + w[j:j + 1, :] * _shift_down(x, CONV_K - 1 - j)
        y_ref[...] = y
        c_ref[...] = _silu(y)

    out = pl.BlockSpec((t, HEAD), lambda j: (0, j))
    return pl.pallas_call(
        body, grid=(width // HEAD,),
        in_specs=[pl.BlockSpec((t, HEAD), lambda j: (0, COL_GQ + j)), pl.BlockSpec((CONV_K, HEAD), lambda j: (0, j))],
        out_specs=[out, out], out_shape=[SDS((t, width), F32), SDS((t, width), F32)],
        name="gdn_conv", compiler_params=_params(1),
    )(proj, conv_w)


def _conv_bwd(proj, conv_w, y, dc_qk, dc_v):
    t = proj.shape[0]
    n_qk = dc_qk.shape[1] // HEAD
    width = dc_qk.shape[1] + dc_v.shape[1]

    def body(x_ref, w_ref, y_ref, dqk_ref, dv_ref, dx_ref, dw_ref):
        x, w, y = x_ref[...], w_ref[...], y_ref[...]
        sg = _sigmoid(y)
        dc = jnp.where(pl.program_id(0) < n_qk, dqk_ref[...], dv_ref[...])
        dy = dc * (sg * (1.0 + y * (1.0 - sg)))
        ahead = [_shift_up(dy, CONV_K - 1 - j) for j in range(CONV_K)]
        dx = w[0:1, :] * ahead[0]
        for j in range(1, CONV_K):
            dx = dx + w[j:j + 1, :] * ahead[j]
        dx_ref[...] = dx.astype(dx_ref.dtype)
        dw_ref[...] = jnp.concatenate([jnp.sum(x * ahead[j], axis=0, keepdims=True) for j in range(CONV_K)], axis=0)

    blk = pl.BlockSpec((t, HEAD), lambda j: (0, j))
    return pl.pallas_call(
        body, grid=(width // HEAD,),
        in_specs=[pl.BlockSpec((t, HEAD), lambda j: (0, COL_GQ + j)), pl.BlockSpec((CONV_K, HEAD), lambda j: (0, j)), blk,
                  pl.BlockSpec((t, HEAD), lambda j: (0, jnp.minimum(j, n_qk - 1))), pl.BlockSpec((t, HEAD), lambda j: (0, jnp.maximum(j - n_qk, 0)))],
        out_specs=[blk, pl.BlockSpec((CONV_K, HEAD), lambda j: (0, j))],
        out_shape=[SDS((t, width), BF16), SDS((CONV_K, width), F32)],
        name="gdn_conv_bwd", compiler_params=_params(1),
    )(proj, conv_w, y, dc_qk, dc_v)


def _l2norm(x, scale):
    return x * lax.rsqrt(jnp.sum(x * x, axis=-1, keepdims=True) + EPS) * scale


def _head(a, h):
    return a[:, h * HEAD:(h + 1) * HEAD]


def _qk_scale(h):
    return HEAD ** -0.5 if h < GDN_QK_HEADS else 1.0


def _qk_norm_fwd(c):
    t = c.shape[0]
    tt = _tile(t, (256, 128))
    width = 2 * D_MODEL

    def fn(x):
        return jnp.concatenate([_l2norm(_head(x, h), _qk_scale(h)) for h in range(2 * GDN_QK_HEADS)], axis=1)

    return _tmap(fn, (1, t // tt), [(c, *_rows(width, tt))], [((t, width), F32, *_rows(width, tt), None)], "gdn_qk_norm")[0]


def _qk_norm_bwd(c, dq_rep, dk_rep):
    t = c.shape[0]
    tt = _tile(t, (256, 128))
    width = 2 * D_MODEL

    def fn(x, dq2, dk2):
        out = []
        for h in range(2 * GDN_QK_HEADS):
            d2, hh = (dq2, h) if h < GDN_QK_HEADS else (dk2, h - GDN_QK_HEADS)
            _, vjp = jax.vjp(lambda x: _l2norm(x, _qk_scale(h)), _head(x, h))
            out.append(vjp(_head(d2, 2 * hh) + _head(d2, 2 * hh + 1))[0])
        return jnp.concatenate(out, axis=1)

    r = _rows(width, tt)
    return _tmap(fn, (1, t // tt), [(c, *r), (dq_rep, *r), (dk_rep, *r)], [((t, width), F32, *r, None)], "gdn_qk_norm_bwd")[0]


def _gdn_gates(x, alog, dtb):
    return -jnp.exp(alog) * _softplus(x + dtb), _sigmoid(x)


def _gates_fwd(pab, alog, dtb):
    t = pab.shape[0]
    tt = _tile(t, (256, 128))

    def fn(x, alog, dtb):
        g, beta = _gdn_gates(x, alog, dtb)
        lane = lax.broadcasted_iota(jnp.int32, g.shape, 1)
        return jnp.where(lane < GDN_V_HEADS, _hdot_raw(_chunk_sum_matrix(tt, GDN_CHUNK), g), beta).T

    p = (alog, (1, HEAD), lambda j, i: (0, 0)), (dtb, (1, HEAD), lambda j, i: (0, 0))
    return _tmap(fn, (1, t // tt), [(pab, *_rows(HEAD, tt)), *p], [((HEAD, t), F32, (HEAD, tt), lambda j, i: (0, i), None)], "gdn_gates")[0]


def _gates_bwd(pab, alog, dtb, dout_t):
    t = pab.shape[0]
    tt = _tile(t, (256, 128))

    def fn(x, alog, dtb, dout_t):
        dout = dout_t.T
        lane = lax.broadcasted_iota(jnp.int32, dout.shape, 1)
        dgam = jnp.where(lane < GDN_V_HEADS, dout, 0.0)
        dbeta = jnp.where(jnp.logical_and(lane >= GDN_V_HEADS, lane < 2 * GDN_V_HEADS), dout, 0.0)
        dg = _hdot_raw(_chunk_sum_matrix(tt, GDN_CHUNK, transpose=True), dgam)
        _, vjp = jax.vjp(_gdn_gates, x, alog, dtb)
        return vjp((dg, dbeta))

    p = (alog, (1, HEAD), lambda j, i: (0, 0)), (dtb, (1, HEAD), lambda j, i: (0, 0))
    acc = ((1, HEAD), F32, (1, HEAD), lambda j, i: (0, 0), "inner")
    return _tmap(fn, (1, t // tt), [(pab, *_rows(HEAD, tt)), *p, (dout_t, (HEAD, tt), lambda j, i: (0, i))],
                 [((t, HEAD), BF16, *_rows(HEAD, tt), None), acc, acc], "gdn_gates_bwd")


def _split_bf16(x):
    hi = x.astype(BF16)
    return hi, (x - hi.astype(F32)).astype(BF16)


def _dot3(a, b):
    (ah, al), (bh, bl) = a, b
    return _bdot_raw(ah, bh, "nn") + (_bdot_raw(ah, bl, "nn") + _bdot_raw(al, bh, "nn"))


def _each(fn, *lists):
    return tuple(fn(*xs) for xs in zip(*lists))


def _unit_lower_inverses_raw(a):
    n = a[0].shape[0]
    row = lax.broadcasted_iota(jnp.int32, (n, n), 0)
    col = lax.broadcasted_iota(jnp.int32, (n, n), 1)
    eye = jnp.where(row == col, 1.0, 0.0).astype(F32)
    p = _each(lambda a: eye - a, a)
    x = _each(_split_bf16, a)
    m = 2
    while m < 2 * n:
        x = _each(_split_bf16, _each(_dot3, x, x))
        p = _each(lambda p, x: p + _bdot_raw(p, x[0], "nn"), p, x)
        m *= 2
    return p


@jax.custom_vjp
def _unit_lower_inverses(a, known):
    return _unit_lower_inverses_raw(a) if known is None else known


def _uli_fwd(a, known):
    inv = _unit_lower_inverses(a, known)
    return inv, (inv, known)


def _uli_bwd(res, ct):
    inv, known = res
    right = _each(lambda ct, inv: _bdot_raw(ct, inv, "nt"), ct, inv)
    da = _each(lambda inv, r: -_bdot_raw(inv, r, "tn"), inv, right)
    return da, (None if known is None else _each(jnp.zeros_like, known))


_unit_lower_inverses.defvjp(_uli_fwd, _uli_bwd)


def _gdn_chunks(q, k, v, beta_rows, gam_rows, s, inv_known=None):
    n = q[0].shape[0]
    heads = range(len(q))
    row = lax.broadcasted_iota(jnp.int32, (n, n), 0)
    col = lax.broadcasted_iota(jnp.int32, (n, n), 1)
    beta_cols, gam_cols = beta_rows.T, gam_rows.T
    beta = tuple(beta_cols[:, g:g + 1] for g in heads)
    gam = tuple(gam_cols[:, g:g + 1] for g in heads)
    gam_row = tuple(gam_rows[g:g + 1, :] for g in heads)
    decay = _each(lambda gam, gam_row: jnp.where(row >= col, jnp.exp(jnp.minimum(gam - gam_row, 0.0)), 0.0), gam, gam_row)
    kb = _each(lambda k, beta: k * beta, k, beta)
    a = _each(lambda kb, k, decay: jnp.where(row > col, _bdot(kb, k, "nt") * decay, 0.0), kb, k, decay)
    inv = _unit_lower_inverses(a, inv_known)
    eg = _each(jnp.exp, gam)
    u = _each(lambda inv, v, beta: _bdot(inv, v * beta, "nn"), inv, v, beta)
    w = _each(lambda inv, kb, eg: _bdot(inv, kb * eg, "nn"), inv, kb, eg)
    qk = _each(lambda q, k, decay: _bdot(q, k, "nt") * decay, q, k, decay)
    v_new = _each(lambda u, w, s: u - _bdot(w, s, "nn"), u, w, s)
    o_state = _each(lambda q, eg, s: _bdot(q * eg, s, "nn"), q, eg, s)
    o = _each(lambda o_state, qk, v_new: o_state + _bdot(qk, v_new, "nn"), o_state, qk, v_new)
    gend = _each(lambda gam: gam[n - 1:n, :], gam)
    s_new = _each(lambda s, k, gam, gend, v_new: s * jnp.exp(gend) + _bdot(k * jnp.exp(gend - gam), v_new, "tn"), s, k, gam, gend, v_new)
    return o, s_new, inv


GDN_GROUP = 16


def _gdn_specs(nc, rev):
    cc = (lambda c: nc - 1 - c) if rev else (lambda c: c)
    grp = GDN_GROUP
    q = pl.BlockSpec((GDN_CHUNK, grp // 2 * HEAD), lambda h, c: (cc(c), h))
    k = pl.BlockSpec((GDN_CHUNK, grp // 2 * HEAD), lambda h, c: (cc(c), 2 * GDN_QK_HEADS // grp + h))
    v = pl.BlockSpec((GDN_CHUNK, grp * HEAD), lambda h, c: (cc(c), 2 * GDN_QK_HEADS // grp + h))
    o = pl.BlockSpec((GDN_CHUNK, grp * HEAD), lambda h, c: (cc(c), h))
    rw = pl.BlockSpec((grp, None, 1, GDN_CHUNK), lambda h, c: (h, cc(c), 0, 0))
    st = pl.BlockSpec((grp, None, HEAD, HEAD), lambda h, c: (h, cc(c), 0, 0))
    inv = pl.BlockSpec((grp, None, GDN_CHUNK, GDN_CHUNK), lambda h, c: (h, cc(c), 0, 0))
    return q, k, v, o, rw, st, inv


def _head_lanes(g, per=1):
    return pl.ds((g // per) * HEAD, HEAD)


def _gdn_rec_fwd(qk, c, beta_row, gam_row):
    t = qk.shape[0]
    nc = t // GDN_CHUNK
    q, k, v, o, rw, st, inv = _gdn_specs(nc, False)

    def body(q_ref, k_ref, v_ref, be_ref, gr_ref, o_ref, ss_ref, inv_ref, s_ref):
        @pl.when(pl.program_id(1) == 0)
        def _():
            s_ref[...] = jnp.zeros_like(s_ref)

        heads = range(GDN_GROUP)
        s = tuple(s_ref[g] for g in heads)
        out, s_new, inv_c = _gdn_chunks(
            tuple(q_ref[:, _head_lanes(g, 2)] for g in heads), tuple(k_ref[:, _head_lanes(g, 2)] for g in heads),
            tuple(v_ref[:, _head_lanes(g)] for g in heads), be_ref[:, 0, :], gr_ref[:, 0, :], s)
        for g in heads:
            ss_ref[g] = s[g]
            o_ref[:, _head_lanes(g)] = out[g]
            inv_ref[g] = inv_c[g]
            s_ref[g] = s_new[g]

    return pl.pallas_call(
        body, grid=(GDN_V_HEADS // GDN_GROUP, nc), in_specs=[q, k, v, rw, rw], out_specs=[o, st, inv],
        out_shape=[SDS((t, 2 * D_MODEL), F32), SDS((GDN_V_HEADS, nc, HEAD, HEAD), F32), SDS((GDN_V_HEADS, nc, GDN_CHUNK, GDN_CHUNK), F32)],
        scratch_shapes=[pltpu.VMEM((GDN_GROUP, HEAD, HEAD), F32)], name="gdn_rec", compiler_params=_params(2),
    )(qk, qk, c, beta_row, gam_row)


def _gdn_rec_bwd(qk, c, beta_row, gam_row, ss, invs, do):
    t = qk.shape[0]
    nc = t // GDN_CHUNK
    q, k, v, o, rw, st, inv = _gdn_specs(nc, True)

    def body(q_ref, k_ref, v_ref, be_ref, gr_ref, ss_ref, inv_ref, do_ref,
             dq_ref, dk_ref, dv_ref, dbe_ref, dgr_ref, ds_ref):
        @pl.when(pl.program_id(1) == 0)
        def _():
            ds_ref[...] = jnp.zeros_like(ds_ref)

        heads = range(GDN_GROUP)
        _, vjp = jax.vjp(
            _gdn_chunks,
            tuple(q_ref[:, _head_lanes(g, 2)] for g in heads), tuple(k_ref[:, _head_lanes(g, 2)] for g in heads),
            tuple(v_ref[:, _head_lanes(g)] for g in heads), be_ref[:, 0, :], gr_ref[:, 0, :],
            tuple(ss_ref[g] for g in heads), tuple(inv_ref[g] for g in heads))
        no_inv_ct = tuple(jnp.zeros((GDN_CHUNK, GDN_CHUNK), F32) for g in heads)
        dq, dk, dv, dbe, dgr, ds, _ = vjp((tuple(do_ref[:, _head_lanes(g)] for g in heads), tuple(ds_ref[g] for g in heads), no_inv_ct))
        for g in heads:
            dq_ref[:, _head_lanes(g)] = dq[g]
            dk_ref[:, _head_lanes(g)] = dk[g]
            dv_ref[:, _head_lanes(g)] = dv[g]
            ds_ref[g] = ds[g]
        dbe_ref[:, 0, :] = dbe
        dgr_ref[:, 0, :] = dgr

    wide = SDS((t, 2 * D_MODEL), F32)
    rowshape = SDS((GDN_V_HEADS, nc, 1, GDN_CHUNK), F32)
    return pl.pallas_call(
        body, grid=(GDN_V_HEADS // GDN_GROUP, nc), in_specs=[q, k, v, rw, rw, st, inv, o], out_specs=[o, o, o, rw, rw],
        out_shape=[wide, wide, wide, rowshape, rowshape],
        scratch_shapes=[pltpu.VMEM((GDN_GROUP, HEAD, HEAD), F32)], name="gdn_rec_bwd", compiler_params=_params(2),
    )(qk, qk, c, beta_row, gam_row, ss, invs, do)


def _gated_norm(o, gate, w):
    return _rms(o, w) * _silu(gate)


def _post_fwd(o, proj, col_off, w, name):
    t, width = o.shape
    tt = _tile(t, (256, 128))

    def fn(o, gate, w):
        return jnp.concatenate([_gated_norm(_head(o, h), _head(gate, h), w) for h in range(width // HEAD)], axis=1)

    return _tmap(fn, (1, t // tt),
                 [(o, *_rows(width, tt)), (proj, *_rows(width, tt, col_off * HEAD // width)), (w, (1, HEAD), lambda j, i: (0, 0))],
                 [((t, width), BF16, *_rows(width, tt), None)], name)[0]


def _post_bwd(o, proj, col_off, w, dout, name):
    t, width = o.shape
    tt = _tile(t, (256, 128))

    def fn(o, gate, w, dout):
        do, dgate, dw = [], [], jnp.zeros((1, HEAD), F32)
        for h in range(width // HEAD):
            _, vjp = jax.vjp(_gated_norm, _head(o, h), _head(gate, h), w)
            a, b, c = vjp(_head(dout, h))
            do.append(a)
            dgate.append(b)
            dw = dw + c
        return jnp.concatenate(do, axis=1), jnp.concatenate(dgate, axis=1), dw

    r = _rows(width, tt)
    return _tmap(fn, (1, t // tt),
                 [(o, *r), (proj, *_rows(width, tt, col_off * HEAD // width)), (w, (1, HEAD), lambda j, i: (0, 0)), (dout, *r)],
                 [((t, width), F32, *r, None), ((t, width), BF16, *r, None), ((1, HEAD), F32, (1, HEAD), lambda j, i: (0, 0), "inner")], name)


def _merge(gate_h, gate_g, yh, yg):
    return _sigmoid(gate_h) * yh + _sigmoid(gate_g) * yg


def _merge_fwd(proj, yh, yg):
    t = yh.shape[0]
    tt, ft = _tile(t, (256, 128)), 512
    r = _rows(ft, tt)
    return _tmap(_merge, (D_MODEL // ft, t // tt),
                 [(proj, *_rows(ft, tt, COL_GATE_H * HEAD // ft)), (proj, *_rows(ft, tt, COL_GATE_G * HEAD // ft)), (yh, *r), (yg, *r)],
                 [((t, D_MODEL), BF16, *r, None)], "merge")[0]


def _merge_bwd(proj, yh, yg, dy):
    t = yh.shape[0]
    tt, ft = _tile(t, (256, 128)), 512
    r = _rows(ft, tt)

    def fn(gate_h, gate_g, yh, yg, dy):
        _, vjp = jax.vjp(_merge, gate_h, gate_g, yh, yg)
        return vjp(dy)

    o = ((t, D_MODEL), BF16, *r, None)
    return _tmap(fn, (D_MODEL // ft, t // tt),
                 [(proj, *_rows(ft, tt, COL_GATE_H * HEAD // ft)), (proj, *_rows(ft, tt, COL_GATE_G * HEAD // ft)), (yh, *r), (yg, *r), (dy, *r)],
                 [o, o, o, o], "merge_bwd")


def _loss_head(h, target, g):
    t, d = h.shape
    tt = _tile(t, (256, 128))

    def fn(h, target, g):
        def f(h, g):
            err = _rms(h, g) - target
            return 0.5 * jnp.sum(jnp.mean(err * err, axis=-1))

        loss, (dh, dg) = jax.value_and_grad(f, (0, 1))(h, g)
        return dh, dg, jnp.full((1, HEAD), loss, F32)

    return _tmap(fn, (1, t // tt), [(h, *_rows(d, tt)), (target, *_rows(d, tt)), (g, (1, d), lambda j, i: (0, 0))],
                 [((t, d), F32, *_rows(d, tt), None), ((1, d), F32, (1, d), lambda j, i: (0, 0), "inner"),
                  ((1, HEAD), F32, (1, HEAD), lambda j, i: (0, 0), "inner")], "loss_head")


def _mixer_fwd(h, p, links):
    t = h.shape[0]
    nc = t // GDN_CHUNK
    u = _rms_fwd(h, p["mix_norm"], "mix_norm")
    w = {n: links.weight(n, h) for n in ("w_in_t", "w_in_b_t", "w_in_ab_t", "conv_w")}
    proj = _mm(u, w["w_in_t"], "nt", F32, "mix_in", after=links.started, b_rows=SCALAR_ROWS)
    proj_b = _mm(u, w["w_in_b_t"], "nt", F32, "mix_in_b")
    pab = _mm(u, w["w_in_ab_t"], "nt", F32, "mix_in_ab")
    qh, kh, bh = _hgrn_prep_fwd(proj, p["lbl"])
    oh, hs = _hgrn_rec_fwd(qh, kh, proj, bh)
    c, conv_y = _conv_fwd(proj, w["conv_w"])
    qk = _qk_norm_fwd(c)
    gates_t = _gates_fwd(pab, p["alog"], p["dtb"])
    gam_row = gates_t[:GDN_V_HEADS].reshape(GDN_V_HEADS, nc, 1, GDN_CHUNK)
    beta_row = gates_t[GDN_V_HEADS:2 * GDN_V_HEADS].reshape(GDN_V_HEADS, nc, 1, GDN_CHUNK)
    og, ss, invs = _gdn_rec_fwd(qk, c, beta_row, gam_row)
    ohn = _post_fwd(oh, proj, COL_HG, p["hgrn_out_norm"], "hgrn_out")
    ogn = _post_fwd(og, proj_b, COL_GZ, p["gdn_out_norm"], "gdn_out")
    w.update({n: links.weight(n, ogn) for n in ("w_branch_hgrn", "w_branch_gdn", "w_out")})
    yh = _mm(ohn, w["w_branch_hgrn"], "nn", BF16, "branch_hgrn")
    yg = _mm(ogn, w["w_branch_gdn"], "nn", BF16, "branch_gdn")
    y = _merge_fwd(proj_b, yh, yg)
    out = _mm(y, w["w_out"], "nn", F32, "mix_out", res=h)
    saved = (w, u, proj, proj_b, pab, qh, kh, bh, oh, hs, c, conv_y, qk, beta_row, gam_row, og, ss, invs, ohn, ogn, yh, yg, y)
    return out, saved


def _mixer_bwd(h, p, links, saved, dout):
    (w, u, proj, proj_b, pab, qh, kh, bh, oh, hs, c, conv_y, qk, beta_row, gam_row, og, ss, invs, ohn, ogn, yh, yg, y) = saved
    t = h.shape[0]
    grads = {}
    dw_out = _mm(y, dout, "tn", BF16, "mix_out_dw")
    dy = _mm(dout, w["w_out"], "nt", F32, "mix_out_dx")
    dgate_h, dgate_g, dyh, dyg = _merge_bwd(proj_b, yh, yg, dy)
    dw_bh = _mm(ohn, dyh, "tn", BF16, "branch_hgrn_dw")
    dw_bg = _mm(ogn, dyg, "tn", BF16, "branch_gdn_dw")
    sent = links.send({"w_out": dw_out, "w_branch_hgrn": dw_bh, "w_branch_gdn": dw_bg})
    dohn = _mm(dyh, w["w_branch_hgrn"], "nt", F32, "branch_hgrn_dx", after=sent)
    dogn = _mm(dyg, w["w_branch_gdn"], "nt", F32, "branch_gdn_dx")
    doh, dhg, grads["hgrn_out_norm"] = _post_bwd(oh, proj, COL_HG, p["hgrn_out_norm"], dohn, "hgrn_out_bwd")
    dog, dgz, grads["gdn_out_norm"] = _post_bwd(og, proj_b, COL_GZ, p["gdn_out_norm"], dogn, "gdn_out_bwd")
    dqh, dkh, dhi, dbh = _hgrn_rec_bwd(qh, kh, proj, bh, hs, doh)
    dhq, dhf, grads["lbl"] = _hgrn_prep_bwd(proj, p["lbl"], dqh, dkh, dbh)
    dqv, dkv, dcv, dbeta_row, dgam_row = _gdn_rec_bwd(qk, c, beta_row, gam_row, ss, invs, dog)
    dcqk = _qk_norm_bwd(c, dqv, dkv)
    dxin, grads["conv_w"] = _conv_bwd(proj, w["conv_w"], conv_y, dcqk, dcv)
    dgates_t = jnp.concatenate([dgam_row.reshape(GDN_V_HEADS, t), dbeta_row.reshape(GDN_V_HEADS, t),
                                jnp.zeros((HEAD - 2 * GDN_V_HEADS, t), F32)], axis=0)
    dpab, grads["alog"], grads["dtb"] = _gates_bwd(pab, p["alog"], p["dtb"], dgates_t)
    dproj = jnp.concatenate([dhq, dhf, dhi, dhg, dxin], axis=1)
    dproj_b = jnp.concatenate([dgz, dgate_h, dgate_g], axis=1)
    dw_t = _mm(dproj, u, "tn", BF16, "mix_in_dw")
    dw_b_t = _mm(dproj_b, u, "tn", BF16, "mix_in_b_dw")
    dw_ab_t = _mm(dpab, u, "tn", BF16, "mix_in_ab_dw")
    sent = links.send({"w_in": jnp.concatenate([dw_t, dw_ab_t[:N_SCALAR], dw_b_t], axis=0)})
    du = _mm(dproj, w["w_in_t"], "nn", F32, "mix_in_dx", after=sent, b_rows=SCALAR_ROWS)
    du = _mm(dproj_b, w["w_in_b_t"], "nn", F32, "mix_in_b_dx", res=du)
    du = _mm(dpab, w["w_in_ab_t"], "nn", F32, "mix_in_ab_dx", res=du)
    dh, grads["mix_norm"] = _rms_bwd(h, p["mix_norm"], du, dout, "mix_norm_bwd")
    return dh, grads


def _local_step(x, target, p, links):
    def ffn_weights(tag, behind):
        def get(n):
            w_in_t, w_out = links.weight(tag + "_w_in", n), links.weight(tag + "_w_out", n)
            return w_in_t, w_out, links.started if behind else None
        return get

    h1, s1 = _ffn_fwd(x, p["ffn1_norm"] + links.started[0, 0], ffn_weights("ffn1", True), "ffn1")
    h2, sm = _mixer_fwd(h1, p, links)
    h3, s2 = _ffn_fwd(h2, p["ffn2_norm"], ffn_weights("ffn2", False), "ffn2")
    dh3, dfinal, loss = _loss_head(h3, target, p["final_norm"])
    g = {"final_norm": dfinal}
    dh2, g["ffn2_norm"] = _ffn_bwd(h2, p["ffn2_norm"], s2, dh3, "ffn2", links)
    dh1, gm = _mixer_bwd(h1, p, links, sm, dh2)
    g.update(gm)
    dx, g["ffn1_norm"] = _ffn_bwd(x, p["ffn1_norm"], s1, dh1, "ffn1", links)
    return loss, dx, g


HBM_SPEC = pl.BlockSpec(memory_space=pltpu.HBM)
SEM_SPEC = pl.BlockSpec(memory_space=pltpu.SEMAPHORE)
DATAFLOW = pltpu.SideEffectType.DATAFLOW_SIDE_EFFECTING


def _position():
    x, y, c = lax.axis_index("x"), lax.axis_index("y"), lax.axis_index("c")
    return x, y, c, 4 * x + 2 * y + c


def _relations(x, y, c):
    for rel in range(1, N_DEV):
        px = 1 - x if rel & 4 else x
        py = 1 - y if rel & 2 else y
        pc = 1 - c if rel & 1 else c
        yield rel, (px, py, pc), 4 * px + 2 * py + pc


def _sem_index(item, rel):
    return item * (N_DEV - 1) + rel - 1


def _landing(a, mode):
    return lax.empty((N_DEV,) + a.shape if mode == "gather" else a.shape, a.dtype)


ALL_PEERS = tuple(range(1, N_DEV))
ONE_PER_CHIP = (1, 2, 4, 6)


def _copies_start(groups, name, rels=ALL_PEERS):
    flat = [item for grp in groups for item in grp]
    n, ng = len(flat), len(groups)
    lands = [_landing(a, mode) for a, mode in flat]

    def body(*refs):
        src_refs, land_refs, sems, token = refs[:n], refs[n:2 * n], refs[2 * n:2 * n + 2 * ng], refs[-1]
        x, y, c, me = _position()
        for rel, where, peer in _relations(x, y, c):
            if rel not in rels:
                continue
            k = 0
            for gi, grp in enumerate(groups):
                for li, (_, mode) in enumerate(grp):
                    src = src_refs[k] if mode == "gather" else src_refs[k].at[peer]
                    pltpu.make_async_remote_copy(src_ref=src, dst_ref=land_refs[k].at[me], send_sem=sems[2 * gi].at[_sem_index(li, rel)],
                                                 recv_sem=sems[2 * gi + 1].at[_sem_index(li, rel)], device_id=where, device_id_type=MESH_IDS).start()
                    k += 1
        token[...] = jnp.zeros_like(token)

    sem_shapes = [pltpu.SemaphoreType.DMA((len(grp) * (N_DEV - 1),)) for grp in groups for _ in range(2)]
    thru = [pltpu.HBM(a.shape, a.dtype) for a, _ in flat] + [pltpu.HBM(l.shape, l.dtype) for l in lands]
    outs = pl.pallas_call(
        body, name=name, out_shape=(*sem_shapes, *thru, SDS((8, HEAD), F32)),
        in_specs=[HBM_SPEC] * (2 * n), out_specs=(*[SEM_SPEC] * (2 * ng), *[HBM_SPEC] * (2 * n), pl.BlockSpec(memory_space=pltpu.VMEM)),
        input_output_aliases={i: 2 * ng + i for i in range(2 * n)}, compiler_params=pltpu.CompilerParams(has_side_effects=DATAFLOW),
    )(*[pltpu.with_memory_space_constraint(a, pltpu.HBM) for a, _ in flat], *[pltpu.with_memory_space_constraint(l, pltpu.HBM) for l in lands])
    sems, srcs, landed, token = outs[:2 * ng], outs[2 * ng:2 * ng + n], outs[2 * ng + n:2 * ng + 2 * n], outs[-1]
    result, k = [], 0
    for gi, grp in enumerate(groups):
        result.append((sems[2 * gi], sems[2 * gi + 1], srcs[k:k + len(grp)], landed[k:k + len(grp)]))
        k += len(grp)
    return result, token


def _copies_wait(started, modes, after, name, rels=ALL_PEERS):
    send_sems, recv_sems, srcs, lands = started
    n = len(srcs)

    def body(*refs):
        src_refs, land_refs, ssem, rsem, token = refs[:n], refs[n:2 * n], refs[2 * n], refs[2 * n + 1], refs[-1]
        x, y, c, _ = _position()
        for rel in rels:
            for i, mode in enumerate(modes):
                src = src_refs[i] if mode == "gather" else src_refs[i].at[0]
                cp = pltpu.make_async_remote_copy(src_ref=src, dst_ref=land_refs[i].at[0], send_sem=ssem.at[_sem_index(i, rel)],
                                                  recv_sem=rsem.at[_sem_index(i, rel)], device_id=(x, y, c), device_id_type=MESH_IDS)
                cp.wait_send()
                cp.wait_recv()
        token[...] = jnp.zeros_like(token)

    outs = pl.pallas_call(
        body, name=name, out_shape=[pltpu.HBM(a.shape, a.dtype) for a in (*srcs, *lands)] + [SDS((8, HEAD), F32)],
        in_specs=[HBM_SPEC] * (2 * n) + [SEM_SPEC, SEM_SPEC, pl.BlockSpec(memory_space=pl.ANY)],
        out_specs=[HBM_SPEC] * (2 * n) + [pl.BlockSpec(memory_space=pltpu.VMEM)],
        input_output_aliases={i: i for i in range(2 * n)}, compiler_params=pltpu.CompilerParams(has_side_effects=DATAFLOW),
    )(*srcs, *lands, send_sems, recv_sems, after)
    return outs[:n], outs[n:2 * n], outs[-1]


OTHER_CHIPS = ((1, 0), (0, 1), (1, 1))


def _pass_on_start(lands, name):
    n = len(lands)

    def body(*refs):
        land_refs, ssem, rsem, token = refs[:n], refs[n], refs[n + 1], refs[-1]
        x, y, c, _ = _position()
        for j, (fx, fy) in enumerate(OTHER_CHIPS):
            slot = 4 * (1 - x if fx else x) + 2 * (1 - y if fy else y) + c
            for i in range(n):
                pltpu.make_async_remote_copy(src_ref=land_refs[i].at[slot], dst_ref=land_refs[i].at[slot], send_sem=ssem.at[i * len(OTHER_CHIPS) + j],
                                             recv_sem=rsem.at[i * len(OTHER_CHIPS) + j], device_id=(x, y, 1 - c), device_id_type=MESH_IDS).start()
        token[...] = jnp.zeros_like(token)

    sems = pltpu.SemaphoreType.DMA((n * len(OTHER_CHIPS),))
    outs = pl.pallas_call(
        body, name=name, out_shape=(sems, sems, *[pltpu.HBM(l.shape, l.dtype) for l in lands], SDS(TOKEN, F32)),
        in_specs=[HBM_SPEC] * n, out_specs=(SEM_SPEC, SEM_SPEC, *[HBM_SPEC] * n, pl.BlockSpec(memory_space=pltpu.VMEM)),
        input_output_aliases={i: 2 + i for i in range(n)}, compiler_params=pltpu.CompilerParams(has_side_effects=DATAFLOW),
    )(*lands)
    return (outs[0], outs[1], outs[2:2 + n]), outs[-1]


def _pass_on_wait(started, after, name):
    send_sems, recv_sems, lands = started
    n = len(lands)

    def body(*refs):
        land_refs, ssem, rsem = refs[:n], refs[n], refs[n + 1]
        x, y, c, _ = _position()
        for j in range(len(OTHER_CHIPS)):
            for i in range(n):
                cp = pltpu.make_async_remote_copy(src_ref=land_refs[i].at[0], dst_ref=land_refs[i].at[0], send_sem=ssem.at[i * len(OTHER_CHIPS) + j],
                                                  recv_sem=rsem.at[i * len(OTHER_CHIPS) + j], device_id=(x, y, c), device_id_type=MESH_IDS)
                cp.wait_send()
                cp.wait_recv()

    return pl.pallas_call(
        body, name=name, out_shape=[pltpu.HBM(l.shape, l.dtype) for l in lands],
        in_specs=[HBM_SPEC] * n + [SEM_SPEC, SEM_SPEC, pl.BlockSpec(memory_space=pl.ANY)], out_specs=[HBM_SPEC] * n,
        input_output_aliases={i: i for i in range(n)}, compiler_params=pltpu.CompilerParams(has_side_effects=DATAFLOW),
    )(*lands, send_sems, recv_sems, after)


WEIGHT_GROUPS = (("ffn1_w_in", "ffn1_w_out", "gdn_conv_w"), ("w_in",), ("w_branch_hgrn", "w_branch_gdn", "w_out", "ffn2_w_in", "ffn2_w_out"))
GROUP_RELS = (ONE_PER_CHIP, ONE_PER_CHIP, ALL_PEERS)


class _Links:
    def __init__(self, shards, me):
        self.me = me
        self.shards = shards
        self.weights = {}
        self.sends = []
        self.gathers = {}
        self.started = None
        self._start_gather(0, None)

    def _start_gather(self, gi, zeros):
        if gi < len(WEIGHT_GROUPS):
            items = [(self.shards[n] if zeros is None else self.shards[n] + zeros[0, 0].astype(self.shards[n].dtype), "gather")
                     for n in WEIGHT_GROUPS[gi]]
            started, self.started = _copies_start([items], "gather_start_%d" % gi, GROUP_RELS[gi])
            self.gathers[gi] = started[0]

    def weight(self, name, after):
        if name not in self.weights:
            source = {"w_in_t": "w_in", "w_in_b_t": "w_in", "w_in_ab_t": "w_in", "conv_w": "gdn_conv_w"}.get(name, name)
            gi = [i for i, grp in enumerate(WEIGHT_GROUPS) if source in grp][0]
            assert gi in self.gathers, "weight groups are asked for in order"
            srcs, lands, zero = _copies_wait(self.gathers[gi], ["gather"] * len(WEIGHT_GROUPS[gi]), after, "gather_wait_%d" % gi, GROUP_RELS[gi])
            if GROUP_RELS[gi] == ONE_PER_CHIP:
                passing, zero = _pass_on_start(lands, "gather_pass_%d" % gi)
                self._start_gather(gi + 1, zero)
                lands = _pass_on_wait(passing, self.started, "gather_passed_%d" % gi)
            else:
                self._start_gather(gi + 1, zero)
            for n, src, land in zip(WEIGHT_GROUPS[gi], srcs, lands):
                full = lax.dynamic_update_index_in_dim(land, src, self.me, 0)
                if n == "gdn_conv_w":
                    self.weights["conv_w"] = full.reshape(N_DEV, CONV_K, 4 * D_MODEL // N_DEV).transpose(1, 0, 2).reshape(CONV_K, 4 * D_MODEL)
                elif n == "w_in":
                    self.weights.update(_w_in_pieces(full.reshape(-1, D_MODEL)))
                else:
                    self.weights[n] = full.reshape(-1, D_MODEL)
        return self.weights[name]

    def send(self, grads):
        names = list(grads)
        blocks = [grads[n].reshape(N_DEV, -1, D_MODEL) for n in names]
        started, token = _copies_start([[(b, "scatter") for b in blocks]], "send_" + names[0])
        self.sends.append((names, started[0]))
        return token

    def landed(self, after):
        out = {}
        for names, started in self.sends:
            srcs, lands, _ = _copies_wait(started, ["scatter"] * len(names), after, "landed_" + names[0])
            for n, src, land in zip(names, srcs, lands):
                out[n] = lax.dynamic_update_index_in_dim(land, lax.dynamic_index_in_dim(src, self.me, 0, keepdims=False), self.me, 0)
        return out


def _adam(parts, w, m, v, name):
    n_parts, r, c = parts.shape
    tc = c if c <= 512 else (256 if r > 1024 else 512)

    def body(p_ref, w_ref, m_ref, v_ref, g_ref, d_ref, mo_ref, vo_ref):
        g = p_ref[0].astype(F32)
        for i in range(1, n_parts):
            g = g + p_ref[i].astype(F32)
        m_new = ADAM_B1 * m_ref[...] + (1.0 - ADAM_B1) * g
        v_new = ADAM_B2 * v_ref[...] + (1.0 - ADAM_B2) * (g * g)
        m_hat = m_new / (1.0 - ADAM_B1 ** ADAM_STEP)
        v_hat = v_new / (1.0 - ADAM_B2 ** ADAM_STEP)
        g_ref[...] = g
        d_ref[...] = -ADAM_LR * (m_hat / (jnp.sqrt(v_hat) + ADAM_EPS) + ADAM_WD * w_ref[...])
        mo_ref[...] = m_new
        vo_ref[...] = v_new

    spec = pl.BlockSpec((r, tc), lambda j: (0, j))
    return pl.pallas_call(
        body, grid=(c // tc,), in_specs=[pl.BlockSpec((n_parts, r, tc), lambda j: (0, 0, j)), spec, spec, spec],
        out_specs=[spec] * 4, out_shape=[SDS((r, c), F32)] * 4, name=name, compiler_params=_params(1),
    )(parts, w, m, v)


BIG = ("ffn1_w_in", "ffn1_w_out", "w_in", "w_branch_hgrn", "w_branch_gdn", "w_out", "ffn2_w_in", "ffn2_w_out")


TRANSPOSED = ("ffn1_w_in", "w_in", "ffn2_w_in")


def _shard_rows(name, shard):
    return shard.T if name in TRANSPOSED else shard


SCALAR_ROWS = 8192
N_SCALAR = 2 * GDN_V_HEADS


def _w_in_pieces(w_in_t):
    return {"w_in_t": w_in_t, "w_in_b_t": w_in_t[SCALAR_ROWS + N_SCALAR:],
            "w_in_ab_t": jnp.pad(w_in_t[SCALAR_ROWS:SCALAR_ROWS + N_SCALAR], ((0, HEAD - N_SCALAR), (0, 0)))}


def _pad_lanes(a, width=HEAD):
    return jnp.pad(a, ((0, 0), (0, width - a.shape[1])))


SMALL_ROWS = 24


def _pack_small(g, loss):
    row6 = jnp.concatenate([g["hgrn_out_norm"], g["gdn_out_norm"], g["alog"], g["dtb"], loss,
                            jnp.zeros((1, D_MODEL - 5 * HEAD), F32)], axis=1)
    return jnp.concatenate([g["ffn1_norm"], g["mix_norm"], g["lbl"], g["ffn2_norm"], g["final_norm"], row6,
                            jnp.zeros((1, D_MODEL), F32), g["conv_w"].reshape(4 * CONV_K, D_MODEL)], axis=0)


def _pack_small_state(a):
    row6 = jnp.concatenate([a["hgrn_out_norm"], a["gdn_out_norm"], _pad_lanes(a["gdn_a_log"]), _pad_lanes(a["gdn_dt_bias"]),
                            jnp.zeros((1, D_MODEL - 4 * HEAD), F32)], axis=1)
    return jnp.concatenate([a["ffn1_norm"], a["mix_norm"], a["hgrn_lb_logits"], a["ffn2_norm"], a["final_norm"].reshape(1, D_MODEL),
                            row6, jnp.zeros((1, D_MODEL), F32)], axis=0)


def _unpack_small(a):
    return {"ffn1_norm": a[0:1], "mix_norm": a[1:2], "hgrn_lb_logits": a[2:4], "ffn2_norm": a[4:5], "final_norm": a[5],
            "hgrn_out_norm": a[6:7, :HEAD], "gdn_out_norm": a[6:7, HEAD:2 * HEAD],
            "gdn_a_log": a[6:7, 2 * HEAD:2 * HEAD + GDN_V_HEADS], "gdn_dt_bias": a[6:7, 3 * HEAD:3 * HEAD + GDN_V_HEADS]}


NAMES = ("ffn1_norm", "ffn1_w_in", "ffn1_w_out", "mix_norm", "w_in", "hgrn_lb_logits", "hgrn_out_norm", "gdn_conv_w", "gdn_a_log",
         "gdn_dt_bias", "gdn_out_norm", "w_branch_hgrn", "w_branch_gdn", "w_out", "ffn2_norm", "ffn2_w_in", "ffn2_w_out", "final_norm")


def kernel(x, ffn1_norm, ffn1_w_in, ffn1_w_out, mix_norm, w_in, hgrn_lb_logits, hgrn_out_norm, gdn_conv_w, gdn_a_log, gdn_dt_bias, gdn_out_norm, w_branch_hgrn, w_branch_gdn, w_out, ffn2_norm, ffn2_w_in, ffn2_w_out, final_norm, loss_target, m_ffn1_norm, m_ffn1_w_in, m_ffn1_w_out, m_mix_norm, m_w_in, m_hgrn_lb_logits, m_hgrn_out_norm, m_gdn_conv_w, m_gdn_a_log, m_gdn_dt_bias, m_gdn_out_norm, m_w_branch_hgrn, m_w_branch_gdn, m_w_out, m_ffn2_norm, m_ffn2_w_in, m_ffn2_w_out, m_final_norm, v_ffn1_norm, v_ffn1_w_in, v_ffn1_w_out, v_mix_norm, v_w_in, v_hgrn_lb_logits, v_hgrn_out_norm, v_gdn_conv_w, v_gdn_a_log, v_gdn_dt_bias, v_gdn_out_norm, v_w_branch_hgrn, v_w_branch_gdn, v_w_out, v_ffn2_norm, v_ffn2_w_in, v_ffn2_w_out, v_final_norm):
    wts = dict(zip(NAMES, (ffn1_norm, ffn1_w_in, ffn1_w_out, mix_norm, w_in, hgrn_lb_logits, hgrn_out_norm, gdn_conv_w, gdn_a_log,
                           gdn_dt_bias, gdn_out_norm, w_branch_hgrn, w_branch_gdn, w_out, ffn2_norm, ffn2_w_in, ffn2_w_out, final_norm)))
    mom = dict(zip(NAMES, (m_ffn1_norm, m_ffn1_w_in, m_ffn1_w_out, m_mix_norm, m_w_in, m_hgrn_lb_logits, m_hgrn_out_norm, m_gdn_conv_w,
                           m_gdn_a_log, m_gdn_dt_bias, m_gdn_out_norm, m_w_branch_hgrn, m_w_branch_gdn, m_w_out, m_ffn2_norm, m_ffn2_w_in,
                           m_ffn2_w_out, m_final_norm)))
    var = dict(zip(NAMES, (v_ffn1_norm, v_ffn1_w_in, v_ffn1_w_out, v_mix_norm, v_w_in, v_hgrn_lb_logits, v_hgrn_out_norm, v_gdn_conv_w,
                           v_gdn_a_log, v_gdn_dt_bias, v_gdn_out_norm, v_w_branch_hgrn, v_w_branch_gdn, v_w_out, v_ffn2_norm, v_ffn2_w_in,
                           v_ffn2_w_out, v_final_norm)))
    me = 4 * lax.axis_index("x") + 2 * lax.axis_index("y") + lax.axis_index("c")

    conv_shard = wts["gdn_conv_w"][0]
    shards = {n: _shard_rows(n, wts[n][0]).astype(BF16) for n in BIG}
    shards["gdn_conv_w"] = conv_shard.reshape(2, D_MODEL)
    links = _Links(shards, me)
    p = {"ffn1_norm": wts["ffn1_norm"], "mix_norm": wts["mix_norm"], "ffn2_norm": wts["ffn2_norm"], "final_norm": wts["final_norm"].reshape(1, D_MODEL),
         "lbl": wts["hgrn_lb_logits"], "hgrn_out_norm": wts["hgrn_out_norm"], "gdn_out_norm": wts["gdn_out_norm"],
         "alog": _pad_lanes(wts["gdn_a_log"]), "dtb": _pad_lanes(wts["gdn_dt_bias"])}

    loss, dx, g = _local_step(x[0], loss_target[0], p, links)

    small_started, small_token = _copies_start([[(_pack_small(g, loss), "gather")]], "small_start")
    landed = links.landed(small_token)

    big = [{} for _ in range(4)]
    for n in BIG:
        res = _adam(landed[n], _shard_rows(n, wts[n][0]), _shard_rows(n, mom[n][0]), _shard_rows(n, var[n][0]), "adam_" + n)
        for kind in range(4):
            big[kind][n] = _shard_rows(n, res[kind])
    small_srcs, small_lands, _ = _copies_wait(small_started[0], ["gather"], res[0], "small_wait")
    small_parts = lax.dynamic_update_index_in_dim(small_lands[0], small_srcs[0], me, 0)
    n_vec = SMALL_ROWS - 4 * CONV_K
    small_raw = _adam(small_parts[:, :n_vec], _pack_small_state(wts), _pack_small_state(mom), _pack_small_state(var), "adam_small")
    small = [_unpack_small(o) for o in small_raw]
    loss_total = small_raw[0][6, 4 * HEAD]
    conv_parts = small_parts[:, n_vec:].reshape(N_DEV, CONV_K, 4 * D_MODEL)
    width = 4 * D_MODEL // N_DEV
    conv_mine = lax.dynamic_slice_in_dim(conv_parts, me * width, width, axis=2)
    conv = _adam(conv_mine, conv_shard, mom["gdn_conv_w"][0], var["gdn_conv_w"][0], "adam_conv")

    outs = []
    for kind in range(4):
        for n in NAMES:
            if n in BIG:
                outs.append(big[kind][n][None])
            elif n == "gdn_conv_w":
                outs.append(conv[kind][None])
            else:
                outs.append(small[kind][n])
    return (loss_total, dx[None], *outs)
```

```python
import functools

import jax
import jax.numpy as jnp
from jax import lax
from jax.experimental import pallas as pl
from jax.experimental.pallas import tpu as pltpu

F32 = jnp.float32
BF16 = jnp.bfloat16
HIGHEST = lax.Precision.HIGHEST
MESH_IDS = pl.DeviceIdType.MESH

D_MODEL = 1024
D_FF = 2816
N_DEV = 8
EPS = 1e-6
HEAD = 128
HG_HEADS = 8
GDN_QK_HEADS = 8
GDN_V_HEADS = 16
GDN_CHUNK = 64
HG_CHUNK = 16
CONV_K = 4
IN_WIDTH = 12320
COL_HQ, COL_HF, COL_HI, COL_HG, COL_GQ, COL_GK, COL_GV = 0, 8, 16, 24, 32, 40, 48
COL_GZ, COL_GATE_H, COL_GATE_G = 0, 16, 24
VMEM_LIMIT = 56 * 1024 * 1024

ADAM_LR, ADAM_B1, ADAM_B2, ADAM_EPS, ADAM_WD, ADAM_STEP = 0.001, 0.9, 0.999, 1e-08, 0.01, 10

SDS = jax.ShapeDtypeStruct


def _params(n_axes):
    return pltpu.CompilerParams(dimension_semantics=("arbitrary",) * n_axes, vmem_limit_bytes=VMEM_LIMIT)


def _tile(n, candidates=(512, 384, 256, 128, 64, 32, 16, 8)):
    for c in candidates:
        if n % c == 0:
            return c
    return n


_DIMS = {"nn": ((1,), (0,)), "nt": ((1,), (1,)), "tn": ((0,), (0,))}


def _bdot_raw(a, b, dims):
    return lax.dot_general(a.astype(BF16), b.astype(BF16), (_DIMS[dims], ((), ())), preferred_element_type=F32)


@functools.partial(jax.custom_vjp, nondiff_argnums=(2,))
def _bdot(a, b, dims):
    return _bdot_raw(a, b, dims)


def _bdot_fwd(a, b, dims):
    return _bdot_raw(a, b, dims), (a, b)


def _bdot_bwd(dims, res, ct):
    a, b = res
    if dims == "nn":
        return _bdot_raw(ct, b, "nt"), _bdot_raw(a, ct, "tn")
    if dims == "nt":
        return _bdot_raw(ct, b, "nn"), _bdot_raw(ct, a, "tn")
    return _bdot_raw(b, ct, "nt"), _bdot_raw(a, ct, "nn")


_bdot.defvjp(_bdot_fwd, _bdot_bwd)


def _hdot_raw(a, b):
    return jnp.dot(a, b, precision=HIGHEST, preferred_element_type=F32)


MM_VMEM_BUDGET = 38 * 1024 * 1024
TOKEN = (8, HEAD)


def _mm_tiles(m, n, k, a_bytes, b_bytes, o_bytes, r_bytes, m_align=8):
    def need(tm, tn, tk):
        return 2 * (tm * tk * a_bytes + tk * tn * b_bytes + tm * tn * (o_bytes + r_bytes)) + (tm * tn * 4 if tk < k else 0)

    def shrink(tm, tn, tk, floor_m, floor_n):
        while need(tm, tn, tk) > MM_VMEM_BUDGET:
            if tn > floor_n and tn % 256 == 0 and tn >= tm:
                tn //= 2
            elif tm > floor_m and tm % (2 * m_align) == 0:
                tm //= 2
            elif tn > floor_n and tn % 256 == 0:
                tn //= 2
            else:
                return None
        return tm, tn, tk

    tm = _tile(m, (1408, 1024, 704, 512, 256, 128, 64, 32, 16, 8))
    tn = _tile(n, (1408, 1024, 512, 256, 128))
    whole = shrink(tm, tn, k, min(tm, 1024), min(tn, 512))
    if whole is not None:
        return whole
    tk = _tile(k, (2048, 1408, 1024, 512, 256, 128, 64, 32, 16, 8))
    while True:
        fit = shrink(tm, tn, tk, min(tm, 256), min(tn, 512))
        if fit is not None or tk <= 512 or tk % 256:
            return fit if fit is not None else (tm, tn, tk)
        tk //= 2


def _mm(a, b, dims, out_dtype, name, res=None, alpha=1.0, after=None, b_rows=None):
    b_shape = b.shape if b_rows is None else (b_rows, b.shape[1])
    if dims == "nn":
        (m, k), (k2, n) = a.shape, b_shape
    elif dims == "nt":
        (m, k), (n, k2) = a.shape, b_shape
    else:
        (k, m), (k2, n) = a.shape, b_shape
    assert k == k2, (a.shape, b.shape, dims)
    has_res = res is not None
    tm, tn, tk = _mm_tiles(m, n, k, a.dtype.itemsize, b.dtype.itemsize, jnp.dtype(out_dtype).itemsize, res.dtype.itemsize if has_res else 0,
                           m_align=HEAD if dims == "tn" else 8)
    nk = k // tk
    a_spec = pl.BlockSpec((tk, tm), lambda i, j, kk: (kk, i)) if dims == "tn" else pl.BlockSpec((tm, tk), lambda i, j, kk: (i, kk))
    b_spec = pl.BlockSpec((tn, tk), lambda i, j, kk: (j, kk)) if dims == "nt" else pl.BlockSpec((tk, tn), lambda i, j, kk: (kk, j))
    o_spec = pl.BlockSpec((tm, tn), lambda i, j, kk: (i, j))

    def finish(acc, r_ref, o_ref):
        out = acc * alpha if alpha != 1.0 else acc
        if has_res:
            out = r_ref[...].astype(F32) + out
        o_ref[...] = out.astype(o_ref.dtype)

    n_in = 2 + has_res + (after is not None)

    def body(*refs):
        a_ref, b_ref = refs[:2]
        r_ref = refs[2] if has_res else None
        o_ref = refs[n_in]
        p = _bdot_raw(a_ref[...], b_ref[...], dims)
        if nk == 1:
            finish(p, r_ref, o_ref)
            return
        acc_ref = refs[-1]
        kk = pl.program_id(2)

        @pl.when(kk == 0)
        def _():
            acc_ref[...] = p

        @pl.when(kk > 0)
        def _():
            acc_ref[...] += p

        @pl.when(kk == nk - 1)
        def _():
            finish(acc_ref[...], r_ref, o_ref)

    args = (a, b) + ((res,) if has_res else ()) + ((after,) if after is not None else ())
    in_specs = [a_spec, b_spec] + ([o_spec] if has_res else []) + ([pl.BlockSpec(TOKEN, lambda i, j, kk: (0, 0))] if after is not None else [])
    return pl.pallas_call(
        body, grid=(m // tm, n // tn, nk), in_specs=in_specs, out_specs=o_spec, out_shape=SDS((m, n), out_dtype),
        scratch_shapes=[pltpu.VMEM((tm, tn), F32)] if nk > 1 else [], name=name, compiler_params=_params(3),
    )(*args)


PIECE_TK = 1024


def _mm_pieces(pieces, b, name, res=None, after=None):
    m, n = pieces[0].shape[0], b.shape[1]
    blocks = [p.shape[1] // PIECE_TK for p in pieces]
    assert all(p.shape[1] % PIECE_TK == 0 and p.shape[0] == m for p in pieces)
    starts = [sum(blocks[:i]) for i in range(len(pieces))]
    nk = sum(blocks)
    tm, tn = _tile(m, (512, 256, 128)), _tile(n, (1024, 512, 256, 128))
    n_p = len(pieces)
    n_in = n_p + 1 + (res is not None) + (after is not None)

    def piece_spec(start, count):
        return pl.BlockSpec((tm, PIECE_TK), lambda i, j, kk: (i, jnp.clip(kk - start, 0, count - 1)))

    def body(*refs):
        b_ref, o_ref, acc_ref = refs[n_p], refs[n_in], refs[-1]
        kk = pl.program_id(2)

        @pl.when(kk == 0)
        def _():
            acc_ref[...] = jnp.zeros_like(acc_ref)

        for p_ref, start, count in zip(refs[:n_p], starts, blocks):
            @pl.when(jnp.logical_and(kk >= start, kk < start + count))
            def _(p_ref=p_ref):
                acc_ref[...] += _bdot_raw(p_ref[...], b_ref[...], "nn")

        @pl.when(kk == nk - 1)
        def _():
            out = acc_ref[...]
            if res is not None:
                out = refs[n_p + 1][...] + out
            o_ref[...] = out

    o_spec = pl.BlockSpec((tm, tn), lambda i, j, kk: (i, j))
    in_specs = [piece_spec(s, c) for s, c in zip(starts, blocks)] + [pl.BlockSpec((PIECE_TK, tn), lambda i, j, kk: (kk, j))]
    args = list(pieces) + [b]
    if res is not None:
        in_specs.append(o_spec)
        args.append(res)
    if after is not None:
        in_specs.append(pl.BlockSpec(TOKEN, lambda i, j, kk: (0, 0)))
        args.append(after)
    return pl.pallas_call(
        body, grid=(m // tm, n // tn, nk), in_specs=in_specs, out_specs=o_spec, out_shape=SDS((m, n), F32),
        scratch_shapes=[pltpu.VMEM((tm, tn), F32)], name=name, compiler_params=_params(3),
    )(*args)


def _tmap(fn, grid, ins, outs, name):
    n_in = len(ins)
    n_ax = len(grid)

    def body(*refs):
        vals = fn(*[r[...] for r in refs[:n_in]])
        if not isinstance(vals, (tuple, list)):
            vals = (vals,)
        first_inner = pl.program_id(n_ax - 1) == 0
        first_all = first_inner
        for ax in range(n_ax - 1):
            first_all = jnp.logical_and(first_all, pl.program_id(ax) == 0)

        def put(ref, val, acc):
            val = val.astype(ref.dtype)
            if acc is None:
                ref[...] = val
                return
            first = first_inner if acc == "inner" else first_all

            @pl.when(first)
            def _():
                ref[...] = val

            @pl.when(jnp.logical_not(first))
            def _():
                ref[...] += val

        for ref, val, o in zip(refs[n_in:], vals, outs):
            put(ref, val, o[4])

    return pl.pallas_call(
        body, grid=grid,
        in_specs=[pl.BlockSpec(bs, im) for _, bs, im in ins],
        out_specs=[pl.BlockSpec(o[2], o[3]) for o in outs],
        out_shape=[SDS(o[0], o[1]) for o in outs],
        name=name, compiler_params=_params(n_ax),
    )(*[a for a, _, _ in ins])


def _rows(width, tt, off=0):
    return (tt, width), (lambda j, i: (i, off + j))


def _rms(x, g):
    x = x.astype(F32)
    return x * lax.rsqrt(jnp.mean(x * x, axis=-1, keepdims=True) + EPS) * g


def _sigmoid(x):
    return jax.nn.sigmoid(x)


def _silu(x):
    return x * _sigmoid(x)


def _softplus(x):
    return jnp.maximum(x, 0.0) + jnp.log1p(jnp.exp(-jnp.abs(x)))


def _rms_fwd(x, g, name):
    t, d = x.shape
    tt = _tile(t, (256, 128))
    return _tmap(_rms, (1, t // tt), [(x, *_rows(d, tt)), (g, (1, d), lambda j, i: (0, 0))],
                 [((t, d), BF16, *_rows(d, tt), None)], name)[0]


def _rms_bwd(x, g, dn, dres, name):
    t, d = x.shape
    tt = _tile(t, (256, 128))

    def fn(x, g, dn, dres):
        _, vjp = jax.vjp(_rms, x, g)
        dx, dg = vjp(dn.astype(F32))
        return dres + dx, dg

    return _tmap(fn, (1, t // tt),
                 [(x, *_rows(d, tt)), (g, (1, d), lambda j, i: (0, 0)), (dn, *_rows(d, tt)), (dres, *_rows(d, tt))],
                 [((t, d), F32, *_rows(d, tt), None), ((1, d), F32, (1, d), lambda j, i: (0, 0), "inner")], name)


def _swiglu(ab):
    return _silu(ab[:, :D_FF].astype(F32)) * ab[:, D_FF:].astype(F32)


def _swiglu_fwd(ab, name):
    t = ab.shape[0]
    tt = _tile(t, (128,))
    return _tmap(_swiglu, (1, t // tt), [(ab, *_rows(2 * D_FF, tt))], [((t, D_FF), BF16, *_rows(D_FF, tt), None)], name)[0]


def _swiglu_bwd(ab, ds, name):
    t = ab.shape[0]
    tt = _tile(t, (128,))

    def fn(ab, ds):
        a, b = ab[:, :D_FF].astype(F32), ab[:, D_FF:].astype(F32)
        _, vjp = jax.vjp(lambda a, b: _silu(a) * b, a, b)
        da, db = vjp(ds.astype(F32))
        return jnp.concatenate([da, db], axis=1)

    return _tmap(fn, (1, t // tt), [(ab, *_rows(2 * D_FF, tt)), (ds, *_rows(D_FF, tt))],
                 [((t, 2 * D_FF), BF16, *_rows(2 * D_FF, tt), None)], name)[0]


def _ffn_fwd(h, g, weights, tag):
    n = _rms_fwd(h, g, tag + "_norm")
    w_in_t, w_out, after = weights(n)
    ab = _mm(n, w_in_t, "nt", BF16, tag + "_in", after=after)
    s = _swiglu_fwd(ab, tag + "_act")
    out = _mm(s, w_out, "nn", F32, tag + "_out", res=h, alpha=0.5)
    return out, (n, ab, s, w_in_t, w_out)


def _ffn_bwd(h, g, saved, dout, tag, links):
    n, ab, s, w_in_t, w_out = saved
    sent = links.send({tag + "_w_out": _mm(s, dout, "tn", BF16, tag + "_dw_out", alpha=0.5)})
    ds = _mm(dout, w_out, "nt", BF16, tag + "_ds", alpha=0.5, after=sent)
    dab = _swiglu_bwd(ab, ds, tag + "_dact")
    sent = links.send({tag + "_w_in": _mm(dab, n, "tn", BF16, tag + "_dw_in")})
    dn = _mm(dab, w_in_t, "nn", F32, tag + "_dn", after=sent)
    return _rms_bwd(h, g, dn, dout, tag + "_dnorm")


def _chunk_sum_matrix(n, chunk, transpose=False):
    row = lax.broadcasted_iota(jnp.int32, (n, n), 0)
    col = lax.broadcasted_iota(jnp.int32, (n, n), 1)
    if transpose:
        row, col = col, row
    return jnp.where(jnp.logical_and(col <= row, row // chunk == col // chunk), 1.0, 0.0).astype(F32)


def _hgrn_gates(hq, hf, lbl):
    lb = _sigmoid(lbl[0:1, :] - lbl[1:2, :])
    sg = _sigmoid(hf)
    f = lb + (1.0 - lb) * sg
    q = _silu(hq) * HEAD ** -0.5
    k = (1.0 - lb) * (1.0 - sg)
    return q, k, jnp.log(f)


def _hgrn_prep_fwd(proj, lbl):
    t = proj.shape[0]
    tt, ft = _tile(t, (256, 128)), 512

    def fn(hq, hf, lbl):
        q, k, log_f = _hgrn_gates(hq, hf, lbl)
        return q, k, _hdot_raw(_chunk_sum_matrix(tt, HG_CHUNK), log_f)

    o = ((t, D_MODEL), F32, *_rows(ft, tt), None)
    return _tmap(fn, (D_MODEL // ft, t // tt),
                 [(proj, *_rows(ft, tt, COL_HQ * HEAD // ft)), (proj, *_rows(ft, tt, COL_HF * HEAD // ft)), (lbl, (2, ft), lambda j, i: (0, j))],
                 [o, o, o], "hgrn_prep")


def _hgrn_prep_bwd(proj, lbl, dq, dk, db):
    t = proj.shape[0]
    tt, ft = _tile(t, (256, 128)), 512

    def fn(hq, hf, lbl, dq, dk, db):
        dlog_f = _hdot_raw(_chunk_sum_matrix(tt, HG_CHUNK, transpose=True), db)
        _, vjp = jax.vjp(_hgrn_gates, hq, hf, lbl)
        return vjp((dq, dk, dlog_f))

    o = ((t, D_MODEL), BF16, *_rows(ft, tt), None)
    r = _rows(ft, tt)
    return _tmap(fn, (D_MODEL // ft, t // tt),
                 [(proj, *_rows(ft, tt, COL_HQ * HEAD // ft)), (proj, *_rows(ft, tt, COL_HF * HEAD // ft)), (lbl, (2, ft), lambda j, i: (0, j)),
                  (dq, *r), (dk, *r), (db, *r)],
                 [o, o, ((2, D_MODEL), F32, (2, ft), lambda j, i: (0, j), "inner")], "hgrn_prep_bwd")


def _hgrn_chunks(q, k, v, b, st):
    n = q[0].shape[0]
    half = n // 2
    srow = lax.broadcasted_iota(jnp.int32, (half, HEAD), 0)
    inter = _each(lambda q, b, st: _bdot(q * jnp.exp(b), st, "nt"), q, b, st)

    def below_scores(q, k, b):
        ref = b[half:half + 1, :]
        return _bdot(q[half:] * jnp.exp(jnp.minimum(b[half:] - ref, 0.0)), k[:half] * jnp.exp(jnp.minimum(ref - b[:half], 0.0)), "nt")

    below = _each(lambda a, v: _bdot(a, v[:half], "nn"), _each(below_scores, q, k, b), v)

    def diagonal(q, k, v, b):
        rows = []
        for lo in (0, half):
            qb, kb, vb, bb = (a[lo:lo + half] for a in (q, k, v, b))
            for t in range(half):
                e = jnp.where(srow <= t, jnp.exp(jnp.minimum(bb[t:t + 1, :] - bb, 0.0)), 0.0)
                a = jnp.sum(qb[t:t + 1, :] * kb * e, axis=1, keepdims=True)
                rows.append(jnp.sum(a * vb, axis=0, keepdims=True))
        return jnp.concatenate(rows, axis=0)

    diag = _each(diagonal, q, k, v, b)
    o = _each(lambda inter, diag, below: inter + diag + jnp.concatenate([jnp.zeros_like(below), below], axis=0), inter, diag, below)

    def new_state(k, v, b, st):
        bend = b[n - 1:n, :]
        return st * jnp.exp(bend) + _bdot(v, k * jnp.exp(bend - b), "tn")

    return o, _each(new_state, k, v, b, st)


HG_GROUP = 4
HG_PER = GDN_CHUNK // HG_CHUNK


def _hgrn_rec_fwd(q, k, proj, b):
    t = q.shape[0]
    nc = t // GDN_CHUNK
    blk = (GDN_CHUNK, HG_GROUP * HEAD)
    im = lambda h, c: (c, h)

    def body(q_ref, k_ref, v_ref, b_ref, o_ref, hs_ref, st_ref):
        @pl.when(pl.program_id(1) == 0)
        def _():
            st_ref[...] = jnp.zeros_like(st_ref)

        heads = range(HG_GROUP)
        for j in range(HG_PER):
            sl = pl.ds(HG_CHUNK * j, HG_CHUNK)
            st = tuple(st_ref[g] for g in heads)
            o, st_new = _hgrn_chunks(*[tuple(r[sl, _head_lanes(g)] for g in heads) for r in (q_ref, k_ref, v_ref, b_ref)], st)
            for g in heads:
                hs_ref[g, j] = st[g]
                o_ref[sl, _head_lanes(g)] = o[g]
                st_ref[g] = st_new[g]

    return pl.pallas_call(
        body, grid=(HG_HEADS // HG_GROUP, nc),
        in_specs=[pl.BlockSpec(blk, im), pl.BlockSpec(blk, im), pl.BlockSpec(blk, lambda h, c: (c, COL_HI // HG_GROUP + h)), pl.BlockSpec(blk, im)],
        out_specs=[pl.BlockSpec(blk, im), pl.BlockSpec((HG_GROUP, HG_PER, HEAD, HEAD), lambda h, c: (h, c, 0, 0))],
        out_shape=[SDS((t, D_MODEL), F32), SDS((HG_HEADS, nc * HG_PER, HEAD, HEAD), F32)],
        scratch_shapes=[pltpu.VMEM((HG_GROUP, HEAD, HEAD), F32)], name="hgrn_rec", compiler_params=_params(2),
    )(q, k, proj, b)


def _hgrn_rec_bwd(q, k, proj, b, hs, do):
    t = q.shape[0]
    nc = t // GDN_CHUNK
    blk = (GDN_CHUNK, HG_GROUP * HEAD)
    im = lambda h, c: (nc - 1 - c, h)

    def body(q_ref, k_ref, v_ref, b_ref, hs_ref, do_ref, dq_ref, dk_ref, dv_ref, db_ref, dst_ref):
        @pl.when(pl.program_id(1) == 0)
        def _():
            dst_ref[...] = jnp.zeros_like(dst_ref)

        heads = range(HG_GROUP)
        for j in reversed(range(HG_PER)):
            sl = pl.ds(HG_CHUNK * j, HG_CHUNK)
            _, vjp = jax.vjp(_hgrn_chunks, *[tuple(r[sl, _head_lanes(g)] for g in heads) for r in (q_ref, k_ref, v_ref, b_ref)],
                             tuple(hs_ref[g, j] for g in heads))
            dq, dk, dv, db, dst = vjp((tuple(do_ref[sl, _head_lanes(g)] for g in heads), tuple(dst_ref[g] for g in heads)))
            for g in heads:
                ln = _head_lanes(g)
                dq_ref[sl, ln] = dq[g]
                dk_ref[sl, ln] = dk[g]
                dv_ref[sl, ln] = dv[g].astype(dv_ref.dtype)
                db_ref[sl, ln] = db[g]
                dst_ref[g] = dst[g]

    spec = pl.BlockSpec(blk, im)
    return pl.pallas_call(
        body, grid=(HG_HEADS // HG_GROUP, nc),
        in_specs=[spec, spec, pl.BlockSpec(blk, lambda h, c: (nc - 1 - c, COL_HI // HG_GROUP + h)), spec,
                  pl.BlockSpec((HG_GROUP, HG_PER, HEAD, HEAD), lambda h, c: (h, nc - 1 - c, 0, 0)), spec],
        out_specs=[spec, spec, spec, spec],
        out_shape=[SDS((t, D_MODEL), F32), SDS((t, D_MODEL), F32), SDS((t, D_MODEL), BF16), SDS((t, D_MODEL), F32)],
        scratch_shapes=[pltpu.VMEM((HG_GROUP, HEAD, HEAD), F32)], name="hgrn_rec_bwd", compiler_params=_params(2),
    )(q, k, proj, b, hs, do)


def _shift_down(x, d):
    if d == 0:
        return x
    row = lax.broadcasted_iota(jnp.int32, x.shape, 0)
    return jnp.where(row >= d, pltpu.roll(x, d, 0), 0.0)


def _shift_up(x, d):
    if d == 0:
        return x
    n = x.shape[0]
    row = lax.broadcasted_iota(jnp.int32, x.shape, 0)
    return jnp.where(row < n - d, pltpu.roll(x, n - d, 0), 0.0)


def _conv_fwd(proj, conv_w):
    t = proj.shape[0]
    width = 2 * D_MODEL + 2 * D_MODEL

    def body(x_ref, w_ref, c_ref, y_ref):
        x, w = x_ref[...], w_ref[...]
        y = w[CONV_K - 1:CONV_K, :] * x
        for j in range(CONV_K - 1):
            y = y + w[j:j + 1, :] * _shift_down(x, CONV_K - 1 - j)
        y_ref[...] = y
        c_ref[...] = _silu(y)

    out = pl.BlockSpec((t, HEAD), lambda j: (0, j))
    return pl.pallas_call(
        body, grid=(width // HEAD,),
        in_specs=[pl.BlockSpec((t, HEAD), lambda j: (0, COL_GQ + j)), pl.BlockSpec((CONV_K, HEAD), lambda j: (0, j))],
        out_specs=[out, out], out_shape=[SDS((t, width), F32), SDS((t, width), F32)],
        name="gdn_conv", compiler_params=_params(1),
    )(proj, conv_w)


def _conv_bwd(proj, conv_w, y, dc_qk, dc_v):
    t = proj.shape[0]
    n_qk = dc_qk.shape[1] // HEAD
    width = dc_qk.shape[1] + dc_v.shape[1]

    def body(x_ref, w_ref, y_ref, dqk_ref, dv_ref, dx_ref, dw_ref):
        x, w, y = x_ref[...], w_ref[...], y_ref[...]
        sg = _sigmoid(y)
        dc = jnp.where(pl.program_id(0) < n_qk, dqk_ref[...], dv_ref[...])
        dy = dc * (sg * (1.0 + y * (1.0 - sg)))
        ahead = [_shift_up(dy, CONV_K - 1 - j) for j in range(CONV_K)]
        dx = w[0:1, :] * ahead[0]
        for j in range(1, CONV_K):
            dx = dx + w[j:j + 1, :] * ahead[j]
        dx_ref[...] = dx.astype(dx_ref.dtype)
        dw_ref[...] = jnp.concatenate([jnp.sum(x * ahead[j], axis=0, keepdims=True) for j in range(CONV_K)], axis=0)

    blk = pl.BlockSpec((t, HEAD), lambda j: (0, j))
    return pl.pallas_call(
        body, grid=(width // HEAD,),
        in_specs=[pl.BlockSpec((t, HEAD), lambda j: (0, COL_GQ + j)), pl.BlockSpec((CONV_K, HEAD), lambda j: (0, j)), blk,
                  pl.BlockSpec((t, HEAD), lambda j: (0, jnp.minimum(j, n_qk - 1))), pl.BlockSpec((t, HEAD), lambda j: (0, jnp.maximum(j - n_qk, 0)))],
        out_specs=[blk, pl.BlockSpec((CONV_K, HEAD), lambda j: (0, j))],
        out_shape=[SDS((t, width), BF16), SDS((CONV_K, width), F32)],
        name="gdn_conv_bwd", compiler_params=_params(1),
    )(proj, conv_w, y, dc_qk, dc_v)


def _l2norm(x, scale):
    return x * lax.rsqrt(jnp.sum(x * x, axis=-1, keepdims=True) + EPS) * scale


def _head(a, h):
    return a[:, h * HEAD:(h + 1) * HEAD]


def _qk_scale(h):
    return HEAD ** -0.5 if h < GDN_QK_HEADS else 1.0


def _qk_norm_fwd(c):
    t = c.shape[0]
    tt = _tile(t, (256, 128))
    width = 2 * D_MODEL

    def fn(x):
        return jnp.concatenate([_l2norm(_head(x, h), _qk_scale(h)) for h in range(2 * GDN_QK_HEADS)], axis=1)

    return _tmap(fn, (1, t // tt), [(c, *_rows(width, tt))], [((t, width), F32, *_rows(width, tt), None)], "gdn_qk_norm")[0]


def _qk_norm_bwd(c, dq_rep, dk_rep):
    t = c.shape[0]
    tt = _tile(t, (256, 128))
    width = 2 * D_MODEL

    def fn(x, dq2, dk2):
        out = []
        for h in range(2 * GDN_QK_HEADS):
            d2, hh = (dq2, h) if h < GDN_QK_HEADS else (dk2, h - GDN_QK_HEADS)
            _, vjp = jax.vjp(lambda x: _l2norm(x, _qk_scale(h)), _head(x, h))
            out.append(vjp(_head(d2, 2 * hh) + _head(d2, 2 * hh + 1))[0])
        return jnp.concatenate(out, axis=1)

    r = _rows(width, tt)
    return _tmap(fn, (1, t // tt), [(c, *r), (dq_rep, *r), (dk_rep, *r)], [((t, width), F32, *r, None)], "gdn_qk_norm_bwd")[0]


def _gdn_gates(x, alog, dtb):
    return -jnp.exp(alog) * _softplus(x + dtb), _sigmoid(x)


def _gates_fwd(pab, alog, dtb):
    t = pab.shape[0]
    tt = _tile(t, (256, 128))

    def fn(x, alog, dtb):
        g, beta = _gdn_gates(x, alog, dtb)
        lane = lax.broadcasted_iota(jnp.int32, g.shape, 1)
        return jnp.where(lane < GDN_V_HEADS, _hdot_raw(_chunk_sum_matrix(tt, GDN_CHUNK), g), beta).T

    p = (alog, (1, HEAD), lambda j, i: (0, 0)), (dtb, (1, HEAD), lambda j, i: (0, 0))
    return _tmap(fn, (1, t // tt), [(pab, *_rows(HEAD, tt)), *p], [((HEAD, t), F32, (HEAD, tt), lambda j, i: (0, i), None)], "gdn_gates")[0]


def _gates_bwd(pab, alog, dtb, dout_t):
    t = pab.shape[0]
    tt = _tile(t, (256, 128))

    def fn(x, alog, dtb, dout_t):
        dout = dout_t.T
        lane = lax.broadcasted_iota(jnp.int32, dout.shape, 1)
        dgam = jnp.where(lane < GDN_V_HEADS, dout, 0.0)
        dbeta = jnp.where(jnp.logical_and(lane >= GDN_V_HEADS, lane < 2 * GDN_V_HEADS), dout, 0.0)
        dg = _hdot_raw(_chunk_sum_matrix(tt, GDN_CHUNK, transpose=True), dgam)
        _, vjp = jax.vjp(_gdn_gates, x, alog, dtb)
        return vjp((dg, dbeta))

    p = (alog, (1, HEAD), lambda j, i: (0, 0)), (dtb, (1, HEAD), lambda j, i: (0, 0))
    acc = ((1, HEAD), F32, (1, HEAD), lambda j, i: (0, 0), "inner")
    return _tmap(fn, (1, t // tt), [(pab, *_rows(HEAD, tt)), *p, (dout_t, (HEAD, tt), lambda j, i: (0, i))],
                 [((t, HEAD), BF16, *_rows(HEAD, tt), None), acc, acc], "gdn_gates_bwd")


def _split_bf16(x):
    hi = x.astype(BF16)
    return hi, (x - hi.astype(F32)).astype(BF16)


def _dot3(a, b):
    (ah, al), (bh, bl) = a, b
    return _bdot_raw(ah, bh, "nn") + (_bdot_raw(ah, bl, "nn") + _bdot_raw(al, bh, "nn"))


def _each(fn, *lists):
    return tuple(fn(*xs) for xs in zip(*lists))


def _unit_lower_inverses_raw(a):
    n = a[0].shape[0]
    row = lax.broadcasted_iota(jnp.int32, (n, n), 0)
    col = lax.broadcasted_iota(jnp.int32, (n, n), 1)
    eye = jnp.where(row == col, 1.0, 0.0).astype(F32)
    p = _each(lambda a: eye - a, a)
    x = _each(_split_bf16, a)
    m = 2
    while m < n:
        x = _each(_split_bf16, _each(_dot3, x, x))
        p = _each(lambda p, x: p + _bdot_raw(p, x[0], "nn"), p, x)
        m *= 2
    return p


@jax.custom_vjp
def _unit_lower_inverses(a, known):
    return _unit_lower_inverses_raw(a) if known is None else known


def _uli_fwd(a, known):
    inv = _unit_lower_inverses(a, known)
    return inv, (inv, known)


def _uli_bwd(res, ct):
    inv, known = res
    right = _each(lambda ct, inv: _bdot_raw(ct, inv, "nt"), ct, inv)
    da = _each(lambda inv, r: -_bdot_raw(inv, r, "tn"), inv, right)
    return da, (None if known is None else _each(jnp.zeros_like, known))


_unit_lower_inverses.defvjp(_uli_fwd, _uli_bwd)


def _gdn_chunks(q, k, v, beta_rows, gam_rows, s, inv_known=None):
    n = q[0].shape[0]
    heads = range(len(q))
    row = lax.broadcasted_iota(jnp.int32, (n, n), 0)
    col = lax.broadcasted_iota(jnp.int32, (n, n), 1)
    beta_cols, gam_cols = beta_rows.T, gam_rows.T
    beta = tuple(beta_cols[:, g:g + 1] for g in heads)
    gam = tuple(gam_cols[:, g:g + 1] for g in heads)
    gam_row = tuple(gam_rows[g:g + 1, :] for g in heads)
    decay = _each(lambda gam, gam_row: jnp.where(row >= col, jnp.exp(jnp.minimum(gam - gam_row, 0.0)), 0.0), gam, gam_row)
    kb = _each(lambda k, beta: k * beta, k, beta)
    a = _each(lambda kb, k, decay: jnp.where(row > col, _bdot(kb, k, "nt") * decay, 0.0), kb, k, decay)
    inv = _unit_lower_inverses(a, inv_known)
    eg = _each(jnp.exp, gam)
    u = _each(lambda inv, v, beta: _bdot(inv, v * beta, "nn"), inv, v, beta)
    w = _each(lambda inv, kb, eg: _bdot(inv, kb * eg, "nn"), inv, kb, eg)
    qk = _each(lambda q, k, decay: _bdot(q, k, "nt") * decay, q, k, decay)
    v_new = _each(lambda u, w, s: u - _bdot(w, s, "nn"), u, w, s)
    o_state = _each(lambda q, eg, s: _bdot(q * eg, s, "nn"), q, eg, s)
    o = _each(lambda o_state, qk, v_new: o_state + _bdot(qk, v_new, "nn"), o_state, qk, v_new)
    gend = _each(lambda gam: gam[n - 1:n, :], gam)
    s_new = _each(lambda s, k, gam, gend, v_new: s * jnp.exp(gend) + _bdot(k * jnp.exp(gend - gam), v_new, "tn"), s, k, gam, gend, v_new)
    return o, s_new, inv


GDN_GROUP = 16


def _gdn_specs(nc, rev):
    cc = (lambda c: nc - 1 - c) if rev else (lambda c: c)
    grp = GDN_GROUP
    q = pl.BlockSpec((GDN_CHUNK, grp // 2 * HEAD), lambda h, c: (cc(c), h))
    k = pl.BlockSpec((GDN_CHUNK, grp // 2 * HEAD), lambda h, c: (cc(c), 2 * GDN_QK_HEADS // grp + h))
    v = pl.BlockSpec((GDN_CHUNK, grp * HEAD), lambda h, c: (cc(c), 2 * GDN_QK_HEADS // grp + h))
    o = pl.BlockSpec((GDN_CHUNK, grp * HEAD), lambda h, c: (cc(c), h))
    rw = pl.BlockSpec((grp, None, 1, GDN_CHUNK), lambda h, c: (h, cc(c), 0, 0))
    st = pl.BlockSpec((grp, None, HEAD, HEAD), lambda h, c: (h, cc(c), 0, 0))
    inv = pl.BlockSpec((grp, None, GDN_CHUNK, GDN_CHUNK), lambda h, c: (h, cc(c), 0, 0))
    return q, k, v, o, rw, st, inv


def _head_lanes(g, per=1):
    return pl.ds((g // per) * HEAD, HEAD)


def _gdn_rec_fwd(qk, c, beta_row, gam_row):
    t = qk.shape[0]
    nc = t // GDN_CHUNK
    q, k, v, o, rw, st, inv = _gdn_specs(nc, False)

    def body(q_ref, k_ref, v_ref, be_ref, gr_ref, o_ref, ss_ref, inv_ref, s_ref):
        @pl.when(pl.program_id(1) == 0)
        def _():
            s_ref[...] = jnp.zeros_like(s_ref)

        heads = range(GDN_GROUP)
        s = tuple(s_ref[g] for g in heads)
        out, s_new, inv_c = _gdn_chunks(
            tuple(q_ref[:, _head_lanes(g, 2)] for g in heads), tuple(k_ref[:, _head_lanes(g, 2)] for g in heads),
            tuple(v_ref[:, _head_lanes(g)] for g in heads), be_ref[:, 0, :], gr_ref[:, 0, :], s)
        for g in heads:
            ss_ref[g] = s[g]
            o_ref[:, _head_lanes(g)] = out[g]
            inv_ref[g] = inv_c[g]
            s_ref[g] = s_new[g]

    return pl.pallas_call(
        body, grid=(GDN_V_HEADS // GDN_GROUP, nc), in_specs=[q, k, v, rw, rw], out_specs=[o, st, inv],
        out_shape=[SDS((t, 2 * D_MODEL), F32), SDS((GDN_V_HEADS, nc, HEAD, HEAD), F32), SDS((GDN_V_HEADS, nc, GDN_CHUNK, GDN_CHUNK), F32)],
        scratch_shapes=[pltpu.VMEM((GDN_GROUP, HEAD, HEAD), F32)], name="gdn_rec", compiler_params=_params(2),
    )(qk, qk, c, beta_row, gam_row)


def _gdn_rec_bwd(qk, c, beta_row, gam_row, ss, invs, do):
    t = qk.shape[0]
    nc = t // GDN_CHUNK
    q, k, v, o, rw, st, inv = _gdn_specs(nc, True)

    def body(q_ref, k_ref, v_ref, be_ref, gr_ref, ss_ref, inv_ref, do_ref,
             dq_ref, dk_ref, dv_ref, dbe_ref, dgr_ref, ds_ref):
        @pl.when(pl.program_id(1) == 0)
        def _():
            ds_ref[...] = jnp.zeros_like(ds_ref)

        heads = range(GDN_GROUP)
        _, vjp = jax.vjp(
            _gdn_chunks,
            tuple(q_ref[:, _head_lanes(g, 2)] for g in heads), tuple(k_ref[:, _head_lanes(g, 2)] for g in heads),
            tuple(v_ref[:, _head_lanes(g)] for g in heads), be_ref[:, 0, :], gr_ref[:, 0, :],
            tuple(ss_ref[g] for g in heads), tuple(inv_ref[g] for g in heads))
        no_inv_ct = tuple(jnp.zeros((GDN_CHUNK, GDN_CHUNK), F32) for g in heads)
        dq, dk, dv, dbe, dgr, ds, _ = vjp((tuple(do_ref[:, _head_lanes(g)] for g in heads), tuple(ds_ref[g] for g in heads), no_inv_ct))
        for g in heads:
            dq_ref[:, _head_lanes(g)] = dq[g]
            dk_ref[:, _head_lanes(g)] = dk[g]
            dv_ref[:, _head_lanes(g)] = dv[g]
            ds_ref[g] = ds[g]
        dbe_ref[:, 0, :] = dbe
        dgr_ref[:, 0, :] = dgr

    wide = SDS((t, 2 * D_MODEL), F32)
    rowshape = SDS((GDN_V_HEADS, nc, 1, GDN_CHUNK), F32)
    return pl.pallas_call(
        body, grid=(GDN_V_HEADS // GDN_GROUP, nc), in_specs=[q, k, v, rw, rw, st, inv, o], out_specs=[o, o, o, rw, rw],
        out_shape=[wide, wide, wide, rowshape, rowshape],
        scratch_shapes=[pltpu.VMEM((GDN_GROUP, HEAD, HEAD), F32)], name="gdn_rec_bwd", compiler_params=_params(2),
    )(qk, qk, c, beta_row, gam_row, ss, invs, do)


def _gated_norm(o, gate, w):
    return _rms(o, w) * _silu(gate)


def _post_fwd(o, proj, col_off, w, name):
    t, width = o.shape
    tt = _tile(t, (256, 128))

    def fn(o, gate, w):
        return jnp.concatenate([_gated_norm(_head(o, h), _head(gate, h), w) for h in range(width // HEAD)], axis=1)

    return _tmap(fn, (1, t // tt),
                 [(o, *_rows(width, tt)), (proj, *_rows(width, tt, col_off * HEAD // width)), (w, (1, HEAD), lambda j, i: (0, 0))],
                 [((t, width), BF16, *_rows(width, tt), None)], name)[0]


def _post_bwd(o, proj, col_off, w, dout, name):
    t, width = o.shape
    tt = _tile(t, (256, 128))

    def fn(o, gate, w, dout):
        do, dgate, dw = [], [], jnp.zeros((1, HEAD), F32)
        for h in range(width // HEAD):
            _, vjp = jax.vjp(_gated_norm, _head(o, h), _head(gate, h), w)
            a, b, c = vjp(_head(dout, h))
            do.append(a)
            dgate.append(b)
            dw = dw + c
        return jnp.concatenate(do, axis=1), jnp.concatenate(dgate, axis=1), dw

    r = _rows(width, tt)
    return _tmap(fn, (1, t // tt),
                 [(o, *r), (proj, *_rows(width, tt, col_off * HEAD // width)), (w, (1, HEAD), lambda j, i: (0, 0)), (dout, *r)],
                 [((t, width), F32, *r, None), ((t, width), BF16, *r, None), ((1, HEAD), F32, (1, HEAD), lambda j, i: (0, 0), "inner")], name)


def _merge(gate_h, gate_g, yh, yg):
    return _sigmoid(gate_h) * yh + _sigmoid(gate_g) * yg


def _merge_fwd(proj, yh, yg):
    t = yh.shape[0]
    tt, ft = _tile(t, (256, 128)), 512
    r = _rows(ft, tt)
    return _tmap(_merge, (D_MODEL // ft, t // tt),
                 [(proj, *_rows(ft, tt, COL_GATE_H * HEAD // ft)), (proj, *_rows(ft, tt, COL_GATE_G * HEAD // ft)), (yh, *r), (yg, *r)],
                 [((t, D_MODEL), BF16, *r, None)], "merge")[0]


def _merge_bwd(proj, yh, yg, dy):
    t = yh.shape[0]
    tt, ft = _tile(t, (256, 128)), 512
    r = _rows(ft, tt)

    def fn(gate_h, gate_g, yh, yg, dy):
        _, vjp = jax.vjp(_merge, gate_h, gate_g, yh, yg)
        return vjp(dy)

    o = ((t, D_MODEL), BF16, *r, None)
    return _tmap(fn, (D_MODEL // ft, t // tt),
                 [(proj, *_rows(ft, tt, COL_GATE_H * HEAD // ft)), (proj, *_rows(ft, tt, COL_GATE_G * HEAD // ft)), (yh, *r), (yg, *r), (dy, *r)],
                 [o, o, o, o], "merge_bwd")


def _loss_head(h, target, g):
    t, d = h.shape
    tt = _tile(t, (256, 128))

    def fn(h, target, g):
        def f(h, g):
            err = _rms(h, g) - target
            return 0.5 * jnp.sum(jnp.mean(err * err, axis=-1))

        loss, (dh, dg) = jax.value_and_grad(f, (0, 1))(h, g)
        return dh, dg, jnp.full((1, HEAD), loss, F32)

    return _tmap(fn, (1, t // tt), [(h, *_rows(d, tt)), (target, *_rows(d, tt)), (g, (1, d), lambda j, i: (0, 0))],
                 [((t, d), F32, *_rows(d, tt), None), ((1, d), F32, (1, d), lambda j, i: (0, 0), "inner"),
                  ((1, HEAD), F32, (1, HEAD), lambda j, i: (0, 0), "inner")], "loss_head")


def _mixer_fwd(h, p, links):
    t = h.shape[0]
    nc = t // GDN_CHUNK
    u = _rms_fwd(h, p["mix_norm"], "mix_norm")
    w = {n: links.weight(n, h) for n in ("w_in_t", "w_in_b_t", "w_in_ab_t", "conv_w")}
    proj = _mm(u, w["w_in_t"], "nt", F32, "mix_in", after=links.started, b_rows=SCALAR_ROWS)
    proj_b = _mm(u, w["w_in_b_t"], "nt", F32, "mix_in_b")
    pab = _mm(u, w["w_in_ab_t"], "nt", F32, "mix_in_ab")
    qh, kh, bh = _hgrn_prep_fwd(proj, p["lbl"])
    oh, hs = _hgrn_rec_fwd(qh, kh, proj, bh)
    c, conv_y = _conv_fwd(proj, w["conv_w"])
    qk = _qk_norm_fwd(c)
    gates_t = _gates_fwd(pab, p["alog"], p["dtb"])
    gam_row = gates_t[:GDN_V_HEADS].reshape(GDN_V_HEADS, nc, 1, GDN_CHUNK)
    beta_row = gates_t[GDN_V_HEADS:2 * GDN_V_HEADS].reshape(GDN_V_HEADS, nc, 1, GDN_CHUNK)
    og, ss, invs = _gdn_rec_fwd(qk, c, beta_row, gam_row)
    ohn = _post_fwd(oh, proj, COL_HG, p["hgrn_out_norm"], "hgrn_out")
    ogn = _post_fwd(og, proj_b, COL_GZ, p["gdn_out_norm"], "gdn_out")
    w.update({n: links.weight(n, ogn) for n in ("w_branch_hgrn", "w_branch_gdn", "w_out")})
    yh = _mm(ohn, w["w_branch_hgrn"], "nn", BF16, "branch_hgrn")
    yg = _mm(ogn, w["w_branch_gdn"], "nn", BF16, "branch_gdn")
    y = _merge_fwd(proj_b, yh, yg)
    out = _mm(y, w["w_out"], "nn", F32, "mix_out", res=h)
    saved = (w, u, proj, proj_b, pab, qh, kh, bh, oh, hs, c, conv_y, qk, beta_row, gam_row, og, ss, invs, ohn, ogn, yh, yg, y)
    return out, saved


def _mixer_bwd(h, p, links, saved, dout):
    (w, u, proj, proj_b, pab, qh, kh, bh, oh, hs, c, conv_y, qk, beta_row, gam_row, og, ss, invs, ohn, ogn, yh, yg, y) = saved
    t = h.shape[0]
    grads = {}
    dw_out = _mm(y, dout, "tn", BF16, "mix_out_dw")
    dy = _mm(dout, w["w_out"], "nt", F32, "mix_out_dx")
    dgate_h, dgate_g, dyh, dyg = _merge_bwd(proj_b, yh, yg, dy)
    dw_bh = _mm(ohn, dyh, "tn", BF16, "branch_hgrn_dw")
    dw_bg = _mm(ogn, dyg, "tn", BF16, "branch_gdn_dw")
    sent = links.send({"w_out": dw_out, "w_branch_hgrn": dw_bh, "w_branch_gdn": dw_bg})
    dohn = _mm(dyh, w["w_branch_hgrn"], "nt", F32, "branch_hgrn_dx", after=sent)
    dogn = _mm(dyg, w["w_branch_gdn"], "nt", F32, "branch_gdn_dx")
    doh, dhg, grads["hgrn_out_norm"] = _post_bwd(oh, proj, COL_HG, p["hgrn_out_norm"], dohn, "hgrn_out_bwd")
    dog, dgz, grads["gdn_out_norm"] = _post_bwd(og, proj_b, COL_GZ, p["gdn_out_norm"], dogn, "gdn_out_bwd")
    dqh, dkh, dhi, dbh = _hgrn_rec_bwd(qh, kh, proj, bh, hs, doh)
    dhq, dhf, grads["lbl"] = _hgrn_prep_bwd(proj, p["lbl"], dqh, dkh, dbh)
    dqv, dkv, dcv, dbeta_row, dgam_row = _gdn_rec_bwd(qk, c, beta_row, gam_row, ss, invs, dog)
    dcqk = _qk_norm_bwd(c, dqv, dkv)
    dxin, grads["conv_w"] = _conv_bwd(proj, w["conv_w"], conv_y, dcqk, dcv)
    dgates_t = jnp.concatenate([dgam_row.reshape(GDN_V_HEADS, t), dbeta_row.reshape(GDN_V_HEADS, t),
                                jnp.zeros((HEAD - 2 * GDN_V_HEADS, t), F32)], axis=0)
    dpab, grads["alog"], grads["dtb"] = _gates_bwd(pab, p["alog"], p["dtb"], dgates_t)
    front, back = [dhq, dhf, dhi, dhg, dxin], [dgz, dgate_h, dgate_g]
    dw_front = [_mm(d, u, "tn", BF16, "mix_in_dw_%d" % i) for i, d in enumerate(front)]
    dw_back = [_mm(d, u, "tn", BF16, "mix_in_b_dw_%d" % i) for i, d in enumerate(back)]
    dw_ab_t = _mm(dpab, u, "tn", BF16, "mix_in_ab_dw")
    sent = links.send({"w_in": jnp.concatenate(dw_front + [dw_ab_t[:N_SCALAR]] + dw_back, axis=0)})
    du = _mm_pieces(front, w["w_in_t"], "mix_in_dx", after=sent)
    du = _mm_pieces(back, w["w_in_b_t"], "mix_in_b_dx", res=du)
    du = _mm(dpab, w["w_in_ab_t"], "nn", F32, "mix_in_ab_dx", res=du)
    dh, grads["mix_norm"] = _rms_bwd(h, p["mix_norm"], du, dout, "mix_norm_bwd")
    return dh, grads


def _local_step(x, target, p, links):
    def ffn_weights(tag, behind):
        def get(n):
            w_in_t, w_out = links.weight(tag + "_w_in", n), links.weight(tag + "_w_out", n)
            return w_in_t, w_out, links.started if behind else None
        return get

    h1, s1 = _ffn_fwd(x, p["ffn1_norm"] + links.started[0, 0], ffn_weights("ffn1", True), "ffn1")
    h2, sm = _mixer_fwd(h1, p, links)
    h3, s2 = _ffn_fwd(h2, p["ffn2_norm"], ffn_weights("ffn2", False), "ffn2")
    dh3, dfinal, loss = _loss_head(h3, target, p["final_norm"])
    g = {"final_norm": dfinal}
    dh2, g["ffn2_norm"] = _ffn_bwd(h2, p["ffn2_norm"], s2, dh3, "ffn2", links)
    dh1, gm = _mixer_bwd(h1, p, links, sm, dh2)
    g.update(gm)
    dx, g["ffn1_norm"] = _ffn_bwd(x, p["ffn1_norm"], s1, dh1, "ffn1", links)
    return loss, dx, g


HBM_SPEC = pl.BlockSpec(memory_space=pltpu.HBM)
SEM_SPEC = pl.BlockSpec(memory_space=pltpu.SEMAPHORE)
DATAFLOW = pltpu.SideEffectType.DATAFLOW_SIDE_EFFECTING


def _position():
    x, y, c = lax.axis_index("x"), lax.axis_index("y"), lax.axis_index("c")
    return x, y, c, 4 * x + 2 * y + c


def _relations(x, y, c):
    for rel in range(1, N_DEV):
        px = 1 - x if rel & 4 else x
        py = 1 - y if rel & 2 else y
        pc = 1 - c if rel & 1 else c
        yield rel, (px, py, pc), 4 * px + 2 * py + pc


def _sem_index(item, rel):
    return item * (N_DEV - 1) + rel - 1


def _landing(a, mode):
    return lax.empty((N_DEV,) + a.shape if mode == "gather" else a.shape, a.dtype)


ALL_PEERS = tuple(range(1, N_DEV))
ONE_PER_CHIP = (1, 2, 4, 6)


def _copies_start(groups, name, rels=ALL_PEERS):
    flat = [item for grp in groups for item in grp]
    n, ng = len(flat), len(groups)
    lands = [_landing(a, mode) for a, mode in flat]

    def body(*refs):
        src_refs, land_refs, sems, token = refs[:n], refs[n:2 * n], refs[2 * n:2 * n + 2 * ng], refs[-1]
        x, y, c, me = _position()
        for rel, where, peer in _relations(x, y, c):
            if rel not in rels:
                continue
            k = 0
            for gi, grp in enumerate(groups):
                for li, (_, mode) in enumerate(grp):
                    src = src_refs[k] if mode == "gather" else src_refs[k].at[peer]
                    pltpu.make_async_remote_copy(src_ref=src, dst_ref=land_refs[k].at[me], send_sem=sems[2 * gi].at[_sem_index(li, rel)],
                                                 recv_sem=sems[2 * gi + 1].at[_sem_index(li, rel)], device_id=where, device_id_type=MESH_IDS).start()
                    k += 1
        token[...] = jnp.zeros_like(token)

    sem_shapes = [pltpu.SemaphoreType.DMA((len(grp) * (N_DEV - 1),)) for grp in groups for _ in range(2)]
    thru = [pltpu.HBM(a.shape, a.dtype) for a, _ in flat] + [pltpu.HBM(l.shape, l.dtype) for l in lands]
    outs = pl.pallas_call(
        body, name=name, out_shape=(*sem_shapes, *thru, SDS((8, HEAD), F32)),
        in_specs=[HBM_SPEC] * (2 * n), out_specs=(*[SEM_SPEC] * (2 * ng), *[HBM_SPEC] * (2 * n), pl.BlockSpec(memory_space=pltpu.VMEM)),
        input_output_aliases={i: 2 * ng + i for i in range(2 * n)}, compiler_params=pltpu.CompilerParams(has_side_effects=DATAFLOW),
    )(*[pltpu.with_memory_space_constraint(a, pltpu.HBM) for a, _ in flat], *[pltpu.with_memory_space_constraint(l, pltpu.HBM) for l in lands])
    sems, srcs, landed, token = outs[:2 * ng], outs[2 * ng:2 * ng + n], outs[2 * ng + n:2 * ng + 2 * n], outs[-1]
    result, k = [], 0
    for gi, grp in enumerate(groups):
        result.append((sems[2 * gi], sems[2 * gi + 1], srcs[k:k + len(grp)], landed[k:k + len(grp)]))
        k += len(grp)
    return result, token


def _copies_wait(started, modes, after, name, rels=ALL_PEERS):
    send_sems, recv_sems, srcs, lands = started
    n = len(srcs)

    def body(*refs):
        src_refs, land_refs, ssem, rsem, token = refs[:n], refs[n:2 * n], refs[2 * n], refs[2 * n + 1], refs[-1]
        x, y, c, _ = _position()
        for rel in rels:
            for i, mode in enumerate(modes):
                src = src_refs[i] if mode == "gather" else src_refs[i].at[0]
                cp = pltpu.make_async_remote_copy(src_ref=src, dst_ref=land_refs[i].at[0], send_sem=ssem.at[_sem_index(i, rel)],
                                                  recv_sem=rsem.at[_sem_index(i, rel)], device_id=(x, y, c), device_id_type=MESH_IDS)
                cp.wait_send()
                cp.wait_recv()
        token[...] = jnp.zeros_like(token)

    outs = pl.pallas_call(
        body, name=name, out_shape=[pltpu.HBM(a.shape, a.dtype) for a in (*srcs, *lands)] + [SDS((8, HEAD), F32)],
        in_specs=[HBM_SPEC] * (2 * n) + [SEM_SPEC, SEM_SPEC, pl.BlockSpec(memory_space=pl.ANY)],
        out_specs=[HBM_SPEC] * (2 * n) + [pl.BlockSpec(memory_space=pltpu.VMEM)],
        input_output_aliases={i: i for i in range(2 * n)}, compiler_params=pltpu.CompilerParams(has_side_effects=DATAFLOW),
    )(*srcs, *lands, send_sems, recv_sems, after)
    return outs[:n], outs[n:2 * n], outs[-1]


OTHER_CHIPS = ((1, 0), (0, 1), (1, 1))


def _pass_on_start(lands, name):
    n = len(lands)

    def body(*refs):
        land_refs, ssem, rsem, token = refs[:n], refs[n], refs[n + 1], refs[-1]
        x, y, c, _ = _position()
        for j, (fx, fy) in enumerate(OTHER_CHIPS):
            slot = 4 * (1 - x if fx else x) + 2 * (1 - y if fy else y) + c
            for i in range(n):
                pltpu.make_async_remote_copy(src_ref=land_refs[i].at[slot], dst_ref=land_refs[i].at[slot], send_sem=ssem.at[i * len(OTHER_CHIPS) + j],
                                             recv_sem=rsem.at[i * len(OTHER_CHIPS) + j], device_id=(x, y, 1 - c), device_id_type=MESH_IDS).start()
        token[...] = jnp.zeros_like(token)

    sems = pltpu.SemaphoreType.DMA((n * len(OTHER_CHIPS),))
    outs = pl.pallas_call(
        body, name=name, out_shape=(sems, sems, *[pltpu.HBM(l.shape, l.dtype) for l in lands], SDS(TOKEN, F32)),
        in_specs=[HBM_SPEC] * n, out_specs=(SEM_SPEC, SEM_SPEC, *[HBM_SPEC] * n, pl.BlockSpec(memory_space=pltpu.VMEM)),
        input_output_aliases={i: 2 + i for i in range(n)}, compiler_params=pltpu.CompilerParams(has_side_effects=DATAFLOW),
    )(*lands)
    return (outs[0], outs[1], outs[2:2 + n]), outs[-1]


def _pass_on_wait(started, after, name):
    send_sems, recv_sems, lands = started
    n = len(lands)

    def body(*refs):
        land_refs, ssem, rsem = refs[:n], refs[n], refs[n + 1]
        x, y, c, _ = _position()
        for j in range(len(OTHER_CHIPS)):
            for i in range(n):
                cp = pltpu.make_async_remote_copy(src_ref=land_refs[i].at[0], dst_ref=land_refs[i].at[0], send_sem=ssem.at[i * len(OTHER_CHIPS) + j],
                                                  recv_sem=rsem.at[i * len(OTHER_CHIPS) + j], device_id=(x, y, c), device_id_type=MESH_IDS)
                cp.wait_send()
                cp.wait_recv()

    return pl.pallas_call(
        body, name=name, out_shape=[pltpu.HBM(l.shape, l.dtype) for l in lands],
        in_specs=[HBM_SPEC] * n + [SEM_SPEC, SEM_SPEC, pl.BlockSpec(memory_space=pl.ANY)], out_specs=[HBM_SPEC] * n,
        input_output_aliases={i: i for i in range(n)}, compiler_params=pltpu.CompilerParams(has_side_effects=DATAFLOW),
    )(*lands, send_sems, recv_sems, after)


WEIGHT_GROUPS = (("ffn1_w_in", "ffn1_w_out", "gdn_conv_w"), ("w_in",), ("w_branch_hgrn", "w_branch_gdn", "w_out", "ffn2_w_in", "ffn2_w_out"))
GROUP_RELS = (ONE_PER_CHIP, ONE_PER_CHIP, ALL_PEERS)


class _Links:
    def __init__(self, shards, me):
        self.me = me
        self.shards = shards
        self.weights = {}
        self.sends = []
        self.gathers = {}
        self.started = None
        self._start_gather(0, None)

    def _start_gather(self, gi, zeros):
        if gi < len(WEIGHT_GROUPS):
            items = [(self.shards[n] if zeros is None else self.shards[n] + zeros[0, 0].astype(self.shards[n].dtype), "gather")
                     for n in WEIGHT_GROUPS[gi]]
            started, self.started = _copies_start([items], "gather_start_%d" % gi, GROUP_RELS[gi])
            self.gathers[gi] = started[0]

    def weight(self, name, after):
        if name not in self.weights:
            source = {"w_in_t": "w_in", "w_in_b_t": "w_in", "w_in_ab_t": "w_in", "conv_w": "gdn_conv_w"}.get(name, name)
            gi = [i for i, grp in enumerate(WEIGHT_GROUPS) if source in grp][0]
            assert gi in self.gathers, "weight groups are asked for in order"
            srcs, lands, zero = _copies_wait(self.gathers[gi], ["gather"] * len(WEIGHT_GROUPS[gi]), after, "gather_wait_%d" % gi, GROUP_RELS[gi])
            if GROUP_RELS[gi] == ONE_PER_CHIP:
                passing, zero = _pass_on_start(lands, "gather_pass_%d" % gi)
                self._start_gather(gi + 1, zero)
                lands = _pass_on_wait(passing, self.started, "gather_passed_%d" % gi)
            else:
                self._start_gather(gi + 1, zero)
            for n, src, land in zip(WEIGHT_GROUPS[gi], srcs, lands):
                full = lax.dynamic_update_index_in_dim(land, src, self.me, 0)
                if n == "gdn_conv_w":
                    self.weights["conv_w"] = full.reshape(N_DEV, CONV_K, 4 * D_MODEL // N_DEV).transpose(1, 0, 2).reshape(CONV_K, 4 * D_MODEL)
                elif n == "w_in":
                    self.weights.update(_w_in_pieces(full.reshape(-1, D_MODEL)))
                else:
                    self.weights[n] = full.reshape(-1, D_MODEL)
        return self.weights[name]

    def send(self, grads):
        names = list(grads)
        blocks = [grads[n].reshape(N_DEV, -1, D_MODEL) for n in names]
        started, token = _copies_start([[(b, "scatter") for b in blocks]], "send_" + names[0])
        self.sends.append((names, started[0]))
        return token

    def landed(self, after):
        out = {}
        for names, started in self.sends:
            srcs, lands, _ = _copies_wait(started, ["scatter"] * len(names), after, "landed_" + names[0])
            for n, src, land in zip(names, srcs, lands):
                out[n] = lax.dynamic_update_index_in_dim(land, lax.dynamic_index_in_dim(src, self.me, 0, keepdims=False), self.me, 0)
        return out


def _adam(parts, w, m, v, name):
    n_parts, r, c = parts.shape
    tc = c if c <= 512 else (256 if r > 1024 else 512)

    def body(p_ref, w_ref, m_ref, v_ref, g_ref, d_ref, mo_ref, vo_ref):
        g = p_ref[0].astype(F32)
        for i in range(1, n_parts):
            g = g + p_ref[i].astype(F32)
        m_new = ADAM_B1 * m_ref[...] + (1.0 - ADAM_B1) * g
        v_new = ADAM_B2 * v_ref[...] + (1.0 - ADAM_B2) * (g * g)
        m_hat = m_new / (1.0 - ADAM_B1 ** ADAM_STEP)
        v_hat = v_new / (1.0 - ADAM_B2 ** ADAM_STEP)
        g_ref[...] = g
        d_ref[...] = -ADAM_LR * (m_hat / (jnp.sqrt(v_hat) + ADAM_EPS) + ADAM_WD * w_ref[...])
        mo_ref[...] = m_new
        vo_ref[...] = v_new

    spec = pl.BlockSpec((r, tc), lambda j: (0, j))
    return pl.pallas_call(
        body, grid=(c // tc,), in_specs=[pl.BlockSpec((n_parts, r, tc), lambda j: (0, 0, j)), spec, spec, spec],
        out_specs=[spec] * 4, out_shape=[SDS((r, c), F32)] * 4, name=name, compiler_params=_params(1),
    )(parts, w, m, v)


BIG = ("ffn1_w_in", "ffn1_w_out", "w_in", "w_branch_hgrn", "w_branch_gdn", "w_out", "ffn2_w_in", "ffn2_w_out")


TRANSPOSED = ("ffn1_w_in", "w_in", "ffn2_w_in")


def _shard_rows(name, shard):
    return shard.T if name in TRANSPOSED else shard


SCALAR_ROWS = 8192
N_SCALAR = 2 * GDN_V_HEADS


def _w_in_pieces(w_in_t):
    return {"w_in_t": w_in_t, "w_in_b_t": w_in_t[SCALAR_ROWS + N_SCALAR:],
            "w_in_ab_t": jnp.pad(w_in_t[SCALAR_ROWS:SCALAR_ROWS + N_SCALAR], ((0, HEAD - N_SCALAR), (0, 0)))}


def _pad_lanes(a, width=HEAD):
    return jnp.pad(a, ((0, 0), (0, width - a.shape[1])))


SMALL_ROWS = 24


def _pack_small(g, loss):
    row6 = jnp.concatenate([g["hgrn_out_norm"], g["gdn_out_norm"], g["alog"], g["dtb"], loss,
                            jnp.zeros((1, D_MODEL - 5 * HEAD), F32)], axis=1)
    return jnp.concatenate([g["ffn1_norm"], g["mix_norm"], g["lbl"], g["ffn2_norm"], g["final_norm"], row6,
                            jnp.zeros((1, D_MODEL), F32), g["conv_w"].reshape(4 * CONV_K, D_MODEL)], axis=0)


def _pack_small_state(a):
    row6 = jnp.concatenate([a["hgrn_out_norm"], a["gdn_out_norm"], _pad_lanes(a["gdn_a_log"]), _pad_lanes(a["gdn_dt_bias"]),
                            jnp.zeros((1, D_MODEL - 4 * HEAD), F32)], axis=1)
    return jnp.concatenate([a["ffn1_norm"], a["mix_norm"], a["hgrn_lb_logits"], a["ffn2_norm"], a["final_norm"].reshape(1, D_MODEL),
                            row6, jnp.zeros((1, D_MODEL), F32)], axis=0)


def _unpack_small(a):
    return {"ffn1_norm": a[0:1], "mix_norm": a[1:2], "hgrn_lb_logits": a[2:4], "ffn2_norm": a[4:5], "final_norm": a[5],
            "hgrn_out_norm": a[6:7, :HEAD], "gdn_out_norm": a[6:7, HEAD:2 * HEAD],
            "gdn_a_log": a[6:7, 2 * HEAD:2 * HEAD + GDN_V_HEADS], "gdn_dt_bias": a[6:7, 3 * HEAD:3 * HEAD + GDN_V_HEADS]}


NAMES = ("ffn1_norm", "ffn1_w_in", "ffn1_w_out", "mix_norm", "w_in", "hgrn_lb_logits", "hgrn_out_norm", "gdn_conv_w", "gdn_a_log",
         "gdn_dt_bias", "gdn_out_norm", "w_branch_hgrn", "w_branch_gdn", "w_out", "ffn2_norm", "ffn2_w_in", "ffn2_w_out", "final_norm")


def kernel(x, ffn1_norm, ffn1_w_in, ffn1_w_out, mix_norm, w_in, hgrn_lb_logits, hgrn_out_norm, gdn_conv_w, gdn_a_log, gdn_dt_bias, gdn_out_norm, w_branch_hgrn, w_branch_gdn, w_out, ffn2_norm, ffn2_w_in, ffn2_w_out, final_norm, loss_target, m_ffn1_norm, m_ffn1_w_in, m_ffn1_w_out, m_mix_norm, m_w_in, m_hgrn_lb_logits, m_hgrn_out_norm, m_gdn_conv_w, m_gdn_a_log, m_gdn_dt_bias, m_gdn_out_norm, m_w_branch_hgrn, m_w_branch_gdn, m_w_out, m_ffn2_norm, m_ffn2_w_in, m_ffn2_w_out, m_final_norm, v_ffn1_norm, v_ffn1_w_in, v_ffn1_w_out, v_mix_norm, v_w_in, v_hgrn_lb_logits, v_hgrn_out_norm, v_gdn_conv_w, v_gdn_a_log, v_gdn_dt_bias, v_gdn_out_norm, v_w_branch_hgrn, v_w_branch_gdn, v_w_out, v_ffn2_norm, v_ffn2_w_in, v_ffn2_w_out, v_final_norm):
    wts = dict(zip(NAMES, (ffn1_norm, ffn1_w_in, ffn1_w_out, mix_norm, w_in, hgrn_lb_logits, hgrn_out_norm, gdn_conv_w, gdn_a_log,
                           gdn_dt_bias, gdn_out_norm, w_branch_hgrn, w_branch_gdn, w_out, ffn2_norm, ffn2_w_in, ffn2_w_out, final_norm)))
    mom = dict(zip(NAMES, (m_ffn1_norm, m_ffn1_w_in, m_ffn1_w_out, m_mix_norm, m_w_in, m_hgrn_lb_logits, m_hgrn_out_norm, m_gdn_conv_w,
                           m_gdn_a_log, m_gdn_dt_bias, m_gdn_out_norm, m_w_branch_hgrn, m_w_branch_gdn, m_w_out, m_ffn2_norm, m_ffn2_w_in,
                           m_ffn2_w_out, m_final_norm)))
    var = dict(zip(NAMES, (v_ffn1_norm, v_ffn1_w_in, v_ffn1_w_out, v_mix_norm, v_w_in, v_hgrn_lb_logits, v_hgrn_out_norm, v_gdn_conv_w,
                           v_gdn_a_log, v_gdn_dt_bias, v_gdn_out_norm, v_w_branch_hgrn, v_w_branch_gdn, v_w_out, v_ffn2_norm, v_ffn2_w_in,
                           v_ffn2_w_out, v_final_norm)))
    me = 4 * lax.axis_index("x") + 2 * lax.axis_index("y") + lax.axis_index("c")

    conv_shard = wts["gdn_conv_w"][0]
    shards = {n: _shard_rows(n, wts[n][0]).astype(BF16) for n in BIG}
    shards["gdn_conv_w"] = conv_shard.reshape(2, D_MODEL)
    links = _Links(shards, me)
    p = {"ffn1_norm": wts["ffn1_norm"], "mix_norm": wts["mix_norm"], "ffn2_norm": wts["ffn2_norm"], "final_norm": wts["final_norm"].reshape(1, D_MODEL),
         "lbl": wts["hgrn_lb_logits"], "hgrn_out_norm": wts["hgrn_out_norm"], "gdn_out_norm": wts["gdn_out_norm"],
         "alog": _pad_lanes(wts["gdn_a_log"]), "dtb": _pad_lanes(wts["gdn_dt_bias"])}

    loss, dx, g = _local_step(x[0], loss_target[0], p, links)

    small_started, small_token = _copies_start([[(_pack_small(g, loss), "gather")]], "small_start")
    landed = links.landed(small_token)

    big = [{} for _ in range(4)]
    for n in BIG:
        res = _adam(landed[n], _shard_rows(n, wts[n][0]), _shard_rows(n, mom[n][0]), _shard_rows(n, var[n][0]), "adam_" + n)
        for kind in range(4):
            big[kind][n] = _shard_rows(n, res[kind])
    small_srcs, small_lands, _ = _copies_wait(small_started[0], ["gather"], res[0], "small_wait")
    small_parts = lax.dynamic_update_index_in_dim(small_lands[0], small_srcs[0], me, 0)
    n_vec = SMALL_ROWS - 4 * CONV_K
    small_raw = _adam(small_parts[:, :n_vec], _pack_small_state(wts), _pack_small_state(mom), _pack_small_state(var), "adam_small")
    small = [_unpack_small(o) for o in small_raw]
    loss_total = small_raw[0][6, 4 * HEAD]
    conv_parts = small_parts[:, n_vec:].reshape(N_DEV, CONV_K, 4 * D_MODEL)
    width = 4 * D_MODEL // N_DEV
    conv_mine = lax.dynamic_slice_in_dim(conv_parts, me * width, width, axis=2)
    conv = _adam(conv_mine, conv_shard, mom["gdn_conv_w"][0], var["gdn_conv_w"][0], "adam_conv")

    outs = []
    for kind in range(4):
        for n in NAMES:
            if n in BIG:
                outs.append(big[kind][n][None])
            elif n == "gdn_conv_w":
                outs.append(conv[kind][None])
            else:
                outs.append(small[kind][n])
    return (loss_total, dx[None], *outs)
```

```python
import functools

import jax
import jax.numpy as jnp
from jax import lax
from jax.experimental import pallas as pl
from jax.experimental.pallas import tpu as pltpu

F32 = jnp.float32
BF16 = jnp.bfloat16
HIGHEST = lax.Precision.HIGHEST
MESH_IDS = pl.DeviceIdType.MESH

D_MODEL = 1024
D_FF = 2816
N_DEV = 8
EPS = 1e-6
HEAD = 128
HG_HEADS = 8
GDN_QK_HEADS = 8
GDN_V_HEADS = 16
GDN_CHUNK = 64
HG_CHUNK = 16
CONV_K = 4
IN_WIDTH = 12320
COL_HQ, COL_HF, COL_HI, COL_HG, COL_GQ, COL_GK, COL_GV = 0, 8, 16, 24, 32, 40, 48
COL_GZ, COL_GATE_H, COL_GATE_G = 0, 16, 24
VMEM_LIMIT = 56 * 1024 * 1024

ADAM_LR, ADAM_B1, ADAM_B2, ADAM_EPS, ADAM_WD, ADAM_STEP = 0.001, 0.9, 0.999, 1e-08, 0.01, 10

SDS = jax.ShapeDtypeStruct


def _params(n_axes):
    return pltpu.CompilerParams(dimension_semantics=("arbitrary",) * n_axes, vmem_limit_bytes=VMEM_LIMIT)


def _tile(n, candidates=(512, 384, 256, 128, 64, 32, 16, 8)):
    for c in candidates:
        if n % c == 0:
            return c
    return n


_DIMS = {"nn": ((1,), (0,)), "nt": ((1,), (1,)), "tn": ((0,), (0,))}


def _bdot_raw(a, b, dims):
    return lax.dot_general(a.astype(BF16), b.astype(BF16), (_DIMS[dims], ((), ())), preferred_element_type=F32)


@functools.partial(jax.custom_vjp, nondiff_argnums=(2,))
def _bdot(a, b, dims):
    return _bdot_raw(a, b, dims)


def _bdot_fwd(a, b, dims):
    return _bdot_raw(a, b, dims), (a, b)


def _bdot_bwd(dims, res, ct):
    a, b = res
    if dims == "nn":
        return _bdot_raw(ct, b, "nt"), _bdot_raw(a, ct, "tn")
    if dims == "nt":
        return _bdot_raw(ct, b, "nn"), _bdot_raw(ct, a, "tn")
    return _bdot_raw(b, ct, "nt"), _bdot_raw(a, ct, "nn")


_bdot.defvjp(_bdot_fwd, _bdot_bwd)


def _hdot_raw(a, b):
    return jnp.dot(a, b, precision=HIGHEST, preferred_element_type=F32)


MM_VMEM_BUDGET = 38 * 1024 * 1024
TOKEN = (8, HEAD)


def _mm_tiles(m, n, k, a_bytes, b_bytes, o_bytes, r_bytes, m_align=8):
    def need(tm, tn, tk):
        return 2 * (tm * tk * a_bytes + tk * tn * b_bytes + tm * tn * (o_bytes + r_bytes)) + (tm * tn * 4 if tk < k else 0)

    def shrink(tm, tn, tk, floor_m, floor_n):
        while need(tm, tn, tk) > MM_VMEM_BUDGET:
            if tn > floor_n and tn % 256 == 0 and tn >= tm:
                tn //= 2
            elif tm > floor_m and tm % (2 * m_align) == 0:
                tm //= 2
            elif tn > floor_n and tn % 256 == 0:
                tn //= 2
            else:
                return None
        return tm, tn, tk

    tm = _tile(m, (1408, 1024, 704, 512, 256, 128, 64, 32, 16, 8))
    tn = _tile(n, (1408, 1024, 512, 256, 128))
    whole = shrink(tm, tn, k, min(tm, 1024), min(tn, 512))
    if whole is not None:
        return whole
    tk = _tile(k, (2048, 1408, 1024, 512, 256, 128, 64, 32, 16, 8))
    while True:
        fit = shrink(tm, tn, tk, min(tm, 256), min(tn, 512))
        if fit is not None or tk <= 512 or tk % 256:
            return fit if fit is not None else (tm, tn, tk)
        tk //= 2


def _mm(a, b, dims, out_dtype, name, res=None, alpha=1.0, after=None, b_rows=None):
    b_shape = b.shape if b_rows is None else (b_rows, b.shape[1])
    if dims == "nn":
        (m, k), (k2, n) = a.shape, b_shape
    elif dims == "nt":
        (m, k), (n, k2) = a.shape, b_shape
    else:
        (k, m), (k2, n) = a.shape, b_shape
    assert k == k2, (a.shape, b.shape, dims)
    has_res = res is not None
    tm, tn, tk = _mm_tiles(m, n, k, a.dtype.itemsize, b.dtype.itemsize, jnp.dtype(out_dtype).itemsize, res.dtype.itemsize if has_res else 0,
                           m_align=HEAD if dims == "tn" else 8)
    nk = k // tk
    a_spec = pl.BlockSpec((tk, tm), lambda i, j, kk: (kk, i)) if dims == "tn" else pl.BlockSpec((tm, tk), lambda i, j, kk: (i, kk))
    b_spec = pl.BlockSpec((tn, tk), lambda i, j, kk: (j, kk)) if dims == "nt" else pl.BlockSpec((tk, tn), lambda i, j, kk: (kk, j))
    o_spec = pl.BlockSpec((tm, tn), lambda i, j, kk: (i, j))

    def finish(acc, r_ref, o_ref):
        out = acc * alpha if alpha != 1.0 else acc
        if has_res:
            out = r_ref[...].astype(F32) + out
        o_ref[...] = out.astype(o_ref.dtype)

    n_in = 2 + has_res + (after is not None)

    def body(*refs):
        a_ref, b_ref = refs[:2]
        r_ref = refs[2] if has_res else None
        o_ref = refs[n_in]
        p = _bdot_raw(a_ref[...], b_ref[...], dims)
        if nk == 1:
            finish(p, r_ref, o_ref)
            return
        acc_ref = refs[-1]
        kk = pl.program_id(2)

        @pl.when(kk == 0)
        def _():
            acc_ref[...] = p

        @pl.when(kk > 0)
        def _():
            acc_ref[...] += p

        @pl.when(kk == nk - 1)
        def _():
            finish(acc_ref[...], r_ref, o_ref)

    args = (a, b) + ((res,) if has_res else ()) + ((after,) if after is not None else ())
    in_specs = [a_spec, b_spec] + ([o_spec] if has_res else []) + ([pl.BlockSpec(TOKEN, lambda i, j, kk: (0, 0))] if after is not None else [])
    return pl.pallas_call(
        body, grid=(m // tm, n // tn, nk), in_specs=in_specs, out_specs=o_spec, out_shape=SDS((m, n), out_dtype),
        scratch_shapes=[pltpu.VMEM((tm, tn), F32)] if nk > 1 else [], name=name, compiler_params=_params(3),
    )(*args)


PIECE_TK = 1024


def _mm_pieces(pieces, b, name, res=None, after=None):
    m, n = pieces[0].shape[0], b.shape[1]
    blocks = [p.shape[1] // PIECE_TK for p in pieces]
    assert all(p.shape[1] % PIECE_TK == 0 and p.shape[0] == m for p in pieces)
    starts = [sum(blocks[:i]) for i in range(len(pieces))]
    nk = sum(blocks)
    tm, tn = _tile(m, (1024, 512, 256, 128)), _tile(n, (1024, 512, 256, 128))
    n_p = len(pieces)
    n_in = n_p + 1 + (res is not None) + (after is not None)

    def piece_spec(start, count):
        return pl.BlockSpec((tm, PIECE_TK), lambda i, j, kk: (i, jnp.clip(kk - start, 0, count - 1)))

    def body(*refs):
        b_ref, o_ref, acc_ref = refs[n_p], refs[n_in], refs[-1]
        kk = pl.program_id(2)

        @pl.when(kk == 0)
        def _():
            acc_ref[...] = jnp.zeros_like(acc_ref)

        for p_ref, start, count in zip(refs[:n_p], starts, blocks):
            @pl.when(jnp.logical_and(kk >= start, kk < start + count))
            def _(p_ref=p_ref):
                acc_ref[...] += _bdot_raw(p_ref[...], b_ref[...], "nn")

        @pl.when(kk == nk - 1)
        def _():
            out = acc_ref[...]
            if res is not None:
                out = refs[n_p + 1][...] + out
            o_ref[...] = out

    o_spec = pl.BlockSpec((tm, tn), lambda i, j, kk: (i, j))
    in_specs = [piece_spec(s, c) for s, c in zip(starts, blocks)] + [pl.BlockSpec((PIECE_TK, tn), lambda i, j, kk: (kk, j))]
    args = list(pieces) + [b]
    if res is not None:
        in_specs.append(o_spec)
        args.append(res)
    if after is not None:
        in_specs.append(pl.BlockSpec(TOKEN, lambda i, j, kk: (0, 0)))
        args.append(after)
    return pl.pallas_call(
        body, grid=(m // tm, n // tn, nk), in_specs=in_specs, out_specs=o_spec, out_shape=SDS((m, n), F32),
        scratch_shapes=[pltpu.VMEM((tm, tn), F32)], name=name, compiler_params=_params(3),
    )(*args)


def _tmap(fn, grid, ins, outs, name):
    n_in = len(ins)
    n_ax = len(grid)

    def body(*refs):
        vals = fn(*[r[...] for r in refs[:n_in]])
        if not isinstance(vals, (tuple, list)):
            vals = (vals,)
        first_inner = pl.program_id(n_ax - 1) == 0
        first_all = first_inner
        for ax in range(n_ax - 1):
            first_all = jnp.logical_and(first_all, pl.program_id(ax) == 0)

        def put(ref, val, acc):
            val = val.astype(ref.dtype)
            if acc is None:
                ref[...] = val
                return
            first = first_inner if acc == "inner" else first_all

            @pl.when(first)
            def _():
                ref[...] = val

            @pl.when(jnp.logical_not(first))
            def _():
                ref[...] += val

        for ref, val, o in zip(refs[n_in:], vals, outs):
            put(ref, val, o[4])

    return pl.pallas_call(
        body, grid=grid,
        in_specs=[pl.BlockSpec(bs, im) for _, bs, im in ins],
        out_specs=[pl.BlockSpec(o[2], o[3]) for o in outs],
        out_shape=[SDS(o[0], o[1]) for o in outs],
        name=name, compiler_params=_params(n_ax),
    )(*[a for a, _, _ in ins])


def _rows(width, tt, off=0):
    return (tt, width), (lambda j, i: (i, off + j))


def _rms(x, g):
    x = x.astype(F32)
    return x * lax.rsqrt(jnp.mean(x * x, axis=-1, keepdims=True) + EPS) * g


def _sigmoid(x):
    return jax.nn.sigmoid(x)


def _silu(x):
    return x * _sigmoid(x)


def _softplus(x):
    return jnp.maximum(x, 0.0) + jnp.log1p(jnp.exp(-jnp.abs(x)))


def _rms_fwd(x, g, name):
    t, d = x.shape
    tt = _tile(t, (256, 128))
    return _tmap(_rms, (1, t // tt), [(x, *_rows(d, tt)), (g, (1, d), lambda j, i: (0, 0))],
                 [((t, d), BF16, *_rows(d, tt), None)], name)[0]


def _rms_bwd(x, g, dn, dres, name):
    t, d = x.shape
    tt = _tile(t, (256, 128))

    def fn(x, g, dn, dres):
        _, vjp = jax.vjp(_rms, x, g)
        dx, dg = vjp(dn.astype(F32))
        return dres + dx, dg

    return _tmap(fn, (1, t // tt),
                 [(x, *_rows(d, tt)), (g, (1, d), lambda j, i: (0, 0)), (dn, *_rows(d, tt)), (dres, *_rows(d, tt))],
                 [((t, d), F32, *_rows(d, tt), None), ((1, d), F32, (1, d), lambda j, i: (0, 0), "inner")], name)


def _swiglu(ab):
    return _silu(ab[:, :D_FF].astype(F32)) * ab[:, D_FF:].astype(F32)


def _swiglu_fwd(ab, name):
    t = ab.shape[0]
    tt = _tile(t, (128,))
    return _tmap(_swiglu, (1, t // tt), [(ab, *_rows(2 * D_FF, tt))], [((t, D_FF), BF16, *_rows(D_FF, tt), None)], name)[0]


def _swiglu_bwd(ab, ds, name):
    t = ab.shape[0]
    tt = _tile(t, (128,))

    def fn(ab, ds):
        a, b = ab[:, :D_FF].astype(F32), ab[:, D_FF:].astype(F32)
        _, vjp = jax.vjp(lambda a, b: _silu(a) * b, a, b)
        da, db = vjp(ds.astype(F32))
        return jnp.concatenate([da, db], axis=1)

    return _tmap(fn, (1, t // tt), [(ab, *_rows(2 * D_FF, tt)), (ds, *_rows(D_FF, tt))],
                 [((t, 2 * D_FF), BF16, *_rows(2 * D_FF, tt), None)], name)[0]


def _ffn_fwd(h, g, weights, tag):
    n = _rms_fwd(h, g, tag + "_norm")
    w_in_t, w_out, after = weights(n)
    ab = _mm(n, w_in_t, "nt", BF16, tag + "_in", after=after)
    s = _swiglu_fwd(ab, tag + "_act")
    out = _mm(s, w_out, "nn", F32, tag + "_out", res=h, alpha=0.5)
    return out, (n, ab, s, w_in_t, w_out)


def _ffn_bwd(h, g, saved, dout, tag, links):
    n, ab, s, w_in_t, w_out = saved
    sent = links.send({tag + "_w_out": _mm(s, dout, "tn", BF16, tag + "_dw_out", alpha=0.5)})
    ds = _mm(dout, w_out, "nt", BF16, tag + "_ds", alpha=0.5, after=sent)
    dab = _swiglu_bwd(ab, ds, tag + "_dact")
    sent = links.send({tag + "_w_in": _mm(dab, n, "tn", BF16, tag + "_dw_in")})
    dn = _mm(dab, w_in_t, "nn", F32, tag + "_dn", after=sent)
    return _rms_bwd(h, g, dn, dout, tag + "_dnorm")


def _chunk_sum_matrix(n, chunk, transpose=False):
    row = lax.broadcasted_iota(jnp.int32, (n, n), 0)
    col = lax.broadcasted_iota(jnp.int32, (n, n), 1)
    if transpose:
        row, col = col, row
    return jnp.where(jnp.logical_and(col <= row, row // chunk == col // chunk), 1.0, 0.0).astype(F32)


def _hgrn_gates(hq, hf, lbl):
    lb = _sigmoid(lbl[0:1, :] - lbl[1:2, :])
    sg = _sigmoid(hf)
    f = lb + (1.0 - lb) * sg
    q = _silu(hq) * HEAD ** -0.5
    k = (1.0 - lb) * (1.0 - sg)
    return q, k, jnp.log(f)


def _hgrn_prep_fwd(proj, lbl):
    t = proj.shape[0]
    tt, ft = _tile(t, (256, 128)), 512

    def fn(hq, hf, lbl):
        q, k, log_f = _hgrn_gates(hq, hf, lbl)
        return q, k, _hdot_raw(_chunk_sum_matrix(tt, HG_CHUNK), log_f)

    o = ((t, D_MODEL), F32, *_rows(ft, tt), None)
    return _tmap(fn, (D_MODEL // ft, t // tt),
                 [(proj, *_rows(ft, tt, COL_HQ * HEAD // ft)), (proj, *_rows(ft, tt, COL_HF * HEAD // ft)), (lbl, (2, ft), lambda j, i: (0, j))],
                 [o, o, o], "hgrn_prep")


def _hgrn_prep_bwd(proj, lbl, dq, dk, db):
    t = proj.shape[0]
    tt, ft = _tile(t, (256, 128)), 512

    def fn(hq, hf, lbl, dq, dk, db):
        dlog_f = _hdot_raw(_chunk_sum_matrix(tt, HG_CHUNK, transpose=True), db)
        _, vjp = jax.vjp(_hgrn_gates, hq, hf, lbl)
        return vjp((dq, dk, dlog_f))

    o = ((t, D_MODEL), BF16, *_rows(ft, tt), None)
    r = _rows(ft, tt)
    return _tmap(fn, (D_MODEL // ft, t // tt),
                 [(proj, *_rows(ft, tt, COL_HQ * HEAD // ft)), (proj, *_rows(ft, tt, COL_HF * HEAD // ft)), (lbl, (2, ft), lambda j, i: (0, j)),
                  (dq, *r), (dk, *r), (db, *r)],
                 [o, o, ((2, D_MODEL), F32, (2, ft), lambda j, i: (0, j), "inner")], "hgrn_prep_bwd")


def _hgrn_chunks(q, k, v, b, st):
    n = q[0].shape[0]
    half = n // 2
    srow = lax.broadcasted_iota(jnp.int32, (half, HEAD), 0)
    inter = _each(lambda q, b, st: _bdot(q * jnp.exp(b), st, "nt"), q, b, st)

    def below_scores(q, k, b):
        ref = b[half:half + 1, :]
        return _bdot(q[half:] * jnp.exp(jnp.minimum(b[half:] - ref, 0.0)), k[:half] * jnp.exp(jnp.minimum(ref - b[:half], 0.0)), "nt")

    below = _each(lambda a, v: _bdot(a, v[:half], "nn"), _each(below_scores, q, k, b), v)

    def diagonal(q, k, v, b):
        rows = []
        for lo in (0, half):
            qb, kb, vb, bb = (a[lo:lo + half] for a in (q, k, v, b))
            for t in range(half):
                e = jnp.where(srow <= t, jnp.exp(jnp.minimum(bb[t:t + 1, :] - bb, 0.0)), 0.0)
                a = jnp.sum(qb[t:t + 1, :] * kb * e, axis=1, keepdims=True)
                rows.append(jnp.sum(a * vb, axis=0, keepdims=True))
        return jnp.concatenate(rows, axis=0)

    diag = _each(diagonal, q, k, v, b)
    o = _each(lambda inter, diag, below: inter + diag + jnp.concatenate([jnp.zeros_like(below), below], axis=0), inter, diag, below)

    def new_state(k, v, b, st):
        bend = b[n - 1:n, :]
        return st * jnp.exp(bend) + _bdot(v, k * jnp.exp(bend - b), "tn")

    return o, _each(new_state, k, v, b, st)


HG_GROUP = 8
HG_PER = GDN_CHUNK // HG_CHUNK


def _hgrn_rec_fwd(q, k, proj, b):
    t = q.shape[0]
    nc = t // GDN_CHUNK
    blk = (GDN_CHUNK, HG_GROUP * HEAD)
    im = lambda h, c: (c, h)

    def body(q_ref, k_ref, v_ref, b_ref, o_ref, hs_ref, st_ref):
        @pl.when(pl.program_id(1) == 0)
        def _():
            st_ref[...] = jnp.zeros_like(st_ref)

        heads = range(HG_GROUP)
        for j in range(HG_PER):
            sl = pl.ds(HG_CHUNK * j, HG_CHUNK)
            st = tuple(st_ref[g] for g in heads)
            o, st_new = _hgrn_chunks(*[tuple(r[sl, _head_lanes(g)] for g in heads) for r in (q_ref, k_ref, v_ref, b_ref)], st)
            for g in heads:
                hs_ref[g, j] = st[g]
                o_ref[sl, _head_lanes(g)] = o[g]
                st_ref[g] = st_new[g]

    return pl.pallas_call(
        body, grid=(HG_HEADS // HG_GROUP, nc),
        in_specs=[pl.BlockSpec(blk, im), pl.BlockSpec(blk, im), pl.BlockSpec(blk, lambda h, c: (c, COL_HI // HG_GROUP + h)), pl.BlockSpec(blk, im)],
        out_specs=[pl.BlockSpec(blk, im), pl.BlockSpec((HG_GROUP, HG_PER, HEAD, HEAD), lambda h, c: (h, c, 0, 0))],
        out_shape=[SDS((t, D_MODEL), F32), SDS((HG_HEADS, nc * HG_PER, HEAD, HEAD), F32)],
        scratch_shapes=[pltpu.VMEM((HG_GROUP, HEAD, HEAD), F32)], name="hgrn_rec", compiler_params=_params(2),
    )(q, k, proj, b)


def _hgrn_rec_bwd(q, k, proj, b, hs, do):
    t = q.shape[0]
    nc = t // GDN_CHUNK
    blk = (GDN_CHUNK, HG_GROUP * HEAD)
    im = lambda h, c: (nc - 1 - c, h)

    def body(q_ref, k_ref, v_ref, b_ref, hs_ref, do_ref, dq_ref, dk_ref, dv_ref, db_ref, dst_ref):
        @pl.when(pl.program_id(1) == 0)
        def _():
            dst_ref[...] = jnp.zeros_like(dst_ref)

        heads = range(HG_GROUP)
        for j in reversed(range(HG_PER)):
            sl = pl.ds(HG_CHUNK * j, HG_CHUNK)
            _, vjp = jax.vjp(_hgrn_chunks, *[tuple(r[sl, _head_lanes(g)] for g in heads) for r in (q_ref, k_ref, v_ref, b_ref)],
                             tuple(hs_ref[g, j] for g in heads))
            dq, dk, dv, db, dst = vjp((tuple(do_ref[sl, _head_lanes(g)] for g in heads), tuple(dst_ref[g] for g in heads)))
            for g in heads:
                ln = _head_lanes(g)
                dq_ref[sl, ln] = dq[g]
                dk_ref[sl, ln] = dk[g]
                dv_ref[sl, ln] = dv[g].astype(dv_ref.dtype)
                db_ref[sl, ln] = db[g]
                dst_ref[g] = dst[g]

    spec = pl.BlockSpec(blk, im)
    return pl.pallas_call(
        body, grid=(HG_HEADS // HG_GROUP, nc),
        in_specs=[spec, spec, pl.BlockSpec(blk, lambda h, c: (nc - 1 - c, COL_HI // HG_GROUP + h)), spec,
                  pl.BlockSpec((HG_GROUP, HG_PER, HEAD, HEAD), lambda h, c: (h, nc - 1 - c, 0, 0)), spec],
        out_specs=[spec, spec, spec, spec],
        out_shape=[SDS((t, D_MODEL), F32), SDS((t, D_MODEL), F32), SDS((t, D_MODEL), BF16), SDS((t, D_MODEL), F32)],
        scratch_shapes=[pltpu.VMEM((HG_GROUP, HEAD, HEAD), F32)], name="hgrn_rec_bwd", compiler_params=_params(2),
    )(q, k, proj, b, hs, do)


def _shift_down(x, d):
    if d == 0:
        return x
    row = lax.broadcasted_iota(jnp.int32, x.shape, 0)
    return jnp.where(row >= d, pltpu.roll(x, d, 0), 0.0)


def _shift_up(x, d):
    if d == 0:
        return x
    n = x.shape[0]
    row = lax.broadcasted_iota(jnp.int32, x.shape, 0)
    return jnp.where(row < n - d, pltpu.roll(x, n - d, 0), 0.0)


def _conv_fwd(proj, conv_w):
    t = proj.shape[0]
    width = 2 * D_MODEL + 2 * D_MODEL

    def body(x_ref, w_ref, c_ref, y_ref):
        x, w = x_ref[...], w_ref[...]
        y = w[CONV_K - 1:CONV_K, :] * x
        for j in range(CONV_K - 1):
            y = y + w[j:j + 1, :] * _shift_down(x, CONV_K - 1 - j)
        y_ref[...] = y
        c_ref[...] = _silu(y)

    out = pl.BlockSpec((t, HEAD), lambda j: (0, j))
    return pl.pallas_call(
        body, grid=(width // HEAD,),
        in_specs=[pl.BlockSpec((t, HEAD), lambda j: (0, COL_GQ + j)), pl.BlockSpec((CONV_K, HEAD), lambda j: (0, j))],
        out_specs=[out, out], out_shape=[SDS((t, width), F32), SDS((t, width), F32)],
        name="gdn_conv", compiler_params=_params(1),
    )(proj, conv_w)


def _conv_bwd(proj, conv_w, y, dc_qk, dc_v):
    t = proj.shape[0]
    n_qk = dc_qk.shape[1] // HEAD
    width = dc_qk.shape[1] + dc_v.shape[1]

    def body(x_ref, w_ref, y_ref, dqk_ref, dv_ref, dx_ref, dw_ref):
        x, w, y = x_ref[...], w_ref[...], y_ref[...]
        sg = _sigmoid(y)
        dc = jnp.where(pl.program_id(0) < n_qk, dqk_ref[...], dv_ref[...])
        dy = dc * (sg * (1.0 + y * (1.0 - sg)))
        ahead = [_shift_up(dy, CONV_K - 1 - j) for j in range(CONV_K)]
        dx = w[0:1, :] * ahead[0]
        for j in range(1, CONV_K):
            dx = dx + w[j:j + 1, :] * ahead[j]
        dx_ref[...] = dx.astype(dx_ref.dtype)
        dw_ref[...] = jnp.concatenate([jnp.sum(x * ahead[j], axis=0, keepdims=True) for j in range(CONV_K)], axis=0)

    blk = pl.BlockSpec((t, HEAD), lambda j: (0, j))
    return pl.pallas_call(
        body, grid=(width // HEAD,),
        in_specs=[pl.BlockSpec((t, HEAD), lambda j: (0, COL_GQ + j)), pl.BlockSpec((CONV_K, HEAD), lambda j: (0, j)), blk,
                  pl.BlockSpec((t, HEAD), lambda j: (0, jnp.minimum(j, n_qk - 1))), pl.BlockSpec((t, HEAD), lambda j: (0, jnp.maximum(j - n_qk, 0)))],
        out_specs=[blk, pl.BlockSpec((CONV_K, HEAD), lambda j: (0, j))],
        out_shape=[SDS((t, width), BF16), SDS((CONV_K, width), F32)],
        name="gdn_conv_bwd", compiler_params=_params(1),
    )(proj, conv_w, y, dc_qk, dc_v)


def _l2norm(x, scale):
    return x * lax.rsqrt(jnp.sum(x * x, axis=-1, keepdims=True) + EPS) * scale


def _head(a, h):
    return a[:, h * HEAD:(h + 1) * HEAD]


def _qk_scale(h):
    return HEAD ** -0.5 if h < GDN_QK_HEADS else 1.0


def _qk_norm_fwd(c):
    t = c.shape[0]
    tt = _tile(t, (256, 128))
    width = 2 * D_MODEL

    def fn(x):
        return jnp.concatenate([_l2norm(_head(x, h), _qk_scale(h)) for h in range(2 * GDN_QK_HEADS)], axis=1)

    return _tmap(fn, (1, t // tt), [(c, *_rows(width, tt))], [((t, width), F32, *_rows(width, tt), None)], "gdn_qk_norm")[0]


def _qk_norm_bwd(c, dq_rep, dk_rep):
    t = c.shape[0]
    tt = _tile(t, (256, 128))
    width = 2 * D_MODEL

    def fn(x, dq2, dk2):
        out = []
        for h in range(2 * GDN_QK_HEADS):
            d2, hh = (dq2, h) if h < GDN_QK_HEADS else (dk2, h - GDN_QK_HEADS)
            _, vjp = jax.vjp(lambda x: _l2norm(x, _qk_scale(h)), _head(x, h))
            out.append(vjp(_head(d2, 2 * hh) + _head(d2, 2 * hh + 1))[0])
        return jnp.concatenate(out, axis=1)

    r = _rows(width, tt)
    return _tmap(fn, (1, t // tt), [(c, *r), (dq_rep, *r), (dk_rep, *r)], [((t, width), F32, *r, None)], "gdn_qk_norm_bwd")[0]


def _gdn_gates(x, alog, dtb):
    return -jnp.exp(alog) * _softplus(x + dtb), _sigmoid(x)


def _gates_fwd(pab, alog, dtb):
    t = pab.shape[0]
    tt = _tile(t, (256, 128))

    def fn(x, alog, dtb):
        g, beta = _gdn_gates(x, alog, dtb)
        lane = lax.broadcasted_iota(jnp.int32, g.shape, 1)
        return jnp.where(lane < GDN_V_HEADS, _hdot_raw(_chunk_sum_matrix(tt, GDN_CHUNK), g), beta).T

    p = (alog, (1, HEAD), lambda j, i: (0, 0)), (dtb, (1, HEAD), lambda j, i: (0, 0))
    return _tmap(fn, (1, t // tt), [(pab, *_rows(HEAD, tt)), *p], [((HEAD, t), F32, (HEAD, tt), lambda j, i: (0, i), None)], "gdn_gates")[0]


def _gates_bwd(pab, alog, dtb, dout_t):
    t = pab.shape[0]
    tt = _tile(t, (256, 128))

    def fn(x, alog, dtb, dout_t):
        dout = dout_t.T
        lane = lax.broadcasted_iota(jnp.int32, dout.shape, 1)
        dgam = jnp.where(lane < GDN_V_HEADS, dout, 0.0)
        dbeta = jnp.where(jnp.logical_and(lane >= GDN_V_HEADS, lane < 2 * GDN_V_HEADS), dout, 0.0)
        dg = _hdot_raw(_chunk_sum_matrix(tt, GDN_CHUNK, transpose=True), dgam)
        _, vjp = jax.vjp(_gdn_gates, x, alog, dtb)
        return vjp((dg, dbeta))

    p = (alog, (1, HEAD), lambda j, i: (0, 0)), (dtb, (1, HEAD), lambda j, i: (0, 0))
    acc = ((1, HEAD), F32, (1, HEAD), lambda j, i: (0, 0), "inner")
    return _tmap(fn, (1, t // tt), [(pab, *_rows(HEAD, tt)), *p, (dout_t, (HEAD, tt), lambda j, i: (0, i))],
                 [((t, HEAD), BF16, *_rows(HEAD, tt), None), acc, acc], "gdn_gates_bwd")


def _split_bf16(x):
    hi = x.astype(BF16)
    return hi, (x - hi.astype(F32)).astype(BF16)


def _dot3(a, b):
    (ah, al), (bh, bl) = a, b
    return _bdot_raw(ah, bh, "nn") + (_bdot_raw(ah, bl, "nn") + _bdot_raw(al, bh, "nn"))


def _each(fn, *lists):
    return tuple(fn(*xs) for xs in zip(*lists))


def _unit_lower_inverses_raw(a):
    n = a[0].shape[0]
    row = lax.broadcasted_iota(jnp.int32, (n, n), 0)
    col = lax.broadcasted_iota(jnp.int32, (n, n), 1)
    eye = jnp.where(row == col, 1.0, 0.0).astype(F32)
    p = _each(lambda a: eye - a, a)
    x = _each(_split_bf16, a)
    m = 2
    while m < n:
        x = _each(_split_bf16, _each(_dot3, x, x))
        p = _each(lambda p, x: p + _bdot_raw(p, x[0], "nn"), p, x)
        m *= 2
    return p


@jax.custom_vjp
def _unit_lower_inverses(a, known):
    return _unit_lower_inverses_raw(a) if known is None else known


def _uli_fwd(a, known):
    inv = _unit_lower_inverses(a, known)
    return inv, (inv, known)


def _uli_bwd(res, ct):
    inv, known = res
    right = _each(lambda ct, inv: _bdot_raw(ct, inv, "nt"), ct, inv)
    da = _each(lambda inv, r: -_bdot_raw(inv, r, "tn"), inv, right)
    return da, (None if known is None else _each(jnp.zeros_like, known))


_unit_lower_inverses.defvjp(_uli_fwd, _uli_bwd)


def _gdn_chunks(q, k, v, beta_rows, gam_rows, s, inv_known=None):
    n = q[0].shape[0]
    heads = range(len(q))
    row = lax.broadcasted_iota(jnp.int32, (n, n), 0)
    col = lax.broadcasted_iota(jnp.int32, (n, n), 1)
    beta_cols, gam_cols = beta_rows.T, gam_rows.T
    beta = tuple(beta_cols[:, g:g + 1] for g in heads)
    gam = tuple(gam_cols[:, g:g + 1] for g in heads)
    gam_row = tuple(gam_rows[g:g + 1, :] for g in heads)
    decay = _each(lambda gam, gam_row: jnp.where(row >= col, jnp.exp(jnp.minimum(gam - gam_row, 0.0)), 0.0), gam, gam_row)
    kb = _each(lambda k, beta: k * beta, k, beta)
    a = _each(lambda kb, k, decay: jnp.where(row > col, _bdot(kb, k, "nt") * decay, 0.0), kb, k, decay)
    inv = _unit_lower_inverses(a, inv_known)
    eg = _each(jnp.exp, gam)
    u = _each(lambda inv, v, beta: _bdot(inv, v * beta, "nn"), inv, v, beta)
    w = _each(lambda inv, kb, eg: _bdot(inv, kb * eg, "nn"), inv, kb, eg)
    qk = _each(lambda q, k, decay: _bdot(q, k, "nt") * decay, q, k, decay)
    v_new = _each(lambda u, w, s: u - _bdot(w, s, "nn"), u, w, s)
    o_state = _each(lambda q, eg, s: _bdot(q * eg, s, "nn"), q, eg, s)
    o = _each(lambda o_state, qk, v_new: o_state + _bdot(qk, v_new, "nn"), o_state, qk, v_new)
    gend = _each(lambda gam: gam[n - 1:n, :], gam)
    s_new = _each(lambda s, k, gam, gend, v_new: s * jnp.exp(gend) + _bdot(k * jnp.exp(gend - gam), v_new, "tn"), s, k, gam, gend, v_new)
    return o, s_new, inv


GDN_GROUP = 16


def _gdn_specs(nc, rev):
    cc = (lambda c: nc - 1 - c) if rev else (lambda c: c)
    grp = GDN_GROUP
    q = pl.BlockSpec((GDN_CHUNK, grp // 2 * HEAD), lambda h, c: (cc(c), h))
    k = pl.BlockSpec((GDN_CHUNK, grp // 2 * HEAD), lambda h, c: (cc(c), 2 * GDN_QK_HEADS // grp + h))
    v = pl.BlockSpec((GDN_CHUNK, grp * HEAD), lambda h, c: (cc(c), 2 * GDN_QK_HEADS // grp + h))
    o = pl.BlockSpec((GDN_CHUNK, grp * HEAD), lambda h, c: (cc(c), h))
    rw = pl.BlockSpec((grp, None, 1, GDN_CHUNK), lambda h, c: (h, cc(c), 0, 0))
    st = pl.BlockSpec((grp, None, HEAD, HEAD), lambda h, c: (h, cc(c), 0, 0))
    inv = pl.BlockSpec((grp, None, GDN_CHUNK, GDN_CHUNK), lambda h, c: (h, cc(c), 0, 0))
    return q, k, v, o, rw, st, inv


def _head_lanes(g, per=1):
    return pl.ds((g // per) * HEAD, HEAD)


def _gdn_rec_fwd(qk, c, beta_row, gam_row):
    t = qk.shape[0]
    nc = t // GDN_CHUNK
    q, k, v, o, rw, st, inv = _gdn_specs(nc, False)

    def body(q_ref, k_ref, v_ref, be_ref, gr_ref, o_ref, ss_ref, inv_ref, s_ref):
        @pl.when(pl.program_id(1) == 0)
        def _():
            s_ref[...] = jnp.zeros_like(s_ref)

        heads = range(GDN_GROUP)
        s = tuple(s_ref[g] for g in heads)
        out, s_new, inv_c = _gdn_chunks(
            tuple(q_ref[:, _head_lanes(g, 2)] for g in heads), tuple(k_ref[:, _head_lanes(g, 2)] for g in heads),
            tuple(v_ref[:, _head_lanes(g)] for g in heads), be_ref[:, 0, :], gr_ref[:, 0, :], s)
        for g in heads:
            ss_ref[g] = s[g]
            o_ref[:, _head_lanes(g)] = out[g]
            inv_ref[g] = inv_c[g]
            s_ref[g] = s_new[g]

    return pl.pallas_call(
        body, grid=(GDN_V_HEADS // GDN_GROUP, nc), in_specs=[q, k, v, rw, rw], out_specs=[o, st, inv],
        out_shape=[SDS((t, 2 * D_MODEL), F32), SDS((GDN_V_HEADS, nc, HEAD, HEAD), F32), SDS((GDN_V_HEADS, nc, GDN_CHUNK, GDN_CHUNK), F32)],
        scratch_shapes=[pltpu.VMEM((GDN_GROUP, HEAD, HEAD), F32)], name="gdn_rec", compiler_params=_params(2),
    )(qk, qk, c, beta_row, gam_row)


def _gdn_rec_bwd(qk, c, beta_row, gam_row, ss, invs, do):
    t = qk.shape[0]
    nc = t // GDN_CHUNK
    q, k, v, o, rw, st, inv = _gdn_specs(nc, True)

    def body(q_ref, k_ref, v_ref, be_ref, gr_ref, ss_ref, inv_ref, do_ref,
             dq_ref, dk_ref, dv_ref, dbe_ref, dgr_ref, ds_ref):
        @pl.when(pl.program_id(1) == 0)
        def _():
            ds_ref[...] = jnp.zeros_like(ds_ref)

        heads = range(GDN_GROUP)
        _, vjp = jax.vjp(
            _gdn_chunks,
            tuple(q_ref[:, _head_lanes(g, 2)] for g in heads), tuple(k_ref[:, _head_lanes(g, 2)] for g in heads),
            tuple(v_ref[:, _head_lanes(g)] for g in heads), be_ref[:, 0, :], gr_ref[:, 0, :],
            tuple(ss_ref[g] for g in heads), tuple(inv_ref[g] for g in heads))
        no_inv_ct = tuple(jnp.zeros((GDN_CHUNK, GDN_CHUNK), F32) for g in heads)
        dq, dk, dv, dbe, dgr, ds, _ = vjp((tuple(do_ref[:, _head_lanes(g)] for g in heads), tuple(ds_ref[g] for g in heads), no_inv_ct))
        for g in heads:
            dq_ref[:, _head_lanes(g)] = dq[g]
            dk_ref[:, _head_lanes(g)] = dk[g]
            dv_ref[:, _head_lanes(g)] = dv[g]
            ds_ref[g] = ds[g]
        dbe_ref[:, 0, :] = dbe
        dgr_ref[:, 0, :] = dgr

    wide = SDS((t, 2 * D_MODEL), F32)
    rowshape = SDS((GDN_V_HEADS, nc, 1, GDN_CHUNK), F32)
    return pl.pallas_call(
        body, grid=(GDN_V_HEADS // GDN_GROUP, nc), in_specs=[q, k, v, rw, rw, st, inv, o], out_specs=[o, o, o, rw, rw],
        out_shape=[wide, wide, wide, rowshape, rowshape],
        scratch_shapes=[pltpu.VMEM((GDN_GROUP, HEAD, HEAD), F32)], name="gdn_rec_bwd", compiler_params=_params(2),
    )(qk, qk, c, beta_row, gam_row, ss, invs, do)


def _gated_norm(o, gate, w):
    return _rms(o, w) * _silu(gate)


def _post_fwd(o, proj, col_off, w, name):
    t, width = o.shape
    tt = _tile(t, (256, 128))

    def fn(o, gate, w):
        return jnp.concatenate([_gated_norm(_head(o, h), _head(gate, h), w) for h in range(width // HEAD)], axis=1)

    return _tmap(fn, (1, t // tt),
                 [(o, *_rows(width, tt)), (proj, *_rows(width, tt, col_off * HEAD // width)), (w, (1, HEAD), lambda j, i: (0, 0))],
                 [((t, width), BF16, *_rows(width, tt), None)], name)[0]


def _post_bwd(o, proj, col_off, w, dout, name):
    t, width = o.shape
    tt = _tile(t, (256, 128))

    def fn(o, gate, w, dout):
        do, dgate, dw = [], [], jnp.zeros((1, HEAD), F32)
        for h in range(width // HEAD):
            _, vjp = jax.vjp(_gated_norm, _head(o, h), _head(gate, h), w)
            a, b, c = vjp(_head(dout, h))
            do.append(a)
            dgate.append(b)
            dw = dw + c
        return jnp.concatenate(do, axis=1), jnp.concatenate(dgate, axis=1), dw

    r = _rows(width, tt)
    return _tmap(fn, (1, t // tt),
                 [(o, *r), (proj, *_rows(width, tt, col_off * HEAD // width)), (w, (1, HEAD), lambda j, i: (0, 0)), (dout, *r)],
                 [((t, width), F32, *r, None), ((t, width), BF16, *r, None), ((1, HEAD), F32, (1, HEAD), lambda j, i: (0, 0), "inner")], name)


def _merge(gate_h, gate_g, yh, yg):
    return _sigmoid(gate_h) * yh + _sigmoid(gate_g) * yg


def _merge_fwd(proj, yh, yg):
    t = yh.shape[0]
    tt, ft = _tile(t, (256, 128)), 512
    r = _rows(ft, tt)
    return _tmap(_merge, (D_MODEL // ft, t // tt),
                 [(proj, *_rows(ft, tt, COL_GATE_H * HEAD // ft)), (proj, *_rows(ft, tt, COL_GATE_G * HEAD // ft)), (yh, *r), (yg, *r)],
                 [((t, D_MODEL), BF16, *r, None)], "merge")[0]


def _merge_bwd(proj, yh, yg, dy):
    t = yh.shape[0]
    tt, ft = _tile(t, (256, 128)), 512
    r = _rows(ft, tt)

    def fn(gate_h, gate_g, yh, yg, dy):
        _, vjp = jax.vjp(_merge, gate_h, gate_g, yh, yg)
        return vjp(dy)

    o = ((t, D_MODEL), BF16, *r, None)
    return _tmap(fn, (D_MODEL // ft, t // tt),
                 [(proj, *_rows(ft, tt, COL_GATE_H * HEAD // ft)), (proj, *_rows(ft, tt, COL_GATE_G * HEAD // ft)), (yh, *r), (yg, *r), (dy, *r)],
                 [o, o, o, o], "merge_bwd")


def _loss_head(h, target, g):
    t, d = h.shape
    tt = _tile(t, (256, 128))

    def fn(h, target, g):
        def f(h, g):
            err = _rms(h, g) - target
            return 0.5 * jnp.sum(jnp.mean(err * err, axis=-1))

        loss, (dh, dg) = jax.value_and_grad(f, (0, 1))(h, g)
        return dh, dg, jnp.full((1, HEAD), loss, F32)

    return _tmap(fn, (1, t // tt), [(h, *_rows(d, tt)), (target, *_rows(d, tt)), (g, (1, d), lambda j, i: (0, 0))],
                 [((t, d), F32, *_rows(d, tt), None), ((1, d), F32, (1, d), lambda j, i: (0, 0), "inner"),
                  ((1, HEAD), F32, (1, HEAD), lambda j, i: (0, 0), "inner")], "loss_head")


def _mixer_fwd(h, p, links):
    t = h.shape[0]
    nc = t // GDN_CHUNK
    u = _rms_fwd(h, p["mix_norm"], "mix_norm")
    w = {n: links.weight(n, h) for n in ("w_in_t", "w_in_b_t", "w_in_ab_t", "conv_w")}
    proj = _mm(u, w["w_in_t"], "nt", F32, "mix_in", after=links.started, b_rows=SCALAR_ROWS)
    proj_b = _mm(u, w["w_in_b_t"], "nt", F32, "mix_in_b")
    pab = _mm(u, w["w_in_ab_t"], "nt", F32, "mix_in_ab")
    qh, kh, bh = _hgrn_prep_fwd(proj, p["lbl"])
    oh, hs = _hgrn_rec_fwd(qh, kh, proj, bh)
    c, conv_y = _conv_fwd(proj, w["conv_w"])
    qk = _qk_norm_fwd(c)
    gates_t = _gates_fwd(pab, p["alog"], p["dtb"])
    gam_row = gates_t[:GDN_V_HEADS].reshape(GDN_V_HEADS, nc, 1, GDN_CHUNK)
    beta_row = gates_t[GDN_V_HEADS:2 * GDN_V_HEADS].reshape(GDN_V_HEADS, nc, 1, GDN_CHUNK)
    og, ss, invs = _gdn_rec_fwd(qk, c, beta_row, gam_row)
    ohn = _post_fwd(oh, proj, COL_HG, p["hgrn_out_norm"], "hgrn_out")
    ogn = _post_fwd(og, proj_b, COL_GZ, p["gdn_out_norm"], "gdn_out")
    w.update({n: links.weight(n, ogn) for n in ("w_branch_hgrn", "w_branch_gdn", "w_out")})
    yh = _mm(ohn, w["w_branch_hgrn"], "nn", BF16, "branch_hgrn")
    yg = _mm(ogn, w["w_branch_gdn"], "nn", BF16, "branch_gdn")
    y = _merge_fwd(proj_b, yh, yg)
    out = _mm(y, w["w_out"], "nn", F32, "mix_out", res=h)
    saved = (w, u, proj, proj_b, pab, qh, kh, bh, oh, hs, c, conv_y, qk, beta_row, gam_row, og, ss, invs, ohn, ogn, yh, yg, y)
    return out, saved


def _mixer_bwd(h, p, links, saved, dout):
    (w, u, proj, proj_b, pab, qh, kh, bh, oh, hs, c, conv_y, qk, beta_row, gam_row, og, ss, invs, ohn, ogn, yh, yg, y) = saved
    t = h.shape[0]
    grads = {}
    dw_out = _mm(y, dout, "tn", BF16, "mix_out_dw")
    dy = _mm(dout, w["w_out"], "nt", F32, "mix_out_dx")
    dgate_h, dgate_g, dyh, dyg = _merge_bwd(proj_b, yh, yg, dy)
    dw_bh = _mm(ohn, dyh, "tn", BF16, "branch_hgrn_dw")
    dw_bg = _mm(ogn, dyg, "tn", BF16, "branch_gdn_dw")
    sent = links.send({"w_out": dw_out, "w_branch_hgrn": dw_bh, "w_branch_gdn": dw_bg})
    dohn = _mm(dyh, w["w_branch_hgrn"], "nt", F32, "branch_hgrn_dx", after=sent)
    dogn = _mm(dyg, w["w_branch_gdn"], "nt", F32, "branch_gdn_dx")
    doh, dhg, grads["hgrn_out_norm"] = _post_bwd(oh, proj, COL_HG, p["hgrn_out_norm"], dohn, "hgrn_out_bwd")
    dog, dgz, grads["gdn_out_norm"] = _post_bwd(og, proj_b, COL_GZ, p["gdn_out_norm"], dogn, "gdn_out_bwd")
    dqh, dkh, dhi, dbh = _hgrn_rec_bwd(qh, kh, proj, bh, hs, doh)
    dhq, dhf, grads["lbl"] = _hgrn_prep_bwd(proj, p["lbl"], dqh, dkh, dbh)
    dqv, dkv, dcv, dbeta_row, dgam_row = _gdn_rec_bwd(qk, c, beta_row, gam_row, ss, invs, dog)
    dcqk = _qk_norm_bwd(c, dqv, dkv)
    dxin, grads["conv_w"] = _conv_bwd(proj, w["conv_w"], conv_y, dcqk, dcv)
    dgates_t = jnp.concatenate([dgam_row.reshape(GDN_V_HEADS, t), dbeta_row.reshape(GDN_V_HEADS, t),
                                jnp.zeros((HEAD - 2 * GDN_V_HEADS, t), F32)], axis=0)
    dpab, grads["alog"], grads["dtb"] = _gates_bwd(pab, p["alog"], p["dtb"], dgates_t)
    front, back = [dhq, dhf, dhi, dhg, dxin], [dgz, dgate_h, dgate_g]
    dw_front = [_mm(d, u, "tn", BF16, "mix_in_dw_%d" % i) for i, d in enumerate(front)]
    dw_back = [_mm(d, u, "tn", BF16, "mix_in_b_dw_%d" % i) for i, d in enumerate(back)]
    dw_ab_t = _mm(dpab, u, "tn", BF16, "mix_in_ab_dw")
    sent = links.send({"w_in": jnp.concatenate(dw_front + [dw_ab_t[:N_SCALAR]] + dw_back, axis=0)})
    du = _mm_pieces(front, w["w_in_t"], "mix_in_dx", after=sent)
    du = _mm_pieces(back, w["w_in_b_t"], "mix_in_b_dx", res=du)
    du = _mm(dpab, w["w_in_ab_t"], "nn", F32, "mix_in_ab_dx", res=du)
    dh, grads["mix_norm"] = _rms_bwd(h, p["mix_norm"], du, dout, "mix_norm_bwd")
    return dh, grads


def _local_step(x, target, p, links):
    def ffn_weights(tag, behind):
        def get(n):
            w_in_t, w_out = links.weight(tag + "_w_in", n), links.weight(tag + "_w_out", n)
            return w_in_t, w_out, links.started if behind else None
        return get

    h1, s1 = _ffn_fwd(x, p["ffn1_norm"] + links.started[0, 0], ffn_weights("ffn1", True), "ffn1")
    h2, sm = _mixer_fwd(h1, p, links)
    h3, s2 = _ffn_fwd(h2, p["ffn2_norm"], ffn_weights("ffn2", False), "ffn2")
    dh3, dfinal, loss = _loss_head(h3, target, p["final_norm"])
    g = {"final_norm": dfinal}
    dh2, g["ffn2_norm"] = _ffn_bwd(h2, p["ffn2_norm"], s2, dh3, "ffn2", links)
    dh1, gm = _mixer_bwd(h1, p, links, sm, dh2)
    g.update(gm)
    dx, g["ffn1_norm"] = _ffn_bwd(x, p["ffn1_norm"], s1, dh1, "ffn1", links)
    return loss, dx, g


HBM_SPEC = pl.BlockSpec(memory_space=pltpu.HBM)
SEM_SPEC = pl.BlockSpec(memory_space=pltpu.SEMAPHORE)
DATAFLOW = pltpu.SideEffectType.DATAFLOW_SIDE_EFFECTING


def _position():
    x, y, c = lax.axis_index("x"), lax.axis_index("y"), lax.axis_index("c")
    return x, y, c, 4 * x + 2 * y + c


def _relations(x, y, c):
    for rel in range(1, N_DEV):
        px = 1 - x if rel & 4 else x
        py = 1 - y if rel & 2 else y
        pc = 1 - c if rel & 1 else c
        yield rel, (px, py, pc), 4 * px + 2 * py + pc


def _sem_index(item, rel):
    return item * (N_DEV - 1) + rel - 1


def _landing(a, mode):
    return lax.empty((N_DEV,) + a.shape if mode == "gather" else a.shape, a.dtype)


ALL_PEERS = tuple(range(1, N_DEV))
ONE_PER_CHIP = (1, 2, 4, 6)


def _copies_start(groups, name, rels=ALL_PEERS):
    flat = [item for grp in groups for item in grp]
    n, ng = len(flat), len(groups)
    lands = [_landing(a, mode) for a, mode in flat]

    def body(*refs):
        src_refs, land_refs, sems, token = refs[:n], refs[n:2 * n], refs[2 * n:2 * n + 2 * ng], refs[-1]
        x, y, c, me = _position()
        for rel, where, peer in _relations(x, y, c):
            if rel not in rels:
                continue
            k = 0
            for gi, grp in enumerate(groups):
                for li, (_, mode) in enumerate(grp):
                    src = src_refs[k] if mode == "gather" else src_refs[k].at[peer]
                    pltpu.make_async_remote_copy(src_ref=src, dst_ref=land_refs[k].at[me], send_sem=sems[2 * gi].at[_sem_index(li, rel)],
                                                 recv_sem=sems[2 * gi + 1].at[_sem_index(li, rel)], device_id=where, device_id_type=MESH_IDS).start()
                    k += 1
        token[...] = jnp.zeros_like(token)

    sem_shapes = [pltpu.SemaphoreType.DMA((len(grp) * (N_DEV - 1),)) for grp in groups for _ in range(2)]
    thru = [pltpu.HBM(a.shape, a.dtype) for a, _ in flat] + [pltpu.HBM(l.shape, l.dtype) for l in lands]
    outs = pl.pallas_call(
        body, name=name, out_shape=(*sem_shapes, *thru, SDS((8, HEAD), F32)),
        in_specs=[HBM_SPEC] * (2 * n), out_specs=(*[SEM_SPEC] * (2 * ng), *[HBM_SPEC] * (2 * n), pl.BlockSpec(memory_space=pltpu.VMEM)),
        input_output_aliases={i: 2 * ng + i for i in range(2 * n)}, compiler_params=pltpu.CompilerParams(has_side_effects=DATAFLOW),
    )(*[pltpu.with_memory_space_constraint(a, pltpu.HBM) for a, _ in flat], *[pltpu.with_memory_space_constraint(l, pltpu.HBM) for l in lands])
    sems, srcs, landed, token = outs[:2 * ng], outs[2 * ng:2 * ng + n], outs[2 * ng + n:2 * ng + 2 * n], outs[-1]
    result, k = [], 0
    for gi, grp in enumerate(groups):
        result.append((sems[2 * gi], sems[2 * gi + 1], srcs[k:k + len(grp)], landed[k:k + len(grp)]))
        k += len(grp)
    return result, token


def _copies_wait(started, modes, after, name, rels=ALL_PEERS):
    send_sems, recv_sems, srcs, lands = started
    n = len(srcs)

    def body(*refs):
        src_refs, land_refs, ssem, rsem, token = refs[:n], refs[n:2 * n], refs[2 * n], refs[2 * n + 1], refs[-1]
        x, y, c, _ = _position()
        for rel in rels:
            for i, mode in enumerate(modes):
                src = src_refs[i] if mode == "gather" else src_refs[i].at[0]
                cp = pltpu.make_async_remote_copy(src_ref=src, dst_ref=land_refs[i].at[0], send_sem=ssem.at[_sem_index(i, rel)],
                                                  recv_sem=rsem.at[_sem_index(i, rel)], device_id=(x, y, c), device_id_type=MESH_IDS)
                cp.wait_send()
                cp.wait_recv()
        token[...] = jnp.zeros_like(token)

    outs = pl.pallas_call(
        body, name=name, out_shape=[pltpu.HBM(a.shape, a.dtype) for a in (*srcs, *lands)] + [SDS((8, HEAD), F32)],
        in_specs=[HBM_SPEC] * (2 * n) + [SEM_SPEC, SEM_SPEC, pl.BlockSpec(memory_space=pl.ANY)],
        out_specs=[HBM_SPEC] * (2 * n) + [pl.BlockSpec(memory_space=pltpu.VMEM)],
        input_output_aliases={i: i for i in range(2 * n)}, compiler_params=pltpu.CompilerParams(has_side_effects=DATAFLOW),
    )(*srcs, *lands, send_sems, recv_sems, after)
    return outs[:n], outs[n:2 * n], outs[-1]


OTHER_CHIPS = ((1, 0), (0, 1), (1, 1))


def _pass_on_start(lands, name):
    n = len(lands)

    def body(*refs):
        land_refs, ssem, rsem, token = refs[:n], refs[n], refs[n + 1], refs[-1]
        x, y, c, _ = _position()
        for j, (fx, fy) in enumerate(OTHER_CHIPS):
            slot = 4 * (1 - x if fx else x) + 2 * (1 - y if fy else y) + c
            for i in range(n):
                pltpu.make_async_remote_copy(src_ref=land_refs[i].at[slot], dst_ref=land_refs[i].at[slot], send_sem=ssem.at[i * len(OTHER_CHIPS) + j],
                                             recv_sem=rsem.at[i * len(OTHER_CHIPS) + j], device_id=(x, y, 1 - c), device_id_type=MESH_IDS).start()
        token[...] = jnp.zeros_like(token)

    sems = pltpu.SemaphoreType.DMA((n * len(OTHER_CHIPS),))
    outs = pl.pallas_call(
        body, name=name, out_shape=(sems, sems, *[pltpu.HBM(l.shape, l.dtype) for l in lands], SDS(TOKEN, F32)),
        in_specs=[HBM_SPEC] * n, out_specs=(SEM_SPEC, SEM_SPEC, *[HBM_SPEC] * n, pl.BlockSpec(memory_space=pltpu.VMEM)),
        input_output_aliases={i: 2 + i for i in range(n)}, compiler_params=pltpu.CompilerParams(has_side_effects=DATAFLOW),
    )(*lands)
    return (outs[0], outs[1], outs[2:2 + n]), outs[-1]


def _pass_on_wait(started, after, name):
    send_sems, recv_sems, lands = started
    n = len(lands)

    def body(*refs):
        land_refs, ssem, rsem = refs[:n], refs[n], refs[n + 1]
        x, y, c, _ = _position()
        for j in range(len(OTHER_CHIPS)):
            for i in range(n):
                cp = pltpu.make_async_remote_copy(src_ref=land_refs[i].at[0], dst_ref=land_refs[i].at[0], send_sem=ssem.at[i * len(OTHER_CHIPS) + j],
                                                  recv_sem=rsem.at[i * len(OTHER_CHIPS) + j], device_id=(x, y, c), device_id_type=MESH_IDS)
                cp.wait_send()
                cp.wait_recv()

    return pl.pallas_call(
        body, name=name, out_shape=[pltpu.HBM(l.shape, l.dtype) for l in lands],
        in_specs=[HBM_SPEC] * n + [SEM_SPEC, SEM_SPEC, pl.BlockSpec(memory_space=pl.ANY)], out_specs=[HBM_SPEC] * n,
        input_output_aliases={i: i for i in range(n)}, compiler_params=pltpu.CompilerParams(has_side_effects=DATAFLOW),
    )(*lands, send_sems, recv_sems, after)


WEIGHT_GROUPS = (("ffn1_w_in", "ffn1_w_out", "gdn_conv_w"), ("w_in",), ("w_branch_hgrn", "w_branch_gdn", "w_out", "ffn2_w_in", "ffn2_w_out"))
GROUP_RELS = (ONE_PER_CHIP, ONE_PER_CHIP, ALL_PEERS)


class _Links:
    def __init__(self, shards, me):
        self.me = me
        self.shards = shards
        self.weights = {}
        self.sends = []
        self.gathers = {}
        self.started = None
        self._start_gather(0, None)

    def _start_gather(self, gi, zeros):
        if gi < len(WEIGHT_GROUPS):
            items = [(self.shards[n] if zeros is None else self.shards[n] + zeros[0, 0].astype(self.shards[n].dtype), "gather")
                     for n in WEIGHT_GROUPS[gi]]
            started, self.started = _copies_start([items], "gather_start_%d" % gi, GROUP_RELS[gi])
            self.gathers[gi] = started[0]

    def weight(self, name, after):
        if name not in self.weights:
            source = {"w_in_t": "w_in", "w_in_b_t": "w_in", "w_in_ab_t": "w_in", "conv_w": "gdn_conv_w"}.get(name, name)
            gi = [i for i, grp in enumerate(WEIGHT_GROUPS) if source in grp][0]
            assert gi in self.gathers, "weight groups are asked for in order"
            srcs, lands, zero = _copies_wait(self.gathers[gi], ["gather"] * len(WEIGHT_GROUPS[gi]), after, "gather_wait_%d" % gi, GROUP_RELS[gi])
            if GROUP_RELS[gi] == ONE_PER_CHIP:
                passing, zero = _pass_on_start(lands, "gather_pass_%d" % gi)
                self._start_gather(gi + 1, zero)
                lands = _pass_on_wait(passing, self.started, "gather_passed_%d" % gi)
            else:
                self._start_gather(gi + 1, zero)
            for n, src, land in zip(WEIGHT_GROUPS[gi], srcs, lands):
                full = lax.dynamic_update_index_in_dim(land, src, self.me, 0)
                if n == "gdn_conv_w":
                    self.weights["conv_w"] = full.reshape(N_DEV, CONV_K, 4 * D_MODEL // N_DEV).transpose(1, 0, 2).reshape(CONV_K, 4 * D_MODEL)
                elif n == "w_in":
                    self.weights.update(_w_in_pieces(full.reshape(-1, D_MODEL)))
                else:
                    self.weights[n] = full.reshape(-1, D_MODEL)
        return self.weights[name]

    def send(self, grads):
        names = list(grads)
        blocks = [grads[n].reshape(N_DEV, -1, D_MODEL) for n in names]
        started, token = _copies_start([[(b, "scatter") for b in blocks]], "send_" + names[0])
        self.sends.append((names, started[0]))
        return token

    def landed(self, after):
        out = {}
        for names, started in self.sends:
            srcs, lands, _ = _copies_wait(started, ["scatter"] * len(names), after, "landed_" + names[0])
            for n, src, land in zip(names, srcs, lands):
                out[n] = lax.dynamic_update_index_in_dim(land, lax.dynamic_index_in_dim(src, self.me, 0, keepdims=False), self.me, 0)
        return out


def _adam(parts, w, m, v, name):
    n_parts, r, c = parts.shape
    tc = c if c <= 512 else (256 if r > 1024 else 512)

    def body(p_ref, w_ref, m_ref, v_ref, g_ref, d_ref, mo_ref, vo_ref):
        g = p_ref[0].astype(F32)
        for i in range(1, n_parts):
            g = g + p_ref[i].astype(F32)
        m_new = ADAM_B1 * m_ref[...] + (1.0 - ADAM_B1) * g
        v_new = ADAM_B2 * v_ref[...] + (1.0 - ADAM_B2) * (g * g)
        m_hat = m_new / (1.0 - ADAM_B1 ** ADAM_STEP)
        v_hat = v_new / (1.0 - ADAM_B2 ** ADAM_STEP)
        g_ref[...] = g
        d_ref[...] = -ADAM_LR * (m_hat / (jnp.sqrt(v_hat) + ADAM_EPS) + ADAM_WD * w_ref[...])
        mo_ref[...] = m_new
        vo_ref[...] = v_new

    spec = pl.BlockSpec((r, tc), lambda j: (0, j))
    return pl.pallas_call(
        body, grid=(c // tc,), in_specs=[pl.BlockSpec((n_parts, r, tc), lambda j: (0, 0, j)), spec, spec, spec],
        out_specs=[spec] * 4, out_shape=[SDS((r, c), F32)] * 4, name=name, compiler_params=_params(1),
    )(parts, w, m, v)


BIG = ("ffn1_w_in", "ffn1_w_out", "w_in", "w_branch_hgrn", "w_branch_gdn", "w_out", "ffn2_w_in", "ffn2_w_out")


TRANSPOSED = ("ffn1_w_in", "w_in", "ffn2_w_in")


def _shard_rows(name, shard):
    return shard.T if name in TRANSPOSED else shard


SCALAR_ROWS = 8192
N_SCALAR = 2 * GDN_V_HEADS


def _w_in_pieces(w_in_t):
    return {"w_in_t": w_in_t, "w_in_b_t": w_in_t[SCALAR_ROWS + N_SCALAR:],
            "w_in_ab_t": jnp.pad(w_in_t[SCALAR_ROWS:SCALAR_ROWS + N_SCALAR], ((0, HEAD - N_SCALAR), (0, 0)))}


def _pad_lanes(a, width=HEAD):
    return jnp.pad(a, ((0, 0), (0, width - a.shape[1])))


SMALL_ROWS = 24


def _pack_small(g, loss):
    row6 = jnp.concatenate([g["hgrn_out_norm"], g["gdn_out_norm"], g["alog"], g["dtb"], loss,
                            jnp.zeros((1, D_MODEL - 5 * HEAD), F32)], axis=1)
    return jnp.concatenate([g["ffn1_norm"], g["mix_norm"], g["lbl"], g["ffn2_norm"], g["final_norm"], row6,
                            jnp.zeros((1, D_MODEL), F32), g["conv_w"].reshape(4 * CONV_K, D_MODEL)], axis=0)


def _pack_small_state(a):
    row6 = jnp.concatenate([a["hgrn_out_norm"], a["gdn_out_norm"], _pad_lanes(a["gdn_a_log"]), _pad_lanes(a["gdn_dt_bias"]),
                            jnp.zeros((1, D_MODEL - 4 * HEAD), F32)], axis=1)
    return jnp.concatenate([a["ffn1_norm"], a["mix_norm"], a["hgrn_lb_logits"], a["ffn2_norm"], a["final_norm"].reshape(1, D_MODEL),
                            row6, jnp.zeros((1, D_MODEL), F32)], axis=0)


def _unpack_small(a):
    return {"ffn1_norm": a[0:1], "mix_norm": a[1:2], "hgrn_lb_logits": a[2:4], "ffn2_norm": a[4:5], "final_norm": a[5],
            "hgrn_out_norm": a[6:7, :HEAD], "gdn_out_norm": a[6:7, HEAD:2 * HEAD],
            "gdn_a_log": a[6:7, 2 * HEAD:2 * HEAD + GDN_V_HEADS], "gdn_dt_bias": a[6:7, 3 * HEAD:3 * HEAD + GDN_V_HEADS]}


NAMES = ("ffn1_norm", "ffn1_w_in", "ffn1_w_out", "mix_norm", "w_in", "hgrn_lb_logits", "hgrn_out_norm", "gdn_conv_w", "gdn_a_log",
         "gdn_dt_bias", "gdn_out_norm", "w_branch_hgrn", "w_branch_gdn", "w_out", "ffn2_norm", "ffn2_w_in", "ffn2_w_out", "final_norm")


def kernel(x, ffn1_norm, ffn1_w_in, ffn1_w_out, mix_norm, w_in, hgrn_lb_logits, hgrn_out_norm, gdn_conv_w, gdn_a_log, gdn_dt_bias, gdn_out_norm, w_branch_hgrn, w_branch_gdn, w_out, ffn2_norm, ffn2_w_in, ffn2_w_out, final_norm, loss_target, m_ffn1_norm, m_ffn1_w_in, m_ffn1_w_out, m_mix_norm, m_w_in, m_hgrn_lb_logits, m_hgrn_out_norm, m_gdn_conv_w, m_gdn_a_log, m_gdn_dt_bias, m_gdn_out_norm, m_w_branch_hgrn, m_w_branch_gdn, m_w_out, m_ffn2_norm, m_ffn2_w_in, m_ffn2_w_out, m_final_norm, v_ffn1_norm, v_ffn1_w_in, v_ffn1_w_out, v_mix_norm, v_w_in, v_hgrn_lb_logits, v_hgrn_out_norm, v_gdn_conv_w, v_gdn_a_log, v_gdn_dt_bias, v_gdn_out_norm, v_w_branch_hgrn, v_w_branch_gdn, v_w_out, v_ffn2_norm, v_ffn2_w_in, v_ffn2_w_out, v_final_norm):
    wts = dict(zip(NAMES, (ffn1_norm, ffn1_w_in, ffn1_w_out, mix_norm, w_in, hgrn_lb_logits, hgrn_out_norm, gdn_conv_w, gdn_a_log,
                           gdn_dt_bias, gdn_out_norm, w_branch_hgrn, w_branch_gdn, w_out, ffn2_norm, ffn2_w_in, ffn2_w_out, final_norm)))
    mom = dict(zip(NAMES, (m_ffn1_norm, m_ffn1_w_in, m_ffn1_w_out, m_mix_norm, m_w_in, m_hgrn_lb_logits, m_hgrn_out_norm, m_gdn_conv_w,
                           m_gdn_a_log, m_gdn_dt_bias, m_gdn_out_norm, m_w_branch_hgrn, m_w_branch_gdn, m_w_out, m_ffn2_norm, m_ffn2_w_in,
                           m_ffn2_w_out, m_final_norm)))
    var = dict(zip(NAMES, (v_ffn1_norm, v_ffn1_w_in, v_ffn1_w_out, v_mix_norm, v_w_in, v_hgrn_lb_logits, v_hgrn_out_norm, v_gdn_conv_w,
                           v_gdn_a_log, v_gdn_dt_bias, v_gdn_out_norm, v_w_branch_hgrn, v_w_branch_gdn, v_w_out, v_ffn2_norm, v_ffn2_w_in,
                           v_ffn2_w_out, v_final_norm)))
    me = 4 * lax.axis_index("x") + 2 * lax.axis_index("y") + lax.axis_index("c")

    conv_shard = wts["gdn_conv_w"][0]
    shards = {n: _shard_rows(n, wts[n][0]).astype(BF16) for n in BIG}
    shards["gdn_conv_w"] = conv_shard.reshape(2, D_MODEL)
    links = _Links(shards, me)
    p = {"ffn1_norm": wts["ffn1_norm"], "mix_norm": wts["mix_norm"], "ffn2_norm": wts["ffn2_norm"], "final_norm": wts["final_norm"].reshape(1, D_MODEL),
         "lbl": wts["hgrn_lb_logits"], "hgrn_out_norm": wts["hgrn_out_norm"], "gdn_out_norm": wts["gdn_out_norm"],
         "alog": _pad_lanes(wts["gdn_a_log"]), "dtb": _pad_lanes(wts["gdn_dt_bias"])}

    loss, dx, g = _local_step(x[0], loss_target[0], p, links)

    small_started, small_token = _copies_start([[(_pack_small(g, loss), "gather")]], "small_start")
    landed = links.landed(small_token)

    big = [{} for _ in range(4)]
    for n in BIG:
        res = _adam(landed[n], _shard_rows(n, wts[n][0]), _shard_rows(n, mom[n][0]), _shard_rows(n, var[n][0]), "adam_" + n)
        for kind in range(4):
            big[kind][n] = _shard_rows(n, res[kind])
    small_srcs, small_lands, _ = _copies_wait(small_started[0], ["gather"], res[0], "small_wait")
    small_parts = lax.dynamic_update_index_in_dim(small_lands[0], small_srcs[0], me, 0)
    n_vec = SMALL_ROWS - 4 * CONV_K
    small_raw = _adam(small_parts[:, :n_vec], _pack_small_state(wts), _pack_small_state(mom), _pack_small_state(var), "adam_small")
    small = [_unpack_small(o) for o in small_raw]
    loss_total = small_raw[0][6, 4 * HEAD]
    conv_parts = small_parts[:, n_vec:].reshape(N_DEV, CONV_K, 4 * D_MODEL)
    width = 4 * D_MODEL // N_DEV
    conv_mine = lax.dynamic_slice_in_dim(conv_parts, me * width, width, axis=2)
    conv = _adam(conv_mine, conv_shard, mom["gdn_conv_w"][0], var["gdn_conv_w"][0], "adam_conv")

    outs = []
    for kind in range(4):
        for n in NAMES:
            if n in BIG:
                outs.append(big[kind][n][None])
            elif n == "gdn_conv_w":
                outs.append(conv[kind][None])
            else:
                outs.append(small[kind][n])
    return (loss_total, dx[None], *outs)
```

```python
import functools

import jax
import jax.numpy as jnp
from jax import lax
from jax.experimental import pallas as pl
from jax.experimental.pallas import tpu as pltpu

F32 = jnp.float32
BF16 = jnp.bfloat16
HIGHEST = lax.Precision.HIGHEST
MESH_IDS = pl.DeviceIdType.MESH

D_MODEL = 1024
D_FF = 2816
N_DEV = 8
EPS = 1e-6
HEAD = 128
HG_HEADS = 8
GDN_QK_HEADS = 8
GDN_V_HEADS = 16
GDN_CHUNK = 64
HG_CHUNK = 16
CONV_K = 4
IN_WIDTH = 12320
COL_HQ, COL_HF, COL_HI, COL_HG, COL_GQ, COL_GK, COL_GV = 0, 8, 16, 24, 32, 40, 48
COL_GZ, COL_GATE_H, COL_GATE_G = 0, 16, 24
VMEM_LIMIT = 56 * 1024 * 1024

ADAM_LR, ADAM_B1, ADAM_B2, ADAM_EPS, ADAM_WD, ADAM_STEP = 0.001, 0.9, 0.999, 1e-08, 0.01, 10

SDS = jax.ShapeDtypeStruct


def _params(n_axes):
    return pltpu.CompilerParams(dimension_semantics=("arbitrary",) * n_axes, vmem_limit_bytes=VMEM_LIMIT)


def _tile(n, candidates=(512, 384, 256, 128, 64, 32, 16, 8)):
    for c in candidates:
        if n % c == 0:
            return c
    return n


_DIMS = {"nn": ((1,), (0,)), "nt": ((1,), (1,)), "tn": ((0,), (0,))}


def _bdot_raw(a, b, dims):
    return lax.dot_general(a.astype(BF16), b.astype(BF16), (_DIMS[dims], ((), ())), preferred_element_type=F32)


@functools.partial(jax.custom_vjp, nondiff_argnums=(2,))
def _bdot(a, b, dims):
    return _bdot_raw(a, b, dims)


def _bdot_fwd(a, b, dims):
    return _bdot_raw(a, b, dims), (a, b)


def _bdot_bwd(dims, res, ct):
    a, b = res
    if dims == "nn":
        return _bdot_raw(ct, b, "nt"), _bdot_raw(a, ct, "tn")
    if dims == "nt":
        return _bdot_raw(ct, b, "nn"), _bdot_raw(ct, a, "tn")
    return _bdot_raw(b, ct, "nt"), _bdot_raw(a, ct, "nn")


_bdot.defvjp(_bdot_fwd, _bdot_bwd)


def _hdot_raw(a, b):
    return jnp.dot(a, b, precision=HIGHEST, preferred_element_type=F32)


MM_VMEM_BUDGET = 38 * 1024 * 1024
TOKEN = (8, HEAD)


def _mm_tiles(m, n, k, a_bytes, b_bytes, o_bytes, r_bytes, m_align=8):
    def need(tm, tn, tk):
        return 2 * (tm * tk * a_bytes + tk * tn * b_bytes + tm * tn * (o_bytes + r_bytes)) + (tm * tn * 4 if tk < k else 0)

    def shrink(tm, tn, tk, floor_m, floor_n):
        while need(tm, tn, tk) > MM_VMEM_BUDGET:
            if tn > floor_n and tn % 256 == 0 and tn >= tm:
                tn //= 2
            elif tm > floor_m and tm % (2 * m_align) == 0:
                tm //= 2
            elif tn > floor_n and tn % 256 == 0:
                tn //= 2
            else:
                return None
        return tm, tn, tk

    tm = _tile(m, (1408, 1024, 704, 512, 256, 128, 64, 32, 16, 8))
    tn = _tile(n, (1408, 1024, 512, 256, 128))
    whole = shrink(tm, tn, k, min(tm, 1024), min(tn, 512))
    if whole is not None:
        return whole
    tk = _tile(k, (2048, 1408, 1024, 512, 256, 128, 64, 32, 16, 8))
    while True:
        fit = shrink(tm, tn, tk, min(tm, 256), min(tn, 512))
        if fit is not None or tk <= 512 or tk % 256:
            return fit if fit is not None else (tm, tn, tk)
        tk //= 2


def _mm(a, b, dims, out_dtype, name, res=None, alpha=1.0, after=None, b_rows=None):
    b_shape = b.shape if b_rows is None else (b_rows, b.shape[1])
    if dims == "nn":
        (m, k), (k2, n) = a.shape, b_shape
    elif dims == "nt":
        (m, k), (n, k2) = a.shape, b_shape
    else:
        (k, m), (k2, n) = a.shape, b_shape
    assert k == k2, (a.shape, b.shape, dims)
    has_res = res is not None
    tm, tn, tk = _mm_tiles(m, n, k, a.dtype.itemsize, b.dtype.itemsize, jnp.dtype(out_dtype).itemsize, res.dtype.itemsize if has_res else 0,
                           m_align=HEAD if dims == "tn" else 8)
    nk = k // tk
    a_spec = pl.BlockSpec((tk, tm), lambda i, j, kk: (kk, i)) if dims == "tn" else pl.BlockSpec((tm, tk), lambda i, j, kk: (i, kk))
    b_spec = pl.BlockSpec((tn, tk), lambda i, j, kk: (j, kk)) if dims == "nt" else pl.BlockSpec((tk, tn), lambda i, j, kk: (kk, j))
    o_spec = pl.BlockSpec((tm, tn), lambda i, j, kk: (i, j))

    def finish(acc, r_ref, o_ref):
        out = acc * alpha if alpha != 1.0 else acc
        if has_res:
            out = r_ref[...].astype(F32) + out
        o_ref[...] = out.astype(o_ref.dtype)

    n_in = 2 + has_res + (after is not None)

    def body(*refs):
        a_ref, b_ref = refs[:2]
        r_ref = refs[2] if has_res else None
        o_ref = refs[n_in]
        p = _bdot_raw(a_ref[...], b_ref[...], dims)
        if nk == 1:
            finish(p, r_ref, o_ref)
            return
        acc_ref = refs[-1]
        kk = pl.program_id(2)

        @pl.when(kk == 0)
        def _():
            acc_ref[...] = p

        @pl.when(kk > 0)
        def _():
            acc_ref[...] += p

        @pl.when(kk == nk - 1)
        def _():
            finish(acc_ref[...], r_ref, o_ref)

    args = (a, b) + ((res,) if has_res else ()) + ((after,) if after is not None else ())
    in_specs = [a_spec, b_spec] + ([o_spec] if has_res else []) + ([pl.BlockSpec(TOKEN, lambda i, j, kk: (0, 0))] if after is not None else [])
    return pl.pallas_call(
        body, grid=(m // tm, n // tn, nk), in_specs=in_specs, out_specs=o_spec, out_shape=SDS((m, n), out_dtype),
        scratch_shapes=[pltpu.VMEM((tm, tn), F32)] if nk > 1 else [], name=name, compiler_params=_params(3),
    )(*args)


PIECE_TK = 1024


def _mm_pieces(pieces, b, name, res=None, after=None):
    m, n = pieces[0].shape[0], b.shape[1]
    blocks = [p.shape[1] // PIECE_TK for p in pieces]
    assert all(p.shape[1] % PIECE_TK == 0 and p.shape[0] == m for p in pieces)
    starts = [sum(blocks[:i]) for i in range(len(pieces))]
    nk = sum(blocks)
    tm, tn = _tile(m, (1024, 512, 256, 128)), _tile(n, (1024, 512, 256, 128))
    n_p = len(pieces)
    n_in = n_p + 1 + (res is not None) + (after is not None)

    def piece_spec(start, count):
        return pl.BlockSpec((tm, PIECE_TK), lambda i, j, kk: (i, jnp.clip(kk - start, 0, count - 1)))

    def body(*refs):
        b_ref, o_ref, acc_ref = refs[n_p], refs[n_in], refs[-1]
        kk = pl.program_id(2)

        @pl.when(kk == 0)
        def _():
            acc_ref[...] = jnp.zeros_like(acc_ref)

        for p_ref, start, count in zip(refs[:n_p], starts, blocks):
            @pl.when(jnp.logical_and(kk >= start, kk < start + count))
            def _(p_ref=p_ref):
                acc_ref[...] += _bdot_raw(p_ref[...], b_ref[...], "nn")

        @pl.when(kk == nk - 1)
        def _():
            out = acc_ref[...]
            if res is not None:
                out = refs[n_p + 1][...] + out
            o_ref[...] = out

    o_spec = pl.BlockSpec((tm, tn), lambda i, j, kk: (i, j))
    in_specs = [piece_spec(s, c) for s, c in zip(starts, blocks)] + [pl.BlockSpec((PIECE_TK, tn), lambda i, j, kk: (kk, j))]
    args = list(pieces) + [b]
    if res is not None:
        in_specs.append(o_spec)
        args.append(res)
    if after is not None:
        in_specs.append(pl.BlockSpec(TOKEN, lambda i, j, kk: (0, 0)))
        args.append(after)
    return pl.pallas_call(
        body, grid=(m // tm, n // tn, nk), in_specs=in_specs, out_specs=o_spec, out_shape=SDS((m, n), F32),
        scratch_shapes=[pltpu.VMEM((tm, tn), F32)], name=name, compiler_params=_params(3),
    )(*args)


def _tmap(fn, grid, ins, outs, name):
    n_in = len(ins)
    n_ax = len(grid)

    def body(*refs):
        vals = fn(*[r[...] for r in refs[:n_in]])
        if not isinstance(vals, (tuple, list)):
            vals = (vals,)
        first_inner = pl.program_id(n_ax - 1) == 0
        first_all = first_inner
        for ax in range(n_ax - 1):
            first_all = jnp.logical_and(first_all, pl.program_id(ax) == 0)

        def put(ref, val, acc):
            val = val.astype(ref.dtype)
            if acc is None:
                ref[...] = val
                return
            first = first_inner if acc == "inner" else first_all

            @pl.when(first)
            def _():
                ref[...] = val

            @pl.when(jnp.logical_not(first))
            def _():
                ref[...] += val

        for ref, val, o in zip(refs[n_in:], vals, outs):
            put(ref, val, o[4])

    return pl.pallas_call(
        body, grid=grid,
        in_specs=[pl.BlockSpec(bs, im) for _, bs, im in ins],
        out_specs=[pl.BlockSpec(o[2], o[3]) for o in outs],
        out_shape=[SDS(o[0], o[1]) for o in outs],
        name=name, compiler_params=_params(n_ax),
    )(*[a for a, _, _ in ins])


def _rows(width, tt, off=0):
    return (tt, width), (lambda j, i: (i, off + j))


def _rms(x, g):
    x = x.astype(F32)
    return x * lax.rsqrt(jnp.mean(x * x, axis=-1, keepdims=True) + EPS) * g


def _sigmoid(x):
    return jax.nn.sigmoid(x)


def _silu(x):
    return x * _sigmoid(x)


def _softplus(x):
    return jnp.maximum(x, 0.0) + jnp.log1p(jnp.exp(-jnp.abs(x)))


def _rms_fwd(x, g, name):
    t, d = x.shape
    tt = _tile(t, (256, 128))
    return _tmap(_rms, (1, t // tt), [(x, *_rows(d, tt)), (g, (1, d), lambda j, i: (0, 0))],
                 [((t, d), BF16, *_rows(d, tt), None)], name)[0]


def _rms_bwd(x, g, dn, dres, name):
    t, d = x.shape
    tt = _tile(t, (256, 128))

    def fn(x, g, dn, dres):
        _, vjp = jax.vjp(_rms, x, g)
        dx, dg = vjp(dn.astype(F32))
        return dres + dx, dg

    return _tmap(fn, (1, t // tt),
                 [(x, *_rows(d, tt)), (g, (1, d), lambda j, i: (0, 0)), (dn, *_rows(d, tt)), (dres, *_rows(d, tt))],
                 [((t, d), F32, *_rows(d, tt), None), ((1, d), F32, (1, d), lambda j, i: (0, 0), "inner")], name)


def _swiglu(ab):
    return _silu(ab[:, :D_FF].astype(F32)) * ab[:, D_FF:].astype(F32)


def _swiglu_fwd(ab, name):
    t = ab.shape[0]
    tt = _tile(t, (128,))
    return _tmap(_swiglu, (1, t // tt), [(ab, *_rows(2 * D_FF, tt))], [((t, D_FF), BF16, *_rows(D_FF, tt), None)], name)[0]


def _swiglu_bwd(ab, ds, name):
    t = ab.shape[0]
    tt = _tile(t, (128,))

    def fn(ab, ds):
        a, b = ab[:, :D_FF].astype(F32), ab[:, D_FF:].astype(F32)
        _, vjp = jax.vjp(lambda a, b: _silu(a) * b, a, b)
        da, db = vjp(ds.astype(F32))
        return jnp.concatenate([da, db], axis=1)

    return _tmap(fn, (1, t // tt), [(ab, *_rows(2 * D_FF, tt)), (ds, *_rows(D_FF, tt))],
                 [((t, 2 * D_FF), BF16, *_rows(2 * D_FF, tt), None)], name)[0]


def _ffn_fwd(h, g, weights, tag):
    n = _rms_fwd(h, g, tag + "_norm")
    w_in_t, w_out, after = weights(n)
    ab = _mm(n, w_in_t, "nt", BF16, tag + "_in", after=after)
    s = _swiglu_fwd(ab, tag + "_act")
    out = _mm(s, w_out, "nn", F32, tag + "_out", res=h, alpha=0.5)
    return out, (n, ab, s, w_in_t, w_out)


def _ffn_bwd(h, g, saved, dout, tag, links):
    n, ab, s, w_in_t, w_out = saved
    sent = links.send({tag + "_w_out": _mm(s, dout, "tn", BF16, tag + "_dw_out", alpha=0.5)})
    ds = _mm(dout, w_out, "nt", BF16, tag + "_ds", alpha=0.5, after=sent)
    dab = _swiglu_bwd(ab, ds, tag + "_dact")
    sent = links.send({tag + "_w_in": _mm(dab, n, "tn", BF16, tag + "_dw_in")})
    dn = _mm(dab, w_in_t, "nn", F32, tag + "_dn", after=sent)
    return _rms_bwd(h, g, dn, dout, tag + "_dnorm")


def _chunk_sum_matrix(n, chunk, transpose=False):
    row = lax.broadcasted_iota(jnp.int32, (n, n), 0)
    col = lax.broadcasted_iota(jnp.int32, (n, n), 1)
    if transpose:
        row, col = col, row
    return jnp.where(jnp.logical_and(col <= row, row // chunk == col // chunk), 1.0, 0.0).astype(F32)


def _hgrn_gates(hq, hf, lbl):
    lb = _sigmoid(lbl[0:1, :] - lbl[1:2, :])
    sg = _sigmoid(hf)
    f = lb + (1.0 - lb) * sg
    q = _silu(hq) * HEAD ** -0.5
    k = (1.0 - lb) * (1.0 - sg)
    return q, k, jnp.log(f)


def _hgrn_prep_fwd(proj, lbl):
    t = proj.shape[0]
    tt, ft = _tile(t, (256, 128)), 512

    def fn(hq, hf, lbl):
        q, k, log_f = _hgrn_gates(hq, hf, lbl)
        return q, k, _hdot_raw(_chunk_sum_matrix(tt, HG_CHUNK), log_f)

    o = ((t, D_MODEL), F32, *_rows(ft, tt), None)
    return _tmap(fn, (D_MODEL // ft, t // tt),
                 [(proj, *_rows(ft, tt, COL_HQ * HEAD // ft)), (proj, *_rows(ft, tt, COL_HF * HEAD // ft)), (lbl, (2, ft), lambda j, i: (0, j))],
                 [o, o, o], "hgrn_prep")


def _hgrn_prep_bwd(proj, lbl, dq, dk, db):
    t = proj.shape[0]
    tt, ft = _tile(t, (256, 128)), 512

    def fn(hq, hf, lbl, dq, dk, db):
        dlog_f = _hdot_raw(_chunk_sum_matrix(tt, HG_CHUNK, transpose=True), db)
        _, vjp = jax.vjp(_hgrn_gates, hq, hf, lbl)
        return vjp((dq, dk, dlog_f))

    o = ((t, D_MODEL), BF16, *_rows(ft, tt), None)
    r = _rows(ft, tt)
    return _tmap(fn, (D_MODEL // ft, t // tt),
                 [(proj, *_rows(ft, tt, COL_HQ * HEAD // ft)), (proj, *_rows(ft, tt, COL_HF * HEAD // ft)), (lbl, (2, ft), lambda j, i: (0, j)),
                  (dq, *r), (dk, *r), (db, *r)],
                 [o, o, ((2, D_MODEL), F32, (2, ft), lambda j, i: (0, j), "inner")], "hgrn_prep_bwd")


@functools.partial(jax.custom_vjp, nondiff_argnums=(1,))
def _roll_rows(x, d):
    return pltpu.roll(x, d, 0)


def _roll_rows_fwd(x, d):
    return pltpu.roll(x, d, 0), None


def _roll_rows_bwd(d, _, ct):
    return (pltpu.roll(ct, ct.shape[0] - d, 0),)


_roll_rows.defvjp(_roll_rows_fwd, _roll_rows_bwd)


def _hgrn_chunks(q, k, v, b, st):
    n = q[0].shape[0]
    half = n // 2
    srow = lax.broadcasted_iota(jnp.int32, (half, HEAD), 0)
    inter = _each(lambda q, b, st: _bdot(q * jnp.exp(b), st, "nt"), q, b, st)

    def below_scores(q, k, b):
        ref = b[half:half + 1, :]
        return _bdot(q[half:] * jnp.exp(jnp.minimum(b[half:] - ref, 0.0)), k[:half] * jnp.exp(jnp.minimum(ref - b[:half], 0.0)), "nt")

    below = _each(lambda a, v: _bdot(a, v[:half], "nn"), _each(below_scores, q, k, b), v)

    def diagonal(q, k, v, b):
        blocks = []
        for lo in (0, half):
            qb, kb, vb, bb = (a[lo:lo + half] for a in (q, k, v, b))
            o = jnp.sum(qb * kb, axis=1, keepdims=True) * vb
            for d in range(1, half):
                kr, vr, br = _roll_rows(kb, d), _roll_rows(vb, d), _roll_rows(bb, d)
                a = jnp.sum(qb * kr * jnp.exp(jnp.minimum(bb - br, 0.0)), axis=1, keepdims=True)
                o = o + jnp.where(srow[:, :1] >= d, a, 0.0) * vr
            blocks.append(o)
        return jnp.concatenate(blocks, axis=0)

    diag = _each(diagonal, q, k, v, b)
    o = _each(lambda inter, diag, below: inter + diag + jnp.concatenate([jnp.zeros_like(below), below], axis=0), inter, diag, below)

    def new_state(k, v, b, st):
        bend = b[n - 1:n, :]
        return st * jnp.exp(bend) + _bdot(v, k * jnp.exp(bend - b), "tn")

    return o, _each(new_state, k, v, b, st)


HG_GROUP = 8
HG_PER = GDN_CHUNK // HG_CHUNK


def _hgrn_rec_fwd(q, k, proj, b):
    t = q.shape[0]
    nc = t // GDN_CHUNK
    blk = (GDN_CHUNK, HG_GROUP * HEAD)
    im = lambda h, c: (c, h)

    def body(q_ref, k_ref, v_ref, b_ref, o_ref, hs_ref, st_ref):
        @pl.when(pl.program_id(1) == 0)
        def _():
            st_ref[...] = jnp.zeros_like(st_ref)

        heads = range(HG_GROUP)
        for j in range(HG_PER):
            sl = pl.ds(HG_CHUNK * j, HG_CHUNK)
            st = tuple(st_ref[g] for g in heads)
            o, st_new = _hgrn_chunks(*[tuple(r[sl, _head_lanes(g)] for g in heads) for r in (q_ref, k_ref, v_ref, b_ref)], st)
            for g in heads:
                hs_ref[g, j] = st[g]
                o_ref[sl, _head_lanes(g)] = o[g]
                st_ref[g] = st_new[g]

    return pl.pallas_call(
        body, grid=(HG_HEADS // HG_GROUP, nc),
        in_specs=[pl.BlockSpec(blk, im), pl.BlockSpec(blk, im), pl.BlockSpec(blk, lambda h, c: (c, COL_HI // HG_GROUP + h)), pl.BlockSpec(blk, im)],
        out_specs=[pl.BlockSpec(blk, im), pl.BlockSpec((HG_GROUP, HG_PER, HEAD, HEAD), lambda h, c: (h, c, 0, 0))],
        out_shape=[SDS((t, D_MODEL), F32), SDS((HG_HEADS, nc * HG_PER, HEAD, HEAD), F32)],
        scratch_shapes=[pltpu.VMEM((HG_GROUP, HEAD, HEAD), F32)], name="hgrn_rec", compiler_params=_params(2),
    )(q, k, proj, b)


def _hgrn_rec_bwd(q, k, proj, b, hs, do):
    t = q.shape[0]
    nc = t // GDN_CHUNK
    blk = (GDN_CHUNK, HG_GROUP * HEAD)
    im = lambda h, c: (nc - 1 - c, h)

    def body(q_ref, k_ref, v_ref, b_ref, hs_ref, do_ref, dq_ref, dk_ref, dv_ref, db_ref, dst_ref):
        @pl.when(pl.program_id(1) == 0)
        def _():
            dst_ref[...] = jnp.zeros_like(dst_ref)

        heads = range(HG_GROUP)
        for j in reversed(range(HG_PER)):
            sl = pl.ds(HG_CHUNK * j, HG_CHUNK)
            _, vjp = jax.vjp(_hgrn_chunks, *[tuple(r[sl, _head_lanes(g)] for g in heads) for r in (q_ref, k_ref, v_ref, b_ref)],
                             tuple(hs_ref[g, j] for g in heads))
            dq, dk, dv, db, dst = vjp((tuple(do_ref[sl, _head_lanes(g)] for g in heads), tuple(dst_ref[g] for g in heads)))
            for g in heads:
                ln = _head_lanes(g)
                dq_ref[sl, ln] = dq[g]
                dk_ref[sl, ln] = dk[g]
                dv_ref[sl, ln] = dv[g].astype(dv_ref.dtype)
                db_ref[sl, ln] = db[g]
                dst_ref[g] = dst[g]

    spec = pl.BlockSpec(blk, im)
    return pl.pallas_call(
        body, grid=(HG_HEADS // HG_GROUP, nc),
        in_specs=[spec, spec, pl.BlockSpec(blk, lambda h, c: (nc - 1 - c, COL_HI // HG_GROUP + h)), spec,
                  pl.BlockSpec((HG_GROUP, HG_PER, HEAD, HEAD), lambda h, c: (h, nc - 1 - c, 0, 0)), spec],
        out_specs=[spec, spec, spec, spec],
        out_shape=[SDS((t, D_MODEL), F32), SDS((t, D_MODEL), F32), SDS((t, D_MODEL), BF16), SDS((t, D_MODEL), F32)],
        scratch_shapes=[pltpu.VMEM((HG_GROUP, HEAD, HEAD), F32)], name="hgrn_rec_bwd", compiler_params=_params(2),
    )(q, k, proj, b, hs, do)


def _shift_down(x, d):
    if d == 0:
        return x
    row = lax.broadcasted_iota(jnp.int32, x.shape, 0)
    return jnp.where(row >= d, pltpu.roll(x, d, 0), 0.0)


def _shift_up(x, d):
    if d == 0:
        return x
    n = x.shape[0]
    row = lax.broadcasted_iota(jnp.int32, x.shape, 0)
    return jnp.where(row < n - d, pltpu.roll(x, n - d, 0), 0.0)


def _conv_fwd(proj, conv_w):
    t = proj.shape[0]
    width = 2 * D_MODEL + 2 * D_MODEL

    def body(x_ref, w_ref, c_ref, y_ref):
        x, w = x_ref[...], w_ref[...]
        y = w[CONV_K - 1:CONV_K, :] * x
        for j in range(CONV_K - 1):
            y = y + w[j:j + 1, :] * _shift_down(x, CONV_K - 1 - j)
        y_ref[...] = y
        c_ref[...] = _silu(y)

    out = pl.BlockSpec((t, HEAD), lambda j: (0, j))
    return pl.pallas_call(
        body, grid=(width // HEAD,),
        in_specs=[pl.BlockSpec((t, HEAD), lambda j: (0, COL_GQ + j)), pl.BlockSpec((CONV_K, HEAD), lambda j: (0, j))],
        out_specs=[out, out], out_shape=[SDS((t, width), F32), SDS((t, width), F32)],
        name="gdn_conv", compiler_params=_params(1),
    )(proj, conv_w)


def _conv_bwd(proj, conv_w, y, dc_qk, dc_v):
    t = proj.shape[0]
    n_qk = dc_qk.shape[1] // HEAD
    width = dc_qk.shape[1] + dc_v.shape[1]

    def body(x_ref, w_ref, y_ref, dqk_ref, dv_ref, dx_ref, dw_ref):
        x, w, y = x_ref[...], w_ref[...], y_ref[...]
        sg = _sigmoid(y)
        dc = jnp.where(pl.program_id(0) < n_qk, dqk_ref[...], dv_ref[...])
        dy = dc * (sg * (1.0 + y * (1.0 - sg)))
        ahead = [_shift_up(dy, CONV_K - 1 - j) for j in range(CONV_K)]
        dx = w[0:1, :] * ahead[0]
        for j in range(1, CONV_K):
            dx = dx + w[j:j + 1, :] * ahead[j]
        dx_ref[...] = dx.astype(dx_ref.dtype)
        dw_ref[...] = jnp.concatenate([jnp.sum(x * ahead[j], axis=0, keepdims=True) for j in range(CONV_K)], axis=0)

    blk = pl.BlockSpec((t, HEAD), lambda j: (0, j))
    return pl.pallas_call(
        body, grid=(width // HEAD,),
        in_specs=[pl.BlockSpec((t, HEAD), lambda j: (0, COL_GQ + j)), pl.BlockSpec((CONV_K, HEAD), lambda j: (0, j)), blk,
                  pl.BlockSpec((t, HEAD), lambda j: (0, jnp.minimum(j, n_qk - 1))), pl.BlockSpec((t, HEAD), lambda j: (0, jnp.maximum(j - n_qk, 0)))],
        out_specs=[blk, pl.BlockSpec((CONV_K, HEAD), lambda j: (0, j))],
        out_shape=[SDS((t, width), BF16), SDS((CONV_K, width), F32)],
        name="gdn_conv_bwd", compiler_params=_params(1),
    )(proj, conv_w, y, dc_qk, dc_v)


def _l2norm(x, scale):
    return x * lax.rsqrt(jnp.sum(x * x, axis=-1, keepdims=True) + EPS) * scale


def _head(a, h):
    return a[:, h * HEAD:(h + 1) * HEAD]


def _qk_scale(h):
    return HEAD ** -0.5 if h < GDN_QK_HEADS else 1.0


def _qk_norm_fwd(c):
    t = c.shape[0]
    tt = _tile(t, (256, 128))
    width = 2 * D_MODEL

    def fn(x):
        return jnp.concatenate([_l2norm(_head(x, h), _qk_scale(h)) for h in range(2 * GDN_QK_HEADS)], axis=1)

    return _tmap(fn, (1, t // tt), [(c, *_rows(width, tt))], [((t, width), F32, *_rows(width, tt), None)], "gdn_qk_norm")[0]


def _qk_norm_bwd(c, dq_rep, dk_rep):
    t = c.shape[0]
    tt = _tile(t, (256, 128))
    width = 2 * D_MODEL

    def fn(x, dq2, dk2):
        out = []
        for h in range(2 * GDN_QK_HEADS):
            d2, hh = (dq2, h) if h < GDN_QK_HEADS else (dk2, h - GDN_QK_HEADS)
            _, vjp = jax.vjp(lambda x: _l2norm(x, _qk_scale(h)), _head(x, h))
            out.append(vjp(_head(d2, 2 * hh) + _head(d2, 2 * hh + 1))[0])
        return jnp.concatenate(out, axis=1)

    r = _rows(width, tt)
    return _tmap(fn, (1, t // tt), [(c, *r), (dq_rep, *r), (dk_rep, *r)], [((t, width), F32, *r, None)], "gdn_qk_norm_bwd")[0]


def _gdn_gates(x, alog, dtb):
    return -jnp.exp(alog) * _softplus(x + dtb), _sigmoid(x)


def _gates_fwd(pab, alog, dtb):
    t = pab.shape[0]
    tt = _tile(t, (256, 128))

    def fn(x, alog, dtb):
        g, beta = _gdn_gates(x, alog, dtb)
        lane = lax.broadcasted_iota(jnp.int32, g.shape, 1)
        return jnp.where(lane < GDN_V_HEADS, _hdot_raw(_chunk_sum_matrix(tt, GDN_CHUNK), g), beta).T

    p = (alog, (1, HEAD), lambda j, i: (0, 0)), (dtb, (1, HEAD), lambda j, i: (0, 0))
    return _tmap(fn, (1, t // tt), [(pab, *_rows(HEAD, tt)), *p], [((HEAD, t), F32, (HEAD, tt), lambda j, i: (0, i), None)], "gdn_gates")[0]


def _gates_bwd(pab, alog, dtb, dout_t):
    t = pab.shape[0]
    tt = _tile(t, (256, 128))

    def fn(x, alog, dtb, dout_t):
        dout = dout_t.T
        lane = lax.broadcasted_iota(jnp.int32, dout.shape, 1)
        dgam = jnp.where(lane < GDN_V_HEADS, dout, 0.0)
        dbeta = jnp.where(jnp.logical_and(lane >= GDN_V_HEADS, lane < 2 * GDN_V_HEADS), dout, 0.0)
        dg = _hdot_raw(_chunk_sum_matrix(tt, GDN_CHUNK, transpose=True), dgam)
        _, vjp = jax.vjp(_gdn_gates, x, alog, dtb)
        return vjp((dg, dbeta))

    p = (alog, (1, HEAD), lambda j, i: (0, 0)), (dtb, (1, HEAD), lambda j, i: (0, 0))
    acc = ((1, HEAD), F32, (1, HEAD), lambda j, i: (0, 0), "inner")
    return _tmap(fn, (1, t // tt), [(pab, *_rows(HEAD, tt)), *p, (dout_t, (HEAD, tt), lambda j, i: (0, i))],
                 [((t, HEAD), BF16, *_rows(HEAD, tt), None), acc, acc], "gdn_gates_bwd")


def _split_bf16(x):
    hi = x.astype(BF16)
    return hi, (x - hi.astype(F32)).astype(BF16)


def _dot3(a, b):
    (ah, al), (bh, bl) = a, b
    return _bdot_raw(ah, bh, "nn") + (_bdot_raw(ah, bl, "nn") + _bdot_raw(al, bh, "nn"))


def _each(fn, *lists):
    return tuple(fn(*xs) for xs in zip(*lists))


def _unit_lower_inverses_raw(a):
    n = a[0].shape[0]
    row = lax.broadcasted_iota(jnp.int32, (n, n), 0)
    col = lax.broadcasted_iota(jnp.int32, (n, n), 1)
    eye = jnp.where(row == col, 1.0, 0.0).astype(F32)
    p = _each(lambda a: eye - a, a)
    x = _each(_split_bf16, a)
    m = 2
    while m < n:
        x = _each(_split_bf16, _each(_dot3, x, x))
        p = _each(lambda p, x: p + _bdot_raw(p, x[0], "nn"), p, x)
        m *= 2
    return p


@jax.custom_vjp
def _unit_lower_inverses(a, known):
    return _unit_lower_inverses_raw(a) if known is None else known


def _uli_fwd(a, known):
    inv = _unit_lower_inverses(a, known)
    return inv, (inv, known)


def _uli_bwd(res, ct):
    inv, known = res
    right = _each(lambda ct, inv: _bdot_raw(ct, inv, "nt"), ct, inv)
    da = _each(lambda inv, r: -_bdot_raw(inv, r, "tn"), inv, right)
    return da, (None if known is None else _each(jnp.zeros_like, known))


_unit_lower_inverses.defvjp(_uli_fwd, _uli_bwd)


def _gdn_chunks(q, k, v, beta_rows, gam_rows, s, inv_known=None):
    n = q[0].shape[0]
    heads = range(len(q))
    row = lax.broadcasted_iota(jnp.int32, (n, n), 0)
    col = lax.broadcasted_iota(jnp.int32, (n, n), 1)
    beta_cols, gam_cols = beta_rows.T, gam_rows.T
    beta = tuple(beta_cols[:, g:g + 1] for g in heads)
    gam = tuple(gam_cols[:, g:g + 1] for g in heads)
    gam_row = tuple(gam_rows[g:g + 1, :] for g in heads)
    decay = _each(lambda gam, gam_row: jnp.where(row >= col, jnp.exp(jnp.minimum(gam - gam_row, 0.0)), 0.0), gam, gam_row)
    kb = _each(lambda k, beta: k * beta, k, beta)
    a = _each(lambda kb, k, decay: jnp.where(row > col, _bdot(kb, k, "nt") * decay, 0.0), kb, k, decay)
    inv = _unit_lower_inverses(a, inv_known)
    eg = _each(jnp.exp, gam)
    u = _each(lambda inv, v, beta: _bdot(inv, v * beta, "nn"), inv, v, beta)
    w = _each(lambda inv, kb, eg: _bdot(inv, kb * eg, "nn"), inv, kb, eg)
    qk = _each(lambda q, k, decay: _bdot(q, k, "nt") * decay, q, k, decay)
    v_new = _each(lambda u, w, s: u - _bdot(w, s, "nn"), u, w, s)
    o_state = _each(lambda q, eg, s: _bdot(q * eg, s, "nn"), q, eg, s)
    o = _each(lambda o_state, qk, v_new: o_state + _bdot(qk, v_new, "nn"), o_state, qk, v_new)
    gend = _each(lambda gam: gam[n - 1:n, :], gam)
    s_new = _each(lambda s, k, gam, gend, v_new: s * jnp.exp(gend) + _bdot(k * jnp.exp(gend - gam), v_new, "tn"), s, k, gam, gend, v_new)
    return o, s_new, inv


GDN_GROUP = 16


def _gdn_specs(nc, rev):
    cc = (lambda c: nc - 1 - c) if rev else (lambda c: c)
    grp = GDN_GROUP
    q = pl.BlockSpec((GDN_CHUNK, grp // 2 * HEAD), lambda h, c: (cc(c), h))
    k = pl.BlockSpec((GDN_CHUNK, grp // 2 * HEAD), lambda h, c: (cc(c), 2 * GDN_QK_HEADS // grp + h))
    v = pl.BlockSpec((GDN_CHUNK, grp * HEAD), lambda h, c: (cc(c), 2 * GDN_QK_HEADS // grp + h))
    o = pl.BlockSpec((GDN_CHUNK, grp * HEAD), lambda h, c: (cc(c), h))
    rw = pl.BlockSpec((grp, None, 1, GDN_CHUNK), lambda h, c: (h, cc(c), 0, 0))
    st = pl.BlockSpec((grp, None, HEAD, HEAD), lambda h, c: (h, cc(c), 0, 0))
    inv = pl.BlockSpec((grp, None, GDN_CHUNK, GDN_CHUNK), lambda h, c: (h, cc(c), 0, 0))
    return q, k, v, o, rw, st, inv


def _head_lanes(g, per=1):
    return pl.ds((g // per) * HEAD, HEAD)


def _gdn_rec_fwd(qk, c, beta_row, gam_row):
    t = qk.shape[0]
    nc = t // GDN_CHUNK
    q, k, v, o, rw, st, inv = _gdn_specs(nc, False)

    def body(q_ref, k_ref, v_ref, be_ref, gr_ref, o_ref, ss_ref, inv_ref, s_ref):
        @pl.when(pl.program_id(1) == 0)
        def _():
            s_ref[...] = jnp.zeros_like(s_ref)

        heads = range(GDN_GROUP)
        s = tuple(s_ref[g] for g in heads)
        out, s_new, inv_c = _gdn_chunks(
            tuple(q_ref[:, _head_lanes(g, 2)] for g in heads), tuple(k_ref[:, _head_lanes(g, 2)] for g in heads),
            tuple(v_ref[:, _head_lanes(g)] for g in heads), be_ref[:, 0, :], gr_ref[:, 0, :], s)
        for g in heads:
            ss_ref[g] = s[g]
            o_ref[:, _head_lanes(g)] = out[g]
            inv_ref[g] = inv_c[g]
            s_ref[g] = s_new[g]

    return pl.pallas_call(
        body, grid=(GDN_V_HEADS // GDN_GROUP, nc), in_specs=[q, k, v, rw, rw], out_specs=[o, st, inv],
        out_shape=[SDS((t, 2 * D_MODEL), F32), SDS((GDN_V_HEADS, nc, HEAD, HEAD), F32), SDS((GDN_V_HEADS, nc, GDN_CHUNK, GDN_CHUNK), F32)],
        scratch_shapes=[pltpu.VMEM((GDN_GROUP, HEAD, HEAD), F32)], name="gdn_rec", compiler_params=_params(2),
    )(qk, qk, c, beta_row, gam_row)


def _gdn_rec_bwd(qk, c, beta_row, gam_row, ss, invs, do):
    t = qk.shape[0]
    nc = t // GDN_CHUNK
    q, k, v, o, rw, st, inv = _gdn_specs(nc, True)

    def body(q_ref, k_ref, v_ref, be_ref, gr_ref, ss_ref, inv_ref, do_ref,
             dq_ref, dk_ref, dv_ref, dbe_ref, dgr_ref, ds_ref):
        @pl.when(pl.program_id(1) == 0)
        def _():
            ds_ref[...] = jnp.zeros_like(ds_ref)

        heads = range(GDN_GROUP)
        _, vjp = jax.vjp(
            _gdn_chunks,
            tuple(q_ref[:, _head_lanes(g, 2)] for g in heads), tuple(k_ref[:, _head_lanes(g, 2)] for g in heads),
            tuple(v_ref[:, _head_lanes(g)] for g in heads), be_ref[:, 0, :], gr_ref[:, 0, :],
            tuple(ss_ref[g] for g in heads), tuple(inv_ref[g] for g in heads))
        no_inv_ct = tuple(jnp.zeros((GDN_CHUNK, GDN_CHUNK), F32) for g in heads)
        dq, dk, dv, dbe, dgr, ds, _ = vjp((tuple(do_ref[:, _head_lanes(g)] for g in heads), tuple(ds_ref[g] for g in heads), no_inv_ct))
        for g in heads:
            dq_ref[:, _head_lanes(g)] = dq[g]
            dk_ref[:, _head_lanes(g)] = dk[g]
            dv_ref[:, _head_lanes(g)] = dv[g]
            ds_ref[g] = ds[g]
        dbe_ref[:, 0, :] = dbe
        dgr_ref[:, 0, :] = dgr

    wide = SDS((t, 2 * D_MODEL), F32)
    rowshape = SDS((GDN_V_HEADS, nc, 1, GDN_CHUNK), F32)
    return pl.pallas_call(
        body, grid=(GDN_V_HEADS // GDN_GROUP, nc), in_specs=[q, k, v, rw, rw, st, inv, o], out_specs=[o, o, o, rw, rw],
        out_shape=[wide, wide, wide, rowshape, rowshape],
        scratch_shapes=[pltpu.VMEM((GDN_GROUP, HEAD, HEAD), F32)], name="gdn_rec_bwd", compiler_params=_params(2),
    )(qk, qk, c, beta_row, gam_row, ss, invs, do)


def _gated_norm(o, gate, w):
    return _rms(o, w) * _silu(gate)


def _post_fwd(o, proj, col_off, w, name):
    t, width = o.shape
    tt = _tile(t, (256, 128))

    def fn(o, gate, w):
        return jnp.concatenate([_gated_norm(_head(o, h), _head(gate, h), w) for h in range(width // HEAD)], axis=1)

    return _tmap(fn, (1, t // tt),
                 [(o, *_rows(width, tt)), (proj, *_rows(width, tt, col_off * HEAD // width)), (w, (1, HEAD), lambda j, i: (0, 0))],
                 [((t, width), BF16, *_rows(width, tt), None)], name)[0]


def _post_bwd(o, proj, col_off, w, dout, name):
    t, width = o.shape
    tt = _tile(t, (256, 128))

    def fn(o, gate, w, dout):
        do, dgate, dw = [], [], jnp.zeros((1, HEAD), F32)
        for h in range(width // HEAD):
            _, vjp = jax.vjp(_gated_norm, _head(o, h), _head(gate, h), w)
            a, b, c = vjp(_head(dout, h))
            do.append(a)
            dgate.append(b)
            dw = dw + c
        return jnp.concatenate(do, axis=1), jnp.concatenate(dgate, axis=1), dw

    r = _rows(width, tt)
    return _tmap(fn, (1, t // tt),
                 [(o, *r), (proj, *_rows(width, tt, col_off * HEAD // width)), (w, (1, HEAD), lambda j, i: (0, 0)), (dout, *r)],
                 [((t, width), F32, *r, None), ((t, width), BF16, *r, None), ((1, HEAD), F32, (1, HEAD), lambda j, i: (0, 0), "inner")], name)


def _merge(gate_h, gate_g, yh, yg):
    return _sigmoid(gate_h) * yh + _sigmoid(gate_g) * yg


def _merge_fwd(proj, yh, yg):
    t = yh.shape[0]
    tt, ft = _tile(t, (256, 128)), 512
    r = _rows(ft, tt)
    return _tmap(_merge, (D_MODEL // ft, t // tt),
                 [(proj, *_rows(ft, tt, COL_GATE_H * HEAD // ft)), (proj, *_rows(ft, tt, COL_GATE_G * HEAD // ft)), (yh, *r), (yg, *r)],
                 [((t, D_MODEL), BF16, *r, None)], "merge")[0]


def _merge_bwd(proj, yh, yg, dy):
    t = yh.shape[0]
    tt, ft = _tile(t, (256, 128)), 512
    r = _rows(ft, tt)

    def fn(gate_h, gate_g, yh, yg, dy):
        _, vjp = jax.vjp(_merge, gate_h, gate_g, yh, yg)
        return vjp(dy)

    o = ((t, D_MODEL), BF16, *r, None)
    return _tmap(fn, (D_MODEL // ft, t // tt),
                 [(proj, *_rows(ft, tt, COL_GATE_H * HEAD // ft)), (proj, *_rows(ft, tt, COL_GATE_G * HEAD // ft)), (yh, *r), (yg, *r), (dy, *r)],
                 [o, o, o, o], "merge_bwd")


def _loss_head(h, target, g):
    t, d = h.shape
    tt = _tile(t, (256, 128))

    def fn(h, target, g):
        def f(h, g):
            err = _rms(h, g) - target
            return 0.5 * jnp.sum(jnp.mean(err * err, axis=-1))

        loss, (dh, dg) = jax.value_and_grad(f, (0, 1))(h, g)
        return dh, dg, jnp.full((1, HEAD), loss, F32)

    return _tmap(fn, (1, t // tt), [(h, *_rows(d, tt)), (target, *_rows(d, tt)), (g, (1, d), lambda j, i: (0, 0))],
                 [((t, d), F32, *_rows(d, tt), None), ((1, d), F32, (1, d), lambda j, i: (0, 0), "inner"),
                  ((1, HEAD), F32, (1, HEAD), lambda j, i: (0, 0), "inner")], "loss_head")


def _mixer_fwd(h, p, links):
    t = h.shape[0]
    nc = t // GDN_CHUNK
    u = _rms_fwd(h, p["mix_norm"], "mix_norm")
    w = {n: links.weight(n, h) for n in ("w_in_t", "w_in_b_t", "w_in_ab_t", "conv_w")}
    proj = _mm(u, w["w_in_t"], "nt", F32, "mix_in", after=links.started, b_rows=SCALAR_ROWS)
    proj_b = _mm(u, w["w_in_b_t"], "nt", F32, "mix_in_b")
    pab = _mm(u, w["w_in_ab_t"], "nt", F32, "mix_in_ab")
    qh, kh, bh = _hgrn_prep_fwd(proj, p["lbl"])
    oh, hs = _hgrn_rec_fwd(qh, kh, proj, bh)
    c, conv_y = _conv_fwd(proj, w["conv_w"])
    qk = _qk_norm_fwd(c)
    gates_t = _gates_fwd(pab, p["alog"], p["dtb"])
    gam_row = gates_t[:GDN_V_HEADS].reshape(GDN_V_HEADS, nc, 1, GDN_CHUNK)
    beta_row = gates_t[GDN_V_HEADS:2 * GDN_V_HEADS].reshape(GDN_V_HEADS, nc, 1, GDN_CHUNK)
    og, ss, invs = _gdn_rec_fwd(qk, c, beta_row, gam_row)
    ohn = _post_fwd(oh, proj, COL_HG, p["hgrn_out_norm"], "hgrn_out")
    ogn = _post_fwd(og, proj_b, COL_GZ, p["gdn_out_norm"], "gdn_out")
    w.update({n: links.weight(n, ogn) for n in ("w_branch_hgrn", "w_branch_gdn", "w_out")})
    yh = _mm(ohn, w["w_branch_hgrn"], "nn", BF16, "branch_hgrn")
    yg = _mm(ogn, w["w_branch_gdn"], "nn", BF16, "branch_gdn")
    y = _merge_fwd(proj_b, yh, yg)
    out = _mm(y, w["w_out"], "nn", F32, "mix_out", res=h)
    saved = (w, u, proj, proj_b, pab, qh, kh, bh, oh, hs, c, conv_y, qk, beta_row, gam_row, og, ss, invs, ohn, ogn, yh, yg, y)
    return out, saved


def _mixer_bwd(h, p, links, saved, dout):
    (w, u, proj, proj_b, pab, qh, kh, bh, oh, hs, c, conv_y, qk, beta_row, gam_row, og, ss, invs, ohn, ogn, yh, yg, y) = saved
    t = h.shape[0]
    grads = {}
    dw_out = _mm(y, dout, "tn", BF16, "mix_out_dw")
    dy = _mm(dout, w["w_out"], "nt", F32, "mix_out_dx")
    dgate_h, dgate_g, dyh, dyg = _merge_bwd(proj_b, yh, yg, dy)
    dw_bh = _mm(ohn, dyh, "tn", BF16, "branch_hgrn_dw")
    dw_bg = _mm(ogn, dyg, "tn", BF16, "branch_gdn_dw")
    sent = links.send({"w_out": dw_out, "w_branch_hgrn": dw_bh, "w_branch_gdn": dw_bg})
    dohn = _mm(dyh, w["w_branch_hgrn"], "nt", F32, "branch_hgrn_dx", after=sent)
    dogn = _mm(dyg, w["w_branch_gdn"], "nt", F32, "branch_gdn_dx")
    doh, dhg, grads["hgrn_out_norm"] = _post_bwd(oh, proj, COL_HG, p["hgrn_out_norm"], dohn, "hgrn_out_bwd")
    dog, dgz, grads["gdn_out_norm"] = _post_bwd(og, proj_b, COL_GZ, p["gdn_out_norm"], dogn, "gdn_out_bwd")
    dqh, dkh, dhi, dbh = _hgrn_rec_bwd(qh, kh, proj, bh, hs, doh)
    dhq, dhf, grads["lbl"] = _hgrn_prep_bwd(proj, p["lbl"], dqh, dkh, dbh)
    dqv, dkv, dcv, dbeta_row, dgam_row = _gdn_rec_bwd(qk, c, beta_row, gam_row, ss, invs, dog)
    dcqk = _qk_norm_bwd(c, dqv, dkv)
    dxin, grads["conv_w"] = _conv_bwd(proj, w["conv_w"], conv_y, dcqk, dcv)
    dgates_t = jnp.concatenate([dgam_row.reshape(GDN_V_HEADS, t), dbeta_row.reshape(GDN_V_HEADS, t),
                                jnp.zeros((HEAD - 2 * GDN_V_HEADS, t), F32)], axis=0)
    dpab, grads["alog"], grads["dtb"] = _gates_bwd(pab, p["alog"], p["dtb"], dgates_t)
    front, back = [dhq, dhf, dhi, dhg, dxin], [dgz, dgate_h, dgate_g]
    dw_front = [_mm(d, u, "tn", BF16, "mix_in_dw_%d" % i) for i, d in enumerate(front)]
    dw_back = [_mm(d, u, "tn", BF16, "mix_in_b_dw_%d" % i) for i, d in enumerate(back)]
    dw_ab_t = _mm(dpab, u, "tn", BF16, "mix_in_ab_dw")
    sent = links.send({"w_in": jnp.concatenate(dw_front + [dw_ab_t[:N_SCALAR]] + dw_back, axis=0)})
    du = _mm_pieces(front, w["w_in_t"], "mix_in_dx", after=sent)
    du = _mm_pieces(back, w["w_in_b_t"], "mix_in_b_dx", res=du)
    du = _mm(dpab, w["w_in_ab_t"], "nn", F32, "mix_in_ab_dx", res=du)
    dh, grads["mix_norm"] = _rms_bwd(h, p["mix_norm"], du, dout, "mix_norm_bwd")
    return dh, grads


def _local_step(x, target, p, links):
    def ffn_weights(tag, behind):
        def get(n):
            w_in_t, w_out = links.weight(tag + "_w_in", n), links.weight(tag + "_w_out", n)
            return w_in_t, w_out, links.started if behind else None
        return get

    h1, s1 = _ffn_fwd(x, p["ffn1_norm"] + links.started[0, 0], ffn_weights("ffn1", True), "ffn1")
    h2, sm = _mixer_fwd(h1, p, links)
    h3, s2 = _ffn_fwd(h2, p["ffn2_norm"], ffn_weights("ffn2", False), "ffn2")
    dh3, dfinal, loss = _loss_head(h3, target, p["final_norm"])
    g = {"final_norm": dfinal}
    dh2, g["ffn2_norm"] = _ffn_bwd(h2, p["ffn2_norm"], s2, dh3, "ffn2", links)
    dh1, gm = _mixer_bwd(h1, p, links, sm, dh2)
    g.update(gm)
    dx, g["ffn1_norm"] = _ffn_bwd(x, p["ffn1_norm"], s1, dh1, "ffn1", links)
    return loss, dx, g


HBM_SPEC = pl.BlockSpec(memory_space=pltpu.HBM)
SEM_SPEC = pl.BlockSpec(memory_space=pltpu.SEMAPHORE)
DATAFLOW = pltpu.SideEffectType.DATAFLOW_SIDE_EFFECTING


def _position():
    x, y, c = lax.axis_index("x"), lax.axis_index("y"), lax.axis_index("c")
    return x, y, c, 4 * x + 2 * y + c


def _relations(x, y, c):
    for rel in range(1, N_DEV):
        px = 1 - x if rel & 4 else x
        py = 1 - y if rel & 2 else y
        pc = 1 - c if rel & 1 else c
        yield rel, (px, py, pc), 4 * px + 2 * py + pc


def _sem_index(item, rel):
    return item * (N_DEV - 1) + rel - 1


def _landing(a, mode):
    return lax.empty((N_DEV,) + a.shape if mode == "gather" else a.shape, a.dtype)


ALL_PEERS = tuple(range(1, N_DEV))
ONE_PER_CHIP = (1, 2, 4, 6)


def _copies_start(groups, name, rels=ALL_PEERS):
    flat = [item for grp in groups for item in grp]
    n, ng = len(flat), len(groups)
    lands = [_landing(a, mode) for a, mode in flat]

    def body(*refs):
        src_refs, land_refs, sems, token = refs[:n], refs[n:2 * n], refs[2 * n:2 * n + 2 * ng], refs[-1]
        x, y, c, me = _position()
        for rel, where, peer in _relations(x, y, c):
            if rel not in rels:
                continue
            k = 0
            for gi, grp in enumerate(groups):
                for li, (_, mode) in enumerate(grp):
                    src = src_refs[k] if mode == "gather" else src_refs[k].at[peer]
                    pltpu.make_async_remote_copy(src_ref=src, dst_ref=land_refs[k].at[me], send_sem=sems[2 * gi].at[_sem_index(li, rel)],
                                                 recv_sem=sems[2 * gi + 1].at[_sem_index(li, rel)], device_id=where, device_id_type=MESH_IDS).start()
                    k += 1
        token[...] = jnp.zeros_like(token)

    sem_shapes = [pltpu.SemaphoreType.DMA((len(grp) * (N_DEV - 1),)) for grp in groups for _ in range(2)]
    thru = [pltpu.HBM(a.shape, a.dtype) for a, _ in flat] + [pltpu.HBM(l.shape, l.dtype) for l in lands]
    outs = pl.pallas_call(
        body, name=name, out_shape=(*sem_shapes, *thru, SDS((8, HEAD), F32)),
        in_specs=[HBM_SPEC] * (2 * n), out_specs=(*[SEM_SPEC] * (2 * ng), *[HBM_SPEC] * (2 * n), pl.BlockSpec(memory_space=pltpu.VMEM)),
        input_output_aliases={i: 2 * ng + i for i in range(2 * n)}, compiler_params=pltpu.CompilerParams(has_side_effects=DATAFLOW),
    )(*[pltpu.with_memory_space_constraint(a, pltpu.HBM) for a, _ in flat], *[pltpu.with_memory_space_constraint(l, pltpu.HBM) for l in lands])
    sems, srcs, landed, token = outs[:2 * ng], outs[2 * ng:2 * ng + n], outs[2 * ng + n:2 * ng + 2 * n], outs[-1]
    result, k = [], 0
    for gi, grp in enumerate(groups):
        result.append((sems[2 * gi], sems[2 * gi + 1], srcs[k:k + len(grp)], landed[k:k + len(grp)]))
        k += len(grp)
    return result, token


def _copies_wait(started, modes, after, name, rels=ALL_PEERS):
    send_sems, recv_sems, srcs, lands = started
    n = len(srcs)

    def body(*refs):
        src_refs, land_refs, ssem, rsem, token = refs[:n], refs[n:2 * n], refs[2 * n], refs[2 * n + 1], refs[-1]
        x, y, c, _ = _position()
        for rel in rels:
            for i, mode in enumerate(modes):
                src = src_refs[i] if mode == "gather" else src_refs[i].at[0]
                cp = pltpu.make_async_remote_copy(src_ref=src, dst_ref=land_refs[i].at[0], send_sem=ssem.at[_sem_index(i, rel)],
                                                  recv_sem=rsem.at[_sem_index(i, rel)], device_id=(x, y, c), device_id_type=MESH_IDS)
                cp.wait_send()
                cp.wait_recv()
        token[...] = jnp.zeros_like(token)

    outs = pl.pallas_call(
        body, name=name, out_shape=[pltpu.HBM(a.shape, a.dtype) for a in (*srcs, *lands)] + [SDS((8, HEAD), F32)],
        in_specs=[HBM_SPEC] * (2 * n) + [SEM_SPEC, SEM_SPEC, pl.BlockSpec(memory_space=pl.ANY)],
        out_specs=[HBM_SPEC] * (2 * n) + [pl.BlockSpec(memory_space=pltpu.VMEM)],
        input_output_aliases={i: i for i in range(2 * n)}, compiler_params=pltpu.CompilerParams(has_side_effects=DATAFLOW),
    )(*srcs, *lands, send_sems, recv_sems, after)
    return outs[:n], outs[n:2 * n], outs[-1]


OTHER_CHIPS = ((1, 0), (0, 1), (1, 1))


def _pass_on_start(lands, name):
    n = len(lands)

    def body(*refs):
        land_refs, ssem, rsem, token = refs[:n], refs[n], refs[n + 1], refs[-1]
        x, y, c, _ = _position()
        for j, (fx, fy) in enumerate(OTHER_CHIPS):
            slot = 4 * (1 - x if fx else x) + 2 * (1 - y if fy else y) + c
            for i in range(n):
                pltpu.make_async_remote_copy(src_ref=land_refs[i].at[slot], dst_ref=land_refs[i].at[slot], send_sem=ssem.at[i * len(OTHER_CHIPS) + j],
                                             recv_sem=rsem.at[i * len(OTHER_CHIPS) + j], device_id=(x, y, 1 - c), device_id_type=MESH_IDS).start()
        token[...] = jnp.zeros_like(token)

    sems = pltpu.SemaphoreType.DMA((n * len(OTHER_CHIPS),))
    outs = pl.pallas_call(
        body, name=name, out_shape=(sems, sems, *[pltpu.HBM(l.shape, l.dtype) for l in lands], SDS(TOKEN, F32)),
        in_specs=[HBM_SPEC] * n, out_specs=(SEM_SPEC, SEM_SPEC, *[HBM_SPEC] * n, pl.BlockSpec(memory_space=pltpu.VMEM)),
        input_output_aliases={i: 2 + i for i in range(n)}, compiler_params=pltpu.CompilerParams(has_side_effects=DATAFLOW),
    )(*lands)
    return (outs[0], outs[1], outs[2:2 + n]), outs[-1]


def _pass_on_wait(started, after, name):
    send_sems, recv_sems, lands = started
    n = len(lands)

    def body(*refs):
        land_refs, ssem, rsem = refs[:n], refs[n], refs[n + 1]
        x, y, c, _ = _position()
        for j in range(len(OTHER_CHIPS)):
            for i in range(n):
                cp = pltpu.make_async_remote_copy(src_ref=land_refs[i].at[0], dst_ref=land_refs[i].at[0], send_sem=ssem.at[i * len(OTHER_CHIPS) + j],
                                                  recv_sem=rsem.at[i * len(OTHER_CHIPS) + j], device_id=(x, y, c), device_id_type=MESH_IDS)
                cp.wait_send()
                cp.wait_recv()

    return pl.pallas_call(
        body, name=name, out_shape=[pltpu.HBM(l.shape, l.dtype) for l in lands],
        in_specs=[HBM_SPEC] * n + [SEM_SPEC, SEM_SPEC, pl.BlockSpec(memory_space=pl.ANY)], out_specs=[HBM_SPEC] * n,
        input_output_aliases={i: i for i in range(n)}, compiler_params=pltpu.CompilerParams(has_side_effects=DATAFLOW),
    )(*lands, send_sems, recv_sems, after)


WEIGHT_GROUPS = (("ffn1_w_in", "ffn1_w_out", "gdn_conv_w"), ("w_in",), ("w_branch_hgrn", "w_branch_gdn", "w_out", "ffn2_w_in", "ffn2_w_out"))
GROUP_RELS = (ONE_PER_CHIP, ONE_PER_CHIP, ALL_PEERS)


class _Links:
    def __init__(self, shards, me):
        self.me = me
        self.shards = shards
        self.weights = {}
        self.sends = []
        self.gathers = {}
        self.started = None
        self._start_gather(0, None)

    def _start_gather(self, gi, zeros):
        if gi < len(WEIGHT_GROUPS):
            items = [(self.shards[n] if zeros is None else self.shards[n] + zeros[0, 0].astype(self.shards[n].dtype), "gather")
                     for n in WEIGHT_GROUPS[gi]]
            started, self.started = _copies_start([items], "gather_start_%d" % gi, GROUP_RELS[gi])
            self.gathers[gi] = started[0]

    def weight(self, name, after):
        if name not in self.weights:
            source = {"w_in_t": "w_in", "w_in_b_t": "w_in", "w_in_ab_t": "w_in", "conv_w": "gdn_conv_w"}.get(name, name)
            gi = [i for i, grp in enumerate(WEIGHT_GROUPS) if source in grp][0]
            assert gi in self.gathers, "weight groups are asked for in order"
            srcs, lands, zero = _copies_wait(self.gathers[gi], ["gather"] * len(WEIGHT_GROUPS[gi]), after, "gather_wait_%d" % gi, GROUP_RELS[gi])
            if GROUP_RELS[gi] == ONE_PER_CHIP:
                passing, zero = _pass_on_start(lands, "gather_pass_%d" % gi)
                self._start_gather(gi + 1, zero)
                lands = _pass_on_wait(passing, self.started, "gather_passed_%d" % gi)
            else:
                self._start_gather(gi + 1, zero)
            for n, src, land in zip(WEIGHT_GROUPS[gi], srcs, lands):
                full = lax.dynamic_update_index_in_dim(land, src, self.me, 0)
                if n == "gdn_conv_w":
                    self.weights["conv_w"] = full.reshape(N_DEV, CONV_K, 4 * D_MODEL // N_DEV).transpose(1, 0, 2).reshape(CONV_K, 4 * D_MODEL)
                elif n == "w_in":
                    self.weights.update(_w_in_pieces(full.reshape(-1, D_MODEL)))
                else:
                    self.weights[n] = full.reshape(-1, D_MODEL)
        return self.weights[name]

    def send(self, grads):
        names = list(grads)
        blocks = [grads[n].reshape(N_DEV, -1, D_MODEL) for n in names]
        started, token = _copies_start([[(b, "scatter") for b in blocks]], "send_" + names[0])
        self.sends.append((names, started[0]))
        return token

    def landed(self, after):
        out = {}
        for names, started in self.sends:
            srcs, lands, _ = _copies_wait(started, ["scatter"] * len(names), after, "landed_" + names[0])
            for n, src, land in zip(names, srcs, lands):
                out[n] = lax.dynamic_update_index_in_dim(land, lax.dynamic_index_in_dim(src, self.me, 0, keepdims=False), self.me, 0)
        return out


def _adam(parts, w, m, v, name):
    n_parts, r, c = parts.shape
    tc = c if c <= 512 else (256 if r > 1024 else 512)

    def body(p_ref, w_ref, m_ref, v_ref, g_ref, d_ref, mo_ref, vo_ref):
        g = p_ref[0].astype(F32)
        for i in range(1, n_parts):
            g = g + p_ref[i].astype(F32)
        m_new = ADAM_B1 * m_ref[...] + (1.0 - ADAM_B1) * g
        v_new = ADAM_B2 * v_ref[...] + (1.0 - ADAM_B2) * (g * g)
        m_hat = m_new / (1.0 - ADAM_B1 ** ADAM_STEP)
        v_hat = v_new / (1.0 - ADAM_B2 ** ADAM_STEP)
        g_ref[...] = g
        d_ref[...] = -ADAM_LR * (m_hat / (jnp.sqrt(v_hat) + ADAM_EPS) + ADAM_WD * w_ref[...])
        mo_ref[...] = m_new
        vo_ref[...] = v_new

    spec = pl.BlockSpec((r, tc), lambda j: (0, j))
    return pl.pallas_call(
        body, grid=(c // tc,), in_specs=[pl.BlockSpec((n_parts, r, tc), lambda j: (0, 0, j)), spec, spec, spec],
        out_specs=[spec] * 4, out_shape=[SDS((r, c), F32)] * 4, name=name, compiler_params=_params(1),
    )(parts, w, m, v)


BIG = ("ffn1_w_in", "ffn1_w_out", "w_in", "w_branch_hgrn", "w_branch_gdn", "w_out", "ffn2_w_in", "ffn2_w_out")


TRANSPOSED = ("ffn1_w_in", "w_in", "ffn2_w_in")


def _shard_rows(name, shard):
    return shard.T if name in TRANSPOSED else shard


SCALAR_ROWS = 8192
N_SCALAR = 2 * GDN_V_HEADS


def _w_in_pieces(w_in_t):
    return {"w_in_t": w_in_t, "w_in_b_t": w_in_t[SCALAR_ROWS + N_SCALAR:],
            "w_in_ab_t": jnp.pad(w_in_t[SCALAR_ROWS:SCALAR_ROWS + N_SCALAR], ((0, HEAD - N_SCALAR), (0, 0)))}


def _pad_lanes(a, width=HEAD):
    return jnp.pad(a, ((0, 0), (0, width - a.shape[1])))


SMALL_ROWS = 24


def _pack_small(g, loss):
    row6 = jnp.concatenate([g["hgrn_out_norm"], g["gdn_out_norm"], g["alog"], g["dtb"], loss,
                            jnp.zeros((1, D_MODEL - 5 * HEAD), F32)], axis=1)
    return jnp.concatenate([g["ffn1_norm"], g["mix_norm"], g["lbl"], g["ffn2_norm"], g["final_norm"], row6,
                            jnp.zeros((1, D_MODEL), F32), g["conv_w"].reshape(4 * CONV_K, D_MODEL)], axis=0)


def _pack_small_state(a):
    row6 = jnp.concatenate([a["hgrn_out_norm"], a["gdn_out_norm"], _pad_lanes(a["gdn_a_log"]), _pad_lanes(a["gdn_dt_bias"]),
                            jnp.zeros((1, D_MODEL - 4 * HEAD), F32)], axis=1)
    return jnp.concatenate([a["ffn1_norm"], a["mix_norm"], a["hgrn_lb_logits"], a["ffn2_norm"], a["final_norm"].reshape(1, D_MODEL),
                            row6, jnp.zeros((1, D_MODEL), F32)], axis=0)


def _unpack_small(a):
    return {"ffn1_norm": a[0:1], "mix_norm": a[1:2], "hgrn_lb_logits": a[2:4], "ffn2_norm": a[4:5], "final_norm": a[5],
            "hgrn_out_norm": a[6:7, :HEAD], "gdn_out_norm": a[6:7, HEAD:2 * HEAD],
            "gdn_a_log": a[6:7, 2 * HEAD:2 * HEAD + GDN_V_HEADS], "gdn_dt_bias": a[6:7, 3 * HEAD:3 * HEAD + GDN_V_HEADS]}


NAMES = ("ffn1_norm", "ffn1_w_in", "ffn1_w_out", "mix_norm", "w_in", "hgrn_lb_logits", "hgrn_out_norm", "gdn_conv_w", "gdn_a_log",
         "gdn_dt_bias", "gdn_out_norm", "w_branch_hgrn", "w_branch_gdn", "w_out", "ffn2_norm", "ffn2_w_in", "ffn2_w_out", "final_norm")


def kernel(x, ffn1_norm, ffn1_w_in, ffn1_w_out, mix_norm, w_in, hgrn_lb_logits, hgrn_out_norm, gdn_conv_w, gdn_a_log, gdn_dt_bias, gdn_out_norm, w_branch_hgrn, w_branch_gdn, w_out, ffn2_norm, ffn2_w_in, ffn2_w_out, final_norm, loss_target, m_ffn1_norm, m_ffn1_w_in, m_ffn1_w_out, m_mix_norm, m_w_in, m_hgrn_lb_logits, m_hgrn_out_norm, m_gdn_conv_w, m_gdn_a_log, m_gdn_dt_bias, m_gdn_out_norm, m_w_branch_hgrn, m_w_branch_gdn, m_w_out, m_ffn2_norm, m_ffn2_w_in, m_ffn2_w_out, m_final_norm, v_ffn1_norm, v_ffn1_w_in, v_ffn1_w_out, v_mix_norm, v_w_in, v_hgrn_lb_logits, v_hgrn_out_norm, v_gdn_conv_w, v_gdn_a_log, v_gdn_dt_bias, v_gdn_out_norm, v_w_branch_hgrn, v_w_branch_gdn, v_w_out, v_ffn2_norm, v_ffn2_w_in, v_ffn2_w_out, v_final_norm):
    wts = dict(zip(NAMES, (ffn1_norm, ffn1_w_in, ffn1_w_out, mix_norm, w_in, hgrn_lb_logits, hgrn_out_norm, gdn_conv_w, gdn_a_log,
                           gdn_dt_bias, gdn_out_norm, w_branch_hgrn, w_branch_gdn, w_out, ffn2_norm, ffn2_w_in, ffn2_w_out, final_norm)))
    mom = dict(zip(NAMES, (m_ffn1_norm, m_ffn1_w_in, m_ffn1_w_out, m_mix_norm, m_w_in, m_hgrn_lb_logits, m_hgrn_out_norm, m_gdn_conv_w,
                           m_gdn_a_log, m_gdn_dt_bias, m_gdn_out_norm, m_w_branch_hgrn, m_w_branch_gdn, m_w_out, m_ffn2_norm, m_ffn2_w_in,
                           m_ffn2_w_out, m_final_norm)))
    var = dict(zip(NAMES, (v_ffn1_norm, v_ffn1_w_in, v_ffn1_w_out, v_mix_norm, v_w_in, v_hgrn_lb_logits, v_hgrn_out_norm, v_gdn_conv_w,
                           v_gdn_a_log, v_gdn_dt_bias, v_gdn_out_norm, v_w_branch_hgrn, v_w_branch_gdn, v_w_out, v_ffn2_norm, v_ffn2_w_in,
                           v_ffn2_w_out, v_final_norm)))
    me = 4 * lax.axis_index("x") + 2 * lax.axis_index("y") + lax.axis_index("c")

    conv_shard = wts["gdn_conv_w"][0]
    shards = {n: _shard_rows(n, wts[n][0]).astype(BF16) for n in BIG}
    shards["gdn_conv_w"] = conv_shard.reshape(2, D_MODEL)
    links = _Links(shards, me)
    p = {"ffn1_norm": wts["ffn1_norm"], "mix_norm": wts["mix_norm"], "ffn2_norm": wts["ffn2_norm"], "final_norm": wts["final_norm"].reshape(1, D_MODEL),
         "lbl": wts["hgrn_lb_logits"], "hgrn_out_norm": wts["hgrn_out_norm"], "gdn_out_norm": wts["gdn_out_norm"],
         "alog": _pad_lanes(wts["gdn_a_log"]), "dtb": _pad_lanes(wts["gdn_dt_bias"])}

    loss, dx, g = _local_step(x[0], loss_target[0], p, links)

    small_started, small_token = _copies_start([[(_pack_small(g, loss), "gather")]], "small_start")
    landed = links.landed(small_token)

    big = [{} for _ in range(4)]
    for n in BIG:
        res = _adam(landed[n], _shard_rows(n, wts[n][0]), _shard_rows(n, mom[n][0]), _shard_rows(n, var[n][0]), "adam_" + n)
        for kind in range(4):
            big[kind][n] = _shard_rows(n, res[kind])
    small_srcs, small_lands, _ = _copies_wait(small_started[0], ["gather"], res[0], "small_wait")
    small_parts = lax.dynamic_update_index_in_dim(small_lands[0], small_srcs[0], me, 0)
    n_vec = SMALL_ROWS - 4 * CONV_K
    small_raw = _adam(small_parts[:, :n_vec], _pack_small_state(wts), _pack_small_state(mom), _pack_small_state(var), "adam_small")
    small = [_unpack_small(o) for o in small_raw]
    loss_total = small_raw[0][6, 4 * HEAD]
    conv_parts = small_parts[:, n_vec:].reshape(N_DEV, CONV_K, 4 * D_MODEL)
    width = 4 * D_MODEL // N_DEV
    conv_mine = lax.dynamic_slice_in_dim(conv_parts, me * width, width, axis=2)
    conv = _adam(conv_mine, conv_shard, mom["gdn_conv_w"][0], var["gdn_conv_w"][0], "adam_conv")

    outs = []
    for kind in range(4):
        for n in NAMES:
            if n in BIG:
                outs.append(big[kind][n][None])
            elif n == "gdn_conv_w":
                outs.append(conv[kind][None])
            else:
                outs.append(small[kind][n])
    return (loss_total, dx[None], *outs)
```

```python
import functools

import jax
import jax.numpy as jnp
from jax import lax
from jax.experimental import pallas as pl
from jax.experimental.pallas import tpu as pltpu

F32 = jnp.float32
BF16 = jnp.bfloat16
HIGHEST = lax.Precision.HIGHEST
MESH_IDS = pl.DeviceIdType.MESH

D_MODEL = 1024
D_FF = 2816
N_DEV = 8
EPS = 1e-6
HEAD = 128
HG_HEADS = 8
GDN_QK_HEADS = 8
GDN_V_HEADS = 16
GDN_CHUNK = 64
HG_CHUNK = 16
CONV_K = 4
IN_WIDTH = 12320
COL_HQ, COL_HF, COL_HI, COL_HG, COL_GQ, COL_GK, COL_GV = 0, 8, 16, 24, 32, 40, 48
COL_GZ, COL_GATE_H, COL_GATE_G = 0, 16, 24
VMEM_LIMIT = 56 * 1024 * 1024

ADAM_LR, ADAM_B1, ADAM_B2, ADAM_EPS, ADAM_WD, ADAM_STEP = 0.001, 0.9, 0.999, 1e-08, 0.01, 10

SDS = jax.ShapeDtypeStruct


def _params(n_axes):
    return pltpu.CompilerParams(dimension_semantics=("arbitrary",) * n_axes, vmem_limit_bytes=VMEM_LIMIT)


def _tile(n, candidates=(512, 384, 256, 128, 64, 32, 16, 8)):
    for c in candidates:
        if n % c == 0:
            return c
    return n


_DIMS = {"nn": ((1,), (0,)), "nt": ((1,), (1,)), "tn": ((0,), (0,))}


def _bdot_raw(a, b, dims):
    return lax.dot_general(a.astype(BF16), b.astype(BF16), (_DIMS[dims], ((), ())), preferred_element_type=F32)


@functools.partial(jax.custom_vjp, nondiff_argnums=(2,))
def _bdot(a, b, dims):
    return _bdot_raw(a, b, dims)


def _bdot_fwd(a, b, dims):
    return _bdot_raw(a, b, dims), (a, b)


def _bdot_bwd(dims, res, ct):
    a, b = res
    if dims == "nn":
        return _bdot_raw(ct, b, "nt"), _bdot_raw(a, ct, "tn")
    if dims == "nt":
        return _bdot_raw(ct, b, "nn"), _bdot_raw(ct, a, "tn")
    return _bdot_raw(b, ct, "nt"), _bdot_raw(a, ct, "nn")


_bdot.defvjp(_bdot_fwd, _bdot_bwd)


def _hdot_raw(a, b):
    return jnp.dot(a, b, precision=HIGHEST, preferred_element_type=F32)


MM_VMEM_BUDGET = 38 * 1024 * 1024
TOKEN = (8, HEAD)


def _mm_tiles(m, n, k, a_bytes, b_bytes, o_bytes, r_bytes, m_align=8):
    def need(tm, tn, tk):
        return 2 * (tm * tk * a_bytes + tk * tn * b_bytes + tm * tn * (o_bytes + r_bytes)) + (tm * tn * 4 if tk < k else 0)

    def shrink(tm, tn, tk, floor_m, floor_n):
        while need(tm, tn, tk) > MM_VMEM_BUDGET:
            if tn > floor_n and tn % 256 == 0 and tn >= tm:
                tn //= 2
            elif tm > floor_m and tm % (2 * m_align) == 0:
                tm //= 2
            elif tn > floor_n and tn % 256 == 0:
                tn //= 2
            else:
                return None
        return tm, tn, tk

    tm = _tile(m, (1408, 1024, 704, 512, 256, 128, 64, 32, 16, 8))
    tn = _tile(n, (1408, 1024, 512, 256, 128))
    whole = shrink(tm, tn, k, min(tm, 1024), min(tn, 512))
    if whole is not None:
        return whole
    tk = _tile(k, (2048, 1408, 1024, 512, 256, 128, 64, 32, 16, 8))
    while True:
        fit = shrink(tm, tn, tk, min(tm, 256), min(tn, 512))
        if fit is not None or tk <= 512 or tk % 256:
            return fit if fit is not None else (tm, tn, tk)
        tk //= 2


def _mm(a, b, dims, out_dtype, name, res=None, alpha=1.0, after=None, b_rows=None):
    b_shape = b.shape if b_rows is None else (b_rows, b.shape[1])
    if dims == "nn":
        (m, k), (k2, n) = a.shape, b_shape
    elif dims == "nt":
        (m, k), (n, k2) = a.shape, b_shape
    else:
        (k, m), (k2, n) = a.shape, b_shape
    assert k == k2, (a.shape, b.shape, dims)
    has_res = res is not None
    tm, tn, tk = _mm_tiles(m, n, k, a.dtype.itemsize, b.dtype.itemsize, jnp.dtype(out_dtype).itemsize, res.dtype.itemsize if has_res else 0,
                           m_align=HEAD if dims == "tn" else 8)
    nk = k // tk
    a_spec = pl.BlockSpec((tk, tm), lambda i, j, kk: (kk, i)) if dims == "tn" else pl.BlockSpec((tm, tk), lambda i, j, kk: (i, kk))
    b_spec = pl.BlockSpec((tn, tk), lambda i, j, kk: (j, kk)) if dims == "nt" else pl.BlockSpec((tk, tn), lambda i, j, kk: (kk, j))
    o_spec = pl.BlockSpec((tm, tn), lambda i, j, kk: (i, j))

    def finish(acc, r_ref, o_ref):
        out = acc * alpha if alpha != 1.0 else acc
        if has_res:
            out = r_ref[...].astype(F32) + out
        o_ref[...] = out.astype(o_ref.dtype)

    n_in = 2 + has_res + (after is not None)

    def body(*refs):
        a_ref, b_ref = refs[:2]
        r_ref = refs[2] if has_res else None
        o_ref = refs[n_in]
        p = _bdot_raw(a_ref[...], b_ref[...], dims)
        if nk == 1:
            finish(p, r_ref, o_ref)
            return
        acc_ref = refs[-1]
        kk = pl.program_id(2)

        @pl.when(kk == 0)
        def _():
            acc_ref[...] = p

        @pl.when(kk > 0)
        def _():
            acc_ref[...] += p

        @pl.when(kk == nk - 1)
        def _():
            finish(acc_ref[...], r_ref, o_ref)

    args = (a, b) + ((res,) if has_res else ()) + ((after,) if after is not None else ())
    in_specs = [a_spec, b_spec] + ([o_spec] if has_res else []) + ([pl.BlockSpec(TOKEN, lambda i, j, kk: (0, 0))] if after is not None else [])
    return pl.pallas_call(
        body, grid=(m // tm, n // tn, nk), in_specs=in_specs, out_specs=o_spec, out_shape=SDS((m, n), out_dtype),
        scratch_shapes=[pltpu.VMEM((tm, tn), F32)] if nk > 1 else [], name=name, compiler_params=_params(3),
    )(*args)


PIECE_TK = 1024


def _mm_pieces(pieces, b, name, res=None, after=None):
    m, n = pieces[0].shape[0], b.shape[1]
    blocks = [p.shape[1] // PIECE_TK for p in pieces]
    assert all(p.shape[1] % PIECE_TK == 0 and p.shape[0] == m for p in pieces)
    starts = [sum(blocks[:i]) for i in range(len(pieces))]
    nk = sum(blocks)
    tm, tn = _tile(m, (1024, 512, 256, 128)), _tile(n, (1024, 512, 256, 128))
    n_p = len(pieces)
    n_in = n_p + 1 + (res is not None) + (after is not None)

    def piece_spec(start, count):
        return pl.BlockSpec((tm, PIECE_TK), lambda i, j, kk: (i, jnp.clip(kk - start, 0, count - 1)))

    def body(*refs):
        b_ref, o_ref, acc_ref = refs[n_p], refs[n_in], refs[-1]
        kk = pl.program_id(2)

        @pl.when(kk == 0)
        def _():
            acc_ref[...] = jnp.zeros_like(acc_ref)

        for p_ref, start, count in zip(refs[:n_p], starts, blocks):
            @pl.when(jnp.logical_and(kk >= start, kk < start + count))
            def _(p_ref=p_ref):
                acc_ref[...] += _bdot_raw(p_ref[...], b_ref[...], "nn")

        @pl.when(kk == nk - 1)
        def _():
            out = acc_ref[...]
            if res is not None:
                out = refs[n_p + 1][...] + out
            o_ref[...] = out

    o_spec = pl.BlockSpec((tm, tn), lambda i, j, kk: (i, j))
    in_specs = [piece_spec(s, c) for s, c in zip(starts, blocks)] + [pl.BlockSpec((PIECE_TK, tn), lambda i, j, kk: (kk, j))]
    args = list(pieces) + [b]
    if res is not None:
        in_specs.append(o_spec)
        args.append(res)
    if after is not None:
        in_specs.append(pl.BlockSpec(TOKEN, lambda i, j, kk: (0, 0)))
        args.append(after)
    return pl.pallas_call(
        body, grid=(m // tm, n // tn, nk), in_specs=in_specs, out_specs=o_spec, out_shape=SDS((m, n), F32),
        scratch_shapes=[pltpu.VMEM((tm, tn), F32)], name=name, compiler_params=_params(3),
    )(*args)


def _tmap(fn, grid, ins, outs, name):
    n_in = len(ins)
    n_ax = len(grid)

    def body(*refs):
        vals = fn(*[r[...] for r in refs[:n_in]])
        if not isinstance(vals, (tuple, list)):
            vals = (vals,)
        first_inner = pl.program_id(n_ax - 1) == 0
        first_all = first_inner
        for ax in range(n_ax - 1):
            first_all = jnp.logical_and(first_all, pl.program_id(ax) == 0)

        def put(ref, val, acc):
            val = val.astype(ref.dtype)
            if acc is None:
                ref[...] = val
                return
            first = first_inner if acc == "inner" else first_all

            @pl.when(first)
            def _():
                ref[...] = val

            @pl.when(jnp.logical_not(first))
            def _():
                ref[...] += val

        for ref, val, o in zip(refs[n_in:], vals, outs):
            put(ref, val, o[4])

    return pl.pallas_call(
        body, grid=grid,
        in_specs=[pl.BlockSpec(bs, im) for _, bs, im in ins],
        out_specs=[pl.BlockSpec(o[2], o[3]) for o in outs],
        out_shape=[SDS(o[0], o[1]) for o in outs],
        name=name, compiler_params=_params(n_ax),
    )(*[a for a, _, _ in ins])


def _rows(width, tt, off=0):
    return (tt, width), (lambda j, i: (i, off + j))


def _rms(x, g):
    x = x.astype(F32)
    return x * lax.rsqrt(jnp.mean(x * x, axis=-1, keepdims=True) + EPS) * g


def _sigmoid(x):
    return jax.nn.sigmoid(x)


def _silu(x):
    return x * _sigmoid(x)


def _softplus(x):
    return jnp.maximum(x, 0.0) + jnp.log1p(jnp.exp(-jnp.abs(x)))


def _rms_fwd(x, g, name):
    t, d = x.shape
    tt = _tile(t, (256, 128))
    return _tmap(_rms, (1, t // tt), [(x, *_rows(d, tt)), (g, (1, d), lambda j, i: (0, 0))],
                 [((t, d), BF16, *_rows(d, tt), None)], name)[0]


def _rms_bwd(x, g, dn, dres, name):
    t, d = x.shape
    tt = _tile(t, (256, 128))

    def fn(x, g, dn, dres):
        _, vjp = jax.vjp(_rms, x, g)
        dx, dg = vjp(dn.astype(F32))
        return dres + dx, dg

    return _tmap(fn, (1, t // tt),
                 [(x, *_rows(d, tt)), (g, (1, d), lambda j, i: (0, 0)), (dn, *_rows(d, tt)), (dres, *_rows(d, tt))],
                 [((t, d), F32, *_rows(d, tt), None), ((1, d), F32, (1, d), lambda j, i: (0, 0), "inner")], name)


def _swiglu(a, b):
    return _silu(a) * b


def _ffn_in_act(n, w_in_t, name, after):
    t, d = n.shape
    tm, tn = _tile(t, (512, 256, 128)), D_FF // 2
    half_blocks = D_FF // tn
    n_in = 3 + (after is not None)

    def body(*refs):
        n_ref, wa_ref, wb_ref = refs[:3]
        a_ref, b_ref, s_ref = refs[n_in:]
        x = n_ref[...]
        a = _bdot_raw(x, wa_ref[...], "nt").astype(BF16)
        b = _bdot_raw(x, wb_ref[...], "nt").astype(BF16)
        a_ref[...] = a
        b_ref[...] = b
        s_ref[...] = _swiglu(a.astype(F32), b.astype(F32)).astype(BF16)

    out = pl.BlockSpec((tm, tn), lambda i, j: (i, j))
    in_specs = [pl.BlockSpec((tm, d), lambda i, j: (i, 0)), pl.BlockSpec((tn, d), lambda i, j: (j, 0)),
                pl.BlockSpec((tn, d), lambda i, j: (j + half_blocks, 0))]
    args = [n, w_in_t, w_in_t]
    if after is not None:
        in_specs.append(pl.BlockSpec(TOKEN, lambda i, j: (0, 0)))
        args.append(after)
    return pl.pallas_call(
        body, grid=(t // tm, half_blocks), in_specs=in_specs, out_specs=[out, out, out], out_shape=[SDS((t, D_FF), BF16)] * 3,
        name=name, compiler_params=_params(2),
    )(*args)


def _swiglu_bwd(a, b, ds, name):
    t = a.shape[0]
    tt = _tile(t, (128,))

    def fn(a, b, ds):
        _, vjp = jax.vjp(_swiglu, a.astype(F32), b.astype(F32))
        da, db = vjp(ds.astype(F32))
        return jnp.concatenate([da, db], axis=1)

    r = _rows(D_FF, tt)
    return _tmap(fn, (1, t // tt), [(a, *r), (b, *r), (ds, *r)], [((t, 2 * D_FF), BF16, *_rows(2 * D_FF, tt), None)], name)[0]


def _ffn_fwd(h, g, weights, tag):
    n = _rms_fwd(h, g, tag + "_norm")
    w_in_t, w_out, after = weights(n)
    a, b, s = _ffn_in_act(n, w_in_t, tag + "_in", after)
    out = _mm(s, w_out, "nn", F32, tag + "_out", res=h, alpha=0.5)
    return out, (n, a, b, s, w_in_t, w_out)


def _ffn_bwd(h, g, saved, dout, tag, links):
    n, a, b, s, w_in_t, w_out = saved
    sent = links.send({tag + "_w_out": _mm(s, dout, "tn", BF16, tag + "_dw_out", alpha=0.5)})
    ds = _mm(dout, w_out, "nt", BF16, tag + "_ds", alpha=0.5, after=sent)
    dab = _swiglu_bwd(a, b, ds, tag + "_dact")
    sent = links.send({tag + "_w_in": _mm(dab, n, "tn", BF16, tag + "_dw_in")})
    dn = _mm(dab, w_in_t, "nn", F32, tag + "_dn", after=sent)
    return _rms_bwd(h, g, dn, dout, tag + "_dnorm")


def _chunk_sum_matrix(n, chunk, transpose=False):
    row = lax.broadcasted_iota(jnp.int32, (n, n), 0)
    col = lax.broadcasted_iota(jnp.int32, (n, n), 1)
    if transpose:
        row, col = col, row
    return jnp.where(jnp.logical_and(col <= row, row // chunk == col // chunk), 1.0, 0.0).astype(F32)


def _hgrn_gates(hq, hf, lbl):
    lb = _sigmoid(lbl[0:1, :] - lbl[1:2, :])
    sg = _sigmoid(hf)
    f = lb + (1.0 - lb) * sg
    q = _silu(hq) * HEAD ** -0.5
    k = (1.0 - lb) * (1.0 - sg)
    return q, k, jnp.log(f)


def _hgrn_prep_fwd(proj, lbl):
    t = proj.shape[0]
    tt, ft = _tile(t, (256, 128)), 512

    def fn(hq, hf, lbl):
        q, k, log_f = _hgrn_gates(hq, hf, lbl)
        return q, k, _hdot_raw(_chunk_sum_matrix(tt, HG_CHUNK), log_f)

    o = ((t, D_MODEL), F32, *_rows(ft, tt), None)
    return _tmap(fn, (D_MODEL // ft, t // tt),
                 [(proj, *_rows(ft, tt, COL_HQ * HEAD // ft)), (proj, *_rows(ft, tt, COL_HF * HEAD // ft)), (lbl, (2, ft), lambda j, i: (0, j))],
                 [o, o, o], "hgrn_prep")


def _hgrn_prep_bwd(proj, lbl, dq, dk, db):
    t = proj.shape[0]
    tt, ft = _tile(t, (256, 128)), 512

    def fn(hq, hf, lbl, dq, dk, db):
        dlog_f = _hdot_raw(_chunk_sum_matrix(tt, HG_CHUNK, transpose=True), db)
        _, vjp = jax.vjp(_hgrn_gates, hq, hf, lbl)
        return vjp((dq, dk, dlog_f))

    o = ((t, D_MODEL), BF16, *_rows(ft, tt), None)
    r = _rows(ft, tt)
    return _tmap(fn, (D_MODEL // ft, t // tt),
                 [(proj, *_rows(ft, tt, COL_HQ * HEAD // ft)), (proj, *_rows(ft, tt, COL_HF * HEAD // ft)), (lbl, (2, ft), lambda j, i: (0, j)),
                  (dq, *r), (dk, *r), (db, *r)],
                 [o, o, ((2, D_MODEL), F32, (2, ft), lambda j, i: (0, j), "inner")], "hgrn_prep_bwd")


@functools.partial(jax.custom_vjp, nondiff_argnums=(1,))
def _roll_rows(x, d):
    return pltpu.roll(x, d, 0)


def _roll_rows_fwd(x, d):
    return pltpu.roll(x, d, 0), None


def _roll_rows_bwd(d, _, ct):
    return (pltpu.roll(ct, ct.shape[0] - d, 0),)


_roll_rows.defvjp(_roll_rows_fwd, _roll_rows_bwd)


def _hgrn_chunks(q, k, v, b, st):
    n = q[0].shape[0]
    half = n // 2
    srow = lax.broadcasted_iota(jnp.int32, (half, HEAD), 0)
    inter = _each(lambda q, b, st: _bdot(q * jnp.exp(b), st, "nt"), q, b, st)

    def below_scores(q, k, b):
        ref = b[half:half + 1, :]
        return _bdot(q[half:] * jnp.exp(jnp.minimum(b[half:] - ref, 0.0)), k[:half] * jnp.exp(jnp.minimum(ref - b[:half], 0.0)), "nt")

    below = _each(lambda a, v: _bdot(a, v[:half], "nn"), _each(below_scores, q, k, b), v)

    def diagonal(q, k, v, b):
        blocks = []
        for lo in (0, half):
            qb, kb, vb, bb = (a[lo:lo + half] for a in (q, k, v, b))
            o = jnp.sum(qb * kb, axis=1, keepdims=True) * vb
            for d in range(1, half):
                kr, vr, br = _roll_rows(kb, d), _roll_rows(vb, d), _roll_rows(bb, d)
                a = jnp.sum(qb * kr * jnp.exp(jnp.minimum(bb - br, 0.0)), axis=1, keepdims=True)
                o = o + jnp.where(srow[:, :1] >= d, a, 0.0) * vr
            blocks.append(o)
        return jnp.concatenate(blocks, axis=0)

    diag = _each(diagonal, q, k, v, b)
    o = _each(lambda inter, diag, below: inter + diag + jnp.concatenate([jnp.zeros_like(below), below], axis=0), inter, diag, below)

    def new_state(k, v, b, st):
        bend = b[n - 1:n, :]
        return st * jnp.exp(bend) + _bdot(v, k * jnp.exp(bend - b), "tn")

    return o, _each(new_state, k, v, b, st)


HG_GROUP = 8
HG_PER = GDN_CHUNK // HG_CHUNK


def _hgrn_rec_fwd(q, k, proj, b):
    t = q.shape[0]
    nc = t // GDN_CHUNK
    blk = (GDN_CHUNK, HG_GROUP * HEAD)
    im = lambda h, c: (c, h)

    def body(q_ref, k_ref, v_ref, b_ref, o_ref, hs_ref, st_ref):
        @pl.when(pl.program_id(1) == 0)
        def _():
            st_ref[...] = jnp.zeros_like(st_ref)

        heads = range(HG_GROUP)
        for j in range(HG_PER):
            sl = pl.ds(HG_CHUNK * j, HG_CHUNK)
            st = tuple(st_ref[g] for g in heads)
            o, st_new = _hgrn_chunks(*[tuple(r[sl, _head_lanes(g)] for g in heads) for r in (q_ref, k_ref, v_ref, b_ref)], st)
            for g in heads:
                hs_ref[g, j] = st[g]
                o_ref[sl, _head_lanes(g)] = o[g]
                st_ref[g] = st_new[g]

    return pl.pallas_call(
        body, grid=(HG_HEADS // HG_GROUP, nc),
        in_specs=[pl.BlockSpec(blk, im), pl.BlockSpec(blk, im), pl.BlockSpec(blk, lambda h, c: (c, COL_HI // HG_GROUP + h)), pl.BlockSpec(blk, im)],
        out_specs=[pl.BlockSpec(blk, im), pl.BlockSpec((HG_GROUP, HG_PER, HEAD, HEAD), lambda h, c: (h, c, 0, 0))],
        out_shape=[SDS((t, D_MODEL), F32), SDS((HG_HEADS, nc * HG_PER, HEAD, HEAD), F32)],
        scratch_shapes=[pltpu.VMEM((HG_GROUP, HEAD, HEAD), F32)], name="hgrn_rec", compiler_params=_params(2),
    )(q, k, proj, b)


def _hgrn_rec_bwd(q, k, proj, b, hs, do):
    t = q.shape[0]
    nc = t // GDN_CHUNK
    blk = (GDN_CHUNK, HG_GROUP * HEAD)
    im = lambda h, c: (nc - 1 - c, h)

    def body(q_ref, k_ref, v_ref, b_ref, hs_ref, do_ref, dq_ref, dk_ref, dv_ref, db_ref, dst_ref):
        @pl.when(pl.program_id(1) == 0)
        def _():
            dst_ref[...] = jnp.zeros_like(dst_ref)

        heads = range(HG_GROUP)
        for j in reversed(range(HG_PER)):
            sl = pl.ds(HG_CHUNK * j, HG_CHUNK)
            _, vjp = jax.vjp(_hgrn_chunks, *[tuple(r[sl, _head_lanes(g)] for g in heads) for r in (q_ref, k_ref, v_ref, b_ref)],
                             tuple(hs_ref[g, j] for g in heads))
            dq, dk, dv, db, dst = vjp((tuple(do_ref[sl, _head_lanes(g)] for g in heads), tuple(dst_ref[g] for g in heads)))
            for g in heads:
                ln = _head_lanes(g)
                dq_ref[sl, ln] = dq[g]
                dk_ref[sl, ln] = dk[g]
                dv_ref[sl, ln] = dv[g].astype(dv_ref.dtype)
                db_ref[sl, ln] = db[g]
                dst_ref[g] = dst[g]

    spec = pl.BlockSpec(blk, im)
    return pl.pallas_call(
        body, grid=(HG_HEADS // HG_GROUP, nc),
        in_specs=[spec, spec, pl.BlockSpec(blk, lambda h, c: (nc - 1 - c, COL_HI // HG_GROUP + h)), spec,
                  pl.BlockSpec((HG_GROUP, HG_PER, HEAD, HEAD), lambda h, c: (h, nc - 1 - c, 0, 0)), spec],
        out_specs=[spec, spec, spec, spec],
        out_shape=[SDS((t, D_MODEL), F32), SDS((t, D_MODEL), F32), SDS((t, D_MODEL), BF16), SDS((t, D_MODEL), F32)],
        scratch_shapes=[pltpu.VMEM((HG_GROUP, HEAD, HEAD), F32)], name="hgrn_rec_bwd", compiler_params=_params(2),
    )(q, k, proj, b, hs, do)


def _shift_down(x, d):
    if d == 0:
        return x
    row = lax.broadcasted_iota(jnp.int32, x.shape, 0)
    return jnp.where(row >= d, pltpu.roll(x, d, 0), 0.0)


def _shift_up(x, d):
    if d == 0:
        return x
    n = x.shape[0]
    row = lax.broadcasted_iota(jnp.int32, x.shape, 0)
    return jnp.where(row < n - d, pltpu.roll(x, n - d, 0), 0.0)


def _conv_fwd(proj, conv_w):
    t = proj.shape[0]
    width = 2 * D_MODEL + 2 * D_MODEL

    def body(x_ref, w_ref, c_ref, y_ref):
        x, w = x_ref[...], w_ref[...]
        y = w[CONV_K - 1:CONV_K, :] * x
        for j in range(CONV_K - 1):
            y = y + w[j:j + 1, :] * _shift_down(x, CONV_K - 1 - j)
        y_ref[...] = y
        c_ref[...] = _silu(y)

    out = pl.BlockSpec((t, HEAD), lambda j: (0, j))
    return pl.pallas_call(
        body, grid=(width // HEAD,),
        in_specs=[pl.BlockSpec((t, HEAD), lambda j: (0, COL_GQ + j)), pl.BlockSpec((CONV_K, HEAD), lambda j: (0, j))],
        out_specs=[out, out], out_shape=[SDS((t, width), F32), SDS((t, width), F32)],
        name="gdn_conv", compiler_params=_params(1),
    )(proj, conv_w)


def _conv_bwd(proj, conv_w, y, dc_qk, dc_v):
    t = proj.shape[0]
    n_qk = dc_qk.shape[1] // HEAD
    width = dc_qk.shape[1] + dc_v.shape[1]

    def body(x_ref, w_ref, y_ref, dqk_ref, dv_ref, dx_ref, dw_ref):
        x, w, y = x_ref[...], w_ref[...], y_ref[...]
        sg = _sigmoid(y)
        dc = jnp.where(pl.program_id(0) < n_qk, dqk_ref[...], dv_ref[...])
        dy = dc * (sg * (1.0 + y * (1.0 - sg)))
        ahead = [_shift_up(dy, CONV_K - 1 - j) for j in range(CONV_K)]
        dx = w[0:1, :] * ahead[0]
        for j in range(1, CONV_K):
            dx = dx + w[j:j + 1, :] * ahead[j]
        dx_ref[...] = dx.astype(dx_ref.dtype)
        dw_ref[...] = jnp.concatenate([jnp.sum(x * ahead[j], axis=0, keepdims=True) for j in range(CONV_K)], axis=0)

    blk = pl.BlockSpec((t, HEAD), lambda j: (0, j))
    return pl.pallas_call(
        body, grid=(width // HEAD,),
        in_specs=[pl.BlockSpec((t, HEAD), lambda j: (0, COL_GQ + j)), pl.BlockSpec((CONV_K, HEAD), lambda j: (0, j)), blk,
                  pl.BlockSpec((t, HEAD), lambda j: (0, jnp.minimum(j, n_qk - 1))), pl.BlockSpec((t, HEAD), lambda j: (0, jnp.maximum(j - n_qk, 0)))],
        out_specs=[blk, pl.BlockSpec((CONV_K, HEAD), lambda j: (0, j))],
        out_shape=[SDS((t, width), BF16), SDS((CONV_K, width), F32)],
        name="gdn_conv_bwd", compiler_params=_params(1),
    )(proj, conv_w, y, dc_qk, dc_v)


def _l2norm(x, scale):
    return x * lax.rsqrt(jnp.sum(x * x, axis=-1, keepdims=True) + EPS) * scale


def _head(a, h):
    return a[:, h * HEAD:(h + 1) * HEAD]


def _qk_scale(h):
    return HEAD ** -0.5 if h < GDN_QK_HEADS else 1.0


def _qk_norm_fwd(c):
    t = c.shape[0]
    tt = _tile(t, (256, 128))
    width = 2 * D_MODEL

    def fn(x):
        return jnp.concatenate([_l2norm(_head(x, h), _qk_scale(h)) for h in range(2 * GDN_QK_HEADS)], axis=1)

    return _tmap(fn, (1, t // tt), [(c, *_rows(width, tt))], [((t, width), F32, *_rows(width, tt), None)], "gdn_qk_norm")[0]


def _qk_norm_bwd(c, dq_rep, dk_rep):
    t = c.shape[0]
    tt = _tile(t, (256, 128))
    width = 2 * D_MODEL

    def fn(x, dq2, dk2):
        out = []
        for h in range(2 * GDN_QK_HEADS):
            d2, hh = (dq2, h) if h < GDN_QK_HEADS else (dk2, h - GDN_QK_HEADS)
            _, vjp = jax.vjp(lambda x: _l2norm(x, _qk_scale(h)), _head(x, h))
            out.append(vjp(_head(d2, 2 * hh) + _head(d2, 2 * hh + 1))[0])
        return jnp.concatenate(out, axis=1)

    r = _rows(width, tt)
    return _tmap(fn, (1, t // tt), [(c, *r), (dq_rep, *r), (dk_rep, *r)], [((t, width), F32, *r, None)], "gdn_qk_norm_bwd")[0]


def _gdn_gates(x, alog, dtb):
    return -jnp.exp(alog) * _softplus(x + dtb), _sigmoid(x)


def _gates_fwd(pab, alog, dtb):
    t = pab.shape[0]
    tt = _tile(t, (256, 128))

    def fn(x, alog, dtb):
        g, beta = _gdn_gates(x, alog, dtb)
        lane = lax.broadcasted_iota(jnp.int32, g.shape, 1)
        return jnp.where(lane < GDN_V_HEADS, _hdot_raw(_chunk_sum_matrix(tt, GDN_CHUNK), g), beta).T

    p = (alog, (1, HEAD), lambda j, i: (0, 0)), (dtb, (1, HEAD), lambda j, i: (0, 0))
    return _tmap(fn, (1, t // tt), [(pab, *_rows(HEAD, tt)), *p], [((HEAD, t), F32, (HEAD, tt), lambda j, i: (0, i), None)], "gdn_gates")[0]


def _gates_bwd(pab, alog, dtb, dout_t):
    t = pab.shape[0]
    tt = _tile(t, (256, 128))

    def fn(x, alog, dtb, dout_t):
        dout = dout_t.T
        lane = lax.broadcasted_iota(jnp.int32, dout.shape, 1)
        dgam = jnp.where(lane < GDN_V_HEADS, dout, 0.0)
        dbeta = jnp.where(jnp.logical_and(lane >= GDN_V_HEADS, lane < 2 * GDN_V_HEADS), dout, 0.0)
        dg = _hdot_raw(_chunk_sum_matrix(tt, GDN_CHUNK, transpose=True), dgam)
        _, vjp = jax.vjp(_gdn_gates, x, alog, dtb)
        return vjp((dg, dbeta))

    p = (alog, (1, HEAD), lambda j, i: (0, 0)), (dtb, (1, HEAD), lambda j, i: (0, 0))
    acc = ((1, HEAD), F32, (1, HEAD), lambda j, i: (0, 0), "inner")
    return _tmap(fn, (1, t // tt), [(pab, *_rows(HEAD, tt)), *p, (dout_t, (HEAD, tt), lambda j, i: (0, i))],
                 [((t, HEAD), BF16, *_rows(HEAD, tt), None), acc, acc], "gdn_gates_bwd")


def _split_bf16(x):
    hi = x.astype(BF16)
    return hi, (x - hi.astype(F32)).astype(BF16)


def _dot3(a, b):
    (ah, al), (bh, bl) = a, b
    return _bdot_raw(ah, bh, "nn") + (_bdot_raw(ah, bl, "nn") + _bdot_raw(al, bh, "nn"))


def _each(fn, *lists):
    return tuple(fn(*xs) for xs in zip(*lists))


def _unit_lower_inverses_raw(a):
    n = a[0].shape[0]
    row = lax.broadcasted_iota(jnp.int32, (n, n), 0)
    col = lax.broadcasted_iota(jnp.int32, (n, n), 1)
    eye = jnp.where(row == col, 1.0, 0.0).astype(F32)
    p = _each(lambda a: eye - a, a)
    x = _each(_split_bf16, a)
    m = 2
    while m < n:
        x = _each(_split_bf16, _each(_dot3, x, x))
        p = _each(lambda p, x: p + _bdot_raw(p, x[0], "nn"), p, x)
        m *= 2
    return p


@jax.custom_vjp
def _unit_lower_inverses(a, known):
    return _unit_lower_inverses_raw(a) if known is None else known


def _uli_fwd(a, known):
    inv = _unit_lower_inverses(a, known)
    return inv, (inv, known)


def _uli_bwd(res, ct):
    inv, known = res
    right = _each(lambda ct, inv: _bdot_raw(ct, inv, "nt"), ct, inv)
    da = _each(lambda inv, r: -_bdot_raw(inv, r, "tn"), inv, right)
    return da, (None if known is None else _each(jnp.zeros_like, known))


_unit_lower_inverses.defvjp(_uli_fwd, _uli_bwd)


def _gdn_chunks(q, k, v, beta_rows, gam_rows, s, inv_known=None):
    n = q[0].shape[0]
    heads = range(len(q))
    row = lax.broadcasted_iota(jnp.int32, (n, n), 0)
    col = lax.broadcasted_iota(jnp.int32, (n, n), 1)
    beta_cols, gam_cols = beta_rows.T, gam_rows.T
    beta = tuple(beta_cols[:, g:g + 1] for g in heads)
    gam = tuple(gam_cols[:, g:g + 1] for g in heads)
    gam_row = tuple(gam_rows[g:g + 1, :] for g in heads)
    decay = _each(lambda gam, gam_row: jnp.where(row >= col, jnp.exp(jnp.minimum(gam - gam_row, 0.0)), 0.0), gam, gam_row)
    kb = _each(lambda k, beta: k * beta, k, beta)
    a = _each(lambda kb, k, decay: jnp.where(row > col, _bdot(kb, k, "nt") * decay, 0.0), kb, k, decay)
    inv = _unit_lower_inverses(a, inv_known)
    eg = _each(jnp.exp, gam)
    u = _each(lambda inv, v, beta: _bdot(inv, v * beta, "nn"), inv, v, beta)
    w = _each(lambda inv, kb, eg: _bdot(inv, kb * eg, "nn"), inv, kb, eg)
    qk = _each(lambda q, k, decay: _bdot(q, k, "nt") * decay, q, k, decay)
    v_new = _each(lambda u, w, s: u - _bdot(w, s, "nn"), u, w, s)
    o_state = _each(lambda q, eg, s: _bdot(q * eg, s, "nn"), q, eg, s)
    o = _each(lambda o_state, qk, v_new: o_state + _bdot(qk, v_new, "nn"), o_state, qk, v_new)
    gend = _each(lambda gam: gam[n - 1:n, :], gam)
    s_new = _each(lambda s, k, gam, gend, v_new: s * jnp.exp(gend) + _bdot(k * jnp.exp(gend - gam), v_new, "tn"), s, k, gam, gend, v_new)
    return o, s_new, inv


GDN_GROUP = 16


def _gdn_specs(nc, rev):
    cc = (lambda c: nc - 1 - c) if rev else (lambda c: c)
    grp = GDN_GROUP
    q = pl.BlockSpec((GDN_CHUNK, grp // 2 * HEAD), lambda h, c: (cc(c), h))
    k = pl.BlockSpec((GDN_CHUNK, grp // 2 * HEAD), lambda h, c: (cc(c), 2 * GDN_QK_HEADS // grp + h))
    v = pl.BlockSpec((GDN_CHUNK, grp * HEAD), lambda h, c: (cc(c), 2 * GDN_QK_HEADS // grp + h))
    o = pl.BlockSpec((GDN_CHUNK, grp * HEAD), lambda h, c: (cc(c), h))
    rw = pl.BlockSpec((grp, None, 1, GDN_CHUNK), lambda h, c: (h, cc(c), 0, 0))
    st = pl.BlockSpec((grp, None, HEAD, HEAD), lambda h, c: (h, cc(c), 0, 0))
    inv = pl.BlockSpec((grp, None, GDN_CHUNK, GDN_CHUNK), lambda h, c: (h, cc(c), 0, 0))
    return q, k, v, o, rw, st, inv


def _head_lanes(g, per=1):
    return pl.ds((g // per) * HEAD, HEAD)


def _gdn_rec_fwd(qk, c, beta_row, gam_row):
    t = qk.shape[0]
    nc = t // GDN_CHUNK
    q, k, v, o, rw, st, inv = _gdn_specs(nc, False)

    def body(q_ref, k_ref, v_ref, be_ref, gr_ref, o_ref, ss_ref, inv_ref, s_ref):
        @pl.when(pl.program_id(1) == 0)
        def _():
            s_ref[...] = jnp.zeros_like(s_ref)

        heads = range(GDN_GROUP)
        s = tuple(s_ref[g] for g in heads)
        out, s_new, inv_c = _gdn_chunks(
            tuple(q_ref[:, _head_lanes(g, 2)] for g in heads), tuple(k_ref[:, _head_lanes(g, 2)] for g in heads),
            tuple(v_ref[:, _head_lanes(g)] for g in heads), be_ref[:, 0, :], gr_ref[:, 0, :], s)
        for g in heads:
            ss_ref[g] = s[g]
            o_ref[:, _head_lanes(g)] = out[g]
            inv_ref[g] = inv_c[g]
            s_ref[g] = s_new[g]

    return pl.pallas_call(
        body, grid=(GDN_V_HEADS // GDN_GROUP, nc), in_specs=[q, k, v, rw, rw], out_specs=[o, st, inv],
        out_shape=[SDS((t, 2 * D_MODEL), F32), SDS((GDN_V_HEADS, nc, HEAD, HEAD), F32), SDS((GDN_V_HEADS, nc, GDN_CHUNK, GDN_CHUNK), F32)],
        scratch_shapes=[pltpu.VMEM((GDN_GROUP, HEAD, HEAD), F32)], name="gdn_rec", compiler_params=_params(2),
    )(qk, qk, c, beta_row, gam_row)


def _gdn_rec_bwd(qk, c, beta_row, gam_row, ss, invs, do):
    t = qk.shape[0]
    nc = t // GDN_CHUNK
    q, k, v, o, rw, st, inv = _gdn_specs(nc, True)

    def body(q_ref, k_ref, v_ref, be_ref, gr_ref, ss_ref, inv_ref, do_ref,
             dq_ref, dk_ref, dv_ref, dbe_ref, dgr_ref, ds_ref):
        @pl.when(pl.program_id(1) == 0)
        def _():
            ds_ref[...] = jnp.zeros_like(ds_ref)

        heads = range(GDN_GROUP)
        _, vjp = jax.vjp(
            _gdn_chunks,
            tuple(q_ref[:, _head_lanes(g, 2)] for g in heads), tuple(k_ref[:, _head_lanes(g, 2)] for g in heads),
            tuple(v_ref[:, _head_lanes(g)] for g in heads), be_ref[:, 0, :], gr_ref[:, 0, :],
            tuple(ss_ref[g] for g in heads), tuple(inv_ref[g] for g in heads))
        no_inv_ct = tuple(jnp.zeros((GDN_CHUNK, GDN_CHUNK), F32) for g in heads)
        dq, dk, dv, dbe, dgr, ds, _ = vjp((tuple(do_ref[:, _head_lanes(g)] for g in heads), tuple(ds_ref[g] for g in heads), no_inv_ct))
        for g in heads:
            dq_ref[:, _head_lanes(g)] = dq[g]
            dk_ref[:, _head_lanes(g)] = dk[g]
            dv_ref[:, _head_lanes(g)] = dv[g]
            ds_ref[g] = ds[g]
        dbe_ref[:, 0, :] = dbe
        dgr_ref[:, 0, :] = dgr

    wide = SDS((t, 2 * D_MODEL), F32)
    rowshape = SDS((GDN_V_HEADS, nc, 1, GDN_CHUNK), F32)
    return pl.pallas_call(
        body, grid=(GDN_V_HEADS // GDN_GROUP, nc), in_specs=[q, k, v, rw, rw, st, inv, o], out_specs=[o, o, o, rw, rw],
        out_shape=[wide, wide, wide, rowshape, rowshape],
        scratch_shapes=[pltpu.VMEM((GDN_GROUP, HEAD, HEAD), F32)], name="gdn_rec_bwd", compiler_params=_params(2),
    )(qk, qk, c, beta_row, gam_row, ss, invs, do)


def _gated_norm(o, gate, w):
    return _rms(o, w) * _silu(gate)


def _post_fwd(o, proj, col_off, w, name):
    t, width = o.shape
    tt = _tile(t, (256, 128))

    def fn(o, gate, w):
        return jnp.concatenate([_gated_norm(_head(o, h), _head(gate, h), w) for h in range(width // HEAD)], axis=1)

    return _tmap(fn, (1, t // tt),
                 [(o, *_rows(width, tt)), (proj, *_rows(width, tt, col_off * HEAD // width)), (w, (1, HEAD), lambda j, i: (0, 0))],
                 [((t, width), BF16, *_rows(width, tt), None)], name)[0]


def _post_bwd(o, proj, col_off, w, dout, name):
    t, width = o.shape
    tt = _tile(t, (256, 128))

    def fn(o, gate, w, dout):
        do, dgate, dw = [], [], jnp.zeros((1, HEAD), F32)
        for h in range(width // HEAD):
            _, vjp = jax.vjp(_gated_norm, _head(o, h), _head(gate, h), w)
            a, b, c = vjp(_head(dout, h))
            do.append(a)
            dgate.append(b)
            dw = dw + c
        return jnp.concatenate(do, axis=1), jnp.concatenate(dgate, axis=1), dw

    r = _rows(width, tt)
    return _tmap(fn, (1, t // tt),
                 [(o, *r), (proj, *_rows(width, tt, col_off * HEAD // width)), (w, (1, HEAD), lambda j, i: (0, 0)), (dout, *r)],
                 [((t, width), F32, *r, None), ((t, width), BF16, *r, None), ((1, HEAD), F32, (1, HEAD), lambda j, i: (0, 0), "inner")], name)


def _merge(gate_h, gate_g, yh, yg):
    return _sigmoid(gate_h) * yh + _sigmoid(gate_g) * yg


def _merge_fwd(proj, yh, yg):
    t = yh.shape[0]
    tt, ft = _tile(t, (256, 128)), 512
    r = _rows(ft, tt)
    return _tmap(_merge, (D_MODEL // ft, t // tt),
                 [(proj, *_rows(ft, tt, COL_GATE_H * HEAD // ft)), (proj, *_rows(ft, tt, COL_GATE_G * HEAD // ft)), (yh, *r), (yg, *r)],
                 [((t, D_MODEL), BF16, *r, None)], "merge")[0]


def _merge_bwd(proj, yh, yg, dy):
    t = yh.shape[0]
    tt, ft = _tile(t, (256, 128)), 512
    r = _rows(ft, tt)

    def fn(gate_h, gate_g, yh, yg, dy):
        _, vjp = jax.vjp(_merge, gate_h, gate_g, yh, yg)
        return vjp(dy)

    o = ((t, D_MODEL), BF16, *r, None)
    return _tmap(fn, (D_MODEL // ft, t // tt),
                 [(proj, *_rows(ft, tt, COL_GATE_H * HEAD // ft)), (proj, *_rows(ft, tt, COL_GATE_G * HEAD // ft)), (yh, *r), (yg, *r), (dy, *r)],
                 [o, o, o, o], "merge_bwd")


def _loss_head(h, target, g):
    t, d = h.shape
    tt = _tile(t, (256, 128))

    def fn(h, target, g):
        def f(h, g):
            err = _rms(h, g) - target
            return 0.5 * jnp.sum(jnp.mean(err * err, axis=-1))

        loss, (dh, dg) = jax.value_and_grad(f, (0, 1))(h, g)
        return dh, dg, jnp.full((1, HEAD), loss, F32)

    return _tmap(fn, (1, t // tt), [(h, *_rows(d, tt)), (target, *_rows(d, tt)), (g, (1, d), lambda j, i: (0, 0))],
                 [((t, d), F32, *_rows(d, tt), None), ((1, d), F32, (1, d), lambda j, i: (0, 0), "inner"),
                  ((1, HEAD), F32, (1, HEAD), lambda j, i: (0, 0), "inner")], "loss_head")


def _mixer_fwd(h, p, links):
    t = h.shape[0]
    nc = t // GDN_CHUNK
    u = _rms_fwd(h, p["mix_norm"], "mix_norm")
    w = {n: links.weight(n, h) for n in ("w_in_t", "w_in_b_t", "w_in_ab_t", "conv_w")}
    proj = _mm(u, w["w_in_t"], "nt", F32, "mix_in", after=links.started, b_rows=SCALAR_ROWS)
    proj_b = _mm(u, w["w_in_b_t"], "nt", F32, "mix_in_b")
    pab = _mm(u, w["w_in_ab_t"], "nt", F32, "mix_in_ab")
    qh, kh, bh = _hgrn_prep_fwd(proj, p["lbl"])
    oh, hs = _hgrn_rec_fwd(qh, kh, proj, bh)
    c, conv_y = _conv_fwd(proj, w["conv_w"])
    qk = _qk_norm_fwd(c)
    gates_t = _gates_fwd(pab, p["alog"], p["dtb"])
    gam_row = gates_t[:GDN_V_HEADS].reshape(GDN_V_HEADS, nc, 1, GDN_CHUNK)
    beta_row = gates_t[GDN_V_HEADS:2 * GDN_V_HEADS].reshape(GDN_V_HEADS, nc, 1, GDN_CHUNK)
    og, ss, invs = _gdn_rec_fwd(qk, c, beta_row, gam_row)
    ohn = _post_fwd(oh, proj, COL_HG, p["hgrn_out_norm"], "hgrn_out")
    ogn = _post_fwd(og, proj_b, COL_GZ, p["gdn_out_norm"], "gdn_out")
    w.update({n: links.weight(n, ogn) for n in ("w_branch_hgrn", "w_branch_gdn", "w_out")})
    yh = _mm(ohn, w["w_branch_hgrn"], "nn", BF16, "branch_hgrn")
    yg = _mm(ogn, w["w_branch_gdn"], "nn", BF16, "branch_gdn")
    y = _merge_fwd(proj_b, yh, yg)
    out = _mm(y, w["w_out"], "nn", F32, "mix_out", res=h)
    saved = (w, u, proj, proj_b, pab, qh, kh, bh, oh, hs, c, conv_y, qk, beta_row, gam_row, og, ss, invs, ohn, ogn, yh, yg, y)
    return out, saved


def _mixer_bwd(h, p, links, saved, dout):
    (w, u, proj, proj_b, pab, qh, kh, bh, oh, hs, c, conv_y, qk, beta_row, gam_row, og, ss, invs, ohn, ogn, yh, yg, y) = saved
    t = h.shape[0]
    grads = {}
    dw_out = _mm(y, dout, "tn", BF16, "mix_out_dw")
    dy = _mm(dout, w["w_out"], "nt", F32, "mix_out_dx")
    dgate_h, dgate_g, dyh, dyg = _merge_bwd(proj_b, yh, yg, dy)
    dw_bh = _mm(ohn, dyh, "tn", BF16, "branch_hgrn_dw")
    dw_bg = _mm(ogn, dyg, "tn", BF16, "branch_gdn_dw")
    sent = links.send({"w_out": dw_out, "w_branch_hgrn": dw_bh, "w_branch_gdn": dw_bg})
    dohn = _mm(dyh, w["w_branch_hgrn"], "nt", F32, "branch_hgrn_dx", after=sent)
    dogn = _mm(dyg, w["w_branch_gdn"], "nt", F32, "branch_gdn_dx")
    doh, dhg, grads["hgrn_out_norm"] = _post_bwd(oh, proj, COL_HG, p["hgrn_out_norm"], dohn, "hgrn_out_bwd")
    dog, dgz, grads["gdn_out_norm"] = _post_bwd(og, proj_b, COL_GZ, p["gdn_out_norm"], dogn, "gdn_out_bwd")
    dqh, dkh, dhi, dbh = _hgrn_rec_bwd(qh, kh, proj, bh, hs, doh)
    dhq, dhf, grads["lbl"] = _hgrn_prep_bwd(proj, p["lbl"], dqh, dkh, dbh)
    dqv, dkv, dcv, dbeta_row, dgam_row = _gdn_rec_bwd(qk, c, beta_row, gam_row, ss, invs, dog)
    dcqk = _qk_norm_bwd(c, dqv, dkv)
    dxin, grads["conv_w"] = _conv_bwd(proj, w["conv_w"], conv_y, dcqk, dcv)
    dgates_t = jnp.concatenate([dgam_row.reshape(GDN_V_HEADS, t), dbeta_row.reshape(GDN_V_HEADS, t),
                                jnp.zeros((HEAD - 2 * GDN_V_HEADS, t), F32)], axis=0)
    dpab, grads["alog"], grads["dtb"] = _gates_bwd(pab, p["alog"], p["dtb"], dgates_t)
    front, back = [dhq, dhf, dhi, dhg, dxin], [dgz, dgate_h, dgate_g]
    dw_front = [_mm(d, u, "tn", BF16, "mix_in_dw_%d" % i) for i, d in enumerate(front)]
    dw_back = [_mm(d, u, "tn", BF16, "mix_in_b_dw_%d" % i) for i, d in enumerate(back)]
    dw_ab_t = _mm(dpab, u, "tn", BF16, "mix_in_ab_dw")
    sent = links.send({"w_in": jnp.concatenate(dw_front + [dw_ab_t[:N_SCALAR]] + dw_back, axis=0)})
    du = _mm_pieces(front, w["w_in_t"], "mix_in_dx", after=sent)
    du = _mm_pieces(back, w["w_in_b_t"], "mix_in_b_dx", res=du)
    du = _mm(dpab, w["w_in_ab_t"], "nn", F32, "mix_in_ab_dx", res=du)
    dh, grads["mix_norm"] = _rms_bwd(h, p["mix_norm"], du, dout, "mix_norm_bwd")
    return dh, grads


def _local_step(x, target, p, links):
    def ffn_weights(tag, behind):
        def get(n):
            w_in_t, w_out = links.weight(tag + "_w_in", n), links.weight(tag + "_w_out", n)
            return w_in_t, w_out, links.started if behind else None
        return get

    h1, s1 = _ffn_fwd(x, p["ffn1_norm"] + links.started[0, 0], ffn_weights("ffn1", True), "ffn1")
    h2, sm = _mixer_fwd(h1, p, links)
    h3, s2 = _ffn_fwd(h2, p["ffn2_norm"], ffn_weights("ffn2", False), "ffn2")
    dh3, dfinal, loss = _loss_head(h3, target, p["final_norm"])
    g = {"final_norm": dfinal}
    dh2, g["ffn2_norm"] = _ffn_bwd(h2, p["ffn2_norm"], s2, dh3, "ffn2", links)
    dh1, gm = _mixer_bwd(h1, p, links, sm, dh2)
    g.update(gm)
    dx, g["ffn1_norm"] = _ffn_bwd(x, p["ffn1_norm"], s1, dh1, "ffn1", links)
    return loss, dx, g


HBM_SPEC = pl.BlockSpec(memory_space=pltpu.HBM)
SEM_SPEC = pl.BlockSpec(memory_space=pltpu.SEMAPHORE)
DATAFLOW = pltpu.SideEffectType.DATAFLOW_SIDE_EFFECTING


def _position():
    x, y, c = lax.axis_index("x"), lax.axis_index("y"), lax.axis_index("c")
    return x, y, c, 4 * x + 2 * y + c


def _relations(x, y, c):
    for rel in range(1, N_DEV):
        px = 1 - x if rel & 4 else x
        py = 1 - y if rel & 2 else y
        pc = 1 - c if rel & 1 else c
        yield rel, (px, py, pc), 4 * px + 2 * py + pc


def _sem_index(item, rel):
    return item * (N_DEV - 1) + rel - 1


def _landing(a, mode):
    return lax.empty((N_DEV,) + a.shape if mode == "gather" else a.shape, a.dtype)


ALL_PEERS = tuple(range(1, N_DEV))
ONE_PER_CHIP = (1, 2, 4, 6)


def _copies_start(groups, name, rels=ALL_PEERS):
    flat = [item for grp in groups for item in grp]
    n, ng = len(flat), len(groups)
    lands = [_landing(a, mode) for a, mode in flat]

    def body(*refs):
        src_refs, land_refs, sems, token = refs[:n], refs[n:2 * n], refs[2 * n:2 * n + 2 * ng], refs[-1]
        x, y, c, me = _position()
        for rel, where, peer in _relations(x, y, c):
            if rel not in rels:
                continue
            k = 0
            for gi, grp in enumerate(groups):
                for li, (_, mode) in enumerate(grp):
                    src = src_refs[k] if mode == "gather" else src_refs[k].at[peer]
                    pltpu.make_async_remote_copy(src_ref=src, dst_ref=land_refs[k].at[me], send_sem=sems[2 * gi].at[_sem_index(li, rel)],
                                                 recv_sem=sems[2 * gi + 1].at[_sem_index(li, rel)], device_id=where, device_id_type=MESH_IDS).start()
                    k += 1
        token[...] = jnp.zeros_like(token)

    sem_shapes = [pltpu.SemaphoreType.DMA((len(grp) * (N_DEV - 1),)) for grp in groups for _ in range(2)]
    thru = [pltpu.HBM(a.shape, a.dtype) for a, _ in flat] + [pltpu.HBM(l.shape, l.dtype) for l in lands]
    outs = pl.pallas_call(
        body, name=name, out_shape=(*sem_shapes, *thru, SDS((8, HEAD), F32)),
        in_specs=[HBM_SPEC] * (2 * n), out_specs=(*[SEM_SPEC] * (2 * ng), *[HBM_SPEC] * (2 * n), pl.BlockSpec(memory_space=pltpu.VMEM)),
        input_output_aliases={i: 2 * ng + i for i in range(2 * n)}, compiler_params=pltpu.CompilerParams(has_side_effects=DATAFLOW),
    )(*[pltpu.with_memory_space_constraint(a, pltpu.HBM) for a, _ in flat], *[pltpu.with_memory_space_constraint(l, pltpu.HBM) for l in lands])
    sems, srcs, landed, token = outs[:2 * ng], outs[2 * ng:2 * ng + n], outs[2 * ng + n:2 * ng + 2 * n], outs[-1]
    result, k = [], 0
    for gi, grp in enumerate(groups):
        result.append((sems[2 * gi], sems[2 * gi + 1], srcs[k:k + len(grp)], landed[k:k + len(grp)]))
        k += len(grp)
    return result, token


def _copies_wait(started, modes, after, name, rels=ALL_PEERS):
    send_sems, recv_sems, srcs, lands = started
    n = len(srcs)

    def body(*refs):
        src_refs, land_refs, ssem, rsem, token = refs[:n], refs[n:2 * n], refs[2 * n], refs[2 * n + 1], refs[-1]
        x, y, c, _ = _position()
        for rel in rels:
            for i, mode in enumerate(modes):
                src = src_refs[i] if mode == "gather" else src_refs[i].at[0]
                cp = pltpu.make_async_remote_copy(src_ref=src, dst_ref=land_refs[i].at[0], send_sem=ssem.at[_sem_index(i, rel)],
                                                  recv_sem=rsem.at[_sem_index(i, rel)], device_id=(x, y, c), device_id_type=MESH_IDS)
                cp.wait_send()
                cp.wait_recv()
        token[...] = jnp.zeros_like(token)

    outs = pl.pallas_call(
        body, name=name, out_shape=[pltpu.HBM(a.shape, a.dtype) for a in (*srcs, *lands)] + [SDS((8, HEAD), F32)],
        in_specs=[HBM_SPEC] * (2 * n) + [SEM_SPEC, SEM_SPEC, pl.BlockSpec(memory_space=pl.ANY)],
        out_specs=[HBM_SPEC] * (2 * n) + [pl.BlockSpec(memory_space=pltpu.VMEM)],
        input_output_aliases={i: i for i in range(2 * n)}, compiler_params=pltpu.CompilerParams(has_side_effects=DATAFLOW),
    )(*srcs, *lands, send_sems, recv_sems, after)
    return outs[:n], outs[n:2 * n], outs[-1]


OTHER_CHIPS = ((1, 0), (0, 1), (1, 1))


def _pass_on_start(lands, name):
    n = len(lands)

    def body(*refs):
        land_refs, ssem, rsem, token = refs[:n], refs[n], refs[n + 1], refs[-1]
        x, y, c, _ = _position()
        for j, (fx, fy) in enumerate(OTHER_CHIPS):
            slot = 4 * (1 - x if fx else x) + 2 * (1 - y if fy else y) + c
            for i in range(n):
                pltpu.make_async_remote_copy(src_ref=land_refs[i].at[slot], dst_ref=land_refs[i].at[slot], send_sem=ssem.at[i * len(OTHER_CHIPS) + j],
                                             recv_sem=rsem.at[i * len(OTHER_CHIPS) + j], device_id=(x, y, 1 - c), device_id_type=MESH_IDS).start()
        token[...] = jnp.zeros_like(token)

    sems = pltpu.SemaphoreType.DMA((n * len(OTHER_CHIPS),))
    outs = pl.pallas_call(
        body, name=name, out_shape=(sems, sems, *[pltpu.HBM(l.shape, l.dtype) for l in lands], SDS(TOKEN, F32)),
        in_specs=[HBM_SPEC] * n, out_specs=(SEM_SPEC, SEM_SPEC, *[HBM_SPEC] * n, pl.BlockSpec(memory_space=pltpu.VMEM)),
        input_output_aliases={i: 2 + i for i in range(n)}, compiler_params=pltpu.CompilerParams(has_side_effects=DATAFLOW),
    )(*lands)
    return (outs[0], outs[1], outs[2:2 + n]), outs[-1]


def _pass_on_wait(started, after, name):
    send_sems, recv_sems, lands = started
    n = len(lands)

    def body(*refs):
        land_refs, ssem, rsem = refs[:n], refs[n], refs[n + 1]
        x, y, c, _ = _position()
        for j in range(len(OTHER_CHIPS)):
            for i in range(n):
                cp = pltpu.make_async_remote_copy(src_ref=land_refs[i].at[0], dst_ref=land_refs[i].at[0], send_sem=ssem.at[i * len(OTHER_CHIPS) + j],
                                                  recv_sem=rsem.at[i * len(OTHER_CHIPS) + j], device_id=(x, y, c), device_id_type=MESH_IDS)
                cp.wait_send()
                cp.wait_recv()

    return pl.pallas_call(
        body, name=name, out_shape=[pltpu.HBM(l.shape, l.dtype) for l in lands],
        in_specs=[HBM_SPEC] * n + [SEM_SPEC, SEM_SPEC, pl.BlockSpec(memory_space=pl.ANY)], out_specs=[HBM_SPEC] * n,
        input_output_aliases={i: i for i in range(n)}, compiler_params=pltpu.CompilerParams(has_side_effects=DATAFLOW),
    )(*lands, send_sems, recv_sems, after)


WEIGHT_GROUPS = (("ffn1_w_in", "ffn1_w_out", "gdn_conv_w"), ("w_in",), ("w_branch_hgrn", "w_branch_gdn", "w_out", "ffn2_w_in", "ffn2_w_out"))
GROUP_RELS = (ONE_PER_CHIP, ONE_PER_CHIP, ALL_PEERS)


class _Links:
    def __init__(self, shards, me):
        self.me = me
        self.shards = shards
        self.weights = {}
        self.sends = []
        self.gathers = {}
        self.started = None
        self._start_gather(0, None)

    def _start_gather(self, gi, zeros):
        if gi < len(WEIGHT_GROUPS):
            items = [(self.shards[n] if zeros is None else self.shards[n] + zeros[0, 0].astype(self.shards[n].dtype), "gather")
                     for n in WEIGHT_GROUPS[gi]]
            started, self.started = _copies_start([items], "gather_start_%d" % gi, GROUP_RELS[gi])
            self.gathers[gi] = started[0]

    def weight(self, name, after):
        if name not in self.weights:
            source = {"w_in_t": "w_in", "w_in_b_t": "w_in", "w_in_ab_t": "w_in", "conv_w": "gdn_conv_w"}.get(name, name)
            gi = [i for i, grp in enumerate(WEIGHT_GROUPS) if source in grp][0]
            assert gi in self.gathers, "weight groups are asked for in order"
            srcs, lands, zero = _copies_wait(self.gathers[gi], ["gather"] * len(WEIGHT_GROUPS[gi]), after, "gather_wait_%d" % gi, GROUP_RELS[gi])
            if GROUP_RELS[gi] == ONE_PER_CHIP:
                passing, zero = _pass_on_start(lands, "gather_pass_%d" % gi)
                self._start_gather(gi + 1, zero)
                lands = _pass_on_wait(passing, self.started, "gather_passed_%d" % gi)
            else:
                self._start_gather(gi + 1, zero)
            for n, src, land in zip(WEIGHT_GROUPS[gi], srcs, lands):
                full = lax.dynamic_update_index_in_dim(land, src, self.me, 0)
                if n == "gdn_conv_w":
                    self.weights["conv_w"] = full.reshape(N_DEV, CONV_K, 4 * D_MODEL // N_DEV).transpose(1, 0, 2).reshape(CONV_K, 4 * D_MODEL)
                elif n == "w_in":
                    self.weights.update(_w_in_pieces(full.reshape(-1, D_MODEL)))
                else:
                    self.weights[n] = full.reshape(-1, D_MODEL)
        return self.weights[name]

    def send(self, grads):
        names = list(grads)
        blocks = [grads[n].reshape(N_DEV, -1, D_MODEL) for n in names]
        started, token = _copies_start([[(b, "scatter") for b in blocks]], "send_" + names[0])
        self.sends.append((names, started[0]))
        return token

    def landed(self, after):
        out = {}
        for names, started in self.sends:
            srcs, lands, _ = _copies_wait(started, ["scatter"] * len(names), after, "landed_" + names[0])
            for n, src, land in zip(names, srcs, lands):
                out[n] = lax.dynamic_update_index_in_dim(land, lax.dynamic_index_in_dim(src, self.me, 0, keepdims=False), self.me, 0)
        return out


def _adam(parts, w, m, v, name):
    n_parts, r, c = parts.shape
    tc = c if c <= 512 else (256 if r > 1024 else 512)

    def body(p_ref, w_ref, m_ref, v_ref, g_ref, d_ref, mo_ref, vo_ref):
        g = p_ref[0].astype(F32)
        for i in range(1, n_parts):
            g = g + p_ref[i].astype(F32)
        m_new = ADAM_B1 * m_ref[...] + (1.0 - ADAM_B1) * g
        v_new = ADAM_B2 * v_ref[...] + (1.0 - ADAM_B2) * (g * g)
        m_hat = m_new / (1.0 - ADAM_B1 ** ADAM_STEP)
        v_hat = v_new / (1.0 - ADAM_B2 ** ADAM_STEP)
        g_ref[...] = g
        d_ref[...] = -ADAM_LR * (m_hat / (jnp.sqrt(v_hat) + ADAM_EPS) + ADAM_WD * w_ref[...])
        mo_ref[...] = m_new
        vo_ref[...] = v_new

    spec = pl.BlockSpec((r, tc), lambda j: (0, j))
    return pl.pallas_call(
        body, grid=(c // tc,), in_specs=[pl.BlockSpec((n_parts, r, tc), lambda j: (0, 0, j)), spec, spec, spec],
        out_specs=[spec] * 4, out_shape=[SDS((r, c), F32)] * 4, name=name, compiler_params=_params(1),
    )(parts, w, m, v)


BIG = ("ffn1_w_in", "ffn1_w_out", "w_in", "w_branch_hgrn", "w_branch_gdn", "w_out", "ffn2_w_in", "ffn2_w_out")


TRANSPOSED = ("ffn1_w_in", "w_in", "ffn2_w_in")


def _shard_rows(name, shard):
    return shard.T if name in TRANSPOSED else shard


SCALAR_ROWS = 8192
N_SCALAR = 2 * GDN_V_HEADS


def _w_in_pieces(w_in_t):
    return {"w_in_t": w_in_t, "w_in_b_t": w_in_t[SCALAR_ROWS + N_SCALAR:],
            "w_in_ab_t": jnp.pad(w_in_t[SCALAR_ROWS:SCALAR_ROWS + N_SCALAR], ((0, HEAD - N_SCALAR), (0, 0)))}


def _pad_lanes(a, width=HEAD):
    return jnp.pad(a, ((0, 0), (0, width - a.shape[1])))


SMALL_ROWS = 24


def _pack_small(g, loss):
    row6 = jnp.concatenate([g["hgrn_out_norm"], g["gdn_out_norm"], g["alog"], g["dtb"], loss,
                            jnp.zeros((1, D_MODEL - 5 * HEAD), F32)], axis=1)
    return jnp.concatenate([g["ffn1_norm"], g["mix_norm"], g["lbl"], g["ffn2_norm"], g["final_norm"], row6,
                            jnp.zeros((1, D_MODEL), F32), g["conv_w"].reshape(4 * CONV_K, D_MODEL)], axis=0)


def _pack_small_state(a):
    row6 = jnp.concatenate([a["hgrn_out_norm"], a["gdn_out_norm"], _pad_lanes(a["gdn_a_log"]), _pad_lanes(a["gdn_dt_bias"]),
                            jnp.zeros((1, D_MODEL - 4 * HEAD), F32)], axis=1)
    return jnp.concatenate([a["ffn1_norm"], a["mix_norm"], a["hgrn_lb_logits"], a["ffn2_norm"], a["final_norm"].reshape(1, D_MODEL),
                            row6, jnp.zeros((1, D_MODEL), F32)], axis=0)


def _unpack_small(a):
    return {"ffn1_norm": a[0:1], "mix_norm": a[1:2], "hgrn_lb_logits": a[2:4], "ffn2_norm": a[4:5], "final_norm": a[5],
            "hgrn_out_norm": a[6:7, :HEAD], "gdn_out_norm": a[6:7, HEAD:2 * HEAD],
            "gdn_a_log": a[6:7, 2 * HEAD:2 * HEAD + GDN_V_HEADS], "gdn_dt_bias": a[6:7, 3 * HEAD:3 * HEAD + GDN_V_HEADS]}


NAMES = ("ffn1_norm", "ffn1_w_in", "ffn1_w_out", "mix_norm", "w_in", "hgrn_lb_logits", "hgrn_out_norm", "gdn_conv_w", "gdn_a_log",
         "gdn_dt_bias", "gdn_out_norm", "w_branch_hgrn", "w_branch_gdn", "w_out", "ffn2_norm", "ffn2_w_in", "ffn2_w_out", "final_norm")


def kernel(x, ffn1_norm, ffn1_w_in, ffn1_w_out, mix_norm, w_in, hgrn_lb_logits, hgrn_out_norm, gdn_conv_w, gdn_a_log, gdn_dt_bias, gdn_out_norm, w_branch_hgrn, w_branch_gdn, w_out, ffn2_norm, ffn2_w_in, ffn2_w_out, final_norm, loss_target, m_ffn1_norm, m_ffn1_w_in, m_ffn1_w_out, m_mix_norm, m_w_in, m_hgrn_lb_logits, m_hgrn_out_norm, m_gdn_conv_w, m_gdn_a_log, m_gdn_dt_bias, m_gdn_out_norm, m_w_branch_hgrn, m_w_branch_gdn, m_w_out, m_ffn2_norm, m_ffn2_w_in, m_ffn2_w_out, m_final_norm, v_ffn1_norm, v_ffn1_w_in, v_ffn1_w_out, v_mix_norm, v_w_in, v_hgrn_lb_logits, v_hgrn_out_norm, v_gdn_conv_w, v_gdn_a_log, v_gdn_dt_bias, v_gdn_out_norm, v_w_branch_hgrn, v_w_branch_gdn, v_w_out, v_ffn2_norm, v_ffn2_w_in, v_ffn2_w_out, v_final_norm):
    wts = dict(zip(NAMES, (ffn1_norm, ffn1_w_in, ffn1_w_out, mix_norm, w_in, hgrn_lb_logits, hgrn_out_norm, gdn_conv_w, gdn_a_log,
                           gdn_dt_bias, gdn_out_norm, w_branch_hgrn, w_branch_gdn, w_out, ffn2_norm, ffn2_w_in, ffn2_w_out, final_norm)))
    mom = dict(zip(NAMES, (m_ffn1_norm, m_ffn1_w_in, m_ffn1_w_out, m_mix_norm, m_w_in, m_hgrn_lb_logits, m_hgrn_out_norm, m_gdn_conv_w,
                           m_gdn_a_log, m_gdn_dt_bias, m_gdn_out_norm, m_w_branch_hgrn, m_w_branch_gdn, m_w_out, m_ffn2_norm, m_ffn2_w_in,
                           m_ffn2_w_out, m_final_norm)))
    var = dict(zip(NAMES, (v_ffn1_norm, v_ffn1_w_in, v_ffn1_w_out, v_mix_norm, v_w_in, v_hgrn_lb_logits, v_hgrn_out_norm, v_gdn_conv_w,
                           v_gdn_a_log, v_gdn_dt_bias, v_gdn_out_norm, v_w_branch_hgrn, v_w_branch_gdn, v_w_out, v_ffn2_norm, v_ffn2_w_in,
                           v_ffn2_w_out, v_final_norm)))
    me = 4 * lax.axis_index("x") + 2 * lax.axis_index("y") + lax.axis_index("c")

    conv_shard = wts["gdn_conv_w"][0]
    shards = {n: _shard_rows(n, wts[n][0]).astype(BF16) for n in BIG}
    shards["gdn_conv_w"] = conv_shard.reshape(2, D_MODEL)
    links = _Links(shards, me)
    p = {"ffn1_norm": wts["ffn1_norm"], "mix_norm": wts["mix_norm"], "ffn2_norm": wts["ffn2_norm"], "final_norm": wts["final_norm"].reshape(1, D_MODEL),
         "lbl": wts["hgrn_lb_logits"], "hgrn_out_norm": wts["hgrn_out_norm"], "gdn_out_norm": wts["gdn_out_norm"],
         "alog": _pad_lanes(wts["gdn_a_log"]), "dtb": _pad_lanes(wts["gdn_dt_bias"])}

    loss, dx, g = _local_step(x[0], loss_target[0], p, links)

    small_started, small_token = _copies_start([[(_pack_small(g, loss), "gather")]], "small_start")
    landed = links.landed(small_token)

    big = [{} for _ in range(4)]
    for n in BIG:
        res = _adam(landed[n], _shard_rows(n, wts[n][0]), _shard_rows(n, mom[n][0]), _shard_rows(n, var[n][0]), "adam_" + n)
        for kind in range(4):
            big[kind][n] = _shard_rows(n, res[kind])
    small_srcs, small_lands, _ = _copies_wait(small_started[0], ["gather"], res[0], "small_wait")
    small_parts = lax.dynamic_update_index_in_dim(small_lands[0], small_srcs[0], me, 0)
    n_vec = SMALL_ROWS - 4 * CONV_K
    small_raw = _adam(small_parts[:, :n_vec], _pack_small_state(wts), _pack_small_state(mom), _pack_small_state(var), "adam_small")
    small = [_unpack_small(o) for o in small_raw]
    loss_total = small_raw[0][6, 4 * HEAD]
    conv_parts = small_parts[:, n_vec:].reshape(N_DEV, CONV_K, 4 * D_MODEL)
    width = 4 * D_MODEL // N_DEV
    conv_mine = lax.dynamic_slice_in_dim(conv_parts, me * width, width, axis=2)
    conv = _adam(conv_mine, conv_shard, mom["gdn_conv_w"][0], var["gdn_conv_w"][0], "adam_conv")

    outs = []
    for kind in range(4):
        for n in NAMES:
            if n in BIG:
                outs.append(big[kind][n][None])
            elif n == "gdn_conv_w":
                outs.append(conv[kind][None])
            else:
                outs.append(small[kind][n])
    return (loss_total, dx[None], *outs)
```

```python
import functools

import jax
import jax.numpy as jnp
from jax import lax
from jax.experimental import pallas as pl
from jax.experimental.pallas import tpu as pltpu

F32 = jnp.float32
BF16 = jnp.bfloat16
MESH_IDS = pl.DeviceIdType.MESH

D_MODEL = 1024
D_FF = 2816
N_DEV = 8
EPS = 1e-6
HEAD = 128
HG_HEADS = 8
GDN_QK_HEADS = 8
GDN_V_HEADS = 16
GDN_CHUNK = 64
HG_CHUNK = 16
CONV_K = 4
LANES = 128
COL_HQ, COL_HF, COL_HI, COL_HG, COL_GQ, COL_GK, COL_GV = 0, 8, 16, 24, 32, 40, 48
COL_GZ, COL_GATE_H, COL_GATE_G = 0, 16, 24
VMEM_LIMIT = 56 * 1024 * 1024

ADAM_LR, ADAM_B1, ADAM_B2, ADAM_EPS, ADAM_WD, ADAM_STEP = 0.001, 0.9, 0.999, 1e-08, 0.01, 10

SDS = jax.ShapeDtypeStruct


def _params(n_axes):
    return pltpu.CompilerParams(dimension_semantics=("arbitrary",) * n_axes, vmem_limit_bytes=VMEM_LIMIT)


def _tile(n, candidates=(512, 384, 256, 128, 64, 32, 16, 8)):
    for c in candidates:
        if n % c == 0:
            return c
    return n


_DIMS = {"nn": ((1,), (0,)), "nt": ((1,), (1,)), "tn": ((0,), (0,))}


def _bdot_raw(a, b, dims):
    return lax.dot_general(a.astype(BF16), b.astype(BF16), (_DIMS[dims], ((), ())), preferred_element_type=F32)


@functools.partial(jax.custom_vjp, nondiff_argnums=(2,))
def _bdot(a, b, dims):
    return _bdot_raw(a, b, dims)


def _bdot_fwd(a, b, dims):
    return _bdot_raw(a, b, dims), (a, b)


def _bdot_bwd(dims, res, ct):
    a, b = res
    if dims == "nn":
        return _bdot_raw(ct, b, "nt"), _bdot_raw(a, ct, "tn")
    if dims == "nt":
        return _bdot_raw(ct, b, "nn"), _bdot_raw(ct, a, "tn")
    return _bdot_raw(b, ct, "nt"), _bdot_raw(a, ct, "nn")


_bdot.defvjp(_bdot_fwd, _bdot_bwd)


def _hdot_raw(ones, x):
    hi = x.astype(BF16)
    rest = x - hi.astype(F32)
    mid = rest.astype(BF16)
    low = (rest - mid.astype(F32)).astype(BF16)
    return _bdot_raw(ones, hi, "nn") + (_bdot_raw(ones, mid, "nn") + _bdot_raw(ones, low, "nn"))


MM_VMEM_BUDGET = 38 * 1024 * 1024
TOKEN = (8, HEAD)


def _mm_tiles(m, n, k, a_bytes, b_bytes, o_bytes, r_bytes, m_align=8):
    def need(tm, tn, tk):
        return 2 * (tm * tk * a_bytes + tk * tn * b_bytes + tm * tn * (o_bytes + r_bytes)) + (tm * tn * 4 if tk < k else 0)

    def shrink(tm, tn, tk, floor_m, floor_n):
        while need(tm, tn, tk) > MM_VMEM_BUDGET:
            if tn > floor_n and tn % 256 == 0 and tn >= tm:
                tn //= 2
            elif tm > floor_m and tm % (2 * m_align) == 0:
                tm //= 2
            elif tn > floor_n and tn % 256 == 0:
                tn //= 2
            else:
                return None
        return tm, tn, tk

    tm = _tile(m, (1408, 1024, 704, 512, 256, 128, 64, 32, 16, 8))
    tn = _tile(n, (1408, 1024, 512, 256, 128))
    whole = shrink(tm, tn, k, min(tm, 1024), min(tn, 512))
    if whole is not None:
        return whole
    tk = _tile(k, (2048, 1408, 1024, 512, 256, 128, 64, 32, 16, 8))
    while True:
        fit = shrink(tm, tn, tk, min(tm, 256), min(tn, 512))
        if fit is not None or tk <= 512 or tk % 256:
            return fit if fit is not None else (tm, tn, tk)
        tk //= 2


def _mm(a, b, dims, out_dtype, name, res=None, alpha=1.0, after=None, b_rows=None):
    b_shape = b.shape if b_rows is None else (b_rows, b.shape[1])
    if dims == "nn":
        (m, k), (k2, n) = a.shape, b_shape
    elif dims == "nt":
        (m, k), (n, k2) = a.shape, b_shape
    else:
        (k, m), (k2, n) = a.shape, b_shape
    assert k == k2, (a.shape, b.shape, dims)
    has_res = res is not None
    tm, tn, tk = _mm_tiles(m, n, k, a.dtype.itemsize, b.dtype.itemsize, jnp.dtype(out_dtype).itemsize, res.dtype.itemsize if has_res else 0,
                           m_align=LANES if dims == "tn" else 8)
    nk = k // tk
    a_spec = pl.BlockSpec((tk, tm), lambda i, j, kk: (kk, i)) if dims == "tn" else pl.BlockSpec((tm, tk), lambda i, j, kk: (i, kk))
    b_spec = pl.BlockSpec((tn, tk), lambda i, j, kk: (j, kk)) if dims == "nt" else pl.BlockSpec((tk, tn), lambda i, j, kk: (kk, j))
    o_spec = pl.BlockSpec((tm, tn), lambda i, j, kk: (i, j))

    def finish(acc, r_ref, o_ref):
        out = acc * alpha if alpha != 1.0 else acc
        if has_res:
            out = r_ref[...].astype(F32) + out
        o_ref[...] = out.astype(o_ref.dtype)

    n_in = 2 + has_res + (after is not None)

    def body(*refs):
        a_ref, b_ref = refs[:2]
        r_ref = refs[2] if has_res else None
        o_ref = refs[n_in]
        p = _bdot_raw(a_ref[...], b_ref[...], dims)
        if nk == 1:
            finish(p, r_ref, o_ref)
            return
        acc_ref = refs[-1]
        kk = pl.program_id(2)

        @pl.when(kk == 0)
        def _():
            acc_ref[...] = p

        @pl.when(kk > 0)
        def _():
            acc_ref[...] += p

        @pl.when(kk == nk - 1)
        def _():
            finish(acc_ref[...], r_ref, o_ref)

    args = (a, b) + ((res,) if has_res else ()) + ((after,) if after is not None else ())
    in_specs = [a_spec, b_spec] + ([o_spec] if has_res else []) + ([pl.BlockSpec(TOKEN, lambda i, j, kk: (0, 0))] if after is not None else [])
    return pl.pallas_call(
        body, grid=(m // tm, n // tn, nk), in_specs=in_specs, out_specs=o_spec, out_shape=SDS((m, n), out_dtype),
        scratch_shapes=[pltpu.VMEM((tm, tn), F32)] if nk > 1 else [], name=name, compiler_params=_params(3),
    )(*args)


PIECE_TK = 1024


def _mm_pieces(pieces, b, name, res=None, after=None):
    m, n = pieces[0].shape[0], b.shape[1]
    blocks = [p.shape[1] // PIECE_TK for p in pieces]
    assert all(p.shape[1] % PIECE_TK == 0 and p.shape[0] == m for p in pieces)
    starts = [sum(blocks[:i]) for i in range(len(pieces))]
    nk = sum(blocks)
    tm, tn = _tile(m, (1024, 512, 256, 128)), _tile(n, (1024, 512, 256, 128))
    n_p = len(pieces)
    n_in = n_p + 1 + (res is not None) + (after is not None)

    def piece_spec(start, count):
        return pl.BlockSpec((tm, PIECE_TK), lambda i, j, kk: (i, jnp.clip(kk - start, 0, count - 1)))

    def body(*refs):
        b_ref, o_ref, acc_ref = refs[n_p], refs[n_in], refs[-1]
        kk = pl.program_id(2)

        @pl.when(kk == 0)
        def _():
            acc_ref[...] = jnp.zeros_like(acc_ref)

        for p_ref, start, count in zip(refs[:n_p], starts, blocks):
            @pl.when(jnp.logical_and(kk >= start, kk < start + count))
            def _(p_ref=p_ref):
                acc_ref[...] += _bdot_raw(p_ref[...], b_ref[...], "nn")

        @pl.when(kk == nk - 1)
        def _():
            out = acc_ref[...]
            if res is not None:
                out = refs[n_p + 1][...] + out
            o_ref[...] = out

    o_spec = pl.BlockSpec((tm, tn), lambda i, j, kk: (i, j))
    in_specs = [piece_spec(s, c) for s, c in zip(starts, blocks)] + [pl.BlockSpec((PIECE_TK, tn), lambda i, j, kk: (kk, j))]
    args = list(pieces) + [b]
    if res is not None:
        in_specs.append(o_spec)
        args.append(res)
    if after is not None:
        in_specs.append(pl.BlockSpec(TOKEN, lambda i, j, kk: (0, 0)))
        args.append(after)
    return pl.pallas_call(
        body, grid=(m // tm, n // tn, nk), in_specs=in_specs, out_specs=o_spec, out_shape=SDS((m, n), F32),
        scratch_shapes=[pltpu.VMEM((tm, tn), F32)], name=name, compiler_params=_params(3),
    )(*args)


def _tmap(fn, grid, ins, outs, name):
    n_in = len(ins)
    n_ax = len(grid)

    def body(*refs):
        vals = fn(*[r[...] for r in refs[:n_in]])
        if not isinstance(vals, (tuple, list)):
            vals = (vals,)
        first_inner = pl.program_id(n_ax - 1) == 0
        first_all = first_inner
        for ax in range(n_ax - 1):
            first_all = jnp.logical_and(first_all, pl.program_id(ax) == 0)

        def put(ref, val, acc):
            val = val.astype(ref.dtype)
            if acc is None:
                ref[...] = val
                return
            first = first_inner if acc == "inner" else first_all

            @pl.when(first)
            def _():
                ref[...] = val

            @pl.when(jnp.logical_not(first))
            def _():
                ref[...] += val

        for ref, val, o in zip(refs[n_in:], vals, outs):
            put(ref, val, o[4])

    return pl.pallas_call(
        body, grid=grid,
        in_specs=[pl.BlockSpec(bs, im) for _, bs, im in ins],
        out_specs=[pl.BlockSpec(o[2], o[3]) for o in outs],
        out_shape=[SDS(o[0], o[1]) for o in outs],
        name=name, compiler_params=_params(n_ax),
    )(*[a for a, _, _ in ins])


def _rows(width, tt, off=0):
    return (tt, width), (lambda j, i: (i, off + j))


def _rms(x, g):
    x = x.astype(F32)
    return x * lax.rsqrt(jnp.mean(x * x, axis=-1, keepdims=True) + EPS) * g


def _sigmoid(x):
    return jax.nn.sigmoid(x)


def _silu(x):
    return x * _sigmoid(x)


def _softplus(x):
    return jnp.maximum(x, 0.0) + jnp.log1p(jnp.exp(-jnp.abs(x)))


def _rms_fwd(x, g, name):
    t, d = x.shape
    tt = _tile(t, (256, 128))
    return _tmap(_rms, (1, t // tt), [(x, *_rows(d, tt)), (g, (1, d), lambda j, i: (0, 0))],
                 [((t, d), BF16, *_rows(d, tt), None)], name)[0]


def _rms_bwd(x, g, dn, dres, name):
    t, d = x.shape
    tt = _tile(t, (256, 128))

    def fn(x, g, dn, dres):
        _, vjp = jax.vjp(_rms, x, g)
        dx, dg = vjp(dn.astype(F32))
        return dres + dx, dg

    return _tmap(fn, (1, t // tt),
                 [(x, *_rows(d, tt)), (g, (1, d), lambda j, i: (0, 0)), (dn, *_rows(d, tt)), (dres, *_rows(d, tt))],
                 [((t, d), F32, *_rows(d, tt), None), ((1, d), F32, (1, d), lambda j, i: (0, 0), "inner")], name)


def _swiglu(a, b):
    return _silu(a) * b


def _ffn_in_act(n, w_in_t, name, after):
    t, d = n.shape
    tm, tn = _tile(t, (512, 256, 128)), D_FF // 2
    half_blocks = D_FF // tn
    n_in = 3 + (after is not None)

    def body(*refs):
        n_ref, wa_ref, wb_ref = refs[:3]
        a_ref, b_ref, s_ref = refs[n_in:]
        x = n_ref[...]
        a = _bdot_raw(x, wa_ref[...], "nt").astype(BF16)
        b = _bdot_raw(x, wb_ref[...], "nt").astype(BF16)
        a_ref[...] = a
        b_ref[...] = b
        s_ref[...] = _swiglu(a.astype(F32), b.astype(F32)).astype(BF16)

    out = pl.BlockSpec((tm, tn), lambda i, j: (i, j))
    in_specs = [pl.BlockSpec((tm, d), lambda i, j: (i, 0)), pl.BlockSpec((tn, d), lambda i, j: (j, 0)),
                pl.BlockSpec((tn, d), lambda i, j: (j + half_blocks, 0))]
    args = [n, w_in_t, w_in_t]
    if after is not None:
        in_specs.append(pl.BlockSpec(TOKEN, lambda i, j: (0, 0)))
        args.append(after)
    return pl.pallas_call(
        body, grid=(t // tm, half_blocks), in_specs=in_specs, out_specs=[out, out, out], out_shape=[SDS((t, D_FF), BF16)] * 3,
        name=name, compiler_params=_params(2),
    )(*args)


def _swiglu_bwd(a, b, ds, name):
    t = a.shape[0]
    tt = _tile(t, (128,))

    def fn(a, b, ds):
        _, vjp = jax.vjp(_swiglu, a.astype(F32), b.astype(F32))
        da, db = vjp(ds.astype(F32))
        return jnp.concatenate([da, db], axis=1)

    r = _rows(D_FF, tt)
    return _tmap(fn, (1, t // tt), [(a, *r), (b, *r), (ds, *r)], [((t, 2 * D_FF), BF16, *_rows(2 * D_FF, tt), None)], name)[0]


def _ffn_fwd(h, g, weights, tag):
    n = _rms_fwd(h, g, tag + "_norm")
    w_in_t, w_out, after = weights(n)
    a, b, s = _ffn_in_act(n, w_in_t, tag + "_in", after)
    out = _mm(s, w_out, "nn", F32, tag + "_out", res=h, alpha=0.5)
    return out, (n, a, b, s, w_in_t, w_out)


def _ffn_bwd(h, g, saved, dout, tag, links):
    n, a, b, s, w_in_t, w_out = saved
    sent = links.send({tag + "_w_out": _mm(s, dout, "tn", BF16, tag + "_dw_out", alpha=0.5)})
    ds = _mm(dout, w_out, "nt", BF16, tag + "_ds", alpha=0.5, after=sent)
    dab = _swiglu_bwd(a, b, ds, tag + "_dact")
    sent = links.send({tag + "_w_in": _mm(dab, n, "tn", BF16, tag + "_dw_in")})
    dn = _mm(dab, w_in_t, "nn", F32, tag + "_dn", after=sent)
    return _rms_bwd(h, g, dn, dout, tag + "_dnorm")


def _chunk_sum_matrix(n, chunk, transpose=False):
    row = lax.broadcasted_iota(jnp.int32, (n, n), 0)
    col = lax.broadcasted_iota(jnp.int32, (n, n), 1)
    if transpose:
        row, col = col, row
    return jnp.where(jnp.logical_and(col <= row, row // chunk == col // chunk), 1.0, 0.0).astype(F32)


def _hgrn_gates(hq, hf, lbl):
    lb = _sigmoid(lbl[0:1, :] - lbl[1:2, :])
    sg = _sigmoid(hf)
    f = lb + (1.0 - lb) * sg
    q = _silu(hq) * HEAD ** -0.5
    k = (1.0 - lb) * (1.0 - sg)
    return q, k, jnp.log(f)


def _hgrn_prep_fwd(proj, lbl):
    t = proj.shape[0]
    tt, ft = _tile(t, (256, 128)), 512

    def fn(hq, hf, lbl):
        q, k, log_f = _hgrn_gates(hq, hf, lbl)
        return q, k, _hdot_raw(_chunk_sum_matrix(tt, HG_CHUNK), log_f)

    o = ((t, D_MODEL), F32, *_rows(ft, tt), None)
    return _tmap(fn, (D_MODEL // ft, t // tt),
                 [(proj, *_rows(ft, tt, COL_HQ * HEAD // ft)), (proj, *_rows(ft, tt, COL_HF * HEAD // ft)), (lbl, (2, ft), lambda j, i: (0, j))],
                 [o, o, o], "hgrn_prep")


def _hgrn_prep_bwd(proj, lbl, dq, dk, db):
    t = proj.shape[0]
    tt, ft = _tile(t, (256, 128)), 512

    def fn(hq, hf, lbl, dq, dk, db):
        dlog_f = _hdot_raw(_chunk_sum_matrix(tt, HG_CHUNK, transpose=True), db)
        _, vjp = jax.vjp(_hgrn_gates, hq, hf, lbl)
        return vjp((dq, dk, dlog_f))

    o = ((t, D_MODEL), BF16, *_rows(ft, tt), None)
    r = _rows(ft, tt)
    return _tmap(fn, (D_MODEL // ft, t // tt),
                 [(proj, *_rows(ft, tt, COL_HQ * HEAD // ft)), (proj, *_rows(ft, tt, COL_HF * HEAD // ft)), (lbl, (2, ft), lambda j, i: (0, j)),
                  (dq, *r), (dk, *r), (db, *r)],
                 [o, o, ((2, D_MODEL), F32, (2, ft), lambda j, i: (0, j), "inner")], "hgrn_prep_bwd")


@functools.partial(jax.custom_vjp, nondiff_argnums=(1,))
def _roll_rows(x, d):
    return pltpu.roll(x, d, 0)


def _roll_rows_fwd(x, d):
    return pltpu.roll(x, d, 0), None


def _roll_rows_bwd(d, _, ct):
    return (pltpu.roll(ct, ct.shape[0] - d, 0),)


_roll_rows.defvjp(_roll_rows_fwd, _roll_rows_bwd)


def _hgrn_chunks(q, k, v, b, st):
    n = q[0].shape[0]
    half = n // 2
    srow = lax.broadcasted_iota(jnp.int32, (half, HEAD), 0)
    inter = _each(lambda q, b, st: _bdot(q * jnp.exp(b), st, "nt"), q, b, st)

    def below_scores(q, k, b):
        ref = b[half:half + 1, :]
        return _bdot(q[half:] * jnp.exp(jnp.minimum(b[half:] - ref, 0.0)), k[:half] * jnp.exp(jnp.minimum(ref - b[:half], 0.0)), "nt")

    below = _each(lambda a, v: _bdot(a, v[:half], "nn"), _each(below_scores, q, k, b), v)

    def diagonal(q, k, v, b):
        blocks = []
        for lo in (0, half):
            qb, kb, vb, bb = (a[lo:lo + half] for a in (q, k, v, b))
            o = jnp.sum(qb * kb, axis=1, keepdims=True) * vb
            for d in range(1, half):
                kr, vr, br = _roll_rows(kb, d), _roll_rows(vb, d), _roll_rows(bb, d)
                a = jnp.sum(qb * kr * jnp.exp(jnp.minimum(bb - br, 0.0)), axis=1, keepdims=True)
                o = o + jnp.where(srow[:, :1] >= d, a, 0.0) * vr
            blocks.append(o)
        return jnp.concatenate(blocks, axis=0)

    diag = _each(diagonal, q, k, v, b)
    o = _each(lambda inter, diag, below: inter + diag + jnp.concatenate([jnp.zeros_like(below), below], axis=0), inter, diag, below)

    def new_state(k, v, b, st):
        bend = b[n - 1:n, :]
        return st * jnp.exp(bend) + _bdot(v, k * jnp.exp(bend - b), "tn")

    return o, _each(new_state, k, v, b, st)


HG_GROUP = 8
HG_PER = GDN_CHUNK // HG_CHUNK


def _hgrn_rec_fwd(q, k, proj, b):
    t = q.shape[0]
    nc = t // GDN_CHUNK
    blk = (GDN_CHUNK, HG_GROUP * HEAD)
    im = lambda h, c: (c, h)

    def body(q_ref, k_ref, v_ref, b_ref, o_ref, hs_ref, st_ref):
        @pl.when(pl.program_id(1) == 0)
        def _():
            st_ref[...] = jnp.zeros_like(st_ref)

        heads = range(HG_GROUP)
        for j in range(HG_PER):
            sl = pl.ds(HG_CHUNK * j, HG_CHUNK)
            st = tuple(st_ref[g] for g in heads)
            o, st_new = _hgrn_chunks(*[tuple(r[sl, _head_lanes(g)] for g in heads) for r in (q_ref, k_ref, v_ref, b_ref)], st)
            for g in heads:
                hs_ref[g, j] = st[g]
                o_ref[sl, _head_lanes(g)] = o[g]
                st_ref[g] = st_new[g]

    return pl.pallas_call(
        body, grid=(HG_HEADS // HG_GROUP, nc),
        in_specs=[pl.BlockSpec(blk, im), pl.BlockSpec(blk, im), pl.BlockSpec(blk, lambda h, c: (c, COL_HI // HG_GROUP + h)), pl.BlockSpec(blk, im)],
        out_specs=[pl.BlockSpec(blk, im), pl.BlockSpec((HG_GROUP, HG_PER, HEAD, HEAD), lambda h, c: (h, c, 0, 0))],
        out_shape=[SDS((t, D_MODEL), F32), SDS((HG_HEADS, nc * HG_PER, HEAD, HEAD), F32)],
        scratch_shapes=[pltpu.VMEM((HG_GROUP, HEAD, HEAD), F32)], name="hgrn_rec", compiler_params=_params(2),
    )(q, k, proj, b)


def _hgrn_rec_bwd(q, k, proj, b, hs, do):
    t = q.shape[0]
    nc = t // GDN_CHUNK
    blk = (GDN_CHUNK, HG_GROUP * HEAD)
    im = lambda h, c: (nc - 1 - c, h)

    def body(q_ref, k_ref, v_ref, b_ref, hs_ref, do_ref, dq_ref, dk_ref, dv_ref, db_ref, dst_ref):
        @pl.when(pl.program_id(1) == 0)
        def _():
            dst_ref[...] = jnp.zeros_like(dst_ref)

        heads = range(HG_GROUP)
        for j in reversed(range(HG_PER)):
            sl = pl.ds(HG_CHUNK * j, HG_CHUNK)
            _, vjp = jax.vjp(_hgrn_chunks, *[tuple(r[sl, _head_lanes(g)] for g in heads) for r in (q_ref, k_ref, v_ref, b_ref)],
                             tuple(hs_ref[g, j] for g in heads))
            dq, dk, dv, db, dst = vjp((tuple(do_ref[sl, _head_lanes(g)] for g in heads), tuple(dst_ref[g] for g in heads)))
            for g in heads:
                ln = _head_lanes(g)
                dq_ref[sl, ln] = dq[g]
                dk_ref[sl, ln] = dk[g]
                dv_ref[sl, ln] = dv[g].astype(dv_ref.dtype)
                db_ref[sl, ln] = db[g]
                dst_ref[g] = dst[g]

    spec = pl.BlockSpec(blk, im)
    return pl.pallas_call(
        body, grid=(HG_HEADS // HG_GROUP, nc),
        in_specs=[spec, spec, pl.BlockSpec(blk, lambda h, c: (nc - 1 - c, COL_HI // HG_GROUP + h)), spec,
                  pl.BlockSpec((HG_GROUP, HG_PER, HEAD, HEAD), lambda h, c: (h, nc - 1 - c, 0, 0)), spec],
        out_specs=[spec, spec, spec, spec],
        out_shape=[SDS((t, D_MODEL), F32), SDS((t, D_MODEL), F32), SDS((t, D_MODEL), BF16), SDS((t, D_MODEL), F32)],
        scratch_shapes=[pltpu.VMEM((HG_GROUP, HEAD, HEAD), F32)], name="hgrn_rec_bwd", compiler_params=_params(2),
    )(q, k, proj, b, hs, do)


def _shift_down(x, d):
    if d == 0:
        return x
    row = lax.broadcasted_iota(jnp.int32, x.shape, 0)
    return jnp.where(row >= d, pltpu.roll(x, d, 0), 0.0)


def _shift_up(x, d):
    if d == 0:
        return x
    n = x.shape[0]
    row = lax.broadcasted_iota(jnp.int32, x.shape, 0)
    return jnp.where(row < n - d, pltpu.roll(x, n - d, 0), 0.0)


def _conv_fwd(proj, conv_w):
    t = proj.shape[0]
    width = 2 * D_MODEL + 2 * D_MODEL

    def body(x_ref, w_ref, c_ref, y_ref):
        x, w = x_ref[...], w_ref[...]
        y = w[CONV_K - 1:CONV_K, :] * x
        for j in range(CONV_K - 1):
            y = y + w[j:j + 1, :] * _shift_down(x, CONV_K - 1 - j)
        y_ref[...] = y
        c_ref[...] = _silu(y)

    out = pl.BlockSpec((t, HEAD), lambda j: (0, j))
    return pl.pallas_call(
        body, grid=(width // HEAD,),
        in_specs=[pl.BlockSpec((t, HEAD), lambda j: (0, COL_GQ + j)), pl.BlockSpec((CONV_K, HEAD), lambda j: (0, j))],
        out_specs=[out, out], out_shape=[SDS((t, width), F32), SDS((t, width), F32)],
        name="gdn_conv", compiler_params=_params(1),
    )(proj, conv_w)


def _conv_bwd(proj, conv_w, y, dc_qk, dc_v):
    t = proj.shape[0]
    n_qk = dc_qk.shape[1] // HEAD
    width = dc_qk.shape[1] + dc_v.shape[1]

    def body(x_ref, w_ref, y_ref, dqk_ref, dv_ref, dx_ref, dw_ref):
        x, w, y = x_ref[...], w_ref[...], y_ref[...]
        sg = _sigmoid(y)
        dc = jnp.where(pl.program_id(0) < n_qk, dqk_ref[...], dv_ref[...])
        dy = dc * (sg * (1.0 + y * (1.0 - sg)))
        ahead = [_shift_up(dy, CONV_K - 1 - j) for j in range(CONV_K)]
        dx = w[0:1, :] * ahead[0]
        for j in range(1, CONV_K):
            dx = dx + w[j:j + 1, :] * ahead[j]
        dx_ref[...] = dx.astype(dx_ref.dtype)
        dw_ref[...] = jnp.concatenate([jnp.sum(x * ahead[j], axis=0, keepdims=True) for j in range(CONV_K)], axis=0)

    blk = pl.BlockSpec((t, HEAD), lambda j: (0, j))
    return pl.pallas_call(
        body, grid=(width // HEAD,),
        in_specs=[pl.BlockSpec((t, HEAD), lambda j: (0, COL_GQ + j)), pl.BlockSpec((CONV_K, HEAD), lambda j: (0, j)), blk,
                  pl.BlockSpec((t, HEAD), lambda j: (0, jnp.minimum(j, n_qk - 1))), pl.BlockSpec((t, HEAD), lambda j: (0, jnp.maximum(j - n_qk, 0)))],
        out_specs=[blk, pl.BlockSpec((CONV_K, HEAD), lambda j: (0, j))],
        out_shape=[SDS((t, width), BF16), SDS((CONV_K, width), F32)],
        name="gdn_conv_bwd", compiler_params=_params(1),
    )(proj, conv_w, y, dc_qk, dc_v)


def _l2norm(x, scale):
    return x * lax.rsqrt(jnp.sum(x * x, axis=-1, keepdims=True) + EPS) * scale


def _head(a, h):
    return a[:, h * HEAD:(h + 1) * HEAD]


def _qk_scale(h):
    return HEAD ** -0.5 if h < GDN_QK_HEADS else 1.0


def _qk_norm_fwd(c):
    t = c.shape[0]
    tt = _tile(t, (256, 128))
    width = 2 * D_MODEL

    def fn(x):
        return jnp.concatenate([_l2norm(_head(x, h), _qk_scale(h)) for h in range(2 * GDN_QK_HEADS)], axis=1)

    return _tmap(fn, (1, t // tt), [(c, *_rows(width, tt))], [((t, width), F32, *_rows(width, tt), None)], "gdn_qk_norm")[0]


def _qk_norm_bwd(c, dq_rep, dk_rep):
    t = c.shape[0]
    tt = _tile(t, (256, 128))
    width = 2 * D_MODEL

    def fn(x, dq2, dk2):
        out = []
        for h in range(2 * GDN_QK_HEADS):
            d2, hh = (dq2, h) if h < GDN_QK_HEADS else (dk2, h - GDN_QK_HEADS)
            _, vjp = jax.vjp(lambda x: _l2norm(x, _qk_scale(h)), _head(x, h))
            out.append(vjp(_head(d2, 2 * hh) + _head(d2, 2 * hh + 1))[0])
        return jnp.concatenate(out, axis=1)

    r = _rows(width, tt)
    return _tmap(fn, (1, t // tt), [(c, *r), (dq_rep, *r), (dk_rep, *r)], [((t, width), F32, *r, None)], "gdn_qk_norm_bwd")[0]


def _gdn_gates(x, alog, dtb):
    return -jnp.exp(alog) * _softplus(x + dtb), _sigmoid(x)


def _gates_fwd(pab, alog, dtb):
    t = pab.shape[0]
    tt = _tile(t, (256, 128))

    def fn(x, alog, dtb):
        g, beta = _gdn_gates(x, alog, dtb)
        lane = lax.broadcasted_iota(jnp.int32, g.shape, 1)
        return jnp.where(lane < GDN_V_HEADS, _hdot_raw(_chunk_sum_matrix(tt, GDN_CHUNK), g), beta).T

    p = (alog, (1, HEAD), lambda j, i: (0, 0)), (dtb, (1, HEAD), lambda j, i: (0, 0))
    return _tmap(fn, (1, t // tt), [(pab, *_rows(HEAD, tt)), *p], [((HEAD, t), F32, (HEAD, tt), lambda j, i: (0, i), None)], "gdn_gates")[0]


def _gates_bwd(pab, alog, dtb, dout_t):
    t = pab.shape[0]
    tt = _tile(t, (256, 128))

    def fn(x, alog, dtb, dout_t):
        dout = dout_t.T
        lane = lax.broadcasted_iota(jnp.int32, dout.shape, 1)
        dgam = jnp.where(lane < GDN_V_HEADS, dout, 0.0)
        dbeta = jnp.where(jnp.logical_and(lane >= GDN_V_HEADS, lane < 2 * GDN_V_HEADS), dout, 0.0)
        dg = _hdot_raw(_chunk_sum_matrix(tt, GDN_CHUNK, transpose=True), dgam)
        _, vjp = jax.vjp(_gdn_gates, x, alog, dtb)
        return vjp((dg, dbeta))

    p = (alog, (1, HEAD), lambda j, i: (0, 0)), (dtb, (1, HEAD), lambda j, i: (0, 0))
    acc = ((1, HEAD), F32, (1, HEAD), lambda j, i: (0, 0), "inner")
    return _tmap(fn, (1, t // tt), [(pab, *_rows(HEAD, tt)), *p, (dout_t, (HEAD, tt), lambda j, i: (0, i))],
                 [((t, HEAD), BF16, *_rows(HEAD, tt), None), acc, acc], "gdn_gates_bwd")


def _split_bf16(x):
    hi = x.astype(BF16)
    return hi, (x - hi.astype(F32)).astype(BF16)


def _dot3(a, b):
    (ah, al), (bh, bl) = a, b
    return _bdot_raw(ah, bh, "nn") + (_bdot_raw(ah, bl, "nn") + _bdot_raw(al, bh, "nn"))


def _each(fn, *lists):
    return tuple(fn(*xs) for xs in zip(*lists))


def _unit_lower_inverses_raw(a):
    n = a[0].shape[0]
    row = lax.broadcasted_iota(jnp.int32, (n, n), 0)
    col = lax.broadcasted_iota(jnp.int32, (n, n), 1)
    eye = jnp.where(row == col, 1.0, 0.0).astype(F32)
    p = _each(lambda a: eye - a, a)
    x = _each(_split_bf16, a)
    m = 2
    while m < n:
        x = _each(_split_bf16, _each(_dot3, x, x))
        p = _each(lambda p, x: p + _bdot_raw(p, x[0], "nn"), p, x)
        m *= 2
    return p


@jax.custom_vjp
def _unit_lower_inverses(a, known):
    return _unit_lower_inverses_raw(a) if known is None else known


def _uli_fwd(a, known):
    inv = _unit_lower_inverses(a, known)
    return inv, (inv, known)


def _uli_bwd(res, ct):
    inv, known = res
    right = _each(lambda ct, inv: _bdot_raw(ct, inv, "nt"), ct, inv)
    da = _each(lambda inv, r: -_bdot_raw(inv, r, "tn"), inv, right)
    return da, (None if known is None else _each(jnp.zeros_like, known))


_unit_lower_inverses.defvjp(_uli_fwd, _uli_bwd)


def _gdn_chunks(q, k, v, beta_rows, gam_rows, s, inv_known=None):
    n = q[0].shape[0]
    heads = range(len(q))
    row = lax.broadcasted_iota(jnp.int32, (n, n), 0)
    col = lax.broadcasted_iota(jnp.int32, (n, n), 1)
    beta_cols, gam_cols = beta_rows.T, gam_rows.T
    beta = tuple(beta_cols[:, g:g + 1] for g in heads)
    gam = tuple(gam_cols[:, g:g + 1] for g in heads)
    gam_row = tuple(gam_rows[g:g + 1, :] for g in heads)
    decay = _each(lambda gam, gam_row: jnp.where(row >= col, jnp.exp(jnp.minimum(gam - gam_row, 0.0)), 0.0), gam, gam_row)
    kb = _each(lambda k, beta: k * beta, k, beta)
    a = _each(lambda kb, k, decay: jnp.where(row > col, _bdot(kb, k, "nt") * decay, 0.0), kb, k, decay)
    inv = _unit_lower_inverses(a, inv_known)
    eg = _each(jnp.exp, gam)
    u = _each(lambda inv, v, beta: _bdot(inv, v * beta, "nn"), inv, v, beta)
    w = _each(lambda inv, kb, eg: _bdot(inv, kb * eg, "nn"), inv, kb, eg)
    qk = _each(lambda q, k, decay: _bdot(q, k, "nt") * decay, q, k, decay)
    v_new = _each(lambda u, w, s: u - _bdot(w, s, "nn"), u, w, s)
    o_state = _each(lambda q, eg, s: _bdot(q * eg, s, "nn"), q, eg, s)
    o = _each(lambda o_state, qk, v_new: o_state + _bdot(qk, v_new, "nn"), o_state, qk, v_new)
    gend = _each(lambda gam: gam[n - 1:n, :], gam)
    s_new = _each(lambda s, k, gam, gend, v_new: s * jnp.exp(gend) + _bdot(k * jnp.exp(gend - gam), v_new, "tn"), s, k, gam, gend, v_new)
    return o, s_new, inv


GDN_GROUP = 16


def _gdn_specs(nc, rev):
    cc = (lambda c: nc - 1 - c) if rev else (lambda c: c)
    grp = GDN_GROUP
    q = pl.BlockSpec((GDN_CHUNK, grp // 2 * HEAD), lambda h, c: (cc(c), h))
    k = pl.BlockSpec((GDN_CHUNK, grp // 2 * HEAD), lambda h, c: (cc(c), 2 * GDN_QK_HEADS // grp + h))
    v = pl.BlockSpec((GDN_CHUNK, grp * HEAD), lambda h, c: (cc(c), 2 * GDN_QK_HEADS // grp + h))
    o = pl.BlockSpec((GDN_CHUNK, grp * HEAD), lambda h, c: (cc(c), h))
    rw = pl.BlockSpec((grp, None, 1, GDN_CHUNK), lambda h, c: (h, cc(c), 0, 0))
    st = pl.BlockSpec((grp, None, HEAD, HEAD), lambda h, c: (h, cc(c), 0, 0))
    inv = pl.BlockSpec((grp, None, GDN_CHUNK, GDN_CHUNK), lambda h, c: (h, cc(c), 0, 0))
    return q, k, v, o, rw, st, inv


def _head_lanes(g, per=1):
    return pl.ds((g // per) * HEAD, HEAD)


def _gdn_rec_fwd(qk, c, beta_row, gam_row):
    t = qk.shape[0]
    nc = t // GDN_CHUNK
    q, k, v, o, rw, st, inv = _gdn_specs(nc, False)

    def body(q_ref, k_ref, v_ref, be_ref, gr_ref, o_ref, ss_ref, inv_ref, s_ref):
        @pl.when(pl.program_id(1) == 0)
        def _():
            s_ref[...] = jnp.zeros_like(s_ref)

        heads = range(GDN_GROUP)
        s = tuple(s_ref[g] for g in heads)
        out, s_new, inv_c = _gdn_chunks(
            tuple(q_ref[:, _head_lanes(g, 2)] for g in heads), tuple(k_ref[:, _head_lanes(g, 2)] for g in heads),
            tuple(v_ref[:, _head_lanes(g)] for g in heads), be_ref[:, 0, :], gr_ref[:, 0, :], s)
        for g in heads:
            ss_ref[g] = s[g]
            o_ref[:, _head_lanes(g)] = out[g]
            inv_ref[g] = inv_c[g]
            s_ref[g] = s_new[g]

    return pl.pallas_call(
        body, grid=(GDN_V_HEADS // GDN_GROUP, nc), in_specs=[q, k, v, rw, rw], out_specs=[o, st, inv],
        out_shape=[SDS((t, 2 * D_MODEL), F32), SDS((GDN_V_HEADS, nc, HEAD, HEAD), F32), SDS((GDN_V_HEADS, nc, GDN_CHUNK, GDN_CHUNK), F32)],
        scratch_shapes=[pltpu.VMEM((GDN_GROUP, HEAD, HEAD), F32)], name="gdn_rec", compiler_params=_params(2),
    )(qk, qk, c, beta_row, gam_row)


def _gdn_rec_bwd(qk, c, beta_row, gam_row, ss, invs, do):
    t = qk.shape[0]
    nc = t // GDN_CHUNK
    q, k, v, o, rw, st, inv = _gdn_specs(nc, True)

    def body(q_ref, k_ref, v_ref, be_ref, gr_ref, ss_ref, inv_ref, do_ref,
             dq_ref, dk_ref, dv_ref, dbe_ref, dgr_ref, ds_ref):
        @pl.when(pl.program_id(1) == 0)
        def _():
            ds_ref[...] = jnp.zeros_like(ds_ref)

        heads = range(GDN_GROUP)
        _, vjp = jax.vjp(
            _gdn_chunks,
            tuple(q_ref[:, _head_lanes(g, 2)] for g in heads), tuple(k_ref[:, _head_lanes(g, 2)] for g in heads),
            tuple(v_ref[:, _head_lanes(g)] for g in heads), be_ref[:, 0, :], gr_ref[:, 0, :],
            tuple(ss_ref[g] for g in heads), tuple(inv_ref[g] for g in heads))
        no_inv_ct = tuple(jnp.zeros((GDN_CHUNK, GDN_CHUNK), F32) for g in heads)
        dq, dk, dv, dbe, dgr, ds, _ = vjp((tuple(do_ref[:, _head_lanes(g)] for g in heads), tuple(ds_ref[g] for g in heads), no_inv_ct))
        for g in heads:
            dq_ref[:, _head_lanes(g)] = dq[g]
            dk_ref[:, _head_lanes(g)] = dk[g]
            dv_ref[:, _head_lanes(g)] = dv[g]
            ds_ref[g] = ds[g]
        dbe_ref[:, 0, :] = dbe
        dgr_ref[:, 0, :] = dgr

    wide = SDS((t, 2 * D_MODEL), F32)
    rowshape = SDS((GDN_V_HEADS, nc, 1, GDN_CHUNK), F32)
    return pl.pallas_call(
        body, grid=(GDN_V_HEADS // GDN_GROUP, nc), in_specs=[q, k, v, rw, rw, st, inv, o], out_specs=[o, o, o, rw, rw],
        out_shape=[wide, wide, wide, rowshape, rowshape],
        scratch_shapes=[pltpu.VMEM((GDN_GROUP, HEAD, HEAD), F32)], name="gdn_rec_bwd", compiler_params=_params(2),
    )(qk, qk, c, beta_row, gam_row, ss, invs, do)


def _gated_norm(o, gate, w):
    return _rms(o, w) * _silu(gate)


def _post_fwd(o, proj, col_off, w, name):
    t, width = o.shape
    tt = _tile(t, (256, 128))

    def fn(o, gate, w):
        return jnp.concatenate([_gated_norm(_head(o, h), _head(gate, h), w) for h in range(width // HEAD)], axis=1)

    return _tmap(fn, (1, t // tt),
                 [(o, *_rows(width, tt)), (proj, *_rows(width, tt, col_off * HEAD // width)), (w, (1, HEAD), lambda j, i: (0, 0))],
                 [((t, width), BF16, *_rows(width, tt), None)], name)[0]


def _post_bwd(o, proj, col_off, w, dout, name):
    t, width = o.shape
    tt = _tile(t, (256, 128))

    def fn(o, gate, w, dout):
        do, dgate, dw = [], [], jnp.zeros((1, HEAD), F32)
        for h in range(width // HEAD):
            _, vjp = jax.vjp(_gated_norm, _head(o, h), _head(gate, h), w)
            a, b, c = vjp(_head(dout, h))
            do.append(a)
            dgate.append(b)
            dw = dw + c
        return jnp.concatenate(do, axis=1), jnp.concatenate(dgate, axis=1), dw

    r = _rows(width, tt)
    return _tmap(fn, (1, t // tt),
                 [(o, *r), (proj, *_rows(width, tt, col_off * HEAD // width)), (w, (1, HEAD), lambda j, i: (0, 0)), (dout, *r)],
                 [((t, width), F32, *r, None), ((t, width), BF16, *r, None), ((1, HEAD), F32, (1, HEAD), lambda j, i: (0, 0), "inner")], name)


def _merge(gate_h, gate_g, yh, yg):
    return _sigmoid(gate_h) * yh + _sigmoid(gate_g) * yg


def _merge_fwd(proj, yh, yg):
    t = yh.shape[0]
    tt, ft = _tile(t, (256, 128)), 512
    r = _rows(ft, tt)
    return _tmap(_merge, (D_MODEL // ft, t // tt),
                 [(proj, *_rows(ft, tt, COL_GATE_H * HEAD // ft)), (proj, *_rows(ft, tt, COL_GATE_G * HEAD // ft)), (yh, *r), (yg, *r)],
                 [((t, D_MODEL), BF16, *r, None)], "merge")[0]


def _merge_bwd(proj, yh, yg, dy):
    t = yh.shape[0]
    tt, ft = _tile(t, (256, 128)), 512
    r = _rows(ft, tt)

    def fn(gate_h, gate_g, yh, yg, dy):
        _, vjp = jax.vjp(_merge, gate_h, gate_g, yh, yg)
        return vjp(dy)

    o = ((t, D_MODEL), BF16, *r, None)
    return _tmap(fn, (D_MODEL // ft, t // tt),
                 [(proj, *_rows(ft, tt, COL_GATE_H * HEAD // ft)), (proj, *_rows(ft, tt, COL_GATE_G * HEAD // ft)), (yh, *r), (yg, *r), (dy, *r)],
                 [o, o, o, o], "merge_bwd")


def _loss_head(h, target, g):
    t, d = h.shape
    tt = _tile(t, (256, 128))

    def fn(h, target, g):
        def f(h, g):
            err = _rms(h, g) - target
            return 0.5 * jnp.sum(jnp.mean(err * err, axis=-1))

        loss, (dh, dg) = jax.value_and_grad(f, (0, 1))(h, g)
        return dh, dg, jnp.full((1, HEAD), loss, F32)

    return _tmap(fn, (1, t // tt), [(h, *_rows(d, tt)), (target, *_rows(d, tt)), (g, (1, d), lambda j, i: (0, 0))],
                 [((t, d), F32, *_rows(d, tt), None), ((1, d), F32, (1, d), lambda j, i: (0, 0), "inner"),
                  ((1, HEAD), F32, (1, HEAD), lambda j, i: (0, 0), "inner")], "loss_head")


def _mixer_fwd(h, p, links):
    t = h.shape[0]
    nc = t // GDN_CHUNK
    u = _rms_fwd(h, p["mix_norm"], "mix_norm")
    w = {n: links.weight(n, h) for n in ("w_in_t", "w_in_b_t", "w_in_ab_t", "conv_w")}
    proj = _mm(u, w["w_in_t"], "nt", F32, "mix_in", after=links.started, b_rows=SCALAR_ROWS)
    proj_b = _mm(u, w["w_in_b_t"], "nt", F32, "mix_in_b")
    pab = _mm(u, w["w_in_ab_t"], "nt", F32, "mix_in_ab")
    qh, kh, bh = _hgrn_prep_fwd(proj, p["lbl"])
    oh, hs = _hgrn_rec_fwd(qh, kh, proj, bh)
    c, conv_y = _conv_fwd(proj, w["conv_w"])
    qk = _qk_norm_fwd(c)
    gates_t = _gates_fwd(pab, p["alog"], p["dtb"])
    gam_row = gates_t[:GDN_V_HEADS].reshape(GDN_V_HEADS, nc, 1, GDN_CHUNK)
    beta_row = gates_t[GDN_V_HEADS:2 * GDN_V_HEADS].reshape(GDN_V_HEADS, nc, 1, GDN_CHUNK)
    og, ss, invs = _gdn_rec_fwd(qk, c, beta_row, gam_row)
    ohn = _post_fwd(oh, proj, COL_HG, p["hgrn_out_norm"], "hgrn_out")
    ogn = _post_fwd(og, proj_b, COL_GZ, p["gdn_out_norm"], "gdn_out")
    w.update({n: links.weight(n, ogn) for n in ("w_branch_hgrn", "w_branch_gdn", "w_out")})
    yh = _mm(ohn, w["w_branch_hgrn"], "nn", BF16, "branch_hgrn")
    yg = _mm(ogn, w["w_branch_gdn"], "nn", BF16, "branch_gdn")
    y = _merge_fwd(proj_b, yh, yg)
    out = _mm(y, w["w_out"], "nn", F32, "mix_out", res=h)
    saved = (w, u, proj, proj_b, pab, qh, kh, bh, oh, hs, c, conv_y, qk, beta_row, gam_row, og, ss, invs, ohn, ogn, yh, yg, y)
    return out, saved


def _mixer_bwd(h, p, links, saved, dout):
    (w, u, proj, proj_b, pab, qh, kh, bh, oh, hs, c, conv_y, qk, beta_row, gam_row, og, ss, invs, ohn, ogn, yh, yg, y) = saved
    t = h.shape[0]
    grads = {}
    dw_out = _mm(y, dout, "tn", BF16, "mix_out_dw")
    dy = _mm(dout, w["w_out"], "nt", F32, "mix_out_dx")
    dgate_h, dgate_g, dyh, dyg = _merge_bwd(proj_b, yh, yg, dy)
    dw_bh = _mm(ohn, dyh, "tn", BF16, "branch_hgrn_dw")
    dw_bg = _mm(ogn, dyg, "tn", BF16, "branch_gdn_dw")
    sent = links.send({"w_out": dw_out, "w_branch_hgrn": dw_bh, "w_branch_gdn": dw_bg})
    dohn = _mm(dyh, w["w_branch_hgrn"], "nt", F32, "branch_hgrn_dx", after=sent)
    dogn = _mm(dyg, w["w_branch_gdn"], "nt", F32, "branch_gdn_dx")
    doh, dhg, grads["hgrn_out_norm"] = _post_bwd(oh, proj, COL_HG, p["hgrn_out_norm"], dohn, "hgrn_out_bwd")
    dog, dgz, grads["gdn_out_norm"] = _post_bwd(og, proj_b, COL_GZ, p["gdn_out_norm"], dogn, "gdn_out_bwd")
    dqh, dkh, dhi, dbh = _hgrn_rec_bwd(qh, kh, proj, bh, hs, doh)
    dhq, dhf, grads["lbl"] = _hgrn_prep_bwd(proj, p["lbl"], dqh, dkh, dbh)
    dqv, dkv, dcv, dbeta_row, dgam_row = _gdn_rec_bwd(qk, c, beta_row, gam_row, ss, invs, dog)
    dcqk = _qk_norm_bwd(c, dqv, dkv)
    dxin, grads["conv_w"] = _conv_bwd(proj, w["conv_w"], conv_y, dcqk, dcv)
    dgates_t = jnp.concatenate([dgam_row.reshape(GDN_V_HEADS, t), dbeta_row.reshape(GDN_V_HEADS, t),
                                jnp.zeros((HEAD - 2 * GDN_V_HEADS, t), F32)], axis=0)
    dpab, grads["alog"], grads["dtb"] = _gates_bwd(pab, p["alog"], p["dtb"], dgates_t)
    front, back = [dhq, dhf, dhi, dhg, dxin], [dgz, dgate_h, dgate_g]
    dw_front = [_mm(d, u, "tn", BF16, "mix_in_dw_%d" % i) for i, d in enumerate(front)]
    dw_back = [_mm(d, u, "tn", BF16, "mix_in_b_dw_%d" % i) for i, d in enumerate(back)]
    dw_ab_t = _mm(dpab, u, "tn", BF16, "mix_in_ab_dw")
    sent = links.send({"w_in": jnp.concatenate(dw_front + [dw_ab_t[:N_SCALAR]] + dw_back, axis=0)})
    du = _mm_pieces(front, w["w_in_t"], "mix_in_dx", after=sent)
    du = _mm_pieces(back, w["w_in_b_t"], "mix_in_b_dx", res=du)
    du = _mm(dpab, w["w_in_ab_t"], "nn", F32, "mix_in_ab_dx", res=du)
    dh, grads["mix_norm"] = _rms_bwd(h, p["mix_norm"], du, dout, "mix_norm_bwd")
    return dh, grads


def _local_step(x, target, p, links):
    def ffn_weights(tag, behind):
        def get(n):
            w_in_t, w_out = links.weight(tag + "_w_in", n), links.weight(tag + "_w_out", n)
            return w_in_t, w_out, links.started if behind else None
        return get

    h1, s1 = _ffn_fwd(x, p["ffn1_norm"] + links.started[0, 0], ffn_weights("ffn1", True), "ffn1")
    h2, sm = _mixer_fwd(h1, p, links)
    h3, s2 = _ffn_fwd(h2, p["ffn2_norm"], ffn_weights("ffn2", False), "ffn2")
    dh3, dfinal, loss = _loss_head(h3, target, p["final_norm"])
    g = {"final_norm": dfinal}
    dh2, g["ffn2_norm"] = _ffn_bwd(h2, p["ffn2_norm"], s2, dh3, "ffn2", links)
    dh1, gm = _mixer_bwd(h1, p, links, sm, dh2)
    g.update(gm)
    dx, g["ffn1_norm"] = _ffn_bwd(x, p["ffn1_norm"], s1, dh1, "ffn1", links)
    return loss, dx, g


HBM_SPEC = pl.BlockSpec(memory_space=pltpu.HBM)
SEM_SPEC = pl.BlockSpec(memory_space=pltpu.SEMAPHORE)
DATAFLOW = pltpu.SideEffectType.DATAFLOW_SIDE_EFFECTING


def _position():
    x, y, c = lax.axis_index("x"), lax.axis_index("y"), lax.axis_index("c")
    return x, y, c, 4 * x + 2 * y + c


def _relations(x, y, c):
    for rel in range(1, N_DEV):
        px = 1 - x if rel & 4 else x
        py = 1 - y if rel & 2 else y
        pc = 1 - c if rel & 1 else c
        yield rel, (px, py, pc), 4 * px + 2 * py + pc


def _sem_index(item, rel):
    return item * (N_DEV - 1) + rel - 1


def _landing(a, mode):
    return lax.empty((N_DEV,) + a.shape if mode == "gather" else a.shape, a.dtype)


ALL_PEERS = tuple(range(1, N_DEV))
ONE_PER_CHIP = (1, 2, 4, 6)


def _copies_start(groups, name, rels=ALL_PEERS):
    flat = [item for grp in groups for item in grp]
    n, ng = len(flat), len(groups)
    lands = [_landing(a, mode) for a, mode in flat]

    def body(*refs):
        src_refs, land_refs, sems, token = refs[:n], refs[n:2 * n], refs[2 * n:2 * n + 2 * ng], refs[-1]
        x, y, c, me = _position()
        for rel, where, peer in _relations(x, y, c):
            if rel not in rels:
                continue
            k = 0
            for gi, grp in enumerate(groups):
                for li, (_, mode) in enumerate(grp):
                    src = src_refs[k] if mode == "gather" else src_refs[k].at[peer]
                    pltpu.make_async_remote_copy(src_ref=src, dst_ref=land_refs[k].at[me], send_sem=sems[2 * gi].at[_sem_index(li, rel)],
                                                 recv_sem=sems[2 * gi + 1].at[_sem_index(li, rel)], device_id=where, device_id_type=MESH_IDS).start()
                    k += 1
        token[...] = jnp.zeros_like(token)

    sem_shapes = [pltpu.SemaphoreType.DMA((len(grp) * (N_DEV - 1),)) for grp in groups for _ in range(2)]
    thru = [pltpu.HBM(a.shape, a.dtype) for a, _ in flat] + [pltpu.HBM(l.shape, l.dtype) for l in lands]
    outs = pl.pallas_call(
        body, name=name, out_shape=(*sem_shapes, *thru, SDS((8, HEAD), F32)),
        in_specs=[HBM_SPEC] * (2 * n), out_specs=(*[SEM_SPEC] * (2 * ng), *[HBM_SPEC] * (2 * n), pl.BlockSpec(memory_space=pltpu.VMEM)),
        input_output_aliases={i: 2 * ng + i for i in range(2 * n)}, compiler_params=pltpu.CompilerParams(has_side_effects=DATAFLOW),
    )(*[pltpu.with_memory_space_constraint(a, pltpu.HBM) for a, _ in flat], *[pltpu.with_memory_space_constraint(l, pltpu.HBM) for l in lands])
    sems, srcs, landed, token = outs[:2 * ng], outs[2 * ng:2 * ng + n], outs[2 * ng + n:2 * ng + 2 * n], outs[-1]
    result, k = [], 0
    for gi, grp in enumerate(groups):
        result.append((sems[2 * gi], sems[2 * gi + 1], srcs[k:k + len(grp)], landed[k:k + len(grp)]))
        k += len(grp)
    return result, token


def _copies_wait(started, modes, after, name, rels=ALL_PEERS):
    send_sems, recv_sems, srcs, lands = started
    n = len(srcs)

    def body(*refs):
        src_refs, land_refs, ssem, rsem, token = refs[:n], refs[n:2 * n], refs[2 * n], refs[2 * n + 1], refs[-1]
        x, y, c, _ = _position()
        for rel in rels:
            for i, mode in enumerate(modes):
                src = src_refs[i] if mode == "gather" else src_refs[i].at[0]
                cp = pltpu.make_async_remote_copy(src_ref=src, dst_ref=land_refs[i].at[0], send_sem=ssem.at[_sem_index(i, rel)],
                                                  recv_sem=rsem.at[_sem_index(i, rel)], device_id=(x, y, c), device_id_type=MESH_IDS)
                cp.wait_send()
                cp.wait_recv()
        token[...] = jnp.zeros_like(token)

    outs = pl.pallas_call(
        body, name=name, out_shape=[pltpu.HBM(a.shape, a.dtype) for a in (*srcs, *lands)] + [SDS((8, HEAD), F32)],
        in_specs=[HBM_SPEC] * (2 * n) + [SEM_SPEC, SEM_SPEC, pl.BlockSpec(memory_space=pl.ANY)],
        out_specs=[HBM_SPEC] * (2 * n) + [pl.BlockSpec(memory_space=pltpu.VMEM)],
        input_output_aliases={i: i for i in range(2 * n)}, compiler_params=pltpu.CompilerParams(has_side_effects=DATAFLOW),
    )(*srcs, *lands, send_sems, recv_sems, after)
    return outs[:n], outs[n:2 * n], outs[-1]


OTHER_CHIPS = ((1, 0), (0, 1), (1, 1))


def _pass_on_start(lands, name):
    n = len(lands)

    def body(*refs):
        land_refs, ssem, rsem, token = refs[:n], refs[n], refs[n + 1], refs[-1]
        x, y, c, _ = _position()
        for j, (fx, fy) in enumerate(OTHER_CHIPS):
            slot = 4 * (1 - x if fx else x) + 2 * (1 - y if fy else y) + c
            for i in range(n):
                pltpu.make_async_remote_copy(src_ref=land_refs[i].at[slot], dst_ref=land_refs[i].at[slot], send_sem=ssem.at[i * len(OTHER_CHIPS) + j],
                                             recv_sem=rsem.at[i * len(OTHER_CHIPS) + j], device_id=(x, y, 1 - c), device_id_type=MESH_IDS).start()
        token[...] = jnp.zeros_like(token)

    sems = pltpu.SemaphoreType.DMA((n * len(OTHER_CHIPS),))
    outs = pl.pallas_call(
        body, name=name, out_shape=(sems, sems, *[pltpu.HBM(l.shape, l.dtype) for l in lands], SDS(TOKEN, F32)),
        in_specs=[HBM_SPEC] * n, out_specs=(SEM_SPEC, SEM_SPEC, *[HBM_SPEC] * n, pl.BlockSpec(memory_space=pltpu.VMEM)),
        input_output_aliases={i: 2 + i for i in range(n)}, compiler_params=pltpu.CompilerParams(has_side_effects=DATAFLOW),
    )(*lands)
    return (outs[0], outs[1], outs[2:2 + n]), outs[-1]


def _pass_on_wait(started, after, name):
    send_sems, recv_sems, lands = started
    n = len(lands)

    def body(*refs):
        land_refs, ssem, rsem = refs[:n], refs[n], refs[n + 1]
        x, y, c, _ = _position()
        for j in range(len(OTHER_CHIPS)):
            for i in range(n):
                cp = pltpu.make_async_remote_copy(src_ref=land_refs[i].at[0], dst_ref=land_refs[i].at[0], send_sem=ssem.at[i * len(OTHER_CHIPS) + j],
                                                  recv_sem=rsem.at[i * len(OTHER_CHIPS) + j], device_id=(x, y, c), device_id_type=MESH_IDS)
                cp.wait_send()
                cp.wait_recv()

    return pl.pallas_call(
        body, name=name, out_shape=[pltpu.HBM(l.shape, l.dtype) for l in lands],
        in_specs=[HBM_SPEC] * n + [SEM_SPEC, SEM_SPEC, pl.BlockSpec(memory_space=pl.ANY)], out_specs=[HBM_SPEC] * n,
        input_output_aliases={i: i for i in range(n)}, compiler_params=pltpu.CompilerParams(has_side_effects=DATAFLOW),
    )(*lands, send_sems, recv_sems, after)


WEIGHT_GROUPS = (("ffn1_w_in", "ffn1_w_out", "gdn_conv_w"), ("w_in",), ("w_branch_hgrn", "w_branch_gdn", "w_out", "ffn2_w_in", "ffn2_w_out"))
GROUP_RELS = (ONE_PER_CHIP, ONE_PER_CHIP, ALL_PEERS)


class _Links:
    def __init__(self, shards, me):
        self.me = me
        self.shards = shards
        self.weights = {}
        self.sends = []
        self.gathers = {}
        self.started = None
        self._start_gather(0, None)

    def _start_gather(self, gi, zeros):
        if gi < len(WEIGHT_GROUPS):
            items = [(self.shards[n] if zeros is None else self.shards[n] + zeros[0, 0].astype(self.shards[n].dtype), "gather")
                     for n in WEIGHT_GROUPS[gi]]
            started, self.started = _copies_start([items], "gather_start_%d" % gi, GROUP_RELS[gi])
            self.gathers[gi] = started[0]

    def weight(self, name, after):
        if name not in self.weights:
            source = {"w_in_t": "w_in", "w_in_b_t": "w_in", "w_in_ab_t": "w_in", "conv_w": "gdn_conv_w"}.get(name, name)
            gi = [i for i, grp in enumerate(WEIGHT_GROUPS) if source in grp][0]
            assert gi in self.gathers, "weight groups are asked for in order"
            srcs, lands, zero = _copies_wait(self.gathers[gi], ["gather"] * len(WEIGHT_GROUPS[gi]), after, "gather_wait_%d" % gi, GROUP_RELS[gi])
            if GROUP_RELS[gi] == ONE_PER_CHIP:
                passing, zero = _pass_on_start(lands, "gather_pass_%d" % gi)
                self._start_gather(gi + 1, zero)
                lands = _pass_on_wait(passing, self.started, "gather_passed_%d" % gi)
            else:
                self._start_gather(gi + 1, zero)
            for n, src, land in zip(WEIGHT_GROUPS[gi], srcs, lands):
                full = lax.dynamic_update_index_in_dim(land, src, self.me, 0)
                if n == "gdn_conv_w":
                    self.weights["conv_w"] = full.reshape(N_DEV, CONV_K, 4 * D_MODEL // N_DEV).transpose(1, 0, 2).reshape(CONV_K, 4 * D_MODEL)
                elif n == "w_in":
                    self.weights.update(_w_in_pieces(full.reshape(-1, D_MODEL)))
                else:
                    self.weights[n] = full.reshape(-1, D_MODEL)
        return self.weights[name]

    def send(self, grads):
        names = list(grads)
        blocks = [grads[n].reshape(N_DEV, -1, D_MODEL) for n in names]
        started, token = _copies_start([[(b, "scatter") for b in blocks]], "send_" + names[0])
        self.sends.append((names, started[0]))
        return token

    def landed(self, after):
        out = {}
        for names, started in self.sends:
            srcs, lands, _ = _copies_wait(started, ["scatter"] * len(names), after, "landed_" + names[0])
            for n, src, land in zip(names, srcs, lands):
                out[n] = lax.dynamic_update_index_in_dim(land, lax.dynamic_index_in_dim(src, self.me, 0, keepdims=False), self.me, 0)
        return out


def _adam(parts, w, m, v, name):
    n_parts, r, c = parts.shape
    tc = c if c <= 512 else (256 if r > 1024 else 512)

    def body(p_ref, w_ref, m_ref, v_ref, g_ref, d_ref, mo_ref, vo_ref):
        g = p_ref[0].astype(F32)
        for i in range(1, n_parts):
            g = g + p_ref[i].astype(F32)
        m_new = ADAM_B1 * m_ref[...] + (1.0 - ADAM_B1) * g
        v_new = ADAM_B2 * v_ref[...] + (1.0 - ADAM_B2) * (g * g)
        m_hat = m_new / (1.0 - ADAM_B1 ** ADAM_STEP)
        v_hat = v_new / (1.0 - ADAM_B2 ** ADAM_STEP)
        g_ref[...] = g
        d_ref[...] = -ADAM_LR * (m_hat / (jnp.sqrt(v_hat) + ADAM_EPS) + ADAM_WD * w_ref[...])
        mo_ref[...] = m_new
        vo_ref[...] = v_new

    spec = pl.BlockSpec((r, tc), lambda j: (0, j))
    return pl.pallas_call(
        body, grid=(c // tc,), in_specs=[pl.BlockSpec((n_parts, r, tc), lambda j: (0, 0, j)), spec, spec, spec],
        out_specs=[spec] * 4, out_shape=[SDS((r, c), F32)] * 4, name=name, compiler_params=_params(1),
    )(parts, w, m, v)


BIG = ("ffn1_w_in", "ffn1_w_out", "w_in", "w_branch_hgrn", "w_branch_gdn", "w_out", "ffn2_w_in", "ffn2_w_out")


TRANSPOSED = ("ffn1_w_in", "w_in", "ffn2_w_in")


def _shard_rows(name, shard):
    return shard.T if name in TRANSPOSED else shard


SCALAR_ROWS = 8192
N_SCALAR = 2 * GDN_V_HEADS


def _w_in_pieces(w_in_t):
    return {"w_in_t": w_in_t, "w_in_b_t": w_in_t[SCALAR_ROWS + N_SCALAR:],
            "w_in_ab_t": jnp.pad(w_in_t[SCALAR_ROWS:SCALAR_ROWS + N_SCALAR], ((0, HEAD - N_SCALAR), (0, 0)))}


def _pad_lanes(a, width=HEAD):
    return jnp.pad(a, ((0, 0), (0, width - a.shape[1])))


SMALL_ROWS = 24


def _pack_small(g, loss):
    row6 = jnp.concatenate([g["hgrn_out_norm"], g["gdn_out_norm"], g["alog"], g["dtb"], loss,
                            jnp.zeros((1, D_MODEL - 5 * HEAD), F32)], axis=1)
    return jnp.concatenate([g["ffn1_norm"], g["mix_norm"], g["lbl"], g["ffn2_norm"], g["final_norm"], row6,
                            jnp.zeros((1, D_MODEL), F32), g["conv_w"].reshape(4 * CONV_K, D_MODEL)], axis=0)


def _pack_small_state(a):
    row6 = jnp.concatenate([a["hgrn_out_norm"], a["gdn_out_norm"], _pad_lanes(a["gdn_a_log"]), _pad_lanes(a["gdn_dt_bias"]),
                            jnp.zeros((1, D_MODEL - 4 * HEAD), F32)], axis=1)
    return jnp.concatenate([a["ffn1_norm"], a["mix_norm"], a["hgrn_lb_logits"], a["ffn2_norm"], a["final_norm"].reshape(1, D_MODEL),
                            row6, jnp.zeros((1, D_MODEL), F32)], axis=0)


def _unpack_small(a):
    return {"ffn1_norm": a[0:1], "mix_norm": a[1:2], "hgrn_lb_logits": a[2:4], "ffn2_norm": a[4:5], "final_norm": a[5],
            "hgrn_out_norm": a[6:7, :HEAD], "gdn_out_norm": a[6:7, HEAD:2 * HEAD],
            "gdn_a_log": a[6:7, 2 * HEAD:2 * HEAD + GDN_V_HEADS], "gdn_dt_bias": a[6:7, 3 * HEAD:3 * HEAD + GDN_V_HEADS]}


NAMES = ("ffn1_norm", "ffn1_w_in", "ffn1_w_out", "mix_norm", "w_in", "hgrn_lb_logits", "hgrn_out_norm", "gdn_conv_w", "gdn_a_log",
         "gdn_dt_bias", "gdn_out_norm", "w_branch_hgrn", "w_branch_gdn", "w_out", "ffn2_norm", "ffn2_w_in", "ffn2_w_out", "final_norm")


def kernel(x, ffn1_norm, ffn1_w_in, ffn1_w_out, mix_norm, w_in, hgrn_lb_logits, hgrn_out_norm, gdn_conv_w, gdn_a_log, gdn_dt_bias, gdn_out_norm, w_branch_hgrn, w_branch_gdn, w_out, ffn2_norm, ffn2_w_in, ffn2_w_out, final_norm, loss_target, m_ffn1_norm, m_ffn1_w_in, m_ffn1_w_out, m_mix_norm, m_w_in, m_hgrn_lb_logits, m_hgrn_out_norm, m_gdn_conv_w, m_gdn_a_log, m_gdn_dt_bias, m_gdn_out_norm, m_w_branch_hgrn, m_w_branch_gdn, m_w_out, m_ffn2_norm, m_ffn2_w_in, m_ffn2_w_out, m_final_norm, v_ffn1_norm, v_ffn1_w_in, v_ffn1_w_out, v_mix_norm, v_w_in, v_hgrn_lb_logits, v_hgrn_out_norm, v_gdn_conv_w, v_gdn_a_log, v_gdn_dt_bias, v_gdn_out_norm, v_w_branch_hgrn, v_w_branch_gdn, v_w_out, v_ffn2_norm, v_ffn2_w_in, v_ffn2_w_out, v_final_norm):
    wts = dict(zip(NAMES, (ffn1_norm, ffn1_w_in, ffn1_w_out, mix_norm, w_in, hgrn_lb_logits, hgrn_out_norm, gdn_conv_w, gdn_a_log,
                           gdn_dt_bias, gdn_out_norm, w_branch_hgrn, w_branch_gdn, w_out, ffn2_norm, ffn2_w_in, ffn2_w_out, final_norm)))
    mom = dict(zip(NAMES, (m_ffn1_norm, m_ffn1_w_in, m_ffn1_w_out, m_mix_norm, m_w_in, m_hgrn_lb_logits, m_hgrn_out_norm, m_gdn_conv_w,
                           m_gdn_a_log, m_gdn_dt_bias, m_gdn_out_norm, m_w_branch_hgrn, m_w_branch_gdn, m_w_out, m_ffn2_norm, m_ffn2_w_in,
                           m_ffn2_w_out, m_final_norm)))
    var = dict(zip(NAMES, (v_ffn1_norm, v_ffn1_w_in, v_ffn1_w_out, v_mix_norm, v_w_in, v_hgrn_lb_logits, v_hgrn_out_norm, v_gdn_conv_w,
                           v_gdn_a_log, v_gdn_dt_bias, v_gdn_out_norm, v_w_branch_hgrn, v_w_branch_gdn, v_w_out, v_ffn2_norm, v_ffn2_w_in,
                           v_ffn2_w_out, v_final_norm)))
    me = 4 * lax.axis_index("x") + 2 * lax.axis_index("y") + lax.axis_index("c")

    conv_shard = wts["gdn_conv_w"][0]
    shards = {n: _shard_rows(n, wts[n][0]).astype(BF16) for n in BIG}
    shards["gdn_conv_w"] = conv_shard.reshape(2, D_MODEL)
    links = _Links(shards, me)
    p = {"ffn1_norm": wts["ffn1_norm"], "mix_norm": wts["mix_norm"], "ffn2_norm": wts["ffn2_norm"], "final_norm": wts["final_norm"].reshape(1, D_MODEL),
         "lbl": wts["hgrn_lb_logits"], "hgrn_out_norm": wts["hgrn_out_norm"], "gdn_out_norm": wts["gdn_out_norm"],
         "alog": _pad_lanes(wts["gdn_a_log"]), "dtb": _pad_lanes(wts["gdn_dt_bias"])}

    loss, dx, g = _local_step(x[0], loss_target[0], p, links)

    small_started, small_token = _copies_start([[(_pack_small(g, loss), "gather")]], "small_start")
    landed = links.landed(small_token)

    big = [{} for _ in range(4)]
    for n in BIG:
        res = _adam(landed[n], _shard_rows(n, wts[n][0]), _shard_rows(n, mom[n][0]), _shard_rows(n, var[n][0]), "adam_" + n)
        for kind in range(4):
            big[kind][n] = _shard_rows(n, res[kind])
    small_srcs, small_lands, _ = _copies_wait(small_started[0], ["gather"], res[0], "small_wait")
    small_parts = lax.dynamic_update_index_in_dim(small_lands[0], small_srcs[0], me, 0)
    n_vec = SMALL_ROWS - 4 * CONV_K
    small_raw = _adam(small_parts[:, :n_vec], _pack_small_state(wts), _pack_small_state(mom), _pack_small_state(var), "adam_small")
    small = [_unpack_small(o) for o in small_raw]
    loss_total = small_raw[0][6, 4 * HEAD]
    conv_parts = small_parts[:, n_vec:].reshape(N_DEV, CONV_K, 4 * D_MODEL)
    width = 4 * D_MODEL // N_DEV
    conv_mine = lax.dynamic_slice_in_dim(conv_parts, me * width, width, axis=2)
    conv = _adam(conv_mine, conv_shard, mom["gdn_conv_w"][0], var["gdn_conv_w"][0], "adam_conv")

    outs = []
    for kind in range(4):
        for n in NAMES:
            if n in BIG:
                outs.append(big[kind][n][None])
            elif n == "gdn_conv_w":
                outs.append(conv[kind][None])
            else:
                outs.append(small[kind][n])
    return (loss_total, dx[None], *outs)
```

```python
import functools

import jax
import jax.numpy as jnp
from jax import lax
from jax.experimental import pallas as pl
from jax.experimental.pallas import tpu as pltpu

F32 = jnp.float32
BF16 = jnp.bfloat16
MESH_IDS = pl.DeviceIdType.MESH

D_MODEL = 1024
D_FF = 2816
N_DEV = 8
EPS = 1e-6
HEAD = 128
HG_HEADS = 8
GDN_QK_HEADS = 8
GDN_V_HEADS = 16
GDN_CHUNK = 64
HG_CHUNK = 16
CONV_K = 4
LANES = 128
COL_HQ, COL_HF, COL_HI, COL_HG, COL_GQ, COL_GK, COL_GV = 0, 8, 16, 24, 32, 40, 48
COL_GZ, COL_GATE_H, COL_GATE_G = 0, 16, 24
VMEM_LIMIT = 56 * 1024 * 1024

ADAM_LR, ADAM_B1, ADAM_B2, ADAM_EPS, ADAM_WD, ADAM_STEP = 0.001, 0.9, 0.999, 1e-08, 0.01, 10

SDS = jax.ShapeDtypeStruct


def _params(n_axes):
    return pltpu.CompilerParams(dimension_semantics=("arbitrary",) * n_axes, vmem_limit_bytes=VMEM_LIMIT)


def _tile(n, candidates=(512, 384, 256, 128, 64, 32, 16, 8)):
    for c in candidates:
        if n % c == 0:
            return c
    return n


_DIMS = {"nn": ((1,), (0,)), "nt": ((1,), (1,)), "tn": ((0,), (0,))}


def _bdot_raw(a, b, dims):
    return lax.dot_general(a.astype(BF16), b.astype(BF16), (_DIMS[dims], ((), ())), preferred_element_type=F32)


@functools.partial(jax.custom_vjp, nondiff_argnums=(2,))
def _bdot(a, b, dims):
    return _bdot_raw(a, b, dims)


def _bdot_fwd(a, b, dims):
    return _bdot_raw(a, b, dims), (a, b)


def _bdot_bwd(dims, res, ct):
    a, b = res
    if dims == "nn":
        return _bdot_raw(ct, b, "nt"), _bdot_raw(a, ct, "tn")
    if dims == "nt":
        return _bdot_raw(ct, b, "nn"), _bdot_raw(ct, a, "tn")
    return _bdot_raw(b, ct, "nt"), _bdot_raw(a, ct, "nn")


_bdot.defvjp(_bdot_fwd, _bdot_bwd)


def _hdot_raw(ones, x):
    hi = x.astype(BF16)
    rest = x - hi.astype(F32)
    mid = rest.astype(BF16)
    low = (rest - mid.astype(F32)).astype(BF16)
    return _bdot_raw(ones, hi, "nn") + (_bdot_raw(ones, mid, "nn") + _bdot_raw(ones, low, "nn"))


MM_VMEM_BUDGET = 38 * 1024 * 1024
TOKEN = (8, HEAD)


def _mm_tiles(m, n, k, a_bytes, b_bytes, o_bytes, r_bytes, m_align=8):
    def need(tm, tn, tk):
        return 2 * (tm * tk * a_bytes + tk * tn * b_bytes + tm * tn * (o_bytes + r_bytes)) + (tm * tn * 4 if tk < k else 0)

    def shrink(tm, tn, tk, floor_m, floor_n):
        while need(tm, tn, tk) > MM_VMEM_BUDGET:
            if tn > floor_n and tn % 256 == 0 and tn >= tm:
                tn //= 2
            elif tm > floor_m and tm % (2 * m_align) == 0:
                tm //= 2
            elif tn > floor_n and tn % 256 == 0:
                tn //= 2
            else:
                return None
        return tm, tn, tk

    tm = _tile(m, (1408, 1024, 704, 512, 256, 128, 64, 32, 16, 8))
    tn = _tile(n, (1408, 1024, 512, 256, 128))
    whole = shrink(tm, tn, k, min(tm, 1024), min(tn, 512))
    if whole is not None:
        return whole
    tk = _tile(k, (2048, 1408, 1024, 512, 256, 128, 64, 32, 16, 8))
    while True:
        fit = shrink(tm, tn, tk, min(tm, 256), min(tn, 512))
        if fit is not None or tk <= 512 or tk % 256:
            return fit if fit is not None else (tm, tn, tk)
        tk //= 2


def _mm(a, b, dims, out_dtype, name, res=None, alpha=1.0, after=None, b_rows=None):
    b_shape = b.shape if b_rows is None else (b_rows, b.shape[1])
    if dims == "nn":
        (m, k), (k2, n) = a.shape, b_shape
    elif dims == "nt":
        (m, k), (n, k2) = a.shape, b_shape
    else:
        (k, m), (k2, n) = a.shape, b_shape
    assert k == k2, (a.shape, b.shape, dims)
    has_res = res is not None
    tm, tn, tk = _mm_tiles(m, n, k, a.dtype.itemsize, b.dtype.itemsize, jnp.dtype(out_dtype).itemsize, res.dtype.itemsize if has_res else 0,
                           m_align=LANES if dims == "tn" else 8)
    nk = k // tk
    a_spec = pl.BlockSpec((tk, tm), lambda i, j, kk: (kk, i)) if dims == "tn" else pl.BlockSpec((tm, tk), lambda i, j, kk: (i, kk))
    b_spec = pl.BlockSpec((tn, tk), lambda i, j, kk: (j, kk)) if dims == "nt" else pl.BlockSpec((tk, tn), lambda i, j, kk: (kk, j))
    o_spec = pl.BlockSpec((tm, tn), lambda i, j, kk: (i, j))

    def finish(acc, r_ref, o_ref):
        out = acc * alpha if alpha != 1.0 else acc
        if has_res:
            out = r_ref[...].astype(F32) + out
        o_ref[...] = out.astype(o_ref.dtype)

    n_in = 2 + has_res + (after is not None)

    def body(*refs):
        a_ref, b_ref = refs[:2]
        r_ref = refs[2] if has_res else None
        o_ref = refs[n_in]
        p = _bdot_raw(a_ref[...], b_ref[...], dims)
        if nk == 1:
            finish(p, r_ref, o_ref)
            return
        acc_ref = refs[-1]
        kk = pl.program_id(2)

        @pl.when(kk == 0)
        def _():
            acc_ref[...] = p

        @pl.when(kk > 0)
        def _():
            acc_ref[...] += p

        @pl.when(kk == nk - 1)
        def _():
            finish(acc_ref[...], r_ref, o_ref)

    args = (a, b) + ((res,) if has_res else ()) + ((after,) if after is not None else ())
    in_specs = [a_spec, b_spec] + ([o_spec] if has_res else []) + ([pl.BlockSpec(TOKEN, lambda i, j, kk: (0, 0))] if after is not None else [])
    return pl.pallas_call(
        body, grid=(m // tm, n // tn, nk), in_specs=in_specs, out_specs=o_spec, out_shape=SDS((m, n), out_dtype),
        scratch_shapes=[pltpu.VMEM((tm, tn), F32)] if nk > 1 else [], name=name, compiler_params=_params(3),
    )(*args)


PIECE_TK = 1024


def _mm_pieces(pieces, b, name, res=None, after=None):
    m, n = pieces[0].shape[0], b.shape[1]
    blocks = [p.shape[1] // PIECE_TK for p in pieces]
    assert all(p.shape[1] % PIECE_TK == 0 and p.shape[0] == m for p in pieces)
    starts = [sum(blocks[:i]) for i in range(len(pieces))]
    nk = sum(blocks)
    tm, tn = _tile(m, (1024, 512, 256, 128)), _tile(n, (1024, 512, 256, 128))
    n_p = len(pieces)
    n_in = n_p + 1 + (res is not None) + (after is not None)

    def piece_spec(start, count):
        return pl.BlockSpec((tm, PIECE_TK), lambda i, j, kk: (i, jnp.clip(kk - start, 0, count - 1)))

    def body(*refs):
        b_ref, o_ref, acc_ref = refs[n_p], refs[n_in], refs[-1]
        kk = pl.program_id(2)

        @pl.when(kk == 0)
        def _():
            acc_ref[...] = jnp.zeros_like(acc_ref)

        for p_ref, start, count in zip(refs[:n_p], starts, blocks):
            @pl.when(jnp.logical_and(kk >= start, kk < start + count))
            def _(p_ref=p_ref):
                acc_ref[...] += _bdot_raw(p_ref[...], b_ref[...], "nn")

        @pl.when(kk == nk - 1)
        def _():
            out = acc_ref[...]
            if res is not None:
                out = refs[n_p + 1][...] + out
            o_ref[...] = out

    o_spec = pl.BlockSpec((tm, tn), lambda i, j, kk: (i, j))
    in_specs = [piece_spec(s, c) for s, c in zip(starts, blocks)] + [pl.BlockSpec((PIECE_TK, tn), lambda i, j, kk: (kk, j))]
    args = list(pieces) + [b]
    if res is not None:
        in_specs.append(o_spec)
        args.append(res)
    if after is not None:
        in_specs.append(pl.BlockSpec(TOKEN, lambda i, j, kk: (0, 0)))
        args.append(after)
    return pl.pallas_call(
        body, grid=(m // tm, n // tn, nk), in_specs=in_specs, out_specs=o_spec, out_shape=SDS((m, n), F32),
        scratch_shapes=[pltpu.VMEM((tm, tn), F32)], name=name, compiler_params=_params(3),
    )(*args)


def _tmap(fn, grid, ins, outs, name):
    n_in = len(ins)
    n_ax = len(grid)

    def body(*refs):
        vals = fn(*[r[...] for r in refs[:n_in]])
        if not isinstance(vals, (tuple, list)):
            vals = (vals,)
        first_inner = pl.program_id(n_ax - 1) == 0
        first_all = first_inner
        for ax in range(n_ax - 1):
            first_all = jnp.logical_and(first_all, pl.program_id(ax) == 0)

        def put(ref, val, acc):
            val = val.astype(ref.dtype)
            if acc is None:
                ref[...] = val
                return
            first = first_inner if acc == "inner" else first_all

            @pl.when(first)
            def _():
                ref[...] = val

            @pl.when(jnp.logical_not(first))
            def _():
                ref[...] += val

        for ref, val, o in zip(refs[n_in:], vals, outs):
            put(ref, val, o[4])

    return pl.pallas_call(
        body, grid=grid,
        in_specs=[pl.BlockSpec(bs, im) for _, bs, im in ins],
        out_specs=[pl.BlockSpec(o[2], o[3]) for o in outs],
        out_shape=[SDS(o[0], o[1]) for o in outs],
        name=name, compiler_params=_params(n_ax),
    )(*[a for a, _, _ in ins])


def _rows(width, tt, off=0):
    return (tt, width), (lambda j, i: (i, off + j))


def _rms(x, g):
    x = x.astype(F32)
    return x * lax.rsqrt(jnp.mean(x * x, axis=-1, keepdims=True) + EPS) * g


def _sigmoid(x):
    return jax.nn.sigmoid(x)


def _silu(x):
    return x * _sigmoid(x)


def _softplus(x):
    return jnp.maximum(x, 0.0) + jnp.log1p(jnp.exp(-jnp.abs(x)))


def _rms_fwd(x, g, name):
    t, d = x.shape
    tt = _tile(t, (256, 128))
    return _tmap(_rms, (1, t // tt), [(x, *_rows(d, tt)), (g, (1, d), lambda j, i: (0, 0))],
                 [((t, d), BF16, *_rows(d, tt), None)], name)[0]


def _rms_bwd(x, g, dn, dres, name):
    t, d = x.shape
    tt = _tile(t, (256, 128))

    def fn(x, g, dn, dres):
        _, vjp = jax.vjp(_rms, x, g)
        dx, dg = vjp(dn.astype(F32))
        return dres + dx, dg

    return _tmap(fn, (1, t // tt),
                 [(x, *_rows(d, tt)), (g, (1, d), lambda j, i: (0, 0)), (dn, *_rows(d, tt)), (dres, *_rows(d, tt))],
                 [((t, d), F32, *_rows(d, tt), None), ((1, d), F32, (1, d), lambda j, i: (0, 0), "inner")], name)


def _swiglu(a, b):
    return _silu(a) * b


def _ffn_in_act(n, w_in_t, name, after):
    t, d = n.shape
    tm, tn = _tile(t, (512, 256, 128)), D_FF // 2
    half_blocks = D_FF // tn
    n_in = 3 + (after is not None)

    def body(*refs):
        n_ref, wa_ref, wb_ref = refs[:3]
        a_ref, b_ref, s_ref = refs[n_in:]
        x = n_ref[...]
        a = _bdot_raw(x, wa_ref[...], "nt").astype(BF16)
        b = _bdot_raw(x, wb_ref[...], "nt").astype(BF16)
        a_ref[...] = a
        b_ref[...] = b
        s_ref[...] = _swiglu(a.astype(F32), b.astype(F32)).astype(BF16)

    out = pl.BlockSpec((tm, tn), lambda i, j: (i, j))
    in_specs = [pl.BlockSpec((tm, d), lambda i, j: (i, 0)), pl.BlockSpec((tn, d), lambda i, j: (j, 0)),
                pl.BlockSpec((tn, d), lambda i, j: (j + half_blocks, 0))]
    args = [n, w_in_t, w_in_t]
    if after is not None:
        in_specs.append(pl.BlockSpec(TOKEN, lambda i, j: (0, 0)))
        args.append(after)
    return pl.pallas_call(
        body, grid=(t // tm, half_blocks), in_specs=in_specs, out_specs=[out, out, out], out_shape=[SDS((t, D_FF), BF16)] * 3,
        name=name, compiler_params=_params(2),
    )(*args)


def _swiglu_bwd(a, b, ds, name):
    t = a.shape[0]
    tt = _tile(t, (128,))

    def fn(a, b, ds):
        _, vjp = jax.vjp(_swiglu, a.astype(F32), b.astype(F32))
        da, db = vjp(ds.astype(F32))
        return jnp.concatenate([da, db], axis=1)

    r = _rows(D_FF, tt)
    return _tmap(fn, (1, t // tt), [(a, *r), (b, *r), (ds, *r)], [((t, 2 * D_FF), BF16, *_rows(2 * D_FF, tt), None)], name)[0]


def _ffn_fwd(h, g, weights, tag):
    n = _rms_fwd(h, g, tag + "_norm")
    w_in_t, w_out, after = weights(n)
    a, b, s = _ffn_in_act(n, w_in_t, tag + "_in", after)
    out = _mm(s, w_out, "nn", F32, tag + "_out", res=h, alpha=0.5)
    return out, (n, a, b, s, w_in_t, w_out)


def _ffn_bwd(h, g, saved, dout, tag, links):
    n, a, b, s, w_in_t, w_out = saved
    sent = links.send({tag + "_w_out": _mm(s, dout, "tn", BF16, tag + "_dw_out", alpha=0.5)})
    ds = _mm(dout, w_out, "nt", BF16, tag + "_ds", alpha=0.5, after=sent)
    dab = _swiglu_bwd(a, b, ds, tag + "_dact")
    sent = links.send({tag + "_w_in": _mm(dab, n, "tn", BF16, tag + "_dw_in")})
    dn = _mm(dab, w_in_t, "nn", F32, tag + "_dn", after=sent)
    return _rms_bwd(h, g, dn, dout, tag + "_dnorm")


def _chunk_sum_matrix(n, chunk, transpose=False):
    row = lax.broadcasted_iota(jnp.int32, (n, n), 0)
    col = lax.broadcasted_iota(jnp.int32, (n, n), 1)
    if transpose:
        row, col = col, row
    return jnp.where(jnp.logical_and(col <= row, row // chunk == col // chunk), 1.0, 0.0).astype(F32)


def _hgrn_gates(hq, hf, lbl):
    lb = _sigmoid(lbl[0:1, :] - lbl[1:2, :])
    sg = _sigmoid(hf)
    f = lb + (1.0 - lb) * sg
    q = _silu(hq) * HEAD ** -0.5
    k = (1.0 - lb) * (1.0 - sg)
    return q, k, jnp.log(f)


def _hgrn_prep_fwd(proj, lbl):
    t = proj.shape[0]
    tt, ft = _tile(t, (256, 128)), 512

    def fn(hq, hf, lbl):
        q, k, log_f = _hgrn_gates(hq, hf, lbl)
        return q, k, _hdot_raw(_chunk_sum_matrix(tt, HG_CHUNK), log_f)

    o = ((t, D_MODEL), F32, *_rows(ft, tt), None)
    return _tmap(fn, (D_MODEL // ft, t // tt),
                 [(proj, *_rows(ft, tt, COL_HQ * HEAD // ft)), (proj, *_rows(ft, tt, COL_HF * HEAD // ft)), (lbl, (2, ft), lambda j, i: (0, j))],
                 [o, o, o], "hgrn_prep")


def _hgrn_prep_bwd(proj, lbl, dq, dk, db):
    t = proj.shape[0]
    tt, ft = _tile(t, (256, 128)), 512

    def fn(hq, hf, lbl, dq, dk, db):
        dlog_f = _hdot_raw(_chunk_sum_matrix(tt, HG_CHUNK, transpose=True), db)
        _, vjp = jax.vjp(_hgrn_gates, hq, hf, lbl)
        return vjp((dq, dk, dlog_f))

    o = ((t, D_MODEL), BF16, *_rows(ft, tt), None)
    r = _rows(ft, tt)
    return _tmap(fn, (D_MODEL // ft, t // tt),
                 [(proj, *_rows(ft, tt, COL_HQ * HEAD // ft)), (proj, *_rows(ft, tt, COL_HF * HEAD // ft)), (lbl, (2, ft), lambda j, i: (0, j)),
                  (dq, *r), (dk, *r), (db, *r)],
                 [o, o, ((2, D_MODEL), F32, (2, ft), lambda j, i: (0, j), "inner")], "hgrn_prep_bwd")


@functools.partial(jax.custom_vjp, nondiff_argnums=(1,))
def _roll_rows(x, d):
    return pltpu.roll(x, d, 0)


def _roll_rows_fwd(x, d):
    return pltpu.roll(x, d, 0), None


def _roll_rows_bwd(d, _, ct):
    return (pltpu.roll(ct, ct.shape[0] - d, 0),)


_roll_rows.defvjp(_roll_rows_fwd, _roll_rows_bwd)


def _hgrn_chunks(q, k, v, b, st):
    n = q[0].shape[0]
    half = n // 2
    srow = lax.broadcasted_iota(jnp.int32, (half, HEAD), 0)
    inter = _each(lambda q, b, st: _bdot(q * jnp.exp(b), st, "nt"), q, b, st)

    def below_scores(q, k, b):
        ref = b[half:half + 1, :]
        return _bdot(q[half:] * jnp.exp(jnp.minimum(b[half:] - ref, 0.0)), k[:half] * jnp.exp(jnp.minimum(ref - b[:half], 0.0)), "nt")

    below = _each(lambda a, v: _bdot(a, v[:half], "nn"), _each(below_scores, q, k, b), v)

    def diagonal(q, k, v, b):
        blocks = []
        for lo in (0, half):
            qb, kb, vb, bb = (a[lo:lo + half] for a in (q, k, v, b))
            o = jnp.sum(qb * kb, axis=1, keepdims=True) * vb
            for d in range(1, half):
                kr, vr, br = _roll_rows(kb, d), _roll_rows(vb, d), _roll_rows(bb, d)
                a = jnp.sum(qb * kr * jnp.exp(jnp.minimum(bb - br, 0.0)), axis=1, keepdims=True)
                o = o + jnp.where(srow[:, :1] >= d, a, 0.0) * vr
            blocks.append(o)
        return jnp.concatenate(blocks, axis=0)

    diag = _each(diagonal, q, k, v, b)
    o = _each(lambda inter, diag, below: inter + diag + jnp.concatenate([jnp.zeros_like(below), below], axis=0), inter, diag, below)

    def new_state(k, v, b, st):
        bend = b[n - 1:n, :]
        return st * jnp.exp(bend) + _bdot(v, k * jnp.exp(bend - b), "tn")

    return o, _each(new_state, k, v, b, st)


HG_GROUP = 8
HG_PER = GDN_CHUNK // HG_CHUNK


def _hgrn_rec_fwd(q, k, proj, b):
    t = q.shape[0]
    nc = t // GDN_CHUNK
    blk = (GDN_CHUNK, HG_GROUP * HEAD)
    im = lambda h, c: (c, h)

    def body(q_ref, k_ref, v_ref, b_ref, o_ref, hs_ref, st_ref):
        @pl.when(pl.program_id(1) == 0)
        def _():
            st_ref[...] = jnp.zeros_like(st_ref)

        heads = range(HG_GROUP)
        for j in range(HG_PER):
            sl = pl.ds(HG_CHUNK * j, HG_CHUNK)
            st = tuple(st_ref[g] for g in heads)
            o, st_new = _hgrn_chunks(*[tuple(r[sl, _head_lanes(g)] for g in heads) for r in (q_ref, k_ref, v_ref, b_ref)], st)
            for g in heads:
                hs_ref[g, j] = st[g]
                o_ref[sl, _head_lanes(g)] = o[g]
                st_ref[g] = st_new[g]

    return pl.pallas_call(
        body, grid=(HG_HEADS // HG_GROUP, nc),
        in_specs=[pl.BlockSpec(blk, im), pl.BlockSpec(blk, im), pl.BlockSpec(blk, lambda h, c: (c, COL_HI // HG_GROUP + h)), pl.BlockSpec(blk, im)],
        out_specs=[pl.BlockSpec(blk, im), pl.BlockSpec((HG_GROUP, HG_PER, HEAD, HEAD), lambda h, c: (h, c, 0, 0))],
        out_shape=[SDS((t, D_MODEL), F32), SDS((HG_HEADS, nc * HG_PER, HEAD, HEAD), F32)],
        scratch_shapes=[pltpu.VMEM((HG_GROUP, HEAD, HEAD), F32)], name="hgrn_rec", compiler_params=_params(2),
    )(q, k, proj, b)


def _hgrn_rec_bwd(q, k, proj, b, hs, do):
    t = q.shape[0]
    nc = t // GDN_CHUNK
    blk = (GDN_CHUNK, HG_GROUP * HEAD)
    im = lambda h, c: (nc - 1 - c, h)

    def body(q_ref, k_ref, v_ref, b_ref, hs_ref, do_ref, dq_ref, dk_ref, dv_ref, db_ref, dst_ref):
        @pl.when(pl.program_id(1) == 0)
        def _():
            dst_ref[...] = jnp.zeros_like(dst_ref)

        heads = range(HG_GROUP)
        for j in reversed(range(HG_PER)):
            sl = pl.ds(HG_CHUNK * j, HG_CHUNK)
            _, vjp = jax.vjp(_hgrn_chunks, *[tuple(r[sl, _head_lanes(g)] for g in heads) for r in (q_ref, k_ref, v_ref, b_ref)],
                             tuple(hs_ref[g, j] for g in heads))
            dq, dk, dv, db, dst = vjp((tuple(do_ref[sl, _head_lanes(g)] for g in heads), tuple(dst_ref[g] for g in heads)))
            for g in heads:
                ln = _head_lanes(g)
                dq_ref[sl, ln] = dq[g]
                dk_ref[sl, ln] = dk[g]
                dv_ref[sl, ln] = dv[g].astype(dv_ref.dtype)
                db_ref[sl, ln] = db[g]
                dst_ref[g] = dst[g]

    spec = pl.BlockSpec(blk, im)
    return pl.pallas_call(
        body, grid=(HG_HEADS // HG_GROUP, nc),
        in_specs=[spec, spec, pl.BlockSpec(blk, lambda h, c: (nc - 1 - c, COL_HI // HG_GROUP + h)), spec,
                  pl.BlockSpec((HG_GROUP, HG_PER, HEAD, HEAD), lambda h, c: (h, nc - 1 - c, 0, 0)), spec],
        out_specs=[spec, spec, spec, spec],
        out_shape=[SDS((t, D_MODEL), F32), SDS((t, D_MODEL), F32), SDS((t, D_MODEL), BF16), SDS((t, D_MODEL), F32)],
        scratch_shapes=[pltpu.VMEM((HG_GROUP, HEAD, HEAD), F32)], name="hgrn_rec_bwd", compiler_params=_params(2),
    )(q, k, proj, b, hs, do)


def _shift_down(x, d):
    if d == 0:
        return x
    row = lax.broadcasted_iota(jnp.int32, x.shape, 0)
    return jnp.where(row >= d, pltpu.roll(x, d, 0), 0.0)


def _shift_up(x, d):
    if d == 0:
        return x
    n = x.shape[0]
    row = lax.broadcasted_iota(jnp.int32, x.shape, 0)
    return jnp.where(row < n - d, pltpu.roll(x, n - d, 0), 0.0)


def _conv_fwd(proj, conv_w):
    t = proj.shape[0]
    width = 2 * D_MODEL + 2 * D_MODEL

    def body(x_ref, w_ref, c_ref, y_ref):
        x, w = x_ref[...], w_ref[...]
        y = w[CONV_K - 1:CONV_K, :] * x
        for j in range(CONV_K - 1):
            y = y + w[j:j + 1, :] * _shift_down(x, CONV_K - 1 - j)
        y_ref[...] = y
        c_ref[...] = _silu(y)

    out = pl.BlockSpec((t, HEAD), lambda j: (0, j))
    return pl.pallas_call(
        body, grid=(width // HEAD,),
        in_specs=[pl.BlockSpec((t, HEAD), lambda j: (0, COL_GQ + j)), pl.BlockSpec((CONV_K, HEAD), lambda j: (0, j))],
        out_specs=[out, out], out_shape=[SDS((t, width), F32), SDS((t, width), F32)],
        name="gdn_conv", compiler_params=_params(1),
    )(proj, conv_w)


def _conv_bwd(proj, conv_w, y, dc_qk, dc_v):
    t = proj.shape[0]
    n_qk = dc_qk.shape[1] // HEAD
    width = dc_qk.shape[1] + dc_v.shape[1]

    def body(x_ref, w_ref, y_ref, dqk_ref, dv_ref, dx_ref, dw_ref):
        x, w, y = x_ref[...], w_ref[...], y_ref[...]
        sg = _sigmoid(y)
        dc = jnp.where(pl.program_id(0) < n_qk, dqk_ref[...], dv_ref[...])
        dy = dc * (sg * (1.0 + y * (1.0 - sg)))
        ahead = [_shift_up(dy, CONV_K - 1 - j) for j in range(CONV_K)]
        dx = w[0:1, :] * ahead[0]
        for j in range(1, CONV_K):
            dx = dx + w[j:j + 1, :] * ahead[j]
        dx_ref[...] = dx.astype(dx_ref.dtype)
        dw_ref[...] = jnp.concatenate([jnp.sum(x * ahead[j], axis=0, keepdims=True) for j in range(CONV_K)], axis=0)

    blk = pl.BlockSpec((t, HEAD), lambda j: (0, j))
    return pl.pallas_call(
        body, grid=(width // HEAD,),
        in_specs=[pl.BlockSpec((t, HEAD), lambda j: (0, COL_GQ + j)), pl.BlockSpec((CONV_K, HEAD), lambda j: (0, j)), blk,
                  pl.BlockSpec((t, HEAD), lambda j: (0, jnp.minimum(j, n_qk - 1))), pl.BlockSpec((t, HEAD), lambda j: (0, jnp.maximum(j - n_qk, 0)))],
        out_specs=[blk, pl.BlockSpec((CONV_K, HEAD), lambda j: (0, j))],
        out_shape=[SDS((t, width), BF16), SDS((CONV_K, width), F32)],
        name="gdn_conv_bwd", compiler_params=_params(1),
    )(proj, conv_w, y, dc_qk, dc_v)


def _l2norm(x, scale):
    return x * lax.rsqrt(jnp.sum(x * x, axis=-1, keepdims=True) + EPS) * scale


def _head(a, h):
    return a[:, h * HEAD:(h + 1) * HEAD]


def _qk_scale(h):
    return HEAD ** -0.5 if h < GDN_QK_HEADS else 1.0


def _qk_norm_fwd(c):
    t = c.shape[0]
    tt = _tile(t, (256, 128))
    width = 2 * D_MODEL

    def fn(x):
        return jnp.concatenate([_l2norm(_head(x, h), _qk_scale(h)) for h in range(2 * GDN_QK_HEADS)], axis=1)

    return _tmap(fn, (1, t // tt), [(c, *_rows(width, tt))], [((t, width), F32, *_rows(width, tt), None)], "gdn_qk_norm")[0]


def _qk_norm_bwd(c, dq_rep, dk_rep):
    t = c.shape[0]
    tt = _tile(t, (256, 128))
    width = 2 * D_MODEL

    def fn(x, dq2, dk2):
        out = []
        for h in range(2 * GDN_QK_HEADS):
            d2, hh = (dq2, h) if h < GDN_QK_HEADS else (dk2, h - GDN_QK_HEADS)
            _, vjp = jax.vjp(lambda x: _l2norm(x, _qk_scale(h)), _head(x, h))
            out.append(vjp(_head(d2, 2 * hh) + _head(d2, 2 * hh + 1))[0])
        return jnp.concatenate(out, axis=1)

    r = _rows(width, tt)
    return _tmap(fn, (1, t // tt), [(c, *r), (dq_rep, *r), (dk_rep, *r)], [((t, width), F32, *r, None)], "gdn_qk_norm_bwd")[0]


def _gdn_gates(x, alog, dtb):
    return -jnp.exp(alog) * _softplus(x + dtb), _sigmoid(x)


def _gates_fwd(pab, alog, dtb):
    t = pab.shape[0]
    tt = _tile(t, (256, 128))

    def fn(x, alog, dtb):
        g, beta = _gdn_gates(x, alog, dtb)
        lane = lax.broadcasted_iota(jnp.int32, g.shape, 1)
        return jnp.where(lane < GDN_V_HEADS, _hdot_raw(_chunk_sum_matrix(tt, GDN_CHUNK), g), beta).T

    p = (alog, (1, HEAD), lambda j, i: (0, 0)), (dtb, (1, HEAD), lambda j, i: (0, 0))
    return _tmap(fn, (1, t // tt), [(pab, *_rows(HEAD, tt)), *p], [((HEAD, t), F32, (HEAD, tt), lambda j, i: (0, i), None)], "gdn_gates")[0]


def _gates_bwd(pab, alog, dtb, dout_t):
    t = pab.shape[0]
    tt = _tile(t, (256, 128))

    def fn(x, alog, dtb, dout_t):
        dout = dout_t.T
        lane = lax.broadcasted_iota(jnp.int32, dout.shape, 1)
        dgam = jnp.where(lane < GDN_V_HEADS, dout, 0.0)
        dbeta = jnp.where(jnp.logical_and(lane >= GDN_V_HEADS, lane < 2 * GDN_V_HEADS), dout, 0.0)
        dg = _hdot_raw(_chunk_sum_matrix(tt, GDN_CHUNK, transpose=True), dgam)
        _, vjp = jax.vjp(_gdn_gates, x, alog, dtb)
        return vjp((dg, dbeta))

    p = (alog, (1, HEAD), lambda j, i: (0, 0)), (dtb, (1, HEAD), lambda j, i: (0, 0))
    acc = ((1, HEAD), F32, (1, HEAD), lambda j, i: (0, 0), "inner")
    return _tmap(fn, (1, t // tt), [(pab, *_rows(HEAD, tt)), *p, (dout_t, (HEAD, tt), lambda j, i: (0, i))],
                 [((t, HEAD), BF16, *_rows(HEAD, tt), None), acc, acc], "gdn_gates_bwd")


def _split_bf16(x):
    hi = x.astype(BF16)
    return hi, (x - hi.astype(F32)).astype(BF16)


def _dot3(a, b):
    (ah, al), (bh, bl) = a, b
    return _bdot_raw(ah, bh, "nn") + (_bdot_raw(ah, bl, "nn") + _bdot_raw(al, bh, "nn"))


def _each(fn, *lists):
    return tuple(fn(*xs) for xs in zip(*lists))


def _unit_lower_inverses_raw(a):
    n = a[0].shape[0]
    row = lax.broadcasted_iota(jnp.int32, (n, n), 0)
    col = lax.broadcasted_iota(jnp.int32, (n, n), 1)
    eye = jnp.where(row == col, 1.0, 0.0).astype(F32)
    p = _each(lambda a: eye - a, a)
    x = _each(_split_bf16, a)
    m = 2
    while m < n:
        x = _each(_split_bf16, _each(_dot3, x, x))
        p = _each(lambda p, x: p + _bdot_raw(p, x[0], "nn"), p, x)
        m *= 2
    return p


@jax.custom_vjp
def _unit_lower_inverses(a, known):
    return _unit_lower_inverses_raw(a) if known is None else known


def _uli_fwd(a, known):
    inv = _unit_lower_inverses(a, known)
    return inv, (inv, known)


def _uli_bwd(res, ct):
    inv, known = res
    right = _each(lambda ct, inv: _bdot_raw(ct, inv, "nt"), ct, inv)
    da = _each(lambda inv, r: -_bdot_raw(inv, r, "tn"), inv, right)
    return da, (None if known is None else _each(jnp.zeros_like, known))


_unit_lower_inverses.defvjp(_uli_fwd, _uli_bwd)


def _gdn_chunks(q, k, v, beta_rows, gam_rows, s, inv_known=None):
    n = q[0].shape[0]
    heads = range(len(q))
    row = lax.broadcasted_iota(jnp.int32, (n, n), 0)
    col = lax.broadcasted_iota(jnp.int32, (n, n), 1)
    beta_cols, gam_cols = beta_rows.T, gam_rows.T
    beta = tuple(beta_cols[:, g:g + 1] for g in heads)
    gam = tuple(gam_cols[:, g:g + 1] for g in heads)
    gam_row = tuple(gam_rows[g:g + 1, :] for g in heads)
    decay = _each(lambda gam, gam_row: jnp.where(row >= col, jnp.exp(jnp.minimum(gam - gam_row, 0.0)), 0.0), gam, gam_row)
    kb = _each(lambda k, beta: k * beta, k, beta)
    a = _each(lambda kb, k, decay: jnp.where(row > col, _bdot(kb, k, "nt") * decay, 0.0), kb, k, decay)
    inv = _unit_lower_inverses(a, inv_known)
    eg = _each(jnp.exp, gam)
    u = _each(lambda inv, v, beta: _bdot(inv, v * beta, "nn"), inv, v, beta)
    w = _each(lambda inv, kb, eg: _bdot(inv, kb * eg, "nn"), inv, kb, eg)
    qk = _each(lambda q, k, decay: _bdot(q, k, "nt") * decay, q, k, decay)
    v_new = _each(lambda u, w, s: u - _bdot(w, s, "nn"), u, w, s)
    o_state = _each(lambda q, eg, s: _bdot(q * eg, s, "nn"), q, eg, s)
    o = _each(lambda o_state, qk, v_new: o_state + _bdot(qk, v_new, "nn"), o_state, qk, v_new)
    gend = _each(lambda gam: gam[n - 1:n, :], gam)
    s_new = _each(lambda s, k, gam, gend, v_new: s * jnp.exp(gend) + _bdot(k * jnp.exp(gend - gam), v_new, "tn"), s, k, gam, gend, v_new)
    return o, s_new, inv


GDN_GROUP = 16


def _gdn_specs(nc, rev):
    cc = (lambda c: nc - 1 - c) if rev else (lambda c: c)
    grp = GDN_GROUP
    q = pl.BlockSpec((GDN_CHUNK, grp // 2 * HEAD), lambda h, c: (cc(c), h))
    k = pl.BlockSpec((GDN_CHUNK, grp // 2 * HEAD), lambda h, c: (cc(c), 2 * GDN_QK_HEADS // grp + h))
    v = pl.BlockSpec((GDN_CHUNK, grp * HEAD), lambda h, c: (cc(c), 2 * GDN_QK_HEADS // grp + h))
    o = pl.BlockSpec((GDN_CHUNK, grp * HEAD), lambda h, c: (cc(c), h))
    rw = pl.BlockSpec((grp, None, 1, GDN_CHUNK), lambda h, c: (h, cc(c), 0, 0))
    st = pl.BlockSpec((grp, None, HEAD, HEAD), lambda h, c: (h, cc(c), 0, 0))
    inv = pl.BlockSpec((grp, None, GDN_CHUNK, GDN_CHUNK), lambda h, c: (h, cc(c), 0, 0))
    return q, k, v, o, rw, st, inv


def _head_lanes(g, per=1):
    return pl.ds((g // per) * HEAD, HEAD)


def _gdn_rec_fwd(qk, c, beta_row, gam_row):
    t = qk.shape[0]
    nc = t // GDN_CHUNK
    q, k, v, o, rw, st, inv = _gdn_specs(nc, False)

    def body(q_ref, k_ref, v_ref, be_ref, gr_ref, o_ref, ss_ref, inv_ref, s_ref):
        @pl.when(pl.program_id(1) == 0)
        def _():
            s_ref[...] = jnp.zeros_like(s_ref)

        heads = range(GDN_GROUP)
        s = tuple(s_ref[g] for g in heads)
        out, s_new, inv_c = _gdn_chunks(
            tuple(q_ref[:, _head_lanes(g, 2)] for g in heads), tuple(k_ref[:, _head_lanes(g, 2)] for g in heads),
            tuple(v_ref[:, _head_lanes(g)] for g in heads), be_ref[:, 0, :], gr_ref[:, 0, :], s)
        for g in heads:
            ss_ref[g] = s[g]
            o_ref[:, _head_lanes(g)] = out[g]
            inv_ref[g] = inv_c[g]
            s_ref[g] = s_new[g]

    return pl.pallas_call(
        body, grid=(GDN_V_HEADS // GDN_GROUP, nc), in_specs=[q, k, v, rw, rw], out_specs=[o, st, inv],
        out_shape=[SDS((t, 2 * D_MODEL), F32), SDS((GDN_V_HEADS, nc, HEAD, HEAD), F32), SDS((GDN_V_HEADS, nc, GDN_CHUNK, GDN_CHUNK), F32)],
        scratch_shapes=[pltpu.VMEM((GDN_GROUP, HEAD, HEAD), F32)], name="gdn_rec", compiler_params=_params(2),
    )(qk, qk, c, beta_row, gam_row)


def _gdn_rec_bwd(qk, c, beta_row, gam_row, ss, invs, do):
    t = qk.shape[0]
    nc = t // GDN_CHUNK
    q, k, v, o, rw, st, inv = _gdn_specs(nc, True)

    def body(q_ref, k_ref, v_ref, be_ref, gr_ref, ss_ref, inv_ref, do_ref,
             dq_ref, dk_ref, dv_ref, dbe_ref, dgr_ref, ds_ref):
        @pl.when(pl.program_id(1) == 0)
        def _():
            ds_ref[...] = jnp.zeros_like(ds_ref)

        heads = range(GDN_GROUP)
        _, vjp = jax.vjp(
            _gdn_chunks,
            tuple(q_ref[:, _head_lanes(g, 2)] for g in heads), tuple(k_ref[:, _head_lanes(g, 2)] for g in heads),
            tuple(v_ref[:, _head_lanes(g)] for g in heads), be_ref[:, 0, :], gr_ref[:, 0, :],
            tuple(ss_ref[g] for g in heads), tuple(inv_ref[g] for g in heads))
        no_inv_ct = tuple(jnp.zeros((GDN_CHUNK, GDN_CHUNK), F32) for g in heads)
        dq, dk, dv, dbe, dgr, ds, _ = vjp((tuple(do_ref[:, _head_lanes(g)] for g in heads), tuple(ds_ref[g] for g in heads), no_inv_ct))
        for g in heads:
            dq_ref[:, _head_lanes(g)] = dq[g]
            dk_ref[:, _head_lanes(g)] = dk[g]
            dv_ref[:, _head_lanes(g)] = dv[g]
            ds_ref[g] = ds[g]
        dbe_ref[:, 0, :] = dbe
        dgr_ref[:, 0, :] = dgr

    wide = SDS((t, 2 * D_MODEL), F32)
    rowshape = SDS((GDN_V_HEADS, nc, 1, GDN_CHUNK), F32)
    return pl.pallas_call(
        body, grid=(GDN_V_HEADS // GDN_GROUP, nc), in_specs=[q, k, v, rw, rw, st, inv, o], out_specs=[o, o, o, rw, rw],
        out_shape=[wide, wide, wide, rowshape, rowshape],
        scratch_shapes=[pltpu.VMEM((GDN_GROUP, HEAD, HEAD), F32)], name="gdn_rec_bwd", compiler_params=_params(2),
    )(qk, qk, c, beta_row, gam_row, ss, invs, do)


def _gated_norm(o, gate, w):
    return _rms(o, w) * _silu(gate)


def _post_fwd(o, proj, col_off, w, name):
    t, width = o.shape
    tt = _tile(t, (256, 128))

    def fn(o, gate, w):
        return jnp.concatenate([_gated_norm(_head(o, h), _head(gate, h), w) for h in range(width // HEAD)], axis=1)

    return _tmap(fn, (1, t // tt),
                 [(o, *_rows(width, tt)), (proj, *_rows(width, tt, col_off * HEAD // width)), (w, (1, HEAD), lambda j, i: (0, 0))],
                 [((t, width), BF16, *_rows(width, tt), None)], name)[0]


def _post_bwd(o, proj, col_off, w, dout, name):
    t, width = o.shape
    tt = _tile(t, (256, 128))

    def fn(o, gate, w, dout):
        do, dgate, dw = [], [], jnp.zeros((1, HEAD), F32)
        for h in range(width // HEAD):
            _, vjp = jax.vjp(_gated_norm, _head(o, h), _head(gate, h), w)
            a, b, c = vjp(_head(dout, h))
            do.append(a)
            dgate.append(b)
            dw = dw + c
        return jnp.concatenate(do, axis=1), jnp.concatenate(dgate, axis=1), dw

    r = _rows(width, tt)
    return _tmap(fn, (1, t // tt),
                 [(o, *r), (proj, *_rows(width, tt, col_off * HEAD // width)), (w, (1, HEAD), lambda j, i: (0, 0)), (dout, *r)],
                 [((t, width), F32, *r, None), ((t, width), BF16, *r, None), ((1, HEAD), F32, (1, HEAD), lambda j, i: (0, 0), "inner")], name)


def _merge(gate_h, gate_g, yh, yg):
    return _sigmoid(gate_h) * yh + _sigmoid(gate_g) * yg


def _merge_fwd(proj, yh, yg):
    t = yh.shape[0]
    tt, ft = _tile(t, (256, 128)), 512
    r = _rows(ft, tt)
    return _tmap(_merge, (D_MODEL // ft, t // tt),
                 [(proj, *_rows(ft, tt, COL_GATE_H * HEAD // ft)), (proj, *_rows(ft, tt, COL_GATE_G * HEAD // ft)), (yh, *r), (yg, *r)],
                 [((t, D_MODEL), BF16, *r, None)], "merge")[0]


def _merge_bwd(proj, yh, yg, dy):
    t = yh.shape[0]
    tt, ft = _tile(t, (256, 128)), 512
    r = _rows(ft, tt)

    def fn(gate_h, gate_g, yh, yg, dy):
        _, vjp = jax.vjp(_merge, gate_h, gate_g, yh, yg)
        return vjp(dy)

    o = ((t, D_MODEL), BF16, *r, None)
    return _tmap(fn, (D_MODEL // ft, t // tt),
                 [(proj, *_rows(ft, tt, COL_GATE_H * HEAD // ft)), (proj, *_rows(ft, tt, COL_GATE_G * HEAD // ft)), (yh, *r), (yg, *r), (dy, *r)],
                 [o, o, o, o], "merge_bwd")


def _loss_head(h, target, g):
    t, d = h.shape
    tt = _tile(t, (256, 128))

    def fn(h, target, g):
        def f(h, g):
            err = _rms(h, g) - target
            return 0.5 * jnp.sum(jnp.mean(err * err, axis=-1))

        loss, (dh, dg) = jax.value_and_grad(f, (0, 1))(h, g)
        return dh, dg, jnp.full((1, HEAD), loss, F32)

    return _tmap(fn, (1, t // tt), [(h, *_rows(d, tt)), (target, *_rows(d, tt)), (g, (1, d), lambda j, i: (0, 0))],
                 [((t, d), F32, *_rows(d, tt), None), ((1, d), F32, (1, d), lambda j, i: (0, 0), "inner"),
                  ((1, HEAD), F32, (1, HEAD), lambda j, i: (0, 0), "inner")], "loss_head")


def _mixer_fwd(h, p, links):
    t = h.shape[0]
    nc = t // GDN_CHUNK
    u = _rms_fwd(h, p["mix_norm"], "mix_norm")
    w = {n: links.weight(n, h) for n in ("w_in_t", "w_in_b_t", "w_in_ab_t", "conv_w")}
    proj = _mm(u, w["w_in_t"], "nt", F32, "mix_in", after=links.started, b_rows=SCALAR_ROWS)
    proj_b = _mm(u, w["w_in_b_t"], "nt", F32, "mix_in_b")
    pab = _mm(u, w["w_in_ab_t"], "nt", F32, "mix_in_ab")
    qh, kh, bh = _hgrn_prep_fwd(proj, p["lbl"])
    oh, hs = _hgrn_rec_fwd(qh, kh, proj, bh)
    c, conv_y = _conv_fwd(proj, w["conv_w"])
    qk = _qk_norm_fwd(c)
    gates_t = _gates_fwd(pab, p["alog"], p["dtb"])
    gam_row = gates_t[:GDN_V_HEADS].reshape(GDN_V_HEADS, nc, 1, GDN_CHUNK)
    beta_row = gates_t[GDN_V_HEADS:2 * GDN_V_HEADS].reshape(GDN_V_HEADS, nc, 1, GDN_CHUNK)
    og, ss, invs = _gdn_rec_fwd(qk, c, beta_row, gam_row)
    ohn = _post_fwd(oh, proj, COL_HG, p["hgrn_out_norm"], "hgrn_out")
    ogn = _post_fwd(og, proj_b, COL_GZ, p["gdn_out_norm"], "gdn_out")
    w.update({n: links.weight(n, ogn) for n in ("w_branch_hgrn", "w_branch_gdn", "w_out")})
    yh = _mm(ohn, w["w_branch_hgrn"], "nn", BF16, "branch_hgrn")
    yg = _mm(ogn, w["w_branch_gdn"], "nn", BF16, "branch_gdn")
    y = _merge_fwd(proj_b, yh, yg)
    out = _mm(y, w["w_out"], "nn", F32, "mix_out", res=h)
    saved = (w, u, proj, proj_b, pab, qh, kh, bh, oh, hs, c, conv_y, qk, beta_row, gam_row, og, ss, invs, ohn, ogn, yh, yg, y)
    return out, saved


def _mixer_bwd(h, p, links, saved, dout):
    (w, u, proj, proj_b, pab, qh, kh, bh, oh, hs, c, conv_y, qk, beta_row, gam_row, og, ss, invs, ohn, ogn, yh, yg, y) = saved
    t = h.shape[0]
    grads = {}
    dw_out = _mm(y, dout, "tn", BF16, "mix_out_dw")
    dy = _mm(dout, w["w_out"], "nt", F32, "mix_out_dx")
    dgate_h, dgate_g, dyh, dyg = _merge_bwd(proj_b, yh, yg, dy)
    dw_bh = _mm(ohn, dyh, "tn", BF16, "branch_hgrn_dw")
    dw_bg = _mm(ogn, dyg, "tn", BF16, "branch_gdn_dw")
    sent = links.send({"w_out": dw_out, "w_branch_hgrn": dw_bh, "w_branch_gdn": dw_bg})
    dohn = _mm(dyh, w["w_branch_hgrn"], "nt", F32, "branch_hgrn_dx", after=sent)
    dogn = _mm(dyg, w["w_branch_gdn"], "nt", F32, "branch_gdn_dx")
    doh, dhg, grads["hgrn_out_norm"] = _post_bwd(oh, proj, COL_HG, p["hgrn_out_norm"], dohn, "hgrn_out_bwd")
    dog, dgz, grads["gdn_out_norm"] = _post_bwd(og, proj_b, COL_GZ, p["gdn_out_norm"], dogn, "gdn_out_bwd")
    dqh, dkh, dhi, dbh = _hgrn_rec_bwd(qh, kh, proj, bh, hs, doh)
    dhq, dhf, grads["lbl"] = _hgrn_prep_bwd(proj, p["lbl"], dqh, dkh, dbh)
    dqv, dkv, dcv, dbeta_row, dgam_row = _gdn_rec_bwd(qk, c, beta_row, gam_row, ss, invs, dog)
    dcqk = _qk_norm_bwd(c, dqv, dkv)
    dxin, grads["conv_w"] = _conv_bwd(proj, w["conv_w"], conv_y, dcqk, dcv)
    dgates_t = jnp.concatenate([dgam_row.reshape(GDN_V_HEADS, t), dbeta_row.reshape(GDN_V_HEADS, t),
                                jnp.zeros((HEAD - 2 * GDN_V_HEADS, t), F32)], axis=0)
    dpab, grads["alog"], grads["dtb"] = _gates_bwd(pab, p["alog"], p["dtb"], dgates_t)
    front, back = [dhq, dhf, dhi, dhg, dxin], [dgz, dgate_h, dgate_g]
    dw_front = [_mm(d, u, "tn", BF16, "mix_in_dw_%d" % i) for i, d in enumerate(front)]
    dw_back = [_mm(d, u, "tn", BF16, "mix_in_b_dw_%d" % i) for i, d in enumerate(back)]
    dw_ab_t = _mm(dpab, u, "tn", BF16, "mix_in_ab_dw")
    sent = links.send({"w_in": jnp.concatenate(dw_front + [dw_ab_t[:N_SCALAR]] + dw_back, axis=0)})
    du = _mm_pieces(front, w["w_in_t"], "mix_in_dx", after=sent)
    du = _mm_pieces(back, w["w_in_b_t"], "mix_in_b_dx", res=du)
    du = _mm(dpab, w["w_in_ab_t"], "nn", F32, "mix_in_ab_dx", res=du)
    dh, grads["mix_norm"] = _rms_bwd(h, p["mix_norm"], du, dout, "mix_norm_bwd")
    return dh, grads


def _local_step(x, target, p, links):
    def ffn_weights(tag, behind):
        def get(n):
            w_in_t, w_out = links.weight(tag + "_w_in", n), links.weight(tag + "_w_out", n)
            return w_in_t, w_out, links.started if behind else None
        return get

    h1, s1 = _ffn_fwd(x, p["ffn1_norm"] + links.started[0, 0], ffn_weights("ffn1", True), "ffn1")
    h2, sm = _mixer_fwd(h1, p, links)
    h3, s2 = _ffn_fwd(h2, p["ffn2_norm"], ffn_weights("ffn2", False), "ffn2")
    dh3, dfinal, loss = _loss_head(h3, target, p["final_norm"])
    g = {"final_norm": dfinal}
    dh2, g["ffn2_norm"] = _ffn_bwd(h2, p["ffn2_norm"], s2, dh3, "ffn2", links)
    dh1, gm = _mixer_bwd(h1, p, links, sm, dh2)
    g.update(gm)
    dx, g["ffn1_norm"] = _ffn_bwd(x, p["ffn1_norm"], s1, dh1, "ffn1", links)
    return loss, dx, g


HBM_SPEC = pl.BlockSpec(memory_space=pltpu.HBM)
SEM_SPEC = pl.BlockSpec(memory_space=pltpu.SEMAPHORE)
DATAFLOW = pltpu.SideEffectType.DATAFLOW_SIDE_EFFECTING


def _position():
    x, y, c = lax.axis_index("x"), lax.axis_index("y"), lax.axis_index("c")
    return x, y, c, 4 * x + 2 * y + c


def _relations(x, y, c):
    for rel in range(1, N_DEV):
        px = 1 - x if rel & 4 else x
        py = 1 - y if rel & 2 else y
        pc = 1 - c if rel & 1 else c
        yield rel, (px, py, pc), 4 * px + 2 * py + pc


def _sem_index(item, rel):
    return item * (N_DEV - 1) + rel - 1


def _landing(a, mode):
    return lax.empty((N_DEV,) + a.shape if mode == "gather" else a.shape, a.dtype)


ALL_PEERS = tuple(range(1, N_DEV))
ONE_PER_CHIP = (1, 2, 4, 6)


def _copies_start(groups, name, rels=ALL_PEERS):
    flat = [item for grp in groups for item in grp]
    n, ng = len(flat), len(groups)
    lands = [_landing(a, mode) for a, mode in flat]

    def body(*refs):
        src_refs, land_refs, sems, token = refs[:n], refs[n:2 * n], refs[2 * n:2 * n + 2 * ng], refs[-1]
        x, y, c, me = _position()
        for rel, where, peer in _relations(x, y, c):
            if rel not in rels:
                continue
            k = 0
            for gi, grp in enumerate(groups):
                for li, (_, mode) in enumerate(grp):
                    src = src_refs[k] if mode == "gather" else src_refs[k].at[peer]
                    pltpu.make_async_remote_copy(src_ref=src, dst_ref=land_refs[k].at[me], send_sem=sems[2 * gi].at[_sem_index(li, rel)],
                                                 recv_sem=sems[2 * gi + 1].at[_sem_index(li, rel)], device_id=where, device_id_type=MESH_IDS).start()
                    k += 1
        token[...] = jnp.zeros_like(token)

    sem_shapes = [pltpu.SemaphoreType.DMA((len(grp) * (N_DEV - 1),)) for grp in groups for _ in range(2)]
    thru = [pltpu.HBM(a.shape, a.dtype) for a, _ in flat] + [pltpu.HBM(l.shape, l.dtype) for l in lands]
    outs = pl.pallas_call(
        body, name=name, out_shape=(*sem_shapes, *thru, SDS((8, HEAD), F32)),
        in_specs=[HBM_SPEC] * (2 * n), out_specs=(*[SEM_SPEC] * (2 * ng), *[HBM_SPEC] * (2 * n), pl.BlockSpec(memory_space=pltpu.VMEM)),
        input_output_aliases={i: 2 * ng + i for i in range(2 * n)}, compiler_params=pltpu.CompilerParams(has_side_effects=DATAFLOW),
    )(*[pltpu.with_memory_space_constraint(a, pltpu.HBM) for a, _ in flat], *[pltpu.with_memory_space_constraint(l, pltpu.HBM) for l in lands])
    sems, srcs, landed, token = outs[:2 * ng], outs[2 * ng:2 * ng + n], outs[2 * ng + n:2 * ng + 2 * n], outs[-1]
    result, k = [], 0
    for gi, grp in enumerate(groups):
        result.append((sems[2 * gi], sems[2 * gi + 1], srcs[k:k + len(grp)], landed[k:k + len(grp)]))
        k += len(grp)
    return result, token


def _copies_wait(started, modes, after, name, rels=ALL_PEERS):
    send_sems, recv_sems, srcs, lands = started
    n = len(srcs)

    def body(*refs):
        src_refs, land_refs, ssem, rsem, token = refs[:n], refs[n:2 * n], refs[2 * n], refs[2 * n + 1], refs[-1]
        x, y, c, _ = _position()
        for rel in rels:
            for i, mode in enumerate(modes):
                src = src_refs[i] if mode == "gather" else src_refs[i].at[0]
                cp = pltpu.make_async_remote_copy(src_ref=src, dst_ref=land_refs[i].at[0], send_sem=ssem.at[_sem_index(i, rel)],
                                                  recv_sem=rsem.at[_sem_index(i, rel)], device_id=(x, y, c), device_id_type=MESH_IDS)
                cp.wait_send()
                cp.wait_recv()
        token[...] = jnp.zeros_like(token)

    outs = pl.pallas_call(
        body, name=name, out_shape=[pltpu.HBM(a.shape, a.dtype) for a in (*srcs, *lands)] + [SDS((8, HEAD), F32)],
        in_specs=[HBM_SPEC] * (2 * n) + [SEM_SPEC, SEM_SPEC, pl.BlockSpec(memory_space=pl.ANY)],
        out_specs=[HBM_SPEC] * (2 * n) + [pl.BlockSpec(memory_space=pltpu.VMEM)],
        input_output_aliases={i: i for i in range(2 * n)}, compiler_params=pltpu.CompilerParams(has_side_effects=DATAFLOW),
    )(*srcs, *lands, send_sems, recv_sems, after)
    return outs[:n], outs[n:2 * n], outs[-1]


OTHER_CHIPS = ((1, 0), (0, 1), (1, 1))


def _pass_on_start(lands, name):
    n = len(lands)

    def body(*refs):
        land_refs, ssem, rsem, token = refs[:n], refs[n], refs[n + 1], refs[-1]
        x, y, c, _ = _position()
        for j, (fx, fy) in enumerate(OTHER_CHIPS):
            slot = 4 * (1 - x if fx else x) + 2 * (1 - y if fy else y) + c
            for i in range(n):
                pltpu.make_async_remote_copy(src_ref=land_refs[i].at[slot], dst_ref=land_refs[i].at[slot], send_sem=ssem.at[i * len(OTHER_CHIPS) + j],
                                             recv_sem=rsem.at[i * len(OTHER_CHIPS) + j], device_id=(x, y, 1 - c), device_id_type=MESH_IDS).start()
        token[...] = jnp.zeros_like(token)

    sems = pltpu.SemaphoreType.DMA((n * len(OTHER_CHIPS),))
    outs = pl.pallas_call(
        body, name=name, out_shape=(sems, sems, *[pltpu.HBM(l.shape, l.dtype) for l in lands], SDS(TOKEN, F32)),
        in_specs=[HBM_SPEC] * n, out_specs=(SEM_SPEC, SEM_SPEC, *[HBM_SPEC] * n, pl.BlockSpec(memory_space=pltpu.VMEM)),
        input_output_aliases={i: 2 + i for i in range(n)}, compiler_params=pltpu.CompilerParams(has_side_effects=DATAFLOW),
    )(*lands)
    return (outs[0], outs[1], outs[2:2 + n]), outs[-1]


def _pass_on_wait(started, after, name):
    send_sems, recv_sems, lands = started
    n = len(lands)

    def body(*refs):
        land_refs, ssem, rsem = refs[:n], refs[n], refs[n + 1]
        x, y, c, _ = _position()
        for j in range(len(OTHER_CHIPS)):
            for i in range(n):
                cp = pltpu.make_async_remote_copy(src_ref=land_refs[i].at[0], dst_ref=land_refs[i].at[0], send_sem=ssem.at[i * len(OTHER_CHIPS) + j],
                                                  recv_sem=rsem.at[i * len(OTHER_CHIPS) + j], device_id=(x, y, c), device_id_type=MESH_IDS)
                cp.wait_send()
                cp.wait_recv()

    return pl.pallas_call(
        body, name=name, out_shape=[pltpu.HBM(l.shape, l.dtype) for l in lands],
        in_specs=[HBM_SPEC] * n + [SEM_SPEC, SEM_SPEC, pl.BlockSpec(memory_space=pl.ANY)], out_specs=[HBM_SPEC] * n,
        input_output_aliases={i: i for i in range(n)}, compiler_params=pltpu.CompilerParams(has_side_effects=DATAFLOW),
    )(*lands, send_sems, recv_sems, after)


WEIGHT_GROUPS = (("ffn1_w_in", "ffn1_w_out", "gdn_conv_w"), ("w_in",), ("w_branch_hgrn", "w_branch_gdn", "w_out", "ffn2_w_in", "ffn2_w_out"))
GROUP_RELS = (ONE_PER_CHIP, ONE_PER_CHIP, ALL_PEERS)


class _Links:
    def __init__(self, shards, me):
        self.me = me
        self.shards = shards
        self.weights = {}
        self.sends = []
        self.gathers = {}
        self.started = None
        self._start_gather(0, None)

    def _start_gather(self, gi, zeros):
        if gi < len(WEIGHT_GROUPS):
            items = [(self.shards[n] if zeros is None else self.shards[n] + zeros[0, 0].astype(self.shards[n].dtype), "gather")
                     for n in WEIGHT_GROUPS[gi]]
            started, self.started = _copies_start([items], "gather_start_%d" % gi, GROUP_RELS[gi])
            self.gathers[gi] = started[0]

    def weight(self, name, after):
        if name not in self.weights:
            source = {"w_in_t": "w_in", "w_in_b_t": "w_in", "w_in_ab_t": "w_in", "conv_w": "gdn_conv_w"}.get(name, name)
            gi = [i for i, grp in enumerate(WEIGHT_GROUPS) if source in grp][0]
            assert gi in self.gathers, "weight groups are asked for in order"
            srcs, lands, zero = _copies_wait(self.gathers[gi], ["gather"] * len(WEIGHT_GROUPS[gi]), after, "gather_wait_%d" % gi, GROUP_RELS[gi])
            if GROUP_RELS[gi] == ONE_PER_CHIP:
                passing, zero = _pass_on_start(lands, "gather_pass_%d" % gi)
                self._start_gather(gi + 1, zero)
                lands = _pass_on_wait(passing, self.started, "gather_passed_%d" % gi)
            else:
                self._start_gather(gi + 1, zero)
            for n, src, land in zip(WEIGHT_GROUPS[gi], srcs, lands):
                full = lax.dynamic_update_index_in_dim(land, src, self.me, 0)
                if n == "gdn_conv_w":
                    self.weights["conv_w"] = full.reshape(N_DEV, CONV_K, 4 * D_MODEL // N_DEV).transpose(1, 0, 2).reshape(CONV_K, 4 * D_MODEL)
                elif n == "w_in":
                    self.weights.update(_w_in_pieces(full.reshape(-1, D_MODEL)))
                else:
                    self.weights[n] = full.reshape(-1, D_MODEL)
        return self.weights[name]

    def send(self, grads):
        names = list(grads)
        blocks = [grads[n].reshape(N_DEV, -1, D_MODEL) for n in names]
        started, token = _copies_start([[(b, "scatter") for b in blocks]], "send_" + names[0])
        self.sends.append((names, started[0]))
        return token

    def landed(self, after):
        out = {}
        for names, started in self.sends:
            srcs, lands, _ = _copies_wait(started, ["scatter"] * len(names), after, "landed_" + names[0])
            for n, src, land in zip(names, srcs, lands):
                out[n] = lax.dynamic_update_index_in_dim(land, lax.dynamic_index_in_dim(src, self.me, 0, keepdims=False), self.me, 0)
        return out


def _adam(parts, w, m, v, name):
    n_parts, r, c = parts.shape
    tc = c if c <= 512 else (256 if r > 1024 else 512)

    def body(p_ref, w_ref, m_ref, v_ref, g_ref, d_ref, mo_ref, vo_ref):
        g = p_ref[0].astype(F32)
        for i in range(1, n_parts):
            g = g + p_ref[i].astype(F32)
        m_new = ADAM_B1 * m_ref[...] + (1.0 - ADAM_B1) * g
        v_new = ADAM_B2 * v_ref[...] + (1.0 - ADAM_B2) * (g * g)
        m_hat = m_new / (1.0 - ADAM_B1 ** ADAM_STEP)
        v_hat = v_new / (1.0 - ADAM_B2 ** ADAM_STEP)
        g_ref[...] = g
        d_ref[...] = -ADAM_LR * (m_hat / (jnp.sqrt(v_hat) + ADAM_EPS) + ADAM_WD * w_ref[...])
        mo_ref[...] = m_new
        vo_ref[...] = v_new

    spec = pl.BlockSpec((r, tc), lambda j: (0, j))
    return pl.pallas_call(
        body, grid=(c // tc,), in_specs=[pl.BlockSpec((n_parts, r, tc), lambda j: (0, 0, j)), spec, spec, spec],
        out_specs=[spec] * 4, out_shape=[SDS((r, c), F32)] * 4, name=name, compiler_params=_params(1),
    )(parts, w, m, v)


BIG = ("ffn1_w_in", "ffn1_w_out", "w_in", "w_branch_hgrn", "w_branch_gdn", "w_out", "ffn2_w_in", "ffn2_w_out")


TRANSPOSED = ("ffn1_w_in", "w_in", "ffn2_w_in")


def _shard_rows(name, shard):
    return shard.T if name in TRANSPOSED else shard


SCALAR_ROWS = 8192
N_SCALAR = 2 * GDN_V_HEADS


def _w_in_pieces(w_in_t):
    return {"w_in_t": w_in_t, "w_in_b_t": w_in_t[SCALAR_ROWS + N_SCALAR:],
            "w_in_ab_t": jnp.pad(w_in_t[SCALAR_ROWS:SCALAR_ROWS + N_SCALAR], ((0, HEAD - N_SCALAR), (0, 0)))}


def _pad_lanes(a, width=HEAD):
    return jnp.pad(a, ((0, 0), (0, width - a.shape[1])))


SMALL_ROWS = 24


def _pack_small(g, loss):
    row6 = jnp.concatenate([g["hgrn_out_norm"], g["gdn_out_norm"], g["alog"], g["dtb"], loss,
                            jnp.zeros((1, D_MODEL - 5 * HEAD), F32)], axis=1)
    return jnp.concatenate([g["ffn1_norm"], g["mix_norm"], g["lbl"], g["ffn2_norm"], g["final_norm"], row6,
                            jnp.zeros((1, D_MODEL), F32), g["conv_w"].reshape(4 * CONV_K, D_MODEL)], axis=0)


def _pack_small_state(a):
    row6 = jnp.concatenate([a["hgrn_out_norm"], a["gdn_out_norm"], _pad_lanes(a["gdn_a_log"]), _pad_lanes(a["gdn_dt_bias"]),
                            jnp.zeros((1, D_MODEL - 4 * HEAD), F32)], axis=1)
    return jnp.concatenate([a["ffn1_norm"], a["mix_norm"], a["hgrn_lb_logits"], a["ffn2_norm"], a["final_norm"].reshape(1, D_MODEL),
                            row6, jnp.zeros((1, D_MODEL), F32)], axis=0)


def _unpack_small(a):
    return {"ffn1_norm": a[0:1], "mix_norm": a[1:2], "hgrn_lb_logits": a[2:4], "ffn2_norm": a[4:5], "final_norm": a[5],
            "hgrn_out_norm": a[6:7, :HEAD], "gdn_out_norm": a[6:7, HEAD:2 * HEAD],
            "gdn_a_log": a[6:7, 2 * HEAD:2 * HEAD + GDN_V_HEADS], "gdn_dt_bias": a[6:7, 3 * HEAD:3 * HEAD + GDN_V_HEADS]}


NAMES = ("ffn1_norm", "ffn1_w_in", "ffn1_w_out", "mix_norm", "w_in", "hgrn_lb_logits", "hgrn_out_norm", "gdn_conv_w", "gdn_a_log",
         "gdn_dt_bias", "gdn_out_norm", "w_branch_hgrn", "w_branch_gdn", "w_out", "ffn2_norm", "ffn2_w_in", "ffn2_w_out", "final_norm")


def kernel(x, ffn1_norm, ffn1_w_in, ffn1_w_out, mix_norm, w_in, hgrn_lb_logits, hgrn_out_norm, gdn_conv_w, gdn_a_log, gdn_dt_bias, gdn_out_norm, w_branch_hgrn, w_branch_gdn, w_out, ffn2_norm, ffn2_w_in, ffn2_w_out, final_norm, loss_target, m_ffn1_norm, m_ffn1_w_in, m_ffn1_w_out, m_mix_norm, m_w_in, m_hgrn_lb_logits, m_hgrn_out_norm, m_gdn_conv_w, m_gdn_a_log, m_gdn_dt_bias, m_gdn_out_norm, m_w_branch_hgrn, m_w_branch_gdn, m_w_out, m_ffn2_norm, m_ffn2_w_in, m_ffn2_w_out, m_final_norm, v_ffn1_norm, v_ffn1_w_in, v_ffn1_w_out, v_mix_norm, v_w_in, v_hgrn_lb_logits, v_hgrn_out_norm, v_gdn_conv_w, v_gdn_a_log, v_gdn_dt_bias, v_gdn_out_norm, v_w_branch_hgrn, v_w_branch_gdn, v_w_out, v_ffn2_norm, v_ffn2_w_in, v_ffn2_w_out, v_final_norm):
    wts = dict(zip(NAMES, (ffn1_norm, ffn1_w_in, ffn1_w_out, mix_norm, w_in, hgrn_lb_logits, hgrn_out_norm, gdn_conv_w, gdn_a_log,
                           gdn_dt_bias, gdn_out_norm, w_branch_hgrn, w_branch_gdn, w_out, ffn2_norm, ffn2_w_in, ffn2_w_out, final_norm)))
    mom = dict(zip(NAMES, (m_ffn1_norm, m_ffn1_w_in, m_ffn1_w_out, m_mix_norm, m_w_in, m_hgrn_lb_logits, m_hgrn_out_norm, m_gdn_conv_w,
                           m_gdn_a_log, m_gdn_dt_bias, m_gdn_out_norm, m_w_branch_hgrn, m_w_branch_gdn, m_w_out, m_ffn2_norm, m_ffn2_w_in,
                           m_ffn2_w_out, m_final_norm)))
    var = dict(zip(NAMES, (v_ffn1_norm, v_ffn1_w_in, v_ffn1_w_out, v_mix_norm, v_w_in, v_hgrn_lb_logits, v_hgrn_out_norm, v_gdn_conv_w,
                           v_gdn_a_log, v_gdn_dt_bias, v_gdn_out_norm, v_w_branch_hgrn, v_w_branch_gdn, v_w_out, v_ffn2_norm, v_ffn2_w_in,
                           v_ffn2_w_out, v_final_norm)))
    me = 4 * lax.axis_index("x") + 2 * lax.axis_index("y") + lax.axis_index("c")

    conv_shard = wts["gdn_conv_w"][0]
    shards = {n: _shard_rows(n, wts[n][0]).astype(BF16) for n in BIG}
    shards["gdn_conv_w"] = conv_shard.reshape(2, D_MODEL)
    links = _Links(shards, me)
    p = {"ffn1_norm": wts["ffn1_norm"], "mix_norm": wts["mix_norm"], "ffn2_norm": wts["ffn2_norm"], "final_norm": wts["final_norm"].reshape(1, D_MODEL),
         "lbl": wts["hgrn_lb_logits"], "hgrn_out_norm": wts["hgrn_out_norm"], "gdn_out_norm": wts["gdn_out_norm"],
         "alog": _pad_lanes(wts["gdn_a_log"]), "dtb": _pad_lanes(wts["gdn_dt_bias"])}

    loss, dx, g = _local_step(x[0], loss_target[0], p, links)

    small_started, small_token = _copies_start([[(_pack_small(g, loss), "gather")]], "small_start")
    landed = links.landed(small_token)

    big = [{} for _ in range(4)]
    for n in BIG:
        res = _adam(landed[n], _shard_rows(n, wts[n][0]), _shard_rows(n, mom[n][0]), _shard_rows(n, var[n][0]), "adam_" + n)
        for kind in range(4):
            big[kind][n] = _shard_rows(n, res[kind])
    all_updated = sum(big[0][n][:1, :1] for n in BIG)
    small_srcs, small_lands, _ = _copies_wait(small_started[0], ["gather"], all_updated, "small_wait")
    small_parts = lax.dynamic_update_index_in_dim(small_lands[0], small_srcs[0], me, 0)
    n_vec = SMALL_ROWS - 4 * CONV_K
    small_raw = _adam(small_parts[:, :n_vec], _pack_small_state(wts), _pack_small_state(mom), _pack_small_state(var), "adam_small")
    small = [_unpack_small(o) for o in small_raw]
    loss_total = small_raw[0][6, 4 * HEAD]
    conv_parts = small_parts[:, n_vec:].reshape(N_DEV, CONV_K, 4 * D_MODEL)
    width = 4 * D_MODEL // N_DEV
    conv_mine = lax.dynamic_slice_in_dim(conv_parts, me * width, width, axis=2)
    conv = _adam(conv_mine, conv_shard, mom["gdn_conv_w"][0], var["gdn_conv_w"][0], "adam_conv")

    outs = []
    for kind in range(4):
        for n in NAMES:
            if n in BIG:
                outs.append(big[kind][n][None])
            elif n == "gdn_conv_w":
                outs.append(conv[kind][None])
            else:
                outs.append(small[kind][n])
    return (loss_total, dx[None], *outs)
```

```python
import functools

import jax
import jax.numpy as jnp
from jax import lax
from jax.experimental import pallas as pl
from jax.experimental.pallas import tpu as pltpu

F32 = jnp.float32
BF16 = jnp.bfloat16
MESH_IDS = pl.DeviceIdType.MESH

D_MODEL = 1024
D_FF = 2816
N_DEV = 8
EPS = 1e-6
HEAD = 128
HG_HEADS = 8
GDN_QK_HEADS = 8
GDN_V_HEADS = 16
GDN_CHUNK = 64
HG_CHUNK = 16
CONV_K = 4
LANES = 128
COL_HQ, COL_HF, COL_HI, COL_HG, COL_GQ, COL_GK, COL_GV = 0, 8, 16, 24, 32, 40, 48
COL_GZ, COL_GATE_H, COL_GATE_G = 0, 16, 24
VMEM_LIMIT = 56 * 1024 * 1024

ADAM_LR, ADAM_B1, ADAM_B2, ADAM_EPS, ADAM_WD, ADAM_STEP = 0.001, 0.9, 0.999, 1e-08, 0.01, 10

SDS = jax.ShapeDtypeStruct


def _params(n_axes):
    return pltpu.CompilerParams(dimension_semantics=("arbitrary",) * n_axes, vmem_limit_bytes=VMEM_LIMIT)


def _tile(n, candidates=(512, 384, 256, 128, 64, 32, 16, 8)):
    for c in candidates:
        if n % c == 0:
            return c
    return n


_DIMS = {"nn": ((1,), (0,)), "nt": ((1,), (1,)), "tn": ((0,), (0,))}


def _bdot_raw(a, b, dims):
    return lax.dot_general(a.astype(BF16), b.astype(BF16), (_DIMS[dims], ((), ())), preferred_element_type=F32)


@functools.partial(jax.custom_vjp, nondiff_argnums=(2,))
def _bdot(a, b, dims):
    return _bdot_raw(a, b, dims)


def _bdot_fwd(a, b, dims):
    return _bdot_raw(a, b, dims), (a, b)


def _bdot_bwd(dims, res, ct):
    a, b = res
    if dims == "nn":
        return _bdot_raw(ct, b, "nt"), _bdot_raw(a, ct, "tn")
    if dims == "nt":
        return _bdot_raw(ct, b, "nn"), _bdot_raw(ct, a, "tn")
    return _bdot_raw(b, ct, "nt"), _bdot_raw(a, ct, "nn")


_bdot.defvjp(_bdot_fwd, _bdot_bwd)


def _hdot_raw(ones, x):
    hi = x.astype(BF16)
    rest = x - hi.astype(F32)
    mid = rest.astype(BF16)
    low = (rest - mid.astype(F32)).astype(BF16)
    return _bdot_raw(ones, hi, "nn") + (_bdot_raw(ones, mid, "nn") + _bdot_raw(ones, low, "nn"))


MM_VMEM_BUDGET = 38 * 1024 * 1024
TOKEN = (8, HEAD)


def _mm_tiles(m, n, k, a_bytes, b_bytes, o_bytes, r_bytes, m_align=8):
    def need(tm, tn, tk):
        return 2 * (tm * tk * a_bytes + tk * tn * b_bytes + tm * tn * (o_bytes + r_bytes)) + (tm * tn * 4 if tk < k else 0)

    def shrink(tm, tn, tk, floor_m, floor_n):
        while need(tm, tn, tk) > MM_VMEM_BUDGET:
            if tn > floor_n and tn % 256 == 0 and tn >= tm:
                tn //= 2
            elif tm > floor_m and tm % (2 * m_align) == 0:
                tm //= 2
            elif tn > floor_n and tn % 256 == 0:
                tn //= 2
            else:
                return None
        return tm, tn, tk

    tm = _tile(m, (1408, 1024, 704, 512, 256, 128, 64, 32, 16, 8))
    tn = _tile(n, (1408, 1024, 512, 256, 128))
    whole = shrink(tm, tn, k, min(tm, 1024), min(tn, 512))
    if whole is not None:
        return whole
    tk = _tile(k, (2048, 1408, 1024, 512, 256, 128, 64, 32, 16, 8))
    while True:
        fit = shrink(tm, tn, tk, min(tm, 256), min(tn, 512))
        if fit is not None or tk <= 512 or tk % 256:
            return fit if fit is not None else (tm, tn, tk)
        tk //= 2


def _mm(a, b, dims, out_dtype, name, res=None, alpha=1.0, after=None, b_rows=None):
    b_shape = b.shape if b_rows is None else (b_rows, b.shape[1])
    if dims == "nn":
        (m, k), (k2, n) = a.shape, b_shape
    elif dims == "nt":
        (m, k), (n, k2) = a.shape, b_shape
    else:
        (k, m), (k2, n) = a.shape, b_shape
    assert k == k2, (a.shape, b.shape, dims)
    has_res = res is not None
    tm, tn, tk = _mm_tiles(m, n, k, a.dtype.itemsize, b.dtype.itemsize, jnp.dtype(out_dtype).itemsize, res.dtype.itemsize if has_res else 0,
                           m_align=LANES if dims == "tn" else 8)
    nk = k // tk
    a_spec = pl.BlockSpec((tk, tm), lambda i, j, kk: (kk, i)) if dims == "tn" else pl.BlockSpec((tm, tk), lambda i, j, kk: (i, kk))
    b_spec = pl.BlockSpec((tn, tk), lambda i, j, kk: (j, kk)) if dims == "nt" else pl.BlockSpec((tk, tn), lambda i, j, kk: (kk, j))
    o_spec = pl.BlockSpec((tm, tn), lambda i, j, kk: (i, j))

    def finish(acc, r_ref, o_ref):
        out = acc * alpha if alpha != 1.0 else acc
        if has_res:
            out = r_ref[...].astype(F32) + out
        o_ref[...] = out.astype(o_ref.dtype)

    n_in = 2 + has_res + (after is not None)

    def body(*refs):
        a_ref, b_ref = refs[:2]
        r_ref = refs[2] if has_res else None
        o_ref = refs[n_in]
        p = _bdot_raw(a_ref[...], b_ref[...], dims)
        if nk == 1:
            finish(p, r_ref, o_ref)
            return
        acc_ref = refs[-1]
        kk = pl.program_id(2)

        @pl.when(kk == 0)
        def _():
            acc_ref[...] = p

        @pl.when(kk > 0)
        def _():
            acc_ref[...] += p

        @pl.when(kk == nk - 1)
        def _():
            finish(acc_ref[...], r_ref, o_ref)

    args = (a, b) + ((res,) if has_res else ()) + ((after,) if after is not None else ())
    in_specs = [a_spec, b_spec] + ([o_spec] if has_res else []) + ([pl.BlockSpec(TOKEN, lambda i, j, kk: (0, 0))] if after is not None else [])
    return pl.pallas_call(
        body, grid=(m // tm, n // tn, nk), in_specs=in_specs, out_specs=o_spec, out_shape=SDS((m, n), out_dtype),
        scratch_shapes=[pltpu.VMEM((tm, tn), F32)] if nk > 1 else [], name=name, compiler_params=_params(3),
    )(*args)


PIECE_TK = 1024


def _mm_pieces(pieces, b, name, res=None, after=None):
    m, n = pieces[0].shape[0], b.shape[1]
    blocks = [p.shape[1] // PIECE_TK for p in pieces]
    assert all(p.shape[1] % PIECE_TK == 0 and p.shape[0] == m for p in pieces)
    starts = [sum(blocks[:i]) for i in range(len(pieces))]
    nk = sum(blocks)
    tm, tn = _tile(m, (1024, 512, 256, 128)), _tile(n, (1024, 512, 256, 128))
    n_p = len(pieces)
    n_in = n_p + 1 + (res is not None) + (after is not None)

    def piece_spec(start, count):
        return pl.BlockSpec((tm, PIECE_TK), lambda i, j, kk: (i, jnp.clip(kk - start, 0, count - 1)))

    def body(*refs):
        b_ref, o_ref, acc_ref = refs[n_p], refs[n_in], refs[-1]
        kk = pl.program_id(2)

        @pl.when(kk == 0)
        def _():
            acc_ref[...] = jnp.zeros_like(acc_ref)

        for p_ref, start, count in zip(refs[:n_p], starts, blocks):
            @pl.when(jnp.logical_and(kk >= start, kk < start + count))
            def _(p_ref=p_ref):
                acc_ref[...] += _bdot_raw(p_ref[...], b_ref[...], "nn")

        @pl.when(kk == nk - 1)
        def _():
            out = acc_ref[...]
            if res is not None:
                out = refs[n_p + 1][...] + out
            o_ref[...] = out

    o_spec = pl.BlockSpec((tm, tn), lambda i, j, kk: (i, j))
    in_specs = [piece_spec(s, c) for s, c in zip(starts, blocks)] + [pl.BlockSpec((PIECE_TK, tn), lambda i, j, kk: (kk, j))]
    args = list(pieces) + [b]
    if res is not None:
        in_specs.append(o_spec)
        args.append(res)
    if after is not None:
        in_specs.append(pl.BlockSpec(TOKEN, lambda i, j, kk: (0, 0)))
        args.append(after)
    return pl.pallas_call(
        body, grid=(m // tm, n // tn, nk), in_specs=in_specs, out_specs=o_spec, out_shape=SDS((m, n), F32),
        scratch_shapes=[pltpu.VMEM((tm, tn), F32)], name=name, compiler_params=_params(3),
    )(*args)


def _tmap(fn, grid, ins, outs, name):
    n_in = len(ins)
    n_ax = len(grid)

    def body(*refs):
        vals = fn(*[r[...] for r in refs[:n_in]])
        if not isinstance(vals, (tuple, list)):
            vals = (vals,)
        first_inner = pl.program_id(n_ax - 1) == 0
        first_all = first_inner
        for ax in range(n_ax - 1):
            first_all = jnp.logical_and(first_all, pl.program_id(ax) == 0)

        def put(ref, val, acc):
            val = val.astype(ref.dtype)
            if acc is None:
                ref[...] = val
                return
            first = first_inner if acc == "inner" else first_all

            @pl.when(first)
            def _():
                ref[...] = val

            @pl.when(jnp.logical_not(first))
            def _():
                ref[...] += val

        for ref, val, o in zip(refs[n_in:], vals, outs):
            put(ref, val, o[4])

    return pl.pallas_call(
        body, grid=grid,
        in_specs=[pl.BlockSpec(bs, im) for _, bs, im in ins],
        out_specs=[pl.BlockSpec(o[2], o[3]) for o in outs],
        out_shape=[SDS(o[0], o[1]) for o in outs],
        name=name, compiler_params=_params(n_ax),
    )(*[a for a, _, _ in ins])


def _rows(width, tt, off=0):
    return (tt, width), (lambda j, i: (i, off + j))


def _rms(x, g):
    x = x.astype(F32)
    return x * lax.rsqrt(jnp.mean(x * x, axis=-1, keepdims=True) + EPS) * g


def _sigmoid(x):
    return jax.nn.sigmoid(x)


def _silu(x):
    return x * _sigmoid(x)


def _softplus(x):
    return jnp.maximum(x, 0.0) + jnp.log1p(jnp.exp(-jnp.abs(x)))


def _rms_fwd(x, g, name):
    t, d = x.shape
    tt = _tile(t, (256, 128))
    return _tmap(_rms, (1, t // tt), [(x, *_rows(d, tt)), (g, (1, d), lambda j, i: (0, 0))],
                 [((t, d), BF16, *_rows(d, tt), None)], name)[0]


def _rms_bwd(x, g, dn, dres, name):
    t, d = x.shape
    tt = _tile(t, (256, 128))

    def fn(x, g, dn, dres):
        x, dn = x.astype(F32), dn.astype(F32)
        r = lax.rsqrt(jnp.mean(x * x, axis=-1, keepdims=True) + EPS)
        xn = x * r
        t = dn * g
        dx = r * (t - xn * jnp.mean(t * xn, axis=-1, keepdims=True))
        return dres + dx, jnp.sum(dn * xn, axis=0, keepdims=True)

    return _tmap(fn, (1, t // tt),
                 [(x, *_rows(d, tt)), (g, (1, d), lambda j, i: (0, 0)), (dn, *_rows(d, tt)), (dres, *_rows(d, tt))],
                 [((t, d), F32, *_rows(d, tt), None), ((1, d), F32, (1, d), lambda j, i: (0, 0), "inner")], name)


def _swiglu(a, b):
    return _silu(a) * b


def _ffn_in_act(n, w_in_t, name, after):
    t, d = n.shape
    tm, tn = _tile(t, (512, 256, 128)), D_FF // 2
    half_blocks = D_FF // tn
    n_in = 3 + (after is not None)

    def body(*refs):
        n_ref, wa_ref, wb_ref = refs[:3]
        a_ref, b_ref, s_ref = refs[n_in:]
        x = n_ref[...]
        a = _bdot_raw(x, wa_ref[...], "nt").astype(BF16)
        b = _bdot_raw(x, wb_ref[...], "nt").astype(BF16)
        a_ref[...] = a
        b_ref[...] = b
        s_ref[...] = _swiglu(a.astype(F32), b.astype(F32)).astype(BF16)

    out = pl.BlockSpec((tm, tn), lambda i, j: (i, j))
    in_specs = [pl.BlockSpec((tm, d), lambda i, j: (i, 0)), pl.BlockSpec((tn, d), lambda i, j: (j, 0)),
                pl.BlockSpec((tn, d), lambda i, j: (j + half_blocks, 0))]
    args = [n, w_in_t, w_in_t]
    if after is not None:
        in_specs.append(pl.BlockSpec(TOKEN, lambda i, j: (0, 0)))
        args.append(after)
    return pl.pallas_call(
        body, grid=(t // tm, half_blocks), in_specs=in_specs, out_specs=[out, out, out], out_shape=[SDS((t, D_FF), BF16)] * 3,
        name=name, compiler_params=_params(2),
    )(*args)


def _swiglu_bwd(a, b, ds, name):
    t = a.shape[0]
    tt = _tile(t, (128,))

    def fn(a, b, ds):
        _, vjp = jax.vjp(_swiglu, a.astype(F32), b.astype(F32))
        da, db = vjp(ds.astype(F32))
        return jnp.concatenate([da, db], axis=1)

    r = _rows(D_FF, tt)
    return _tmap(fn, (1, t // tt), [(a, *r), (b, *r), (ds, *r)], [((t, 2 * D_FF), BF16, *_rows(2 * D_FF, tt), None)], name)[0]


def _ffn_fwd(h, g, weights, tag):
    n = _rms_fwd(h, g, tag + "_norm")
    w_in_t, w_out, after = weights(n)
    a, b, s = _ffn_in_act(n, w_in_t, tag + "_in", after)
    out = _mm(s, w_out, "nn", F32, tag + "_out", res=h, alpha=0.5)
    return out, (n, a, b, s, w_in_t, w_out)


def _ffn_bwd(h, g, saved, dout, tag, links):
    n, a, b, s, w_in_t, w_out = saved
    sent = links.send({tag + "_w_out": _mm(s, dout, "tn", BF16, tag + "_dw_out", alpha=0.5)})
    ds = _mm(dout, w_out, "nt", BF16, tag + "_ds", alpha=0.5, after=sent)
    dab = _swiglu_bwd(a, b, ds, tag + "_dact")
    sent = links.send({tag + "_w_in": _mm(dab, n, "tn", BF16, tag + "_dw_in")})
    dn = _mm(dab, w_in_t, "nn", F32, tag + "_dn", after=sent)
    return _rms_bwd(h, g, dn, dout, tag + "_dnorm")


def _chunk_sum_matrix(n, chunk, transpose=False):
    row = lax.broadcasted_iota(jnp.int32, (n, n), 0)
    col = lax.broadcasted_iota(jnp.int32, (n, n), 1)
    if transpose:
        row, col = col, row
    return jnp.where(jnp.logical_and(col <= row, row // chunk == col // chunk), 1.0, 0.0).astype(F32)


def _hgrn_gates(hq, hf, lbl):
    lb = _sigmoid(lbl[0:1, :] - lbl[1:2, :])
    sg = _sigmoid(hf)
    f = lb + (1.0 - lb) * sg
    q = _silu(hq) * HEAD ** -0.5
    k = (1.0 - lb) * (1.0 - sg)
    return q, k, jnp.log(f)


def _hgrn_prep_fwd(proj, lbl):
    t = proj.shape[0]
    tt, ft = _tile(t, (256, 128)), 512

    def fn(hq, hf, lbl):
        q, k, log_f = _hgrn_gates(hq, hf, lbl)
        return q, k, _hdot_raw(_chunk_sum_matrix(tt, HG_CHUNK), log_f)

    o = ((t, D_MODEL), F32, *_rows(ft, tt), None)
    return _tmap(fn, (D_MODEL // ft, t // tt),
                 [(proj, *_rows(ft, tt, COL_HQ * HEAD // ft)), (proj, *_rows(ft, tt, COL_HF * HEAD // ft)), (lbl, (2, ft), lambda j, i: (0, j))],
                 [o, o, o], "hgrn_prep")


def _hgrn_prep_bwd(proj, lbl, dq, dk, db):
    t = proj.shape[0]
    tt, ft = _tile(t, (256, 128)), 512

    def fn(hq, hf, lbl, dq, dk, db):
        dlog_f = _hdot_raw(_chunk_sum_matrix(tt, HG_CHUNK, transpose=True), db)
        _, vjp = jax.vjp(_hgrn_gates, hq, hf, lbl)
        return vjp((dq, dk, dlog_f))

    o = ((t, D_MODEL), BF16, *_rows(ft, tt), None)
    r = _rows(ft, tt)
    return _tmap(fn, (D_MODEL // ft, t // tt),
                 [(proj, *_rows(ft, tt, COL_HQ * HEAD // ft)), (proj, *_rows(ft, tt, COL_HF * HEAD // ft)), (lbl, (2, ft), lambda j, i: (0, j)),
                  (dq, *r), (dk, *r), (db, *r)],
                 [o, o, ((2, D_MODEL), F32, (2, ft), lambda j, i: (0, j), "inner")], "hgrn_prep_bwd")


@functools.partial(jax.custom_vjp, nondiff_argnums=(1,))
def _roll_rows(x, d):
    return pltpu.roll(x, d, 0)


def _roll_rows_fwd(x, d):
    return pltpu.roll(x, d, 0), None


def _roll_rows_bwd(d, _, ct):
    return (pltpu.roll(ct, ct.shape[0] - d, 0),)


_roll_rows.defvjp(_roll_rows_fwd, _roll_rows_bwd)


def _hgrn_chunks(q, k, v, b, st):
    n = q[0].shape[0]
    half = n // 2
    srow = lax.broadcasted_iota(jnp.int32, (half, HEAD), 0)
    inter = _each(lambda q, b, st: _bdot(q * jnp.exp(b), st, "nt"), q, b, st)

    def below_scores(q, k, b):
        ref = b[half:half + 1, :]
        return _bdot(q[half:] * jnp.exp(jnp.minimum(b[half:] - ref, 0.0)), k[:half] * jnp.exp(jnp.minimum(ref - b[:half], 0.0)), "nt")

    below = _each(lambda a, v: _bdot(a, v[:half], "nn"), _each(below_scores, q, k, b), v)

    def diagonal(q, k, v, b):
        blocks = []
        for lo in (0, half):
            qb, kb, vb, bb = (a[lo:lo + half] for a in (q, k, v, b))
            o = jnp.sum(qb * kb, axis=1, keepdims=True) * vb
            for d in range(1, half):
                kr, vr, br = _roll_rows(kb, d), _roll_rows(vb, d), _roll_rows(bb, d)
                a = jnp.sum(qb * kr * jnp.exp(jnp.minimum(bb - br, 0.0)), axis=1, keepdims=True)
                o = o + jnp.where(srow[:, :1] >= d, a, 0.0) * vr
            blocks.append(o)
        return jnp.concatenate(blocks, axis=0)

    diag = _each(diagonal, q, k, v, b)
    o = _each(lambda inter, diag, below: inter + diag + jnp.concatenate([jnp.zeros_like(below), below], axis=0), inter, diag, below)

    def new_state(k, v, b, st):
        bend = b[n - 1:n, :]
        return st * jnp.exp(bend) + _bdot(v, k * jnp.exp(bend - b), "tn")

    return o, _each(new_state, k, v, b, st)


HG_GROUP = 8
HG_PER = GDN_CHUNK // HG_CHUNK


def _hgrn_rec_fwd(q, k, proj, b):
    t = q.shape[0]
    nc = t // GDN_CHUNK
    blk = (GDN_CHUNK, HG_GROUP * HEAD)
    im = lambda h, c: (c, h)

    def body(q_ref, k_ref, v_ref, b_ref, o_ref, hs_ref, st_ref):
        @pl.when(pl.program_id(1) == 0)
        def _():
            st_ref[...] = jnp.zeros_like(st_ref)

        heads = range(HG_GROUP)
        for j in range(HG_PER):
            sl = pl.ds(HG_CHUNK * j, HG_CHUNK)
            st = tuple(st_ref[g] for g in heads)
            o, st_new = _hgrn_chunks(*[tuple(r[sl, _head_lanes(g)] for g in heads) for r in (q_ref, k_ref, v_ref, b_ref)], st)
            for g in heads:
                hs_ref[g, j] = st[g]
                o_ref[sl, _head_lanes(g)] = o[g]
                st_ref[g] = st_new[g]

    return pl.pallas_call(
        body, grid=(HG_HEADS // HG_GROUP, nc),
        in_specs=[pl.BlockSpec(blk, im), pl.BlockSpec(blk, im), pl.BlockSpec(blk, lambda h, c: (c, COL_HI // HG_GROUP + h)), pl.BlockSpec(blk, im)],
        out_specs=[pl.BlockSpec(blk, im), pl.BlockSpec((HG_GROUP, HG_PER, HEAD, HEAD), lambda h, c: (h, c, 0, 0))],
        out_shape=[SDS((t, D_MODEL), F32), SDS((HG_HEADS, nc * HG_PER, HEAD, HEAD), F32)],
        scratch_shapes=[pltpu.VMEM((HG_GROUP, HEAD, HEAD), F32)], name="hgrn_rec", compiler_params=_params(2),
    )(q, k, proj, b)


def _hgrn_rec_bwd(q, k, proj, b, hs, do):
    t = q.shape[0]
    nc = t // GDN_CHUNK
    blk = (GDN_CHUNK, HG_GROUP * HEAD)
    im = lambda h, c: (nc - 1 - c, h)

    def body(q_ref, k_ref, v_ref, b_ref, hs_ref, do_ref, dq_ref, dk_ref, dv_ref, db_ref, dst_ref):
        @pl.when(pl.program_id(1) == 0)
        def _():
            dst_ref[...] = jnp.zeros_like(dst_ref)

        heads = range(HG_GROUP)
        for j in reversed(range(HG_PER)):
            sl = pl.ds(HG_CHUNK * j, HG_CHUNK)
            _, vjp = jax.vjp(_hgrn_chunks, *[tuple(r[sl, _head_lanes(g)] for g in heads) for r in (q_ref, k_ref, v_ref, b_ref)],
                             tuple(hs_ref[g, j] for g in heads))
            dq, dk, dv, db, dst = vjp((tuple(do_ref[sl, _head_lanes(g)] for g in heads), tuple(dst_ref[g] for g in heads)))
            for g in heads:
                ln = _head_lanes(g)
                dq_ref[sl, ln] = dq[g]
                dk_ref[sl, ln] = dk[g]
                dv_ref[sl, ln] = dv[g].astype(dv_ref.dtype)
                db_ref[sl, ln] = db[g]
                dst_ref[g] = dst[g]

    spec = pl.BlockSpec(blk, im)
    return pl.pallas_call(
        body, grid=(HG_HEADS // HG_GROUP, nc),
        in_specs=[spec, spec, pl.BlockSpec(blk, lambda h, c: (nc - 1 - c, COL_HI // HG_GROUP + h)), spec,
                  pl.BlockSpec((HG_GROUP, HG_PER, HEAD, HEAD), lambda h, c: (h, nc - 1 - c, 0, 0)), spec],
        out_specs=[spec, spec, spec, spec],
        out_shape=[SDS((t, D_MODEL), F32), SDS((t, D_MODEL), F32), SDS((t, D_MODEL), BF16), SDS((t, D_MODEL), F32)],
        scratch_shapes=[pltpu.VMEM((HG_GROUP, HEAD, HEAD), F32)], name="hgrn_rec_bwd", compiler_params=_params(2),
    )(q, k, proj, b, hs, do)


def _shift_down(x, d):
    if d == 0:
        return x
    row = lax.broadcasted_iota(jnp.int32, x.shape, 0)
    return jnp.where(row >= d, pltpu.roll(x, d, 0), 0.0)


def _shift_up(x, d):
    if d == 0:
        return x
    n = x.shape[0]
    row = lax.broadcasted_iota(jnp.int32, x.shape, 0)
    return jnp.where(row < n - d, pltpu.roll(x, n - d, 0), 0.0)


def _conv_fwd(proj, conv_w):
    t = proj.shape[0]
    width = 2 * D_MODEL + 2 * D_MODEL

    def body(x_ref, w_ref, c_ref, y_ref):
        x, w = x_ref[...], w_ref[...]
        y = w[CONV_K - 1:CONV_K, :] * x
        for j in range(CONV_K - 1):
            y = y + w[j:j + 1, :] * _shift_down(x, CONV_K - 1 - j)
        y_ref[...] = y
        c_ref[...] = _silu(y)

    out = pl.BlockSpec((t, HEAD), lambda j: (0, j))
    return pl.pallas_call(
        body, grid=(width // HEAD,),
        in_specs=[pl.BlockSpec((t, HEAD), lambda j: (0, COL_GQ + j)), pl.BlockSpec((CONV_K, HEAD), lambda j: (0, j))],
        out_specs=[out, out], out_shape=[SDS((t, width), F32), SDS((t, width), F32)],
        name="gdn_conv", compiler_params=_params(1),
    )(proj, conv_w)


def _conv_bwd(proj, conv_w, y, dc_qk, dc_v):
    t = proj.shape[0]
    n_qk = dc_qk.shape[1] // HEAD
    width = dc_qk.shape[1] + dc_v.shape[1]

    def body(x_ref, w_ref, y_ref, dqk_ref, dv_ref, dx_ref, dw_ref):
        x, w, y = x_ref[...], w_ref[...], y_ref[...]
        sg = _sigmoid(y)
        dc = jnp.where(pl.program_id(0) < n_qk, dqk_ref[...], dv_ref[...])
        dy = dc * (sg * (1.0 + y * (1.0 - sg)))
        ahead = [_shift_up(dy, CONV_K - 1 - j) for j in range(CONV_K)]
        dx = w[0:1, :] * ahead[0]
        for j in range(1, CONV_K):
            dx = dx + w[j:j + 1, :] * ahead[j]
        dx_ref[...] = dx.astype(dx_ref.dtype)
        dw_ref[...] = jnp.concatenate([jnp.sum(x * ahead[j], axis=0, keepdims=True) for j in range(CONV_K)], axis=0)

    blk = pl.BlockSpec((t, HEAD), lambda j: (0, j))
    return pl.pallas_call(
        body, grid=(width // HEAD,),
        in_specs=[pl.BlockSpec((t, HEAD), lambda j: (0, COL_GQ + j)), pl.BlockSpec((CONV_K, HEAD), lambda j: (0, j)), blk,
                  pl.BlockSpec((t, HEAD), lambda j: (0, jnp.minimum(j, n_qk - 1))), pl.BlockSpec((t, HEAD), lambda j: (0, jnp.maximum(j - n_qk, 0)))],
        out_specs=[blk, pl.BlockSpec((CONV_K, HEAD), lambda j: (0, j))],
        out_shape=[SDS((t, width), BF16), SDS((CONV_K, width), F32)],
        name="gdn_conv_bwd", compiler_params=_params(1),
    )(proj, conv_w, y, dc_qk, dc_v)


def _l2norm(x, scale):
    return x * lax.rsqrt(jnp.sum(x * x, axis=-1, keepdims=True) + EPS) * scale


def _head(a, h):
    return a[:, h * HEAD:(h + 1) * HEAD]


def _qk_scale(h):
    return HEAD ** -0.5 if h < GDN_QK_HEADS else 1.0


def _qk_norm_fwd(c):
    t = c.shape[0]
    tt = _tile(t, (256, 128))
    width = 2 * D_MODEL

    def fn(x):
        return jnp.concatenate([_l2norm(_head(x, h), _qk_scale(h)) for h in range(2 * GDN_QK_HEADS)], axis=1)

    return _tmap(fn, (1, t // tt), [(c, *_rows(width, tt))], [((t, width), F32, *_rows(width, tt), None)], "gdn_qk_norm")[0]


def _qk_norm_bwd(c, dq_rep, dk_rep):
    t = c.shape[0]
    tt = _tile(t, (256, 128))
    width = 2 * D_MODEL

    def fn(x, dq2, dk2):
        out = []
        for h in range(2 * GDN_QK_HEADS):
            d2, hh = (dq2, h) if h < GDN_QK_HEADS else (dk2, h - GDN_QK_HEADS)
            _, vjp = jax.vjp(lambda x: _l2norm(x, _qk_scale(h)), _head(x, h))
            out.append(vjp(_head(d2, 2 * hh) + _head(d2, 2 * hh + 1))[0])
        return jnp.concatenate(out, axis=1)

    r = _rows(width, tt)
    return _tmap(fn, (1, t // tt), [(c, *r), (dq_rep, *r), (dk_rep, *r)], [((t, width), F32, *r, None)], "gdn_qk_norm_bwd")[0]


def _gdn_gates(x, alog, dtb):
    return -jnp.exp(alog) * _softplus(x + dtb), _sigmoid(x)


def _gates_fwd(pab, alog, dtb):
    t = pab.shape[0]
    tt = _tile(t, (256, 128))

    def fn(x, alog, dtb):
        g, beta = _gdn_gates(x, alog, dtb)
        lane = lax.broadcasted_iota(jnp.int32, g.shape, 1)
        return jnp.where(lane < GDN_V_HEADS, _hdot_raw(_chunk_sum_matrix(tt, GDN_CHUNK), g), beta).T

    p = (alog, (1, HEAD), lambda j, i: (0, 0)), (dtb, (1, HEAD), lambda j, i: (0, 0))
    return _tmap(fn, (1, t // tt), [(pab, *_rows(HEAD, tt)), *p], [((HEAD, t), F32, (HEAD, tt), lambda j, i: (0, i), None)], "gdn_gates")[0]


def _gates_bwd(pab, alog, dtb, dout_t):
    t = pab.shape[0]
    tt = _tile(t, (256, 128))

    def fn(x, alog, dtb, dout_t):
        dout = dout_t.T
        lane = lax.broadcasted_iota(jnp.int32, dout.shape, 1)
        dgam = jnp.where(lane < GDN_V_HEADS, dout, 0.0)
        dbeta = jnp.where(jnp.logical_and(lane >= GDN_V_HEADS, lane < 2 * GDN_V_HEADS), dout, 0.0)
        dg = _hdot_raw(_chunk_sum_matrix(tt, GDN_CHUNK, transpose=True), dgam)
        _, vjp = jax.vjp(_gdn_gates, x, alog, dtb)
        return vjp((dg, dbeta))

    p = (alog, (1, HEAD), lambda j, i: (0, 0)), (dtb, (1, HEAD), lambda j, i: (0, 0))
    acc = ((1, HEAD), F32, (1, HEAD), lambda j, i: (0, 0), "inner")
    return _tmap(fn, (1, t // tt), [(pab, *_rows(HEAD, tt)), *p, (dout_t, (HEAD, tt), lambda j, i: (0, i))],
                 [((t, HEAD), BF16, *_rows(HEAD, tt), None), acc, acc], "gdn_gates_bwd")


def _split_bf16(x):
    hi = x.astype(BF16)
    return hi, (x - hi.astype(F32)).astype(BF16)


def _dot3(a, b):
    (ah, al), (bh, bl) = a, b
    return _bdot_raw(ah, bh, "nn") + (_bdot_raw(ah, bl, "nn") + _bdot_raw(al, bh, "nn"))


def _each(fn, *lists):
    return tuple(fn(*xs) for xs in zip(*lists))


def _unit_lower_inverses_raw(a):
    n = a[0].shape[0]
    row = lax.broadcasted_iota(jnp.int32, (n, n), 0)
    col = lax.broadcasted_iota(jnp.int32, (n, n), 1)
    eye = jnp.where(row == col, 1.0, 0.0).astype(F32)
    p = _each(lambda a: eye - a, a)
    x = _each(_split_bf16, a)
    m = 2
    while m < n:
        x = _each(_split_bf16, _each(_dot3, x, x))
        p = _each(lambda p, x: p + _bdot_raw(p, x[0], "nn"), p, x)
        m *= 2
    return p


@jax.custom_vjp
def _unit_lower_inverses(a, known):
    return _unit_lower_inverses_raw(a) if known is None else known


def _uli_fwd(a, known):
    inv = _unit_lower_inverses(a, known)
    return inv, (inv, known)


def _uli_bwd(res, ct):
    inv, known = res
    right = _each(lambda ct, inv: _bdot_raw(ct, inv, "nt"), ct, inv)
    da = _each(lambda inv, r: -_bdot_raw(inv, r, "tn"), inv, right)
    return da, (None if known is None else _each(jnp.zeros_like, known))


_unit_lower_inverses.defvjp(_uli_fwd, _uli_bwd)


def _gdn_chunks(q, k, v, beta_rows, gam_rows, s, inv_known=None):
    n = q[0].shape[0]
    heads = range(len(q))
    row = lax.broadcasted_iota(jnp.int32, (n, n), 0)
    col = lax.broadcasted_iota(jnp.int32, (n, n), 1)
    beta_cols, gam_cols = beta_rows.T, gam_rows.T
    beta = tuple(beta_cols[:, g:g + 1] for g in heads)
    gam = tuple(gam_cols[:, g:g + 1] for g in heads)
    gam_row = tuple(gam_rows[g:g + 1, :] for g in heads)
    decay = _each(lambda gam, gam_row: jnp.where(row >= col, jnp.exp(jnp.minimum(gam - gam_row, 0.0)), 0.0), gam, gam_row)
    kb = _each(lambda k, beta: k * beta, k, beta)
    a = _each(lambda kb, k, decay: jnp.where(row > col, _bdot(kb, k, "nt") * decay, 0.0), kb, k, decay)
    inv = _unit_lower_inverses(a, inv_known)
    eg = _each(jnp.exp, gam)
    u = _each(lambda inv, v, beta: _bdot(inv, v * beta, "nn"), inv, v, beta)
    w = _each(lambda inv, kb, eg: _bdot(inv, kb * eg, "nn"), inv, kb, eg)
    qk = _each(lambda q, k, decay: _bdot(q, k, "nt") * decay, q, k, decay)
    v_new = _each(lambda u, w, s: u - _bdot(w, s, "nn"), u, w, s)
    o_state = _each(lambda q, eg, s: _bdot(q * eg, s, "nn"), q, eg, s)
    o = _each(lambda o_state, qk, v_new: o_state + _bdot(qk, v_new, "nn"), o_state, qk, v_new)
    gend = _each(lambda gam: gam[n - 1:n, :], gam)
    s_new = _each(lambda s, k, gam, gend, v_new: s * jnp.exp(gend) + _bdot(k * jnp.exp(gend - gam), v_new, "tn"), s, k, gam, gend, v_new)
    return o, s_new, inv


GDN_GROUP = 16


def _gdn_specs(nc, rev):
    cc = (lambda c: nc - 1 - c) if rev else (lambda c: c)
    grp = GDN_GROUP
    q = pl.BlockSpec((GDN_CHUNK, grp // 2 * HEAD), lambda h, c: (cc(c), h))
    k = pl.BlockSpec((GDN_CHUNK, grp // 2 * HEAD), lambda h, c: (cc(c), 2 * GDN_QK_HEADS // grp + h))
    v = pl.BlockSpec((GDN_CHUNK, grp * HEAD), lambda h, c: (cc(c), 2 * GDN_QK_HEADS // grp + h))
    o = pl.BlockSpec((GDN_CHUNK, grp * HEAD), lambda h, c: (cc(c), h))
    rw = pl.BlockSpec((grp, None, 1, GDN_CHUNK), lambda h, c: (h, cc(c), 0, 0))
    st = pl.BlockSpec((grp, None, HEAD, HEAD), lambda h, c: (h, cc(c), 0, 0))
    inv = pl.BlockSpec((grp, None, GDN_CHUNK, GDN_CHUNK), lambda h, c: (h, cc(c), 0, 0))
    return q, k, v, o, rw, st, inv


def _head_lanes(g, per=1):
    return pl.ds((g // per) * HEAD, HEAD)


def _gdn_rec_fwd(qk, c, beta_row, gam_row):
    t = qk.shape[0]
    nc = t // GDN_CHUNK
    q, k, v, o, rw, st, inv = _gdn_specs(nc, False)

    def body(q_ref, k_ref, v_ref, be_ref, gr_ref, o_ref, ss_ref, inv_ref, s_ref):
        @pl.when(pl.program_id(1) == 0)
        def _():
            s_ref[...] = jnp.zeros_like(s_ref)

        heads = range(GDN_GROUP)
        s = tuple(s_ref[g] for g in heads)
        out, s_new, inv_c = _gdn_chunks(
            tuple(q_ref[:, _head_lanes(g, 2)] for g in heads), tuple(k_ref[:, _head_lanes(g, 2)] for g in heads),
            tuple(v_ref[:, _head_lanes(g)] for g in heads), be_ref[:, 0, :], gr_ref[:, 0, :], s)
        for g in heads:
            ss_ref[g] = s[g]
            o_ref[:, _head_lanes(g)] = out[g]
            inv_ref[g] = inv_c[g]
            s_ref[g] = s_new[g]

    return pl.pallas_call(
        body, grid=(GDN_V_HEADS // GDN_GROUP, nc), in_specs=[q, k, v, rw, rw], out_specs=[o, st, inv],
        out_shape=[SDS((t, 2 * D_MODEL), F32), SDS((GDN_V_HEADS, nc, HEAD, HEAD), F32), SDS((GDN_V_HEADS, nc, GDN_CHUNK, GDN_CHUNK), F32)],
        scratch_shapes=[pltpu.VMEM((GDN_GROUP, HEAD, HEAD), F32)], name="gdn_rec", compiler_params=_params(2),
    )(qk, qk, c, beta_row, gam_row)


def _gdn_rec_bwd(qk, c, beta_row, gam_row, ss, invs, do):
    t = qk.shape[0]
    nc = t // GDN_CHUNK
    q, k, v, o, rw, st, inv = _gdn_specs(nc, True)

    def body(q_ref, k_ref, v_ref, be_ref, gr_ref, ss_ref, inv_ref, do_ref,
             dq_ref, dk_ref, dv_ref, dbe_ref, dgr_ref, ds_ref):
        @pl.when(pl.program_id(1) == 0)
        def _():
            ds_ref[...] = jnp.zeros_like(ds_ref)

        heads = range(GDN_GROUP)
        _, vjp = jax.vjp(
            _gdn_chunks,
            tuple(q_ref[:, _head_lanes(g, 2)] for g in heads), tuple(k_ref[:, _head_lanes(g, 2)] for g in heads),
            tuple(v_ref[:, _head_lanes(g)] for g in heads), be_ref[:, 0, :], gr_ref[:, 0, :],
            tuple(ss_ref[g] for g in heads), tuple(inv_ref[g] for g in heads))
        no_inv_ct = tuple(jnp.zeros((GDN_CHUNK, GDN_CHUNK), F32) for g in heads)
        dq, dk, dv, dbe, dgr, ds, _ = vjp((tuple(do_ref[:, _head_lanes(g)] for g in heads), tuple(ds_ref[g] for g in heads), no_inv_ct))
        for g in heads:
            dq_ref[:, _head_lanes(g)] = dq[g]
            dk_ref[:, _head_lanes(g)] = dk[g]
            dv_ref[:, _head_lanes(g)] = dv[g]
            ds_ref[g] = ds[g]
        dbe_ref[:, 0, :] = dbe
        dgr_ref[:, 0, :] = dgr

    wide = SDS((t, 2 * D_MODEL), F32)
    rowshape = SDS((GDN_V_HEADS, nc, 1, GDN_CHUNK), F32)
    return pl.pallas_call(
        body, grid=(GDN_V_HEADS // GDN_GROUP, nc), in_specs=[q, k, v, rw, rw, st, inv, o], out_specs=[o, o, o, rw, rw],
        out_shape=[wide, wide, wide, rowshape, rowshape],
        scratch_shapes=[pltpu.VMEM((GDN_GROUP, HEAD, HEAD), F32)], name="gdn_rec_bwd", compiler_params=_params(2),
    )(qk, qk, c, beta_row, gam_row, ss, invs, do)


def _gated_norm(o, gate, w):
    return _rms(o, w) * _silu(gate)


def _post_fwd(o, proj, col_off, w, name):
    t, width = o.shape
    tt = _tile(t, (256, 128))

    def fn(o, gate, w):
        return jnp.concatenate([_gated_norm(_head(o, h), _head(gate, h), w) for h in range(width // HEAD)], axis=1)

    return _tmap(fn, (1, t // tt),
                 [(o, *_rows(width, tt)), (proj, *_rows(width, tt, col_off * HEAD // width)), (w, (1, HEAD), lambda j, i: (0, 0))],
                 [((t, width), BF16, *_rows(width, tt), None)], name)[0]


def _post_bwd(o, proj, col_off, w, dout, name):
    t, width = o.shape
    tt = _tile(t, (256, 128))

    def fn(o, gate, w, dout):
        do, dgate, dw = [], [], jnp.zeros((1, HEAD), F32)
        for h in range(width // HEAD):
            _, vjp = jax.vjp(_gated_norm, _head(o, h), _head(gate, h), w)
            a, b, c = vjp(_head(dout, h))
            do.append(a)
            dgate.append(b)
            dw = dw + c
        return jnp.concatenate(do, axis=1), jnp.concatenate(dgate, axis=1), dw

    r = _rows(width, tt)
    return _tmap(fn, (1, t // tt),
                 [(o, *r), (proj, *_rows(width, tt, col_off * HEAD // width)), (w, (1, HEAD), lambda j, i: (0, 0)), (dout, *r)],
                 [((t, width), F32, *r, None), ((t, width), BF16, *r, None), ((1, HEAD), F32, (1, HEAD), lambda j, i: (0, 0), "inner")], name)


def _merge(gate_h, gate_g, yh, yg):
    return _sigmoid(gate_h) * yh + _sigmoid(gate_g) * yg


def _merge_fwd(proj, yh, yg):
    t = yh.shape[0]
    tt, ft = _tile(t, (256, 128)), 512
    r = _rows(ft, tt)
    return _tmap(_merge, (D_MODEL // ft, t // tt),
                 [(proj, *_rows(ft, tt, COL_GATE_H * HEAD // ft)), (proj, *_rows(ft, tt, COL_GATE_G * HEAD // ft)), (yh, *r), (yg, *r)],
                 [((t, D_MODEL), BF16, *r, None)], "merge")[0]


def _merge_bwd(proj, yh, yg, dy):
    t = yh.shape[0]
    tt, ft = _tile(t, (256, 128)), 512
    r = _rows(ft, tt)

    def fn(gate_h, gate_g, yh, yg, dy):
        _, vjp = jax.vjp(_merge, gate_h, gate_g, yh, yg)
        return vjp(dy)

    o = ((t, D_MODEL), BF16, *r, None)
    return _tmap(fn, (D_MODEL // ft, t // tt),
                 [(proj, *_rows(ft, tt, COL_GATE_H * HEAD // ft)), (proj, *_rows(ft, tt, COL_GATE_G * HEAD // ft)), (yh, *r), (yg, *r), (dy, *r)],
                 [o, o, o, o], "merge_bwd")


def _loss_head(h, target, g):
    t, d = h.shape
    tt = _tile(t, (256, 128))

    def fn(h, target, g):
        def f(h, g):
            err = _rms(h, g) - target
            return 0.5 * jnp.sum(jnp.mean(err * err, axis=-1))

        loss, (dh, dg) = jax.value_and_grad(f, (0, 1))(h, g)
        return dh, dg, jnp.full((1, HEAD), loss, F32)

    return _tmap(fn, (1, t // tt), [(h, *_rows(d, tt)), (target, *_rows(d, tt)), (g, (1, d), lambda j, i: (0, 0))],
                 [((t, d), F32, *_rows(d, tt), None), ((1, d), F32, (1, d), lambda j, i: (0, 0), "inner"),
                  ((1, HEAD), F32, (1, HEAD), lambda j, i: (0, 0), "inner")], "loss_head")


def _mixer_fwd(h, p, links):
    t = h.shape[0]
    nc = t // GDN_CHUNK
    u = _rms_fwd(h, p["mix_norm"], "mix_norm")
    w = {n: links.weight(n, h) for n in ("w_in_t", "w_in_b_t", "w_in_ab_t", "conv_w")}
    proj = _mm(u, w["w_in_t"], "nt", F32, "mix_in", after=links.started, b_rows=SCALAR_ROWS)
    proj_b = _mm(u, w["w_in_b_t"], "nt", F32, "mix_in_b")
    pab = _mm(u, w["w_in_ab_t"], "nt", F32, "mix_in_ab")
    qh, kh, bh = _hgrn_prep_fwd(proj, p["lbl"])
    oh, hs = _hgrn_rec_fwd(qh, kh, proj, bh)
    c, conv_y = _conv_fwd(proj, w["conv_w"])
    qk = _qk_norm_fwd(c)
    gates_t = _gates_fwd(pab, p["alog"], p["dtb"])
    gam_row = gates_t[:GDN_V_HEADS].reshape(GDN_V_HEADS, nc, 1, GDN_CHUNK)
    beta_row = gates_t[GDN_V_HEADS:2 * GDN_V_HEADS].reshape(GDN_V_HEADS, nc, 1, GDN_CHUNK)
    og, ss, invs = _gdn_rec_fwd(qk, c, beta_row, gam_row)
    ohn = _post_fwd(oh, proj, COL_HG, p["hgrn_out_norm"], "hgrn_out")
    ogn = _post_fwd(og, proj_b, COL_GZ, p["gdn_out_norm"], "gdn_out")
    w.update({n: links.weight(n, ogn) for n in ("w_branch_hgrn", "w_branch_gdn", "w_out")})
    yh = _mm(ohn, w["w_branch_hgrn"], "nn", BF16, "branch_hgrn")
    yg = _mm(ogn, w["w_branch_gdn"], "nn", BF16, "branch_gdn")
    y = _merge_fwd(proj_b, yh, yg)
    out = _mm(y, w["w_out"], "nn", F32, "mix_out", res=h)
    saved = (w, u, proj, proj_b, pab, qh, kh, bh, oh, hs, c, conv_y, qk, beta_row, gam_row, og, ss, invs, ohn, ogn, yh, yg, y)
    return out, saved


def _mixer_bwd(h, p, links, saved, dout):
    (w, u, proj, proj_b, pab, qh, kh, bh, oh, hs, c, conv_y, qk, beta_row, gam_row, og, ss, invs, ohn, ogn, yh, yg, y) = saved
    t = h.shape[0]
    grads = {}
    dw_out = _mm(y, dout, "tn", BF16, "mix_out_dw")
    dy = _mm(dout, w["w_out"], "nt", F32, "mix_out_dx")
    dgate_h, dgate_g, dyh, dyg = _merge_bwd(proj_b, yh, yg, dy)
    dw_bh = _mm(ohn, dyh, "tn", BF16, "branch_hgrn_dw")
    dw_bg = _mm(ogn, dyg, "tn", BF16, "branch_gdn_dw")
    sent = links.send({"w_out": dw_out, "w_branch_hgrn": dw_bh, "w_branch_gdn": dw_bg})
    dohn = _mm(dyh, w["w_branch_hgrn"], "nt", F32, "branch_hgrn_dx", after=sent)
    dogn = _mm(dyg, w["w_branch_gdn"], "nt", F32, "branch_gdn_dx")
    doh, dhg, grads["hgrn_out_norm"] = _post_bwd(oh, proj, COL_HG, p["hgrn_out_norm"], dohn, "hgrn_out_bwd")
    dog, dgz, grads["gdn_out_norm"] = _post_bwd(og, proj_b, COL_GZ, p["gdn_out_norm"], dogn, "gdn_out_bwd")
    dqh, dkh, dhi, dbh = _hgrn_rec_bwd(qh, kh, proj, bh, hs, doh)
    dhq, dhf, grads["lbl"] = _hgrn_prep_bwd(proj, p["lbl"], dqh, dkh, dbh)
    dqv, dkv, dcv, dbeta_row, dgam_row = _gdn_rec_bwd(qk, c, beta_row, gam_row, ss, invs, dog)
    dcqk = _qk_norm_bwd(c, dqv, dkv)
    dxin, grads["conv_w"] = _conv_bwd(proj, w["conv_w"], conv_y, dcqk, dcv)
    dgates_t = jnp.concatenate([dgam_row.reshape(GDN_V_HEADS, t), dbeta_row.reshape(GDN_V_HEADS, t),
                                jnp.zeros((HEAD - 2 * GDN_V_HEADS, t), F32)], axis=0)
    dpab, grads["alog"], grads["dtb"] = _gates_bwd(pab, p["alog"], p["dtb"], dgates_t)
    front, back = [dhq, dhf, dhi, dhg, dxin], [dgz, dgate_h, dgate_g]
    dw_front = [_mm(d, u, "tn", BF16, "mix_in_dw_%d" % i) for i, d in enumerate(front)]
    dw_back = [_mm(d, u, "tn", BF16, "mix_in_b_dw_%d" % i) for i, d in enumerate(back)]
    dw_ab_t = _mm(dpab, u, "tn", BF16, "mix_in_ab_dw")
    sent = links.send({"w_in": jnp.concatenate(dw_front + [dw_ab_t[:N_SCALAR]] + dw_back, axis=0)})
    du = _mm_pieces(front, w["w_in_t"], "mix_in_dx", after=sent)
    du = _mm_pieces(back, w["w_in_b_t"], "mix_in_b_dx", res=du)
    du = _mm(dpab, w["w_in_ab_t"], "nn", F32, "mix_in_ab_dx", res=du)
    dh, grads["mix_norm"] = _rms_bwd(h, p["mix_norm"], du, dout, "mix_norm_bwd")
    return dh, grads


def _local_step(x, target, p, links):
    def ffn_weights(tag, behind):
        def get(n):
            w_in_t, w_out = links.weight(tag + "_w_in", n), links.weight(tag + "_w_out", n)
            return w_in_t, w_out, links.started if behind else None
        return get

    h1, s1 = _ffn_fwd(x, p["ffn1_norm"] + links.started[0, 0], ffn_weights("ffn1", True), "ffn1")
    h2, sm = _mixer_fwd(h1, p, links)
    h3, s2 = _ffn_fwd(h2, p["ffn2_norm"], ffn_weights("ffn2", False), "ffn2")
    dh3, dfinal, loss = _loss_head(h3, target, p["final_norm"])
    g = {"final_norm": dfinal}
    dh2, g["ffn2_norm"] = _ffn_bwd(h2, p["ffn2_norm"], s2, dh3, "ffn2", links)
    dh1, gm = _mixer_bwd(h1, p, links, sm, dh2)
    g.update(gm)
    dx, g["ffn1_norm"] = _ffn_bwd(x, p["ffn1_norm"], s1, dh1, "ffn1", links)
    return loss, dx, g


HBM_SPEC = pl.BlockSpec(memory_space=pltpu.HBM)
SEM_SPEC = pl.BlockSpec(memory_space=pltpu.SEMAPHORE)
DATAFLOW = pltpu.SideEffectType.DATAFLOW_SIDE_EFFECTING


def _position():
    x, y, c = lax.axis_index("x"), lax.axis_index("y"), lax.axis_index("c")
    return x, y, c, 4 * x + 2 * y + c


def _relations(x, y, c):
    for rel in range(1, N_DEV):
        px = 1 - x if rel & 4 else x
        py = 1 - y if rel & 2 else y
        pc = 1 - c if rel & 1 else c
        yield rel, (px, py, pc), 4 * px + 2 * py + pc


def _sem_index(item, rel):
    return item * (N_DEV - 1) + rel - 1


def _landing(a, mode):
    return lax.empty((N_DEV,) + a.shape if mode == "gather" else a.shape, a.dtype)


ALL_PEERS = tuple(range(1, N_DEV))
ONE_PER_CHIP = (1, 2, 4, 6)


def _copies_start(groups, name, rels=ALL_PEERS):
    flat = [item for grp in groups for item in grp]
    n, ng = len(flat), len(groups)
    lands = [_landing(a, mode) for a, mode in flat]

    def body(*refs):
        src_refs, land_refs, sems, token = refs[:n], refs[n:2 * n], refs[2 * n:2 * n + 2 * ng], refs[-1]
        x, y, c, me = _position()
        for rel, where, peer in _relations(x, y, c):
            if rel not in rels:
                continue
            k = 0
            for gi, grp in enumerate(groups):
                for li, (_, mode) in enumerate(grp):
                    src = src_refs[k] if mode == "gather" else src_refs[k].at[peer]
                    pltpu.make_async_remote_copy(src_ref=src, dst_ref=land_refs[k].at[me], send_sem=sems[2 * gi].at[_sem_index(li, rel)],
                                                 recv_sem=sems[2 * gi + 1].at[_sem_index(li, rel)], device_id=where, device_id_type=MESH_IDS).start()
                    k += 1
        token[...] = jnp.zeros_like(token)

    sem_shapes = [pltpu.SemaphoreType.DMA((len(grp) * (N_DEV - 1),)) for grp in groups for _ in range(2)]
    thru = [pltpu.HBM(a.shape, a.dtype) for a, _ in flat] + [pltpu.HBM(l.shape, l.dtype) for l in lands]
    outs = pl.pallas_call(
        body, name=name, out_shape=(*sem_shapes, *thru, SDS((8, HEAD), F32)),
        in_specs=[HBM_SPEC] * (2 * n), out_specs=(*[SEM_SPEC] * (2 * ng), *[HBM_SPEC] * (2 * n), pl.BlockSpec(memory_space=pltpu.VMEM)),
        input_output_aliases={i: 2 * ng + i for i in range(2 * n)}, compiler_params=pltpu.CompilerParams(has_side_effects=DATAFLOW),
    )(*[pltpu.with_memory_space_constraint(a, pltpu.HBM) for a, _ in flat], *[pltpu.with_memory_space_constraint(l, pltpu.HBM) for l in lands])
    sems, srcs, landed, token = outs[:2 * ng], outs[2 * ng:2 * ng + n], outs[2 * ng + n:2 * ng + 2 * n], outs[-1]
    result, k = [], 0
    for gi, grp in enumerate(groups):
        result.append((sems[2 * gi], sems[2 * gi + 1], srcs[k:k + len(grp)], landed[k:k + len(grp)]))
        k += len(grp)
    return result, token


def _copies_wait(started, modes, after, name, rels=ALL_PEERS):
    send_sems, recv_sems, srcs, lands = started
    n = len(srcs)

    def body(*refs):
        src_refs, land_refs, ssem, rsem, token = refs[:n], refs[n:2 * n], refs[2 * n], refs[2 * n + 1], refs[-1]
        x, y, c, _ = _position()
        for rel in rels:
            for i, mode in enumerate(modes):
                src = src_refs[i] if mode == "gather" else src_refs[i].at[0]
                cp = pltpu.make_async_remote_copy(src_ref=src, dst_ref=land_refs[i].at[0], send_sem=ssem.at[_sem_index(i, rel)],
                                                  recv_sem=rsem.at[_sem_index(i, rel)], device_id=(x, y, c), device_id_type=MESH_IDS)
                cp.wait_send()
                cp.wait_recv()
        token[...] = jnp.zeros_like(token)

    outs = pl.pallas_call(
        body, name=name, out_shape=[pltpu.HBM(a.shape, a.dtype) for a in (*srcs, *lands)] + [SDS((8, HEAD), F32)],
        in_specs=[HBM_SPEC] * (2 * n) + [SEM_SPEC, SEM_SPEC, pl.BlockSpec(memory_space=pl.ANY)],
        out_specs=[HBM_SPEC] * (2 * n) + [pl.BlockSpec(memory_space=pltpu.VMEM)],
        input_output_aliases={i: i for i in range(2 * n)}, compiler_params=pltpu.CompilerParams(has_side_effects=DATAFLOW),
    )(*srcs, *lands, send_sems, recv_sems, after)
    return outs[:n], outs[n:2 * n], outs[-1]


OTHER_CHIPS = ((1, 0), (0, 1), (1, 1))


def _pass_on_start(lands, name):
    n = len(lands)

    def body(*refs):
        land_refs, ssem, rsem, token = refs[:n], refs[n], refs[n + 1], refs[-1]
        x, y, c, _ = _position()
        for j, (fx, fy) in enumerate(OTHER_CHIPS):
            slot = 4 * (1 - x if fx else x) + 2 * (1 - y if fy else y) + c
            for i in range(n):
                pltpu.make_async_remote_copy(src_ref=land_refs[i].at[slot], dst_ref=land_refs[i].at[slot], send_sem=ssem.at[i * len(OTHER_CHIPS) + j],
                                             recv_sem=rsem.at[i * len(OTHER_CHIPS) + j], device_id=(x, y, 1 - c), device_id_type=MESH_IDS).start()
        token[...] = jnp.zeros_like(token)

    sems = pltpu.SemaphoreType.DMA((n * len(OTHER_CHIPS),))
    outs = pl.pallas_call(
        body, name=name, out_shape=(sems, sems, *[pltpu.HBM(l.shape, l.dtype) for l in lands], SDS(TOKEN, F32)),
        in_specs=[HBM_SPEC] * n, out_specs=(SEM_SPEC, SEM_SPEC, *[HBM_SPEC] * n, pl.BlockSpec(memory_space=pltpu.VMEM)),
        input_output_aliases={i: 2 + i for i in range(n)}, compiler_params=pltpu.CompilerParams(has_side_effects=DATAFLOW),
    )(*lands)
    return (outs[0], outs[1], outs[2:2 + n]), outs[-1]


def _pass_on_wait(started, after, name):
    send_sems, recv_sems, lands = started
    n = len(lands)

    def body(*refs):
        land_refs, ssem, rsem = refs[:n], refs[n], refs[n + 1]
        x, y, c, _ = _position()
        for j in range(len(OTHER_CHIPS)):
            for i in range(n):
                cp = pltpu.make_async_remote_copy(src_ref=land_refs[i].at[0], dst_ref=land_refs[i].at[0], send_sem=ssem.at[i * len(OTHER_CHIPS) + j],
                                                  recv_sem=rsem.at[i * len(OTHER_CHIPS) + j], device_id=(x, y, c), device_id_type=MESH_IDS)
                cp.wait_send()
                cp.wait_recv()

    return pl.pallas_call(
        body, name=name, out_shape=[pltpu.HBM(l.shape, l.dtype) for l in lands],
        in_specs=[HBM_SPEC] * n + [SEM_SPEC, SEM_SPEC, pl.BlockSpec(memory_space=pl.ANY)], out_specs=[HBM_SPEC] * n,
        input_output_aliases={i: i for i in range(n)}, compiler_params=pltpu.CompilerParams(has_side_effects=DATAFLOW),
    )(*lands, send_sems, recv_sems, after)


WEIGHT_GROUPS = (("ffn1_w_in", "ffn1_w_out", "gdn_conv_w"), ("w_in",), ("w_branch_hgrn", "w_branch_gdn", "w_out", "ffn2_w_in", "ffn2_w_out"))
GROUP_RELS = (ONE_PER_CHIP, ONE_PER_CHIP, ALL_PEERS)


class _Links:
    def __init__(self, shards, me):
        self.me = me
        self.shards = shards
        self.weights = {}
        self.sends = []
        self.gathers = {}
        self.started = None
        self._start_gather(0, None)

    def _start_gather(self, gi, zeros):
        if gi < len(WEIGHT_GROUPS):
            items = [(self.shards[n] if zeros is None else self.shards[n] + zeros[0, 0].astype(self.shards[n].dtype), "gather")
                     for n in WEIGHT_GROUPS[gi]]
            started, self.started = _copies_start([items], "gather_start_%d" % gi, GROUP_RELS[gi])
            self.gathers[gi] = started[0]

    def weight(self, name, after):
        if name not in self.weights:
            source = {"w_in_t": "w_in", "w_in_b_t": "w_in", "w_in_ab_t": "w_in", "conv_w": "gdn_conv_w"}.get(name, name)
            gi = [i for i, grp in enumerate(WEIGHT_GROUPS) if source in grp][0]
            assert gi in self.gathers, "weight groups are asked for in order"
            srcs, lands, zero = _copies_wait(self.gathers[gi], ["gather"] * len(WEIGHT_GROUPS[gi]), after, "gather_wait_%d" % gi, GROUP_RELS[gi])
            if GROUP_RELS[gi] == ONE_PER_CHIP:
                passing, zero = _pass_on_start(lands, "gather_pass_%d" % gi)
                self._start_gather(gi + 1, zero)
                lands = _pass_on_wait(passing, self.started, "gather_passed_%d" % gi)
            else:
                self._start_gather(gi + 1, zero)
            for n, src, land in zip(WEIGHT_GROUPS[gi], srcs, lands):
                full = lax.dynamic_update_index_in_dim(land, src, self.me, 0)
                if n == "gdn_conv_w":
                    self.weights["conv_w"] = full.reshape(N_DEV, CONV_K, 4 * D_MODEL // N_DEV).transpose(1, 0, 2).reshape(CONV_K, 4 * D_MODEL)
                elif n == "w_in":
                    self.weights.update(_w_in_pieces(full.reshape(-1, D_MODEL)))
                else:
                    self.weights[n] = full.reshape(-1, D_MODEL)
        return self.weights[name]

    def send(self, grads):
        names = list(grads)
        blocks = [grads[n].reshape(N_DEV, -1, D_MODEL) for n in names]
        started, token = _copies_start([[(b, "scatter") for b in blocks]], "send_" + names[0])
        self.sends.append((names, started[0]))
        return token

    def landed(self, after):
        out = {}
        for names, started in self.sends:
            srcs, lands, _ = _copies_wait(started, ["scatter"] * len(names), after, "landed_" + names[0])
            for n, src, land in zip(names, srcs, lands):
                out[n] = lax.dynamic_update_index_in_dim(land, lax.dynamic_index_in_dim(src, self.me, 0, keepdims=False), self.me, 0)
        return out


def _adam(parts, w, m, v, name):
    n_parts, r, c = parts.shape
    tc = c if c <= 512 else (256 if r > 1024 else 512)

    def body(p_ref, w_ref, m_ref, v_ref, g_ref, d_ref, mo_ref, vo_ref):
        g = p_ref[0].astype(F32)
        for i in range(1, n_parts):
            g = g + p_ref[i].astype(F32)
        m_new = ADAM_B1 * m_ref[...] + (1.0 - ADAM_B1) * g
        v_new = ADAM_B2 * v_ref[...] + (1.0 - ADAM_B2) * (g * g)
        m_hat = m_new / (1.0 - ADAM_B1 ** ADAM_STEP)
        v_hat = v_new / (1.0 - ADAM_B2 ** ADAM_STEP)
        g_ref[...] = g
        d_ref[...] = -ADAM_LR * (m_hat / (jnp.sqrt(v_hat) + ADAM_EPS) + ADAM_WD * w_ref[...])
        mo_ref[...] = m_new
        vo_ref[...] = v_new

    spec = pl.BlockSpec((r, tc), lambda j: (0, j))
    return pl.pallas_call(
        body, grid=(c // tc,), in_specs=[pl.BlockSpec((n_parts, r, tc), lambda j: (0, 0, j)), spec, spec, spec],
        out_specs=[spec] * 4, out_shape=[SDS((r, c), F32)] * 4, name=name, compiler_params=_params(1),
    )(parts, w, m, v)


BIG = ("ffn1_w_in", "ffn1_w_out", "w_in", "w_branch_hgrn", "w_branch_gdn", "w_out", "ffn2_w_in", "ffn2_w_out")


TRANSPOSED = ("ffn1_w_in", "w_in", "ffn2_w_in")


def _shard_rows(name, shard):
    return shard.T if name in TRANSPOSED else shard


SCALAR_ROWS = 8192
N_SCALAR = 2 * GDN_V_HEADS


def _w_in_pieces(w_in_t):
    return {"w_in_t": w_in_t, "w_in_b_t": w_in_t[SCALAR_ROWS + N_SCALAR:],
            "w_in_ab_t": jnp.pad(w_in_t[SCALAR_ROWS:SCALAR_ROWS + N_SCALAR], ((0, HEAD - N_SCALAR), (0, 0)))}


def _pad_lanes(a, width=HEAD):
    return jnp.pad(a, ((0, 0), (0, width - a.shape[1])))


SMALL_ROWS = 24


def _pack_small(g, loss):
    row6 = jnp.concatenate([g["hgrn_out_norm"], g["gdn_out_norm"], g["alog"], g["dtb"], loss,
                            jnp.zeros((1, D_MODEL - 5 * HEAD), F32)], axis=1)
    return jnp.concatenate([g["ffn1_norm"], g["mix_norm"], g["lbl"], g["ffn2_norm"], g["final_norm"], row6,
                            jnp.zeros((1, D_MODEL), F32), g["conv_w"].reshape(4 * CONV_K, D_MODEL)], axis=0)


def _pack_small_state(a):
    row6 = jnp.concatenate([a["hgrn_out_norm"], a["gdn_out_norm"], _pad_lanes(a["gdn_a_log"]), _pad_lanes(a["gdn_dt_bias"]),
                            jnp.zeros((1, D_MODEL - 4 * HEAD), F32)], axis=1)
    return jnp.concatenate([a["ffn1_norm"], a["mix_norm"], a["hgrn_lb_logits"], a["ffn2_norm"], a["final_norm"].reshape(1, D_MODEL),
                            row6, jnp.zeros((1, D_MODEL), F32)], axis=0)


def _unpack_small(a):
    return {"ffn1_norm": a[0:1], "mix_norm": a[1:2], "hgrn_lb_logits": a[2:4], "ffn2_norm": a[4:5], "final_norm": a[5],
            "hgrn_out_norm": a[6:7, :HEAD], "gdn_out_norm": a[6:7, HEAD:2 * HEAD],
            "gdn_a_log": a[6:7, 2 * HEAD:2 * HEAD + GDN_V_HEADS], "gdn_dt_bias": a[6:7, 3 * HEAD:3 * HEAD + GDN_V_HEADS]}


NAMES = ("ffn1_norm", "ffn1_w_in", "ffn1_w_out", "mix_norm", "w_in", "hgrn_lb_logits", "hgrn_out_norm", "gdn_conv_w", "gdn_a_log",
         "gdn_dt_bias", "gdn_out_norm", "w_branch_hgrn", "w_branch_gdn", "w_out", "ffn2_norm", "ffn2_w_in", "ffn2_w_out", "final_norm")


def kernel(x, ffn1_norm, ffn1_w_in, ffn1_w_out, mix_norm, w_in, hgrn_lb_logits, hgrn_out_norm, gdn_conv_w, gdn_a_log, gdn_dt_bias, gdn_out_norm, w_branch_hgrn, w_branch_gdn, w_out, ffn2_norm, ffn2_w_in, ffn2_w_out, final_norm, loss_target, m_ffn1_norm, m_ffn1_w_in, m_ffn1_w_out, m_mix_norm, m_w_in, m_hgrn_lb_logits, m_hgrn_out_norm, m_gdn_conv_w, m_gdn_a_log, m_gdn_dt_bias, m_gdn_out_norm, m_w_branch_hgrn, m_w_branch_gdn, m_w_out, m_ffn2_norm, m_ffn2_w_in, m_ffn2_w_out, m_final_norm, v_ffn1_norm, v_ffn1_w_in, v_ffn1_w_out, v_mix_norm, v_w_in, v_hgrn_lb_logits, v_hgrn_out_norm, v_gdn_conv_w, v_gdn_a_log, v_gdn_dt_bias, v_gdn_out_norm, v_w_branch_hgrn, v_w_branch_gdn, v_w_out, v_ffn2_norm, v_ffn2_w_in, v_ffn2_w_out, v_final_norm):
    wts = dict(zip(NAMES, (ffn1_norm, ffn1_w_in, ffn1_w_out, mix_norm, w_in, hgrn_lb_logits, hgrn_out_norm, gdn_conv_w, gdn_a_log,
                           gdn_dt_bias, gdn_out_norm, w_branch_hgrn, w_branch_gdn, w_out, ffn2_norm, ffn2_w_in, ffn2_w_out, final_norm)))
    mom = dict(zip(NAMES, (m_ffn1_norm, m_ffn1_w_in, m_ffn1_w_out, m_mix_norm, m_w_in, m_hgrn_lb_logits, m_hgrn_out_norm, m_gdn_conv_w,
                           m_gdn_a_log, m_gdn_dt_bias, m_gdn_out_norm, m_w_branch_hgrn, m_w_branch_gdn, m_w_out, m_ffn2_norm, m_ffn2_w_in,
                           m_ffn2_w_out, m_final_norm)))
    var = dict(zip(NAMES, (v_ffn1_norm, v_ffn1_w_in, v_ffn1_w_out, v_mix_norm, v_w_in, v_hgrn_lb_logits, v_hgrn_out_norm, v_gdn_conv_w,
                           v_gdn_a_log, v_gdn_dt_bias, v_gdn_out_norm, v_w_branch_hgrn, v_w_branch_gdn, v_w_out, v_ffn2_norm, v_ffn2_w_in,
                           v_ffn2_w_out, v_final_norm)))
    me = 4 * lax.axis_index("x") + 2 * lax.axis_index("y") + lax.axis_index("c")

    conv_shard = wts["gdn_conv_w"][0]
    shards = {n: _shard_rows(n, wts[n][0]).astype(BF16) for n in BIG}
    shards["gdn_conv_w"] = conv_shard.reshape(2, D_MODEL)
    links = _Links(shards, me)
    p = {"ffn1_norm": wts["ffn1_norm"], "mix_norm": wts["mix_norm"], "ffn2_norm": wts["ffn2_norm"], "final_norm": wts["final_norm"].reshape(1, D_MODEL),
         "lbl": wts["hgrn_lb_logits"], "hgrn_out_norm": wts["hgrn_out_norm"], "gdn_out_norm": wts["gdn_out_norm"],
         "alog": _pad_lanes(wts["gdn_a_log"]), "dtb": _pad_lanes(wts["gdn_dt_bias"])}

    loss, dx, g = _local_step(x[0], loss_target[0], p, links)

    small_started, small_token = _copies_start([[(_pack_small(g, loss), "gather")]], "small_start")
    landed = links.landed(small_token)

    big = [{} for _ in range(4)]
    for n in BIG:
        res = _adam(landed[n], _shard_rows(n, wts[n][0]), _shard_rows(n, mom[n][0]), _shard_rows(n, var[n][0]), "adam_" + n)
        for kind in range(4):
            big[kind][n] = _shard_rows(n, res[kind])
    small_srcs, small_lands, _ = _copies_wait(small_started[0], ["gather"], res[0], "small_wait")
    small_parts = lax.dynamic_update_index_in_dim(small_lands[0], small_srcs[0], me, 0)
    n_vec = SMALL_ROWS - 4 * CONV_K
    small_raw = _adam(small_parts[:, :n_vec], _pack_small_state(wts), _pack_small_state(mom), _pack_small_state(var), "adam_small")
    small = [_unpack_small(o) for o in small_raw]
    loss_total = small_raw[0][6, 4 * HEAD]
    conv_parts = small_parts[:, n_vec:].reshape(N_DEV, CONV_K, 4 * D_MODEL)
    width = 4 * D_MODEL // N_DEV
    conv_mine = lax.dynamic_slice_in_dim(conv_parts, me * width, width, axis=2)
    conv = _adam(conv_mine, conv_shard, mom["gdn_conv_w"][0], var["gdn_conv_w"][0], "adam_conv")

    outs = []
    for kind in range(4):
        for n in NAMES:
            if n in BIG:
                outs.append(big[kind][n][None])
            elif n == "gdn_conv_w":
                outs.append(conv[kind][None])
            else:
                outs.append(small[kind][n])
    return (loss_total, dx[None], *outs)
```

```python
import functools

import jax
import jax.numpy as jnp
from jax import lax
from jax.experimental import pallas as pl
from jax.experimental.pallas import tpu as pltpu

F32 = jnp.float32
BF16 = jnp.bfloat16
MESH_IDS = pl.DeviceIdType.MESH

D_MODEL = 1024
D_FF = 2816
N_DEV = 8
EPS = 1e-6
HEAD = 128
HG_HEADS = 8
GDN_QK_HEADS = 8
GDN_V_HEADS = 16
GDN_CHUNK = 64
HG_CHUNK = 16
CONV_K = 4
LANES = 128
COL_HQ, COL_HF, COL_HI, COL_HG, COL_GQ, COL_GK, COL_GV = 0, 8, 16, 24, 32, 40, 48
COL_GZ, COL_GATE_H, COL_GATE_G = 0, 16, 24
VMEM_LIMIT = 56 * 1024 * 1024

ADAM_LR, ADAM_B1, ADAM_B2, ADAM_EPS, ADAM_WD, ADAM_STEP = 0.001, 0.9, 0.999, 1e-08, 0.01, 10

SDS = jax.ShapeDtypeStruct


def _params(n_axes):
    return pltpu.CompilerParams(dimension_semantics=("arbitrary",) * n_axes, vmem_limit_bytes=VMEM_LIMIT)


def _tile(n, candidates=(512, 384, 256, 128, 64, 32, 16, 8)):
    for c in candidates:
        if n % c == 0:
            return c
    return n


_DIMS = {"nn": ((1,), (0,)), "nt": ((1,), (1,)), "tn": ((0,), (0,))}


def _bdot_raw(a, b, dims):
    return lax.dot_general(a.astype(BF16), b.astype(BF16), (_DIMS[dims], ((), ())), preferred_element_type=F32)


@functools.partial(jax.custom_vjp, nondiff_argnums=(2,))
def _bdot(a, b, dims):
    return _bdot_raw(a, b, dims)


def _bdot_fwd(a, b, dims):
    return _bdot_raw(a, b, dims), (a, b)


def _bdot_bwd(dims, res, ct):
    a, b = res
    if dims == "nn":
        return _bdot_raw(ct, b, "nt"), _bdot_raw(a, ct, "tn")
    if dims == "nt":
        return _bdot_raw(ct, b, "nn"), _bdot_raw(ct, a, "tn")
    return _bdot_raw(b, ct, "nt"), _bdot_raw(a, ct, "nn")


_bdot.defvjp(_bdot_fwd, _bdot_bwd)


def _hdot_raw(ones, x):
    hi = x.astype(BF16)
    rest = x - hi.astype(F32)
    mid = rest.astype(BF16)
    low = (rest - mid.astype(F32)).astype(BF16)
    return _bdot_raw(ones, hi, "nn") + (_bdot_raw(ones, mid, "nn") + _bdot_raw(ones, low, "nn"))


MM_VMEM_BUDGET = 38 * 1024 * 1024
TOKEN = (8, HEAD)


def _mm_tiles(m, n, k, a_bytes, b_bytes, o_bytes, r_bytes, m_align=8):
    def need(tm, tn, tk):
        return 2 * (tm * tk * a_bytes + tk * tn * b_bytes + tm * tn * (o_bytes + r_bytes)) + (tm * tn * 4 if tk < k else 0)

    def shrink(tm, tn, tk, floor_m, floor_n):
        while need(tm, tn, tk) > MM_VMEM_BUDGET:
            if tn > floor_n and tn % 256 == 0 and tn >= tm:
                tn //= 2
            elif tm > floor_m and tm % (2 * m_align) == 0:
                tm //= 2
            elif tn > floor_n and tn % 256 == 0:
                tn //= 2
            else:
                return None
        return tm, tn, tk

    tm = _tile(m, (1408, 1024, 704, 512, 256, 128, 64, 32, 16, 8))
    tn = _tile(n, (1408, 1024, 512, 256, 128))
    whole = shrink(tm, tn, k, min(tm, 1024), min(tn, 512))
    if whole is not None:
        return whole
    tk = _tile(k, (2048, 1408, 1024, 512, 256, 128, 64, 32, 16, 8))
    while True:
        fit = shrink(tm, tn, tk, min(tm, 256), min(tn, 512))
        if fit is not None or tk <= 512 or tk % 256:
            return fit if fit is not None else (tm, tn, tk)
        tk //= 2


def _mm(a, b, dims, out_dtype, name, res=None, alpha=1.0, after=None, b_rows=None):
    b_shape = b.shape if b_rows is None else (b_rows, b.shape[1])
    if dims == "nn":
        (m, k), (k2, n) = a.shape, b_shape
    elif dims == "nt":
        (m, k), (n, k2) = a.shape, b_shape
    else:
        (k, m), (k2, n) = a.shape, b_shape
    assert k == k2, (a.shape, b.shape, dims)
    has_res = res is not None
    tm, tn, tk = _mm_tiles(m, n, k, a.dtype.itemsize, b.dtype.itemsize, jnp.dtype(out_dtype).itemsize, res.dtype.itemsize if has_res else 0,
                           m_align=LANES if dims == "tn" else 8)
    nk = k // tk
    a_spec = pl.BlockSpec((tk, tm), lambda i, j, kk: (kk, i)) if dims == "tn" else pl.BlockSpec((tm, tk), lambda i, j, kk: (i, kk))
    b_spec = pl.BlockSpec((tn, tk), lambda i, j, kk: (j, kk)) if dims == "nt" else pl.BlockSpec((tk, tn), lambda i, j, kk: (kk, j))
    o_spec = pl.BlockSpec((tm, tn), lambda i, j, kk: (i, j))

    def finish(acc, r_ref, o_ref):
        out = acc * alpha if alpha != 1.0 else acc
        if has_res:
            out = r_ref[...].astype(F32) + out
        o_ref[...] = out.astype(o_ref.dtype)

    n_in = 2 + has_res + (after is not None)

    def body(*refs):
        a_ref, b_ref = refs[:2]
        r_ref = refs[2] if has_res else None
        o_ref = refs[n_in]
        p = _bdot_raw(a_ref[...], b_ref[...], dims)
        if nk == 1:
            finish(p, r_ref, o_ref)
            return
        acc_ref = refs[-1]
        kk = pl.program_id(2)

        @pl.when(kk == 0)
        def _():
            acc_ref[...] = p

        @pl.when(kk > 0)
        def _():
            acc_ref[...] += p

        @pl.when(kk == nk - 1)
        def _():
            finish(acc_ref[...], r_ref, o_ref)

    args = (a, b) + ((res,) if has_res else ()) + ((after,) if after is not None else ())
    in_specs = [a_spec, b_spec] + ([o_spec] if has_res else []) + ([pl.BlockSpec(TOKEN, lambda i, j, kk: (0, 0))] if after is not None else [])
    return pl.pallas_call(
        body, grid=(m // tm, n // tn, nk), in_specs=in_specs, out_specs=o_spec, out_shape=SDS((m, n), out_dtype),
        scratch_shapes=[pltpu.VMEM((tm, tn), F32)] if nk > 1 else [], name=name, compiler_params=_params(3),
    )(*args)


PIECE_TK = 1024


def _mm_pieces(pieces, b, name, res=None, after=None):
    m, n = pieces[0].shape[0], b.shape[1]
    blocks = [p.shape[1] // PIECE_TK for p in pieces]
    assert all(p.shape[1] % PIECE_TK == 0 and p.shape[0] == m for p in pieces)
    starts = [sum(blocks[:i]) for i in range(len(pieces))]
    nk = sum(blocks)
    tm, tn = _tile(m, (1024, 512, 256, 128)), _tile(n, (1024, 512, 256, 128))
    n_p = len(pieces)
    n_in = n_p + 1 + (res is not None) + (after is not None)

    def piece_spec(start, count):
        return pl.BlockSpec((tm, PIECE_TK), lambda i, j, kk: (i, jnp.clip(kk - start, 0, count - 1)))

    def body(*refs):
        b_ref, o_ref, acc_ref = refs[n_p], refs[n_in], refs[-1]
        kk = pl.program_id(2)

        @pl.when(kk == 0)
        def _():
            acc_ref[...] = jnp.zeros_like(acc_ref)

        for p_ref, start, count in zip(refs[:n_p], starts, blocks):
            @pl.when(jnp.logical_and(kk >= start, kk < start + count))
            def _(p_ref=p_ref):
                acc_ref[...] += _bdot_raw(p_ref[...], b_ref[...], "nn")

        @pl.when(kk == nk - 1)
        def _():
            out = acc_ref[...]
            if res is not None:
                out = refs[n_p + 1][...] + out
            o_ref[...] = out

    o_spec = pl.BlockSpec((tm, tn), lambda i, j, kk: (i, j))
    in_specs = [piece_spec(s, c) for s, c in zip(starts, blocks)] + [pl.BlockSpec((PIECE_TK, tn), lambda i, j, kk: (kk, j))]
    args = list(pieces) + [b]
    if res is not None:
        in_specs.append(o_spec)
        args.append(res)
    if after is not None:
        in_specs.append(pl.BlockSpec(TOKEN, lambda i, j, kk: (0, 0)))
        args.append(after)
    return pl.pallas_call(
        body, grid=(m // tm, n // tn, nk), in_specs=in_specs, out_specs=o_spec, out_shape=SDS((m, n), F32),
        scratch_shapes=[pltpu.VMEM((tm, tn), F32)], name=name, compiler_params=_params(3),
    )(*args)


def _tmap(fn, grid, ins, outs, name):
    n_in = len(ins)
    n_ax = len(grid)

    def body(*refs):
        vals = fn(*[r[...] for r in refs[:n_in]])
        if not isinstance(vals, (tuple, list)):
            vals = (vals,)
        first_inner = pl.program_id(n_ax - 1) == 0
        first_all = first_inner
        for ax in range(n_ax - 1):
            first_all = jnp.logical_and(first_all, pl.program_id(ax) == 0)

        def put(ref, val, acc):
            val = val.astype(ref.dtype)
            if acc is None:
                ref[...] = val
                return
            first = first_inner if acc == "inner" else first_all

            @pl.when(first)
            def _():
                ref[...] = val

            @pl.when(jnp.logical_not(first))
            def _():
                ref[...] += val

        for ref, val, o in zip(refs[n_in:], vals, outs):
            put(ref, val, o[4])

    return pl.pallas_call(
        body, grid=grid,
        in_specs=[pl.BlockSpec(bs, im) for _, bs, im in ins],
        out_specs=[pl.BlockSpec(o[2], o[3]) for o in outs],
        out_shape=[SDS(o[0], o[1]) for o in outs],
        name=name, compiler_params=_params(n_ax),
    )(*[a for a, _, _ in ins])


def _rows(width, tt, off=0):
    return (tt, width), (lambda j, i: (i, off + j))


def _rms(x, g):
    x = x.astype(F32)
    return x * lax.rsqrt(jnp.mean(x * x, axis=-1, keepdims=True) + EPS) * g


def _sigmoid(x):
    return jax.nn.sigmoid(x)


def _silu(x):
    return x * _sigmoid(x)


def _softplus(x):
    return jnp.maximum(x, 0.0) + jnp.log1p(jnp.exp(-jnp.abs(x)))


def _rms_fwd(x, g, name):
    t, d = x.shape
    tt = _tile(t, (256, 128))
    return _tmap(_rms, (1, t // tt), [(x, *_rows(d, tt)), (g, (1, d), lambda j, i: (0, 0))],
                 [((t, d), BF16, *_rows(d, tt), None)], name)[0]


def _rms_bwd(x, g, dn, dres, name):
    t, d = x.shape
    tt = _tile(t, (256, 128))

    def fn(x, g, dn, dres):
        x, dn = x.astype(F32), dn.astype(F32)
        r = lax.rsqrt(jnp.mean(x * x, axis=-1, keepdims=True) + EPS)
        xn = x * r
        t = dn * g
        dx = r * (t - xn * jnp.mean(t * xn, axis=-1, keepdims=True))
        return dres + dx, jnp.sum(dn * xn, axis=0, keepdims=True)

    return _tmap(fn, (1, t // tt),
                 [(x, *_rows(d, tt)), (g, (1, d), lambda j, i: (0, 0)), (dn, *_rows(d, tt)), (dres, *_rows(d, tt))],
                 [((t, d), F32, *_rows(d, tt), None), ((1, d), F32, (1, d), lambda j, i: (0, 0), "inner")], name)


def _swiglu(a, b):
    return _silu(a) * b


def _ffn_in_act(n, w_in_t, name, after):
    t, d = n.shape
    tm, tn = _tile(t, (512, 256, 128)), D_FF // 2
    half_blocks = D_FF // tn
    n_in = 3 + (after is not None)

    def body(*refs):
        n_ref, wa_ref, wb_ref = refs[:3]
        a_ref, b_ref, s_ref = refs[n_in:]
        x = n_ref[...]
        a = _bdot_raw(x, wa_ref[...], "nt").astype(BF16)
        b = _bdot_raw(x, wb_ref[...], "nt").astype(BF16)
        a_ref[...] = a
        b_ref[...] = b
        s_ref[...] = _swiglu(a.astype(F32), b.astype(F32)).astype(BF16)

    out = pl.BlockSpec((tm, tn), lambda i, j: (i, j))
    in_specs = [pl.BlockSpec((tm, d), lambda i, j: (i, 0)), pl.BlockSpec((tn, d), lambda i, j: (j, 0)),
                pl.BlockSpec((tn, d), lambda i, j: (j + half_blocks, 0))]
    args = [n, w_in_t, w_in_t]
    if after is not None:
        in_specs.append(pl.BlockSpec(TOKEN, lambda i, j: (0, 0)))
        args.append(after)
    return pl.pallas_call(
        body, grid=(t // tm, half_blocks), in_specs=in_specs, out_specs=[out, out, out], out_shape=[SDS((t, D_FF), BF16)] * 3,
        name=name, compiler_params=_params(2),
    )(*args)


def _swiglu_bwd(a, b, ds, name):
    t = a.shape[0]
    tt = _tile(t, (128,))

    def fn(a, b, ds):
        _, vjp = jax.vjp(_swiglu, a.astype(F32), b.astype(F32))
        da, db = vjp(ds.astype(F32))
        return jnp.concatenate([da, db], axis=1)

    r = _rows(D_FF, tt)
    return _tmap(fn, (1, t // tt), [(a, *r), (b, *r), (ds, *r)], [((t, 2 * D_FF), BF16, *_rows(2 * D_FF, tt), None)], name)[0]


def _ffn_fwd(h, g, weights, tag):
    n = _rms_fwd(h, g, tag + "_norm")
    w_in_t, w_out, after = weights(n)
    a, b, s = _ffn_in_act(n, w_in_t, tag + "_in", after)
    out = _mm(s, w_out, "nn", F32, tag + "_out", res=h, alpha=0.5)
    return out, (n, a, b, s, w_in_t, w_out)


def _ffn_bwd(h, g, saved, dout, tag, links):
    n, a, b, s, w_in_t, w_out = saved
    sent = links.send({tag + "_w_out": _mm(s, dout, "tn", BF16, tag + "_dw_out", alpha=0.5)})
    ds = _mm(dout, w_out, "nt", BF16, tag + "_ds", alpha=0.5, after=sent)
    dab = _swiglu_bwd(a, b, ds, tag + "_dact")
    sent = links.send({tag + "_w_in": _mm(dab, n, "tn", BF16, tag + "_dw_in")})
    dn = _mm(dab, w_in_t, "nn", BF16, tag + "_dn", after=sent)
    return _rms_bwd(h, g, dn, dout, tag + "_dnorm")


def _chunk_sum_matrix(n, chunk, transpose=False):
    row = lax.broadcasted_iota(jnp.int32, (n, n), 0)
    col = lax.broadcasted_iota(jnp.int32, (n, n), 1)
    if transpose:
        row, col = col, row
    return jnp.where(jnp.logical_and(col <= row, row // chunk == col // chunk), 1.0, 0.0).astype(F32)


def _hgrn_gates(hq, hf, lbl):
    lb = _sigmoid(lbl[0:1, :] - lbl[1:2, :])
    sg = _sigmoid(hf)
    f = lb + (1.0 - lb) * sg
    q = _silu(hq) * HEAD ** -0.5
    k = (1.0 - lb) * (1.0 - sg)
    return q, k, jnp.log(f)


def _hgrn_prep_fwd(proj, lbl):
    t = proj.shape[0]
    tt, ft = _tile(t, (256, 128)), 512

    def fn(hq, hf, lbl):
        q, k, log_f = _hgrn_gates(hq, hf, lbl)
        return q, k, _hdot_raw(_chunk_sum_matrix(tt, HG_CHUNK), log_f)

    o = ((t, D_MODEL), F32, *_rows(ft, tt), None)
    return _tmap(fn, (D_MODEL // ft, t // tt),
                 [(proj, *_rows(ft, tt, COL_HQ * HEAD // ft)), (proj, *_rows(ft, tt, COL_HF * HEAD // ft)), (lbl, (2, ft), lambda j, i: (0, j))],
                 [o, o, o], "hgrn_prep")


def _hgrn_prep_bwd(proj, lbl, dq, dk, db):
    t = proj.shape[0]
    tt, ft = _tile(t, (256, 128)), 512

    def fn(hq, hf, lbl, dq, dk, db):
        dlog_f = _hdot_raw(_chunk_sum_matrix(tt, HG_CHUNK, transpose=True), db)
        _, vjp = jax.vjp(_hgrn_gates, hq, hf, lbl)
        return vjp((dq, dk, dlog_f))

    o = ((t, D_MODEL), BF16, *_rows(ft, tt), None)
    r = _rows(ft, tt)
    return _tmap(fn, (D_MODEL // ft, t // tt),
                 [(proj, *_rows(ft, tt, COL_HQ * HEAD // ft)), (proj, *_rows(ft, tt, COL_HF * HEAD // ft)), (lbl, (2, ft), lambda j, i: (0, j)),
                  (dq, *r), (dk, *r), (db, *r)],
                 [o, o, ((2, D_MODEL), F32, (2, ft), lambda j, i: (0, j), "inner")], "hgrn_prep_bwd")


@functools.partial(jax.custom_vjp, nondiff_argnums=(1,))
def _roll_rows(x, d):
    return pltpu.roll(x, d, 0)


def _roll_rows_fwd(x, d):
    return pltpu.roll(x, d, 0), None


def _roll_rows_bwd(d, _, ct):
    return (pltpu.roll(ct, ct.shape[0] - d, 0),)


_roll_rows.defvjp(_roll_rows_fwd, _roll_rows_bwd)


def _hgrn_chunks(q, k, v, b, st):
    n = q[0].shape[0]
    half = n // 2
    srow = lax.broadcasted_iota(jnp.int32, (half, HEAD), 0)
    inter = _each(lambda q, b, st: _bdot(q * jnp.exp(b), st, "nt"), q, b, st)

    def below_scores(q, k, b):
        ref = b[half:half + 1, :]
        return _bdot(q[half:] * jnp.exp(jnp.minimum(b[half:] - ref, 0.0)), k[:half] * jnp.exp(jnp.minimum(ref - b[:half], 0.0)), "nt")

    below = _each(lambda a, v: _bdot(a, v[:half], "nn"), _each(below_scores, q, k, b), v)

    def diagonal(q, k, v, b):
        blocks = []
        for lo in (0, half):
            qb, kb, vb, bb = (a[lo:lo + half] for a in (q, k, v, b))
            o = jnp.sum(qb * kb, axis=1, keepdims=True) * vb
            for d in range(1, half):
                kr, vr, br = _roll_rows(kb, d), _roll_rows(vb, d), _roll_rows(bb, d)
                a = jnp.sum(qb * kr * jnp.exp(jnp.minimum(bb - br, 0.0)), axis=1, keepdims=True)
                o = o + jnp.where(srow[:, :1] >= d, a, 0.0) * vr
            blocks.append(o)
        return jnp.concatenate(blocks, axis=0)

    diag = _each(diagonal, q, k, v, b)
    o = _each(lambda inter, diag, below: inter + diag + jnp.concatenate([jnp.zeros_like(below), below], axis=0), inter, diag, below)

    def new_state(k, v, b, st):
        bend = b[n - 1:n, :]
        return st * jnp.exp(bend) + _bdot(v, k * jnp.exp(bend - b), "tn")

    return o, _each(new_state, k, v, b, st)


HG_GROUP = 8
HG_PER = GDN_CHUNK // HG_CHUNK


def _hgrn_rec_fwd(q, k, proj, b):
    t = q.shape[0]
    nc = t // GDN_CHUNK
    blk = (GDN_CHUNK, HG_GROUP * HEAD)
    im = lambda h, c: (c, h)

    def body(q_ref, k_ref, v_ref, b_ref, o_ref, hs_ref, st_ref):
        @pl.when(pl.program_id(1) == 0)
        def _():
            st_ref[...] = jnp.zeros_like(st_ref)

        heads = range(HG_GROUP)
        for j in range(HG_PER):
            sl = pl.ds(HG_CHUNK * j, HG_CHUNK)
            st = tuple(st_ref[g] for g in heads)
            o, st_new = _hgrn_chunks(*[tuple(r[sl, _head_lanes(g)] for g in heads) for r in (q_ref, k_ref, v_ref, b_ref)], st)
            for g in heads:
                hs_ref[g, j] = st[g]
                o_ref[sl, _head_lanes(g)] = o[g]
                st_ref[g] = st_new[g]

    return pl.pallas_call(
        body, grid=(HG_HEADS // HG_GROUP, nc),
        in_specs=[pl.BlockSpec(blk, im), pl.BlockSpec(blk, im), pl.BlockSpec(blk, lambda h, c: (c, COL_HI // HG_GROUP + h)), pl.BlockSpec(blk, im)],
        out_specs=[pl.BlockSpec(blk, im), pl.BlockSpec((HG_GROUP, HG_PER, HEAD, HEAD), lambda h, c: (h, c, 0, 0))],
        out_shape=[SDS((t, D_MODEL), F32), SDS((HG_HEADS, nc * HG_PER, HEAD, HEAD), F32)],
        scratch_shapes=[pltpu.VMEM((HG_GROUP, HEAD, HEAD), F32)], name="hgrn_rec", compiler_params=_params(2),
    )(q, k, proj, b)


def _hgrn_rec_bwd(q, k, proj, b, hs, do):
    t = q.shape[0]
    nc = t // GDN_CHUNK
    blk = (GDN_CHUNK, HG_GROUP * HEAD)
    im = lambda h, c: (nc - 1 - c, h)

    def body(q_ref, k_ref, v_ref, b_ref, hs_ref, do_ref, dq_ref, dk_ref, dv_ref, db_ref, dst_ref):
        @pl.when(pl.program_id(1) == 0)
        def _():
            dst_ref[...] = jnp.zeros_like(dst_ref)

        heads = range(HG_GROUP)
        for j in reversed(range(HG_PER)):
            sl = pl.ds(HG_CHUNK * j, HG_CHUNK)
            _, vjp = jax.vjp(_hgrn_chunks, *[tuple(r[sl, _head_lanes(g)] for g in heads) for r in (q_ref, k_ref, v_ref, b_ref)],
                             tuple(hs_ref[g, j] for g in heads))
            dq, dk, dv, db, dst = vjp((tuple(do_ref[sl, _head_lanes(g)] for g in heads), tuple(dst_ref[g] for g in heads)))
            for g in heads:
                ln = _head_lanes(g)
                dq_ref[sl, ln] = dq[g]
                dk_ref[sl, ln] = dk[g]
                dv_ref[sl, ln] = dv[g].astype(dv_ref.dtype)
                db_ref[sl, ln] = db[g]
                dst_ref[g] = dst[g]

    spec = pl.BlockSpec(blk, im)
    return pl.pallas_call(
        body, grid=(HG_HEADS // HG_GROUP, nc),
        in_specs=[spec, spec, pl.BlockSpec(blk, lambda h, c: (nc - 1 - c, COL_HI // HG_GROUP + h)), spec,
                  pl.BlockSpec((HG_GROUP, HG_PER, HEAD, HEAD), lambda h, c: (h, nc - 1 - c, 0, 0)), spec],
        out_specs=[spec, spec, spec, spec],
        out_shape=[SDS((t, D_MODEL), F32), SDS((t, D_MODEL), F32), SDS((t, D_MODEL), BF16), SDS((t, D_MODEL), F32)],
        scratch_shapes=[pltpu.VMEM((HG_GROUP, HEAD, HEAD), F32)], name="hgrn_rec_bwd", compiler_params=_params(2),
    )(q, k, proj, b, hs, do)


def _shift_down(x, d):
    if d == 0:
        return x
    row = lax.broadcasted_iota(jnp.int32, x.shape, 0)
    return jnp.where(row >= d, pltpu.roll(x, d, 0), 0.0)


def _shift_up(x, d):
    if d == 0:
        return x
    n = x.shape[0]
    row = lax.broadcasted_iota(jnp.int32, x.shape, 0)
    return jnp.where(row < n - d, pltpu.roll(x, n - d, 0), 0.0)


def _conv_fwd(proj, conv_w):
    t = proj.shape[0]
    width = 2 * D_MODEL + 2 * D_MODEL

    def body(x_ref, w_ref, c_ref, y_ref):
        x, w = x_ref[...], w_ref[...]
        y = w[CONV_K - 1:CONV_K, :] * x
        for j in range(CONV_K - 1):
            y = y + w[j:j + 1, :] * _shift_down(x, CONV_K - 1 - j)
        y_ref[...] = y
        c_ref[...] = _silu(y)

    out = pl.BlockSpec((t, HEAD), lambda j: (0, j))
    return pl.pallas_call(
        body, grid=(width // HEAD,),
        in_specs=[pl.BlockSpec((t, HEAD), lambda j: (0, COL_GQ + j)), pl.BlockSpec((CONV_K, HEAD), lambda j: (0, j))],
        out_specs=[out, out], out_shape=[SDS((t, width), F32), SDS((t, width), F32)],
        name="gdn_conv", compiler_params=_params(1),
    )(proj, conv_w)


def _conv_bwd(proj, conv_w, y, dc_qk, dc_v):
    t = proj.shape[0]
    n_qk = dc_qk.shape[1] // HEAD
    width = dc_qk.shape[1] + dc_v.shape[1]

    def body(x_ref, w_ref, y_ref, dqk_ref, dv_ref, dx_ref, dw_ref):
        x, w, y = x_ref[...], w_ref[...], y_ref[...]
        sg = _sigmoid(y)
        dc = jnp.where(pl.program_id(0) < n_qk, dqk_ref[...], dv_ref[...].astype(F32))
        dy = dc * (sg * (1.0 + y * (1.0 - sg)))
        ahead = [_shift_up(dy, CONV_K - 1 - j) for j in range(CONV_K)]
        dx = w[0:1, :] * ahead[0]
        for j in range(1, CONV_K):
            dx = dx + w[j:j + 1, :] * ahead[j]
        dx_ref[...] = dx.astype(dx_ref.dtype)
        dw_ref[...] = jnp.concatenate([jnp.sum(x * ahead[j], axis=0, keepdims=True) for j in range(CONV_K)], axis=0)

    blk = pl.BlockSpec((t, HEAD), lambda j: (0, j))
    return pl.pallas_call(
        body, grid=(width // HEAD,),
        in_specs=[pl.BlockSpec((t, HEAD), lambda j: (0, COL_GQ + j)), pl.BlockSpec((CONV_K, HEAD), lambda j: (0, j)), blk,
                  pl.BlockSpec((t, HEAD), lambda j: (0, jnp.minimum(j, n_qk - 1))), pl.BlockSpec((t, HEAD), lambda j: (0, jnp.maximum(j - n_qk, 0)))],
        out_specs=[blk, pl.BlockSpec((CONV_K, HEAD), lambda j: (0, j))],
        out_shape=[SDS((t, width), BF16), SDS((CONV_K, width), F32)],
        name="gdn_conv_bwd", compiler_params=_params(1),
    )(proj, conv_w, y, dc_qk, dc_v)


def _l2norm(x, scale):
    return x * lax.rsqrt(jnp.sum(x * x, axis=-1, keepdims=True) + EPS) * scale


def _head(a, h):
    return a[:, h * HEAD:(h + 1) * HEAD]


def _qk_scale(h):
    return HEAD ** -0.5 if h < GDN_QK_HEADS else 1.0


def _qk_norm_fwd(c):
    t = c.shape[0]
    tt = _tile(t, (256, 128))
    width = 2 * D_MODEL

    def fn(x):
        return jnp.concatenate([_l2norm(_head(x, h), _qk_scale(h)) for h in range(2 * GDN_QK_HEADS)], axis=1)

    return _tmap(fn, (1, t // tt), [(c, *_rows(width, tt))], [((t, width), F32, *_rows(width, tt), None)], "gdn_qk_norm")[0]


def _qk_norm_bwd(c, dq_rep, dk_rep):
    t = c.shape[0]
    tt = _tile(t, (256, 128))
    width = 2 * D_MODEL

    def fn(x, dq2, dk2):
        out = []
        for h in range(2 * GDN_QK_HEADS):
            d2, hh = (dq2, h) if h < GDN_QK_HEADS else (dk2, h - GDN_QK_HEADS)
            _, vjp = jax.vjp(lambda x: _l2norm(x, _qk_scale(h)), _head(x, h))
            out.append(vjp(_head(d2, 2 * hh).astype(F32) + _head(d2, 2 * hh + 1).astype(F32))[0])
        return jnp.concatenate(out, axis=1)

    r = _rows(width, tt)
    return _tmap(fn, (1, t // tt), [(c, *r), (dq_rep, *r), (dk_rep, *r)], [((t, width), F32, *r, None)], "gdn_qk_norm_bwd")[0]


def _gdn_gates(x, alog, dtb):
    return -jnp.exp(alog) * _softplus(x + dtb), _sigmoid(x)


def _gates_fwd(pab, alog, dtb):
    t = pab.shape[0]
    tt = _tile(t, (256, 128))

    def fn(x, alog, dtb):
        g, beta = _gdn_gates(x, alog, dtb)
        lane = lax.broadcasted_iota(jnp.int32, g.shape, 1)
        return jnp.where(lane < GDN_V_HEADS, _hdot_raw(_chunk_sum_matrix(tt, GDN_CHUNK), g), beta).T

    p = (alog, (1, HEAD), lambda j, i: (0, 0)), (dtb, (1, HEAD), lambda j, i: (0, 0))
    return _tmap(fn, (1, t // tt), [(pab, *_rows(HEAD, tt)), *p], [((HEAD, t), F32, (HEAD, tt), lambda j, i: (0, i), None)], "gdn_gates")[0]


def _gates_bwd(pab, alog, dtb, dout_t):
    t = pab.shape[0]
    tt = _tile(t, (256, 128))

    def fn(x, alog, dtb, dout_t):
        dout = dout_t.T
        lane = lax.broadcasted_iota(jnp.int32, dout.shape, 1)
        dgam = jnp.where(lane < GDN_V_HEADS, dout, 0.0)
        dbeta = jnp.where(jnp.logical_and(lane >= GDN_V_HEADS, lane < 2 * GDN_V_HEADS), dout, 0.0)
        dg = _hdot_raw(_chunk_sum_matrix(tt, GDN_CHUNK, transpose=True), dgam)
        _, vjp = jax.vjp(_gdn_gates, x, alog, dtb)
        return vjp((dg, dbeta))

    p = (alog, (1, HEAD), lambda j, i: (0, 0)), (dtb, (1, HEAD), lambda j, i: (0, 0))
    acc = ((1, HEAD), F32, (1, HEAD), lambda j, i: (0, 0), "inner")
    return _tmap(fn, (1, t // tt), [(pab, *_rows(HEAD, tt)), *p, (dout_t, (HEAD, tt), lambda j, i: (0, i))],
                 [((t, HEAD), BF16, *_rows(HEAD, tt), None), acc, acc], "gdn_gates_bwd")


def _split_bf16(x):
    hi = x.astype(BF16)
    return hi, (x - hi.astype(F32)).astype(BF16)


def _dot3(a, b):
    (ah, al), (bh, bl) = a, b
    return _bdot_raw(ah, bh, "nn") + (_bdot_raw(ah, bl, "nn") + _bdot_raw(al, bh, "nn"))


def _each(fn, *lists):
    return tuple(fn(*xs) for xs in zip(*lists))


def _unit_lower_inverses_raw(a):
    n = a[0].shape[0]
    row = lax.broadcasted_iota(jnp.int32, (n, n), 0)
    col = lax.broadcasted_iota(jnp.int32, (n, n), 1)
    eye = jnp.where(row == col, 1.0, 0.0).astype(F32)
    p = _each(lambda a: eye - a, a)
    x = _each(_split_bf16, a)
    m = 2
    while m < n:
        x = _each(_split_bf16, _each(_dot3, x, x))
        p = _each(lambda p, x: p + _bdot_raw(p, x[0], "nn"), p, x)
        m *= 2
    return p


@jax.custom_vjp
def _unit_lower_inverses(a, known):
    return _unit_lower_inverses_raw(a) if known is None else known


def _uli_fwd(a, known):
    inv = _unit_lower_inverses(a, known)
    return inv, (inv, known)


def _uli_bwd(res, ct):
    inv, known = res
    right = _each(lambda ct, inv: _bdot_raw(ct, inv, "nt"), ct, inv)
    da = _each(lambda inv, r: -_bdot_raw(inv, r, "tn"), inv, right)
    return da, (None if known is None else _each(jnp.zeros_like, known))


_unit_lower_inverses.defvjp(_uli_fwd, _uli_bwd)


def _gdn_chunks(q, k, v, beta_rows, gam_rows, s, inv_known=None):
    n = q[0].shape[0]
    heads = range(len(q))
    row = lax.broadcasted_iota(jnp.int32, (n, n), 0)
    col = lax.broadcasted_iota(jnp.int32, (n, n), 1)
    beta_cols, gam_cols = beta_rows.T, gam_rows.T
    beta = tuple(beta_cols[:, g:g + 1] for g in heads)
    gam = tuple(gam_cols[:, g:g + 1] for g in heads)
    gam_row = tuple(gam_rows[g:g + 1, :] for g in heads)
    decay = _each(lambda gam, gam_row: jnp.where(row >= col, jnp.exp(jnp.minimum(gam - gam_row, 0.0)), 0.0), gam, gam_row)
    kb = _each(lambda k, beta: k * beta, k, beta)
    a = _each(lambda kb, k, decay: jnp.where(row > col, _bdot(kb, k, "nt") * decay, 0.0), kb, k, decay)
    inv = _unit_lower_inverses(a, inv_known)
    eg = _each(jnp.exp, gam)
    u = _each(lambda inv, v, beta: _bdot(inv, v * beta, "nn"), inv, v, beta)
    w = _each(lambda inv, kb, eg: _bdot(inv, kb * eg, "nn"), inv, kb, eg)
    qk = _each(lambda q, k, decay: _bdot(q, k, "nt") * decay, q, k, decay)
    v_new = _each(lambda u, w, s: u - _bdot(w, s, "nn"), u, w, s)
    o_state = _each(lambda q, eg, s: _bdot(q * eg, s, "nn"), q, eg, s)
    o = _each(lambda o_state, qk, v_new: o_state + _bdot(qk, v_new, "nn"), o_state, qk, v_new)
    gend = _each(lambda gam: gam[n - 1:n, :], gam)
    s_new = _each(lambda s, k, gam, gend, v_new: s * jnp.exp(gend) + _bdot(k * jnp.exp(gend - gam), v_new, "tn"), s, k, gam, gend, v_new)
    return o, s_new, inv


GDN_GROUP = 16


def _gdn_specs(nc, rev):
    cc = (lambda c: nc - 1 - c) if rev else (lambda c: c)
    grp = GDN_GROUP
    q = pl.BlockSpec((GDN_CHUNK, grp // 2 * HEAD), lambda h, c: (cc(c), h))
    k = pl.BlockSpec((GDN_CHUNK, grp // 2 * HEAD), lambda h, c: (cc(c), 2 * GDN_QK_HEADS // grp + h))
    v = pl.BlockSpec((GDN_CHUNK, grp * HEAD), lambda h, c: (cc(c), 2 * GDN_QK_HEADS // grp + h))
    o = pl.BlockSpec((GDN_CHUNK, grp * HEAD), lambda h, c: (cc(c), h))
    rw = pl.BlockSpec((grp, None, 1, GDN_CHUNK), lambda h, c: (h, cc(c), 0, 0))
    st = pl.BlockSpec((grp, None, HEAD, HEAD), lambda h, c: (h, cc(c), 0, 0))
    inv = pl.BlockSpec((grp, None, GDN_CHUNK, GDN_CHUNK), lambda h, c: (h, cc(c), 0, 0))
    return q, k, v, o, rw, st, inv


def _head_lanes(g, per=1):
    return pl.ds((g // per) * HEAD, HEAD)


def _gdn_rec_fwd(qk, c, beta_row, gam_row):
    t = qk.shape[0]
    nc = t // GDN_CHUNK
    q, k, v, o, rw, st, inv = _gdn_specs(nc, False)

    def body(q_ref, k_ref, v_ref, be_ref, gr_ref, o_ref, ss_ref, inv_ref, s_ref):
        @pl.when(pl.program_id(1) == 0)
        def _():
            s_ref[...] = jnp.zeros_like(s_ref)

        heads = range(GDN_GROUP)
        s = tuple(s_ref[g] for g in heads)
        out, s_new, inv_c = _gdn_chunks(
            tuple(q_ref[:, _head_lanes(g, 2)] for g in heads), tuple(k_ref[:, _head_lanes(g, 2)] for g in heads),
            tuple(v_ref[:, _head_lanes(g)] for g in heads), be_ref[:, 0, :], gr_ref[:, 0, :], s)
        for g in heads:
            ss_ref[g] = s[g]
            o_ref[:, _head_lanes(g)] = out[g]
            inv_ref[g] = inv_c[g]
            s_ref[g] = s_new[g]

    return pl.pallas_call(
        body, grid=(GDN_V_HEADS // GDN_GROUP, nc), in_specs=[q, k, v, rw, rw], out_specs=[o, st, inv],
        out_shape=[SDS((t, 2 * D_MODEL), F32), SDS((GDN_V_HEADS, nc, HEAD, HEAD), F32), SDS((GDN_V_HEADS, nc, GDN_CHUNK, GDN_CHUNK), F32)],
        scratch_shapes=[pltpu.VMEM((GDN_GROUP, HEAD, HEAD), F32)], name="gdn_rec", compiler_params=_params(2),
    )(qk, qk, c, beta_row, gam_row)


def _gdn_rec_bwd(qk, c, beta_row, gam_row, ss, invs, do):
    t = qk.shape[0]
    nc = t // GDN_CHUNK
    q, k, v, o, rw, st, inv = _gdn_specs(nc, True)

    def body(q_ref, k_ref, v_ref, be_ref, gr_ref, ss_ref, inv_ref, do_ref,
             dq_ref, dk_ref, dv_ref, dbe_ref, dgr_ref, ds_ref):
        @pl.when(pl.program_id(1) == 0)
        def _():
            ds_ref[...] = jnp.zeros_like(ds_ref)

        heads = range(GDN_GROUP)
        _, vjp = jax.vjp(
            _gdn_chunks,
            tuple(q_ref[:, _head_lanes(g, 2)] for g in heads), tuple(k_ref[:, _head_lanes(g, 2)] for g in heads),
            tuple(v_ref[:, _head_lanes(g)] for g in heads), be_ref[:, 0, :], gr_ref[:, 0, :],
            tuple(ss_ref[g] for g in heads), tuple(inv_ref[g] for g in heads))
        no_inv_ct = tuple(jnp.zeros((GDN_CHUNK, GDN_CHUNK), F32) for g in heads)
        dq, dk, dv, dbe, dgr, ds, _ = vjp((tuple(do_ref[:, _head_lanes(g)] for g in heads), tuple(ds_ref[g] for g in heads), no_inv_ct))
        for g in heads:
            dq_ref[:, _head_lanes(g)] = dq[g].astype(dq_ref.dtype)
            dk_ref[:, _head_lanes(g)] = dk[g].astype(dk_ref.dtype)
            dv_ref[:, _head_lanes(g)] = dv[g].astype(dv_ref.dtype)
            ds_ref[g] = ds[g]
        dbe_ref[:, 0, :] = dbe
        dgr_ref[:, 0, :] = dgr

    wide = SDS((t, 2 * D_MODEL), BF16)
    rowshape = SDS((GDN_V_HEADS, nc, 1, GDN_CHUNK), F32)
    return pl.pallas_call(
        body, grid=(GDN_V_HEADS // GDN_GROUP, nc), in_specs=[q, k, v, rw, rw, st, inv, o], out_specs=[o, o, o, rw, rw],
        out_shape=[wide, wide, wide, rowshape, rowshape],
        scratch_shapes=[pltpu.VMEM((GDN_GROUP, HEAD, HEAD), F32)], name="gdn_rec_bwd", compiler_params=_params(2),
    )(qk, qk, c, beta_row, gam_row, ss, invs, do)


def _gated_norm(o, gate, w):
    return _rms(o, w) * _silu(gate)


def _post_fwd(o, proj, col_off, w, name):
    t, width = o.shape
    tt = _tile(t, (256, 128))

    def fn(o, gate, w):
        return jnp.concatenate([_gated_norm(_head(o, h), _head(gate, h), w) for h in range(width // HEAD)], axis=1)

    return _tmap(fn, (1, t // tt),
                 [(o, *_rows(width, tt)), (proj, *_rows(width, tt, col_off * HEAD // width)), (w, (1, HEAD), lambda j, i: (0, 0))],
                 [((t, width), BF16, *_rows(width, tt), None)], name)[0]


def _post_bwd(o, proj, col_off, w, dout, name):
    t, width = o.shape
    tt = _tile(t, (256, 128))

    def fn(o, gate, w, dout):
        do, dgate, dw = [], [], jnp.zeros((1, HEAD), F32)
        for h in range(width // HEAD):
            _, vjp = jax.vjp(_gated_norm, _head(o, h), _head(gate, h), w)
            a, b, c = vjp(_head(dout, h).astype(F32))
            do.append(a)
            dgate.append(b)
            dw = dw + c
        return jnp.concatenate(do, axis=1), jnp.concatenate(dgate, axis=1), dw

    r = _rows(width, tt)
    return _tmap(fn, (1, t // tt),
                 [(o, *r), (proj, *_rows(width, tt, col_off * HEAD // width)), (w, (1, HEAD), lambda j, i: (0, 0)), (dout, *r)],
                 [((t, width), F32, *r, None), ((t, width), BF16, *r, None), ((1, HEAD), F32, (1, HEAD), lambda j, i: (0, 0), "inner")], name)


def _merge(gate_h, gate_g, yh, yg):
    return _sigmoid(gate_h) * yh + _sigmoid(gate_g) * yg


def _merge_fwd(proj, yh, yg):
    t = yh.shape[0]
    tt, ft = _tile(t, (256, 128)), 512
    r = _rows(ft, tt)
    return _tmap(_merge, (D_MODEL // ft, t // tt),
                 [(proj, *_rows(ft, tt, COL_GATE_H * HEAD // ft)), (proj, *_rows(ft, tt, COL_GATE_G * HEAD // ft)), (yh, *r), (yg, *r)],
                 [((t, D_MODEL), BF16, *r, None)], "merge")[0]


def _merge_bwd(proj, yh, yg, dy):
    t = yh.shape[0]
    tt, ft = _tile(t, (256, 128)), 512
    r = _rows(ft, tt)

    def fn(gate_h, gate_g, yh, yg, dy):
        _, vjp = jax.vjp(_merge, gate_h, gate_g, yh, yg)
        return vjp(dy.astype(F32))

    o = ((t, D_MODEL), BF16, *r, None)
    return _tmap(fn, (D_MODEL // ft, t // tt),
                 [(proj, *_rows(ft, tt, COL_GATE_H * HEAD // ft)), (proj, *_rows(ft, tt, COL_GATE_G * HEAD // ft)), (yh, *r), (yg, *r), (dy, *r)],
                 [o, o, o, o], "merge_bwd")


def _loss_head(h, target, g):
    t, d = h.shape
    tt = _tile(t, (256, 128))

    def fn(h, target, g):
        def f(h, g):
            err = _rms(h, g) - target
            return 0.5 * jnp.sum(jnp.mean(err * err, axis=-1))

        loss, (dh, dg) = jax.value_and_grad(f, (0, 1))(h, g)
        return dh, dg, jnp.full((1, HEAD), loss, F32)

    return _tmap(fn, (1, t // tt), [(h, *_rows(d, tt)), (target, *_rows(d, tt)), (g, (1, d), lambda j, i: (0, 0))],
                 [((t, d), F32, *_rows(d, tt), None), ((1, d), F32, (1, d), lambda j, i: (0, 0), "inner"),
                  ((1, HEAD), F32, (1, HEAD), lambda j, i: (0, 0), "inner")], "loss_head")


def _mixer_fwd(h, p, links):
    t = h.shape[0]
    nc = t // GDN_CHUNK
    u = _rms_fwd(h, p["mix_norm"], "mix_norm")
    w = {n: links.weight(n, h) for n in ("w_in_t", "w_in_b_t", "w_in_ab_t", "conv_w")}
    proj = _mm(u, w["w_in_t"], "nt", F32, "mix_in", after=links.started, b_rows=SCALAR_ROWS)
    proj_b = _mm(u, w["w_in_b_t"], "nt", F32, "mix_in_b")
    pab = _mm(u, w["w_in_ab_t"], "nt", F32, "mix_in_ab")
    qh, kh, bh = _hgrn_prep_fwd(proj, p["lbl"])
    oh, hs = _hgrn_rec_fwd(qh, kh, proj, bh)
    c, conv_y = _conv_fwd(proj, w["conv_w"])
    qk = _qk_norm_fwd(c)
    gates_t = _gates_fwd(pab, p["alog"], p["dtb"])
    gam_row = gates_t[:GDN_V_HEADS].reshape(GDN_V_HEADS, nc, 1, GDN_CHUNK)
    beta_row = gates_t[GDN_V_HEADS:2 * GDN_V_HEADS].reshape(GDN_V_HEADS, nc, 1, GDN_CHUNK)
    og, ss, invs = _gdn_rec_fwd(qk, c, beta_row, gam_row)
    ohn = _post_fwd(oh, proj, COL_HG, p["hgrn_out_norm"], "hgrn_out")
    ogn = _post_fwd(og, proj_b, COL_GZ, p["gdn_out_norm"], "gdn_out")
    w.update({n: links.weight(n, ogn) for n in ("w_branch_hgrn", "w_branch_gdn", "w_out")})
    yh = _mm(ohn, w["w_branch_hgrn"], "nn", BF16, "branch_hgrn")
    yg = _mm(ogn, w["w_branch_gdn"], "nn", BF16, "branch_gdn")
    y = _merge_fwd(proj_b, yh, yg)
    out = _mm(y, w["w_out"], "nn", F32, "mix_out", res=h)
    saved = (w, u, proj, proj_b, pab, qh, kh, bh, oh, hs, c, conv_y, qk, beta_row, gam_row, og, ss, invs, ohn, ogn, yh, yg, y)
    return out, saved


def _mixer_bwd(h, p, links, saved, dout):
    (w, u, proj, proj_b, pab, qh, kh, bh, oh, hs, c, conv_y, qk, beta_row, gam_row, og, ss, invs, ohn, ogn, yh, yg, y) = saved
    t = h.shape[0]
    grads = {}
    dw_out = _mm(y, dout, "tn", BF16, "mix_out_dw")
    dy = _mm(dout, w["w_out"], "nt", BF16, "mix_out_dx")
    dgate_h, dgate_g, dyh, dyg = _merge_bwd(proj_b, yh, yg, dy)
    dw_bh = _mm(ohn, dyh, "tn", BF16, "branch_hgrn_dw")
    dw_bg = _mm(ogn, dyg, "tn", BF16, "branch_gdn_dw")
    sent = links.send({"w_out": dw_out, "w_branch_hgrn": dw_bh, "w_branch_gdn": dw_bg})
    dohn = _mm(dyh, w["w_branch_hgrn"], "nt", BF16, "branch_hgrn_dx", after=sent)
    dogn = _mm(dyg, w["w_branch_gdn"], "nt", BF16, "branch_gdn_dx")
    doh, dhg, grads["hgrn_out_norm"] = _post_bwd(oh, proj, COL_HG, p["hgrn_out_norm"], dohn, "hgrn_out_bwd")
    dog, dgz, grads["gdn_out_norm"] = _post_bwd(og, proj_b, COL_GZ, p["gdn_out_norm"], dogn, "gdn_out_bwd")
    dqh, dkh, dhi, dbh = _hgrn_rec_bwd(qh, kh, proj, bh, hs, doh)
    dhq, dhf, grads["lbl"] = _hgrn_prep_bwd(proj, p["lbl"], dqh, dkh, dbh)
    dqv, dkv, dcv, dbeta_row, dgam_row = _gdn_rec_bwd(qk, c, beta_row, gam_row, ss, invs, dog)
    dcqk = _qk_norm_bwd(c, dqv, dkv)
    dxin, grads["conv_w"] = _conv_bwd(proj, w["conv_w"], conv_y, dcqk, dcv)
    dgates_t = jnp.concatenate([dgam_row.reshape(GDN_V_HEADS, t), dbeta_row.reshape(GDN_V_HEADS, t),
                                jnp.zeros((HEAD - 2 * GDN_V_HEADS, t), F32)], axis=0)
    dpab, grads["alog"], grads["dtb"] = _gates_bwd(pab, p["alog"], p["dtb"], dgates_t)
    front, back = [dhq, dhf, dhi, dhg, dxin], [dgz, dgate_h, dgate_g]
    dw_front = [_mm(d, u, "tn", BF16, "mix_in_dw_%d" % i) for i, d in enumerate(front)]
    dw_back = [_mm(d, u, "tn", BF16, "mix_in_b_dw_%d" % i) for i, d in enumerate(back)]
    dw_ab_t = _mm(dpab, u, "tn", BF16, "mix_in_ab_dw")
    sent = links.send({"w_in": jnp.concatenate(dw_front + [dw_ab_t[:N_SCALAR]] + dw_back, axis=0)})
    du = _mm_pieces(front, w["w_in_t"], "mix_in_dx", after=sent)
    du = _mm_pieces(back, w["w_in_b_t"], "mix_in_b_dx", res=du)
    du = _mm(dpab, w["w_in_ab_t"], "nn", F32, "mix_in_ab_dx", res=du)
    dh, grads["mix_norm"] = _rms_bwd(h, p["mix_norm"], du, dout, "mix_norm_bwd")
    return dh, grads


def _local_step(x, target, p, links):
    def ffn_weights(tag, behind):
        def get(n):
            w_in_t, w_out = links.weight(tag + "_w_in", n), links.weight(tag + "_w_out", n)
            return w_in_t, w_out, links.started if behind else None
        return get

    h1, s1 = _ffn_fwd(x, p["ffn1_norm"] + links.started[0, 0], ffn_weights("ffn1", True), "ffn1")
    h2, sm = _mixer_fwd(h1, p, links)
    h3, s2 = _ffn_fwd(h2, p["ffn2_norm"], ffn_weights("ffn2", False), "ffn2")
    dh3, dfinal, loss = _loss_head(h3, target, p["final_norm"])
    g = {"final_norm": dfinal}
    dh2, g["ffn2_norm"] = _ffn_bwd(h2, p["ffn2_norm"], s2, dh3, "ffn2", links)
    dh1, gm = _mixer_bwd(h1, p, links, sm, dh2)
    g.update(gm)
    dx, g["ffn1_norm"] = _ffn_bwd(x, p["ffn1_norm"], s1, dh1, "ffn1", links)
    return loss, dx, g


HBM_SPEC = pl.BlockSpec(memory_space=pltpu.HBM)
SEM_SPEC = pl.BlockSpec(memory_space=pltpu.SEMAPHORE)
DATAFLOW = pltpu.SideEffectType.DATAFLOW_SIDE_EFFECTING


def _position():
    x, y, c = lax.axis_index("x"), lax.axis_index("y"), lax.axis_index("c")
    return x, y, c, 4 * x + 2 * y + c


def _relations(x, y, c):
    for rel in range(1, N_DEV):
        px = 1 - x if rel & 4 else x
        py = 1 - y if rel & 2 else y
        pc = 1 - c if rel & 1 else c
        yield rel, (px, py, pc), 4 * px + 2 * py + pc


def _sem_index(item, rel):
    return item * (N_DEV - 1) + rel - 1


def _landing(a, mode):
    return lax.empty((N_DEV,) + a.shape if mode == "gather" else a.shape, a.dtype)


ALL_PEERS = tuple(range(1, N_DEV))
ONE_PER_CHIP = (1, 2, 4, 6)


def _copies_start(groups, name, rels=ALL_PEERS):
    flat = [item for grp in groups for item in grp]
    n, ng = len(flat), len(groups)
    lands = [_landing(a, mode) for a, mode in flat]

    def body(*refs):
        src_refs, land_refs, sems, token = refs[:n], refs[n:2 * n], refs[2 * n:2 * n + 2 * ng], refs[-1]
        x, y, c, me = _position()
        for rel, where, peer in _relations(x, y, c):
            if rel not in rels:
                continue
            k = 0
            for gi, grp in enumerate(groups):
                for li, (_, mode) in enumerate(grp):
                    src = src_refs[k] if mode == "gather" else src_refs[k].at[peer]
                    pltpu.make_async_remote_copy(src_ref=src, dst_ref=land_refs[k].at[me], send_sem=sems[2 * gi].at[_sem_index(li, rel)],
                                                 recv_sem=sems[2 * gi + 1].at[_sem_index(li, rel)], device_id=where, device_id_type=MESH_IDS).start()
                    k += 1
        token[...] = jnp.zeros_like(token)

    sem_shapes = [pltpu.SemaphoreType.DMA((len(grp) * (N_DEV - 1),)) for grp in groups for _ in range(2)]
    thru = [pltpu.HBM(a.shape, a.dtype) for a, _ in flat] + [pltpu.HBM(l.shape, l.dtype) for l in lands]
    outs = pl.pallas_call(
        body, name=name, out_shape=(*sem_shapes, *thru, SDS((8, HEAD), F32)),
        in_specs=[HBM_SPEC] * (2 * n), out_specs=(*[SEM_SPEC] * (2 * ng), *[HBM_SPEC] * (2 * n), pl.BlockSpec(memory_space=pltpu.VMEM)),
        input_output_aliases={i: 2 * ng + i for i in range(2 * n)}, compiler_params=pltpu.CompilerParams(has_side_effects=DATAFLOW),
    )(*[pltpu.with_memory_space_constraint(a, pltpu.HBM) for a, _ in flat], *[pltpu.with_memory_space_constraint(l, pltpu.HBM) for l in lands])
    sems, srcs, landed, token = outs[:2 * ng], outs[2 * ng:2 * ng + n], outs[2 * ng + n:2 * ng + 2 * n], outs[-1]
    result, k = [], 0
    for gi, grp in enumerate(groups):
        result.append((sems[2 * gi], sems[2 * gi + 1], srcs[k:k + len(grp)], landed[k:k + len(grp)]))
        k += len(grp)
    return result, token


def _copies_wait(started, modes, after, name, rels=ALL_PEERS):
    send_sems, recv_sems, srcs, lands = started
    n = len(srcs)

    def body(*refs):
        src_refs, land_refs, ssem, rsem, token = refs[:n], refs[n:2 * n], refs[2 * n], refs[2 * n + 1], refs[-1]
        x, y, c, _ = _position()
        for rel in rels:
            for i, mode in enumerate(modes):
                src = src_refs[i] if mode == "gather" else src_refs[i].at[0]
                cp = pltpu.make_async_remote_copy(src_ref=src, dst_ref=land_refs[i].at[0], send_sem=ssem.at[_sem_index(i, rel)],
                                                  recv_sem=rsem.at[_sem_index(i, rel)], device_id=(x, y, c), device_id_type=MESH_IDS)
                cp.wait_send()
                cp.wait_recv()
        token[...] = jnp.zeros_like(token)

    outs = pl.pallas_call(
        body, name=name, out_shape=[pltpu.HBM(a.shape, a.dtype) for a in (*srcs, *lands)] + [SDS((8, HEAD), F32)],
        in_specs=[HBM_SPEC] * (2 * n) + [SEM_SPEC, SEM_SPEC, pl.BlockSpec(memory_space=pl.ANY)],
        out_specs=[HBM_SPEC] * (2 * n) + [pl.BlockSpec(memory_space=pltpu.VMEM)],
        input_output_aliases={i: i for i in range(2 * n)}, compiler_params=pltpu.CompilerParams(has_side_effects=DATAFLOW),
    )(*srcs, *lands, send_sems, recv_sems, after)
    return outs[:n], outs[n:2 * n], outs[-1]


OTHER_CHIPS = ((1, 0), (0, 1), (1, 1))


def _pass_on_start(lands, name):
    n = len(lands)

    def body(*refs):
        land_refs, ssem, rsem, token = refs[:n], refs[n], refs[n + 1], refs[-1]
        x, y, c, _ = _position()
        for j, (fx, fy) in enumerate(OTHER_CHIPS):
            slot = 4 * (1 - x if fx else x) + 2 * (1 - y if fy else y) + c
            for i in range(n):
                pltpu.make_async_remote_copy(src_ref=land_refs[i].at[slot], dst_ref=land_refs[i].at[slot], send_sem=ssem.at[i * len(OTHER_CHIPS) + j],
                                             recv_sem=rsem.at[i * len(OTHER_CHIPS) + j], device_id=(x, y, 1 - c), device_id_type=MESH_IDS).start()
        token[...] = jnp.zeros_like(token)

    sems = pltpu.SemaphoreType.DMA((n * len(OTHER_CHIPS),))
    outs = pl.pallas_call(
        body, name=name, out_shape=(sems, sems, *[pltpu.HBM(l.shape, l.dtype) for l in lands], SDS(TOKEN, F32)),
        in_specs=[HBM_SPEC] * n, out_specs=(SEM_SPEC, SEM_SPEC, *[HBM_SPEC] * n, pl.BlockSpec(memory_space=pltpu.VMEM)),
        input_output_aliases={i: 2 + i for i in range(n)}, compiler_params=pltpu.CompilerParams(has_side_effects=DATAFLOW),
    )(*lands)
    return (outs[0], outs[1], outs[2:2 + n]), outs[-1]


def _pass_on_wait(started, after, name):
    send_sems, recv_sems, lands = started
    n = len(lands)

    def body(*refs):
        land_refs, ssem, rsem = refs[:n], refs[n], refs[n + 1]
        x, y, c, _ = _position()
        for j in range(len(OTHER_CHIPS)):
            for i in range(n):
                cp = pltpu.make_async_remote_copy(src_ref=land_refs[i].at[0], dst_ref=land_refs[i].at[0], send_sem=ssem.at[i * len(OTHER_CHIPS) + j],
                                                  recv_sem=rsem.at[i * len(OTHER_CHIPS) + j], device_id=(x, y, c), device_id_type=MESH_IDS)
                cp.wait_send()
                cp.wait_recv()

    return pl.pallas_call(
        body, name=name, out_shape=[pltpu.HBM(l.shape, l.dtype) for l in lands],
        in_specs=[HBM_SPEC] * n + [SEM_SPEC, SEM_SPEC, pl.BlockSpec(memory_space=pl.ANY)], out_specs=[HBM_SPEC] * n,
        input_output_aliases={i: i for i in range(n)}, compiler_params=pltpu.CompilerParams(has_side_effects=DATAFLOW),
    )(*lands, send_sems, recv_sems, after)


WEIGHT_GROUPS = (("ffn1_w_in", "ffn1_w_out", "gdn_conv_w"), ("w_in",), ("w_branch_hgrn", "w_branch_gdn", "w_out", "ffn2_w_in", "ffn2_w_out"))
GROUP_RELS = (ONE_PER_CHIP, ONE_PER_CHIP, ALL_PEERS)


class _Links:
    def __init__(self, shards, me):
        self.me = me
        self.shards = shards
        self.weights = {}
        self.sends = []
        self.gathers = {}
        self.started = None
        self._start_gather(0, None)

    def _start_gather(self, gi, zeros):
        if gi < len(WEIGHT_GROUPS):
            items = [(self.shards[n] if zeros is None else self.shards[n] + zeros[0, 0].astype(self.shards[n].dtype), "gather")
                     for n in WEIGHT_GROUPS[gi]]
            started, self.started = _copies_start([items], "gather_start_%d" % gi, GROUP_RELS[gi])
            self.gathers[gi] = started[0]

    def weight(self, name, after):
        if name not in self.weights:
            source = {"w_in_t": "w_in", "w_in_b_t": "w_in", "w_in_ab_t": "w_in", "conv_w": "gdn_conv_w"}.get(name, name)
            gi = [i for i, grp in enumerate(WEIGHT_GROUPS) if source in grp][0]
            assert gi in self.gathers, "weight groups are asked for in order"
            srcs, lands, zero = _copies_wait(self.gathers[gi], ["gather"] * len(WEIGHT_GROUPS[gi]), after, "gather_wait_%d" % gi, GROUP_RELS[gi])
            if GROUP_RELS[gi] == ONE_PER_CHIP:
                passing, zero = _pass_on_start(lands, "gather_pass_%d" % gi)
                self._start_gather(gi + 1, zero)
                lands = _pass_on_wait(passing, self.started, "gather_passed_%d" % gi)
            else:
                self._start_gather(gi + 1, zero)
            for n, src, land in zip(WEIGHT_GROUPS[gi], srcs, lands):
                full = lax.dynamic_update_index_in_dim(land, src, self.me, 0)
                if n == "gdn_conv_w":
                    self.weights["conv_w"] = full.reshape(N_DEV, CONV_K, 4 * D_MODEL // N_DEV).transpose(1, 0, 2).reshape(CONV_K, 4 * D_MODEL)
                elif n == "w_in":
                    self.weights.update(_w_in_pieces(full.reshape(-1, D_MODEL)))
                else:
                    self.weights[n] = full.reshape(-1, D_MODEL)
        return self.weights[name]

    def send(self, grads):
        names = list(grads)
        blocks = [grads[n].reshape(N_DEV, -1, D_MODEL) for n in names]
        started, token = _copies_start([[(b, "scatter") for b in blocks]], "send_" + names[0])
        self.sends.append((names, started[0]))
        return token

    def landed(self, after):
        out = {}
        for names, started in self.sends:
            srcs, lands, _ = _copies_wait(started, ["scatter"] * len(names), after, "landed_" + names[0])
            for n, src, land in zip(names, srcs, lands):
                out[n] = lax.dynamic_update_index_in_dim(land, lax.dynamic_index_in_dim(src, self.me, 0, keepdims=False), self.me, 0)
        return out


def _adam(parts, w, m, v, name):
    n_parts, r, c = parts.shape
    tc = c if c <= 512 else (256 if r > 1024 else 512)

    def body(p_ref, w_ref, m_ref, v_ref, g_ref, d_ref, mo_ref, vo_ref):
        g = p_ref[0].astype(F32)
        for i in range(1, n_parts):
            g = g + p_ref[i].astype(F32)
        m_new = ADAM_B1 * m_ref[...] + (1.0 - ADAM_B1) * g
        v_new = ADAM_B2 * v_ref[...] + (1.0 - ADAM_B2) * (g * g)
        m_hat = m_new / (1.0 - ADAM_B1 ** ADAM_STEP)
        v_hat = v_new / (1.0 - ADAM_B2 ** ADAM_STEP)
        g_ref[...] = g
        d_ref[...] = -ADAM_LR * (m_hat / (jnp.sqrt(v_hat) + ADAM_EPS) + ADAM_WD * w_ref[...])
        mo_ref[...] = m_new
        vo_ref[...] = v_new

    spec = pl.BlockSpec((r, tc), lambda j: (0, j))
    return pl.pallas_call(
        body, grid=(c // tc,), in_specs=[pl.BlockSpec((n_parts, r, tc), lambda j: (0, 0, j)), spec, spec, spec],
        out_specs=[spec] * 4, out_shape=[SDS((r, c), F32)] * 4, name=name, compiler_params=_params(1),
    )(parts, w, m, v)


BIG = ("ffn1_w_in", "ffn1_w_out", "w_in", "w_branch_hgrn", "w_branch_gdn", "w_out", "ffn2_w_in", "ffn2_w_out")


TRANSPOSED = ("ffn1_w_in", "w_in", "ffn2_w_in")


def _shard_rows(name, shard):
    return shard.T if name in TRANSPOSED else shard


SCALAR_ROWS = 8192
N_SCALAR = 2 * GDN_V_HEADS


def _w_in_pieces(w_in_t):
    return {"w_in_t": w_in_t, "w_in_b_t": w_in_t[SCALAR_ROWS + N_SCALAR:],
            "w_in_ab_t": jnp.pad(w_in_t[SCALAR_ROWS:SCALAR_ROWS + N_SCALAR], ((0, HEAD - N_SCALAR), (0, 0)))}


def _pad_lanes(a, width=HEAD):
    return jnp.pad(a, ((0, 0), (0, width - a.shape[1])))


SMALL_ROWS = 24


def _pack_small(g, loss):
    row6 = jnp.concatenate([g["hgrn_out_norm"], g["gdn_out_norm"], g["alog"], g["dtb"], loss,
                            jnp.zeros((1, D_MODEL - 5 * HEAD), F32)], axis=1)
    return jnp.concatenate([g["ffn1_norm"], g["mix_norm"], g["lbl"], g["ffn2_norm"], g["final_norm"], row6,
                            jnp.zeros((1, D_MODEL), F32), g["conv_w"].reshape(4 * CONV_K, D_MODEL)], axis=0)


def _pack_small_state(a):
    row6 = jnp.concatenate([a["hgrn_out_norm"], a["gdn_out_norm"], _pad_lanes(a["gdn_a_log"]), _pad_lanes(a["gdn_dt_bias"]),
                            jnp.zeros((1, D_MODEL - 4 * HEAD), F32)], axis=1)
    return jnp.concatenate([a["ffn1_norm"], a["mix_norm"], a["hgrn_lb_logits"], a["ffn2_norm"], a["final_norm"].reshape(1, D_MODEL),
                            row6, jnp.zeros((1, D_MODEL), F32)], axis=0)


def _unpack_small(a):
    return {"ffn1_norm": a[0:1], "mix_norm": a[1:2], "hgrn_lb_logits": a[2:4], "ffn2_norm": a[4:5], "final_norm": a[5],
            "hgrn_out_norm": a[6:7, :HEAD], "gdn_out_norm": a[6:7, HEAD:2 * HEAD],
            "gdn_a_log": a[6:7, 2 * HEAD:2 * HEAD + GDN_V_HEADS], "gdn_dt_bias": a[6:7, 3 * HEAD:3 * HEAD + GDN_V_HEADS]}


NAMES = ("ffn1_norm", "ffn1_w_in", "ffn1_w_out", "mix_norm", "w_in", "hgrn_lb_logits", "hgrn_out_norm", "gdn_conv_w", "gdn_a_log",
         "gdn_dt_bias", "gdn_out_norm", "w_branch_hgrn", "w_branch_gdn", "w_out", "ffn2_norm", "ffn2_w_in", "ffn2_w_out", "final_norm")


def kernel(x, ffn1_norm, ffn1_w_in, ffn1_w_out, mix_norm, w_in, hgrn_lb_logits, hgrn_out_norm, gdn_conv_w, gdn_a_log, gdn_dt_bias, gdn_out_norm, w_branch_hgrn, w_branch_gdn, w_out, ffn2_norm, ffn2_w_in, ffn2_w_out, final_norm, loss_target, m_ffn1_norm, m_ffn1_w_in, m_ffn1_w_out, m_mix_norm, m_w_in, m_hgrn_lb_logits, m_hgrn_out_norm, m_gdn_conv_w, m_gdn_a_log, m_gdn_dt_bias, m_gdn_out_norm, m_w_branch_hgrn, m_w_branch_gdn, m_w_out, m_ffn2_norm, m_ffn2_w_in, m_ffn2_w_out, m_final_norm, v_ffn1_norm, v_ffn1_w_in, v_ffn1_w_out, v_mix_norm, v_w_in, v_hgrn_lb_logits, v_hgrn_out_norm, v_gdn_conv_w, v_gdn_a_log, v_gdn_dt_bias, v_gdn_out_norm, v_w_branch_hgrn, v_w_branch_gdn, v_w_out, v_ffn2_norm, v_ffn2_w_in, v_ffn2_w_out, v_final_norm):
    wts = dict(zip(NAMES, (ffn1_norm, ffn1_w_in, ffn1_w_out, mix_norm, w_in, hgrn_lb_logits, hgrn_out_norm, gdn_conv_w, gdn_a_log,
                           gdn_dt_bias, gdn_out_norm, w_branch_hgrn, w_branch_gdn, w_out, ffn2_norm, ffn2_w_in, ffn2_w_out, final_norm)))
    mom = dict(zip(NAMES, (m_ffn1_norm, m_ffn1_w_in, m_ffn1_w_out, m_mix_norm, m_w_in, m_hgrn_lb_logits, m_hgrn_out_norm, m_gdn_conv_w,
                           m_gdn_a_log, m_gdn_dt_bias, m_gdn_out_norm, m_w_branch_hgrn, m_w_branch_gdn, m_w_out, m_ffn2_norm, m_ffn2_w_in,
                           m_ffn2_w_out, m_final_norm)))
    var = dict(zip(NAMES, (v_ffn1_norm, v_ffn1_w_in, v_ffn1_w_out, v_mix_norm, v_w_in, v_hgrn_lb_logits, v_hgrn_out_norm, v_gdn_conv_w,
                           v_gdn_a_log, v_gdn_dt_bias, v_gdn_out_norm, v_w_branch_hgrn, v_w_branch_gdn, v_w_out, v_ffn2_norm, v_ffn2_w_in,
                           v_ffn2_w_out, v_final_norm)))
    me = 4 * lax.axis_index("x") + 2 * lax.axis_index("y") + lax.axis_index("c")

    conv_shard = wts["gdn_conv_w"][0]
    shards = {n: _shard_rows(n, wts[n][0]).astype(BF16) for n in BIG}
    shards["gdn_conv_w"] = conv_shard.reshape(2, D_MODEL)
    links = _Links(shards, me)
    p = {"ffn1_norm": wts["ffn1_norm"], "mix_norm": wts["mix_norm"], "ffn2_norm": wts["ffn2_norm"], "final_norm": wts["final_norm"].reshape(1, D_MODEL),
         "lbl": wts["hgrn_lb_logits"], "hgrn_out_norm": wts["hgrn_out_norm"], "gdn_out_norm": wts["gdn_out_norm"],
         "alog": _pad_lanes(wts["gdn_a_log"]), "dtb": _pad_lanes(wts["gdn_dt_bias"])}

    loss, dx, g = _local_step(x[0], loss_target[0], p, links)

    small_started, small_token = _copies_start([[(_pack_small(g, loss), "gather")]], "small_start")
    landed = links.landed(small_token)

    big = [{} for _ in range(4)]
    for n in BIG:
        res = _adam(landed[n], _shard_rows(n, wts[n][0]), _shard_rows(n, mom[n][0]), _shard_rows(n, var[n][0]), "adam_" + n)
        for kind in range(4):
            big[kind][n] = _shard_rows(n, res[kind])
    small_srcs, small_lands, _ = _copies_wait(small_started[0], ["gather"], res[0], "small_wait")
    small_parts = lax.dynamic_update_index_in_dim(small_lands[0], small_srcs[0], me, 0)
    n_vec = SMALL_ROWS - 4 * CONV_K
    small_raw = _adam(small_parts[:, :n_vec], _pack_small_state(wts), _pack_small_state(mom), _pack_small_state(var), "adam_small")
    small = [_unpack_small(o) for o in small_raw]
    loss_total = small_raw[0][6, 4 * HEAD]
    conv_parts = small_parts[:, n_vec:].reshape(N_DEV, CONV_K, 4 * D_MODEL)
    width = 4 * D_MODEL // N_DEV
    conv_mine = lax.dynamic_slice_in_dim(conv_parts, me * width, width, axis=2)
    conv = _adam(conv_mine, conv_shard, mom["gdn_conv_w"][0], var["gdn_conv_w"][0], "adam_conv")

    outs = []
    for kind in range(4):
        for n in NAMES:
            if n in BIG:
                outs.append(big[kind][n][None])
            elif n == "gdn_conv_w":
                outs.append(conv[kind][None])
            else:
                outs.append(small[kind][n])
    return (loss_total, dx[None], *outs)
```

```python
import functools

import jax
import jax.numpy as jnp
from jax import lax
from jax.experimental import pallas as pl
from jax.experimental.pallas import tpu as pltpu

F32 = jnp.float32
BF16 = jnp.bfloat16
MESH_IDS = pl.DeviceIdType.MESH

D_MODEL = 1024
D_FF = 2816
N_DEV = 8
EPS = 1e-6
HEAD = 128
HG_HEADS = 8
GDN_QK_HEADS = 8
GDN_V_HEADS = 16
GDN_CHUNK = 64
HG_CHUNK = 16
CONV_K = 4
LANES = 128
COL_HQ, COL_HF, COL_HI, COL_HG, COL_GQ, COL_GK, COL_GV = 0, 8, 16, 24, 32, 40, 48
COL_GZ, COL_GATE_H, COL_GATE_G = 0, 16, 24
VMEM_LIMIT = 56 * 1024 * 1024

ADAM_LR, ADAM_B1, ADAM_B2, ADAM_EPS, ADAM_WD, ADAM_STEP = 0.001, 0.9, 0.999, 1e-08, 0.01, 10

SDS = jax.ShapeDtypeStruct


def _params(n_axes):
    return pltpu.CompilerParams(dimension_semantics=("arbitrary",) * n_axes, vmem_limit_bytes=VMEM_LIMIT)


def _tile(n, candidates=(512, 384, 256, 128, 64, 32, 16, 8)):
    for c in candidates:
        if n % c == 0:
            return c
    return n


_DIMS = {"nn": ((1,), (0,)), "nt": ((1,), (1,)), "tn": ((0,), (0,))}


def _bdot_raw(a, b, dims):
    return lax.dot_general(a.astype(BF16), b.astype(BF16), (_DIMS[dims], ((), ())), preferred_element_type=F32)


@functools.partial(jax.custom_vjp, nondiff_argnums=(2,))
def _bdot(a, b, dims):
    return _bdot_raw(a, b, dims)


def _bdot_fwd(a, b, dims):
    return _bdot_raw(a, b, dims), (a, b)


def _bdot_bwd(dims, res, ct):
    a, b = res
    if dims == "nn":
        return _bdot_raw(ct, b, "nt"), _bdot_raw(a, ct, "tn")
    if dims == "nt":
        return _bdot_raw(ct, b, "nn"), _bdot_raw(ct, a, "tn")
    return _bdot_raw(b, ct, "nt"), _bdot_raw(a, ct, "nn")


_bdot.defvjp(_bdot_fwd, _bdot_bwd)


def _hdot_raw(ones, x):
    hi = x.astype(BF16)
    rest = x - hi.astype(F32)
    mid = rest.astype(BF16)
    low = (rest - mid.astype(F32)).astype(BF16)
    return _bdot_raw(ones, hi, "nn") + (_bdot_raw(ones, mid, "nn") + _bdot_raw(ones, low, "nn"))


MM_VMEM_BUDGET = 38 * 1024 * 1024
TOKEN = (8, HEAD)


def _mm_tiles(m, n, k, a_bytes, b_bytes, o_bytes, r_bytes, m_align=8):
    def need(tm, tn, tk):
        return 2 * (tm * tk * a_bytes + tk * tn * b_bytes + tm * tn * (o_bytes + r_bytes)) + (tm * tn * 4 if tk < k else 0)

    def shrink(tm, tn, tk, floor_m, floor_n):
        while need(tm, tn, tk) > MM_VMEM_BUDGET:
            if tn > floor_n and tn % 256 == 0 and tn >= tm:
                tn //= 2
            elif tm > floor_m and tm % (2 * m_align) == 0:
                tm //= 2
            elif tn > floor_n and tn % 256 == 0:
                tn //= 2
            else:
                return None
        return tm, tn, tk

    tm = _tile(m, (1408, 1024, 704, 512, 256, 128, 64, 32, 16, 8))
    tn = _tile(n, (1408, 1024, 512, 256, 128))
    whole = shrink(tm, tn, k, min(tm, 1024), min(tn, 512))
    if whole is not None:
        return whole
    tk = _tile(k, (2048, 1408, 1024, 512, 256, 128, 64, 32, 16, 8))
    while True:
        fit = shrink(tm, tn, tk, min(tm, 256), min(tn, 512))
        if fit is not None or tk <= 512 or tk % 256:
            return fit if fit is not None else (tm, tn, tk)
        tk //= 2


def _mm(a, b, dims, out_dtype, name, res=None, alpha=1.0, after=None, b_rows=None):
    b_shape = b.shape if b_rows is None else (b_rows, b.shape[1])
    if dims == "nn":
        (m, k), (k2, n) = a.shape, b_shape
    elif dims == "nt":
        (m, k), (n, k2) = a.shape, b_shape
    else:
        (k, m), (k2, n) = a.shape, b_shape
    assert k == k2, (a.shape, b.shape, dims)
    has_res = res is not None
    tm, tn, tk = _mm_tiles(m, n, k, a.dtype.itemsize, b.dtype.itemsize, jnp.dtype(out_dtype).itemsize, res.dtype.itemsize if has_res else 0,
                           m_align=LANES if dims == "tn" else 8)
    nk = k // tk
    a_spec = pl.BlockSpec((tk, tm), lambda i, j, kk: (kk, i)) if dims == "tn" else pl.BlockSpec((tm, tk), lambda i, j, kk: (i, kk))
    b_spec = pl.BlockSpec((tn, tk), lambda i, j, kk: (j, kk)) if dims == "nt" else pl.BlockSpec((tk, tn), lambda i, j, kk: (kk, j))
    o_spec = pl.BlockSpec((tm, tn), lambda i, j, kk: (i, j))

    def finish(acc, r_ref, o_ref):
        out = acc * alpha if alpha != 1.0 else acc
        if has_res:
            out = r_ref[...].astype(F32) + out
        o_ref[...] = out.astype(o_ref.dtype)

    n_in = 2 + has_res + (after is not None)

    def body(*refs):
        a_ref, b_ref = refs[:2]
        r_ref = refs[2] if has_res else None
        o_ref = refs[n_in]
        p = _bdot_raw(a_ref[...], b_ref[...], dims)
        if nk == 1:
            finish(p, r_ref, o_ref)
            return
        acc_ref = refs[-1]
        kk = pl.program_id(2)

        @pl.when(kk == 0)
        def _():
            acc_ref[...] = p

        @pl.when(kk > 0)
        def _():
            acc_ref[...] += p

        @pl.when(kk == nk - 1)
        def _():
            finish(acc_ref[...], r_ref, o_ref)

    args = (a, b) + ((res,) if has_res else ()) + ((after,) if after is not None else ())
    in_specs = [a_spec, b_spec] + ([o_spec] if has_res else []) + ([pl.BlockSpec(TOKEN, lambda i, j, kk: (0, 0))] if after is not None else [])
    return pl.pallas_call(
        body, grid=(m // tm, n // tn, nk), in_specs=in_specs, out_specs=o_spec, out_shape=SDS((m, n), out_dtype),
        scratch_shapes=[pltpu.VMEM((tm, tn), F32)] if nk > 1 else [], name=name, compiler_params=_params(3),
    )(*args)


PIECE_TK = 1024


def _mm_pieces(pieces, b, name, res=None, after=None):
    m, n = pieces[0].shape[0], b.shape[1]
    blocks = [p.shape[1] // PIECE_TK for p in pieces]
    assert all(p.shape[1] % PIECE_TK == 0 and p.shape[0] == m for p in pieces)
    starts = [sum(blocks[:i]) for i in range(len(pieces))]
    nk = sum(blocks)
    tm, tn = _tile(m, (1024, 512, 256, 128)), _tile(n, (1024, 512, 256, 128))
    n_p = len(pieces)
    n_in = n_p + 1 + (res is not None) + (after is not None)

    def piece_spec(start, count):
        return pl.BlockSpec((tm, PIECE_TK), lambda i, j, kk: (i, jnp.clip(kk - start, 0, count - 1)))

    def body(*refs):
        b_ref, o_ref, acc_ref = refs[n_p], refs[n_in], refs[-1]
        kk = pl.program_id(2)

        @pl.when(kk == 0)
        def _():
            acc_ref[...] = jnp.zeros_like(acc_ref)

        for p_ref, start, count in zip(refs[:n_p], starts, blocks):
            @pl.when(jnp.logical_and(kk >= start, kk < start + count))
            def _(p_ref=p_ref):
                acc_ref[...] += _bdot_raw(p_ref[...], b_ref[...], "nn")

        @pl.when(kk == nk - 1)
        def _():
            out = acc_ref[...]
            if res is not None:
                out = refs[n_p + 1][...] + out
            o_ref[...] = out

    o_spec = pl.BlockSpec((tm, tn), lambda i, j, kk: (i, j))
    in_specs = [piece_spec(s, c) for s, c in zip(starts, blocks)] + [pl.BlockSpec((PIECE_TK, tn), lambda i, j, kk: (kk, j))]
    args = list(pieces) + [b]
    if res is not None:
        in_specs.append(o_spec)
        args.append(res)
    if after is not None:
        in_specs.append(pl.BlockSpec(TOKEN, lambda i, j, kk: (0, 0)))
        args.append(after)
    return pl.pallas_call(
        body, grid=(m // tm, n // tn, nk), in_specs=in_specs, out_specs=o_spec, out_shape=SDS((m, n), F32),
        scratch_shapes=[pltpu.VMEM((tm, tn), F32)], name=name, compiler_params=_params(3),
    )(*args)


def _tmap(fn, grid, ins, outs, name):
    n_in = len(ins)
    n_ax = len(grid)

    def body(*refs):
        vals = fn(*[r[...] for r in refs[:n_in]])
        if not isinstance(vals, (tuple, list)):
            vals = (vals,)
        first_inner = pl.program_id(n_ax - 1) == 0
        first_all = first_inner
        for ax in range(n_ax - 1):
            first_all = jnp.logical_and(first_all, pl.program_id(ax) == 0)

        def put(ref, val, acc):
            val = val.astype(ref.dtype)
            if acc is None:
                ref[...] = val
                return
            first = first_inner if acc == "inner" else first_all

            @pl.when(first)
            def _():
                ref[...] = val

            @pl.when(jnp.logical_not(first))
            def _():
                ref[...] += val

        for ref, val, o in zip(refs[n_in:], vals, outs):
            put(ref, val, o[4])

    return pl.pallas_call(
        body, grid=grid,
        in_specs=[pl.BlockSpec(bs, im) for _, bs, im in ins],
        out_specs=[pl.BlockSpec(o[2], o[3]) for o in outs],
        out_shape=[SDS(o[0], o[1]) for o in outs],
        name=name, compiler_params=_params(n_ax),
    )(*[a for a, _, _ in ins])


def _rows(width, tt, off=0):
    return (tt, width), (lambda j, i: (i, off + j))


def _rms(x, g):
    x = x.astype(F32)
    return x * lax.rsqrt(jnp.mean(x * x, axis=-1, keepdims=True) + EPS) * g


def _sigmoid(x):
    return jax.nn.sigmoid(x)


def _silu(x):
    return x * _sigmoid(x)


def _softplus(x):
    return jnp.maximum(x, 0.0) + jnp.log1p(jnp.exp(-jnp.abs(x)))


def _rms_fwd(x, g, name):
    t, d = x.shape
    tt = _tile(t, (256, 128))
    return _tmap(_rms, (1, t // tt), [(x, *_rows(d, tt)), (g, (1, d), lambda j, i: (0, 0))],
                 [((t, d), BF16, *_rows(d, tt), None)], name)[0]


def _rms_bwd(x, g, dn, dres, name):
    t, d = x.shape
    tt = _tile(t, (256, 128))

    def fn(x, g, dn, dres):
        x, dn = x.astype(F32), dn.astype(F32)
        r = lax.rsqrt(jnp.mean(x * x, axis=-1, keepdims=True) + EPS)
        xn = x * r
        t = dn * g
        dx = r * (t - xn * jnp.mean(t * xn, axis=-1, keepdims=True))
        return dres + dx, jnp.sum(dn * xn, axis=0, keepdims=True)

    return _tmap(fn, (1, t // tt),
                 [(x, *_rows(d, tt)), (g, (1, d), lambda j, i: (0, 0)), (dn, *_rows(d, tt)), (dres, *_rows(d, tt))],
                 [((t, d), F32, *_rows(d, tt), None), ((1, d), F32, (1, d), lambda j, i: (0, 0), "inner")], name)


def _swiglu(a, b):
    return _silu(a) * b


def _ffn_in_act(n, w_in_t, name, after):
    t, d = n.shape
    tm, tn = _tile(t, (512, 256, 128)), D_FF // 2
    half_blocks = D_FF // tn
    n_in = 3 + (after is not None)

    def body(*refs):
        n_ref, wa_ref, wb_ref = refs[:3]
        a_ref, b_ref, s_ref = refs[n_in:]
        x = n_ref[...]
        a = _bdot_raw(x, wa_ref[...], "nt").astype(BF16)
        b = _bdot_raw(x, wb_ref[...], "nt").astype(BF16)
        a_ref[...] = a
        b_ref[...] = b
        s_ref[...] = _swiglu(a.astype(F32), b.astype(F32)).astype(BF16)

    out = pl.BlockSpec((tm, tn), lambda i, j: (i, j))
    in_specs = [pl.BlockSpec((tm, d), lambda i, j: (i, 0)), pl.BlockSpec((tn, d), lambda i, j: (j, 0)),
                pl.BlockSpec((tn, d), lambda i, j: (j + half_blocks, 0))]
    args = [n, w_in_t, w_in_t]
    if after is not None:
        in_specs.append(pl.BlockSpec(TOKEN, lambda i, j: (0, 0)))
        args.append(after)
    return pl.pallas_call(
        body, grid=(t // tm, half_blocks), in_specs=in_specs, out_specs=[out, out, out], out_shape=[SDS((t, D_FF), BF16)] * 3,
        name=name, compiler_params=_params(2),
    )(*args)


def _swiglu_bwd(a, b, ds, name):
    t = a.shape[0]
    tt = _tile(t, (128,))

    def fn(a, b, ds):
        _, vjp = jax.vjp(_swiglu, a.astype(F32), b.astype(F32))
        da, db = vjp(ds.astype(F32))
        return jnp.concatenate([da, db], axis=1)

    r = _rows(D_FF, tt)
    return _tmap(fn, (1, t // tt), [(a, *r), (b, *r), (ds, *r)], [((t, 2 * D_FF), BF16, *_rows(2 * D_FF, tt), None)], name)[0]


def _ffn_fwd(h, g, weights, tag):
    n = _rms_fwd(h, g, tag + "_norm")
    w_in_t, w_out, after = weights(n)
    a, b, s = _ffn_in_act(n, w_in_t, tag + "_in", after)
    out = _mm(s, w_out, "nn", F32, tag + "_out", res=h, alpha=0.5)
    return out, (n, a, b, s, w_in_t, w_out)


def _ffn_bwd(h, g, saved, dout, tag, links):
    n, a, b, s, w_in_t, w_out = saved
    sent = links.send({tag + "_w_out": _mm(s, dout, "tn", BF16, tag + "_dw_out", alpha=0.5)})
    ds = _mm(dout, w_out, "nt", BF16, tag + "_ds", alpha=0.5, after=sent)
    dab = _swiglu_bwd(a, b, ds, tag + "_dact")
    sent = links.send({tag + "_w_in": _mm(dab, n, "tn", BF16, tag + "_dw_in")})
    dn = _mm(dab, w_in_t, "nn", BF16, tag + "_dn", after=sent)
    return _rms_bwd(h, g, dn, dout, tag + "_dnorm")


def _chunk_sum_matrix(n, chunk, transpose=False):
    row = lax.broadcasted_iota(jnp.int32, (n, n), 0)
    col = lax.broadcasted_iota(jnp.int32, (n, n), 1)
    if transpose:
        row, col = col, row
    return jnp.where(jnp.logical_and(col <= row, row // chunk == col // chunk), 1.0, 0.0).astype(F32)


def _hgrn_gates(hq, hf, lbl):
    lb = _sigmoid(lbl[0:1, :] - lbl[1:2, :])
    sg = _sigmoid(hf)
    f = lb + (1.0 - lb) * sg
    q = _silu(hq) * HEAD ** -0.5
    k = (1.0 - lb) * (1.0 - sg)
    return q, k, jnp.log(f)


def _hgrn_prep_fwd(proj, lbl):
    t = proj.shape[0]
    tt, ft = _tile(t, (256, 128)), 512

    def fn(hq, hf, lbl):
        q, k, log_f = _hgrn_gates(hq, hf, lbl)
        return q, k, _hdot_raw(_chunk_sum_matrix(tt, HG_CHUNK), log_f)

    o = ((t, D_MODEL), F32, *_rows(ft, tt), None)
    return _tmap(fn, (D_MODEL // ft, t // tt),
                 [(proj, *_rows(ft, tt, COL_HQ * HEAD // ft)), (proj, *_rows(ft, tt, COL_HF * HEAD // ft)), (lbl, (2, ft), lambda j, i: (0, j))],
                 [o, o, o], "hgrn_prep")


def _hgrn_prep_bwd(proj, lbl, dq, dk, db):
    t = proj.shape[0]
    tt, ft = _tile(t, (256, 128)), 512

    def fn(hq, hf, lbl, dq, dk, db):
        dlog_f = _hdot_raw(_chunk_sum_matrix(tt, HG_CHUNK, transpose=True), db)
        _, vjp = jax.vjp(_hgrn_gates, hq, hf, lbl)
        return vjp((dq.astype(F32), dk.astype(F32), dlog_f))

    o = ((t, D_MODEL), BF16, *_rows(ft, tt), None)
    r = _rows(ft, tt)
    return _tmap(fn, (D_MODEL // ft, t // tt),
                 [(proj, *_rows(ft, tt, COL_HQ * HEAD // ft)), (proj, *_rows(ft, tt, COL_HF * HEAD // ft)), (lbl, (2, ft), lambda j, i: (0, j)),
                  (dq, *r), (dk, *r), (db, *r)],
                 [o, o, ((2, D_MODEL), F32, (2, ft), lambda j, i: (0, j), "inner")], "hgrn_prep_bwd")


@functools.partial(jax.custom_vjp, nondiff_argnums=(1,))
def _roll_rows(x, d):
    return pltpu.roll(x, d, 0)


def _roll_rows_fwd(x, d):
    return pltpu.roll(x, d, 0), None


def _roll_rows_bwd(d, _, ct):
    return (pltpu.roll(ct, ct.shape[0] - d, 0),)


_roll_rows.defvjp(_roll_rows_fwd, _roll_rows_bwd)


def _hgrn_chunks(q, k, v, b, st):
    n = q[0].shape[0]
    half = n // 2
    srow = lax.broadcasted_iota(jnp.int32, (half, HEAD), 0)
    inter = _each(lambda q, b, st: _bdot(q * jnp.exp(b), st, "nt"), q, b, st)

    def below_scores(q, k, b):
        ref = b[half:half + 1, :]
        return _bdot(q[half:] * jnp.exp(jnp.minimum(b[half:] - ref, 0.0)), k[:half] * jnp.exp(jnp.minimum(ref - b[:half], 0.0)), "nt")

    below = _each(lambda a, v: _bdot(a, v[:half], "nn"), _each(below_scores, q, k, b), v)

    def diagonal(q, k, v, b):
        blocks = []
        for lo in (0, half):
            qb, kb, vb, bb = (a[lo:lo + half] for a in (q, k, v, b))
            o = jnp.sum(qb * kb, axis=1, keepdims=True) * vb
            for d in range(1, half):
                kr, vr, br = _roll_rows(kb, d), _roll_rows(vb, d), _roll_rows(bb, d)
                a = jnp.sum(qb * kr * jnp.exp(jnp.minimum(bb - br, 0.0)), axis=1, keepdims=True)
                o = o + jnp.where(srow[:, :1] >= d, a, 0.0) * vr
            blocks.append(o)
        return jnp.concatenate(blocks, axis=0)

    diag = _each(diagonal, q, k, v, b)
    o = _each(lambda inter, diag, below: inter + diag + jnp.concatenate([jnp.zeros_like(below), below], axis=0), inter, diag, below)

    def new_state(k, v, b, st):
        bend = b[n - 1:n, :]
        return st * jnp.exp(bend) + _bdot(v, k * jnp.exp(bend - b), "tn")

    return o, _each(new_state, k, v, b, st)


HG_GROUP = 8
HG_PER = GDN_CHUNK // HG_CHUNK


def _hgrn_rec_fwd(q, k, proj, b):
    t = q.shape[0]
    nc = t // GDN_CHUNK
    blk = (GDN_CHUNK, HG_GROUP * HEAD)
    im = lambda h, c: (c, h)

    def body(q_ref, k_ref, v_ref, b_ref, o_ref, hs_ref, st_ref):
        @pl.when(pl.program_id(1) == 0)
        def _():
            st_ref[...] = jnp.zeros_like(st_ref)

        heads = range(HG_GROUP)
        for j in range(HG_PER):
            sl = pl.ds(HG_CHUNK * j, HG_CHUNK)
            st = tuple(st_ref[g] for g in heads)
            o, st_new = _hgrn_chunks(*[tuple(r[sl, _head_lanes(g)] for g in heads) for r in (q_ref, k_ref, v_ref, b_ref)], st)
            for g in heads:
                hs_ref[g, j] = st[g]
                o_ref[sl, _head_lanes(g)] = o[g]
                st_ref[g] = st_new[g]

    return pl.pallas_call(
        body, grid=(HG_HEADS // HG_GROUP, nc),
        in_specs=[pl.BlockSpec(blk, im), pl.BlockSpec(blk, im), pl.BlockSpec(blk, lambda h, c: (c, COL_HI // HG_GROUP + h)), pl.BlockSpec(blk, im)],
        out_specs=[pl.BlockSpec(blk, im), pl.BlockSpec((HG_GROUP, HG_PER, HEAD, HEAD), lambda h, c: (h, c, 0, 0))],
        out_shape=[SDS((t, D_MODEL), F32), SDS((HG_HEADS, nc * HG_PER, HEAD, HEAD), F32)],
        scratch_shapes=[pltpu.VMEM((HG_GROUP, HEAD, HEAD), F32)], name="hgrn_rec", compiler_params=_params(2),
    )(q, k, proj, b)


def _hgrn_rec_bwd(q, k, proj, b, hs, do):
    t = q.shape[0]
    nc = t // GDN_CHUNK
    blk = (GDN_CHUNK, HG_GROUP * HEAD)
    im = lambda h, c: (nc - 1 - c, h)

    def body(q_ref, k_ref, v_ref, b_ref, hs_ref, do_ref, dq_ref, dk_ref, dv_ref, db_ref, dst_ref):
        @pl.when(pl.program_id(1) == 0)
        def _():
            dst_ref[...] = jnp.zeros_like(dst_ref)

        heads = range(HG_GROUP)
        for j in reversed(range(HG_PER)):
            sl = pl.ds(HG_CHUNK * j, HG_CHUNK)
            _, vjp = jax.vjp(_hgrn_chunks, *[tuple(r[sl, _head_lanes(g)] for g in heads) for r in (q_ref, k_ref, v_ref, b_ref)],
                             tuple(hs_ref[g, j] for g in heads))
            dq, dk, dv, db, dst = vjp((tuple(do_ref[sl, _head_lanes(g)].astype(F32) for g in heads), tuple(dst_ref[g] for g in heads)))
            for g in heads:
                ln = _head_lanes(g)
                dq_ref[sl, ln] = dq[g].astype(dq_ref.dtype)
                dk_ref[sl, ln] = dk[g].astype(dk_ref.dtype)
                dv_ref[sl, ln] = dv[g].astype(dv_ref.dtype)
                db_ref[sl, ln] = db[g]
                dst_ref[g] = dst[g]

    spec = pl.BlockSpec(blk, im)
    return pl.pallas_call(
        body, grid=(HG_HEADS // HG_GROUP, nc),
        in_specs=[spec, spec, pl.BlockSpec(blk, lambda h, c: (nc - 1 - c, COL_HI // HG_GROUP + h)), spec,
                  pl.BlockSpec((HG_GROUP, HG_PER, HEAD, HEAD), lambda h, c: (h, nc - 1 - c, 0, 0)), spec],
        out_specs=[spec, spec, spec, spec],
        out_shape=[SDS((t, D_MODEL), BF16), SDS((t, D_MODEL), BF16), SDS((t, D_MODEL), BF16), SDS((t, D_MODEL), F32)],
        scratch_shapes=[pltpu.VMEM((HG_GROUP, HEAD, HEAD), F32)], name="hgrn_rec_bwd", compiler_params=_params(2),
    )(q, k, proj, b, hs, do)


def _shift_down(x, d):
    if d == 0:
        return x
    row = lax.broadcasted_iota(jnp.int32, x.shape, 0)
    return jnp.where(row >= d, pltpu.roll(x, d, 0), 0.0)


def _shift_up(x, d):
    if d == 0:
        return x
    n = x.shape[0]
    row = lax.broadcasted_iota(jnp.int32, x.shape, 0)
    return jnp.where(row < n - d, pltpu.roll(x, n - d, 0), 0.0)


def _conv_fwd(proj, conv_w):
    t = proj.shape[0]
    width = 2 * D_MODEL + 2 * D_MODEL

    def body(x_ref, w_ref, c_ref, y_ref):
        x, w = x_ref[...], w_ref[...]
        y = w[CONV_K - 1:CONV_K, :] * x
        for j in range(CONV_K - 1):
            y = y + w[j:j + 1, :] * _shift_down(x, CONV_K - 1 - j)
        y_ref[...] = y
        c_ref[...] = _silu(y)

    out = pl.BlockSpec((t, HEAD), lambda j: (0, j))
    return pl.pallas_call(
        body, grid=(width // HEAD,),
        in_specs=[pl.BlockSpec((t, HEAD), lambda j: (0, COL_GQ + j)), pl.BlockSpec((CONV_K, HEAD), lambda j: (0, j))],
        out_specs=[out, out], out_shape=[SDS((t, width), F32), SDS((t, width), F32)],
        name="gdn_conv", compiler_params=_params(1),
    )(proj, conv_w)


def _conv_bwd(proj, conv_w, y, dc_qk, dc_v):
    t = proj.shape[0]
    n_qk = dc_qk.shape[1] // HEAD
    width = dc_qk.shape[1] + dc_v.shape[1]

    def body(x_ref, w_ref, y_ref, dqk_ref, dv_ref, dx_ref, dw_ref):
        x, w, y = x_ref[...], w_ref[...], y_ref[...]
        sg = _sigmoid(y)
        dc = jnp.where(pl.program_id(0) < n_qk, dqk_ref[...], dv_ref[...].astype(F32))
        dy = dc * (sg * (1.0 + y * (1.0 - sg)))
        ahead = [_shift_up(dy, CONV_K - 1 - j) for j in range(CONV_K)]
        dx = w[0:1, :] * ahead[0]
        for j in range(1, CONV_K):
            dx = dx + w[j:j + 1, :] * ahead[j]
        dx_ref[...] = dx.astype(dx_ref.dtype)
        dw_ref[...] = jnp.concatenate([jnp.sum(x * ahead[j], axis=0, keepdims=True) for j in range(CONV_K)], axis=0)

    blk = pl.BlockSpec((t, HEAD), lambda j: (0, j))
    return pl.pallas_call(
        body, grid=(width // HEAD,),
        in_specs=[pl.BlockSpec((t, HEAD), lambda j: (0, COL_GQ + j)), pl.BlockSpec((CONV_K, HEAD), lambda j: (0, j)), blk,
                  pl.BlockSpec((t, HEAD), lambda j: (0, jnp.minimum(j, n_qk - 1))), pl.BlockSpec((t, HEAD), lambda j: (0, jnp.maximum(j - n_qk, 0)))],
        out_specs=[blk, pl.BlockSpec((CONV_K, HEAD), lambda j: (0, j))],
        out_shape=[SDS((t, width), BF16), SDS((CONV_K, width), F32)],
        name="gdn_conv_bwd", compiler_params=_params(1),
    )(proj, conv_w, y, dc_qk, dc_v)


def _l2norm(x, scale):
    return x * lax.rsqrt(jnp.sum(x * x, axis=-1, keepdims=True) + EPS) * scale


def _head(a, h):
    return a[:, h * HEAD:(h + 1) * HEAD]


def _qk_scale(h):
    return HEAD ** -0.5 if h < GDN_QK_HEADS else 1.0


def _qk_norm_fwd(c):
    t = c.shape[0]
    tt = _tile(t, (256, 128))
    width = 2 * D_MODEL

    def fn(x):
        return jnp.concatenate([_l2norm(_head(x, h), _qk_scale(h)) for h in range(2 * GDN_QK_HEADS)], axis=1)

    return _tmap(fn, (1, t // tt), [(c, *_rows(width, tt))], [((t, width), F32, *_rows(width, tt), None)], "gdn_qk_norm")[0]


def _qk_norm_bwd(c, dq_rep, dk_rep):
    t = c.shape[0]
    tt = _tile(t, (256, 128))
    width = 2 * D_MODEL

    def fn(x, dq2, dk2):
        out = []
        for h in range(2 * GDN_QK_HEADS):
            d2, hh = (dq2, h) if h < GDN_QK_HEADS else (dk2, h - GDN_QK_HEADS)
            _, vjp = jax.vjp(lambda x: _l2norm(x, _qk_scale(h)), _head(x, h))
            out.append(vjp(_head(d2, 2 * hh).astype(F32) + _head(d2, 2 * hh + 1).astype(F32))[0])
        return jnp.concatenate(out, axis=1)

    r = _rows(width, tt)
    return _tmap(fn, (1, t // tt), [(c, *r), (dq_rep, *r), (dk_rep, *r)], [((t, width), F32, *r, None)], "gdn_qk_norm_bwd")[0]


def _gdn_gates(x, alog, dtb):
    return -jnp.exp(alog) * _softplus(x + dtb), _sigmoid(x)


def _gates_fwd(pab, alog, dtb):
    t = pab.shape[0]
    tt = _tile(t, (256, 128))

    def fn(x, alog, dtb):
        g, beta = _gdn_gates(x, alog, dtb)
        lane = lax.broadcasted_iota(jnp.int32, g.shape, 1)
        return jnp.where(lane < GDN_V_HEADS, _hdot_raw(_chunk_sum_matrix(tt, GDN_CHUNK), g), beta).T

    p = (alog, (1, HEAD), lambda j, i: (0, 0)), (dtb, (1, HEAD), lambda j, i: (0, 0))
    return _tmap(fn, (1, t // tt), [(pab, *_rows(HEAD, tt)), *p], [((HEAD, t), F32, (HEAD, tt), lambda j, i: (0, i), None)], "gdn_gates")[0]


def _gates_bwd(pab, alog, dtb, dout_t):
    t = pab.shape[0]
    tt = _tile(t, (256, 128))

    def fn(x, alog, dtb, dout_t):
        dout = dout_t.T
        lane = lax.broadcasted_iota(jnp.int32, dout.shape, 1)
        dgam = jnp.where(lane < GDN_V_HEADS, dout, 0.0)
        dbeta = jnp.where(jnp.logical_and(lane >= GDN_V_HEADS, lane < 2 * GDN_V_HEADS), dout, 0.0)
        dg = _hdot_raw(_chunk_sum_matrix(tt, GDN_CHUNK, transpose=True), dgam)
        _, vjp = jax.vjp(_gdn_gates, x, alog, dtb)
        return vjp((dg, dbeta))

    p = (alog, (1, HEAD), lambda j, i: (0, 0)), (dtb, (1, HEAD), lambda j, i: (0, 0))
    acc = ((1, HEAD), F32, (1, HEAD), lambda j, i: (0, 0), "inner")
    return _tmap(fn, (1, t // tt), [(pab, *_rows(HEAD, tt)), *p, (dout_t, (HEAD, tt), lambda j, i: (0, i))],
                 [((t, HEAD), BF16, *_rows(HEAD, tt), None), acc, acc], "gdn_gates_bwd")


def _split_bf16(x):
    hi = x.astype(BF16)
    return hi, (x - hi.astype(F32)).astype(BF16)


def _dot3(a, b):
    (ah, al), (bh, bl) = a, b
    return _bdot_raw(ah, bh, "nn") + (_bdot_raw(ah, bl, "nn") + _bdot_raw(al, bh, "nn"))


def _each(fn, *lists):
    return tuple(fn(*xs) for xs in zip(*lists))


def _unit_lower_inverses_raw(a):
    n = a[0].shape[0]
    row = lax.broadcasted_iota(jnp.int32, (n, n), 0)
    col = lax.broadcasted_iota(jnp.int32, (n, n), 1)
    eye = jnp.where(row == col, 1.0, 0.0).astype(F32)
    p = _each(lambda a: eye - a, a)
    x = _each(_split_bf16, a)
    m = 2
    while m < n:
        x = _each(_split_bf16, _each(_dot3, x, x))
        p = _each(lambda p, x: p + _bdot_raw(p, x[0], "nn"), p, x)
        m *= 2
    return p


@jax.custom_vjp
def _unit_lower_inverses(a, known):
    return _unit_lower_inverses_raw(a) if known is None else known


def _uli_fwd(a, known):
    inv = _unit_lower_inverses(a, known)
    return inv, (inv, known)


def _uli_bwd(res, ct):
    inv, known = res
    right = _each(lambda ct, inv: _bdot_raw(ct, inv, "nt"), ct, inv)
    da = _each(lambda inv, r: -_bdot_raw(inv, r, "tn"), inv, right)
    return da, (None if known is None else _each(jnp.zeros_like, known))


_unit_lower_inverses.defvjp(_uli_fwd, _uli_bwd)


def _gdn_chunks(q, k, v, beta_rows, gam_rows, s, inv_known=None):
    n = q[0].shape[0]
    heads = range(len(q))
    row = lax.broadcasted_iota(jnp.int32, (n, n), 0)
    col = lax.broadcasted_iota(jnp.int32, (n, n), 1)
    beta_cols, gam_cols = beta_rows.T, gam_rows.T
    beta = tuple(beta_cols[:, g:g + 1] for g in heads)
    gam = tuple(gam_cols[:, g:g + 1] for g in heads)
    gam_row = tuple(gam_rows[g:g + 1, :] for g in heads)
    decay = _each(lambda gam, gam_row: jnp.where(row >= col, jnp.exp(jnp.minimum(gam - gam_row, 0.0)), 0.0), gam, gam_row)
    kb = _each(lambda k, beta: k * beta, k, beta)
    a = _each(lambda kb, k, decay: jnp.where(row > col, _bdot(kb, k, "nt") * decay, 0.0), kb, k, decay)
    inv = _unit_lower_inverses(a, inv_known)
    eg = _each(jnp.exp, gam)
    u = _each(lambda inv, v, beta: _bdot(inv, v * beta, "nn"), inv, v, beta)
    w = _each(lambda inv, kb, eg: _bdot(inv, kb * eg, "nn"), inv, kb, eg)
    qk = _each(lambda q, k, decay: _bdot(q, k, "nt") * decay, q, k, decay)
    v_new = _each(lambda u, w, s: u - _bdot(w, s, "nn"), u, w, s)
    o_state = _each(lambda q, eg, s: _bdot(q * eg, s, "nn"), q, eg, s)
    o = _each(lambda o_state, qk, v_new: o_state + _bdot(qk, v_new, "nn"), o_state, qk, v_new)
    gend = _each(lambda gam: gam[n - 1:n, :], gam)
    s_new = _each(lambda s, k, gam, gend, v_new: s * jnp.exp(gend) + _bdot(k * jnp.exp(gend - gam), v_new, "tn"), s, k, gam, gend, v_new)
    return o, s_new, inv


GDN_GROUP = 16


def _gdn_specs(nc, rev):
    cc = (lambda c: nc - 1 - c) if rev else (lambda c: c)
    grp = GDN_GROUP
    q = pl.BlockSpec((GDN_CHUNK, grp // 2 * HEAD), lambda h, c: (cc(c), h))
    k = pl.BlockSpec((GDN_CHUNK, grp // 2 * HEAD), lambda h, c: (cc(c), 2 * GDN_QK_HEADS // grp + h))
    v = pl.BlockSpec((GDN_CHUNK, grp * HEAD), lambda h, c: (cc(c), 2 * GDN_QK_HEADS // grp + h))
    o = pl.BlockSpec((GDN_CHUNK, grp * HEAD), lambda h, c: (cc(c), h))
    rw = pl.BlockSpec((grp, None, 1, GDN_CHUNK), lambda h, c: (h, cc(c), 0, 0))
    st = pl.BlockSpec((grp, None, HEAD, HEAD), lambda h, c: (h, cc(c), 0, 0))
    inv = pl.BlockSpec((grp, None, GDN_CHUNK, GDN_CHUNK), lambda h, c: (h, cc(c), 0, 0))
    return q, k, v, o, rw, st, inv


def _head_lanes(g, per=1):
    return pl.ds((g // per) * HEAD, HEAD)


def _gdn_rec_fwd(qk, c, beta_row, gam_row):
    t = qk.shape[0]
    nc = t // GDN_CHUNK
    q, k, v, o, rw, st, inv = _gdn_specs(nc, False)

    def body(q_ref, k_ref, v_ref, be_ref, gr_ref, o_ref, ss_ref, inv_ref, s_ref):
        @pl.when(pl.program_id(1) == 0)
        def _():
            s_ref[...] = jnp.zeros_like(s_ref)

        heads = range(GDN_GROUP)
        s = tuple(s_ref[g] for g in heads)
        out, s_new, inv_c = _gdn_chunks(
            tuple(q_ref[:, _head_lanes(g, 2)] for g in heads), tuple(k_ref[:, _head_lanes(g, 2)] for g in heads),
            tuple(v_ref[:, _head_lanes(g)] for g in heads), be_ref[:, 0, :], gr_ref[:, 0, :], s)
        for g in heads:
            ss_ref[g] = s[g]
            o_ref[:, _head_lanes(g)] = out[g]
            inv_ref[g] = inv_c[g]
            s_ref[g] = s_new[g]

    return pl.pallas_call(
        body, grid=(GDN_V_HEADS // GDN_GROUP, nc), in_specs=[q, k, v, rw, rw], out_specs=[o, st, inv],
        out_shape=[SDS((t, 2 * D_MODEL), F32), SDS((GDN_V_HEADS, nc, HEAD, HEAD), F32), SDS((GDN_V_HEADS, nc, GDN_CHUNK, GDN_CHUNK), F32)],
        scratch_shapes=[pltpu.VMEM((GDN_GROUP, HEAD, HEAD), F32)], name="gdn_rec", compiler_params=_params(2),
    )(qk, qk, c, beta_row, gam_row)


def _gdn_rec_bwd(qk, c, beta_row, gam_row, ss, invs, do):
    t = qk.shape[0]
    nc = t // GDN_CHUNK
    q, k, v, o, rw, st, inv = _gdn_specs(nc, True)

    def body(q_ref, k_ref, v_ref, be_ref, gr_ref, ss_ref, inv_ref, do_ref,
             dq_ref, dk_ref, dv_ref, dbe_ref, dgr_ref, ds_ref):
        @pl.when(pl.program_id(1) == 0)
        def _():
            ds_ref[...] = jnp.zeros_like(ds_ref)

        heads = range(GDN_GROUP)
        _, vjp = jax.vjp(
            _gdn_chunks,
            tuple(q_ref[:, _head_lanes(g, 2)] for g in heads), tuple(k_ref[:, _head_lanes(g, 2)] for g in heads),
            tuple(v_ref[:, _head_lanes(g)] for g in heads), be_ref[:, 0, :], gr_ref[:, 0, :],
            tuple(ss_ref[g] for g in heads), tuple(inv_ref[g] for g in heads))
        no_inv_ct = tuple(jnp.zeros((GDN_CHUNK, GDN_CHUNK), F32) for g in heads)
        dq, dk, dv, dbe, dgr, ds, _ = vjp((tuple(do_ref[:, _head_lanes(g)].astype(F32) for g in heads), tuple(ds_ref[g] for g in heads), no_inv_ct))
        for g in heads:
            dq_ref[:, _head_lanes(g)] = dq[g].astype(dq_ref.dtype)
            dk_ref[:, _head_lanes(g)] = dk[g].astype(dk_ref.dtype)
            dv_ref[:, _head_lanes(g)] = dv[g].astype(dv_ref.dtype)
            ds_ref[g] = ds[g]
        dbe_ref[:, 0, :] = dbe
        dgr_ref[:, 0, :] = dgr

    wide = SDS((t, 2 * D_MODEL), BF16)
    rowshape = SDS((GDN_V_HEADS, nc, 1, GDN_CHUNK), F32)
    return pl.pallas_call(
        body, grid=(GDN_V_HEADS // GDN_GROUP, nc), in_specs=[q, k, v, rw, rw, st, inv, o], out_specs=[o, o, o, rw, rw],
        out_shape=[wide, wide, wide, rowshape, rowshape],
        scratch_shapes=[pltpu.VMEM((GDN_GROUP, HEAD, HEAD), F32)], name="gdn_rec_bwd", compiler_params=_params(2),
    )(qk, qk, c, beta_row, gam_row, ss, invs, do)


def _gated_norm(o, gate, w):
    return _rms(o, w) * _silu(gate)


def _post_fwd(o, proj, col_off, w, name):
    t, width = o.shape
    tt = _tile(t, (256, 128))

    def fn(o, gate, w):
        return jnp.concatenate([_gated_norm(_head(o, h), _head(gate, h), w) for h in range(width // HEAD)], axis=1)

    return _tmap(fn, (1, t // tt),
                 [(o, *_rows(width, tt)), (proj, *_rows(width, tt, col_off * HEAD // width)), (w, (1, HEAD), lambda j, i: (0, 0))],
                 [((t, width), BF16, *_rows(width, tt), None)], name)[0]


def _post_bwd(o, proj, col_off, w, dout, name):
    t, width = o.shape
    tt = _tile(t, (256, 128))

    def fn(o, gate, w, dout):
        do, dgate, dw = [], [], jnp.zeros((1, HEAD), F32)
        for h in range(width // HEAD):
            _, vjp = jax.vjp(_gated_norm, _head(o, h), _head(gate, h), w)
            a, b, c = vjp(_head(dout, h).astype(F32))
            do.append(a)
            dgate.append(b)
            dw = dw + c
        return jnp.concatenate(do, axis=1), jnp.concatenate(dgate, axis=1), dw

    r = _rows(width, tt)
    return _tmap(fn, (1, t // tt),
                 [(o, *r), (proj, *_rows(width, tt, col_off * HEAD // width)), (w, (1, HEAD), lambda j, i: (0, 0)), (dout, *r)],
                 [((t, width), BF16, *r, None), ((t, width), BF16, *r, None), ((1, HEAD), F32, (1, HEAD), lambda j, i: (0, 0), "inner")], name)


def _merge(gate_h, gate_g, yh, yg):
    return _sigmoid(gate_h) * yh + _sigmoid(gate_g) * yg


def _merge_fwd(proj, yh, yg):
    t = yh.shape[0]
    tt, ft = _tile(t, (256, 128)), 512
    r = _rows(ft, tt)
    return _tmap(_merge, (D_MODEL // ft, t // tt),
                 [(proj, *_rows(ft, tt, COL_GATE_H * HEAD // ft)), (proj, *_rows(ft, tt, COL_GATE_G * HEAD // ft)), (yh, *r), (yg, *r)],
                 [((t, D_MODEL), BF16, *r, None)], "merge")[0]


def _merge_bwd(proj, yh, yg, dy):
    t = yh.shape[0]
    tt, ft = _tile(t, (256, 128)), 512
    r = _rows(ft, tt)

    def fn(gate_h, gate_g, yh, yg, dy):
        _, vjp = jax.vjp(_merge, gate_h, gate_g, yh, yg)
        return vjp(dy.astype(F32))

    o = ((t, D_MODEL), BF16, *r, None)
    return _tmap(fn, (D_MODEL // ft, t // tt),
                 [(proj, *_rows(ft, tt, COL_GATE_H * HEAD // ft)), (proj, *_rows(ft, tt, COL_GATE_G * HEAD // ft)), (yh, *r), (yg, *r), (dy, *r)],
                 [o, o, o, o], "merge_bwd")


def _loss_head(h, target, g):
    t, d = h.shape
    tt = _tile(t, (256, 128))

    def fn(h, target, g):
        def f(h, g):
            err = _rms(h, g) - target
            return 0.5 * jnp.sum(jnp.mean(err * err, axis=-1))

        loss, (dh, dg) = jax.value_and_grad(f, (0, 1))(h, g)
        return dh, dg, jnp.full((1, HEAD), loss, F32)

    return _tmap(fn, (1, t // tt), [(h, *_rows(d, tt)), (target, *_rows(d, tt)), (g, (1, d), lambda j, i: (0, 0))],
                 [((t, d), F32, *_rows(d, tt), None), ((1, d), F32, (1, d), lambda j, i: (0, 0), "inner"),
                  ((1, HEAD), F32, (1, HEAD), lambda j, i: (0, 0), "inner")], "loss_head")


def _mixer_fwd(h, p, links):
    t = h.shape[0]
    nc = t // GDN_CHUNK
    u = _rms_fwd(h, p["mix_norm"], "mix_norm")
    w = {n: links.weight(n, h) for n in ("w_in_t", "w_in_b_t", "w_in_ab_t", "conv_w")}
    proj = _mm(u, w["w_in_t"], "nt", F32, "mix_in", after=links.started, b_rows=SCALAR_ROWS)
    proj_b = _mm(u, w["w_in_b_t"], "nt", F32, "mix_in_b")
    pab = _mm(u, w["w_in_ab_t"], "nt", F32, "mix_in_ab")
    qh, kh, bh = _hgrn_prep_fwd(proj, p["lbl"])
    oh, hs = _hgrn_rec_fwd(qh, kh, proj, bh)
    c, conv_y = _conv_fwd(proj, w["conv_w"])
    qk = _qk_norm_fwd(c)
    gates_t = _gates_fwd(pab, p["alog"], p["dtb"])
    gam_row = gates_t[:GDN_V_HEADS].reshape(GDN_V_HEADS, nc, 1, GDN_CHUNK)
    beta_row = gates_t[GDN_V_HEADS:2 * GDN_V_HEADS].reshape(GDN_V_HEADS, nc, 1, GDN_CHUNK)
    og, ss, invs = _gdn_rec_fwd(qk, c, beta_row, gam_row)
    ohn = _post_fwd(oh, proj, COL_HG, p["hgrn_out_norm"], "hgrn_out")
    ogn = _post_fwd(og, proj_b, COL_GZ, p["gdn_out_norm"], "gdn_out")
    w.update({n: links.weight(n, ogn) for n in ("w_branch_hgrn", "w_branch_gdn", "w_out")})
    yh = _mm(ohn, w["w_branch_hgrn"], "nn", BF16, "branch_hgrn")
    yg = _mm(ogn, w["w_branch_gdn"], "nn", BF16, "branch_gdn")
    y = _merge_fwd(proj_b, yh, yg)
    out = _mm(y, w["w_out"], "nn", F32, "mix_out", res=h)
    saved = (w, u, proj, proj_b, pab, qh, kh, bh, oh, hs, c, conv_y, qk, beta_row, gam_row, og, ss, invs, ohn, ogn, yh, yg, y)
    return out, saved


def _mixer_bwd(h, p, links, saved, dout):
    (w, u, proj, proj_b, pab, qh, kh, bh, oh, hs, c, conv_y, qk, beta_row, gam_row, og, ss, invs, ohn, ogn, yh, yg, y) = saved
    t = h.shape[0]
    grads = {}
    dw_out = _mm(y, dout, "tn", BF16, "mix_out_dw")
    dy = _mm(dout, w["w_out"], "nt", BF16, "mix_out_dx")
    dgate_h, dgate_g, dyh, dyg = _merge_bwd(proj_b, yh, yg, dy)
    dw_bh = _mm(ohn, dyh, "tn", BF16, "branch_hgrn_dw")
    dw_bg = _mm(ogn, dyg, "tn", BF16, "branch_gdn_dw")
    sent = links.send({"w_out": dw_out, "w_branch_hgrn": dw_bh, "w_branch_gdn": dw_bg})
    dohn = _mm(dyh, w["w_branch_hgrn"], "nt", BF16, "branch_hgrn_dx", after=sent)
    dogn = _mm(dyg, w["w_branch_gdn"], "nt", BF16, "branch_gdn_dx")
    doh, dhg, grads["hgrn_out_norm"] = _post_bwd(oh, proj, COL_HG, p["hgrn_out_norm"], dohn, "hgrn_out_bwd")
    dog, dgz, grads["gdn_out_norm"] = _post_bwd(og, proj_b, COL_GZ, p["gdn_out_norm"], dogn, "gdn_out_bwd")
    dqh, dkh, dhi, dbh = _hgrn_rec_bwd(qh, kh, proj, bh, hs, doh)
    dhq, dhf, grads["lbl"] = _hgrn_prep_bwd(proj, p["lbl"], dqh, dkh, dbh)
    dqv, dkv, dcv, dbeta_row, dgam_row = _gdn_rec_bwd(qk, c, beta_row, gam_row, ss, invs, dog)
    dcqk = _qk_norm_bwd(c, dqv, dkv)
    dxin, grads["conv_w"] = _conv_bwd(proj, w["conv_w"], conv_y, dcqk, dcv)
    dgates_t = jnp.concatenate([dgam_row.reshape(GDN_V_HEADS, t), dbeta_row.reshape(GDN_V_HEADS, t),
                                jnp.zeros((HEAD - 2 * GDN_V_HEADS, t), F32)], axis=0)
    dpab, grads["alog"], grads["dtb"] = _gates_bwd(pab, p["alog"], p["dtb"], dgates_t)
    front, back = [dhq, dhf, dhi, dhg, dxin], [dgz, dgate_h, dgate_g]
    dw_front = [_mm(d, u, "tn", BF16, "mix_in_dw_%d" % i) for i, d in enumerate(front)]
    dw_back = [_mm(d, u, "tn", BF16, "mix_in_b_dw_%d" % i) for i, d in enumerate(back)]
    dw_ab_t = _mm(dpab, u, "tn", BF16, "mix_in_ab_dw")
    sent = links.send({"w_in": jnp.concatenate(dw_front + [dw_ab_t[:N_SCALAR]] + dw_back, axis=0)})
    du = _mm_pieces(front, w["w_in_t"], "mix_in_dx", after=sent)
    du = _mm_pieces(back, w["w_in_b_t"], "mix_in_b_dx", res=du)
    du = _mm(dpab, w["w_in_ab_t"], "nn", F32, "mix_in_ab_dx", res=du)
    dh, grads["mix_norm"] = _rms_bwd(h, p["mix_norm"], du, dout, "mix_norm_bwd")
    return dh, grads


def _local_step(x, target, p, links):
    def ffn_weights(tag, behind):
        def get(n):
            w_in_t, w_out = links.weight(tag + "_w_in", n), links.weight(tag + "_w_out", n)
            return w_in_t, w_out, links.started if behind else None
        return get

    h1, s1 = _ffn_fwd(x, p["ffn1_norm"] + links.started[0, 0], ffn_weights("ffn1", True), "ffn1")
    h2, sm = _mixer_fwd(h1, p, links)
    h3, s2 = _ffn_fwd(h2, p["ffn2_norm"], ffn_weights("ffn2", False), "ffn2")
    dh3, dfinal, loss = _loss_head(h3, target, p["final_norm"])
    g = {"final_norm": dfinal}
    dh2, g["ffn2_norm"] = _ffn_bwd(h2, p["ffn2_norm"], s2, dh3, "ffn2", links)
    dh1, gm = _mixer_bwd(h1, p, links, sm, dh2)
    g.update(gm)
    dx, g["ffn1_norm"] = _ffn_bwd(x, p["ffn1_norm"], s1, dh1, "ffn1", links)
    return loss, dx, g


HBM_SPEC = pl.BlockSpec(memory_space=pltpu.HBM)
SEM_SPEC = pl.BlockSpec(memory_space=pltpu.SEMAPHORE)
DATAFLOW = pltpu.SideEffectType.DATAFLOW_SIDE_EFFECTING


def _position():
    x, y, c = lax.axis_index("x"), lax.axis_index("y"), lax.axis_index("c")
    return x, y, c, 4 * x + 2 * y + c


def _relations(x, y, c):
    for rel in range(1, N_DEV):
        px = 1 - x if rel & 4 else x
        py = 1 - y if rel & 2 else y
        pc = 1 - c if rel & 1 else c
        yield rel, (px, py, pc), 4 * px + 2 * py + pc


def _sem_index(item, rel):
    return item * (N_DEV - 1) + rel - 1


def _landing(a, mode):
    return lax.empty((N_DEV,) + a.shape if mode == "gather" else a.shape, a.dtype)


ALL_PEERS = tuple(range(1, N_DEV))
ONE_PER_CHIP = (1, 2, 4, 6)


def _copies_start(groups, name, rels=ALL_PEERS):
    flat = [item for grp in groups for item in grp]
    n, ng = len(flat), len(groups)
    lands = [_landing(a, mode) for a, mode in flat]

    def body(*refs):
        src_refs, land_refs, sems, token = refs[:n], refs[n:2 * n], refs[2 * n:2 * n + 2 * ng], refs[-1]
        x, y, c, me = _position()
        for rel, where, peer in _relations(x, y, c):
            if rel not in rels:
                continue
            k = 0
            for gi, grp in enumerate(groups):
                for li, (_, mode) in enumerate(grp):
                    src = src_refs[k] if mode == "gather" else src_refs[k].at[peer]
                    pltpu.make_async_remote_copy(src_ref=src, dst_ref=land_refs[k].at[me], send_sem=sems[2 * gi].at[_sem_index(li, rel)],
                                                 recv_sem=sems[2 * gi + 1].at[_sem_index(li, rel)], device_id=where, device_id_type=MESH_IDS).start()
                    k += 1
        token[...] = jnp.zeros_like(token)

    sem_shapes = [pltpu.SemaphoreType.DMA((len(grp) * (N_DEV - 1),)) for grp in groups for _ in range(2)]
    thru = [pltpu.HBM(a.shape, a.dtype) for a, _ in flat] + [pltpu.HBM(l.shape, l.dtype) for l in lands]
    outs = pl.pallas_call(
        body, name=name, out_shape=(*sem_shapes, *thru, SDS((8, HEAD), F32)),
        in_specs=[HBM_SPEC] * (2 * n), out_specs=(*[SEM_SPEC] * (2 * ng), *[HBM_SPEC] * (2 * n), pl.BlockSpec(memory_space=pltpu.VMEM)),
        input_output_aliases={i: 2 * ng + i for i in range(2 * n)}, compiler_params=pltpu.CompilerParams(has_side_effects=DATAFLOW),
    )(*[pltpu.with_memory_space_constraint(a, pltpu.HBM) for a, _ in flat], *[pltpu.with_memory_space_constraint(l, pltpu.HBM) for l in lands])
    sems, srcs, landed, token = outs[:2 * ng], outs[2 * ng:2 * ng + n], outs[2 * ng + n:2 * ng + 2 * n], outs[-1]
    result, k = [], 0
    for gi, grp in enumerate(groups):
        result.append((sems[2 * gi], sems[2 * gi + 1], srcs[k:k + len(grp)], landed[k:k + len(grp)]))
        k += len(grp)
    return result, token


def _copies_wait(started, modes, after, name, rels=ALL_PEERS):
    send_sems, recv_sems, srcs, lands = started
    n = len(srcs)

    def body(*refs):
        src_refs, land_refs, ssem, rsem, token = refs[:n], refs[n:2 * n], refs[2 * n], refs[2 * n + 1], refs[-1]
        x, y, c, _ = _position()
        for rel in rels:
            for i, mode in enumerate(modes):
                src = src_refs[i] if mode == "gather" else src_refs[i].at[0]
                cp = pltpu.make_async_remote_copy(src_ref=src, dst_ref=land_refs[i].at[0], send_sem=ssem.at[_sem_index(i, rel)],
                                                  recv_sem=rsem.at[_sem_index(i, rel)], device_id=(x, y, c), device_id_type=MESH_IDS)
                cp.wait_send()
                cp.wait_recv()
        token[...] = jnp.zeros_like(token)

    outs = pl.pallas_call(
        body, name=name, out_shape=[pltpu.HBM(a.shape, a.dtype) for a in (*srcs, *lands)] + [SDS((8, HEAD), F32)],
        in_specs=[HBM_SPEC] * (2 * n) + [SEM_SPEC, SEM_SPEC, pl.BlockSpec(memory_space=pl.ANY)],
        out_specs=[HBM_SPEC] * (2 * n) + [pl.BlockSpec(memory_space=pltpu.VMEM)],
        input_output_aliases={i: i for i in range(2 * n)}, compiler_params=pltpu.CompilerParams(has_side_effects=DATAFLOW),
    )(*srcs, *lands, send_sems, recv_sems, after)
    return outs[:n], outs[n:2 * n], outs[-1]


OTHER_CHIPS = ((1, 0), (0, 1), (1, 1))


def _pass_on_start(lands, name):
    n = len(lands)

    def body(*refs):
        land_refs, ssem, rsem, token = refs[:n], refs[n], refs[n + 1], refs[-1]
        x, y, c, _ = _position()
        for j, (fx, fy) in enumerate(OTHER_CHIPS):
            slot = 4 * (1 - x if fx else x) + 2 * (1 - y if fy else y) + c
            for i in range(n):
                pltpu.make_async_remote_copy(src_ref=land_refs[i].at[slot], dst_ref=land_refs[i].at[slot], send_sem=ssem.at[i * len(OTHER_CHIPS) + j],
                                             recv_sem=rsem.at[i * len(OTHER_CHIPS) + j], device_id=(x, y, 1 - c), device_id_type=MESH_IDS).start()
        token[...] = jnp.zeros_like(token)

    sems = pltpu.SemaphoreType.DMA((n * len(OTHER_CHIPS),))
    outs = pl.pallas_call(
        body, name=name, out_shape=(sems, sems, *[pltpu.HBM(l.shape, l.dtype) for l in lands], SDS(TOKEN, F32)),
        in_specs=[HBM_SPEC] * n, out_specs=(SEM_SPEC, SEM_SPEC, *[HBM_SPEC] * n, pl.BlockSpec(memory_space=pltpu.VMEM)),
        input_output_aliases={i: 2 + i for i in range(n)}, compiler_params=pltpu.CompilerParams(has_side_effects=DATAFLOW),
    )(*lands)
    return (outs[0], outs[1], outs[2:2 + n]), outs[-1]


def _pass_on_wait(started, after, name):
    send_sems, recv_sems, lands = started
    n = len(lands)

    def body(*refs):
        land_refs, ssem, rsem = refs[:n], refs[n], refs[n + 1]
        x, y, c, _ = _position()
        for j in range(len(OTHER_CHIPS)):
            for i in range(n):
                cp = pltpu.make_async_remote_copy(src_ref=land_refs[i].at[0], dst_ref=land_refs[i].at[0], send_sem=ssem.at[i * len(OTHER_CHIPS) + j],
                                                  recv_sem=rsem.at[i * len(OTHER_CHIPS) + j], device_id=(x, y, c), device_id_type=MESH_IDS)
                cp.wait_send()
                cp.wait_recv()

    return pl.pallas_call(
        body, name=name, out_shape=[pltpu.HBM(l.shape, l.dtype) for l in lands],
        in_specs=[HBM_SPEC] * n + [SEM_SPEC, SEM_SPEC, pl.BlockSpec(memory_space=pl.ANY)], out_specs=[HBM_SPEC] * n,
        input_output_aliases={i: i for i in range(n)}, compiler_params=pltpu.CompilerParams(has_side_effects=DATAFLOW),
    )(*lands, send_sems, recv_sems, after)


WEIGHT_GROUPS = (("ffn1_w_in", "ffn1_w_out", "gdn_conv_w"), ("w_in",), ("w_branch_hgrn", "w_branch_gdn", "w_out", "ffn2_w_in", "ffn2_w_out"))
GROUP_RELS = (ONE_PER_CHIP, ONE_PER_CHIP, ALL_PEERS)


class _Links:
    def __init__(self, shards, me):
        self.me = me
        self.shards = shards
        self.weights = {}
        self.sends = []
        self.gathers = {}
        self.started = None
        self._start_gather(0, None)

    def _start_gather(self, gi, zeros):
        if gi < len(WEIGHT_GROUPS):
            items = [(self.shards[n] if zeros is None else self.shards[n] + zeros[0, 0].astype(self.shards[n].dtype), "gather")
                     for n in WEIGHT_GROUPS[gi]]
            started, self.started = _copies_start([items], "gather_start_%d" % gi, GROUP_RELS[gi])
            self.gathers[gi] = started[0]

    def weight(self, name, after):
        if name not in self.weights:
            source = {"w_in_t": "w_in", "w_in_b_t": "w_in", "w_in_ab_t": "w_in", "conv_w": "gdn_conv_w"}.get(name, name)
            gi = [i for i, grp in enumerate(WEIGHT_GROUPS) if source in grp][0]
            assert gi in self.gathers, "weight groups are asked for in order"
            srcs, lands, zero = _copies_wait(self.gathers[gi], ["gather"] * len(WEIGHT_GROUPS[gi]), after, "gather_wait_%d" % gi, GROUP_RELS[gi])
            if GROUP_RELS[gi] == ONE_PER_CHIP:
                passing, zero = _pass_on_start(lands, "gather_pass_%d" % gi)
                self._start_gather(gi + 1, zero)
                lands = _pass_on_wait(passing, self.started, "gather_passed_%d" % gi)
            else:
                self._start_gather(gi + 1, zero)
            for n, src, land in zip(WEIGHT_GROUPS[gi], srcs, lands):
                full = lax.dynamic_update_index_in_dim(land, src, self.me, 0)
                if n == "gdn_conv_w":
                    self.weights["conv_w"] = full.reshape(N_DEV, CONV_K, 4 * D_MODEL // N_DEV).transpose(1, 0, 2).reshape(CONV_K, 4 * D_MODEL)
                elif n == "w_in":
                    self.weights.update(_w_in_pieces(full.reshape(-1, D_MODEL)))
                else:
                    self.weights[n] = full.reshape(-1, D_MODEL)
        return self.weights[name]

    def send(self, grads):
        names = list(grads)
        blocks = [grads[n].reshape(N_DEV, -1, D_MODEL) for n in names]
        started, token = _copies_start([[(b, "scatter") for b in blocks]], "send_" + names[0])
        self.sends.append((names, started[0]))
        return token

    def landed(self, after):
        out = {}
        for names, started in self.sends:
            srcs, lands, _ = _copies_wait(started, ["scatter"] * len(names), after, "landed_" + names[0])
            for n, src, land in zip(names, srcs, lands):
                out[n] = lax.dynamic_update_index_in_dim(land, lax.dynamic_index_in_dim(src, self.me, 0, keepdims=False), self.me, 0)
        return out


def _adam(parts, w, m, v, name):
    n_parts, r, c = parts.shape
    tc = c if c <= 512 else (256 if r > 1024 else 512)

    def body(p_ref, w_ref, m_ref, v_ref, g_ref, d_ref, mo_ref, vo_ref):
        g = p_ref[0].astype(F32)
        for i in range(1, n_parts):
            g = g + p_ref[i].astype(F32)
        m_new = ADAM_B1 * m_ref[...] + (1.0 - ADAM_B1) * g
        v_new = ADAM_B2 * v_ref[...] + (1.0 - ADAM_B2) * (g * g)
        m_hat = m_new / (1.0 - ADAM_B1 ** ADAM_STEP)
        v_hat = v_new / (1.0 - ADAM_B2 ** ADAM_STEP)
        g_ref[...] = g
        d_ref[...] = -ADAM_LR * (m_hat / (jnp.sqrt(v_hat) + ADAM_EPS) + ADAM_WD * w_ref[...])
        mo_ref[...] = m_new
        vo_ref[...] = v_new

    spec = pl.BlockSpec((r, tc), lambda j: (0, j))
    return pl.pallas_call(
        body, grid=(c // tc,), in_specs=[pl.BlockSpec((n_parts, r, tc), lambda j: (0, 0, j)), spec, spec, spec],
        out_specs=[spec] * 4, out_shape=[SDS((r, c), F32)] * 4, name=name, compiler_params=_params(1),
    )(parts, w, m, v)


BIG = ("ffn1_w_in", "ffn1_w_out", "w_in", "w_branch_hgrn", "w_branch_gdn", "w_out", "ffn2_w_in", "ffn2_w_out")


TRANSPOSED = ("ffn1_w_in", "w_in", "ffn2_w_in")


def _shard_rows(name, shard):
    return shard.T if name in TRANSPOSED else shard


SCALAR_ROWS = 8192
N_SCALAR = 2 * GDN_V_HEADS


def _w_in_pieces(w_in_t):
    return {"w_in_t": w_in_t, "w_in_b_t": w_in_t[SCALAR_ROWS + N_SCALAR:],
            "w_in_ab_t": jnp.pad(w_in_t[SCALAR_ROWS:SCALAR_ROWS + N_SCALAR], ((0, HEAD - N_SCALAR), (0, 0)))}


def _pad_lanes(a, width=HEAD):
    return jnp.pad(a, ((0, 0), (0, width - a.shape[1])))


SMALL_ROWS = 24


def _pack_small(g, loss):
    row6 = jnp.concatenate([g["hgrn_out_norm"], g["gdn_out_norm"], g["alog"], g["dtb"], loss,
                            jnp.zeros((1, D_MODEL - 5 * HEAD), F32)], axis=1)
    return jnp.concatenate([g["ffn1_norm"], g["mix_norm"], g["lbl"], g["ffn2_norm"], g["final_norm"], row6,
                            jnp.zeros((1, D_MODEL), F32), g["conv_w"].reshape(4 * CONV_K, D_MODEL)], axis=0)


def _pack_small_state(a):
    row6 = jnp.concatenate([a["hgrn_out_norm"], a["gdn_out_norm"], _pad_lanes(a["gdn_a_log"]), _pad_lanes(a["gdn_dt_bias"]),
                            jnp.zeros((1, D_MODEL - 4 * HEAD), F32)], axis=1)
    return jnp.concatenate([a["ffn1_norm"], a["mix_norm"], a["hgrn_lb_logits"], a["ffn2_norm"], a["final_norm"].reshape(1, D_MODEL),
                            row6, jnp.zeros((1, D_MODEL), F32)], axis=0)


def _unpack_small(a):
    return {"ffn1_norm": a[0:1], "mix_norm": a[1:2], "hgrn_lb_logits": a[2:4], "ffn2_norm": a[4:5], "final_norm": a[5],
            "hgrn_out_norm": a[6:7, :HEAD], "gdn_out_norm": a[6:7, HEAD:2 * HEAD],
            "gdn_a_log": a[6:7, 2 * HEAD:2 * HEAD + GDN_V_HEADS], "gdn_dt_bias": a[6:7, 3 * HEAD:3 * HEAD + GDN_V_HEADS]}


NAMES = ("ffn1_norm", "ffn1_w_in", "ffn1_w_out", "mix_norm", "w_in", "hgrn_lb_logits", "hgrn_out_norm", "gdn_conv_w", "gdn_a_log",
         "gdn_dt_bias", "gdn_out_norm", "w_branch_hgrn", "w_branch_gdn", "w_out", "ffn2_norm", "ffn2_w_in", "ffn2_w_out", "final_norm")


def kernel(x, ffn1_norm, ffn1_w_in, ffn1_w_out, mix_norm, w_in, hgrn_lb_logits, hgrn_out_norm, gdn_conv_w, gdn_a_log, gdn_dt_bias, gdn_out_norm, w_branch_hgrn, w_branch_gdn, w_out, ffn2_norm, ffn2_w_in, ffn2_w_out, final_norm, loss_target, m_ffn1_norm, m_ffn1_w_in, m_ffn1_w_out, m_mix_norm, m_w_in, m_hgrn_lb_logits, m_hgrn_out_norm, m_gdn_conv_w, m_gdn_a_log, m_gdn_dt_bias, m_gdn_out_norm, m_w_branch_hgrn, m_w_branch_gdn, m_w_out, m_ffn2_norm, m_ffn2_w_in, m_ffn2_w_out, m_final_norm, v_ffn1_norm, v_ffn1_w_in, v_ffn1_w_out, v_mix_norm, v_w_in, v_hgrn_lb_logits, v_hgrn_out_norm, v_gdn_conv_w, v_gdn_a_log, v_gdn_dt_bias, v_gdn_out_norm, v_w_branch_hgrn, v_w_branch_gdn, v_w_out, v_ffn2_norm, v_ffn2_w_in, v_ffn2_w_out, v_final_norm):
    wts = dict(zip(NAMES, (ffn1_norm, ffn1_w_in, ffn1_w_out, mix_norm, w_in, hgrn_lb_logits, hgrn_out_norm, gdn_conv_w, gdn_a_log,
                           gdn_dt_bias, gdn_out_norm, w_branch_hgrn, w_branch_gdn, w_out, ffn2_norm, ffn2_w_in, ffn2_w_out, final_norm)))
    mom = dict(zip(NAMES, (m_ffn1_norm, m_ffn1_w_in, m_ffn1_w_out, m_mix_norm, m_w_in, m_hgrn_lb_logits, m_hgrn_out_norm, m_gdn_conv_w,
                           m_gdn_a_log, m_gdn_dt_bias, m_gdn_out_norm, m_w_branch_hgrn, m_w_branch_gdn, m_w_out, m_ffn2_norm, m_ffn2_w_in,
                           m_ffn2_w_out, m_final_norm)))
    var = dict(zip(NAMES, (v_ffn1_norm, v_ffn1_w_in, v_ffn1_w_out, v_mix_norm, v_w_in, v_hgrn_lb_logits, v_hgrn_out_norm, v_gdn_conv_w,
                           v_gdn_a_log, v_gdn_dt_bias, v_gdn_out_norm, v_w_branch_hgrn, v_w_branch_gdn, v_w_out, v_ffn2_norm, v_ffn2_w_in,
                           v_ffn2_w_out, v_final_norm)))
    me = 4 * lax.axis_index("x") + 2 * lax.axis_index("y") + lax.axis_index("c")

    conv_shard = wts["gdn_conv_w"][0]
    shards = {n: _shard_rows(n, wts[n][0]).astype(BF16) for n in BIG}
    shards["gdn_conv_w"] = conv_shard.reshape(2, D_MODEL)
    links = _Links(shards, me)
    p = {"ffn1_norm": wts["ffn1_norm"], "mix_norm": wts["mix_norm"], "ffn2_norm": wts["ffn2_norm"], "final_norm": wts["final_norm"].reshape(1, D_MODEL),
         "lbl": wts["hgrn_lb_logits"], "hgrn_out_norm": wts["hgrn_out_norm"], "gdn_out_norm": wts["gdn_out_norm"],
         "alog": _pad_lanes(wts["gdn_a_log"]), "dtb": _pad_lanes(wts["gdn_dt_bias"])}

    loss, dx, g = _local_step(x[0], loss_target[0], p, links)

    small_started, small_token = _copies_start([[(_pack_small(g, loss), "gather")]], "small_start")
    landed = links.landed(small_token)

    big = [{} for _ in range(4)]
    for n in BIG:
        res = _adam(landed[n], _shard_rows(n, wts[n][0]), _shard_rows(n, mom[n][0]), _shard_rows(n, var[n][0]), "adam_" + n)
        for kind in range(4):
            big[kind][n] = _shard_rows(n, res[kind])
    small_srcs, small_lands, _ = _copies_wait(small_started[0], ["gather"], res[0], "small_wait")
    small_parts = lax.dynamic_update_index_in_dim(small_lands[0], small_srcs[0], me, 0)
    n_vec = SMALL_ROWS - 4 * CONV_K
    small_raw = _adam(small_parts[:, :n_vec], _pack_small_state(wts), _pack_small_state(mom), _pack_small_state(var), "adam_small")
    small = [_unpack_small(o) for o in small_raw]
    loss_total = small_raw[0][6, 4 * HEAD]
    conv_parts = small_parts[:, n_vec:].reshape(N_DEV, CONV_K, 4 * D_MODEL)
    width = 4 * D_MODEL // N_DEV
    conv_mine = lax.dynamic_slice_in_dim(conv_parts, me * width, width, axis=2)
    conv = _adam(conv_mine, conv_shard, mom["gdn_conv_w"][0], var["gdn_conv_w"][0], "adam_conv")

    outs = []
    for kind in range(4):
        for n in NAMES:
            if n in BIG:
                outs.append(big[kind][n][None])
            elif n == "gdn_conv_w":
                outs.append(conv[kind][None])
            else:
                outs.append(small[kind][n])
    return (loss_total, dx[None], *outs)
```

```python
import functools

import jax
import jax.numpy as jnp
from jax import lax
from jax.experimental import pallas as pl
from jax.experimental.pallas import tpu as pltpu

F32 = jnp.float32
BF16 = jnp.bfloat16
MESH_IDS = pl.DeviceIdType.MESH

D_MODEL = 1024
D_FF = 2816
N_DEV = 8
EPS = 1e-6
HEAD = 128
HG_HEADS = 8
GDN_QK_HEADS = 8
GDN_V_HEADS = 16
GDN_CHUNK = 64
HG_CHUNK = 16
CONV_K = 4
LANES = 128
COL_HQ, COL_HF, COL_HI, COL_HG, COL_GQ, COL_GK, COL_GV = 0, 8, 16, 24, 32, 40, 48
COL_GZ, COL_GATE_H, COL_GATE_G = 0, 16, 24
VMEM_LIMIT = 56 * 1024 * 1024

ADAM_LR, ADAM_B1, ADAM_B2, ADAM_EPS, ADAM_WD, ADAM_STEP = 0.001, 0.9, 0.999, 1e-08, 0.01, 10

SDS = jax.ShapeDtypeStruct


def _params(n_axes):
    return pltpu.CompilerParams(dimension_semantics=("arbitrary",) * n_axes, vmem_limit_bytes=VMEM_LIMIT)


def _tile(n, candidates=(512, 384, 256, 128, 64, 32, 16, 8)):
    for c in candidates:
        if n % c == 0:
            return c
    return n


_DIMS = {"nn": ((1,), (0,)), "nt": ((1,), (1,)), "tn": ((0,), (0,))}


def _bdot_raw(a, b, dims):
    return lax.dot_general(a.astype(BF16), b.astype(BF16), (_DIMS[dims], ((), ())), preferred_element_type=F32)


@functools.partial(jax.custom_vjp, nondiff_argnums=(2,))
def _bdot(a, b, dims):
    return _bdot_raw(a, b, dims)


def _bdot_fwd(a, b, dims):
    return _bdot_raw(a, b, dims), (a, b)


def _bdot_bwd(dims, res, ct):
    a, b = res
    if dims == "nn":
        return _bdot_raw(ct, b, "nt"), _bdot_raw(a, ct, "tn")
    if dims == "nt":
        return _bdot_raw(ct, b, "nn"), _bdot_raw(ct, a, "tn")
    return _bdot_raw(b, ct, "nt"), _bdot_raw(a, ct, "nn")


_bdot.defvjp(_bdot_fwd, _bdot_bwd)


def _hdot_raw(ones, x):
    hi = x.astype(BF16)
    rest = x - hi.astype(F32)
    mid = rest.astype(BF16)
    low = (rest - mid.astype(F32)).astype(BF16)
    return _bdot_raw(ones, hi, "nn") + (_bdot_raw(ones, mid, "nn") + _bdot_raw(ones, low, "nn"))


MM_VMEM_BUDGET = 38 * 1024 * 1024
TOKEN = (8, HEAD)


def _mm_tiles(m, n, k, a_bytes, b_bytes, o_bytes, r_bytes, m_align=8):
    def need(tm, tn, tk):
        return 2 * (tm * tk * a_bytes + tk * tn * b_bytes + tm * tn * (o_bytes + r_bytes)) + (tm * tn * 4 if tk < k else 0)

    def shrink(tm, tn, tk, floor_m, floor_n):
        while need(tm, tn, tk) > MM_VMEM_BUDGET:
            if tn > floor_n and tn % 256 == 0 and tn >= tm:
                tn //= 2
            elif tm > floor_m and tm % (2 * m_align) == 0:
                tm //= 2
            elif tn > floor_n and tn % 256 == 0:
                tn //= 2
            else:
                return None
        return tm, tn, tk

    tm = _tile(m, (1408, 1024, 704, 512, 256, 128, 64, 32, 16, 8))
    tn = _tile(n, (1408, 1024, 512, 256, 128))
    whole = shrink(tm, tn, k, min(tm, 1024), min(tn, 512))
    if whole is not None:
        return whole
    tk = _tile(k, (2048, 1408, 1024, 512, 256, 128, 64, 32, 16, 8))
    while True:
        fit = shrink(tm, tn, tk, min(tm, 256), min(tn, 512))
        if fit is not None or tk <= 512 or tk % 256:
            return fit if fit is not None else (tm, tn, tk)
        tk //= 2


def _mm(a, b, dims, out_dtype, name, res=None, alpha=1.0, after=None, b_rows=None):
    b_shape = b.shape if b_rows is None else (b_rows, b.shape[1])
    if dims == "nn":
        (m, k), (k2, n) = a.shape, b_shape
    elif dims == "nt":
        (m, k), (n, k2) = a.shape, b_shape
    else:
        (k, m), (k2, n) = a.shape, b_shape
    assert k == k2, (a.shape, b.shape, dims)
    has_res = res is not None
    tm, tn, tk = _mm_tiles(m, n, k, a.dtype.itemsize, b.dtype.itemsize, jnp.dtype(out_dtype).itemsize, res.dtype.itemsize if has_res else 0,
                           m_align=LANES if dims == "tn" else 8)
    nk = k // tk
    a_spec = pl.BlockSpec((tk, tm), lambda i, j, kk: (kk, i)) if dims == "tn" else pl.BlockSpec((tm, tk), lambda i, j, kk: (i, kk))
    b_spec = pl.BlockSpec((tn, tk), lambda i, j, kk: (j, kk)) if dims == "nt" else pl.BlockSpec((tk, tn), lambda i, j, kk: (kk, j))
    o_spec = pl.BlockSpec((tm, tn), lambda i, j, kk: (i, j))

    def finish(acc, r_ref, o_ref):
        out = acc * alpha if alpha != 1.0 else acc
        if has_res:
            out = r_ref[...].astype(F32) + out
        o_ref[...] = out.astype(o_ref.dtype)

    n_in = 2 + has_res + (after is not None)

    def body(*refs):
        a_ref, b_ref = refs[:2]
        r_ref = refs[2] if has_res else None
        o_ref = refs[n_in]
        p = _bdot_raw(a_ref[...], b_ref[...], dims)
        if nk == 1:
            finish(p, r_ref, o_ref)
            return
        acc_ref = refs[-1]
        kk = pl.program_id(2)

        @pl.when(kk == 0)
        def _():
            acc_ref[...] = p

        @pl.when(kk > 0)
        def _():
            acc_ref[...] += p

        @pl.when(kk == nk - 1)
        def _():
            finish(acc_ref[...], r_ref, o_ref)

    args = (a, b) + ((res,) if has_res else ()) + ((after,) if after is not None else ())
    in_specs = [a_spec, b_spec] + ([o_spec] if has_res else []) + ([pl.BlockSpec(TOKEN, lambda i, j, kk: (0, 0))] if after is not None else [])
    return pl.pallas_call(
        body, grid=(m // tm, n // tn, nk), in_specs=in_specs, out_specs=o_spec, out_shape=SDS((m, n), out_dtype),
        scratch_shapes=[pltpu.VMEM((tm, tn), F32)] if nk > 1 else [], name=name, compiler_params=_params(3),
    )(*args)


PIECE_TK = 1024


def _mm_pieces(pieces, b, name, res=None, after=None):
    m, n = pieces[0].shape[0], b.shape[1]
    blocks = [p.shape[1] // PIECE_TK for p in pieces]
    assert all(p.shape[1] % PIECE_TK == 0 and p.shape[0] == m for p in pieces)
    starts = [sum(blocks[:i]) for i in range(len(pieces))]
    nk = sum(blocks)
    tm, tn = _tile(m, (1024, 512, 256, 128)), _tile(n, (1024, 512, 256, 128))
    n_p = len(pieces)
    n_in = n_p + 1 + (res is not None) + (after is not None)

    def piece_spec(start, count):
        return pl.BlockSpec((tm, PIECE_TK), lambda i, j, kk: (i, jnp.clip(kk - start, 0, count - 1)))

    def body(*refs):
        b_ref, o_ref, acc_ref = refs[n_p], refs[n_in], refs[-1]
        kk = pl.program_id(2)

        @pl.when(kk == 0)
        def _():
            acc_ref[...] = jnp.zeros_like(acc_ref)

        for p_ref, start, count in zip(refs[:n_p], starts, blocks):
            @pl.when(jnp.logical_and(kk >= start, kk < start + count))
            def _(p_ref=p_ref):
                acc_ref[...] += _bdot_raw(p_ref[...], b_ref[...], "nn")

        @pl.when(kk == nk - 1)
        def _():
            out = acc_ref[...]
            if res is not None:
                out = refs[n_p + 1][...] + out
            o_ref[...] = out

    o_spec = pl.BlockSpec((tm, tn), lambda i, j, kk: (i, j))
    in_specs = [piece_spec(s, c) for s, c in zip(starts, blocks)] + [pl.BlockSpec((PIECE_TK, tn), lambda i, j, kk: (kk, j))]
    args = list(pieces) + [b]
    if res is not None:
        in_specs.append(o_spec)
        args.append(res)
    if after is not None:
        in_specs.append(pl.BlockSpec(TOKEN, lambda i, j, kk: (0, 0)))
        args.append(after)
    return pl.pallas_call(
        body, grid=(m // tm, n // tn, nk), in_specs=in_specs, out_specs=o_spec, out_shape=SDS((m, n), F32),
        scratch_shapes=[pltpu.VMEM((tm, tn), F32)], name=name, compiler_params=_params(3),
    )(*args)


def _tmap(fn, grid, ins, outs, name):
    n_in = len(ins)
    n_ax = len(grid)

    def body(*refs):
        vals = fn(*[r[...] for r in refs[:n_in]])
        if not isinstance(vals, (tuple, list)):
            vals = (vals,)
        first_inner = pl.program_id(n_ax - 1) == 0
        first_all = first_inner
        for ax in range(n_ax - 1):
            first_all = jnp.logical_and(first_all, pl.program_id(ax) == 0)

        def put(ref, val, acc):
            val = val.astype(ref.dtype)
            if acc is None:
                ref[...] = val
                return
            first = first_inner if acc == "inner" else first_all

            @pl.when(first)
            def _():
                ref[...] = val

            @pl.when(jnp.logical_not(first))
            def _():
                ref[...] += val

        for ref, val, o in zip(refs[n_in:], vals, outs):
            put(ref, val, o[4])

    return pl.pallas_call(
        body, grid=grid,
        in_specs=[pl.BlockSpec(bs, im) for _, bs, im in ins],
        out_specs=[pl.BlockSpec(o[2], o[3]) for o in outs],
        out_shape=[SDS(o[0], o[1]) for o in outs],
        name=name, compiler_params=_params(n_ax),
    )(*[a for a, _, _ in ins])


def _rows(width, tt, off=0):
    return (tt, width), (lambda j, i: (i, off + j))


def _rms(x, g):
    x = x.astype(F32)
    return x * lax.rsqrt(jnp.mean(x * x, axis=-1, keepdims=True) + EPS) * g


def _sigmoid(x):
    return jax.nn.sigmoid(x)


def _silu(x):
    return x * _sigmoid(x)


def _softplus(x):
    return jnp.maximum(x, 0.0) + jnp.log1p(jnp.exp(-jnp.abs(x)))


def _rms_fwd(x, g, name):
    t, d = x.shape
    tt = _tile(t, (256, 128))
    return _tmap(_rms, (1, t // tt), [(x, *_rows(d, tt)), (g, (1, d), lambda j, i: (0, 0))],
                 [((t, d), BF16, *_rows(d, tt), None)], name)[0]


def _rms_bwd(x, g, dn, dres, name):
    t, d = x.shape
    tt = _tile(t, (256, 128))

    def fn(x, g, dn, dres):
        x, dn = x.astype(F32), dn.astype(F32)
        r = lax.rsqrt(jnp.mean(x * x, axis=-1, keepdims=True) + EPS)
        xn = x * r
        t = dn * g
        dx = r * (t - xn * jnp.mean(t * xn, axis=-1, keepdims=True))
        return dres + dx, jnp.sum(dn * xn, axis=0, keepdims=True)

    return _tmap(fn, (1, t // tt),
                 [(x, *_rows(d, tt)), (g, (1, d), lambda j, i: (0, 0)), (dn, *_rows(d, tt)), (dres, *_rows(d, tt))],
                 [((t, d), F32, *_rows(d, tt), None), ((1, d), F32, (1, d), lambda j, i: (0, 0), "inner")], name)


def _swiglu(a, b):
    return _silu(a) * b


def _ffn_in_act(n, w_in_t, name, after):
    t, d = n.shape
    tm, tn = _tile(t, (512, 256, 128)), D_FF // 2
    half_blocks = D_FF // tn
    n_in = 3 + (after is not None)

    def body(*refs):
        n_ref, wa_ref, wb_ref = refs[:3]
        a_ref, b_ref, s_ref = refs[n_in:]
        x = n_ref[...]
        a = _bdot_raw(x, wa_ref[...], "nt").astype(BF16)
        b = _bdot_raw(x, wb_ref[...], "nt").astype(BF16)
        a_ref[...] = a
        b_ref[...] = b
        s_ref[...] = _swiglu(a.astype(F32), b.astype(F32)).astype(BF16)

    out = pl.BlockSpec((tm, tn), lambda i, j: (i, j))
    in_specs = [pl.BlockSpec((tm, d), lambda i, j: (i, 0)), pl.BlockSpec((tn, d), lambda i, j: (j, 0)),
                pl.BlockSpec((tn, d), lambda i, j: (j + half_blocks, 0))]
    args = [n, w_in_t, w_in_t]
    if after is not None:
        in_specs.append(pl.BlockSpec(TOKEN, lambda i, j: (0, 0)))
        args.append(after)
    return pl.pallas_call(
        body, grid=(t // tm, half_blocks), in_specs=in_specs, out_specs=[out, out, out], out_shape=[SDS((t, D_FF), BF16)] * 3,
        name=name, compiler_params=_params(2),
    )(*args)


def _swiglu_bwd(a, b, ds, name):
    t = a.shape[0]
    tt = _tile(t, (128,))

    def fn(a, b, ds):
        _, vjp = jax.vjp(_swiglu, a.astype(F32), b.astype(F32))
        da, db = vjp(ds.astype(F32))
        return jnp.concatenate([da, db], axis=1)

    r = _rows(D_FF, tt)
    return _tmap(fn, (1, t // tt), [(a, *r), (b, *r), (ds, *r)], [((t, 2 * D_FF), BF16, *_rows(2 * D_FF, tt), None)], name)[0]


def _ffn_fwd(h, g, weights, tag):
    n = _rms_fwd(h, g, tag + "_norm")
    w_in_t, w_out, after = weights(n)
    a, b, s = _ffn_in_act(n, w_in_t, tag + "_in", after)
    out = _mm(s, w_out, "nn", F32, tag + "_out", res=h, alpha=0.5)
    return out, (n, a, b, s, w_in_t, w_out)


def _ffn_bwd(h, g, saved, dout, tag, links):
    n, a, b, s, w_in_t, w_out = saved
    sent = links.send({tag + "_w_out": _mm(s, dout, "tn", BF16, tag + "_dw_out", alpha=0.5)})
    ds = _mm(dout, w_out, "nt", BF16, tag + "_ds", alpha=0.5, after=sent)
    dab = _swiglu_bwd(a, b, ds, tag + "_dact")
    sent = links.send({tag + "_w_in": _mm(dab, n, "tn", BF16, tag + "_dw_in")})
    dn = _mm(dab, w_in_t, "nn", BF16, tag + "_dn", after=sent)
    return _rms_bwd(h, g, dn, dout, tag + "_dnorm")


def _chunk_sum_matrix(n, chunk, transpose=False):
    row = lax.broadcasted_iota(jnp.int32, (n, n), 0)
    col = lax.broadcasted_iota(jnp.int32, (n, n), 1)
    if transpose:
        row, col = col, row
    return jnp.where(jnp.logical_and(col <= row, row // chunk == col // chunk), 1.0, 0.0).astype(F32)


def _hgrn_gates(hq, hf, lbl):
    lb = _sigmoid(lbl[0:1, :] - lbl[1:2, :])
    sg = _sigmoid(hf)
    f = lb + (1.0 - lb) * sg
    q = _silu(hq) * HEAD ** -0.5
    k = (1.0 - lb) * (1.0 - sg)
    return q, k, jnp.log(f)


def _hgrn_prep_fwd(proj, lbl):
    t = proj.shape[0]
    tt, ft = _tile(t, (256, 128)), 512

    def fn(hq, hf, lbl):
        q, k, log_f = _hgrn_gates(hq, hf, lbl)
        return q, k, _hdot_raw(_chunk_sum_matrix(tt, HG_CHUNK), log_f)

    o = ((t, D_MODEL), F32, *_rows(ft, tt), None)
    return _tmap(fn, (D_MODEL // ft, t // tt),
                 [(proj, *_rows(ft, tt, COL_HQ * HEAD // ft)), (proj, *_rows(ft, tt, COL_HF * HEAD // ft)), (lbl, (2, ft), lambda j, i: (0, j))],
                 [o, o, o], "hgrn_prep")


def _hgrn_prep_bwd(proj, lbl, dq, dk, db):
    t = proj.shape[0]
    tt, ft = _tile(t, (256, 128)), 512

    def fn(hq, hf, lbl, dq, dk, db):
        dlog_f = _hdot_raw(_chunk_sum_matrix(tt, HG_CHUNK, transpose=True), db)
        _, vjp = jax.vjp(_hgrn_gates, hq, hf, lbl)
        return vjp((dq, dk, dlog_f))

    o = ((t, D_MODEL), BF16, *_rows(ft, tt), None)
    r = _rows(ft, tt)
    return _tmap(fn, (D_MODEL // ft, t // tt),
                 [(proj, *_rows(ft, tt, COL_HQ * HEAD // ft)), (proj, *_rows(ft, tt, COL_HF * HEAD // ft)), (lbl, (2, ft), lambda j, i: (0, j)),
                  (dq, *r), (dk, *r), (db, *r)],
                 [o, o, ((2, D_MODEL), F32, (2, ft), lambda j, i: (0, j), "inner")], "hgrn_prep_bwd")


@functools.partial(jax.custom_vjp, nondiff_argnums=(1,))
def _roll_rows(x, d):
    return pltpu.roll(x, d, 0)


def _roll_rows_fwd(x, d):
    return pltpu.roll(x, d, 0), None


def _roll_rows_bwd(d, _, ct):
    return (pltpu.roll(ct, ct.shape[0] - d, 0),)


_roll_rows.defvjp(_roll_rows_fwd, _roll_rows_bwd)


def _hgrn_chunks(q, k, v, b, st):
    n = q[0].shape[0]
    half = n // 2
    srow = lax.broadcasted_iota(jnp.int32, (half, HEAD), 0)
    inter = _each(lambda q, b, st: _bdot(q * jnp.exp(b), st, "nt"), q, b, st)

    def below_scores(q, k, b):
        ref = b[half:half + 1, :]
        return _bdot(q[half:] * jnp.exp(jnp.minimum(b[half:] - ref, 0.0)), k[:half] * jnp.exp(jnp.minimum(ref - b[:half], 0.0)), "nt")

    below = _each(lambda a, v: _bdot(a, v[:half], "nn"), _each(below_scores, q, k, b), v)

    def diagonal(q, k, v, b):
        blocks = []
        for lo in (0, half):
            qb, kb, vb, bb = (a[lo:lo + half] for a in (q, k, v, b))
            o = jnp.sum(qb * kb, axis=1, keepdims=True) * vb
            for d in range(1, half):
                kr, vr, br = _roll_rows(kb, d), _roll_rows(vb, d), _roll_rows(bb, d)
                a = jnp.sum(qb * kr * jnp.exp(jnp.minimum(bb - br, 0.0)), axis=1, keepdims=True)
                o = o + jnp.where(srow[:, :1] >= d, a, 0.0) * vr
            blocks.append(o)
        return jnp.concatenate(blocks, axis=0)

    diag = _each(diagonal, q, k, v, b)
    o = _each(lambda inter, diag, below: inter + diag + jnp.concatenate([jnp.zeros_like(below), below], axis=0), inter, diag, below)

    def new_state(k, v, b, st):
        bend = b[n - 1:n, :]
        return st * jnp.exp(bend) + _bdot(v, k * jnp.exp(bend - b), "tn")

    return o, _each(new_state, k, v, b, st)


HG_GROUP = 8
HG_PER = GDN_CHUNK // HG_CHUNK


def _hgrn_rec_fwd(q, k, proj, b):
    t = q.shape[0]
    nc = t // GDN_CHUNK
    blk = (GDN_CHUNK, HG_GROUP * HEAD)
    im = lambda h, c: (c, h)

    def body(q_ref, k_ref, v_ref, b_ref, o_ref, hs_ref, st_ref):
        @pl.when(pl.program_id(1) == 0)
        def _():
            st_ref[...] = jnp.zeros_like(st_ref)

        heads = range(HG_GROUP)
        for j in range(HG_PER):
            sl = pl.ds(HG_CHUNK * j, HG_CHUNK)
            st = tuple(st_ref[g] for g in heads)
            o, st_new = _hgrn_chunks(*[tuple(r[sl, _head_lanes(g)] for g in heads) for r in (q_ref, k_ref, v_ref, b_ref)], st)
            for g in heads:
                hs_ref[g, j] = st[g]
                o_ref[sl, _head_lanes(g)] = o[g].astype(o_ref.dtype)
                st_ref[g] = st_new[g]

    return pl.pallas_call(
        body, grid=(HG_HEADS // HG_GROUP, nc),
        in_specs=[pl.BlockSpec(blk, im), pl.BlockSpec(blk, im), pl.BlockSpec(blk, lambda h, c: (c, COL_HI // HG_GROUP + h)), pl.BlockSpec(blk, im)],
        out_specs=[pl.BlockSpec(blk, im), pl.BlockSpec((HG_GROUP, HG_PER, HEAD, HEAD), lambda h, c: (h, c, 0, 0))],
        out_shape=[SDS((t, D_MODEL), BF16), SDS((HG_HEADS, nc * HG_PER, HEAD, HEAD), F32)],
        scratch_shapes=[pltpu.VMEM((HG_GROUP, HEAD, HEAD), F32)], name="hgrn_rec", compiler_params=_params(2),
    )(q, k, proj, b)


def _hgrn_rec_bwd(q, k, proj, b, hs, do):
    t = q.shape[0]
    nc = t // GDN_CHUNK
    blk = (GDN_CHUNK, HG_GROUP * HEAD)
    im = lambda h, c: (nc - 1 - c, h)

    def body(q_ref, k_ref, v_ref, b_ref, hs_ref, do_ref, dq_ref, dk_ref, dv_ref, db_ref, dst_ref):
        @pl.when(pl.program_id(1) == 0)
        def _():
            dst_ref[...] = jnp.zeros_like(dst_ref)

        heads = range(HG_GROUP)
        for j in reversed(range(HG_PER)):
            sl = pl.ds(HG_CHUNK * j, HG_CHUNK)
            _, vjp = jax.vjp(_hgrn_chunks, *[tuple(r[sl, _head_lanes(g)] for g in heads) for r in (q_ref, k_ref, v_ref, b_ref)],
                             tuple(hs_ref[g, j] for g in heads))
            dq, dk, dv, db, dst = vjp((tuple(do_ref[sl, _head_lanes(g)] for g in heads), tuple(dst_ref[g] for g in heads)))
            for g in heads:
                ln = _head_lanes(g)
                dq_ref[sl, ln] = dq[g]
                dk_ref[sl, ln] = dk[g]
                dv_ref[sl, ln] = dv[g].astype(dv_ref.dtype)
                db_ref[sl, ln] = db[g]
                dst_ref[g] = dst[g]

    spec = pl.BlockSpec(blk, im)
    return pl.pallas_call(
        body, grid=(HG_HEADS // HG_GROUP, nc),
        in_specs=[spec, spec, pl.BlockSpec(blk, lambda h, c: (nc - 1 - c, COL_HI // HG_GROUP + h)), spec,
                  pl.BlockSpec((HG_GROUP, HG_PER, HEAD, HEAD), lambda h, c: (h, nc - 1 - c, 0, 0)), spec],
        out_specs=[spec, spec, spec, spec],
        out_shape=[SDS((t, D_MODEL), F32), SDS((t, D_MODEL), F32), SDS((t, D_MODEL), BF16), SDS((t, D_MODEL), F32)],
        scratch_shapes=[pltpu.VMEM((HG_GROUP, HEAD, HEAD), F32)], name="hgrn_rec_bwd", compiler_params=_params(2),
    )(q, k, proj, b, hs, do)


def _shift_down(x, d):
    if d == 0:
        return x
    row = lax.broadcasted_iota(jnp.int32, x.shape, 0)
    return jnp.where(row >= d, pltpu.roll(x, d, 0), 0.0)


def _shift_up(x, d):
    if d == 0:
        return x
    n = x.shape[0]
    row = lax.broadcasted_iota(jnp.int32, x.shape, 0)
    return jnp.where(row < n - d, pltpu.roll(x, n - d, 0), 0.0)


def _conv_fwd(proj, conv_w):
    t = proj.shape[0]
    width = 2 * D_MODEL + 2 * D_MODEL

    def body(x_ref, w_ref, c_ref, y_ref):
        x, w = x_ref[...], w_ref[...]
        y = w[CONV_K - 1:CONV_K, :] * x
        for j in range(CONV_K - 1):
            y = y + w[j:j + 1, :] * _shift_down(x, CONV_K - 1 - j)
        y_ref[...] = y
        c_ref[...] = _silu(y)

    out = pl.BlockSpec((t, HEAD), lambda j: (0, j))
    return pl.pallas_call(
        body, grid=(width // HEAD,),
        in_specs=[pl.BlockSpec((t, HEAD), lambda j: (0, COL_GQ + j)), pl.BlockSpec((CONV_K, HEAD), lambda j: (0, j))],
        out_specs=[out, out], out_shape=[SDS((t, width), F32), SDS((t, width), F32)],
        name="gdn_conv", compiler_params=_params(1),
    )(proj, conv_w)


def _conv_bwd(proj, conv_w, y, dc_qk, dc_v):
    t = proj.shape[0]
    n_qk = dc_qk.shape[1] // HEAD
    width = dc_qk.shape[1] + dc_v.shape[1]

    def body(x_ref, w_ref, y_ref, dqk_ref, dv_ref, dx_ref, dw_ref):
        x, w, y = x_ref[...], w_ref[...], y_ref[...]
        sg = _sigmoid(y)
        dc = jnp.where(pl.program_id(0) < n_qk, dqk_ref[...], dv_ref[...].astype(F32))
        dy = dc * (sg * (1.0 + y * (1.0 - sg)))
        ahead = [_shift_up(dy, CONV_K - 1 - j) for j in range(CONV_K)]
        dx = w[0:1, :] * ahead[0]
        for j in range(1, CONV_K):
            dx = dx + w[j:j + 1, :] * ahead[j]
        dx_ref[...] = dx.astype(dx_ref.dtype)
        dw_ref[...] = jnp.concatenate([jnp.sum(x * ahead[j], axis=0, keepdims=True) for j in range(CONV_K)], axis=0)

    blk = pl.BlockSpec((t, HEAD), lambda j: (0, j))
    return pl.pallas_call(
        body, grid=(width // HEAD,),
        in_specs=[pl.BlockSpec((t, HEAD), lambda j: (0, COL_GQ + j)), pl.BlockSpec((CONV_K, HEAD), lambda j: (0, j)), blk,
                  pl.BlockSpec((t, HEAD), lambda j: (0, jnp.minimum(j, n_qk - 1))), pl.BlockSpec((t, HEAD), lambda j: (0, jnp.maximum(j - n_qk, 0)))],
        out_specs=[blk, pl.BlockSpec((CONV_K, HEAD), lambda j: (0, j))],
        out_shape=[SDS((t, width), BF16), SDS((CONV_K, width), F32)],
        name="gdn_conv_bwd", compiler_params=_params(1),
    )(proj, conv_w, y, dc_qk, dc_v)


def _l2norm(x, scale):
    return x * lax.rsqrt(jnp.sum(x * x, axis=-1, keepdims=True) + EPS) * scale


def _head(a, h):
    return a[:, h * HEAD:(h + 1) * HEAD]


def _qk_scale(h):
    return HEAD ** -0.5 if h < GDN_QK_HEADS else 1.0


def _qk_norm_fwd(c):
    t = c.shape[0]
    tt = _tile(t, (256, 128))
    width = 2 * D_MODEL

    def fn(x):
        return jnp.concatenate([_l2norm(_head(x, h), _qk_scale(h)) for h in range(2 * GDN_QK_HEADS)], axis=1)

    return _tmap(fn, (1, t // tt), [(c, *_rows(width, tt))], [((t, width), BF16, *_rows(width, tt), None)], "gdn_qk_norm")[0]


def _qk_norm_bwd(c, dq_rep, dk_rep):
    t = c.shape[0]
    tt = _tile(t, (256, 128))
    width = 2 * D_MODEL

    def fn(x, dq2, dk2):
        out = []
        for h in range(2 * GDN_QK_HEADS):
            d2, hh = (dq2, h) if h < GDN_QK_HEADS else (dk2, h - GDN_QK_HEADS)
            _, vjp = jax.vjp(lambda x: _l2norm(x, _qk_scale(h)), _head(x, h))
            out.append(vjp(_head(d2, 2 * hh).astype(F32) + _head(d2, 2 * hh + 1).astype(F32))[0])
        return jnp.concatenate(out, axis=1)

    r = _rows(width, tt)
    return _tmap(fn, (1, t // tt), [(c, *r), (dq_rep, *r), (dk_rep, *r)], [((t, width), F32, *r, None)], "gdn_qk_norm_bwd")[0]


def _gdn_gates(x, alog, dtb):
    return -jnp.exp(alog) * _softplus(x + dtb), _sigmoid(x)


def _gates_fwd(pab, alog, dtb):
    t = pab.shape[0]
    tt = _tile(t, (256, 128))

    def fn(x, alog, dtb):
        g, beta = _gdn_gates(x, alog, dtb)
        lane = lax.broadcasted_iota(jnp.int32, g.shape, 1)
        return jnp.where(lane < GDN_V_HEADS, _hdot_raw(_chunk_sum_matrix(tt, GDN_CHUNK), g), beta).T

    p = (alog, (1, HEAD), lambda j, i: (0, 0)), (dtb, (1, HEAD), lambda j, i: (0, 0))
    return _tmap(fn, (1, t // tt), [(pab, *_rows(HEAD, tt)), *p], [((HEAD, t), F32, (HEAD, tt), lambda j, i: (0, i), None)], "gdn_gates")[0]


def _gates_bwd(pab, alog, dtb, dout_t):
    t = pab.shape[0]
    tt = _tile(t, (256, 128))

    def fn(x, alog, dtb, dout_t):
        dout = dout_t.T
        lane = lax.broadcasted_iota(jnp.int32, dout.shape, 1)
        dgam = jnp.where(lane < GDN_V_HEADS, dout, 0.0)
        dbeta = jnp.where(jnp.logical_and(lane >= GDN_V_HEADS, lane < 2 * GDN_V_HEADS), dout, 0.0)
        dg = _hdot_raw(_chunk_sum_matrix(tt, GDN_CHUNK, transpose=True), dgam)
        _, vjp = jax.vjp(_gdn_gates, x, alog, dtb)
        return vjp((dg, dbeta))

    p = (alog, (1, HEAD), lambda j, i: (0, 0)), (dtb, (1, HEAD), lambda j, i: (0, 0))
    acc = ((1, HEAD), F32, (1, HEAD), lambda j, i: (0, 0), "inner")
    return _tmap(fn, (1, t // tt), [(pab, *_rows(HEAD, tt)), *p, (dout_t, (HEAD, tt), lambda j, i: (0, i))],
                 [((t, HEAD), BF16, *_rows(HEAD, tt), None), acc, acc], "gdn_gates_bwd")


def _split_bf16(x):
    hi = x.astype(BF16)
    return hi, (x - hi.astype(F32)).astype(BF16)


def _dot3(a, b):
    (ah, al), (bh, bl) = a, b
    return _bdot_raw(ah, bh, "nn") + (_bdot_raw(ah, bl, "nn") + _bdot_raw(al, bh, "nn"))


def _each(fn, *lists):
    return tuple(fn(*xs) for xs in zip(*lists))


def _unit_lower_inverses_raw(a):
    n = a[0].shape[0]
    row = lax.broadcasted_iota(jnp.int32, (n, n), 0)
    col = lax.broadcasted_iota(jnp.int32, (n, n), 1)
    eye = jnp.where(row == col, 1.0, 0.0).astype(F32)
    p = _each(lambda a: eye - a, a)
    x = _each(_split_bf16, a)
    m = 2
    while m < n:
        x = _each(_split_bf16, _each(_dot3, x, x))
        p = _each(lambda p, x: p + _bdot_raw(p, x[0], "nn"), p, x)
        m *= 2
    return p


@jax.custom_vjp
def _unit_lower_inverses(a, known):
    return _unit_lower_inverses_raw(a) if known is None else known


def _uli_fwd(a, known):
    inv = _unit_lower_inverses(a, known)
    return inv, (inv, known)


def _uli_bwd(res, ct):
    inv, known = res
    right = _each(lambda ct, inv: _bdot_raw(ct, inv, "nt"), ct, inv)
    da = _each(lambda inv, r: -_bdot_raw(inv, r, "tn"), inv, right)
    return da, (None if known is None else _each(jnp.zeros_like, known))


_unit_lower_inverses.defvjp(_uli_fwd, _uli_bwd)


def _gdn_chunks(q, k, v, beta_rows, gam_rows, s, inv_known=None):
    n = q[0].shape[0]
    heads = range(len(q))
    row = lax.broadcasted_iota(jnp.int32, (n, n), 0)
    col = lax.broadcasted_iota(jnp.int32, (n, n), 1)
    beta_cols, gam_cols = beta_rows.T, gam_rows.T
    beta = tuple(beta_cols[:, g:g + 1] for g in heads)
    gam = tuple(gam_cols[:, g:g + 1] for g in heads)
    gam_row = tuple(gam_rows[g:g + 1, :] for g in heads)
    decay = _each(lambda gam, gam_row: jnp.where(row >= col, jnp.exp(jnp.minimum(gam - gam_row, 0.0)), 0.0), gam, gam_row)
    kb = _each(lambda k, beta: k * beta, k, beta)
    a = _each(lambda kb, k, decay: jnp.where(row > col, _bdot(kb, k, "nt") * decay, 0.0), kb, k, decay)
    inv = _unit_lower_inverses(a, inv_known)
    eg = _each(jnp.exp, gam)
    u = _each(lambda inv, v, beta: _bdot(inv, v * beta, "nn"), inv, v, beta)
    w = _each(lambda inv, kb, eg: _bdot(inv, kb * eg, "nn"), inv, kb, eg)
    qk = _each(lambda q, k, decay: _bdot(q, k, "nt") * decay, q, k, decay)
    v_new = _each(lambda u, w, s: u - _bdot(w, s, "nn"), u, w, s)
    o_state = _each(lambda q, eg, s: _bdot(q * eg, s, "nn"), q, eg, s)
    o = _each(lambda o_state, qk, v_new: o_state + _bdot(qk, v_new, "nn"), o_state, qk, v_new)
    gend = _each(lambda gam: gam[n - 1:n, :], gam)
    s_new = _each(lambda s, k, gam, gend, v_new: s * jnp.exp(gend) + _bdot(k * jnp.exp(gend - gam), v_new, "tn"), s, k, gam, gend, v_new)
    return o, s_new, inv


GDN_GROUP = 16


def _gdn_specs(nc, rev):
    cc = (lambda c: nc - 1 - c) if rev else (lambda c: c)
    grp = GDN_GROUP
    q = pl.BlockSpec((GDN_CHUNK, grp // 2 * HEAD), lambda h, c: (cc(c), h))
    k = pl.BlockSpec((GDN_CHUNK, grp // 2 * HEAD), lambda h, c: (cc(c), 2 * GDN_QK_HEADS // grp + h))
    v = pl.BlockSpec((GDN_CHUNK, grp * HEAD), lambda h, c: (cc(c), 2 * GDN_QK_HEADS // grp + h))
    o = pl.BlockSpec((GDN_CHUNK, grp * HEAD), lambda h, c: (cc(c), h))
    rw = pl.BlockSpec((grp, None, 1, GDN_CHUNK), lambda h, c: (h, cc(c), 0, 0))
    st = pl.BlockSpec((grp, None, HEAD, HEAD), lambda h, c: (h, cc(c), 0, 0))
    inv = pl.BlockSpec((grp, None, GDN_CHUNK, GDN_CHUNK), lambda h, c: (h, cc(c), 0, 0))
    return q, k, v, o, rw, st, inv


def _head_lanes(g, per=1):
    return pl.ds((g // per) * HEAD, HEAD)


def _gdn_rec_fwd(qk, c, beta_row, gam_row):
    t = qk.shape[0]
    nc = t // GDN_CHUNK
    q, k, v, o, rw, st, inv = _gdn_specs(nc, False)

    def body(q_ref, k_ref, v_ref, be_ref, gr_ref, o_ref, ss_ref, inv_ref, s_ref):
        @pl.when(pl.program_id(1) == 0)
        def _():
            s_ref[...] = jnp.zeros_like(s_ref)

        heads = range(GDN_GROUP)
        s = tuple(s_ref[g] for g in heads)
        out, s_new, inv_c = _gdn_chunks(
            tuple(q_ref[:, _head_lanes(g, 2)].astype(F32) for g in heads), tuple(k_ref[:, _head_lanes(g, 2)].astype(F32) for g in heads),
            tuple(v_ref[:, _head_lanes(g)] for g in heads), be_ref[:, 0, :], gr_ref[:, 0, :], s)
        for g in heads:
            ss_ref[g] = s[g]
            o_ref[:, _head_lanes(g)] = out[g].astype(o_ref.dtype)
            inv_ref[g] = inv_c[g]
            s_ref[g] = s_new[g]

    return pl.pallas_call(
        body, grid=(GDN_V_HEADS // GDN_GROUP, nc), in_specs=[q, k, v, rw, rw], out_specs=[o, st, inv],
        out_shape=[SDS((t, 2 * D_MODEL), BF16), SDS((GDN_V_HEADS, nc, HEAD, HEAD), F32), SDS((GDN_V_HEADS, nc, GDN_CHUNK, GDN_CHUNK), F32)],
        scratch_shapes=[pltpu.VMEM((GDN_GROUP, HEAD, HEAD), F32)], name="gdn_rec", compiler_params=_params(2),
    )(qk, qk, c, beta_row, gam_row)


def _gdn_rec_bwd(qk, c, beta_row, gam_row, ss, invs, do):
    t = qk.shape[0]
    nc = t // GDN_CHUNK
    q, k, v, o, rw, st, inv = _gdn_specs(nc, True)

    def body(q_ref, k_ref, v_ref, be_ref, gr_ref, ss_ref, inv_ref, do_ref,
             dq_ref, dk_ref, dv_ref, dbe_ref, dgr_ref, ds_ref):
        @pl.when(pl.program_id(1) == 0)
        def _():
            ds_ref[...] = jnp.zeros_like(ds_ref)

        heads = range(GDN_GROUP)
        _, vjp = jax.vjp(
            _gdn_chunks,
            tuple(q_ref[:, _head_lanes(g, 2)].astype(F32) for g in heads), tuple(k_ref[:, _head_lanes(g, 2)].astype(F32) for g in heads),
            tuple(v_ref[:, _head_lanes(g)] for g in heads), be_ref[:, 0, :], gr_ref[:, 0, :],
            tuple(ss_ref[g] for g in heads), tuple(inv_ref[g] for g in heads))
        no_inv_ct = tuple(jnp.zeros((GDN_CHUNK, GDN_CHUNK), F32) for g in heads)
        dq, dk, dv, dbe, dgr, ds, _ = vjp((tuple(do_ref[:, _head_lanes(g)] for g in heads), tuple(ds_ref[g] for g in heads), no_inv_ct))
        for g in heads:
            dq_ref[:, _head_lanes(g)] = dq[g].astype(dq_ref.dtype)
            dk_ref[:, _head_lanes(g)] = dk[g].astype(dk_ref.dtype)
            dv_ref[:, _head_lanes(g)] = dv[g].astype(dv_ref.dtype)
            ds_ref[g] = ds[g]
        dbe_ref[:, 0, :] = dbe
        dgr_ref[:, 0, :] = dgr

    wide = SDS((t, 2 * D_MODEL), BF16)
    rowshape = SDS((GDN_V_HEADS, nc, 1, GDN_CHUNK), F32)
    return pl.pallas_call(
        body, grid=(GDN_V_HEADS // GDN_GROUP, nc), in_specs=[q, k, v, rw, rw, st, inv, o], out_specs=[o, o, o, rw, rw],
        out_shape=[wide, wide, wide, rowshape, rowshape],
        scratch_shapes=[pltpu.VMEM((GDN_GROUP, HEAD, HEAD), F32)], name="gdn_rec_bwd", compiler_params=_params(2),
    )(qk, qk, c, beta_row, gam_row, ss, invs, do)


def _gated_norm(o, gate, w):
    return _rms(o, w) * _silu(gate)


def _post_fwd(o, proj, col_off, w, name):
    t, width = o.shape
    tt = _tile(t, (256, 128))

    def fn(o, gate, w):
        return jnp.concatenate([_gated_norm(_head(o, h), _head(gate, h), w) for h in range(width // HEAD)], axis=1)

    return _tmap(fn, (1, t // tt),
                 [(o, *_rows(width, tt)), (proj, *_rows(width, tt, col_off * HEAD // width)), (w, (1, HEAD), lambda j, i: (0, 0))],
                 [((t, width), BF16, *_rows(width, tt), None)], name)[0]


def _post_bwd(o, proj, col_off, w, dout, name):
    t, width = o.shape
    tt = _tile(t, (256, 128))

    def fn(o, gate, w, dout):
        do, dgate, dw = [], [], jnp.zeros((1, HEAD), F32)
        for h in range(width // HEAD):
            _, vjp = jax.vjp(_gated_norm, _head(o, h).astype(F32), _head(gate, h), w)
            a, b, c = vjp(_head(dout, h).astype(F32))
            do.append(a)
            dgate.append(b)
            dw = dw + c
        return jnp.concatenate(do, axis=1), jnp.concatenate(dgate, axis=1), dw

    r = _rows(width, tt)
    return _tmap(fn, (1, t // tt),
                 [(o, *r), (proj, *_rows(width, tt, col_off * HEAD // width)), (w, (1, HEAD), lambda j, i: (0, 0)), (dout, *r)],
                 [((t, width), F32, *r, None), ((t, width), BF16, *r, None), ((1, HEAD), F32, (1, HEAD), lambda j, i: (0, 0), "inner")], name)


def _merge(gate_h, gate_g, yh, yg):
    return _sigmoid(gate_h) * yh + _sigmoid(gate_g) * yg


def _merge_fwd(proj, yh, yg):
    t = yh.shape[0]
    tt, ft = _tile(t, (256, 128)), 512
    r = _rows(ft, tt)
    return _tmap(_merge, (D_MODEL // ft, t // tt),
                 [(proj, *_rows(ft, tt, COL_GATE_H * HEAD // ft)), (proj, *_rows(ft, tt, COL_GATE_G * HEAD // ft)), (yh, *r), (yg, *r)],
                 [((t, D_MODEL), BF16, *r, None)], "merge")[0]


def _merge_bwd(proj, yh, yg, dy):
    t = yh.shape[0]
    tt, ft = _tile(t, (256, 128)), 512
    r = _rows(ft, tt)

    def fn(gate_h, gate_g, yh, yg, dy):
        _, vjp = jax.vjp(_merge, gate_h, gate_g, yh, yg)
        return vjp(dy.astype(F32))

    o = ((t, D_MODEL), BF16, *r, None)
    return _tmap(fn, (D_MODEL // ft, t // tt),
                 [(proj, *_rows(ft, tt, COL_GATE_H * HEAD // ft)), (proj, *_rows(ft, tt, COL_GATE_G * HEAD // ft)), (yh, *r), (yg, *r), (dy, *r)],
                 [o, o, o, o], "merge_bwd")


def _loss_head(h, target, g):
    t, d = h.shape
    tt = _tile(t, (256, 128))

    def fn(h, target, g):
        def f(h, g):
            err = _rms(h, g) - target
            return 0.5 * jnp.sum(jnp.mean(err * err, axis=-1))

        loss, (dh, dg) = jax.value_and_grad(f, (0, 1))(h, g)
        return dh, dg, jnp.full((1, HEAD), loss, F32)

    return _tmap(fn, (1, t // tt), [(h, *_rows(d, tt)), (target, *_rows(d, tt)), (g, (1, d), lambda j, i: (0, 0))],
                 [((t, d), F32, *_rows(d, tt), None), ((1, d), F32, (1, d), lambda j, i: (0, 0), "inner"),
                  ((1, HEAD), F32, (1, HEAD), lambda j, i: (0, 0), "inner")], "loss_head")


def _mixer_fwd(h, p, links):
    t = h.shape[0]
    nc = t // GDN_CHUNK
    u = _rms_fwd(h, p["mix_norm"], "mix_norm")
    w = {n: links.weight(n, h) for n in ("w_in_t", "w_in_b_t", "w_in_ab_t", "conv_w")}
    proj = _mm(u, w["w_in_t"], "nt", F32, "mix_in", after=links.started, b_rows=SCALAR_ROWS)
    proj_b = _mm(u, w["w_in_b_t"], "nt", F32, "mix_in_b")
    pab = _mm(u, w["w_in_ab_t"], "nt", F32, "mix_in_ab")
    qh, kh, bh = _hgrn_prep_fwd(proj, p["lbl"])
    oh, hs = _hgrn_rec_fwd(qh, kh, proj, bh)
    c, conv_y = _conv_fwd(proj, w["conv_w"])
    qk = _qk_norm_fwd(c)
    gates_t = _gates_fwd(pab, p["alog"], p["dtb"])
    gam_row = gates_t[:GDN_V_HEADS].reshape(GDN_V_HEADS, nc, 1, GDN_CHUNK)
    beta_row = gates_t[GDN_V_HEADS:2 * GDN_V_HEADS].reshape(GDN_V_HEADS, nc, 1, GDN_CHUNK)
    og, ss, invs = _gdn_rec_fwd(qk, c, beta_row, gam_row)
    ohn = _post_fwd(oh, proj, COL_HG, p["hgrn_out_norm"], "hgrn_out")
    ogn = _post_fwd(og, proj_b, COL_GZ, p["gdn_out_norm"], "gdn_out")
    w.update({n: links.weight(n, ogn) for n in ("w_branch_hgrn", "w_branch_gdn", "w_out")})
    yh = _mm(ohn, w["w_branch_hgrn"], "nn", BF16, "branch_hgrn")
    yg = _mm(ogn, w["w_branch_gdn"], "nn", BF16, "branch_gdn")
    y = _merge_fwd(proj_b, yh, yg)
    out = _mm(y, w["w_out"], "nn", F32, "mix_out", res=h)
    saved = (w, u, proj, proj_b, pab, qh, kh, bh, oh, hs, c, conv_y, qk, beta_row, gam_row, og, ss, invs, ohn, ogn, yh, yg, y)
    return out, saved


def _mixer_bwd(h, p, links, saved, dout):
    (w, u, proj, proj_b, pab, qh, kh, bh, oh, hs, c, conv_y, qk, beta_row, gam_row, og, ss, invs, ohn, ogn, yh, yg, y) = saved
    t = h.shape[0]
    grads = {}
    dw_out = _mm(y, dout, "tn", BF16, "mix_out_dw")
    dy = _mm(dout, w["w_out"], "nt", BF16, "mix_out_dx")
    dgate_h, dgate_g, dyh, dyg = _merge_bwd(proj_b, yh, yg, dy)
    dw_bh = _mm(ohn, dyh, "tn", BF16, "branch_hgrn_dw")
    dw_bg = _mm(ogn, dyg, "tn", BF16, "branch_gdn_dw")
    sent = links.send({"w_out": dw_out, "w_branch_hgrn": dw_bh, "w_branch_gdn": dw_bg})
    dohn = _mm(dyh, w["w_branch_hgrn"], "nt", BF16, "branch_hgrn_dx", after=sent)
    dogn = _mm(dyg, w["w_branch_gdn"], "nt", BF16, "branch_gdn_dx")
    doh, dhg, grads["hgrn_out_norm"] = _post_bwd(oh, proj, COL_HG, p["hgrn_out_norm"], dohn, "hgrn_out_bwd")
    dog, dgz, grads["gdn_out_norm"] = _post_bwd(og, proj_b, COL_GZ, p["gdn_out_norm"], dogn, "gdn_out_bwd")
    dqh, dkh, dhi, dbh = _hgrn_rec_bwd(qh, kh, proj, bh, hs, doh)
    dhq, dhf, grads["lbl"] = _hgrn_prep_bwd(proj, p["lbl"], dqh, dkh, dbh)
    dqv, dkv, dcv, dbeta_row, dgam_row = _gdn_rec_bwd(qk, c, beta_row, gam_row, ss, invs, dog)
    dcqk = _qk_norm_bwd(c, dqv, dkv)
    dxin, grads["conv_w"] = _conv_bwd(proj, w["conv_w"], conv_y, dcqk, dcv)
    dgates_t = jnp.concatenate([dgam_row.reshape(GDN_V_HEADS, t), dbeta_row.reshape(GDN_V_HEADS, t),
                                jnp.zeros((HEAD - 2 * GDN_V_HEADS, t), F32)], axis=0)
    dpab, grads["alog"], grads["dtb"] = _gates_bwd(pab, p["alog"], p["dtb"], dgates_t)
    front, back = [dhq, dhf, dhi, dhg, dxin], [dgz, dgate_h, dgate_g]
    dw_front = [_mm(d, u, "tn", BF16, "mix_in_dw_%d" % i) for i, d in enumerate(front)]
    dw_back = [_mm(d, u, "tn", BF16, "mix_in_b_dw_%d" % i) for i, d in enumerate(back)]
    dw_ab_t = _mm(dpab, u, "tn", BF16, "mix_in_ab_dw")
    sent = links.send({"w_in": jnp.concatenate(dw_front + [dw_ab_t[:N_SCALAR]] + dw_back, axis=0)})
    du = _mm_pieces(front, w["w_in_t"], "mix_in_dx", after=sent)
    du = _mm_pieces(back, w["w_in_b_t"], "mix_in_b_dx", res=du)
    du = _mm(dpab, w["w_in_ab_t"], "nn", F32, "mix_in_ab_dx", res=du)
    dh, grads["mix_norm"] = _rms_bwd(h, p["mix_norm"], du, dout, "mix_norm_bwd")
    return dh, grads


def _local_step(x, target, p, links):
    def ffn_weights(tag, behind):
        def get(n):
            w_in_t, w_out = links.weight(tag + "_w_in", n), links.weight(tag + "_w_out", n)
            return w_in_t, w_out, links.started if behind else None
        return get

    h1, s1 = _ffn_fwd(x, p["ffn1_norm"] + links.started[0, 0], ffn_weights("ffn1", True), "ffn1")
    h2, sm = _mixer_fwd(h1, p, links)
    h3, s2 = _ffn_fwd(h2, p["ffn2_norm"], ffn_weights("ffn2", False), "ffn2")
    dh3, dfinal, loss = _loss_head(h3, target, p["final_norm"])
    g = {"final_norm": dfinal}
    dh2, g["ffn2_norm"] = _ffn_bwd(h2, p["ffn2_norm"], s2, dh3, "ffn2", links)
    dh1, gm = _mixer_bwd(h1, p, links, sm, dh2)
    g.update(gm)
    dx, g["ffn1_norm"] = _ffn_bwd(x, p["ffn1_norm"], s1, dh1, "ffn1", links)
    return loss, dx, g


HBM_SPEC = pl.BlockSpec(memory_space=pltpu.HBM)
SEM_SPEC = pl.BlockSpec(memory_space=pltpu.SEMAPHORE)
DATAFLOW = pltpu.SideEffectType.DATAFLOW_SIDE_EFFECTING


def _position():
    x, y, c = lax.axis_index("x"), lax.axis_index("y"), lax.axis_index("c")
    return x, y, c, 4 * x + 2 * y + c


def _relations(x, y, c):
    for rel in range(1, N_DEV):
        px = 1 - x if rel & 4 else x
        py = 1 - y if rel & 2 else y
        pc = 1 - c if rel & 1 else c
        yield rel, (px, py, pc), 4 * px + 2 * py + pc


def _sem_index(item, rel):
    return item * (N_DEV - 1) + rel - 1


def _landing(a, mode):
    return lax.empty((N_DEV,) + a.shape if mode == "gather" else a.shape, a.dtype)


ALL_PEERS = tuple(range(1, N_DEV))
ONE_PER_CHIP = (1, 2, 4, 6)


def _copies_start(groups, name, rels=ALL_PEERS):
    flat = [item for grp in groups for item in grp]
    n, ng = len(flat), len(groups)
    lands = [_landing(a, mode) for a, mode in flat]

    def body(*refs):
        src_refs, land_refs, sems, token = refs[:n], refs[n:2 * n], refs[2 * n:2 * n + 2 * ng], refs[-1]
        x, y, c, me = _position()
        for rel, where, peer in _relations(x, y, c):
            if rel not in rels:
                continue
            k = 0
            for gi, grp in enumerate(groups):
                for li, (_, mode) in enumerate(grp):
                    src = src_refs[k] if mode == "gather" else src_refs[k].at[peer]
                    pltpu.make_async_remote_copy(src_ref=src, dst_ref=land_refs[k].at[me], send_sem=sems[2 * gi].at[_sem_index(li, rel)],
                                                 recv_sem=sems[2 * gi + 1].at[_sem_index(li, rel)], device_id=where, device_id_type=MESH_IDS).start()
                    k += 1
        token[...] = jnp.zeros_like(token)

    sem_shapes = [pltpu.SemaphoreType.DMA((len(grp) * (N_DEV - 1),)) for grp in groups for _ in range(2)]
    thru = [pltpu.HBM(a.shape, a.dtype) for a, _ in flat] + [pltpu.HBM(l.shape, l.dtype) for l in lands]
    outs = pl.pallas_call(
        body, name=name, out_shape=(*sem_shapes, *thru, SDS((8, HEAD), F32)),
        in_specs=[HBM_SPEC] * (2 * n), out_specs=(*[SEM_SPEC] * (2 * ng), *[HBM_SPEC] * (2 * n), pl.BlockSpec(memory_space=pltpu.VMEM)),
        input_output_aliases={i: 2 * ng + i for i in range(2 * n)}, compiler_params=pltpu.CompilerParams(has_side_effects=DATAFLOW),
    )(*[pltpu.with_memory_space_constraint(a, pltpu.HBM) for a, _ in flat], *[pltpu.with_memory_space_constraint(l, pltpu.HBM) for l in lands])
    sems, srcs, landed, token = outs[:2 * ng], outs[2 * ng:2 * ng + n], outs[2 * ng + n:2 * ng + 2 * n], outs[-1]
    result, k = [], 0
    for gi, grp in enumerate(groups):
        result.append((sems[2 * gi], sems[2 * gi + 1], srcs[k:k + len(grp)], landed[k:k + len(grp)]))
        k += len(grp)
    return result, token


def _copies_wait(started, modes, after, name, rels=ALL_PEERS):
    send_sems, recv_sems, srcs, lands = started
    n = len(srcs)

    def body(*refs):
        src_refs, land_refs, ssem, rsem, token = refs[:n], refs[n:2 * n], refs[2 * n], refs[2 * n + 1], refs[-1]
        x, y, c, _ = _position()
        for rel in rels:
            for i, mode in enumerate(modes):
                src = src_refs[i] if mode == "gather" else src_refs[i].at[0]
                cp = pltpu.make_async_remote_copy(src_ref=src, dst_ref=land_refs[i].at[0], send_sem=ssem.at[_sem_index(i, rel)],
                                                  recv_sem=rsem.at[_sem_index(i, rel)], device_id=(x, y, c), device_id_type=MESH_IDS)
                cp.wait_send()
                cp.wait_recv()
        token[...] = jnp.zeros_like(token)

    outs = pl.pallas_call(
        body, name=name, out_shape=[pltpu.HBM(a.shape, a.dtype) for a in (*srcs, *lands)] + [SDS((8, HEAD), F32)],
        in_specs=[HBM_SPEC] * (2 * n) + [SEM_SPEC, SEM_SPEC, pl.BlockSpec(memory_space=pl.ANY)],
        out_specs=[HBM_SPEC] * (2 * n) + [pl.BlockSpec(memory_space=pltpu.VMEM)],
        input_output_aliases={i: i for i in range(2 * n)}, compiler_params=pltpu.CompilerParams(has_side_effects=DATAFLOW),
    )(*srcs, *lands, send_sems, recv_sems, after)
    return outs[:n], outs[n:2 * n], outs[-1]


OTHER_CHIPS = ((1, 0), (0, 1), (1, 1))


def _pass_on_start(lands, name):
    n = len(lands)

    def body(*refs):
        land_refs, ssem, rsem, token = refs[:n], refs[n], refs[n + 1], refs[-1]
        x, y, c, _ = _position()
        for j, (fx, fy) in enumerate(OTHER_CHIPS):
            slot = 4 * (1 - x if fx else x) + 2 * (1 - y if fy else y) + c
            for i in range(n):
                pltpu.make_async_remote_copy(src_ref=land_refs[i].at[slot], dst_ref=land_refs[i].at[slot], send_sem=ssem.at[i * len(OTHER_CHIPS) + j],
                                             recv_sem=rsem.at[i * len(OTHER_CHIPS) + j], device_id=(x, y, 1 - c), device_id_type=MESH_IDS).start()
        token[...] = jnp.zeros_like(token)

    sems = pltpu.SemaphoreType.DMA((n * len(OTHER_CHIPS),))
    outs = pl.pallas_call(
        body, name=name, out_shape=(sems, sems, *[pltpu.HBM(l.shape, l.dtype) for l in lands], SDS(TOKEN, F32)),
        in_specs=[HBM_SPEC] * n, out_specs=(SEM_SPEC, SEM_SPEC, *[HBM_SPEC] * n, pl.BlockSpec(memory_space=pltpu.VMEM)),
        input_output_aliases={i: 2 + i for i in range(n)}, compiler_params=pltpu.CompilerParams(has_side_effects=DATAFLOW),
    )(*lands)
    return (outs[0], outs[1], outs[2:2 + n]), outs[-1]


def _pass_on_wait(started, after, name):
    send_sems, recv_sems, lands = started
    n = len(lands)

    def body(*refs):
        land_refs, ssem, rsem = refs[:n], refs[n], refs[n + 1]
        x, y, c, _ = _position()
        for j in range(len(OTHER_CHIPS)):
            for i in range(n):
                cp = pltpu.make_async_remote_copy(src_ref=land_refs[i].at[0], dst_ref=land_refs[i].at[0], send_sem=ssem.at[i * len(OTHER_CHIPS) + j],
                                                  recv_sem=rsem.at[i * len(OTHER_CHIPS) + j], device_id=(x, y, c), device_id_type=MESH_IDS)
                cp.wait_send()
                cp.wait_recv()

    return pl.pallas_call(
        body, name=name, out_shape=[pltpu.HBM(l.shape, l.dtype) for l in lands],
        in_specs=[HBM_SPEC] * n + [SEM_SPEC, SEM_SPEC, pl.BlockSpec(memory_space=pl.ANY)], out_specs=[HBM_SPEC] * n,
        input_output_aliases={i: i for i in range(n)}, compiler_params=pltpu.CompilerParams(has_side_effects=DATAFLOW),
    )(*lands, send_sems, recv_sems, after)


WEIGHT_GROUPS = (("ffn1_w_in", "ffn1_w_out", "gdn_conv_w"), ("w_in",), ("w_branch_hgrn", "w_branch_gdn", "w_out", "ffn2_w_in", "ffn2_w_out"))
GROUP_RELS = (ONE_PER_CHIP, ONE_PER_CHIP, ALL_PEERS)


class _Links:
    def __init__(self, shards, me):
        self.me = me
        self.shards = shards
        self.weights = {}
        self.sends = []
        self.gathers = {}
        self.started = None
        self._start_gather(0, None)

    def _start_gather(self, gi, zeros):
        if gi < len(WEIGHT_GROUPS):
            items = [(self.shards[n] if zeros is None else self.shards[n] + zeros[0, 0].astype(self.shards[n].dtype), "gather")
                     for n in WEIGHT_GROUPS[gi]]
            started, self.started = _copies_start([items], "gather_start_%d" % gi, GROUP_RELS[gi])
            self.gathers[gi] = started[0]

    def weight(self, name, after):
        if name not in self.weights:
            source = {"w_in_t": "w_in", "w_in_b_t": "w_in", "w_in_ab_t": "w_in", "conv_w": "gdn_conv_w"}.get(name, name)
            gi = [i for i, grp in enumerate(WEIGHT_GROUPS) if source in grp][0]
            assert gi in self.gathers, "weight groups are asked for in order"
            srcs, lands, zero = _copies_wait(self.gathers[gi], ["gather"] * len(WEIGHT_GROUPS[gi]), after, "gather_wait_%d" % gi, GROUP_RELS[gi])
            if GROUP_RELS[gi] == ONE_PER_CHIP:
                passing, zero = _pass_on_start(lands, "gather_pass_%d" % gi)
                self._start_gather(gi + 1, zero)
                lands = _pass_on_wait(passing, self.started, "gather_passed_%d" % gi)
            else:
                self._start_gather(gi + 1, zero)
            for n, src, land in zip(WEIGHT_GROUPS[gi], srcs, lands):
                full = lax.dynamic_update_index_in_dim(land, src, self.me, 0)
                if n == "gdn_conv_w":
                    self.weights["conv_w"] = full.reshape(N_DEV, CONV_K, 4 * D_MODEL // N_DEV).transpose(1, 0, 2).reshape(CONV_K, 4 * D_MODEL)
                elif n == "w_in":
                    self.weights.update(_w_in_pieces(full.reshape(-1, D_MODEL)))
                else:
                    self.weights[n] = full.reshape(-1, D_MODEL)
        return self.weights[name]

    def send(self, grads):
        names = list(grads)
        blocks = [grads[n].reshape(N_DEV, -1, D_MODEL) for n in names]
        started, token = _copies_start([[(b, "scatter") for b in blocks]], "send_" + names[0])
        self.sends.append((names, started[0]))
        return token

    def landed(self, after):
        out = {}
        for names, started in self.sends:
            srcs, lands, _ = _copies_wait(started, ["scatter"] * len(names), after, "landed_" + names[0])
            for n, src, land in zip(names, srcs, lands):
                out[n] = lax.dynamic_update_index_in_dim(land, lax.dynamic_index_in_dim(src, self.me, 0, keepdims=False), self.me, 0)
        return out


def _adam(parts, w, m, v, name):
    n_parts, r, c = parts.shape
    tc = c if c <= 512 else (256 if r > 1024 else 512)

    def body(p_ref, w_ref, m_ref, v_ref, g_ref, d_ref, mo_ref, vo_ref):
        g = p_ref[0].astype(F32)
        for i in range(1, n_parts):
            g = g + p_ref[i].astype(F32)
        m_new = ADAM_B1 * m_ref[...] + (1.0 - ADAM_B1) * g
        v_new = ADAM_B2 * v_ref[...] + (1.0 - ADAM_B2) * (g * g)
        m_hat = m_new / (1.0 - ADAM_B1 ** ADAM_STEP)
        v_hat = v_new / (1.0 - ADAM_B2 ** ADAM_STEP)
        g_ref[...] = g
        d_ref[...] = -ADAM_LR * (m_hat / (jnp.sqrt(v_hat) + ADAM_EPS) + ADAM_WD * w_ref[...])
        mo_ref[...] = m_new
        vo_ref[...] = v_new

    spec = pl.BlockSpec((r, tc), lambda j: (0, j))
    return pl.pallas_call(
        body, grid=(c // tc,), in_specs=[pl.BlockSpec((n_parts, r, tc), lambda j: (0, 0, j)), spec, spec, spec],
        out_specs=[spec] * 4, out_shape=[SDS((r, c), F32)] * 4, name=name, compiler_params=_params(1),
    )(parts, w, m, v)


BIG = ("ffn1_w_in", "ffn1_w_out", "w_in", "w_branch_hgrn", "w_branch_gdn", "w_out", "ffn2_w_in", "ffn2_w_out")


TRANSPOSED = ("ffn1_w_in", "w_in", "ffn2_w_in")


def _shard_rows(name, shard):
    return shard.T if name in TRANSPOSED else shard


SCALAR_ROWS = 8192
N_SCALAR = 2 * GDN_V_HEADS


def _w_in_pieces(w_in_t):
    return {"w_in_t": w_in_t, "w_in_b_t": w_in_t[SCALAR_ROWS + N_SCALAR:],
            "w_in_ab_t": jnp.pad(w_in_t[SCALAR_ROWS:SCALAR_ROWS + N_SCALAR], ((0, HEAD - N_SCALAR), (0, 0)))}


def _pad_lanes(a, width=HEAD):
    return jnp.pad(a, ((0, 0), (0, width - a.shape[1])))


SMALL_ROWS = 24


def _pack_small(g, loss):
    row6 = jnp.concatenate([g["hgrn_out_norm"], g["gdn_out_norm"], g["alog"], g["dtb"], loss,
                            jnp.zeros((1, D_MODEL - 5 * HEAD), F32)], axis=1)
    return jnp.concatenate([g["ffn1_norm"], g["mix_norm"], g["lbl"], g["ffn2_norm"], g["final_norm"], row6,
                            jnp.zeros((1, D_MODEL), F32), g["conv_w"].reshape(4 * CONV_K, D_MODEL)], axis=0)


def _pack_small_state(a):
    row6 = jnp.concatenate([a["hgrn_out_norm"], a["gdn_out_norm"], _pad_lanes(a["gdn_a_log"]), _pad_lanes(a["gdn_dt_bias"]),
                            jnp.zeros((1, D_MODEL - 4 * HEAD), F32)], axis=1)
    return jnp.concatenate([a["ffn1_norm"], a["mix_norm"], a["hgrn_lb_logits"], a["ffn2_norm"], a["final_norm"].reshape(1, D_MODEL),
                            row6, jnp.zeros((1, D_MODEL), F32)], axis=0)


def _unpack_small(a):
    return {"ffn1_norm": a[0:1], "mix_norm": a[1:2], "hgrn_lb_logits": a[2:4], "ffn2_norm": a[4:5], "final_norm": a[5],
            "hgrn_out_norm": a[6:7, :HEAD], "gdn_out_norm": a[6:7, HEAD:2 * HEAD],
            "gdn_a_log": a[6:7, 2 * HEAD:2 * HEAD + GDN_V_HEADS], "gdn_dt_bias": a[6:7, 3 * HEAD:3 * HEAD + GDN_V_HEADS]}


NAMES = ("ffn1_norm", "ffn1_w_in", "ffn1_w_out", "mix_norm", "w_in", "hgrn_lb_logits", "hgrn_out_norm", "gdn_conv_w", "gdn_a_log",
         "gdn_dt_bias", "gdn_out_norm", "w_branch_hgrn", "w_branch_gdn", "w_out", "ffn2_norm", "ffn2_w_in", "ffn2_w_out", "final_norm")


def kernel(x, ffn1_norm, ffn1_w_in, ffn1_w_out, mix_norm, w_in, hgrn_lb_logits, hgrn_out_norm, gdn_conv_w, gdn_a_log, gdn_dt_bias, gdn_out_norm, w_branch_hgrn, w_branch_gdn, w_out, ffn2_norm, ffn2_w_in, ffn2_w_out, final_norm, loss_target, m_ffn1_norm, m_ffn1_w_in, m_ffn1_w_out, m_mix_norm, m_w_in, m_hgrn_lb_logits, m_hgrn_out_norm, m_gdn_conv_w, m_gdn_a_log, m_gdn_dt_bias, m_gdn_out_norm, m_w_branch_hgrn, m_w_branch_gdn, m_w_out, m_ffn2_norm, m_ffn2_w_in, m_ffn2_w_out, m_final_norm, v_ffn1_norm, v_ffn1_w_in, v_ffn1_w_out, v_mix_norm, v_w_in, v_hgrn_lb_logits, v_hgrn_out_norm, v_gdn_conv_w, v_gdn_a_log, v_gdn_dt_bias, v_gdn_out_norm, v_w_branch_hgrn, v_w_branch_gdn, v_w_out, v_ffn2_norm, v_ffn2_w_in, v_ffn2_w_out, v_final_norm):
    wts = dict(zip(NAMES, (ffn1_norm, ffn1_w_in, ffn1_w_out, mix_norm, w_in, hgrn_lb_logits, hgrn_out_norm, gdn_conv_w, gdn_a_log,
                           gdn_dt_bias, gdn_out_norm, w_branch_hgrn, w_branch_gdn, w_out, ffn2_norm, ffn2_w_in, ffn2_w_out, final_norm)))
    mom = dict(zip(NAMES, (m_ffn1_norm, m_ffn1_w_in, m_ffn1_w_out, m_mix_norm, m_w_in, m_hgrn_lb_logits, m_hgrn_out_norm, m_gdn_conv_w,
                           m_gdn_a_log, m_gdn_dt_bias, m_gdn_out_norm, m_w_branch_hgrn, m_w_branch_gdn, m_w_out, m_ffn2_norm, m_ffn2_w_in,
                           m_ffn2_w_out, m_final_norm)))
    var = dict(zip(NAMES, (v_ffn1_norm, v_ffn1_w_in, v_ffn1_w_out, v_mix_norm, v_w_in, v_hgrn_lb_logits, v_hgrn_out_norm, v_gdn_conv_w,
                           v_gdn_a_log, v_gdn_dt_bias, v_gdn_out_norm, v_w_branch_hgrn, v_w_branch_gdn, v_w_out, v_ffn2_norm, v_ffn2_w_in,
                           v_ffn2_w_out, v_final_norm)))
    me = 4 * lax.axis_index("x") + 2 * lax.axis_index("y") + lax.axis_index("c")

    conv_shard = wts["gdn_conv_w"][0]
    shards = {n: _shard_rows(n, wts[n][0]).astype(BF16) for n in BIG}
    shards["gdn_conv_w"] = conv_shard.reshape(2, D_MODEL)
    links = _Links(shards, me)
    p = {"ffn1_norm": wts["ffn1_norm"], "mix_norm": wts["mix_norm"], "ffn2_norm": wts["ffn2_norm"], "final_norm": wts["final_norm"].reshape(1, D_MODEL),
         "lbl": wts["hgrn_lb_logits"], "hgrn_out_norm": wts["hgrn_out_norm"], "gdn_out_norm": wts["gdn_out_norm"],
         "alog": _pad_lanes(wts["gdn_a_log"]), "dtb": _pad_lanes(wts["gdn_dt_bias"])}

    loss, dx, g = _local_step(x[0], loss_target[0], p, links)

    small_started, small_token = _copies_start([[(_pack_small(g, loss), "gather")]], "small_start")
    landed = links.landed(small_token)

    big = [{} for _ in range(4)]
    for n in BIG:
        res = _adam(landed[n], _shard_rows(n, wts[n][0]), _shard_rows(n, mom[n][0]), _shard_rows(n, var[n][0]), "adam_" + n)
        for kind in range(4):
            big[kind][n] = _shard_rows(n, res[kind])
    small_srcs, small_lands, _ = _copies_wait(small_started[0], ["gather"], res[0], "small_wait")
    small_parts = lax.dynamic_update_index_in_dim(small_lands[0], small_srcs[0], me, 0)
    n_vec = SMALL_ROWS - 4 * CONV_K
    small_raw = _adam(small_parts[:, :n_vec], _pack_small_state(wts), _pack_small_state(mom), _pack_small_state(var), "adam_small")
    small = [_unpack_small(o) for o in small_raw]
    loss_total = small_raw[0][6, 4 * HEAD]
    conv_parts = small_parts[:, n_vec:].reshape(N_DEV, CONV_K, 4 * D_MODEL)
    width = 4 * D_MODEL // N_DEV
    conv_mine = lax.dynamic_slice_in_dim(conv_parts, me * width, width, axis=2)
    conv = _adam(conv_mine, conv_shard, mom["gdn_conv_w"][0], var["gdn_conv_w"][0], "adam_conv")

    outs = []
    for kind in range(4):
        for n in NAMES:
            if n in BIG:
                outs.append(big[kind][n][None])
            elif n == "gdn_conv_w":
                outs.append(conv[kind][None])
            else:
                outs.append(small[kind][n])
    return (loss_total, dx[None], *outs)
```
